```python
import numpy as np
import jax, jax.numpy as jnp
from jax import lax


D_MODEL = 1024
BATCH = 16
SEQ = 2048
DEPTH = 2

GRID_W = 64
CTX_LEN = 256
HEAD_DIM = 64
RET_HEADS = D_MODEL // (2 * HEAD_DIM)
NA_HEADS = D_MODEL // (2 * HEAD_DIM)
RET_W = RET_HEADS * HEAD_DIM
NA_W = NA_HEADS * HEAD_DIM
MIX_W = RET_W + NA_W
AB_IN = 4 * RET_W + 3 * NA_W
RET_CHUNK = 128
RET_DECAY_BASE = 5.0
GN_EPS = 1e-5
NA_KH = 8
NA_KW = 16
SWA_HEADS = D_MODEL // HEAD_DIM
SWA_KV_HEADS = SWA_HEADS // 4
SWA_WINDOW = 128
SWA_BLOCK = 128
SWA_IN = (SWA_HEADS + 2 * SWA_KV_HEADS) * HEAD_DIM
ROPE_BASE = 10000.0
MOE_GROUPS = 4
MOE_EPG = 8
MOE_EXPERTS = MOE_GROUPS * MOE_EPG
MOE_TOPK = 2
MOE_FF = D_MODEL // 2
MOE_BLOCK = 128
NORM_EPS = 1e-6
NEG_INF = -1e30

kernel_name = 'hybrid_retention_natten_swa_hmoe_dit'


def rms_norm(x, g):
    xf = x.astype(jnp.float32)
    y = xf * lax.rsqrt(jnp.mean(xf * xf, axis=-1, keepdims=True) + NORM_EPS)
    return (y * g.astype(jnp.float32)).astype(x.dtype)


def modulate(h, shift, scale):
    return h * (1 + scale) + shift


def split_heads(t, n):
    return t.reshape(*t.shape[:-1], n, HEAD_DIM)


def axial_rope_tables(L):
    nf = HEAD_DIM // 4
    inv = ROPE_BASE ** (-jnp.arange(nf, dtype=jnp.float32) / nf)
    t = jnp.arange(L)
    row = (t // GRID_W).astype(jnp.float32)
    col = (t % GRID_W).astype(jnp.float32)
    ang = jnp.concatenate([row[:, None] * inv, col[:, None] * inv], axis=-1)
    return jnp.cos(ang), jnp.sin(ang)


def apply_rope(x, cos, sin):
    nf = x.shape[-1] // 4
    xr = x.reshape(*x.shape[:-1], 2, 2, nf)
    x1, x2 = xr[..., 0, :], xr[..., 1, :]
    c = cos.reshape(cos.shape[0], 1, 2, nf).astype(x.dtype)
    s = sin.reshape(sin.shape[0], 1, 2, nf).astype(x.dtype)
    out = jnp.stack([x1 * c - x2 * s, x2 * c + x1 * s], axis=-2)
    return out.reshape(x.shape)


def retention_scan(q, k, v, log_gamma, state0):
    b, h, L, dk = q.shape
    dv = v.shape[-1]
    n = L // RET_CHUNK
    lg = log_gamma.astype(jnp.float32)[:, None]
    pos = jnp.arange(RET_CHUNK, dtype=jnp.float32)
    rel = pos[:, None] - pos[None, :]
    intra = jnp.where(rel >= 0, jnp.exp(lg[:, :, None] * jnp.maximum(rel, 0.0)), 0.0)
    q_decay = jnp.exp(lg * (pos + 1.0))[..., None]
    k_decay = jnp.exp(lg * (RET_CHUNK - 1.0 - pos))[..., None]
    chunk_decay = jnp.exp(lg[:, 0] * RET_CHUNK)[:, None, None]

    def to_chunks(t):
        return t.reshape(b, h, n, RET_CHUNK, t.shape[-1]).transpose(2, 0, 1, 3, 4)

    def step(state, qkv):
        qc, kc, vc = qkv
        s = jnp.einsum('bhnd,bhmd->bhnm', qc, kc) * intra
        out = jnp.einsum('bhnm,bhme->bhne', s, vc) + jnp.einsum('bhnd,bhde->bhne', qc, state) * q_decay
        state = state * chunk_decay + jnp.einsum('bhmd,bhme->bhde', kc * k_decay, vc)
        return state, out

    state, out = lax.scan(step, state0, (to_chunks(q), to_chunks(k), to_chunks(v)))
    return out.transpose(1, 2, 0, 3, 4).reshape(b, h, L, dv), state


def retention_readout(o, gate, gain):
    o = o.transpose(0, 2, 1, 3)
    mu = jnp.mean(o, axis=-1, keepdims=True)
    var = jnp.mean(jnp.square(o - mu), axis=-1, keepdims=True)
    o = ((o - mu) * lax.rsqrt(var + GN_EPS)).reshape(gate.shape) * gain.astype(jnp.float32)
    return (jax.nn.silu(gate.astype(jnp.float32)) * o).astype(gate.dtype)


def context_attention(q, k, v, sink=None):
    b, lc, hq, d = q.shape
    hk = k.shape[2]
    g = hq // hk
    qg = q.reshape(b, lc, hk, g, d)
    s = jnp.einsum('bqkgd,bmkd->bkgqm', qg, k).astype(jnp.float32)
    if sink is not None:
        sink_col = jnp.broadcast_to(sink.astype(jnp.float32).reshape(hk, g, 1, 1), (b, hk, g, lc, 1))
        s = jnp.concatenate([s, sink_col], axis=-1)
    p = jax.nn.softmax(s, axis=-1)[..., :lc].astype(v.dtype)
    o = jnp.einsum('bkgqm,bmkd->bqkgd', p, v)
    return o.reshape(b, lc, hq * d)


def na_column_layout():
    nj = GRID_W // NA_KW
    kb = 2 * NA_KW
    starts = np.clip(np.arange(nj) * NA_KW - NA_KW // 2, 0, GRID_W - kb)
    key_col = starts[:, None] + np.arange(kb)[None, :]
    q_col = np.arange(nj)[:, None] * NA_KW + np.arange(NA_KW)[None, :]
    win = np.clip(q_col - NA_KW // 2, 0, GRID_W - NA_KW)
    kc3 = key_col[:, None, :]
    valid = (kc3 >= win[..., None]) & (kc3 < win[..., None] + NA_KW)
    col_off = np.clip(kc3 - q_col[..., None] + NA_KW - 1, 0, 2 * NA_KW - 2)
    return key_col, valid, col_off


def neighbourhood_attention(q, k, v, kc, vc, rpb):
    b, L, h, d = q.shape
    rows = L // GRID_W
    kh = min(NA_KH, rows)
    key_col, col_valid, col_off = na_column_layout()
    nj, kb = key_col.shape
    grid = lambda t: t.reshape(b, rows, GRID_W, h, d).transpose(0, 3, 1, 2, 4)
    qg, kg, vg = grid(q), grid(k), grid(v)
    kct, vct = kc.transpose(0, 2, 1, 3), vc.transpose(0, 2, 1, 3)
    valid = jnp.asarray(np.broadcast_to(col_valid[:, :, None, :], (nj, NA_KW, kh, kb)).reshape(nj, NA_KW, kh * kb))
    rpb_cols = rpb[:, :, col_off]
    n_loc = kh * kb

    def row_block(r):
        rs = jnp.clip(r - kh // 2, 0, rows - kh)
        q_r = lax.dynamic_index_in_dim(qg, r, axis=2, keepdims=False).reshape(b, h, nj, NA_KW, d)

        def gather(t):
            t = lax.dynamic_slice_in_dim(t, rs, kh, axis=2)[:, :, :, key_col]
            return t.transpose(0, 1, 3, 2, 4, 5).reshape(b, h, nj, n_loc, d)

        k_r, v_r = gather(kg), gather(vg)
        row_off = rs - r + (NA_KH - 1) + jnp.arange(kh)
        bias = rpb_cols[:, row_off].transpose(0, 2, 3, 1, 4).reshape(h, nj, NA_KW, n_loc)
        s_loc = jnp.einsum('bhjqd,bhjkd->bhjqk', q_r, k_r).astype(jnp.float32) + bias.astype(jnp.float32)
        s_loc = jnp.where(valid, s_loc, NEG_INF)
        s_ctx = jnp.einsum('bhjqd,bhkd->bhjqk', q_r, kct).astype(jnp.float32)
        p = jax.nn.softmax(jnp.concatenate([s_loc, s_ctx], axis=-1), axis=-1).astype(v.dtype)
        o = (jnp.einsum('bhjqk,bhjkd->bhjqd', p[..., :n_loc], v_r)
             + jnp.einsum('bhjqk,bhkd->bhjqd', p[..., n_loc:], vct))
        return o.reshape(b, h, GRID_W, d)

    o = lax.map(row_block, jnp.arange(rows))
    return o.transpose(1, 0, 3, 2, 4).reshape(b, L, h * d)


def windowed_sink_attention(q, k, v, kc, vc, sink):
    b, L, hq, d = q.shape
    hk = k.shape[2]
    g = hq // hk
    lc = kc.shape[1]
    nb = L // SWA_BLOCK
    band = SWA_BLOCK + 2 * SWA_WINDOW
    qg = q.reshape(b, L, hk, g, d).transpose(0, 2, 3, 1, 4)
    pad = lambda t: jnp.pad(t.transpose(0, 2, 1, 3), ((0, 0), (0, 0), (SWA_WINDOW, SWA_WINDOW), (0, 0)))
    kp, vp = pad(k), pad(v)
    kct, vct = kc.transpose(0, 2, 1, 3), vc.transpose(0, 2, 1, 3)
    sink_col = jnp.broadcast_to(sink.astype(jnp.float32).reshape(hk, g, 1, 1), (b, hk, g, SWA_BLOCK, 1))
    qq = np.arange(SWA_BLOCK)[:, None]
    kk = np.arange(band)[None, :]
    in_window = np.abs(qq - (kk - SWA_WINDOW)) <= SWA_WINDOW

    def block(i):
        q_b = lax.dynamic_slice_in_dim(qg, i * SWA_BLOCK, SWA_BLOCK, axis=3)
        k_b = lax.dynamic_slice_in_dim(kp, i * SWA_BLOCK, band, axis=2)
        v_b = lax.dynamic_slice_in_dim(vp, i * SWA_BLOCK, band, axis=2)
        key_pos = i * SWA_BLOCK - SWA_WINDOW + jnp.arange(band)
        valid = in_window & (key_pos >= 0) & (key_pos < L)
        s = jnp.einsum('bkgqd,bkmd->bkgqm', q_b, k_b).astype(jnp.float32)
        s = jnp.where(valid, s, NEG_INF)
        s_ctx = jnp.einsum('bkgqd,bkmd->bkgqm', q_b, kct).astype(jnp.float32)
        p = jax.nn.softmax(jnp.concatenate([s, s_ctx, sink_col], axis=-1), axis=-1).astype(v.dtype)
        return (jnp.einsum('bkgqm,bkmd->bkgqd', p[..., :band], v_b)
                + jnp.einsum('bkgqm,bkmd->bkgqd', p[..., band:band + lc], vct))

    o = lax.map(block, jnp.arange(nb))
    return o.transpose(1, 0, 4, 2, 3, 5).reshape(b, L, hq * d)


def retention_na_mixer(hx, hc, w_in, w_out, ret_decay, ret_gn, na_rpb, cos, sin, need_ctx):
    cuts = [RET_W, 2 * RET_W, 3 * RET_W, 4 * RET_W, 4 * RET_W + NA_W, 4 * RET_W + 2 * NA_W]
    rq, rk, rv, rg, nq, nk, nv = jnp.split(hx @ w_in, cuts, axis=-1)
    crq, crk, crv, crg, cnq, cnk, cnv = jnp.split(hc @ w_in, cuts, axis=-1)
    scale = HEAD_DIM ** -0.5
    f32 = lambda t: t.transpose(0, 2, 1, 3).astype(jnp.float32)
    flip = lambda t: t[:, :, ::-1]
    log_gamma = jnp.log1p(-jnp.exp2(-ret_decay.astype(jnp.float32)))
    q = f32(apply_rope(split_heads(rq, RET_HEADS), cos, sin))
    k = f32(apply_rope(split_heads(rk, RET_HEADS), cos, sin)) * scale
    v = f32(split_heads(rv, RET_HEADS))
    cq = f32(split_heads(crq, RET_HEADS))
    ck = f32(split_heads(crk, RET_HEADS)) * scale
    cv = f32(split_heads(crv, RET_HEADS))
    zero = jnp.zeros((hx.shape[0], RET_HEADS, HEAD_DIM, HEAD_DIM), jnp.float32)
    c_f, st_f = retention_scan(cq, ck, cv, log_gamma[0], zero)
    c_b, st_b = retention_scan(flip(cq), flip(ck), flip(cv), log_gamma[1], zero)
    l_f, _ = retention_scan(q, k, v, log_gamma[0], st_f)
    l_b, _ = retention_scan(flip(q), flip(k), flip(v), log_gamma[1], st_b)
    ret_x = retention_readout(l_f + flip(l_b), rg, ret_gn)
    cnk_h, cnv_h = split_heads(cnk, NA_HEADS), split_heads(cnv, NA_HEADS)
    na_x = neighbourhood_attention(split_heads(nq, NA_HEADS) * scale, split_heads(nk, NA_HEADS),
                                   split_heads(nv, NA_HEADS), cnk_h, cnv_h, na_rpb)
    out_x = jnp.concatenate([ret_x, na_x], axis=-1) @ w_out
    out_c = None
    if need_ctx:
        ret_c = retention_readout(c_f + flip(c_b), crg, ret_gn)
        na_c = context_attention(split_heads(cnq, NA_HEADS) * scale, cnk_h, cnv_h)
        out_c = jnp.concatenate([ret_c, na_c], axis=-1) @ w_out
    return out_x, out_c


def swa_mixer(hx, hc, w_in, w_out, sink, cos, sin, need_ctx):
    cuts = [SWA_HEADS * HEAD_DIM, (SWA_HEADS + SWA_KV_HEADS) * HEAD_DIM]
    q, k, v = jnp.split(hx @ w_in, cuts, axis=-1)
    cq, ck, cv = jnp.split(hc @ w_in, cuts, axis=-1)
    scale = HEAD_DIM ** -0.5
    q = apply_rope(split_heads(q, SWA_HEADS), cos, sin) * scale
    k = apply_rope(split_heads(k, SWA_KV_HEADS), cos, sin)
    v = split_heads(v, SWA_KV_HEADS)
    ck, cv = split_heads(ck, SWA_KV_HEADS), split_heads(cv, SWA_KV_HEADS)
    out_x = windowed_sink_attention(q, k, v, ck, cv, sink) @ w_out
    out_c = None
    if need_ctx:
        out_c = context_attention(split_heads(cq, SWA_HEADS) * scale, ck, cv, sink) @ w_out
    return out_x, out_c


def grouped_expert_mlp(h, expert, weight, w_gu, w_down):
    n, d = h.shape
    a = expert.size
    e_flat = expert.reshape(-1)
    tok = jnp.arange(a, dtype=jnp.int32) // MOE_TOPK
    order = jnp.argsort(e_flat)
    e_sorted = e_flat[order]
    counts = jnp.bincount(e_flat, length=MOE_EXPERTS)
    padded = (counts + MOE_BLOCK - 1) // MOE_BLOCK * MOE_BLOCK
    ends = jnp.cumsum(padded)
    dest = (ends - padded)[e_sorted] + jnp.arange(a) - (jnp.cumsum(counts) - counts)[e_sorted]
    n_blocks = (a + MOE_EXPERTS * (MOE_BLOCK - 1) + MOE_BLOCK - 1) // MOE_BLOCK
    slot_tok = jnp.full((n_blocks * MOE_BLOCK,), n, jnp.int32).at[dest].set(tok[order])
    slot_w = jnp.zeros((n_blocks * MOE_BLOCK,), h.dtype).at[dest].set(weight.reshape(-1)[order])
    block_e = jnp.minimum(jnp.searchsorted(ends, jnp.arange(n_blocks) * MOE_BLOCK, side='right'), MOE_EXPERTS - 1)
    h_pad = jnp.concatenate([h, jnp.zeros((1, d), h.dtype)], axis=0)

    def expert_block(args):
        idx, e = args
        gate, up = jnp.split(h_pad[idx] @ w_gu[e], 2, axis=-1)
        return (jax.nn.silu(gate) * up) @ w_down[e]

    y = lax.map(expert_block, (slot_tok.reshape(n_blocks, MOE_BLOCK), block_e))
    y = y.reshape(-1, d) * slot_w[:, None]
    return jnp.zeros((n + 1, d), y.dtype).at[slot_tok].add(y)[:n]


def hierarchical_moe(h, wg, bg, we, be, w_gu, w_down):
    hf = h.astype(jnp.float32)
    pg = jax.nn.softmax(hf @ wg.astype(jnp.float32) + bg.astype(jnp.float32), axis=-1)
    gi = jnp.argmax(pg, axis=-1)
    gw = jnp.max(pg, axis=-1, keepdims=True)
    le = (hf @ we.astype(jnp.float32) + be.astype(jnp.float32)).reshape(-1, MOE_GROUPS, MOE_EPG)
    le = jnp.einsum('ng,nge->ne', jax.nn.one_hot(gi, MOE_GROUPS, dtype=jnp.float32), le)
    tw, ti = lax.top_k(jax.nn.softmax(le, axis=-1), MOE_TOPK)
    tw = tw / jnp.sum(tw, axis=-1, keepdims=True)
    weight = (gw * tw).astype(h.dtype)
    expert = gi[:, None].astype(jnp.int32) * MOE_EPG + ti
    return grouped_expert_mlp(h, expert, weight, w_gu, w_down)


def setup_inputs(seed: int = 0) -> dict:
    key = jax.random.key(seed)
    ks = jax.random.split(key, 22)
    d = D_MODEL
    n_even, n_odd = (DEPTH + 1) // 2, DEPTH // 2
    nrm = lambda k, shape, s: jax.random.normal(k, shape, jnp.float32) * s
    decay = RET_DECAY_BASE + jnp.arange(RET_HEADS, dtype=jnp.float32)
    return {
        'x': nrm(ks[0], (BATCH, SEQ, d), 1.0),
        'c': nrm(ks[1], (BATCH, d), 1.0),
        'ctx': nrm(ks[2], (BATCH, CTX_LEN, d), 1.0),
        'c_ctx': nrm(ks[3], (d,), 1.0),
        'ada_w': nrm(ks[4], (DEPTH, d, 6 * d), 0.5 * d ** -0.5),
        'ada_b': nrm(ks[5], (DEPTH, 6 * d), 0.02),
        'norm_g': 1.0 + nrm(ks[6], (DEPTH, 2, d), 0.02),
        'final_g': 1.0 + nrm(ks[7], (d,), 0.02),
        'ab_w_in': nrm(ks[8], (n_even, d, AB_IN), d ** -0.5),
        'ab_w_out': nrm(ks[9], (n_even, MIX_W, d), MIX_W ** -0.5),
        'ret_decay': decay + nrm(ks[10], (n_even, 2, RET_HEADS), 0.1),
        'ret_gn': 1.0 + nrm(ks[11], (n_even, RET_W), 0.02),
        'na_rpb': nrm(ks[12], (n_even, NA_HEADS, 2 * NA_KH - 1, 2 * NA_KW - 1), 0.1),
        'swa_w_in': nrm(ks[13], (n_odd, d, SWA_IN), d ** -0.5),
        'swa_w_out': nrm(ks[14], (n_odd, SWA_HEADS * HEAD_DIM, d), (SWA_HEADS * HEAD_DIM) ** -0.5),
        'swa_sink': nrm(ks[15], (n_odd, SWA_HEADS), 0.5),
        'router_g_w': nrm(ks[16], (DEPTH, d, MOE_GROUPS), d ** -0.5),
        'router_g_b': nrm(ks[17], (DEPTH, MOE_GROUPS), 0.01),
        'router_e_w': nrm(ks[18], (DEPTH, d, MOE_EXPERTS), d ** -0.5),
        'router_e_b': nrm(ks[19], (DEPTH, MOE_EXPERTS), 0.01),
        'expert_w_gu': nrm(ks[20], (DEPTH, MOE_EXPERTS, d, 2 * MOE_FF), d ** -0.5),
        'expert_w_down': nrm(ks[21], (DEPTH, MOE_EXPERTS, MOE_FF, d), MOE_FF ** -0.5),
    }


def reference(x, c, ctx, c_ctx, ada_w, ada_b, norm_g, final_g, ab_w_in, ab_w_out, ret_decay, ret_gn,
              na_rpb, swa_w_in, swa_w_out, swa_sink, router_g_w, router_g_b, router_e_w, router_e_b,
              expert_w_gu, expert_w_down):
    L = x.shape[1]
    cos, sin = axial_rope_tables(L)
    s_c = jax.nn.silu(c)
    s_cc = jax.nn.silu(c_ctx)
    xc = ctx
    for layer in range(DEPTH):
        last = layer == DEPTH - 1
        sh1, sc1, g1, sh2, sc2, g2 = jnp.split((s_c @ ada_w[layer] + ada_b[layer])[:, None, :], 6, axis=-1)
        csh1, csc1, cg1, csh2, csc2, cg2 = jnp.split((s_cc @ ada_w[layer] + ada_b[layer])[None, None, :], 6, axis=-1)
        hx = modulate(rms_norm(x, norm_g[layer, 0]), sh1, sc1)
        hc = modulate(rms_norm(xc, norm_g[layer, 0]), csh1, csc1)
        if layer % 2 == 0:
            j = layer // 2
            ox, oc = retention_na_mixer(hx, hc, ab_w_in[j], ab_w_out[j], ret_decay[j], ret_gn[j], na_rpb[j],
                                        cos, sin, not last)
        else:
            j = layer // 2
            ox, oc = swa_mixer(hx, hc, swa_w_in[j], swa_w_out[j], swa_sink[j], cos, sin, not last)
        x = x + g1 * ox
        hx = modulate(rms_norm(x, norm_g[layer, 1]), sh2, sc2).reshape(-1, D_MODEL)
        moe_w = (router_g_w[layer], router_g_b[layer], router_e_w[layer], router_e_b[layer],
                 expert_w_gu[layer], expert_w_down[layer])
        if last:
            x = x + g2 * hierarchical_moe(hx, *moe_w).reshape(x.shape)
        else:
            xc = xc + cg1 * oc
            hc = modulate(rms_norm(xc, norm_g[layer, 1]), csh2, csc2).reshape(-1, D_MODEL)
            y = hierarchical_moe(jnp.concatenate([hx, hc], axis=0), *moe_w)
            n_lat = hx.shape[0]
            x = x + g2 * y[:n_lat].reshape(x.shape)
            xc = xc + cg2 * y[n_lat:].reshape(xc.shape)
    return rms_norm(x, final_g)
```

```python
import functools

import numpy as np
import jax
import jax.numpy as jnp
from jax import lax
from jax.experimental import pallas as pl
from jax.experimental.pallas import tpu as pltpu

F32 = jnp.float32
BF16 = jnp.bfloat16

D_MODEL = 1024
DEPTH = 2
GRID_W = 64
HEAD_DIM = 64
RET_HEADS = 8
NA_HEADS = 8
RET_W = 512
NA_W = 512
AB_IN = 4 * RET_W + 3 * NA_W
RET_CHUNK = 128
GN_EPS = 1e-5
NA_KH = 8
NA_KW = 16
SWA_HEADS = 16
SWA_KV_HEADS = 4
SWA_WINDOW = 128
SWA_BLOCK = 128
SWA_IN = (SWA_HEADS + 2 * SWA_KV_HEADS) * HEAD_DIM
ROPE_BASE = 10000.0
MOE_GROUPS = 4
MOE_EPG = 8
MOE_EXPERTS = 32
MOE_FF = 512
NORM_EPS = 1e-6
NEG_INF = -1e30

LANES = 128
TM = 256
MOE_ROWS = 256
ROUTE_LANE0 = 4
VMEM_LIMIT = 56 * 1024 * 1024


def _cparams(sem, vmem=VMEM_LIMIT):
    return pltpu.CompilerParams(dimension_semantics=sem, vmem_limit_bytes=vmem)


def _split_bf16(a):
    hi = a.astype(BF16)
    lo = (a - hi.astype(F32)).astype(BF16)
    return hi, lo


def _dot3(a, b):
    ah, al = _split_bf16(a)
    bh, bl = _split_bf16(b)
    d = lambda x, y: jnp.dot(x, y, preferred_element_type=F32)
    return d(ah, bh) + (d(ah, bl) + d(al, bh))


def _dot_nt(a, b):
    return lax.dot_general(a, b, (((1,), (1,)), ((), ())), preferred_element_type=F32)


def _dot_tn(a, b):
    return lax.dot_general(a, b, (((0,), (0,)), ((), ())), preferred_element_type=F32)


def _silu(x):
    return x / (1.0 + jnp.exp(-x))


def _adaln_kernel(c_ref, w_ref, b_ref, o_ref):
    o_ref[...] = _dot3(_silu(c_ref[...]), w_ref[...]) + b_ref[...]


def _adaln(cvec, ada_w, ada_b):
    depth, d, n6 = ada_w.shape
    rows = cvec.shape[0]
    tn = 1024
    return pl.pallas_call(
        _adaln_kernel,
        grid=(depth, n6 // tn),
        in_specs=[
            pl.BlockSpec((rows, d), lambda l, j: (0, 0)),
            pl.BlockSpec((None, d, tn), lambda l, j: (l, 0, j)),
            pl.BlockSpec((None, 1, tn), lambda l, j: (l, 0, j)),
        ],
        out_specs=pl.BlockSpec((None, rows, tn), lambda l, j: (l, 0, j)),
        out_shape=jax.ShapeDtypeStruct((depth, rows, n6), F32),
        compiler_params=_cparams(("arbitrary", "arbitrary")),
        name="adaln",
    )(cvec, ada_w, ada_b.reshape(depth, 1, n6))


def _rms_mod(x, g, sh, sc):
    ms = jnp.mean(x * x, axis=-1, keepdims=True)
    return (x * lax.rsqrt(ms + NORM_EPS) * g) * (1.0 + sc) + sh


def _proj_kernel(x_ref, g_ref, sh_ref, sc_ref, w_ref, rope_ref, o_ref, *, rope_blocks, scale_blocks, cn):
    is_lat = pl.program_id(1) > 0
    hb = _rms_mod(x_ref[...], g_ref[...], sh_ref[...], sc_ref[...]).astype(BF16)
    nout = w_ref.shape[1]
    for c in range(nout // cn):
        o = jnp.dot(hb, w_ref[:, c * cn:(c + 1) * cn], preferred_element_type=F32)
        for s in range(cn // LANES):
            blk = c * (cn // LANES) + s
            ob = o[:, s * LANES:(s + 1) * LANES]
            if blk in rope_blocks:
                r = (ob * rope_ref[0] + pltpu.roll(ob, 16, 1) * rope_ref[1]
                     + pltpu.roll(ob, LANES - 16, 1) * rope_ref[2])
                ob = jnp.where(is_lat, r, ob)
            if blk in scale_blocks:
                ob = ob * (HEAD_DIM ** -0.5)
            o_ref[:, blk * LANES:(blk + 1) * LANES] = ob


def _proj(xa, gain, sh, sc, w_bf16, rope, rope_blocks, scale_blocks):
    b, s, d = xa.shape
    nout = w_bf16.shape[1]
    nt = s // TM
    mod_idx = lambda bi, t: (jnp.where(t == 0, b, bi), 0, 0)
    kern = functools.partial(_proj_kernel, rope_blocks=frozenset(rope_blocks),
                             scale_blocks=frozenset(scale_blocks), cn=512)
    return pl.pallas_call(
        kern,
        grid=(b, nt),
        in_specs=[
            pl.BlockSpec((None, TM, d), lambda bi, t: (bi, t, 0)),
            pl.BlockSpec((1, d), lambda bi, t: (0, 0)),
            pl.BlockSpec((None, 1, d), mod_idx),
            pl.BlockSpec((None, 1, d), mod_idx),
            pl.BlockSpec((d, nout), lambda bi, t: (0, 0)),
            pl.BlockSpec((3, TM, LANES), lambda bi, t: (0, jnp.maximum(t - 1, 0), 0)),
        ],
        out_specs=pl.BlockSpec((None, TM, nout), lambda bi, t: (bi, t, 0)),
        out_shape=jax.ShapeDtypeStruct((b, s, nout), F32),
        compiler_params=_cparams(("arbitrary", "arbitrary")),
        name="proj",
    )(xa, gain, sh, sc, w_bf16, rope)


def _rope_tables(seq):
    nf = HEAD_DIM // 4
    inv = ROPE_BASE ** (-jnp.arange(nf, dtype=F32) / nf)
    t = jnp.arange(seq)
    row = (t // GRID_W).astype(F32)
    col = (t % GRID_W).astype(F32)
    lane = np.arange(LANES)
    jj = lane % HEAD_DIM
    axis_is_col = (jj // 32) == 1
    second_half = (jj % 32) >= 16
    f = jj % 16
    pos = jnp.where(axis_is_col[None, :], col[:, None], row[:, None])
    ang = pos * inv[f][None, :]
    c, s = jnp.cos(ang), jnp.sin(ang)
    sa = jnp.where(second_half[None, :], s, 0.0)
    sb = jnp.where(second_half[None, :], 0.0, -s)
    return jnp.stack([c, sa, sb], axis=0)


def _ret_kernel(lg_ref, q_ref, k_ref, v_ref, g_ref, gn_ref, o_ref, accf_ref, accb_ref):
    hp = pl.program_id(1)
    c = RET_CHUNK
    s_len = q_ref.shape[0]
    n_chunks = s_len // c
    ctx_chunks = TM // c
    pos_c = lax.broadcasted_iota(jnp.int32, (c, 1), 0).astype(F32)
    rel = (lax.broadcasted_iota(jnp.int32, (c, c), 0) - lax.broadcasted_iota(jnp.int32, (c, c), 1)).astype(F32)

    decs = []
    for hh in range(2):
        lgf = lg_ref[0, hp * 2 + hh]
        lgb = lg_ref[1, hp * 2 + hh]
        decs.append(dict(
            intra_f=jnp.where(rel >= 0, jnp.exp(lgf * jnp.maximum(rel, 0.0)), 0.0),
            intra_b=jnp.where(rel <= 0, jnp.exp(lgb * jnp.maximum(-rel, 0.0)), 0.0),
            qd_f=jnp.exp(lgf * (pos_c + 1.0)), kd_f=jnp.exp(lgf * (c - 1.0 - pos_c)),
            cd_f=jnp.exp(lgf * float(c)),
            qd_b=jnp.exp(lgb * (c - pos_c)), kd_b=jnp.exp(lgb * pos_c),
            cd_b=jnp.exp(lgb * float(c)),
        ))

    def chunk(r0, hh, state, intra, qd, kd, cd):
        sl = slice(hh * HEAD_DIM, (hh + 1) * HEAD_DIM)
        q = q_ref[pl.ds(r0, c), :][:, sl]
        k = k_ref[pl.ds(r0, c), :][:, sl]
        v = v_ref[pl.ds(r0, c), :][:, sl].astype(BF16)
        qb = q.astype(BF16)
        s = _dot_nt(qb, k.astype(BF16)) * intra
        out = jnp.dot(s.astype(BF16), v, preferred_element_type=F32)
        out = out + jnp.dot(qb, state.astype(BF16), preferred_element_type=F32) * qd
        new_state = state * cd + _dot_tn((k * kd).astype(BF16), v)
        return out, new_state

    def body(i, states):
        sf0, sf1, sb0, sb1 = states
        rf = pl.multiple_of(i * c, c)
        ib = jnp.where(i < ctx_chunks, ctx_chunks - 1 - i, n_chunks + ctx_chunks - 1 - i)
        rb = pl.multiple_of(ib * c, c)
        of0, sf0 = chunk(rf, 0, sf0, decs[0]["intra_f"], decs[0]["qd_f"], decs[0]["kd_f"], decs[0]["cd_f"])
        of1, sf1 = chunk(rf, 1, sf1, decs[1]["intra_f"], decs[1]["qd_f"], decs[1]["kd_f"], decs[1]["cd_f"])
        ob0, sb0 = chunk(rb, 0, sb0, decs[0]["intra_b"], decs[0]["qd_b"], decs[0]["kd_b"], decs[0]["cd_b"])
        ob1, sb1 = chunk(rb, 1, sb1, decs[1]["intra_b"], decs[1]["qd_b"], decs[1]["kd_b"], decs[1]["cd_b"])
        accf_ref[pl.ds(rf, c), :] = jnp.concatenate([of0, of1], axis=-1)
        accb_ref[pl.ds(rb, c), :] = jnp.concatenate([ob0, ob1], axis=-1)
        return sf0, sf1, sb0, sb1

    z = jnp.zeros((HEAD_DIM, HEAD_DIM), F32)
    lax.fori_loop(0, n_chunks, body, (z, z, z, z))

    lane = lax.broadcasted_iota(jnp.int32, (c, LANES), 1)
    m0 = lane < HEAD_DIM
    inv_n = 1.0 / HEAD_DIM

    def readout(i, carry):
        r0 = pl.multiple_of(i * c, c)
        o = accf_ref[pl.ds(r0, c), :] + accb_ref[pl.ds(r0, c), :]
        s0 = jnp.sum(jnp.where(m0, o, 0.0), axis=-1, keepdims=True)
        s1 = jnp.sum(jnp.where(m0, 0.0, o), axis=-1, keepdims=True)
        dlt = o - jnp.where(m0, s0, s1) * inv_n
        d2 = dlt * dlt
        v0 = jnp.sum(jnp.where(m0, d2, 0.0), axis=-1, keepdims=True)
        v1 = jnp.sum(jnp.where(m0, 0.0, d2), axis=-1, keepdims=True)
        y = dlt * lax.rsqrt(jnp.where(m0, v0, v1) * inv_n + GN_EPS) * gn_ref[...]
        o_ref[pl.ds(r0, c), :] = _silu(g_ref[pl.ds(r0, c), :]) * y
        return carry

    lax.fori_loop(0, n_chunks, readout, 0)


def _retention(p, log_gamma, ret_gn):
    b, s, _ = p.shape
    nb = RET_W // LANES
    blk = lambda off: pl.BlockSpec((None, s, LANES), lambda bi, hp: (bi, 0, off + hp))
    return pl.pallas_call(
        _ret_kernel,
        grid=(b, nb),
        in_specs=[
            pl.BlockSpec(memory_space=pltpu.SMEM),
            blk(0), blk(nb), blk(2 * nb), blk(3 * nb),
            pl.BlockSpec((1, LANES), lambda bi, hp: (0, hp)),
        ],
        out_specs=pl.BlockSpec((None, s, LANES), lambda bi, hp: (bi, 0, hp)),
        out_shape=jax.ShapeDtypeStruct((b, s, RET_W), F32),
        scratch_shapes=[pltpu.VMEM((s, LANES), F32), pltpu.VMEM((s, LANES), F32)],
        compiler_params=_cparams(("arbitrary", "arbitrary")),
        name="retention",
    )(log_gamma, p, p, p, p, ret_gn.reshape(1, RET_W))


def _softmax_pv(s_list, v_list, extra=None):
    m = s_list[0].max(axis=-1, keepdims=True)
    for s in s_list[1:]:
        m = jnp.maximum(m, s.max(axis=-1, keepdims=True))
    if extra is not None:
        m = jnp.maximum(m, extra)
    den = None
    o = None
    for s, v in zip(s_list, v_list):
        p = jnp.exp(s - m)
        ds = p.sum(axis=-1, keepdims=True)
        den = ds if den is None else den + ds
        pv = jnp.dot(p.astype(BF16), v, preferred_element_type=F32)
        o = pv if o is None else o + pv
    if extra is not None:
        den = den + jnp.exp(extra - m)
    return o / den


def _na_kernel(q_ref, k_ref, v_ref, bias_ref, o_ref, qh_ref, kh_ref, vh_ref):
    s_len = q_ref.shape[0]
    rows = (s_len - TM) // GRID_W
    nloc = NA_KH * GRID_W
    for hh in range(2):
        sl = slice(hh * HEAD_DIM, (hh + 1) * HEAD_DIM)
        qh_ref[hh] = q_ref[...][:, sl].astype(BF16)
        kh_ref[hh] = k_ref[...][:, sl].astype(BF16)
        vh_ref[hh] = v_ref[...][:, sl].astype(BF16)

    outs = []
    for hh in range(2):
        kc = kh_ref[hh, 0:TM, :]
        vc = vh_ref[hh, 0:TM, :]
        outs.append(_softmax_pv([_dot_nt(qh_ref[hh, 0:TM, :], kc)], [vc]))
    o_ref[0:TM, :] = jnp.concatenate(outs, axis=-1)

    def row_block(r, carry):
        rs = jnp.clip(r - NA_KH // 2, 0, rows - NA_KH)
        pat = r - rs
        q0 = pl.multiple_of(TM + r * GRID_W, GRID_W)
        k0 = pl.multiple_of(TM + rs * GRID_W, GRID_W)
        res = []
        for hh in range(2):
            q = qh_ref[hh, pl.ds(q0, GRID_W), :]
            kl = kh_ref[hh, pl.ds(k0, nloc), :]
            vl = vh_ref[hh, pl.ds(k0, nloc), :]
            kc = kh_ref[hh, 0:TM, :]
            vc = vh_ref[hh, 0:TM, :]
            s_loc = _dot_nt(q, kl) + bias_ref[pat, hh]
            s_ctx = _dot_nt(q, kc)
            res.append(_softmax_pv([s_loc, s_ctx], [vl, vc]))
        o_ref[pl.ds(q0, GRID_W), :] = jnp.concatenate(res, axis=-1)
        return carry

    lax.fori_loop(0, rows, row_block, 0)


def _na_bias(rpb):
    h = rpb.shape[0]
    qc = np.arange(GRID_W)[:, None]
    kc = np.arange(GRID_W)[None, :]
    win = np.clip(qc - NA_KW // 2, 0, GRID_W - NA_KW)
    valid = (kc >= win) & (kc < win + NA_KW)
    col_off = np.clip(kc - qc + NA_KW - 1, 0, 2 * NA_KW - 2)
    pat = np.arange(NA_KH)[:, None]
    row_off = np.arange(NA_KH)[None, :] - pat + NA_KH - 1
    bias = rpb[:, row_off][:, :, :, col_off]
    bias = jnp.where(valid[None, None, None], bias.astype(F32), NEG_INF)
    bias = bias.transpose(1, 0, 3, 2, 4)
    return bias.reshape(NA_KH, h, GRID_W, NA_KH * GRID_W)


def _neighbourhood(p, bias):
    b, s, _ = p.shape
    nb = NA_W // LANES
    c0 = 4 * RET_W // LANES
    blk = lambda off: pl.BlockSpec((None, s, LANES), lambda bi, hp: (bi, 0, c0 + off + hp))
    return pl.pallas_call(
        _na_kernel,
        grid=(b, nb),
        in_specs=[
            blk(0), blk(nb), blk(2 * nb),
            pl.BlockSpec((NA_KH, 2, GRID_W, NA_KH * GRID_W), lambda bi, hp: (0, hp, 0, 0)),
        ],
        out_specs=pl.BlockSpec((None, s, LANES), lambda bi, hp: (bi, 0, hp)),
        out_shape=jax.ShapeDtypeStruct((b, s, NA_W), F32),
        scratch_shapes=[pltpu.VMEM((2, s, HEAD_DIM), BF16)] * 3,
        compiler_params=_cparams(("arbitrary", "arbitrary")),
        name="neighbourhood",
    )(p, p, p, bias)


def _swa_kernel(sink_ref, q_ref, k_ref, v_ref, o_ref, kh_ref, vh_ref):
    kp = pl.program_id(1)
    s_len = q_ref.shape[0]
    seq = s_len - TM
    nblk = seq // SWA_BLOCK
    band = SWA_BLOCK + 2 * SWA_WINDOW
    group = SWA_HEADS // SWA_KV_HEADS
    for hh in range(2):
        sl = slice(hh * HEAD_DIM, (hh + 1) * HEAD_DIM)
        kh_ref[hh] = k_ref[...][:, sl].astype(BF16)
        vh_ref[hh] = v_ref[...][:, sl].astype(BF16)
    o_ref[0:TM, :] = jnp.zeros((TM, o_ref.shape[1]), F32)

    qi = lax.broadcasted_iota(jnp.int32, (SWA_BLOCK, band), 0)
    ki = lax.broadcasted_iota(jnp.int32, (SWA_BLOCK, band), 1)

    def block(i, carry):
        start = jnp.clip((i - 1) * SWA_BLOCK, 0, seq - band)
        k0 = pl.multiple_of(TM + start, SWA_BLOCK)
        q0 = pl.multiple_of(TM + i * SWA_BLOCK, SWA_BLOCK)
        valid = jnp.abs(qi + (i * SWA_BLOCK - start) - ki) <= SWA_WINDOW
        q_all = q_ref[pl.ds(q0, SWA_BLOCK), :]
        res = []
        for hh in range(2):
            kb = kh_ref[hh, pl.ds(k0, band), :]
            vb = vh_ref[hh, pl.ds(k0, band), :]
            kc = kh_ref[hh, 0:TM, :]
            vc = vh_ref[hh, 0:TM, :]
            for g in range(group):
                hq = hh * group + g
                q = q_all[:, hq * HEAD_DIM:(hq + 1) * HEAD_DIM].astype(BF16)
                s = jnp.where(valid, _dot_nt(q, kb), NEG_INF)
                s_ctx = _dot_nt(q, kc)
                sink = sink_ref[kp * 2 * group + hq]
                res.append(_softmax_pv([s, s_ctx], [vb, vc], extra=sink))
        o_ref[pl.ds(q0, SWA_BLOCK), :] = jnp.concatenate(res, axis=-1)
        return carry

    lax.fori_loop(0, nblk, block, 0)


def _swa(p, sink):
    b, s, _ = p.shape
    qw = SWA_HEADS * HEAD_DIM // 2
    kblk = SWA_HEADS * HEAD_DIM // LANES
    vblk = kblk + SWA_KV_HEADS * HEAD_DIM // LANES
    return pl.pallas_call(
        _swa_kernel,
        grid=(b, 2),
        in_specs=[
            pl.BlockSpec(memory_space=pltpu.SMEM),
            pl.BlockSpec((None, s, qw), lambda bi, kp: (bi, 0, kp)),
            pl.BlockSpec((None, s, LANES), lambda bi, kp: (bi, 0, kblk + kp)),
            pl.BlockSpec((None, s, LANES), lambda bi, kp: (bi, 0, vblk + kp)),
        ],
        out_specs=pl.BlockSpec((None, s, qw), lambda bi, kp: (bi, 0, kp)),
        out_shape=jax.ShapeDtypeStruct((b, s, SWA_HEADS * HEAD_DIM), F32),
        scratch_shapes=[pltpu.VMEM((2, s, HEAD_DIM), BF16)] * 2,
        compiler_params=_cparams(("arbitrary", "arbitrary")),
        name="swa",
    )(sink, p, p, p)


def _outproj_kernel(*refs, n_mix):
    mix_refs = refs[:n_mix]
    (w_ref, x_ref, g1_ref, gn_ref, sh_ref, sc_ref, wr_ref, br_ref,
     xo_ref, h_ref, ri_ref, cnt_ref, carry_ref) = refs[n_mix:]
    first = (pl.program_id(0) == 0) & (pl.program_id(1) == 0)

    @pl.when(first)
    def _():
        carry_ref[...] = jnp.zeros_like(carry_ref)

    o = None
    off = 0
    for m_ref in mix_refs:
        kw = m_ref.shape[1]
        part = jnp.dot(m_ref[...].astype(BF16), w_ref[off:off + kw, :], preferred_element_type=F32)
        o = part if o is None else o + part
        off += kw
    xn = x_ref[...] + g1_ref[...] * o
    xo_ref[...] = xn
    h = _rms_mod(xn, gn_ref[...], sh_ref[...], sc_ref[...])
    h_ref[...] = h

    logits = _dot3(h, wr_ref[...]) + br_ref[...]
    tm = logits.shape[0]
    lane = lax.broadcasted_iota(jnp.int32, (tm, LANES), 1).astype(F32)
    big = 1e9
    gmask = lane < MOE_GROUPS
    mg = jnp.max(jnp.where(gmask, logits, -big), axis=-1, keepdims=True)
    sg = jnp.sum(jnp.where(gmask, jnp.exp(jnp.minimum(logits - mg, 0.0)), 0.0), axis=-1, keepdims=True)
    gw = 1.0 / sg
    gi = jnp.min(jnp.where(gmask & (logits == mg), lane, big), axis=-1, keepdims=True)
    lo = ROUTE_LANE0 + MOE_EPG * gi
    emask = (lane >= lo) & (lane < lo + MOE_EPG)
    l1 = jnp.max(jnp.where(emask, logits, -big), axis=-1, keepdims=True)
    i1 = jnp.min(jnp.where(emask & (logits == l1), lane, big), axis=-1, keepdims=True)
    emask2 = emask & (lane != i1)
    l2 = jnp.max(jnp.where(emask2, logits, -big), axis=-1, keepdims=True)
    i2 = jnp.min(jnp.where(emask2 & (logits == l2), lane, big), axis=-1, keepdims=True)
    e21 = jnp.exp(l2 - l1)
    w1 = gw / (1.0 + e21)
    w2 = gw * e21 / (1.0 + e21)

    oh = jnp.where((lane == i1) | (lane == i2), 1.0, 0.0)
    tri = (lax.broadcasted_iota(jnp.int32, (tm, tm), 0) > lax.broadcasted_iota(jnp.int32, (tm, tm), 1))
    cum = jnp.dot(jnp.where(tri, 1.0, 0.0).astype(BF16), oh.astype(BF16), preferred_element_type=F32) + carry_ref[...]
    r1 = jnp.sum(jnp.where(lane == i1, cum, 0.0), axis=-1, keepdims=True)
    r2 = jnp.sum(jnp.where(lane == i2, cum, 0.0), axis=-1, keepdims=True)
    carry_ref[...] = carry_ref[...] + jnp.sum(oh, axis=0, keepdims=True)
    cnt_ref[...] = carry_ref[...]

    ri = jnp.where(lane == 0, i1 - ROUTE_LANE0, 0.0)
    ri = jnp.where(lane == 1, i2 - ROUTE_LANE0, ri)
    ri = jnp.where(lane == 2, w1, ri)
    ri = jnp.where(lane == 3, w2, ri)
    ri = jnp.where(lane == 4, r1, ri)
    ri = jnp.where(lane == 5, r2, ri)
    ri_ref[...] = ri


def _outproj(mixes, w_bf16, xa, g1, gain2, sh2, sc2, wr, br, t0):
    b, s, d = xa.shape
    nt = s // TM - t0
    mod_idx = lambda bi, t: (jnp.where(t + t0 == 0, b, bi), 0, 0)
    tile = lambda wdt: pl.BlockSpec((None, TM, wdt), lambda bi, t: (bi, t + t0, 0))
    const = lambda shape: pl.BlockSpec(shape, lambda bi, t: (0,) * len(shape))
    in_specs = [tile(m.shape[2]) for m in mixes] + [
        const(w_bf16.shape), tile(d),
        pl.BlockSpec((None, 1, d), mod_idx), const((1, d)),
        pl.BlockSpec((None, 1, d), mod_idx), pl.BlockSpec((None, 1, d), mod_idx),
        const((d, LANES)), const((1, LANES)),
    ]
    return pl.pallas_call(
        functools.partial(_outproj_kernel, n_mix=len(mixes)),
        grid=(b, nt),
        in_specs=in_specs,
        out_specs=[tile(d), tile(d), tile(LANES), const((1, LANES))],
        out_shape=[jax.ShapeDtypeStruct((b, s, d), F32), jax.ShapeDtypeStruct((b, s, d), F32),
                   jax.ShapeDtypeStruct((b, s, LANES), F32), jax.ShapeDtypeStruct((1, LANES), F32)],
        scratch_shapes=[pltpu.VMEM((1, LANES), F32)],
        compiler_params=_cparams(("arbitrary", "arbitrary")),
        name="outproj",
    )(*mixes, w_bf16, xa, g1, gain2, sh2, sc2, wr, br)


def _dispatch_kernel(dest_ref, h_ref, xs_in_ref, xs_ref, sem, *, t0, s_len):
    del xs_in_ref
    bi = pl.program_id(0)
    t = pl.program_id(1)
    nt = pl.num_programs(1)
    step = bi * nt + t
    last = pl.num_programs(0) * nt - 1
    row0 = bi * s_len + (t + t0) * TM

    def copy(src_row, dst_row):
        return pltpu.make_async_copy(h_ref.at[pl.ds(src_row, 1)], xs_ref.at[pl.ds(dst_row, 1)], sem)

    def issue(i, c):
        copy(row0 + i, dest_ref[step, i]).start()
        copy(row0 + i, dest_ref[step, TM + i]).start()
        return c

    lax.fori_loop(0, TM, issue, 0, unroll=8)

    def drain(i, c):
        copy(0, 0).wait()
        return c

    @pl.when(step > 0)
    def _():
        lax.fori_loop(0, 2 * TM, drain, 0, unroll=8)

    @pl.when(step == last)
    def _():
        lax.fori_loop(0, 2 * TM, drain, 0, unroll=8)


def _dispatch(dest, h2, n_pad, t0):
    b, s, d = h2.shape
    nt = s // TM - t0
    xs0 = jnp.zeros((n_pad, d), F32)
    return pl.pallas_call(
        functools.partial(_dispatch_kernel, t0=t0, s_len=s),
        grid_spec=pltpu.PrefetchScalarGridSpec(
            num_scalar_prefetch=1,
            grid=(b, nt),
            in_specs=[pl.BlockSpec(memory_space=pl.ANY), pl.BlockSpec(memory_space=pl.ANY)],
            out_specs=pl.BlockSpec(memory_space=pl.ANY),
            scratch_shapes=[pltpu.SemaphoreType.DMA],
        ),
        out_shape=jax.ShapeDtypeStruct((n_pad, d), F32),
        input_output_aliases={2: 0},
        compiler_params=_cparams(("arbitrary", "arbitrary")),
        name="dispatch",
    )(dest, h2.reshape(b * s, d), xs0)


def _mlp_kernel(be_ref, nu_ref, x_ref, wgu_ref, wd_ref, y_ref, wgu_b, wd_b):
    i = pl.program_id(0)
    prev = be_ref[jnp.maximum(i - 1, 0)]
    used = i < nu_ref[0]

    @pl.when(used & ((i == 0) | (be_ref[i] != prev)))
    def _():
        wgu_b[...] = wgu_ref[...].astype(BF16)
        wd_b[...] = wd_ref[...].astype(BF16)

    @pl.when(used)
    def _():
        gu = jnp.dot(x_ref[...].astype(BF16), wgu_b[...], preferred_element_type=F32)
        act = _silu(gu[:, :MOE_FF]) * gu[:, MOE_FF:]
        y_ref[...] = jnp.dot(act.astype(BF16), wd_b[...], preferred_element_type=F32)

    @pl.when(jnp.logical_not(used))
    def _():
        y_ref[...] = jnp.zeros_like(y_ref)


def _expert_mlp(block_e, n_used, xs, w_gu, w_down, layer):
    n_pad, d = xs.shape
    ff2 = w_gu.shape[-1]
    return pl.pallas_call(
        _mlp_kernel,
        grid_spec=pltpu.PrefetchScalarGridSpec(
            num_scalar_prefetch=2,
            grid=(n_pad // MOE_ROWS,),
            in_specs=[
                pl.BlockSpec((MOE_ROWS, d), lambda i, be, nu: (i, 0)),
                pl.BlockSpec((None, None, d, ff2), lambda i, be, nu: (layer, be[i], 0, 0)),
                pl.BlockSpec((None, None, ff2 // 2, d), lambda i, be, nu: (layer, be[i], 0, 0)),
            ],
            out_specs=pl.BlockSpec((MOE_ROWS, d), lambda i, be, nu: (i, 0)),
            scratch_shapes=[pltpu.VMEM((d, ff2), BF16), pltpu.VMEM((ff2 // 2, d), BF16)],
        ),
        out_shape=jax.ShapeDtypeStruct((n_pad, d), F32),
        compiler_params=_cparams(("arbitrary",)),
        name="expert_mlp",
    )(block_e, n_used, xs, w_gu, w_down)


def _combine_kernel(dest_ref, ys_ref, x_ref, ri_ref, g2_ref, fg_ref, o_ref, buf, sem, *, final):
    bi = pl.program_id(0)
    t = pl.program_id(1)
    nt = pl.num_programs(1)
    step = bi * nt + t
    total = pl.num_programs(0) * nt

    def copy(src_row, slot, k, i):
        return pltpu.make_async_copy(ys_ref.at[pl.ds(src_row, 1)], buf.at[slot, k, pl.ds(i, 1)], sem.at[slot])

    def issue(st, slot):
        def one(i, c):
            copy(dest_ref[st, i], slot, 0, i).start()
            copy(dest_ref[st, TM + i], slot, 1, i).start()
            return c
        lax.fori_loop(0, TM, one, 0, unroll=8)

    slot = step % 2

    @pl.when(step == 0)
    def _():
        issue(0, 0)

    @pl.when(step + 1 < total)
    def _():
        issue(step + 1, 1 - slot)

    def drain(i, c):
        copy(0, slot, 0, 0).wait()
        return c

    lax.fori_loop(0, 2 * TM, drain, 0, unroll=8)

    lane = lax.broadcasted_iota(jnp.int32, (TM, LANES), 1)
    ri = ri_ref[...]
    w1 = jnp.sum(jnp.where(lane == 2, ri, 0.0), axis=-1, keepdims=True)
    w2 = jnp.sum(jnp.where(lane == 3, ri, 0.0), axis=-1, keepdims=True)
    y = buf[slot, 0] * w1 + buf[slot, 1] * w2
    xn = x_ref[...] + g2_ref[...] * y
    if final:
        ms = jnp.mean(xn * xn, axis=-1, keepdims=True)
        xn = xn * lax.rsqrt(ms + NORM_EPS) * fg_ref[...]
    o_ref[...] = xn


def _combine(dest, ys, xa, rinfo, g2, final_g, t0, final):
    b, s, d = xa.shape
    nt = s // TM - t0
    mod_idx = lambda bi, t, dr: (jnp.where(t + t0 == 0, b, bi), 0, 0)
    tile = lambda wdt: pl.BlockSpec((None, TM, wdt), lambda bi, t, dr: (bi, t + t0, 0))
    if final:
        out_spec = pl.BlockSpec((None, TM, d), lambda bi, t, dr: (bi, t, 0))
        out_shape = jax.ShapeDtypeStruct((b, nt * TM, d), F32)
    else:
        out_spec = tile(d)
        out_shape = jax.ShapeDtypeStruct((b, s, d), F32)
    return pl.pallas_call(
        functools.partial(_combine_kernel, final=final),
        grid_spec=pltpu.PrefetchScalarGridSpec(
            num_scalar_prefetch=1,
            grid=(b, nt),
            in_specs=[
                pl.BlockSpec(memory_space=pl.ANY),
                tile(d), tile(LANES),
                pl.BlockSpec((None, 1, d), mod_idx),
                pl.BlockSpec((1, d), lambda bi, t, dr: (0, 0)),
            ],
            out_specs=out_spec,
            scratch_shapes=[pltpu.VMEM((2, 2, TM, d), F32), pltpu.SemaphoreType.DMA((2,))],
        ),
        out_shape=out_shape,
        compiler_params=_cparams(("arbitrary", "arbitrary")),
        name="combine",
    )(dest, ys, xa, rinfo, g2, final_g)


def _moe(h2, rinfo, counts, w_gu, w_down, layer, t0):
    b, s, d = h2.shape
    nt = s // TM - t0
    r = rinfo[:, t0 * TM:, :8]
    e = r[..., 0:2].astype(jnp.int32)
    rank = r[..., 4:6].astype(jnp.int32)
    cnt = counts[0, ROUTE_LANE0:ROUTE_LANE0 + MOE_EXPERTS].astype(jnp.int32)
    padded = (cnt + MOE_ROWS - 1) // MOE_ROWS * MOE_ROWS
    ends = jnp.cumsum(padded)
    starts = ends - padded
    dest = starts[e] + rank
    dest = dest.reshape(b, nt, TM, 2).transpose(0, 1, 3, 2).reshape(b * nt, 2 * TM)
    n_assign = b * nt * TM * 2
    n_blocks = (n_assign + MOE_EXPERTS * (MOE_ROWS - 1)) // MOE_ROWS + 1
    n_pad = n_blocks * MOE_ROWS
    block_e = jnp.minimum(jnp.searchsorted(ends, jnp.arange(n_blocks) * MOE_ROWS, side="right"),
                          MOE_EXPERTS - 1).astype(jnp.int32)
    n_used = (ends[-1] // MOE_ROWS).astype(jnp.int32).reshape(1)
    xs = _dispatch(dest, h2, n_pad, t0)
    ys = _expert_mlp(block_e, n_used, xs, w_gu, w_down, layer)
    return dest, ys


def _router_weights(wg, bg, we, be):
    d = wg.shape[0]
    wr = jnp.zeros((d, LANES), F32)
    wr = wr.at[:, :MOE_GROUPS].set(wg).at[:, ROUTE_LANE0:ROUTE_LANE0 + MOE_EXPERTS].set(we)
    br = jnp.zeros((1, LANES), F32)
    br = br.at[0, :MOE_GROUPS].set(bg).at[0, ROUTE_LANE0:ROUTE_LANE0 + MOE_EXPERTS].set(be)
    return wr, br


def kernel(x, c, ctx, c_ctx, ada_w, ada_b, norm_g, final_g, ab_w_in, ab_w_out, ret_decay, ret_gn, na_rpb,
           swa_w_in, swa_w_out, swa_sink, router_g_w, router_g_b, router_e_w, router_e_b,
           expert_w_gu, expert_w_down):
    b, seq, d = x.shape
    assert ctx.shape[1] == TM and seq % TM == 0 and d == D_MODEL
    xa = jnp.concatenate([ctx, x], axis=1)
    rope = _rope_tables(seq)

    cvec = jnp.concatenate([c, c_ctx[None, :], jnp.zeros((7, d), F32)], axis=0)
    mod = _adaln(cvec, ada_w, ada_b)
    mod = mod.reshape(DEPTH, b + 8, 6, 1, d)[:, :b + 1].transpose(0, 2, 1, 3, 4)

    for layer in range(DEPTH):
        last = layer == DEPTH - 1
        sh1, sc1, g1, sh2, sc2, g2 = (mod[layer, i] for i in range(6))
        gain1 = norm_g[layer, 0].reshape(1, d)
        gain2 = norm_g[layer, 1].reshape(1, d)
        j = layer // 2
        if layer % 2 == 0:
            nb = RET_W // LANES
            p = _proj(xa, gain1, sh1, sc1, ab_w_in[j].astype(BF16), rope,
                      rope_blocks=range(0, 2 * nb),
                      scale_blocks=list(range(nb, 2 * nb)) + list(range(4 * nb, 5 * nb)))
            log_gamma = jnp.log1p(-jnp.exp2(-ret_decay[j].astype(F32)))
            mixes = [_retention(p, log_gamma, ret_gn[j]), _neighbourhood(p, _na_bias(na_rpb[j]))]
            w_out = ab_w_out[j]
        else:
            qb = SWA_HEADS * HEAD_DIM // LANES
            kb = SWA_KV_HEADS * HEAD_DIM // LANES
            p = _proj(xa, gain1, sh1, sc1, swa_w_in[j].astype(BF16), rope,
                      rope_blocks=range(0, qb + kb), scale_blocks=range(0, qb))
            mixes = [_swa(p, swa_sink[j].astype(F32))]
            w_out = swa_w_out[j]
        t0 = 1 if last else 0
        wr, br = _router_weights(router_g_w[layer], router_g_b[layer], router_e_w[layer], router_e_b[layer])
        xa, h2, rinfo, counts = _outproj(mixes, w_out.astype(BF16), xa, g1, gain2, sh2, sc2, wr, br, t0)
        dest, ys = _moe(h2, rinfo, counts, expert_w_gu, expert_w_down, layer, t0)
        xa = _combine(dest, ys, xa, rinfo, g2, final_g.reshape(1, d), t0, last)
    return xa
```

```python
import functools

import numpy as np
import jax
import jax.numpy as jnp
from jax import lax
from jax.experimental import pallas as pl
from jax.experimental.pallas import tpu as pltpu

F32 = jnp.float32
BF16 = jnp.bfloat16

D_MODEL = 1024
DEPTH = 2
GRID_W = 64
HEAD_DIM = 64
RET_HEADS = 8
NA_HEADS = 8
RET_W = 512
NA_W = 512
AB_IN = 4 * RET_W + 3 * NA_W
RET_CHUNK = 128
GN_EPS = 1e-5
NA_KH = 8
NA_KW = 16
SWA_HEADS = 16
SWA_KV_HEADS = 4
SWA_WINDOW = 128
SWA_BLOCK = 128
SWA_IN = (SWA_HEADS + 2 * SWA_KV_HEADS) * HEAD_DIM
ROPE_BASE = 10000.0
MOE_GROUPS = 4
MOE_EPG = 8
MOE_EXPERTS = 32
MOE_FF = 512
NORM_EPS = 1e-6
NEG_INF = -1e30

LANES = 128
TM = 256
MOE_ROWS = 256
ROUTE_LANE0 = 4
VMEM_LIMIT = 56 * 1024 * 1024


def _cparams(sem, vmem=VMEM_LIMIT):
    return pltpu.CompilerParams(dimension_semantics=sem, vmem_limit_bytes=vmem)


def _split_bf16(a):
    hi = a.astype(BF16)
    lo = (a - hi.astype(F32)).astype(BF16)
    return hi, lo


def _dot3(a, b):
    ah, al = _split_bf16(a)
    bh, bl = _split_bf16(b)
    d = lambda x, y: jnp.dot(x, y, preferred_element_type=F32)
    return d(ah, bh) + (d(ah, bl) + d(al, bh))


def _dot_nt(a, b):
    return lax.dot_general(a, b, (((1,), (1,)), ((), ())), preferred_element_type=F32)


def _dot_tn(a, b):
    return lax.dot_general(a, b, (((0,), (0,)), ((), ())), preferred_element_type=F32)


def _silu(x):
    return x / (1.0 + jnp.exp(-x))


def _adaln_kernel(c_ref, w_ref, b_ref, o_ref):
    o_ref[...] = _dot3(_silu(c_ref[...]), w_ref[...]) + b_ref[...]


def _adaln(cvec, ada_w, ada_b):
    depth, d, n6 = ada_w.shape
    rows = cvec.shape[0]
    tn = 1024
    return pl.pallas_call(
        _adaln_kernel,
        grid=(depth, n6 // tn),
        in_specs=[
            pl.BlockSpec((rows, d), lambda l, j: (0, 0)),
            pl.BlockSpec((None, d, tn), lambda l, j: (l, 0, j)),
            pl.BlockSpec((None, 1, tn), lambda l, j: (l, 0, j)),
        ],
        out_specs=pl.BlockSpec((None, rows, tn), lambda l, j: (l, 0, j)),
        out_shape=jax.ShapeDtypeStruct((depth, rows, n6), F32),
        compiler_params=_cparams(("arbitrary", "arbitrary")),
        name="adaln",
    )(cvec, ada_w, ada_b.reshape(depth, 1, n6))


def _rms_mod(x, g, sh, sc):
    ms = jnp.mean(x * x, axis=-1, keepdims=True)
    return (x * lax.rsqrt(ms + NORM_EPS) * g) * (1.0 + sc) + sh


def _proj_kernel(x_ref, g_ref, sh_ref, sc_ref, w_ref, rope_ref, o_ref, *, rope_blocks, scale_blocks, cn):
    is_lat = pl.program_id(1) > 0
    hb = _rms_mod(x_ref[...], g_ref[...], sh_ref[...], sc_ref[...]).astype(BF16)
    nout = w_ref.shape[1]
    for c in range(nout // cn):
        o = jnp.dot(hb, w_ref[:, c * cn:(c + 1) * cn], preferred_element_type=F32)
        for s in range(cn // LANES):
            blk = c * (cn // LANES) + s
            ob = o[:, s * LANES:(s + 1) * LANES]
            if blk in rope_blocks:
                r = (ob * rope_ref[0] + pltpu.roll(ob, 16, 1) * rope_ref[1]
                     + pltpu.roll(ob, LANES - 16, 1) * rope_ref[2])
                ob = jnp.where(is_lat, r, ob)
            if blk in scale_blocks:
                ob = ob * (HEAD_DIM ** -0.5)
            o_ref[:, blk * LANES:(blk + 1) * LANES] = ob


def _proj(xa, gain, sh, sc, w_bf16, rope, rope_blocks, scale_blocks):
    b, s, d = xa.shape
    nout = w_bf16.shape[1]
    nt = s // TM
    mod_idx = lambda bi, t: (jnp.where(t == 0, b, bi), 0, 0)
    kern = functools.partial(_proj_kernel, rope_blocks=frozenset(rope_blocks),
                             scale_blocks=frozenset(scale_blocks), cn=512)
    return pl.pallas_call(
        kern,
        grid=(b, nt),
        in_specs=[
            pl.BlockSpec((None, TM, d), lambda bi, t: (bi, t, 0)),
            pl.BlockSpec((1, d), lambda bi, t: (0, 0)),
            pl.BlockSpec((None, 1, d), mod_idx),
            pl.BlockSpec((None, 1, d), mod_idx),
            pl.BlockSpec((d, nout), lambda bi, t: (0, 0)),
            pl.BlockSpec((3, TM, LANES), lambda bi, t: (0, jnp.maximum(t - 1, 0), 0)),
        ],
        out_specs=pl.BlockSpec((None, TM, nout), lambda bi, t: (bi, t, 0)),
        out_shape=jax.ShapeDtypeStruct((b, s, nout), F32),
        compiler_params=_cparams(("arbitrary", "arbitrary")),
        name="proj",
    )(xa, gain, sh, sc, w_bf16, rope)


def _rope_tables(seq):
    nf = HEAD_DIM // 4
    inv = ROPE_BASE ** (-jnp.arange(nf, dtype=F32) / nf)
    t = jnp.arange(seq)
    row = (t // GRID_W).astype(F32)
    col = (t % GRID_W).astype(F32)
    lane = np.arange(LANES)
    jj = lane % HEAD_DIM
    axis_is_col = (jj // 32) == 1
    second_half = (jj % 32) >= 16
    f = jj % 16
    pos = jnp.where(axis_is_col[None, :], col[:, None], row[:, None])
    ang = pos * inv[f][None, :]
    c, s = jnp.cos(ang), jnp.sin(ang)
    sa = jnp.where(second_half[None, :], s, 0.0)
    sb = jnp.where(second_half[None, :], 0.0, -s)
    return jnp.stack([c, sa, sb], axis=0)


def _ret_kernel(lg_ref, q_ref, k_ref, v_ref, g_ref, gn_ref, o_ref, accf_ref, accb_ref):
    hp = pl.program_id(1)
    c = RET_CHUNK
    s_len = q_ref.shape[0]
    n_chunks = s_len // c
    ctx_chunks = TM // c
    pos_c = lax.broadcasted_iota(jnp.int32, (c, 1), 0).astype(F32)
    rel = (lax.broadcasted_iota(jnp.int32, (c, c), 0) - lax.broadcasted_iota(jnp.int32, (c, c), 1)).astype(F32)

    decs = []
    for hh in range(2):
        lgf = lg_ref[0, hp * 2 + hh]
        lgb = lg_ref[1, hp * 2 + hh]
        decs.append(dict(
            intra_f=jnp.where(rel >= 0, jnp.exp(lgf * jnp.maximum(rel, 0.0)), 0.0),
            intra_b=jnp.where(rel <= 0, jnp.exp(lgb * jnp.maximum(-rel, 0.0)), 0.0),
            qd_f=jnp.exp(lgf * (pos_c + 1.0)), kd_f=jnp.exp(lgf * (c - 1.0 - pos_c)),
            cd_f=jnp.exp(lgf * float(c)),
            qd_b=jnp.exp(lgb * (c - pos_c)), kd_b=jnp.exp(lgb * pos_c),
            cd_b=jnp.exp(lgb * float(c)),
        ))

    def chunk(r0, hh, state, intra, qd, kd, cd):
        sl = slice(hh * HEAD_DIM, (hh + 1) * HEAD_DIM)
        q = q_ref[pl.ds(r0, c), :][:, sl]
        k = k_ref[pl.ds(r0, c), :][:, sl]
        v = v_ref[pl.ds(r0, c), :][:, sl].astype(BF16)
        qb = q.astype(BF16)
        s = _dot_nt(qb, k.astype(BF16)) * intra
        out = jnp.dot(s.astype(BF16), v, preferred_element_type=F32)
        out = out + jnp.dot(qb, state.astype(BF16), preferred_element_type=F32) * qd
        new_state = state * cd + _dot_tn((k * kd).astype(BF16), v)
        return out, new_state

    def body(i, states):
        sf0, sf1, sb0, sb1 = states
        rf = pl.multiple_of(i * c, c)
        ib = jnp.where(i < ctx_chunks, ctx_chunks - 1 - i, n_chunks + ctx_chunks - 1 - i)
        rb = pl.multiple_of(ib * c, c)
        of0, sf0 = chunk(rf, 0, sf0, decs[0]["intra_f"], decs[0]["qd_f"], decs[0]["kd_f"], decs[0]["cd_f"])
        of1, sf1 = chunk(rf, 1, sf1, decs[1]["intra_f"], decs[1]["qd_f"], decs[1]["kd_f"], decs[1]["cd_f"])
        ob0, sb0 = chunk(rb, 0, sb0, decs[0]["intra_b"], decs[0]["qd_b"], decs[0]["kd_b"], decs[0]["cd_b"])
        ob1, sb1 = chunk(rb, 1, sb1, decs[1]["intra_b"], decs[1]["qd_b"], decs[1]["kd_b"], decs[1]["cd_b"])
        accf_ref[pl.ds(rf, c), :] = jnp.concatenate([of0, of1], axis=-1)
        accb_ref[pl.ds(rb, c), :] = jnp.concatenate([ob0, ob1], axis=-1)
        return sf0, sf1, sb0, sb1

    z = jnp.zeros((HEAD_DIM, HEAD_DIM), F32)
    lax.fori_loop(0, n_chunks, body, (z, z, z, z))

    lane = lax.broadcasted_iota(jnp.int32, (c, LANES), 1)
    m0 = lane < HEAD_DIM
    inv_n = 1.0 / HEAD_DIM

    def readout(i, carry):
        r0 = pl.multiple_of(i * c, c)
        o = accf_ref[pl.ds(r0, c), :] + accb_ref[pl.ds(r0, c), :]
        s0 = jnp.sum(jnp.where(m0, o, 0.0), axis=-1, keepdims=True)
        s1 = jnp.sum(jnp.where(m0, 0.0, o), axis=-1, keepdims=True)
        dlt = o - jnp.where(m0, s0, s1) * inv_n
        d2 = dlt * dlt
        v0 = jnp.sum(jnp.where(m0, d2, 0.0), axis=-1, keepdims=True)
        v1 = jnp.sum(jnp.where(m0, 0.0, d2), axis=-1, keepdims=True)
        y = dlt * lax.rsqrt(jnp.where(m0, v0, v1) * inv_n + GN_EPS) * gn_ref[...]
        o_ref[pl.ds(r0, c), :] = _silu(g_ref[pl.ds(r0, c), :]) * y
        return carry

    lax.fori_loop(0, n_chunks, readout, 0)


def _retention(p, log_gamma, ret_gn):
    b, s, _ = p.shape
    nb = RET_W // LANES
    blk = lambda off: pl.BlockSpec((None, s, LANES), lambda bi, hp: (bi, 0, off + hp))
    return pl.pallas_call(
        _ret_kernel,
        grid=(b, nb),
        in_specs=[
            pl.BlockSpec(memory_space=pltpu.SMEM),
            blk(0), blk(nb), blk(2 * nb), blk(3 * nb),
            pl.BlockSpec((1, LANES), lambda bi, hp: (0, hp)),
        ],
        out_specs=pl.BlockSpec((None, s, LANES), lambda bi, hp: (bi, 0, hp)),
        out_shape=jax.ShapeDtypeStruct((b, s, RET_W), F32),
        scratch_shapes=[pltpu.VMEM((s, LANES), F32), pltpu.VMEM((s, LANES), F32)],
        compiler_params=_cparams(("arbitrary", "arbitrary")),
        name="retention",
    )(log_gamma, p, p, p, p, ret_gn.reshape(1, RET_W))


def _softmax_pv(s_list, v_list, extra=None):
    m = s_list[0].max(axis=-1, keepdims=True)
    for s in s_list[1:]:
        m = jnp.maximum(m, s.max(axis=-1, keepdims=True))
    if extra is not None:
        m = jnp.maximum(m, extra)
    den = None
    o = None
    for s, v in zip(s_list, v_list):
        p = jnp.exp(s - m)
        ds = p.sum(axis=-1, keepdims=True)
        den = ds if den is None else den + ds
        pv = jnp.dot(p.astype(BF16), v, preferred_element_type=F32)
        o = pv if o is None else o + pv
    if extra is not None:
        den = den + jnp.exp(extra - m)
    return o / den


def _na_kernel(q_ref, k_ref, v_ref, bias_ref, o_ref, qh_ref, kh_ref, vh_ref):
    s_len = q_ref.shape[0]
    rows = (s_len - TM) // GRID_W
    nloc = NA_KH * GRID_W
    for hh in range(2):
        sl = slice(hh * HEAD_DIM, (hh + 1) * HEAD_DIM)
        qh_ref[hh] = q_ref[...][:, sl].astype(BF16)
        kh_ref[hh] = k_ref[...][:, sl].astype(BF16)
        vh_ref[hh] = v_ref[...][:, sl].astype(BF16)

    outs = []
    for hh in range(2):
        kc = kh_ref[hh, 0:TM, :]
        vc = vh_ref[hh, 0:TM, :]
        outs.append(_softmax_pv([_dot_nt(qh_ref[hh, 0:TM, :], kc)], [vc]))
    o_ref[0:TM, :] = jnp.concatenate(outs, axis=-1)

    def row_block(r, carry):
        rs = jnp.clip(r - NA_KH // 2, 0, rows - NA_KH)
        pat = r - rs
        q0 = pl.multiple_of(TM + r * GRID_W, GRID_W)
        k0 = pl.multiple_of(TM + rs * GRID_W, GRID_W)
        res = []
        for hh in range(2):
            q = qh_ref[hh, pl.ds(q0, GRID_W), :]
            kl = kh_ref[hh, pl.ds(k0, nloc), :]
            vl = vh_ref[hh, pl.ds(k0, nloc), :]
            kc = kh_ref[hh, 0:TM, :]
            vc = vh_ref[hh, 0:TM, :]
            s_loc = _dot_nt(q, kl) + bias_ref[pat, hh]
            s_ctx = _dot_nt(q, kc)
            res.append(_softmax_pv([s_loc, s_ctx], [vl, vc]))
        o_ref[pl.ds(q0, GRID_W), :] = jnp.concatenate(res, axis=-1)
        return carry

    lax.fori_loop(0, rows, row_block, 0)


def _na_bias(rpb):
    h = rpb.shape[0]
    qc = np.arange(GRID_W)[:, None]
    kc = np.arange(GRID_W)[None, :]
    win = np.clip(qc - NA_KW // 2, 0, GRID_W - NA_KW)
    valid = (kc >= win) & (kc < win + NA_KW)
    col_off = np.clip(kc - qc + NA_KW - 1, 0, 2 * NA_KW - 2)
    pat = np.arange(NA_KH)[:, None]
    row_off = np.arange(NA_KH)[None, :] - pat + NA_KH - 1
    bias = rpb[:, row_off][:, :, :, col_off]
    bias = jnp.where(valid[None, None, None], bias.astype(F32), NEG_INF)
    bias = bias.transpose(1, 0, 3, 2, 4)
    return bias.reshape(NA_KH, h, GRID_W, NA_KH * GRID_W)


def _neighbourhood(p, bias):
    b, s, _ = p.shape
    nb = NA_W // LANES
    c0 = 4 * RET_W // LANES
    blk = lambda off: pl.BlockSpec((None, s, LANES), lambda bi, hp: (bi, 0, c0 + off + hp))
    return pl.pallas_call(
        _na_kernel,
        grid=(b, nb),
        in_specs=[
            blk(0), blk(nb), blk(2 * nb),
            pl.BlockSpec((NA_KH, 2, GRID_W, NA_KH * GRID_W), lambda bi, hp: (0, hp, 0, 0)),
        ],
        out_specs=pl.BlockSpec((None, s, LANES), lambda bi, hp: (bi, 0, hp)),
        out_shape=jax.ShapeDtypeStruct((b, s, NA_W), F32),
        scratch_shapes=[pltpu.VMEM((2, s, HEAD_DIM), BF16)] * 3,
        compiler_params=_cparams(("arbitrary", "arbitrary")),
        name="neighbourhood",
    )(p, p, p, bias)


def _swa_kernel(sink_ref, q_ref, k_ref, v_ref, o_ref, kh_ref, vh_ref):
    kp = pl.program_id(1)
    s_len = q_ref.shape[0]
    seq = s_len - TM
    nblk = seq // SWA_BLOCK
    band = SWA_BLOCK + 2 * SWA_WINDOW
    group = SWA_HEADS // SWA_KV_HEADS
    for hh in range(2):
        sl = slice(hh * HEAD_DIM, (hh + 1) * HEAD_DIM)
        kh_ref[hh] = k_ref[...][:, sl].astype(BF16)
        vh_ref[hh] = v_ref[...][:, sl].astype(BF16)
    o_ref[0:TM, :] = jnp.zeros((TM, o_ref.shape[1]), F32)

    qi = lax.broadcasted_iota(jnp.int32, (SWA_BLOCK, band), 0)
    ki = lax.broadcasted_iota(jnp.int32, (SWA_BLOCK, band), 1)

    def block(i, carry):
        start = jnp.clip((i - 1) * SWA_BLOCK, 0, seq - band)
        k0 = pl.multiple_of(TM + start, SWA_BLOCK)
        q0 = pl.multiple_of(TM + i * SWA_BLOCK, SWA_BLOCK)
        valid = jnp.abs(qi + (i * SWA_BLOCK - start) - ki) <= SWA_WINDOW
        q_all = q_ref[pl.ds(q0, SWA_BLOCK), :]
        res = []
        for hh in range(2):
            kb = kh_ref[hh, pl.ds(k0, band), :]
            vb = vh_ref[hh, pl.ds(k0, band), :]
            kc = kh_ref[hh, 0:TM, :]
            vc = vh_ref[hh, 0:TM, :]
            for g in range(group):
                hq = hh * group + g
                q = q_all[:, hq * HEAD_DIM:(hq + 1) * HEAD_DIM].astype(BF16)
                s = jnp.where(valid, _dot_nt(q, kb), NEG_INF)
                s_ctx = _dot_nt(q, kc)
                sink = sink_ref[kp * 2 * group + hq]
                res.append(_softmax_pv([s, s_ctx], [vb, vc], extra=sink))
        o_ref[pl.ds(q0, SWA_BLOCK), :] = jnp.concatenate(res, axis=-1)
        return carry

    lax.fori_loop(0, nblk, block, 0)


def _swa(p, sink):
    b, s, _ = p.shape
    qw = SWA_HEADS * HEAD_DIM // 2
    kblk = SWA_HEADS * HEAD_DIM // LANES
    vblk = kblk + SWA_KV_HEADS * HEAD_DIM // LANES
    return pl.pallas_call(
        _swa_kernel,
        grid=(b, 2),
        in_specs=[
            pl.BlockSpec(memory_space=pltpu.SMEM),
            pl.BlockSpec((None, s, qw), lambda bi, kp: (bi, 0, kp)),
            pl.BlockSpec((None, s, LANES), lambda bi, kp: (bi, 0, kblk + kp)),
            pl.BlockSpec((None, s, LANES), lambda bi, kp: (bi, 0, vblk + kp)),
        ],
        out_specs=pl.BlockSpec((None, s, qw), lambda bi, kp: (bi, 0, kp)),
        out_shape=jax.ShapeDtypeStruct((b, s, SWA_HEADS * HEAD_DIM), F32),
        scratch_shapes=[pltpu.VMEM((2, s, HEAD_DIM), BF16)] * 2,
        compiler_params=_cparams(("arbitrary", "arbitrary")),
        name="swa",
    )(sink, p, p, p)


def _outproj_kernel(*refs, n_mix):
    mix_refs = refs[:n_mix]
    (w_ref, x_ref, g1_ref, gn_ref, sh_ref, sc_ref, wr_ref, br_ref,
     xo_ref, h_ref, ri_ref, cnt_ref, carry_ref) = refs[n_mix:]
    first = (pl.program_id(0) == 0) & (pl.program_id(1) == 0)

    @pl.when(first)
    def _():
        carry_ref[...] = jnp.zeros_like(carry_ref)

    o = None
    off = 0
    for m_ref in mix_refs:
        kw = m_ref.shape[1]
        part = jnp.dot(m_ref[...].astype(BF16), w_ref[off:off + kw, :], preferred_element_type=F32)
        o = part if o is None else o + part
        off += kw
    xn = x_ref[...] + g1_ref[...] * o
    xo_ref[...] = xn
    h = _rms_mod(xn, gn_ref[...], sh_ref[...], sc_ref[...])
    h_ref[...] = h

    logits = _dot3(h, wr_ref[...]) + br_ref[...]
    tm = logits.shape[0]
    lane = lax.broadcasted_iota(jnp.int32, (tm, LANES), 1).astype(F32)
    big = 1e9
    gmask = lane < MOE_GROUPS
    mg = jnp.max(jnp.where(gmask, logits, -big), axis=-1, keepdims=True)
    sg = jnp.sum(jnp.where(gmask, jnp.exp(jnp.minimum(logits - mg, 0.0)), 0.0), axis=-1, keepdims=True)
    gw = 1.0 / sg
    gi = jnp.min(jnp.where(gmask & (logits == mg), lane, big), axis=-1, keepdims=True)
    lo = ROUTE_LANE0 + MOE_EPG * gi
    emask = (lane >= lo) & (lane < lo + MOE_EPG)
    l1 = jnp.max(jnp.where(emask, logits, -big), axis=-1, keepdims=True)
    i1 = jnp.min(jnp.where(emask & (logits == l1), lane, big), axis=-1, keepdims=True)
    emask2 = emask & (lane != i1)
    l2 = jnp.max(jnp.where(emask2, logits, -big), axis=-1, keepdims=True)
    i2 = jnp.min(jnp.where(emask2 & (logits == l2), lane, big), axis=-1, keepdims=True)
    e21 = jnp.exp(l2 - l1)
    w1 = gw / (1.0 + e21)
    w2 = gw * e21 / (1.0 + e21)

    oh = jnp.where((lane == i1) | (lane == i2), 1.0, 0.0)
    tri = (lax.broadcasted_iota(jnp.int32, (tm, tm), 0) > lax.broadcasted_iota(jnp.int32, (tm, tm), 1))
    cum = jnp.dot(jnp.where(tri, 1.0, 0.0).astype(BF16), oh.astype(BF16), preferred_element_type=F32) + carry_ref[...]
    r1 = jnp.sum(jnp.where(lane == i1, cum, 0.0), axis=-1, keepdims=True)
    r2 = jnp.sum(jnp.where(lane == i2, cum, 0.0), axis=-1, keepdims=True)
    carry_ref[...] = carry_ref[...] + jnp.sum(oh, axis=0, keepdims=True)
    cnt_ref[...] = carry_ref[...]

    ri = jnp.where(lane == 0, i1 - ROUTE_LANE0, 0.0)
    ri = jnp.where(lane == 1, i2 - ROUTE_LANE0, ri)
    ri = jnp.where(lane == 2, w1, ri)
    ri = jnp.where(lane == 3, w2, ri)
    ri = jnp.where(lane == 4, r1, ri)
    ri = jnp.where(lane == 5, r2, ri)
    ri_ref[...] = ri


def _outproj(mixes, w_bf16, xa, g1, gain2, sh2, sc2, wr, br, t0):
    b, s, d = xa.shape
    nt = s // TM - t0
    mod_idx = lambda bi, t: (jnp.where(t + t0 == 0, b, bi), 0, 0)
    tile = lambda wdt: pl.BlockSpec((None, TM, wdt), lambda bi, t: (bi, t + t0, 0))
    const = lambda shape: pl.BlockSpec(shape, lambda bi, t: (0,) * len(shape))
    in_specs = [tile(m.shape[2]) for m in mixes] + [
        const(w_bf16.shape), tile(d),
        pl.BlockSpec((None, 1, d), mod_idx), const((1, d)),
        pl.BlockSpec((None, 1, d), mod_idx), pl.BlockSpec((None, 1, d), mod_idx),
        const((d, LANES)), const((1, LANES)),
    ]
    return pl.pallas_call(
        functools.partial(_outproj_kernel, n_mix=len(mixes)),
        grid=(b, nt),
        in_specs=in_specs,
        out_specs=[tile(d), tile(d), tile(LANES), const((1, LANES))],
        out_shape=[jax.ShapeDtypeStruct((b, s, d), F32), jax.ShapeDtypeStruct((b, s, d), F32),
                   jax.ShapeDtypeStruct((b, s, LANES), F32), jax.ShapeDtypeStruct((1, LANES), F32)],
        scratch_shapes=[pltpu.VMEM((1, LANES), F32)],
        compiler_params=_cparams(("arbitrary", "arbitrary")),
        name="outproj",
    )(*mixes, w_bf16, xa, g1, gain2, sh2, sc2, wr, br)


def _dispatch_kernel(dest_ref, h_ref, xs_in_ref, xs_ref, hbuf, sem):
    del xs_in_ref
    nt = pl.num_programs(1)
    step = pl.program_id(0) * nt + pl.program_id(1)
    last = pl.num_programs(0) * nt - 1
    slot = step % 2
    hbuf[slot] = h_ref[...]

    def copy(i, dst_row):
        return pltpu.make_async_copy(hbuf.at[slot, pl.ds(i, 1)], xs_ref.at[pl.ds(dst_row, 1)], sem)

    def issue(i, c):
        copy(i, dest_ref[step, i]).start()
        copy(i, dest_ref[step, TM + i]).start()
        return c

    lax.fori_loop(0, TM, issue, 0, unroll=8)

    def drain(i, c):
        copy(0, 0).wait()
        return c

    @pl.when(step > 0)
    def _():
        lax.fori_loop(0, 2 * TM, drain, 0, unroll=8)

    @pl.when(step == last)
    def _():
        lax.fori_loop(0, 2 * TM, drain, 0, unroll=8)


def _dispatch(dest, h2, n_pad, t0):
    b, s, d = h2.shape
    nt = s // TM - t0
    xs0 = jnp.zeros((n_pad, d), F32)
    return pl.pallas_call(
        _dispatch_kernel,
        grid_spec=pltpu.PrefetchScalarGridSpec(
            num_scalar_prefetch=1,
            grid=(b, nt),
            in_specs=[pl.BlockSpec((None, TM, d), lambda bi, t, dr: (bi, t + t0, 0)),
                      pl.BlockSpec(memory_space=pl.ANY)],
            out_specs=pl.BlockSpec(memory_space=pl.ANY),
            scratch_shapes=[pltpu.VMEM((2, TM, d), F32), pltpu.SemaphoreType.DMA],
        ),
        out_shape=jax.ShapeDtypeStruct((n_pad, d), F32),
        input_output_aliases={2: 0},
        compiler_params=_cparams(("arbitrary", "arbitrary")),
        name="dispatch",
    )(dest, h2, xs0)


def _mlp_kernel(be_ref, nu_ref, x_ref, wgu_ref, wd_ref, y_ref, wgu_b, wd_b):
    i = pl.program_id(0)
    prev = be_ref[jnp.maximum(i - 1, 0)]
    used = i < nu_ref[0]

    @pl.when(used & ((i == 0) | (be_ref[i] != prev)))
    def _():
        wgu_b[...] = wgu_ref[...].astype(BF16)
        wd_b[...] = wd_ref[...].astype(BF16)

    @pl.when(used)
    def _():
        gu = jnp.dot(x_ref[...].astype(BF16), wgu_b[...], preferred_element_type=F32)
        act = _silu(gu[:, :MOE_FF]) * gu[:, MOE_FF:]
        y_ref[...] = jnp.dot(act.astype(BF16), wd_b[...], preferred_element_type=F32)

    @pl.when(jnp.logical_not(used))
    def _():
        y_ref[...] = jnp.zeros_like(y_ref)


def _expert_mlp(block_e, n_used, xs, w_gu, w_down, layer):
    n_pad, d = xs.shape
    ff2 = w_gu.shape[-1]
    return pl.pallas_call(
        _mlp_kernel,
        grid_spec=pltpu.PrefetchScalarGridSpec(
            num_scalar_prefetch=2,
            grid=(n_pad // MOE_ROWS,),
            in_specs=[
                pl.BlockSpec((MOE_ROWS, d), lambda i, be, nu: (i, 0)),
                pl.BlockSpec((None, None, d, ff2), lambda i, be, nu: (layer, be[i], 0, 0)),
                pl.BlockSpec((None, None, ff2 // 2, d), lambda i, be, nu: (layer, be[i], 0, 0)),
            ],
            out_specs=pl.BlockSpec((MOE_ROWS, d), lambda i, be, nu: (i, 0)),
            scratch_shapes=[pltpu.VMEM((d, ff2), BF16), pltpu.VMEM((ff2 // 2, d), BF16)],
        ),
        out_shape=jax.ShapeDtypeStruct((n_pad, d), F32),
        compiler_params=_cparams(("arbitrary",)),
        name="expert_mlp",
    )(block_e, n_used, xs, w_gu, w_down)


def _combine_kernel(dest_ref, ys_ref, x_ref, ri_ref, g2_ref, fg_ref, o_ref, buf, sem, *, final):
    bi = pl.program_id(0)
    t = pl.program_id(1)
    nt = pl.num_programs(1)
    step = bi * nt + t
    total = pl.num_programs(0) * nt

    def copy(src_row, slot, k, i):
        return pltpu.make_async_copy(ys_ref.at[pl.ds(src_row, 1)], buf.at[slot, k, pl.ds(i, 1)], sem.at[slot])

    def issue(st, slot):
        def one(i, c):
            copy(dest_ref[st, i], slot, 0, i).start()
            copy(dest_ref[st, TM + i], slot, 1, i).start()
            return c
        lax.fori_loop(0, TM, one, 0, unroll=8)

    slot = step % 2

    @pl.when(step == 0)
    def _():
        issue(0, 0)

    @pl.when(step + 1 < total)
    def _():
        issue(step + 1, 1 - slot)

    def drain(i, c):
        copy(0, slot, 0, 0).wait()
        return c

    lax.fori_loop(0, 2 * TM, drain, 0, unroll=8)

    lane = lax.broadcasted_iota(jnp.int32, (TM, LANES), 1)
    ri = ri_ref[...]
    w1 = jnp.sum(jnp.where(lane == 2, ri, 0.0), axis=-1, keepdims=True)
    w2 = jnp.sum(jnp.where(lane == 3, ri, 0.0), axis=-1, keepdims=True)
    y = buf[slot, 0] * w1 + buf[slot, 1] * w2
    xn = x_ref[...] + g2_ref[...] * y
    if final:
        ms = jnp.mean(xn * xn, axis=-1, keepdims=True)
        xn = xn * lax.rsqrt(ms + NORM_EPS) * fg_ref[...]
    o_ref[...] = xn


def _combine(dest, ys, xa, rinfo, g2, final_g, t0, final):
    b, s, d = xa.shape
    nt = s // TM - t0
    mod_idx = lambda bi, t, dr: (jnp.where(t + t0 == 0, b, bi), 0, 0)
    tile = lambda wdt: pl.BlockSpec((None, TM, wdt), lambda bi, t, dr: (bi, t + t0, 0))
    if final:
        out_spec = pl.BlockSpec((None, TM, d), lambda bi, t, dr: (bi, t, 0))
        out_shape = jax.ShapeDtypeStruct((b, nt * TM, d), F32)
    else:
        out_spec = tile(d)
        out_shape = jax.ShapeDtypeStruct((b, s, d), F32)
    return pl.pallas_call(
        functools.partial(_combine_kernel, final=final),
        grid_spec=pltpu.PrefetchScalarGridSpec(
            num_scalar_prefetch=1,
            grid=(b, nt),
            in_specs=[
                pl.BlockSpec(memory_space=pl.ANY),
                tile(d), tile(LANES),
                pl.BlockSpec((None, 1, d), mod_idx),
                pl.BlockSpec((1, d), lambda bi, t, dr: (0, 0)),
            ],
            out_specs=out_spec,
            scratch_shapes=[pltpu.VMEM((2, 2, TM, d), F32), pltpu.SemaphoreType.DMA((2,))],
        ),
        out_shape=out_shape,
        compiler_params=_cparams(("arbitrary", "arbitrary")),
        name="combine",
    )(dest, ys, xa, rinfo, g2, final_g)


def _moe(h2, rinfo, counts, w_gu, w_down, layer, t0):
    b, s, d = h2.shape
    nt = s // TM - t0
    r = rinfo[:, t0 * TM:, :8]
    e = r[..., 0:2].astype(jnp.int32)
    rank = r[..., 4:6].astype(jnp.int32)
    cnt = counts[0, ROUTE_LANE0:ROUTE_LANE0 + MOE_EXPERTS].astype(jnp.int32)
    padded = (cnt + MOE_ROWS - 1) // MOE_ROWS * MOE_ROWS
    ends = jnp.cumsum(padded)
    starts = ends - padded
    eids = jnp.arange(MOE_EXPERTS, dtype=jnp.int32)
    dest = jnp.sum(jnp.where(e[..., None] == eids, starts, 0), axis=-1) + rank
    dest = dest.reshape(b, nt, TM, 2).transpose(0, 1, 3, 2).reshape(b * nt, 2 * TM)
    n_assign = b * nt * TM * 2
    n_blocks = (n_assign + MOE_EXPERTS * (MOE_ROWS - 1)) // MOE_ROWS + 1
    n_pad = n_blocks * MOE_ROWS
    blk_row = jnp.arange(n_blocks, dtype=jnp.int32) * MOE_ROWS
    block_e = jnp.minimum(jnp.sum((blk_row[:, None] >= ends[None, :]).astype(jnp.int32), axis=1),
                          MOE_EXPERTS - 1)
    n_used = (ends[-1] // MOE_ROWS).astype(jnp.int32).reshape(1)
    xs = _dispatch(dest, h2, n_pad, t0)
    ys = _expert_mlp(block_e, n_used, xs, w_gu, w_down, layer)
    return dest, ys


def _router_weights(wg, bg, we, be):
    d = wg.shape[0]
    pad = LANES - MOE_GROUPS - MOE_EXPERTS
    assert ROUTE_LANE0 == MOE_GROUPS
    wr = jnp.concatenate([wg.astype(F32), we.astype(F32), jnp.zeros((d, pad), F32)], axis=1)
    br = jnp.concatenate([bg.astype(F32), be.astype(F32), jnp.zeros((pad,), F32)]).reshape(1, LANES)
    return wr, br


def kernel(x, c, ctx, c_ctx, ada_w, ada_b, norm_g, final_g, ab_w_in, ab_w_out, ret_decay, ret_gn, na_rpb,
           swa_w_in, swa_w_out, swa_sink, router_g_w, router_g_b, router_e_w, router_e_b,
           expert_w_gu, expert_w_down):
    b, seq, d = x.shape
    assert ctx.shape[1] == TM and seq % TM == 0 and d == D_MODEL
    xa = jnp.concatenate([ctx, x], axis=1)
    rope = _rope_tables(seq)

    cvec = jnp.concatenate([c, c_ctx[None, :], jnp.zeros((7, d), F32)], axis=0)
    mod = _adaln(cvec, ada_w, ada_b)
    mod = mod.reshape(DEPTH, b + 8, 6, 1, d)[:, :b + 1].transpose(0, 2, 1, 3, 4)

    for layer in range(DEPTH):
        last = layer == DEPTH - 1
        sh1, sc1, g1, sh2, sc2, g2 = (mod[layer, i] for i in range(6))
        gain1 = norm_g[layer, 0].reshape(1, d)
        gain2 = norm_g[layer, 1].reshape(1, d)
        j = layer // 2
        if layer % 2 == 0:
            nb = RET_W // LANES
            p = _proj(xa, gain1, sh1, sc1, ab_w_in[j].astype(BF16), rope,
                      rope_blocks=range(0, 2 * nb),
                      scale_blocks=list(range(nb, 2 * nb)) + list(range(4 * nb, 5 * nb)))
            log_gamma = jnp.log1p(-jnp.exp2(-ret_decay[j].astype(F32)))
            mixes = [_retention(p, log_gamma, ret_gn[j]), _neighbourhood(p, _na_bias(na_rpb[j]))]
            w_out = ab_w_out[j]
        else:
            qb = SWA_HEADS * HEAD_DIM // LANES
            kb = SWA_KV_HEADS * HEAD_DIM // LANES
            p = _proj(xa, gain1, sh1, sc1, swa_w_in[j].astype(BF16), rope,
                      rope_blocks=range(0, qb + kb), scale_blocks=range(0, qb))
            mixes = [_swa(p, swa_sink[j].astype(F32))]
            w_out = swa_w_out[j]
        t0 = 1 if last else 0
        wr, br = _router_weights(router_g_w[layer], router_g_b[layer], router_e_w[layer], router_e_b[layer])
        xa, h2, rinfo, counts = _outproj(mixes, w_out.astype(BF16), xa, g1, gain2, sh2, sc2, wr, br, t0)
        dest, ys = _moe(h2, rinfo, counts, expert_w_gu, expert_w_down, layer, t0)
        xa = _combine(dest, ys, xa, rinfo, g2, final_g.reshape(1, d), t0, last)
    return xa
```

```python
import functools

import numpy as np
import jax
import jax.numpy as jnp
from jax import lax
from jax.experimental import pallas as pl
from jax.experimental.pallas import tpu as pltpu

F32 = jnp.float32
BF16 = jnp.bfloat16

D_MODEL = 1024
DEPTH = 2
GRID_W = 64
HEAD_DIM = 64
RET_HEADS = 8
NA_HEADS = 8
RET_W = 512
NA_W = 512
AB_IN = 4 * RET_W + 3 * NA_W
RET_CHUNK = 128
GN_EPS = 1e-5
NA_KH = 8
NA_KW = 16
SWA_HEADS = 16
SWA_KV_HEADS = 4
SWA_WINDOW = 128
SWA_BLOCK = 128
SWA_IN = (SWA_HEADS + 2 * SWA_KV_HEADS) * HEAD_DIM
ROPE_BASE = 10000.0
MOE_GROUPS = 4
MOE_EPG = 8
MOE_EXPERTS = 32
MOE_FF = 512
NORM_EPS = 1e-6
NEG_INF = -1e30

LANES = 128
TM = 256
MOE_ROWS = 256
ROUTE_LANE0 = 4
VMEM_LIMIT = 56 * 1024 * 1024


def _cparams(sem, vmem=VMEM_LIMIT):
    return pltpu.CompilerParams(dimension_semantics=sem, vmem_limit_bytes=vmem)


def _split_bf16(a):
    hi = a.astype(BF16)
    lo = (a - hi.astype(F32)).astype(BF16)
    return hi, lo


def _dot3(a, b):
    ah, al = _split_bf16(a)
    bh, bl = _split_bf16(b)
    d = lambda x, y: jnp.dot(x, y, preferred_element_type=F32)
    return d(ah, bh) + (d(ah, bl) + d(al, bh))


def _dot_nt(a, b):
    return lax.dot_general(a, b, (((1,), (1,)), ((), ())), preferred_element_type=F32)


def _dot_tn(a, b):
    return lax.dot_general(a, b, (((0,), (0,)), ((), ())), preferred_element_type=F32)


def _silu(x):
    return x / (1.0 + jnp.exp(-x))


def _adaln_kernel(c_ref, w_ref, b_ref, o_ref):
    o_ref[...] = _dot3(_silu(c_ref[...]), w_ref[...]) + b_ref[...]


def _adaln(cvec, ada_w, ada_b):
    depth, d, n6 = ada_w.shape
    rows = cvec.shape[0]
    tn = 1024
    return pl.pallas_call(
        _adaln_kernel,
        grid=(depth, n6 // tn),
        in_specs=[
            pl.BlockSpec((rows, d), lambda l, j: (0, 0)),
            pl.BlockSpec((None, d, tn), lambda l, j: (l, 0, j)),
            pl.BlockSpec((None, 1, tn), lambda l, j: (l, 0, j)),
        ],
        out_specs=pl.BlockSpec((None, rows, tn), lambda l, j: (l, 0, j)),
        out_shape=jax.ShapeDtypeStruct((depth, rows, n6), F32),
        compiler_params=_cparams(("arbitrary", "arbitrary")),
        name="adaln",
    )(cvec, ada_w, ada_b.reshape(depth, 1, n6))


def _rms_mod(x, g, sh, sc):
    ms = jnp.mean(x * x, axis=-1, keepdims=True)
    return (x * lax.rsqrt(ms + NORM_EPS) * g) * (1.0 + sc) + sh


def _proj_kernel(x_ref, g_ref, sh_ref, sc_ref, w_ref, rope_ref, o_ref, *, rope_blocks, scale_blocks, cn):
    is_lat = pl.program_id(1) > 0
    hb = _rms_mod(x_ref[...], g_ref[...], sh_ref[...], sc_ref[...]).astype(BF16)
    nout = w_ref.shape[1]
    for c in range(nout // cn):
        o = jnp.dot(hb, w_ref[:, c * cn:(c + 1) * cn], preferred_element_type=F32)
        for s in range(cn // LANES):
            blk = c * (cn // LANES) + s
            ob = o[:, s * LANES:(s + 1) * LANES]
            if blk in rope_blocks:
                r = (ob * rope_ref[0] + pltpu.roll(ob, 16, 1) * rope_ref[1]
                     + pltpu.roll(ob, LANES - 16, 1) * rope_ref[2])
                ob = jnp.where(is_lat, r, ob)
            if blk in scale_blocks:
                ob = ob * (HEAD_DIM ** -0.5)
            o_ref[:, blk * LANES:(blk + 1) * LANES] = ob


def _proj(xa, gain, sh, sc, w_bf16, rope, rope_blocks, scale_blocks):
    b, s, d = xa.shape
    nout = w_bf16.shape[1]
    nt = s // TM
    mod_idx = lambda bi, t: (jnp.where(t == 0, b, bi), 0, 0)
    kern = functools.partial(_proj_kernel, rope_blocks=frozenset(rope_blocks),
                             scale_blocks=frozenset(scale_blocks), cn=512)
    return pl.pallas_call(
        kern,
        grid=(b, nt),
        in_specs=[
            pl.BlockSpec((None, TM, d), lambda bi, t: (bi, t, 0)),
            pl.BlockSpec((1, d), lambda bi, t: (0, 0)),
            pl.BlockSpec((None, 1, d), mod_idx),
            pl.BlockSpec((None, 1, d), mod_idx),
            pl.BlockSpec((d, nout), lambda bi, t: (0, 0)),
            pl.BlockSpec((3, TM, LANES), lambda bi, t: (0, jnp.maximum(t - 1, 0), 0)),
        ],
        out_specs=pl.BlockSpec((None, TM, nout), lambda bi, t: (bi, t, 0)),
        out_shape=jax.ShapeDtypeStruct((b, s, nout), F32),
        compiler_params=_cparams(("arbitrary", "arbitrary")),
        name="proj",
    )(xa, gain, sh, sc, w_bf16, rope)


def _rope_tables(seq):
    nf = HEAD_DIM // 4
    inv = ROPE_BASE ** (-jnp.arange(nf, dtype=F32) / nf)
    t = jnp.arange(seq)
    row = (t // GRID_W).astype(F32)
    col = (t % GRID_W).astype(F32)
    lane = np.arange(LANES)
    jj = lane % HEAD_DIM
    axis_is_col = (jj // 32) == 1
    second_half = (jj % 32) >= 16
    f = jj % 16
    pos = jnp.where(axis_is_col[None, :], col[:, None], row[:, None])
    ang = pos * inv[f][None, :]
    c, s = jnp.cos(ang), jnp.sin(ang)
    sa = jnp.where(second_half[None, :], s, 0.0)
    sb = jnp.where(second_half[None, :], 0.0, -s)
    return jnp.stack([c, sa, sb], axis=0)


def _ret_kernel(lg_ref, q_ref, k_ref, v_ref, g_ref, gn_ref, o_ref, accf_ref, accb_ref):
    hp = pl.program_id(1)
    c = RET_CHUNK
    s_len = q_ref.shape[0]
    n_chunks = s_len // c
    ctx_chunks = TM // c
    pos = lax.broadcasted_iota(jnp.int32, (c, LANES), 0).astype(F32)
    lane = lax.broadcasted_iota(jnp.int32, (c, LANES), 1)
    row = lax.broadcasted_iota(jnp.int32, (c, LANES), 0)
    m0 = lane < HEAD_DIM
    same_head = (row < HEAD_DIM) == m0
    rel = (row - lane).astype(F32)
    lgf = [lg_ref[0, hp * 2 + hh] for hh in range(2)]
    lgb = [lg_ref[1, hp * 2 + hh] for hh in range(2)]
    lgf_l = jnp.where(m0, lgf[0], lgf[1])
    lgb_l = jnp.where(m0, lgb[0], lgb[1])
    fwd = dict(
        intra=[jnp.where(rel >= 0, jnp.exp(lgf[hh] * jnp.maximum(rel, 0.0)), 0.0) for hh in range(2)],
        qd=jnp.exp(lgf_l * (pos + 1.0)), kd=jnp.exp(lgf_l * (c - 1.0 - pos)), cd=jnp.exp(lgf_l[0:1] * float(c)))
    bwd = dict(
        intra=[jnp.where(rel <= 0, jnp.exp(lgb[hh] * jnp.maximum(-rel, 0.0)), 0.0) for hh in range(2)],
        qd=jnp.exp(lgb_l * (c - pos)), kd=jnp.exp(lgb_l * pos), cd=jnp.exp(lgb_l[0:1] * float(c)))

    def chunk(r0, state, dec):
        q = q_ref[pl.ds(r0, c), :]
        k = k_ref[pl.ds(r0, c), :]
        vb = v_ref[pl.ds(r0, c), :].astype(BF16)
        kb = k.astype(BF16)
        pv = []
        for hh in range(2):
            qh = jnp.where(m0 if hh == 0 else jnp.logical_not(m0), q, 0.0).astype(BF16)
            s = _dot_nt(qh, kb) * dec["intra"][hh]
            pv.append(jnp.dot(s.astype(BF16), vb, preferred_element_type=F32))
        out = jnp.where(m0, pv[0], pv[1])
        out = out + jnp.dot(q.astype(BF16), state.astype(BF16), preferred_element_type=F32) * dec["qd"]
        kv = _dot_tn((k * dec["kd"]).astype(BF16), vb)
        return out, state * dec["cd"] + jnp.where(same_head, kv, 0.0)

    def body(i, states):
        sf, sb = states
        rf = pl.multiple_of(i * c, c)
        ib = jnp.where(i < ctx_chunks, ctx_chunks - 1 - i, n_chunks + ctx_chunks - 1 - i)
        rb = pl.multiple_of(ib * c, c)
        of, sf = chunk(rf, sf, fwd)
        ob, sb = chunk(rb, sb, bwd)
        accf_ref[pl.ds(rf, c), :] = of
        accb_ref[pl.ds(rb, c), :] = ob
        return sf, sb

    z = jnp.zeros((LANES, LANES), F32)
    lax.fori_loop(0, n_chunks, body, (z, z), unroll=2)

    inv_n = 1.0 / HEAD_DIM

    def readout(i, carry):
        r0 = pl.multiple_of(i * c, c)
        o = accf_ref[pl.ds(r0, c), :] + accb_ref[pl.ds(r0, c), :]
        s0 = jnp.sum(jnp.where(m0, o, 0.0), axis=-1, keepdims=True)
        s1 = jnp.sum(jnp.where(m0, 0.0, o), axis=-1, keepdims=True)
        dlt = o - jnp.where(m0, s0, s1) * inv_n
        d2 = dlt * dlt
        v0 = jnp.sum(jnp.where(m0, d2, 0.0), axis=-1, keepdims=True)
        v1 = jnp.sum(jnp.where(m0, 0.0, d2), axis=-1, keepdims=True)
        y = dlt * lax.rsqrt(jnp.where(m0, v0, v1) * inv_n + GN_EPS) * gn_ref[...]
        o_ref[pl.ds(r0, c), :] = _silu(g_ref[pl.ds(r0, c), :]) * y
        return carry

    lax.fori_loop(0, n_chunks, readout, 0)


def _retention(p, log_gamma, ret_gn):
    b, s, _ = p.shape
    nb = RET_W // LANES
    blk = lambda off: pl.BlockSpec((None, s, LANES), lambda bi, hp: (bi, 0, off + hp))
    return pl.pallas_call(
        _ret_kernel,
        grid=(b, nb),
        in_specs=[
            pl.BlockSpec(memory_space=pltpu.SMEM),
            blk(0), blk(nb), blk(2 * nb), blk(3 * nb),
            pl.BlockSpec((1, LANES), lambda bi, hp: (0, hp)),
        ],
        out_specs=pl.BlockSpec((None, s, LANES), lambda bi, hp: (bi, 0, hp)),
        out_shape=jax.ShapeDtypeStruct((b, s, RET_W), F32),
        scratch_shapes=[pltpu.VMEM((s, LANES), F32), pltpu.VMEM((s, LANES), F32)],
        compiler_params=_cparams(("arbitrary", "arbitrary")),
        name="retention",
    )(log_gamma, p, p, p, p, ret_gn.reshape(1, RET_W))


def _softmax_pv(s_list, v_list, extra=None):
    m = s_list[0].max(axis=-1, keepdims=True)
    for s in s_list[1:]:
        m = jnp.maximum(m, s.max(axis=-1, keepdims=True))
    if extra is not None:
        m = jnp.maximum(m, extra)
    den = None
    o = None
    for s, v in zip(s_list, v_list):
        p = jnp.exp(s - m)
        ds = p.sum(axis=-1, keepdims=True)
        den = ds if den is None else den + ds
        pv = jnp.dot(p.astype(BF16), v, preferred_element_type=F32)
        o = pv if o is None else o + pv
    if extra is not None:
        den = den + jnp.exp(extra - m)
    return o / den


def _stage_heads(q_ref, k_ref, v_ref, qm_ref, kb_ref, vb_ref):
    m0 = lax.broadcasted_iota(jnp.int32, (TM, LANES), 1) < HEAD_DIM

    def stage(i, carry):
        r0 = pl.multiple_of(i * TM, TM)
        q = q_ref[pl.ds(r0, TM), :]
        qm_ref[0, pl.ds(r0, TM), :] = jnp.where(m0, q, 0.0).astype(BF16)
        qm_ref[1, pl.ds(r0, TM), :] = jnp.where(m0, 0.0, q).astype(BF16)
        kb_ref[pl.ds(r0, TM), :] = k_ref[pl.ds(r0, TM), :].astype(BF16)
        vb_ref[pl.ds(r0, TM), :] = v_ref[pl.ds(r0, TM), :].astype(BF16)
        return carry

    lax.fori_loop(0, q_ref.shape[0] // TM, stage, 0)


def _na_kernel(q_ref, k_ref, v_ref, bias_ref, o_ref, qm_ref, kb_ref, vb_ref):
    s_len = q_ref.shape[0]
    rows = (s_len - TM) // GRID_W
    nloc = NA_KH * GRID_W
    _stage_heads(q_ref, k_ref, v_ref, qm_ref, kb_ref, vb_ref)

    kc = kb_ref[0:TM, :]
    vc = vb_ref[0:TM, :]
    outs = [_softmax_pv([_dot_nt(qm_ref[hh, 0:TM, :], kc)], [vc]) for hh in range(2)]
    m0c = lax.broadcasted_iota(jnp.int32, (TM, LANES), 1) < HEAD_DIM
    o_ref[0:TM, :] = jnp.where(m0c, outs[0], outs[1])

    m0 = lax.broadcasted_iota(jnp.int32, (GRID_W, LANES), 1) < HEAD_DIM

    def row_block(r, carry):
        rs = jnp.clip(r - NA_KH // 2, 0, rows - NA_KH)
        pat = r - rs
        q0 = pl.multiple_of(TM + r * GRID_W, GRID_W)
        k0 = pl.multiple_of(TM + rs * GRID_W, GRID_W)
        kl = kb_ref[pl.ds(k0, nloc), :]
        vl = vb_ref[pl.ds(k0, nloc), :]
        kc = kb_ref[0:TM, :]
        vc = vb_ref[0:TM, :]
        q = jnp.concatenate([qm_ref[0, pl.ds(q0, GRID_W), :], qm_ref[1, pl.ds(q0, GRID_W), :]], axis=0)
        s_loc = _dot_nt(q, kl) + bias_ref[pat].reshape(2 * GRID_W, nloc)
        s_ctx = _dot_nt(q, kc)
        res = _softmax_pv([s_loc, s_ctx], [vl, vc])
        o_ref[pl.ds(q0, GRID_W), :] = jnp.where(m0, res[:GRID_W], res[GRID_W:])
        return carry

    lax.fori_loop(0, rows, row_block, 0, unroll=4)


def _na_bias(rpb):
    h = rpb.shape[0]
    qc = np.arange(GRID_W)[:, None]
    kc = np.arange(GRID_W)[None, :]
    win = np.clip(qc - NA_KW // 2, 0, GRID_W - NA_KW)
    valid = (kc >= win) & (kc < win + NA_KW)
    col_off = np.clip(kc - qc + NA_KW - 1, 0, 2 * NA_KW - 2)
    pat = np.arange(NA_KH)[:, None]
    row_off = np.arange(NA_KH)[None, :] - pat + NA_KH - 1
    bias = rpb[:, row_off][:, :, :, col_off]
    bias = jnp.where(valid[None, None, None], bias.astype(F32), NEG_INF)
    bias = bias.transpose(1, 0, 3, 2, 4)
    return bias.reshape(NA_KH, h, GRID_W, NA_KH * GRID_W)


def _neighbourhood(p, bias):
    b, s, _ = p.shape
    nb = NA_W // LANES
    c0 = 4 * RET_W // LANES
    blk = lambda off: pl.BlockSpec((None, s, LANES), lambda bi, hp: (bi, 0, c0 + off + hp))
    return pl.pallas_call(
        _na_kernel,
        grid=(b, nb),
        in_specs=[
            blk(0), blk(nb), blk(2 * nb),
            pl.BlockSpec((NA_KH, 2, GRID_W, NA_KH * GRID_W), lambda bi, hp: (0, hp, 0, 0)),
        ],
        out_specs=pl.BlockSpec((None, s, LANES), lambda bi, hp: (bi, 0, hp)),
        out_shape=jax.ShapeDtypeStruct((b, s, NA_W), F32),
        scratch_shapes=[pltpu.VMEM((2, s, LANES), BF16), pltpu.VMEM((s, LANES), BF16),
                        pltpu.VMEM((s, LANES), BF16)],
        compiler_params=_cparams(("arbitrary", "arbitrary")),
        name="neighbourhood",
    )(p, p, p, bias)


def _swa_kernel(sink_ref, q_ref, k_ref, v_ref, o_ref, kd_ref, vd_ref):
    kp = pl.program_id(1)
    s_len = q_ref.shape[0]
    seq = s_len - TM
    nblk = seq // SWA_BLOCK
    band = SWA_BLOCK + 2 * SWA_WINDOW
    group = SWA_HEADS // SWA_KV_HEADS
    heads_per_step = 2 * group
    m0t = lax.broadcasted_iota(jnp.int32, (TM, LANES), 1) < HEAD_DIM

    def stage(i, carry):
        r0 = pl.multiple_of(i * TM, TM)
        for src, dst in ((k_ref, kd_ref), (v_ref, vd_ref)):
            x = src[pl.ds(r0, TM), :]
            xr = pltpu.roll(x, HEAD_DIM, 1)
            dst[0, pl.ds(r0, TM), :] = jnp.where(m0t, x, xr).astype(BF16)
            dst[1, pl.ds(r0, TM), :] = jnp.where(m0t, xr, x).astype(BF16)
        return carry

    lax.fori_loop(0, s_len // TM, stage, 0)
    o_ref[0:TM, :] = jnp.zeros((TM, o_ref.shape[1]), F32)

    qi = lax.broadcasted_iota(jnp.int32, (2 * SWA_BLOCK, band), 0) % SWA_BLOCK
    ki = lax.broadcasted_iota(jnp.int32, (2 * SWA_BLOCK, band), 1)
    m0 = lax.broadcasted_iota(jnp.int32, (SWA_BLOCK, LANES), 1) < HEAD_DIM
    first_head = lax.broadcasted_iota(jnp.int32, (2 * SWA_BLOCK, 1), 0) < SWA_BLOCK

    def block(i, carry):
        start = jnp.clip((i - 1) * SWA_BLOCK, 0, seq - band)
        k0 = pl.multiple_of(TM + start, SWA_BLOCK)
        q0 = pl.multiple_of(TM + i * SWA_BLOCK, SWA_BLOCK)
        valid = jnp.abs(qi + (i * SWA_BLOCK - start) - ki) <= SWA_WINDOW
        for hh in range(2):
            kb = kd_ref[hh, pl.ds(k0, band), :]
            vb = vd_ref[hh, pl.ds(k0, band), :]
            kc = kd_ref[hh, 0:TM, :]
            vc = vd_ref[hh, 0:TM, :]
            for j in range(group // 2):
                pair = hh * (group // 2) + j
                qp = q_ref[pl.ds(q0, SWA_BLOCK), pair * LANES:(pair + 1) * LANES]
                q = jnp.concatenate([jnp.where(m0, qp, 0.0), jnp.where(m0, 0.0, qp)], axis=0).astype(BF16)
                s = jnp.where(valid, _dot_nt(q, kb), NEG_INF)
                s_ctx = _dot_nt(q, kc)
                h0 = kp * heads_per_step + 2 * pair
                sink = jnp.where(first_head, sink_ref[h0], sink_ref[h0 + 1])
                res = _softmax_pv([s, s_ctx], [vb, vc], extra=sink)
                o_ref[pl.ds(q0, SWA_BLOCK), pair * LANES:(pair + 1) * LANES] = jnp.where(
                    m0, res[:SWA_BLOCK], res[SWA_BLOCK:])
        return carry

    lax.fori_loop(0, nblk, block, 0, unroll=2)


def _swa(p, sink):
    b, s, _ = p.shape
    qw = SWA_HEADS * HEAD_DIM // 2
    kblk = SWA_HEADS * HEAD_DIM // LANES
    vblk = kblk + SWA_KV_HEADS * HEAD_DIM // LANES
    return pl.pallas_call(
        _swa_kernel,
        grid=(b, 2),
        in_specs=[
            pl.BlockSpec(memory_space=pltpu.SMEM),
            pl.BlockSpec((None, s, qw), lambda bi, kp: (bi, 0, kp)),
            pl.BlockSpec((None, s, LANES), lambda bi, kp: (bi, 0, kblk + kp)),
            pl.BlockSpec((None, s, LANES), lambda bi, kp: (bi, 0, vblk + kp)),
        ],
        out_specs=pl.BlockSpec((None, s, qw), lambda bi, kp: (bi, 0, kp)),
        out_shape=jax.ShapeDtypeStruct((b, s, SWA_HEADS * HEAD_DIM), F32),
        scratch_shapes=[pltpu.VMEM((2, s, LANES), BF16)] * 2,
        compiler_params=_cparams(("arbitrary", "arbitrary")),
        name="swa",
    )(sink, p, p, p)


def _outproj_kernel(*refs, n_mix):
    mix_refs = refs[:n_mix]
    (w_ref, x_ref, g1_ref, gn_ref, sh_ref, sc_ref, wr_ref, br_ref,
     xo_ref, h_ref, ri_ref, cnt_ref, carry_ref) = refs[n_mix:]
    first = (pl.program_id(0) == 0) & (pl.program_id(1) == 0)

    @pl.when(first)
    def _():
        carry_ref[...] = jnp.zeros_like(carry_ref)

    o = None
    off = 0
    for m_ref in mix_refs:
        kw = m_ref.shape[1]
        part = jnp.dot(m_ref[...].astype(BF16), w_ref[off:off + kw, :], preferred_element_type=F32)
        o = part if o is None else o + part
        off += kw
    xn = x_ref[...] + g1_ref[...] * o
    xo_ref[...] = xn
    h = _rms_mod(xn, gn_ref[...], sh_ref[...], sc_ref[...])
    h_ref[...] = h

    logits = _dot3(h, wr_ref[...]) + br_ref[...]
    tm = logits.shape[0]
    lane = lax.broadcasted_iota(jnp.int32, (tm, LANES), 1).astype(F32)
    big = 1e9
    gmask = lane < MOE_GROUPS
    mg = jnp.max(jnp.where(gmask, logits, -big), axis=-1, keepdims=True)
    sg = jnp.sum(jnp.where(gmask, jnp.exp(jnp.minimum(logits - mg, 0.0)), 0.0), axis=-1, keepdims=True)
    gw = 1.0 / sg
    gi = jnp.min(jnp.where(gmask & (logits == mg), lane, big), axis=-1, keepdims=True)
    lo = ROUTE_LANE0 + MOE_EPG * gi
    emask = (lane >= lo) & (lane < lo + MOE_EPG)
    l1 = jnp.max(jnp.where(emask, logits, -big), axis=-1, keepdims=True)
    i1 = jnp.min(jnp.where(emask & (logits == l1), lane, big), axis=-1, keepdims=True)
    emask2 = emask & (lane != i1)
    l2 = jnp.max(jnp.where(emask2, logits, -big), axis=-1, keepdims=True)
    i2 = jnp.min(jnp.where(emask2 & (logits == l2), lane, big), axis=-1, keepdims=True)
    e21 = jnp.exp(l2 - l1)
    w1 = gw / (1.0 + e21)
    w2 = gw * e21 / (1.0 + e21)

    oh = jnp.where((lane == i1) | (lane == i2), 1.0, 0.0)
    tri = (lax.broadcasted_iota(jnp.int32, (tm, tm), 0) > lax.broadcasted_iota(jnp.int32, (tm, tm), 1))
    cum = jnp.dot(jnp.where(tri, 1.0, 0.0).astype(BF16), oh.astype(BF16), preferred_element_type=F32) + carry_ref[...]
    r1 = jnp.sum(jnp.where(lane == i1, cum, 0.0), axis=-1, keepdims=True)
    r2 = jnp.sum(jnp.where(lane == i2, cum, 0.0), axis=-1, keepdims=True)
    carry_ref[...] = carry_ref[...] + jnp.sum(oh, axis=0, keepdims=True)
    cnt_ref[...] = carry_ref[...]

    ri = jnp.where(lane == 0, i1 - ROUTE_LANE0, 0.0)
    ri = jnp.where(lane == 1, i2 - ROUTE_LANE0, ri)
    ri = jnp.where(lane == 2, w1, ri)
    ri = jnp.where(lane == 3, w2, ri)
    ri = jnp.where(lane == 4, r1, ri)
    ri = jnp.where(lane == 5, r2, ri)
    ri_ref[...] = ri


def _outproj(mixes, w_bf16, xa, g1, gain2, sh2, sc2, wr, br, t0):
    b, s, d = xa.shape
    nt = s // TM - t0
    so = nt * TM
    mod_idx = lambda bi, t: (jnp.where(t + t0 == 0, b, bi), 0, 0)
    tile = lambda wdt: pl.BlockSpec((None, TM, wdt), lambda bi, t: (bi, t + t0, 0))
    otile = lambda wdt: pl.BlockSpec((None, TM, wdt), lambda bi, t: (bi, t, 0))
    const = lambda shape: pl.BlockSpec(shape, lambda bi, t: (0,) * len(shape))
    in_specs = [tile(m.shape[2]) for m in mixes] + [
        const(w_bf16.shape), tile(d),
        pl.BlockSpec((None, 1, d), mod_idx), const((1, d)),
        pl.BlockSpec((None, 1, d), mod_idx), pl.BlockSpec((None, 1, d), mod_idx),
        const((d, LANES)), const((1, LANES)),
    ]
    return pl.pallas_call(
        functools.partial(_outproj_kernel, n_mix=len(mixes)),
        grid=(b, nt),
        in_specs=in_specs,
        out_specs=[otile(d), otile(d), otile(LANES), const((1, LANES))],
        out_shape=[jax.ShapeDtypeStruct((b, so, d), F32), jax.ShapeDtypeStruct((b, so, d), F32),
                   jax.ShapeDtypeStruct((b, so, LANES), F32), jax.ShapeDtypeStruct((1, LANES), F32)],
        scratch_shapes=[pltpu.VMEM((1, LANES), F32)],
        compiler_params=_cparams(("arbitrary", "arbitrary")),
        name="outproj",
    )(*mixes, w_bf16, xa, g1, gain2, sh2, sc2, wr, br)


def _dispatch_kernel(dest_ref, h_ref, xs_in_ref, xs_ref, hbuf, sem):
    del xs_in_ref
    nt = pl.num_programs(1)
    step = pl.program_id(0) * nt + pl.program_id(1)
    last = pl.num_programs(0) * nt - 1
    slot = step % 2
    hbuf[slot] = h_ref[...]

    def copy(sl, i, dst_row):
        return pltpu.make_async_copy(hbuf.at[sl, pl.ds(i, 1)], xs_ref.at[pl.ds(dst_row, 1)], sem.at[sl])

    def issue(i, c):
        copy(slot, i, dest_ref[step, i]).start()
        copy(slot, i, dest_ref[step, TM + i]).start()
        return c

    lax.fori_loop(0, TM, issue, 0, unroll=8)

    def drain(sl):
        def one(i, c):
            copy(sl, 0, 0).wait()
            return c
        lax.fori_loop(0, 2 * TM, one, 0, unroll=8)

    @pl.when(step > 0)
    def _():
        drain(1 - slot)

    @pl.when(step == last)
    def _():
        drain(slot)


def _dispatch(dest, h2, n_pad):
    b, s, d = h2.shape
    nt = s // TM
    xs0 = jnp.zeros((n_pad, d), F32)
    return pl.pallas_call(
        _dispatch_kernel,
        grid_spec=pltpu.PrefetchScalarGridSpec(
            num_scalar_prefetch=1,
            grid=(b, nt),
            in_specs=[pl.BlockSpec((None, TM, d), lambda bi, t, dr: (bi, t, 0)),
                      pl.BlockSpec(memory_space=pl.ANY)],
            out_specs=pl.BlockSpec(memory_space=pl.ANY),
            scratch_shapes=[pltpu.VMEM((2, TM, d), F32), pltpu.SemaphoreType.DMA((2,))],
        ),
        out_shape=jax.ShapeDtypeStruct((n_pad, d), F32),
        input_output_aliases={2: 0},
        compiler_params=_cparams(("arbitrary", "arbitrary")),
        name="dispatch",
    )(dest, h2, xs0)


def _mlp_kernel(be_ref, nu_ref, x_ref, wgu_ref, wd_ref, y_ref, wgu_b, wd_b):
    i = pl.program_id(0)
    prev = be_ref[jnp.maximum(i - 1, 0)]
    used = i < nu_ref[0]

    @pl.when(used & ((i == 0) | (be_ref[i] != prev)))
    def _():
        wgu_b[...] = wgu_ref[...].astype(BF16)
        wd_b[...] = wd_ref[...].astype(BF16)

    @pl.when(used)
    def _():
        gu = jnp.dot(x_ref[...].astype(BF16), wgu_b[...], preferred_element_type=F32)
        act = _silu(gu[:, :MOE_FF]) * gu[:, MOE_FF:]
        y_ref[...] = jnp.dot(act.astype(BF16), wd_b[...], preferred_element_type=F32)

    @pl.when(jnp.logical_not(used))
    def _():
        y_ref[...] = jnp.zeros_like(y_ref)


def _expert_mlp(block_e, n_used, xs, w_gu, w_down, layer):
    n_pad, d = xs.shape
    ff2 = w_gu.shape[-1]
    return pl.pallas_call(
        _mlp_kernel,
        grid_spec=pltpu.PrefetchScalarGridSpec(
            num_scalar_prefetch=2,
            grid=(n_pad // MOE_ROWS,),
            in_specs=[
                pl.BlockSpec((MOE_ROWS, d), lambda i, be, nu: (i, 0)),
                pl.BlockSpec((None, None, d, ff2), lambda i, be, nu: (layer, be[i], 0, 0)),
                pl.BlockSpec((None, None, ff2 // 2, d), lambda i, be, nu: (layer, be[i], 0, 0)),
            ],
            out_specs=pl.BlockSpec((MOE_ROWS, d), lambda i, be, nu: (i, 0)),
            scratch_shapes=[pltpu.VMEM((d, ff2), BF16), pltpu.VMEM((ff2 // 2, d), BF16)],
        ),
        out_shape=jax.ShapeDtypeStruct((n_pad, d), F32),
        compiler_params=_cparams(("arbitrary",)),
        name="expert_mlp",
    )(block_e, n_used, xs, w_gu, w_down)


def _combine_kernel(dest_ref, ys_ref, x_ref, ri_ref, g2_ref, fg_ref, o_ref, buf, sem, *, final):
    bi = pl.program_id(0)
    t = pl.program_id(1)
    nt = pl.num_programs(1)
    step = bi * nt + t
    total = pl.num_programs(0) * nt

    def copy(src_row, slot, k, i):
        return pltpu.make_async_copy(ys_ref.at[pl.ds(src_row, 1)], buf.at[slot, k, pl.ds(i, 1)], sem.at[slot])

    def issue(st, slot):
        def one(i, c):
            copy(dest_ref[st, i], slot, 0, i).start()
            copy(dest_ref[st, TM + i], slot, 1, i).start()
            return c
        lax.fori_loop(0, TM, one, 0, unroll=8)

    slot = step % 2

    @pl.when(step == 0)
    def _():
        issue(0, 0)

    @pl.when(step + 1 < total)
    def _():
        issue(step + 1, 1 - slot)

    def drain(i, c):
        copy(0, slot, 0, 0).wait()
        return c

    lax.fori_loop(0, 2 * TM, drain, 0, unroll=8)

    lane = lax.broadcasted_iota(jnp.int32, (TM, LANES), 1)
    ri = ri_ref[...]
    w1 = jnp.sum(jnp.where(lane == 2, ri, 0.0), axis=-1, keepdims=True)
    w2 = jnp.sum(jnp.where(lane == 3, ri, 0.0), axis=-1, keepdims=True)
    y = buf[slot, 0] * w1 + buf[slot, 1] * w2
    xn = x_ref[...] + g2_ref[...] * y
    if final:
        ms = jnp.mean(xn * xn, axis=-1, keepdims=True)
        xn = xn * lax.rsqrt(ms + NORM_EPS) * fg_ref[...]
    o_ref[...] = xn


def _combine(dest, ys, xa, rinfo, g2, final_g, has_ctx, final):
    b, s, d = xa.shape
    nt = s // TM
    mod_idx = lambda bi, t, dr: (jnp.where(t == 0, b, bi) if has_ctx else bi, 0, 0)
    tile = lambda wdt: pl.BlockSpec((None, TM, wdt), lambda bi, t, dr: (bi, t, 0))
    out_spec = tile(d)
    out_shape = jax.ShapeDtypeStruct((b, s, d), F32)
    return pl.pallas_call(
        functools.partial(_combine_kernel, final=final),
        grid_spec=pltpu.PrefetchScalarGridSpec(
            num_scalar_prefetch=1,
            grid=(b, nt),
            in_specs=[
                pl.BlockSpec(memory_space=pl.ANY),
                tile(d), tile(LANES),
                pl.BlockSpec((None, 1, d), mod_idx),
                pl.BlockSpec((1, d), lambda bi, t, dr: (0, 0)),
            ],
            out_specs=out_spec,
            scratch_shapes=[pltpu.VMEM((2, 2, TM, d), F32), pltpu.SemaphoreType.DMA((2,))],
        ),
        out_shape=out_shape,
        compiler_params=_cparams(("arbitrary", "arbitrary")),
        name="combine",
    )(dest, ys, xa, rinfo, g2, final_g)


def _moe(h2, rinfo, counts, w_gu, w_down, layer):
    b, s, d = h2.shape
    nt = s // TM
    r = rinfo[:, :, :8]
    e = r[..., 0:2].astype(jnp.int32)
    rank = r[..., 4:6].astype(jnp.int32)
    cnt = counts[0, ROUTE_LANE0:ROUTE_LANE0 + MOE_EXPERTS].astype(jnp.int32)
    padded = (cnt + MOE_ROWS - 1) // MOE_ROWS * MOE_ROWS
    ends = jnp.cumsum(padded)
    starts = ends - padded
    eids = jnp.arange(MOE_EXPERTS, dtype=jnp.int32)
    dest = jnp.sum(jnp.where(e[..., None] == eids, starts, 0), axis=-1) + rank
    dest = dest.reshape(b, nt, TM, 2).transpose(0, 1, 3, 2).reshape(b * nt, 2 * TM)
    n_assign = b * nt * TM * 2
    n_blocks = (n_assign + MOE_EXPERTS * (MOE_ROWS - 1)) // MOE_ROWS + 1
    n_pad = n_blocks * MOE_ROWS
    blk_row = jnp.arange(n_blocks, dtype=jnp.int32) * MOE_ROWS
    block_e = jnp.minimum(jnp.sum((blk_row[:, None] >= ends[None, :]).astype(jnp.int32), axis=1),
                          MOE_EXPERTS - 1)
    n_used = (ends[-1] // MOE_ROWS).astype(jnp.int32).reshape(1)
    xs = _dispatch(dest, h2, n_pad)
    ys = _expert_mlp(block_e, n_used, xs, w_gu, w_down, layer)
    return dest, ys


def _router_weights(wg, bg, we, be):
    d = wg.shape[0]
    pad = LANES - MOE_GROUPS - MOE_EXPERTS
    assert ROUTE_LANE0 == MOE_GROUPS
    wr = jnp.concatenate([wg.astype(F32), we.astype(F32), jnp.zeros((d, pad), F32)], axis=1)
    br = jnp.concatenate([bg.astype(F32), be.astype(F32), jnp.zeros((pad,), F32)]).reshape(1, LANES)
    return wr, br


def kernel(x, c, ctx, c_ctx, ada_w, ada_b, norm_g, final_g, ab_w_in, ab_w_out, ret_decay, ret_gn, na_rpb,
           swa_w_in, swa_w_out, swa_sink, router_g_w, router_g_b, router_e_w, router_e_b,
           expert_w_gu, expert_w_down):
    b, seq, d = x.shape
    assert ctx.shape[1] == TM and seq % TM == 0 and d == D_MODEL
    xa = jnp.concatenate([ctx, x], axis=1)
    rope = _rope_tables(seq)

    cvec = jnp.concatenate([c, c_ctx[None, :], jnp.zeros((7, d), F32)], axis=0)
    mod = _adaln(cvec, ada_w, ada_b)
    mod = mod.reshape(DEPTH, b + 8, 6, 1, d)[:, :b + 1].transpose(0, 2, 1, 3, 4)

    for layer in range(DEPTH):
        last = layer == DEPTH - 1
        sh1, sc1, g1, sh2, sc2, g2 = (mod[layer, i] for i in range(6))
        gain1 = norm_g[layer, 0].reshape(1, d)
        gain2 = norm_g[layer, 1].reshape(1, d)
        j = layer // 2
        if layer % 2 == 0:
            nb = RET_W // LANES
            p = _proj(xa, gain1, sh1, sc1, ab_w_in[j].astype(BF16), rope,
                      rope_blocks=range(0, 2 * nb),
                      scale_blocks=list(range(nb, 2 * nb)) + list(range(4 * nb, 5 * nb)))
            log_gamma = jnp.log1p(-jnp.exp2(-ret_decay[j].astype(F32)))
            mixes = [_retention(p, log_gamma, ret_gn[j]), _neighbourhood(p, _na_bias(na_rpb[j]))]
            w_out = ab_w_out[j]
        else:
            qb = SWA_HEADS * HEAD_DIM // LANES
            kb = SWA_KV_HEADS * HEAD_DIM // LANES
            p = _proj(xa, gain1, sh1, sc1, swa_w_in[j].astype(BF16), rope,
                      rope_blocks=range(0, qb + kb), scale_blocks=range(0, qb))
            mixes = [_swa(p, swa_sink[j].astype(F32))]
            w_out = swa_w_out[j]
        t0 = 1 if last else 0
        wr, br = _router_weights(router_g_w[layer], router_g_b[layer], router_e_w[layer], router_e_b[layer])
        xa, h2, rinfo, counts = _outproj(mixes, w_out.astype(BF16), xa, g1, gain2, sh2, sc2, wr, br, t0)
        dest, ys = _moe(h2, rinfo, counts, expert_w_gu, expert_w_down, layer)
        xa = _combine(dest, ys, xa, rinfo, g2, final_g.reshape(1, d), not last, last)
    return xa
```

```python
import functools

import numpy as np
import jax
import jax.numpy as jnp
from jax import lax
from jax.experimental import pallas as pl
from jax.experimental.pallas import tpu as pltpu

F32 = jnp.float32
BF16 = jnp.bfloat16

D_MODEL = 1024
DEPTH = 2
GRID_W = 64
HEAD_DIM = 64
RET_HEADS = 8
NA_HEADS = 8
RET_W = 512
NA_W = 512
AB_IN = 4 * RET_W + 3 * NA_W
RET_CHUNK = 128
GN_EPS = 1e-5
NA_KH = 8
NA_KW = 16
SWA_HEADS = 16
SWA_KV_HEADS = 4
SWA_WINDOW = 128
SWA_BLOCK = 128
SWA_IN = (SWA_HEADS + 2 * SWA_KV_HEADS) * HEAD_DIM
ROPE_BASE = 10000.0
MOE_GROUPS = 4
MOE_EPG = 8
MOE_EXPERTS = 32
MOE_FF = 512
NORM_EPS = 1e-6
NEG_INF = -1e30

LANES = 128
TM = 256
MOE_ROWS = 256
ROUTE_LANE0 = 4
VMEM_LIMIT = 56 * 1024 * 1024


def _cparams(sem, vmem=VMEM_LIMIT):
    return pltpu.CompilerParams(dimension_semantics=sem, vmem_limit_bytes=vmem)


def _split_bf16(a):
    hi = a.astype(BF16)
    lo = (a - hi.astype(F32)).astype(BF16)
    return hi, lo


def _dot3(a, b):
    ah, al = _split_bf16(a)
    bh, bl = _split_bf16(b)
    d = lambda x, y: jnp.dot(x, y, preferred_element_type=F32)
    return d(ah, bh) + (d(ah, bl) + d(al, bh))


def _dot_nt(a, b):
    return lax.dot_general(a, b, (((1,), (1,)), ((), ())), preferred_element_type=F32)


def _dot_tn(a, b):
    return lax.dot_general(a, b, (((0,), (0,)), ((), ())), preferred_element_type=F32)


def _silu(x):
    return x / (1.0 + jnp.exp(-x))


def _adaln_kernel(c_ref, w_ref, b_ref, o_ref):
    o_ref[...] = _dot3(_silu(c_ref[...]), w_ref[...]) + b_ref[...]


def _adaln(cvec, ada_w, ada_b):
    depth, d, n6 = ada_w.shape
    rows = cvec.shape[0]
    tn = 1024
    return pl.pallas_call(
        _adaln_kernel,
        grid=(depth, n6 // tn),
        in_specs=[
            pl.BlockSpec((rows, d), lambda l, j: (0, 0)),
            pl.BlockSpec((None, d, tn), lambda l, j: (l, 0, j)),
            pl.BlockSpec((None, 1, tn), lambda l, j: (l, 0, j)),
        ],
        out_specs=pl.BlockSpec((None, rows, tn), lambda l, j: (l, 0, j)),
        out_shape=jax.ShapeDtypeStruct((depth, rows, n6), F32),
        compiler_params=_cparams(("arbitrary", "arbitrary")),
        name="adaln",
    )(cvec, ada_w, ada_b.reshape(depth, 1, n6))


def _rms_mod(x, g, sh, sc):
    ms = jnp.mean(x * x, axis=-1, keepdims=True)
    return (x * lax.rsqrt(ms + NORM_EPS) * g) * (1.0 + sc) + sh


def _proj_kernel(x_ref, g_ref, sh_ref, sc_ref, w_ref, rope_ref, o_ref, *, rope_blocks, scale_blocks, cn):
    is_lat = pl.program_id(1) > 0
    hb = _rms_mod(x_ref[...], g_ref[...], sh_ref[...], sc_ref[...]).astype(BF16)
    nout = w_ref.shape[1]
    for c in range(nout // cn):
        o = jnp.dot(hb, w_ref[:, c * cn:(c + 1) * cn], preferred_element_type=F32)
        for s in range(cn // LANES):
            blk = c * (cn // LANES) + s
            ob = o[:, s * LANES:(s + 1) * LANES]
            if blk in rope_blocks:
                r = (ob * rope_ref[0] + pltpu.roll(ob, 16, 1) * rope_ref[1]
                     + pltpu.roll(ob, LANES - 16, 1) * rope_ref[2])
                ob = jnp.where(is_lat, r, ob)
            if blk in scale_blocks:
                ob = ob * (HEAD_DIM ** -0.5)
            o_ref[:, blk * LANES:(blk + 1) * LANES] = ob


def _proj(xa, gain, sh, sc, w_bf16, rope, rope_blocks, scale_blocks):
    b, s, d = xa.shape
    nout = w_bf16.shape[1]
    nt = s // TM
    mod_idx = lambda bi, t: (jnp.where(t == 0, b, bi), 0, 0)
    kern = functools.partial(_proj_kernel, rope_blocks=frozenset(rope_blocks),
                             scale_blocks=frozenset(scale_blocks), cn=512)
    return pl.pallas_call(
        kern,
        grid=(b, nt),
        in_specs=[
            pl.BlockSpec((None, TM, d), lambda bi, t: (bi, t, 0)),
            pl.BlockSpec((1, d), lambda bi, t: (0, 0)),
            pl.BlockSpec((None, 1, d), mod_idx),
            pl.BlockSpec((None, 1, d), mod_idx),
            pl.BlockSpec((d, nout), lambda bi, t: (0, 0)),
            pl.BlockSpec((3, TM, LANES), lambda bi, t: (0, jnp.maximum(t - 1, 0), 0)),
        ],
        out_specs=pl.BlockSpec((None, TM, nout), lambda bi, t: (bi, t, 0)),
        out_shape=jax.ShapeDtypeStruct((b, s, nout), F32),
        compiler_params=_cparams(("arbitrary", "arbitrary")),
        name="proj",
    )(xa, gain, sh, sc, w_bf16, rope)


def _rope_tables(seq):
    nf = HEAD_DIM // 4
    inv = ROPE_BASE ** (-jnp.arange(nf, dtype=F32) / nf)
    t = jnp.arange(seq)
    row = (t // GRID_W).astype(F32)
    col = (t % GRID_W).astype(F32)
    lane = np.arange(LANES)
    jj = lane % HEAD_DIM
    axis_is_col = (jj // 32) == 1
    second_half = (jj % 32) >= 16
    f = jj % 16
    pos = jnp.where(axis_is_col[None, :], col[:, None], row[:, None])
    ang = pos * inv[f][None, :]
    c, s = jnp.cos(ang), jnp.sin(ang)
    sa = jnp.where(second_half[None, :], s, 0.0)
    sb = jnp.where(second_half[None, :], 0.0, -s)
    return jnp.stack([c, sa, sb], axis=0)


def _ret_kernel(lg_ref, q_ref, k_ref, v_ref, g_ref, gn_ref, o_ref, accf_ref, accb_ref,
                intra_ref, qd_ref, kd_ref):
    hp = pl.program_id(1)
    c = RET_CHUNK
    s_len = q_ref.shape[0]
    n_chunks = s_len // c
    ctx_chunks = TM // c
    pos = lax.broadcasted_iota(jnp.int32, (c, LANES), 0).astype(F32)
    lane = lax.broadcasted_iota(jnp.int32, (c, LANES), 1)
    row = lax.broadcasted_iota(jnp.int32, (c, LANES), 0)
    m0 = lane < HEAD_DIM
    same_head = (row < HEAD_DIM) == m0
    rel = (row - lane).astype(F32)
    lgf = [lg_ref[0, hp * 2 + hh] for hh in range(2)]
    lgb = [lg_ref[1, hp * 2 + hh] for hh in range(2)]
    lgf_l = jnp.where(m0, lgf[0], lgf[1])
    lgb_l = jnp.where(m0, lgb[0], lgb[1])
    for hh in range(2):
        intra_ref[0, :, hh * c:(hh + 1) * c] = jnp.where(rel >= 0, jnp.exp(lgf[hh] * jnp.maximum(rel, 0.0)), 0.0)
        intra_ref[1, :, hh * c:(hh + 1) * c] = jnp.where(rel <= 0, jnp.exp(lgb[hh] * jnp.maximum(-rel, 0.0)), 0.0)
    qd_ref[0] = jnp.exp(lgf_l * (pos + 1.0))
    qd_ref[1] = jnp.exp(lgb_l * (c - pos))
    kd_ref[0] = jnp.exp(lgf_l * (c - 1.0 - pos))
    kd_ref[1] = jnp.exp(lgb_l * pos)
    cd = [jnp.exp(lgf_l[0:1] * float(c)), jnp.exp(lgb_l[0:1] * float(c))]

    def chunk(r0, state, d):
        q = q_ref[pl.ds(r0, c), :]
        k = k_ref[pl.ds(r0, c), :]
        v = v_ref[pl.ds(r0, c), :]
        qb = q.astype(BF16)
        kcat = jnp.concatenate([jnp.where(m0, k, 0.0), jnp.where(m0, 0.0, k)], axis=0).astype(BF16)
        vcat = jnp.concatenate([jnp.where(m0, v, 0.0), jnp.where(m0, 0.0, v)], axis=0).astype(BF16)
        s = _dot_nt(qb, kcat) * intra_ref[d]
        out = jnp.dot(s.astype(BF16), vcat, preferred_element_type=F32)
        out = out + jnp.dot(qb, state.astype(BF16), preferred_element_type=F32) * qd_ref[d]
        kv = _dot_tn((k * kd_ref[d]).astype(BF16), v.astype(BF16))
        return out, state * cd[d] + jnp.where(same_head, kv, 0.0)

    def body(i, states):
        sf, sb = states
        rf = pl.multiple_of(i * c, c)
        ib = jnp.where(i < ctx_chunks, ctx_chunks - 1 - i, n_chunks + ctx_chunks - 1 - i)
        rb = pl.multiple_of(ib * c, c)
        of, sf = chunk(rf, sf, 0)
        ob, sb = chunk(rb, sb, 1)
        accf_ref[pl.ds(rf, c), :] = of
        accb_ref[pl.ds(rb, c), :] = ob
        return sf, sb

    z = jnp.zeros((LANES, LANES), F32)
    lax.fori_loop(0, n_chunks, body, (z, z), unroll=2)

    avg = jnp.where(same_head, 1.0 / HEAD_DIM, 0.0).astype(BF16)

    def head_mean(x):
        hi, lo = _split_bf16(x)
        return (jnp.dot(hi, avg, preferred_element_type=F32) + jnp.dot(lo, avg, preferred_element_type=F32))

    def readout(i, carry):
        r0 = pl.multiple_of(i * c, c)
        o = accf_ref[pl.ds(r0, c), :] + accb_ref[pl.ds(r0, c), :]
        dlt = o - head_mean(o)
        var = head_mean(dlt * dlt)
        y = dlt * lax.rsqrt(var + GN_EPS) * gn_ref[...]
        o_ref[pl.ds(r0, c), :] = _silu(g_ref[pl.ds(r0, c), :]) * y
        return carry

    lax.fori_loop(0, n_chunks, readout, 0, unroll=2)


def _retention(p, log_gamma, ret_gn):
    b, s, _ = p.shape
    nb = RET_W // LANES
    blk = lambda off: pl.BlockSpec((None, s, LANES), lambda bi, hp: (bi, 0, off + hp))
    return pl.pallas_call(
        _ret_kernel,
        grid=(b, nb),
        in_specs=[
            pl.BlockSpec(memory_space=pltpu.SMEM),
            blk(0), blk(nb), blk(2 * nb), blk(3 * nb),
            pl.BlockSpec((1, LANES), lambda bi, hp: (0, hp)),
        ],
        out_specs=pl.BlockSpec((None, s, LANES), lambda bi, hp: (bi, 0, hp)),
        out_shape=jax.ShapeDtypeStruct((b, s, RET_W), F32),
        scratch_shapes=[pltpu.VMEM((s, LANES), F32), pltpu.VMEM((s, LANES), F32),
                        pltpu.VMEM((2, RET_CHUNK, 2 * RET_CHUNK), F32),
                        pltpu.VMEM((2, RET_CHUNK, LANES), F32), pltpu.VMEM((2, RET_CHUNK, LANES), F32)],
        compiler_params=_cparams(("arbitrary", "arbitrary")),
        name="retention",
    )(log_gamma, p, p, p, p, ret_gn.reshape(1, RET_W))


def _softmax_pv(s_list, v_list, extra=None):
    m = s_list[0].max(axis=-1, keepdims=True)
    for s in s_list[1:]:
        m = jnp.maximum(m, s.max(axis=-1, keepdims=True))
    if extra is not None:
        m = jnp.maximum(m, extra)
    den = None
    o = None
    for s, v in zip(s_list, v_list):
        p = jnp.exp(s - m)
        ds = p.sum(axis=-1, keepdims=True)
        den = ds if den is None else den + ds
        pv = jnp.dot(p.astype(BF16), v, preferred_element_type=F32)
        o = pv if o is None else o + pv
    if extra is not None:
        den = den + jnp.exp(extra - m)
    return o / den


def _stage_heads(q_ref, k_ref, v_ref, qm_ref, kb_ref, vb_ref):
    m0 = lax.broadcasted_iota(jnp.int32, (TM, LANES), 1) < HEAD_DIM

    def stage(i, carry):
        r0 = pl.multiple_of(i * TM, TM)
        q = q_ref[pl.ds(r0, TM), :]
        qm_ref[0, pl.ds(r0, TM), :] = jnp.where(m0, q, 0.0).astype(BF16)
        qm_ref[1, pl.ds(r0, TM), :] = jnp.where(m0, 0.0, q).astype(BF16)
        kb_ref[pl.ds(r0, TM), :] = k_ref[pl.ds(r0, TM), :].astype(BF16)
        vb_ref[pl.ds(r0, TM), :] = v_ref[pl.ds(r0, TM), :].astype(BF16)
        return carry

    lax.fori_loop(0, q_ref.shape[0] // TM, stage, 0)


def _na_kernel(q_ref, k_ref, v_ref, bias_ref, o_ref, qm_ref, kb_ref, vb_ref):
    s_len = q_ref.shape[0]
    rows = (s_len - TM) // GRID_W
    nloc = NA_KH * GRID_W
    _stage_heads(q_ref, k_ref, v_ref, qm_ref, kb_ref, vb_ref)

    kc = kb_ref[0:TM, :]
    vc = vb_ref[0:TM, :]
    outs = [_softmax_pv([_dot_nt(qm_ref[hh, 0:TM, :], kc)], [vc]) for hh in range(2)]
    m0c = lax.broadcasted_iota(jnp.int32, (TM, LANES), 1) < HEAD_DIM
    o_ref[0:TM, :] = jnp.where(m0c, outs[0], outs[1])

    m0 = lax.broadcasted_iota(jnp.int32, (GRID_W, LANES), 1) < HEAD_DIM

    def row_block(r, carry):
        rs = jnp.clip(r - NA_KH // 2, 0, rows - NA_KH)
        pat = r - rs
        q0 = pl.multiple_of(TM + r * GRID_W, GRID_W)
        k0 = pl.multiple_of(TM + rs * GRID_W, GRID_W)
        kl = kb_ref[pl.ds(k0, nloc), :]
        vl = vb_ref[pl.ds(k0, nloc), :]
        kc = kb_ref[0:TM, :]
        vc = vb_ref[0:TM, :]
        q = jnp.concatenate([qm_ref[0, pl.ds(q0, GRID_W), :], qm_ref[1, pl.ds(q0, GRID_W), :]], axis=0)
        s_loc = _dot_nt(q, kl) + bias_ref[pat].reshape(2 * GRID_W, nloc)
        s_ctx = _dot_nt(q, kc)
        res = _softmax_pv([s_loc, s_ctx], [vl, vc])
        o_ref[pl.ds(q0, GRID_W), :] = jnp.where(m0, res[:GRID_W], res[GRID_W:])
        return carry

    lax.fori_loop(0, rows, row_block, 0, unroll=4)


def _na_bias(rpb):
    h = rpb.shape[0]
    qc = np.arange(GRID_W)[:, None]
    kc = np.arange(GRID_W)[None, :]
    win = np.clip(qc - NA_KW // 2, 0, GRID_W - NA_KW)
    valid = (kc >= win) & (kc < win + NA_KW)
    col_off = np.clip(kc - qc + NA_KW - 1, 0, 2 * NA_KW - 2)
    pat = np.arange(NA_KH)[:, None]
    row_off = np.arange(NA_KH)[None, :] - pat + NA_KH - 1
    bias = rpb[:, row_off][:, :, :, col_off]
    bias = jnp.where(valid[None, None, None], bias.astype(F32), NEG_INF)
    bias = bias.transpose(1, 0, 3, 2, 4)
    return bias.reshape(NA_KH, h, GRID_W, NA_KH * GRID_W)


def _neighbourhood(p, bias):
    b, s, _ = p.shape
    nb = NA_W // LANES
    c0 = 4 * RET_W // LANES
    blk = lambda off: pl.BlockSpec((None, s, LANES), lambda bi, hp: (bi, 0, c0 + off + hp))
    return pl.pallas_call(
        _na_kernel,
        grid=(b, nb),
        in_specs=[
            blk(0), blk(nb), blk(2 * nb),
            pl.BlockSpec((NA_KH, 2, GRID_W, NA_KH * GRID_W), lambda bi, hp: (0, hp, 0, 0)),
        ],
        out_specs=pl.BlockSpec((None, s, LANES), lambda bi, hp: (bi, 0, hp)),
        out_shape=jax.ShapeDtypeStruct((b, s, NA_W), F32),
        scratch_shapes=[pltpu.VMEM((2, s, LANES), BF16), pltpu.VMEM((s, LANES), BF16),
                        pltpu.VMEM((s, LANES), BF16)],
        compiler_params=_cparams(("arbitrary", "arbitrary")),
        name="neighbourhood",
    )(p, p, p, bias)


def _swa_kernel(sink_ref, q_ref, k_ref, v_ref, o_ref, kd_ref, vd_ref):
    kp = pl.program_id(1)
    s_len = q_ref.shape[0]
    seq = s_len - TM
    nblk = seq // SWA_BLOCK
    band = SWA_BLOCK + 2 * SWA_WINDOW
    group = SWA_HEADS // SWA_KV_HEADS
    heads_per_step = 2 * group
    m0t = lax.broadcasted_iota(jnp.int32, (TM, LANES), 1) < HEAD_DIM

    def stage(i, carry):
        r0 = pl.multiple_of(i * TM, TM)
        for src, dst in ((k_ref, kd_ref), (v_ref, vd_ref)):
            x = src[pl.ds(r0, TM), :]
            xr = pltpu.roll(x, HEAD_DIM, 1)
            dst[0, pl.ds(r0, TM), :] = jnp.where(m0t, x, xr).astype(BF16)
            dst[1, pl.ds(r0, TM), :] = jnp.where(m0t, xr, x).astype(BF16)
        return carry

    lax.fori_loop(0, s_len // TM, stage, 0)
    o_ref[0:TM, :] = jnp.zeros((TM, o_ref.shape[1]), F32)

    qi = lax.broadcasted_iota(jnp.int32, (2 * SWA_BLOCK, band), 0) % SWA_BLOCK
    ki = lax.broadcasted_iota(jnp.int32, (2 * SWA_BLOCK, band), 1)
    m0 = lax.broadcasted_iota(jnp.int32, (SWA_BLOCK, LANES), 1) < HEAD_DIM
    first_head = lax.broadcasted_iota(jnp.int32, (2 * SWA_BLOCK, 1), 0) < SWA_BLOCK

    def block(i, carry):
        start = jnp.clip((i - 1) * SWA_BLOCK, 0, seq - band)
        k0 = pl.multiple_of(TM + start, SWA_BLOCK)
        q0 = pl.multiple_of(TM + i * SWA_BLOCK, SWA_BLOCK)
        valid = jnp.abs(qi + (i * SWA_BLOCK - start) - ki) <= SWA_WINDOW
        for hh in range(2):
            kb = kd_ref[hh, pl.ds(k0, band), :]
            vb = vd_ref[hh, pl.ds(k0, band), :]
            kc = kd_ref[hh, 0:TM, :]
            vc = vd_ref[hh, 0:TM, :]
            for j in range(group // 2):
                pair = hh * (group // 2) + j
                qp = q_ref[pl.ds(q0, SWA_BLOCK), pair * LANES:(pair + 1) * LANES]
                q = jnp.concatenate([jnp.where(m0, qp, 0.0), jnp.where(m0, 0.0, qp)], axis=0).astype(BF16)
                s = jnp.where(valid, _dot_nt(q, kb), NEG_INF)
                s_ctx = _dot_nt(q, kc)
                h0 = kp * heads_per_step + 2 * pair
                sink = jnp.where(first_head, sink_ref[h0], sink_ref[h0 + 1])
                res = _softmax_pv([s, s_ctx], [vb, vc], extra=sink)
                o_ref[pl.ds(q0, SWA_BLOCK), pair * LANES:(pair + 1) * LANES] = jnp.where(
                    m0, res[:SWA_BLOCK], res[SWA_BLOCK:])
        return carry

    lax.fori_loop(0, nblk, block, 0, unroll=2)


def _swa(p, sink):
    b, s, _ = p.shape
    qw = SWA_HEADS * HEAD_DIM // 2
    kblk = SWA_HEADS * HEAD_DIM // LANES
    vblk = kblk + SWA_KV_HEADS * HEAD_DIM // LANES
    return pl.pallas_call(
        _swa_kernel,
        grid=(b, 2),
        in_specs=[
            pl.BlockSpec(memory_space=pltpu.SMEM),
            pl.BlockSpec((None, s, qw), lambda bi, kp: (bi, 0, kp)),
            pl.BlockSpec((None, s, LANES), lambda bi, kp: (bi, 0, kblk + kp)),
            pl.BlockSpec((None, s, LANES), lambda bi, kp: (bi, 0, vblk + kp)),
        ],
        out_specs=pl.BlockSpec((None, s, qw), lambda bi, kp: (bi, 0, kp)),
        out_shape=jax.ShapeDtypeStruct((b, s, SWA_HEADS * HEAD_DIM), F32),
        scratch_shapes=[pltpu.VMEM((2, s, LANES), BF16)] * 2,
        compiler_params=_cparams(("arbitrary", "arbitrary")),
        name="swa",
    )(sink, p, p, p)


def _outproj_kernel(*refs, n_mix):
    mix_refs = refs[:n_mix]
    (w_ref, x_ref, g1_ref, gn_ref, sh_ref, sc_ref, wr_ref, br_ref,
     xo_ref, h_ref, ri_ref, cnt_ref, carry_ref) = refs[n_mix:]
    first = (pl.program_id(0) == 0) & (pl.program_id(1) == 0)

    @pl.when(first)
    def _():
        carry_ref[...] = jnp.zeros_like(carry_ref)

    o = None
    off = 0
    for m_ref in mix_refs:
        kw = m_ref.shape[1]
        part = jnp.dot(m_ref[...].astype(BF16), w_ref[off:off + kw, :], preferred_element_type=F32)
        o = part if o is None else o + part
        off += kw
    xn = x_ref[...] + g1_ref[...] * o
    xo_ref[...] = xn
    h = _rms_mod(xn, gn_ref[...], sh_ref[...], sc_ref[...])
    h_ref[...] = h

    logits = _dot3(h, wr_ref[...]) + br_ref[...]
    tm = logits.shape[0]
    lane = lax.broadcasted_iota(jnp.int32, (tm, LANES), 1).astype(F32)
    big = 1e9
    gmask = lane < MOE_GROUPS
    mg = jnp.max(jnp.where(gmask, logits, -big), axis=-1, keepdims=True)
    sg = jnp.sum(jnp.where(gmask, jnp.exp(jnp.minimum(logits - mg, 0.0)), 0.0), axis=-1, keepdims=True)
    gw = 1.0 / sg
    gi = jnp.min(jnp.where(gmask & (logits == mg), lane, big), axis=-1, keepdims=True)
    lo = ROUTE_LANE0 + MOE_EPG * gi
    emask = (lane >= lo) & (lane < lo + MOE_EPG)
    l1 = jnp.max(jnp.where(emask, logits, -big), axis=-1, keepdims=True)
    i1 = jnp.min(jnp.where(emask & (logits == l1), lane, big), axis=-1, keepdims=True)
    emask2 = emask & (lane != i1)
    l2 = jnp.max(jnp.where(emask2, logits, -big), axis=-1, keepdims=True)
    i2 = jnp.min(jnp.where(emask2 & (logits == l2), lane, big), axis=-1, keepdims=True)
    e21 = jnp.exp(l2 - l1)
    w1 = gw / (1.0 + e21)
    w2 = gw * e21 / (1.0 + e21)

    oh = jnp.where((lane == i1) | (lane == i2), 1.0, 0.0)
    tri = (lax.broadcasted_iota(jnp.int32, (tm, tm), 0) > lax.broadcasted_iota(jnp.int32, (tm, tm), 1))
    cum = jnp.dot(jnp.where(tri, 1.0, 0.0).astype(BF16), oh.astype(BF16), preferred_element_type=F32) + carry_ref[...]
    r1 = jnp.sum(jnp.where(lane == i1, cum, 0.0), axis=-1, keepdims=True)
    r2 = jnp.sum(jnp.where(lane == i2, cum, 0.0), axis=-1, keepdims=True)
    carry_ref[...] = carry_ref[...] + jnp.sum(oh, axis=0, keepdims=True)
    cnt_ref[...] = carry_ref[...]

    ri = jnp.where(lane == 0, i1 - ROUTE_LANE0, 0.0)
    ri = jnp.where(lane == 1, i2 - ROUTE_LANE0, ri)
    ri = jnp.where(lane == 2, w1, ri)
    ri = jnp.where(lane == 3, w2, ri)
    ri = jnp.where(lane == 4, r1, ri)
    ri = jnp.where(lane == 5, r2, ri)
    ri_ref[...] = ri


def _outproj(mixes, w_bf16, xa, g1, gain2, sh2, sc2, wr, br, t0):
    b, s, d = xa.shape
    nt = s // TM - t0
    so = nt * TM
    mod_idx = lambda bi, t: (jnp.where(t + t0 == 0, b, bi), 0, 0)
    tile = lambda wdt: pl.BlockSpec((None, TM, wdt), lambda bi, t: (bi, t + t0, 0))
    otile = lambda wdt: pl.BlockSpec((None, TM, wdt), lambda bi, t: (bi, t, 0))
    const = lambda shape: pl.BlockSpec(shape, lambda bi, t: (0,) * len(shape))
    in_specs = [tile(m.shape[2]) for m in mixes] + [
        const(w_bf16.shape), tile(d),
        pl.BlockSpec((None, 1, d), mod_idx), const((1, d)),
        pl.BlockSpec((None, 1, d), mod_idx), pl.BlockSpec((None, 1, d), mod_idx),
        const((d, LANES)), const((1, LANES)),
    ]
    return pl.pallas_call(
        functools.partial(_outproj_kernel, n_mix=len(mixes)),
        grid=(b, nt),
        in_specs=in_specs,
        out_specs=[otile(d), otile(d), otile(LANES), const((1, LANES))],
        out_shape=[jax.ShapeDtypeStruct((b, so, d), F32), jax.ShapeDtypeStruct((b, so, d), F32),
                   jax.ShapeDtypeStruct((b, so, LANES), F32), jax.ShapeDtypeStruct((1, LANES), F32)],
        scratch_shapes=[pltpu.VMEM((1, LANES), F32)],
        compiler_params=_cparams(("arbitrary", "arbitrary")),
        name="outproj",
    )(*mixes, w_bf16, xa, g1, gain2, sh2, sc2, wr, br)


def _dispatch_kernel(dest_ref, h_ref, xs_in_ref, xs_ref, hbuf, sem):
    del xs_in_ref
    nt = pl.num_programs(1)
    step = pl.program_id(0) * nt + pl.program_id(1)
    last = pl.num_programs(0) * nt - 1
    slot = step % 2
    hbuf[slot] = h_ref[...]

    def copy(sl, i, dst_row):
        return pltpu.make_async_copy(hbuf.at[sl, pl.ds(i, 1)], xs_ref.at[pl.ds(dst_row, 1)], sem.at[sl])

    def issue(i, c):
        copy(slot, i, dest_ref[step, i]).start()
        copy(slot, i, dest_ref[step, TM + i]).start()
        return c

    lax.fori_loop(0, TM, issue, 0, unroll=8)

    def drain(sl):
        def one(i, c):
            copy(sl, 0, 0).wait()
            return c
        lax.fori_loop(0, 2 * TM, one, 0, unroll=8)

    @pl.when(step > 0)
    def _():
        drain(1 - slot)

    @pl.when(step == last)
    def _():
        drain(slot)


def _dispatch(dest, h2, n_pad):
    b, s, d = h2.shape
    nt = s // TM
    xs0 = jnp.zeros((n_pad, d), F32)
    return pl.pallas_call(
        _dispatch_kernel,
        grid_spec=pltpu.PrefetchScalarGridSpec(
            num_scalar_prefetch=1,
            grid=(b, nt),
            in_specs=[pl.BlockSpec((None, TM, d), lambda bi, t, dr: (bi, t, 0)),
                      pl.BlockSpec(memory_space=pl.ANY)],
            out_specs=pl.BlockSpec(memory_space=pl.ANY),
            scratch_shapes=[pltpu.VMEM((2, TM, d), F32), pltpu.SemaphoreType.DMA((2,))],
        ),
        out_shape=jax.ShapeDtypeStruct((n_pad, d), F32),
        input_output_aliases={2: 0},
        compiler_params=_cparams(("arbitrary", "arbitrary")),
        name="dispatch",
    )(dest, h2, xs0)


def _mlp_kernel(be_ref, nu_ref, x_ref, wgu_ref, wd_ref, y_ref, wgu_b, wd_b):
    i = pl.program_id(0)
    prev = be_ref[jnp.maximum(i - 1, 0)]
    used = i < nu_ref[0]

    @pl.when(used & ((i == 0) | (be_ref[i] != prev)))
    def _():
        wgu_b[...] = wgu_ref[...].astype(BF16)
        wd_b[...] = wd_ref[...].astype(BF16)

    @pl.when(used)
    def _():
        gu = jnp.dot(x_ref[...].astype(BF16), wgu_b[...], preferred_element_type=F32)
        act = _silu(gu[:, :MOE_FF]) * gu[:, MOE_FF:]
        y_ref[...] = jnp.dot(act.astype(BF16), wd_b[...], preferred_element_type=F32)

    @pl.when(jnp.logical_not(used))
    def _():
        y_ref[...] = jnp.zeros_like(y_ref)


def _expert_mlp(block_e, n_used, xs, w_gu, w_down, layer):
    n_pad, d = xs.shape
    ff2 = w_gu.shape[-1]
    return pl.pallas_call(
        _mlp_kernel,
        grid_spec=pltpu.PrefetchScalarGridSpec(
            num_scalar_prefetch=2,
            grid=(n_pad // MOE_ROWS,),
            in_specs=[
                pl.BlockSpec((MOE_ROWS, d), lambda i, be, nu: (i, 0)),
                pl.BlockSpec((None, None, d, ff2), lambda i, be, nu: (layer, be[i], 0, 0)),
                pl.BlockSpec((None, None, ff2 // 2, d), lambda i, be, nu: (layer, be[i], 0, 0)),
            ],
            out_specs=pl.BlockSpec((MOE_ROWS, d), lambda i, be, nu: (i, 0)),
            scratch_shapes=[pltpu.VMEM((d, ff2), BF16), pltpu.VMEM((ff2 // 2, d), BF16)],
        ),
        out_shape=jax.ShapeDtypeStruct((n_pad, d), F32),
        compiler_params=_cparams(("arbitrary",)),
        name="expert_mlp",
    )(block_e, n_used, xs, w_gu, w_down)


def _combine_kernel(dest_ref, ys_ref, x_ref, ri_ref, g2_ref, fg_ref, o_ref, buf, sem, *, final):
    bi = pl.program_id(0)
    t = pl.program_id(1)
    nt = pl.num_programs(1)
    step = bi * nt + t
    total = pl.num_programs(0) * nt

    def copy(src_row, slot, k, i):
        return pltpu.make_async_copy(ys_ref.at[pl.ds(src_row, 1)], buf.at[slot, k, pl.ds(i, 1)], sem.at[slot])

    def issue(st, slot):
        def one(i, c):
            copy(dest_ref[st, i], slot, 0, i).start()
            copy(dest_ref[st, TM + i], slot, 1, i).start()
            return c
        lax.fori_loop(0, TM, one, 0, unroll=8)

    slot = step % 2

    @pl.when(step == 0)
    def _():
        issue(0, 0)

    @pl.when(step + 1 < total)
    def _():
        issue(step + 1, 1 - slot)

    def drain(i, c):
        copy(0, slot, 0, 0).wait()
        return c

    lax.fori_loop(0, 2 * TM, drain, 0, unroll=8)

    lane = lax.broadcasted_iota(jnp.int32, (TM, LANES), 1)
    ri = ri_ref[...]
    w1 = jnp.sum(jnp.where(lane == 2, ri, 0.0), axis=-1, keepdims=True)
    w2 = jnp.sum(jnp.where(lane == 3, ri, 0.0), axis=-1, keepdims=True)
    y = buf[slot, 0] * w1 + buf[slot, 1] * w2
    xn = x_ref[...] + g2_ref[...] * y
    if final:
        ms = jnp.mean(xn * xn, axis=-1, keepdims=True)
        xn = xn * lax.rsqrt(ms + NORM_EPS) * fg_ref[...]
    o_ref[...] = xn


def _combine(dest, ys, xa, rinfo, g2, final_g, has_ctx, final):
    b, s, d = xa.shape
    nt = s // TM
    mod_idx = lambda bi, t, dr: (jnp.where(t == 0, b, bi) if has_ctx else bi, 0, 0)
    tile = lambda wdt: pl.BlockSpec((None, TM, wdt), lambda bi, t, dr: (bi, t, 0))
    out_spec = tile(d)
    out_shape = jax.ShapeDtypeStruct((b, s, d), F32)
    return pl.pallas_call(
        functools.partial(_combine_kernel, final=final),
        grid_spec=pltpu.PrefetchScalarGridSpec(
            num_scalar_prefetch=1,
            grid=(b, nt),
            in_specs=[
                pl.BlockSpec(memory_space=pl.ANY),
                tile(d), tile(LANES),
                pl.BlockSpec((None, 1, d), mod_idx),
                pl.BlockSpec((1, d), lambda bi, t, dr: (0, 0)),
            ],
            out_specs=out_spec,
            scratch_shapes=[pltpu.VMEM((2, 2, TM, d), F32), pltpu.SemaphoreType.DMA((2,))],
        ),
        out_shape=out_shape,
        compiler_params=_cparams(("arbitrary", "arbitrary")),
        name="combine",
    )(dest, ys, xa, rinfo, g2, final_g)


def _moe(h2, rinfo, counts, w_gu, w_down, layer):
    b, s, d = h2.shape
    nt = s // TM
    r = rinfo[:, :, :8]
    e = r[..., 0:2].astype(jnp.int32)
    rank = r[..., 4:6].astype(jnp.int32)
    cnt = counts[0, ROUTE_LANE0:ROUTE_LANE0 + MOE_EXPERTS].astype(jnp.int32)
    padded = (cnt + MOE_ROWS - 1) // MOE_ROWS * MOE_ROWS
    ends = jnp.cumsum(padded)
    starts = ends - padded
    eids = jnp.arange(MOE_EXPERTS, dtype=jnp.int32)
    dest = jnp.sum(jnp.where(e[..., None] == eids, starts, 0), axis=-1) + rank
    dest = dest.reshape(b, nt, TM, 2).transpose(0, 1, 3, 2).reshape(b * nt, 2 * TM)
    n_assign = b * nt * TM * 2
    n_blocks = (n_assign + MOE_EXPERTS * (MOE_ROWS - 1)) // MOE_ROWS + 1
    n_pad = n_blocks * MOE_ROWS
    blk_row = jnp.arange(n_blocks, dtype=jnp.int32) * MOE_ROWS
    block_e = jnp.minimum(jnp.sum((blk_row[:, None] >= ends[None, :]).astype(jnp.int32), axis=1),
                          MOE_EXPERTS - 1)
    n_used = (ends[-1] // MOE_ROWS).astype(jnp.int32).reshape(1)
    xs = _dispatch(dest, h2, n_pad)
    ys = _expert_mlp(block_e, n_used, xs, w_gu, w_down, layer)
    return dest, ys


def _router_weights(wg, bg, we, be):
    d = wg.shape[0]
    pad = LANES - MOE_GROUPS - MOE_EXPERTS
    assert ROUTE_LANE0 == MOE_GROUPS
    wr = jnp.concatenate([wg.astype(F32), we.astype(F32), jnp.zeros((d, pad), F32)], axis=1)
    br = jnp.concatenate([bg.astype(F32), be.astype(F32), jnp.zeros((pad,), F32)]).reshape(1, LANES)
    return wr, br


def kernel(x, c, ctx, c_ctx, ada_w, ada_b, norm_g, final_g, ab_w_in, ab_w_out, ret_decay, ret_gn, na_rpb,
           swa_w_in, swa_w_out, swa_sink, router_g_w, router_g_b, router_e_w, router_e_b,
           expert_w_gu, expert_w_down):
    b, seq, d = x.shape
    assert ctx.shape[1] == TM and seq % TM == 0 and d == D_MODEL
    xa = jnp.concatenate([ctx, x], axis=1)
    rope = _rope_tables(seq)

    cvec = jnp.concatenate([c, c_ctx[None, :], jnp.zeros((7, d), F32)], axis=0)
    mod = _adaln(cvec, ada_w, ada_b)
    mod = mod.reshape(DEPTH, b + 8, 6, 1, d)[:, :b + 1].transpose(0, 2, 1, 3, 4)

    for layer in range(DEPTH):
        last = layer == DEPTH - 1
        sh1, sc1, g1, sh2, sc2, g2 = (mod[layer, i] for i in range(6))
        gain1 = norm_g[layer, 0].reshape(1, d)
        gain2 = norm_g[layer, 1].reshape(1, d)
        j = layer // 2
        if layer % 2 == 0:
            nb = RET_W // LANES
            p = _proj(xa, gain1, sh1, sc1, ab_w_in[j].astype(BF16), rope,
                      rope_blocks=range(0, 2 * nb),
                      scale_blocks=list(range(nb, 2 * nb)) + list(range(4 * nb, 5 * nb)))
            log_gamma = jnp.log1p(-jnp.exp2(-ret_decay[j].astype(F32)))
            mixes = [_retention(p, log_gamma, ret_gn[j]), _neighbourhood(p, _na_bias(na_rpb[j]))]
            w_out = ab_w_out[j]
        else:
            qb = SWA_HEADS * HEAD_DIM // LANES
            kb = SWA_KV_HEADS * HEAD_DIM // LANES
            p = _proj(xa, gain1, sh1, sc1, swa_w_in[j].astype(BF16), rope,
                      rope_blocks=range(0, qb + kb), scale_blocks=range(0, qb))
            mixes = [_swa(p, swa_sink[j].astype(F32))]
            w_out = swa_w_out[j]
        t0 = 1 if last else 0
        wr, br = _router_weights(router_g_w[layer], router_g_b[layer], router_e_w[layer], router_e_b[layer])
        xa, h2, rinfo, counts = _outproj(mixes, w_out.astype(BF16), xa, g1, gain2, sh2, sc2, wr, br, t0)
        dest, ys = _moe(h2, rinfo, counts, expert_w_gu, expert_w_down, layer)
        xa = _combine(dest, ys, xa, rinfo, g2, final_g.reshape(1, d), not last, last)
    return xa
```

```python
import functools

import numpy as np
import jax
import jax.numpy as jnp
from jax import lax
from jax.experimental import pallas as pl
from jax.experimental.pallas import tpu as pltpu

F32 = jnp.float32
BF16 = jnp.bfloat16

D_MODEL = 1024
DEPTH = 2
GRID_W = 64
HEAD_DIM = 64
RET_HEADS = 8
NA_HEADS = 8
RET_W = 512
NA_W = 512
AB_IN = 4 * RET_W + 3 * NA_W
RET_CHUNK = 128
GN_EPS = 1e-5
NA_KH = 8
NA_KW = 16
SWA_HEADS = 16
SWA_KV_HEADS = 4
SWA_WINDOW = 128
SWA_BLOCK = 128
SWA_IN = (SWA_HEADS + 2 * SWA_KV_HEADS) * HEAD_DIM
ROPE_BASE = 10000.0
MOE_GROUPS = 4
MOE_EPG = 8
MOE_EXPERTS = 32
MOE_FF = 512
NORM_EPS = 1e-6
NEG_INF = -1e30

LANES = 128
TM = 256
MOE_ROWS = 256
ROUTE_LANE0 = 4
VMEM_LIMIT = 56 * 1024 * 1024


def _cparams(sem, vmem=VMEM_LIMIT):
    return pltpu.CompilerParams(dimension_semantics=sem, vmem_limit_bytes=vmem)


def _split_bf16(a):
    hi = a.astype(BF16)
    lo = (a - hi.astype(F32)).astype(BF16)
    return hi, lo


def _dot3(a, b):
    ah, al = _split_bf16(a)
    bh, bl = _split_bf16(b)
    d = lambda x, y: jnp.dot(x, y, preferred_element_type=F32)
    return d(ah, bh) + (d(ah, bl) + d(al, bh))


def _dot_nt(a, b):
    return lax.dot_general(a, b, (((1,), (1,)), ((), ())), preferred_element_type=F32)


def _dot_tn(a, b):
    return lax.dot_general(a, b, (((0,), (0,)), ((), ())), preferred_element_type=F32)


def _silu(x):
    return x / (1.0 + jnp.exp(-x))


def _adaln_kernel(c_ref, w_ref, b_ref, o_ref):
    o_ref[...] = _dot3(_silu(c_ref[...]), w_ref[...]) + b_ref[...]


def _adaln(cvec, ada_w, ada_b):
    depth, d, n6 = ada_w.shape
    rows = cvec.shape[0]
    tn = 1024
    return pl.pallas_call(
        _adaln_kernel,
        grid=(depth, n6 // tn),
        in_specs=[
            pl.BlockSpec((rows, d), lambda l, j: (0, 0)),
            pl.BlockSpec((None, d, tn), lambda l, j: (l, 0, j)),
            pl.BlockSpec((None, 1, tn), lambda l, j: (l, 0, j)),
        ],
        out_specs=pl.BlockSpec((None, rows, tn), lambda l, j: (l, 0, j)),
        out_shape=jax.ShapeDtypeStruct((depth, rows, n6), F32),
        compiler_params=_cparams(("arbitrary", "arbitrary")),
        name="adaln",
    )(cvec, ada_w, ada_b.reshape(depth, 1, n6))


def _rms_mod(x, g, sh, sc):
    ms = jnp.mean(x * x, axis=-1, keepdims=True)
    return (x * lax.rsqrt(ms + NORM_EPS) * g) * (1.0 + sc) + sh


def _proj_kernel(x_ref, g_ref, sh_ref, sc_ref, w_ref, rope_ref, o_ref, *, rope_blocks, scale_blocks, cn):
    is_lat = pl.program_id(1) > 0
    hb = _rms_mod(x_ref[...], g_ref[...], sh_ref[...], sc_ref[...]).astype(BF16)
    nout = w_ref.shape[1]
    for c in range(nout // cn):
        o = jnp.dot(hb, w_ref[:, c * cn:(c + 1) * cn], preferred_element_type=F32)
        for s in range(cn // LANES):
            blk = c * (cn // LANES) + s
            ob = o[:, s * LANES:(s + 1) * LANES]
            if blk in rope_blocks:
                r = (ob * rope_ref[0] + pltpu.roll(ob, 16, 1) * rope_ref[1]
                     + pltpu.roll(ob, LANES - 16, 1) * rope_ref[2])
                ob = jnp.where(is_lat, r, ob)
            if blk in scale_blocks:
                ob = ob * (HEAD_DIM ** -0.5)
            o_ref[:, blk * LANES:(blk + 1) * LANES] = ob


def _proj(xa, gain, sh, sc, w_bf16, rope, rope_blocks, scale_blocks):
    b, s, d = xa.shape
    nout = w_bf16.shape[1]
    nt = s // TM
    mod_idx = lambda bi, t: (jnp.where(t == 0, b, bi), 0, 0)
    kern = functools.partial(_proj_kernel, rope_blocks=frozenset(rope_blocks),
                             scale_blocks=frozenset(scale_blocks), cn=512)
    return pl.pallas_call(
        kern,
        grid=(b, nt),
        in_specs=[
            pl.BlockSpec((None, TM, d), lambda bi, t: (bi, t, 0)),
            pl.BlockSpec((1, d), lambda bi, t: (0, 0)),
            pl.BlockSpec((None, 1, d), mod_idx),
            pl.BlockSpec((None, 1, d), mod_idx),
            pl.BlockSpec((d, nout), lambda bi, t: (0, 0)),
            pl.BlockSpec((3, TM, LANES), lambda bi, t: (0, jnp.maximum(t - 1, 0), 0)),
        ],
        out_specs=pl.BlockSpec((None, TM, nout), lambda bi, t: (bi, t, 0)),
        out_shape=jax.ShapeDtypeStruct((b, s, nout), F32),
        compiler_params=_cparams(("arbitrary", "arbitrary")),
        name="proj",
    )(xa, gain, sh, sc, w_bf16, rope)


def _rope_tables(seq):
    nf = HEAD_DIM // 4
    inv = ROPE_BASE ** (-jnp.arange(nf, dtype=F32) / nf)
    t = jnp.arange(seq)
    row = (t // GRID_W).astype(F32)
    col = (t % GRID_W).astype(F32)
    lane = np.arange(LANES)
    jj = lane % HEAD_DIM
    axis_is_col = (jj // 32) == 1
    second_half = (jj % 32) >= 16
    f = jj % 16
    pos = jnp.where(axis_is_col[None, :], col[:, None], row[:, None])
    ang = pos * inv[f][None, :]
    c, s = jnp.cos(ang), jnp.sin(ang)
    sa = jnp.where(second_half[None, :], s, 0.0)
    sb = jnp.where(second_half[None, :], 0.0, -s)
    return jnp.stack([c, sa, sb], axis=0)


def _ret_kernel(lg_ref, q_ref, k_ref, v_ref, g_ref, gn_ref, o_ref, accf_ref, accb_ref,
                intra_ref, qd_ref, kd_ref):
    hp = pl.program_id(1)
    c = RET_CHUNK
    s_len = q_ref.shape[0]
    n_chunks = s_len // c
    ctx_chunks = TM // c
    pos = lax.broadcasted_iota(jnp.int32, (c, LANES), 0).astype(F32)
    lane = lax.broadcasted_iota(jnp.int32, (c, LANES), 1)
    row = lax.broadcasted_iota(jnp.int32, (c, LANES), 0)
    m0 = lane < HEAD_DIM
    same_head = (row < HEAD_DIM) == m0
    rel = (row - lane).astype(F32)
    lgf = [lg_ref[0, hp * 2 + hh] for hh in range(2)]
    lgb = [lg_ref[1, hp * 2 + hh] for hh in range(2)]
    lgf_l = jnp.where(m0, lgf[0], lgf[1])
    lgb_l = jnp.where(m0, lgb[0], lgb[1])
    for hh in range(2):
        intra_ref[0, :, hh * c:(hh + 1) * c] = jnp.where(rel >= 0, jnp.exp(lgf[hh] * jnp.maximum(rel, 0.0)), 0.0)
        intra_ref[1, :, hh * c:(hh + 1) * c] = jnp.where(rel <= 0, jnp.exp(lgb[hh] * jnp.maximum(-rel, 0.0)), 0.0)
    qd_ref[0] = jnp.exp(lgf_l * (pos + 1.0))
    qd_ref[1] = jnp.exp(lgb_l * (c - pos))
    kd_ref[0] = jnp.exp(lgf_l * (c - 1.0 - pos))
    kd_ref[1] = jnp.exp(lgb_l * pos)
    cd = [jnp.exp(lgf_l[0:1] * float(c)), jnp.exp(lgb_l[0:1] * float(c))]

    def chunk(r0, state, d):
        q = q_ref[pl.ds(r0, c), :]
        k = k_ref[pl.ds(r0, c), :]
        v = v_ref[pl.ds(r0, c), :]
        qb = q.astype(BF16)
        kcat = jnp.concatenate([jnp.where(m0, k, 0.0), jnp.where(m0, 0.0, k)], axis=0).astype(BF16)
        vcat = jnp.concatenate([jnp.where(m0, v, 0.0), jnp.where(m0, 0.0, v)], axis=0).astype(BF16)
        s = _dot_nt(qb, kcat) * intra_ref[d]
        out = jnp.dot(s.astype(BF16), vcat, preferred_element_type=F32)
        out = out + jnp.dot(qb, state.astype(BF16), preferred_element_type=F32) * qd_ref[d]
        kv = _dot_tn((k * kd_ref[d]).astype(BF16), v.astype(BF16))
        return out, state * cd[d] + jnp.where(same_head, kv, 0.0)

    def body(i, states):
        sf, sb = states
        rf = pl.multiple_of(i * c, c)
        ib = jnp.where(i < ctx_chunks, ctx_chunks - 1 - i, n_chunks + ctx_chunks - 1 - i)
        rb = pl.multiple_of(ib * c, c)
        of, sf = chunk(rf, sf, 0)
        ob, sb = chunk(rb, sb, 1)
        accf_ref[pl.ds(rf, c), :] = of
        accb_ref[pl.ds(rb, c), :] = ob
        return sf, sb

    z = jnp.zeros((LANES, LANES), F32)
    lax.fori_loop(0, n_chunks, body, (z, z), unroll=2)

    avg = jnp.where(same_head, 1.0 / HEAD_DIM, 0.0).astype(BF16)

    def head_mean(x):
        hi, lo = _split_bf16(x)
        return (jnp.dot(hi, avg, preferred_element_type=F32) + jnp.dot(lo, avg, preferred_element_type=F32))

    def readout(i, carry):
        r0 = pl.multiple_of(i * c, c)
        o = accf_ref[pl.ds(r0, c), :] + accb_ref[pl.ds(r0, c), :]
        dlt = o - head_mean(o)
        var = head_mean(dlt * dlt)
        y = dlt * lax.rsqrt(var + GN_EPS) * gn_ref[...]
        o_ref[pl.ds(r0, c), :] = _silu(g_ref[pl.ds(r0, c), :]) * y
        return carry

    lax.fori_loop(0, n_chunks, readout, 0, unroll=2)


def _retention(p, log_gamma, ret_gn):
    b, s, _ = p.shape
    nb = RET_W // LANES
    blk = lambda off: pl.BlockSpec((None, s, LANES), lambda bi, hp: (bi, 0, off + hp))
    return pl.pallas_call(
        _ret_kernel,
        grid=(b, nb),
        in_specs=[
            pl.BlockSpec(memory_space=pltpu.SMEM),
            blk(0), blk(nb), blk(2 * nb), blk(3 * nb),
            pl.BlockSpec((1, LANES), lambda bi, hp: (0, hp)),
        ],
        out_specs=pl.BlockSpec((None, s, LANES), lambda bi, hp: (bi, 0, hp)),
        out_shape=jax.ShapeDtypeStruct((b, s, RET_W), F32),
        scratch_shapes=[pltpu.VMEM((s, LANES), F32), pltpu.VMEM((s, LANES), F32),
                        pltpu.VMEM((2, RET_CHUNK, 2 * RET_CHUNK), F32),
                        pltpu.VMEM((2, RET_CHUNK, LANES), F32), pltpu.VMEM((2, RET_CHUNK, LANES), F32)],
        compiler_params=_cparams(("arbitrary", "arbitrary")),
        name="retention",
    )(log_gamma, p, p, p, p, ret_gn.reshape(1, RET_W))


def _softmax_pv(s_list, v_list, extra=None):
    m = s_list[0].max(axis=-1, keepdims=True)
    for s in s_list[1:]:
        m = jnp.maximum(m, s.max(axis=-1, keepdims=True))
    if extra is not None:
        m = jnp.maximum(m, extra)
    den = None
    o = None
    for s, v in zip(s_list, v_list):
        p = jnp.exp(s - m)
        ds = p.sum(axis=-1, keepdims=True)
        den = ds if den is None else den + ds
        pv = jnp.dot(p.astype(BF16), v, preferred_element_type=F32)
        o = pv if o is None else o + pv
    if extra is not None:
        den = den + jnp.exp(extra - m)
    return o / den


def _stage_heads(q_ref, k_ref, v_ref, qm_ref, kb_ref, vb_ref):
    m0 = lax.broadcasted_iota(jnp.int32, (TM, LANES), 1) < HEAD_DIM

    def stage(i, carry):
        r0 = pl.multiple_of(i * TM, TM)
        q = q_ref[pl.ds(r0, TM), :]
        qm_ref[0, pl.ds(r0, TM), :] = jnp.where(m0, q, 0.0).astype(BF16)
        qm_ref[1, pl.ds(r0, TM), :] = jnp.where(m0, 0.0, q).astype(BF16)
        kb_ref[pl.ds(r0, TM), :] = k_ref[pl.ds(r0, TM), :].astype(BF16)
        vb_ref[pl.ds(r0, TM), :] = v_ref[pl.ds(r0, TM), :].astype(BF16)
        return carry

    lax.fori_loop(0, q_ref.shape[0] // TM, stage, 0)


def _na_kernel(q_ref, k_ref, v_ref, bias_ref, o_ref, qm_ref, kb_ref, vb_ref):
    s_len = q_ref.shape[0]
    rows = (s_len - TM) // GRID_W
    nloc = NA_KH * GRID_W
    _stage_heads(q_ref, k_ref, v_ref, qm_ref, kb_ref, vb_ref)

    kc = kb_ref[0:TM, :]
    vc = vb_ref[0:TM, :]
    outs = [_softmax_pv([_dot_nt(qm_ref[hh, 0:TM, :], kc)], [vc]) for hh in range(2)]
    m0c = lax.broadcasted_iota(jnp.int32, (TM, LANES), 1) < HEAD_DIM
    o_ref[0:TM, :] = jnp.where(m0c, outs[0], outs[1])

    m0 = lax.broadcasted_iota(jnp.int32, (GRID_W, LANES), 1) < HEAD_DIM

    def row_block(r, carry):
        rs = jnp.clip(r - NA_KH // 2, 0, rows - NA_KH)
        pat = r - rs
        q0 = pl.multiple_of(TM + r * GRID_W, GRID_W)
        k0 = pl.multiple_of(TM + rs * GRID_W, GRID_W)
        kl = kb_ref[pl.ds(k0, nloc), :]
        vl = vb_ref[pl.ds(k0, nloc), :]
        kc = kb_ref[0:TM, :]
        vc = vb_ref[0:TM, :]
        q = jnp.concatenate([qm_ref[0, pl.ds(q0, GRID_W), :], qm_ref[1, pl.ds(q0, GRID_W), :]], axis=0)
        s_loc = _dot_nt(q, kl) + bias_ref[pat].reshape(2 * GRID_W, nloc)
        s_ctx = _dot_nt(q, kc)
        res = _softmax_pv([s_loc, s_ctx], [vl, vc])
        o_ref[pl.ds(q0, GRID_W), :] = jnp.where(m0, res[:GRID_W], res[GRID_W:])
        return carry

    lax.fori_loop(0, rows, row_block, 0, unroll=4)


def _na_bias(rpb):
    h = rpb.shape[0]
    qc = np.arange(GRID_W)[:, None]
    kc = np.arange(GRID_W)[None, :]
    win = np.clip(qc - NA_KW // 2, 0, GRID_W - NA_KW)
    valid = (kc >= win) & (kc < win + NA_KW)
    col_off = np.clip(kc - qc + NA_KW - 1, 0, 2 * NA_KW - 2)
    pat = np.arange(NA_KH)[:, None]
    row_off = np.arange(NA_KH)[None, :] - pat + NA_KH - 1
    bias = rpb[:, row_off][:, :, :, col_off]
    bias = jnp.where(valid[None, None, None], bias.astype(F32), NEG_INF)
    bias = bias.transpose(1, 0, 3, 2, 4)
    return bias.reshape(NA_KH, h, GRID_W, NA_KH * GRID_W)


def _neighbourhood(p, bias):
    b, s, _ = p.shape
    nb = NA_W // LANES
    c0 = 4 * RET_W // LANES
    blk = lambda off: pl.BlockSpec((None, s, LANES), lambda bi, hp: (bi, 0, c0 + off + hp))
    return pl.pallas_call(
        _na_kernel,
        grid=(b, nb),
        in_specs=[
            blk(0), blk(nb), blk(2 * nb),
            pl.BlockSpec((NA_KH, 2, GRID_W, NA_KH * GRID_W), lambda bi, hp: (0, hp, 0, 0)),
        ],
        out_specs=pl.BlockSpec((None, s, LANES), lambda bi, hp: (bi, 0, hp)),
        out_shape=jax.ShapeDtypeStruct((b, s, NA_W), F32),
        scratch_shapes=[pltpu.VMEM((2, s, LANES), BF16), pltpu.VMEM((s, LANES), BF16),
                        pltpu.VMEM((s, LANES), BF16)],
        compiler_params=_cparams(("arbitrary", "arbitrary")),
        name="neighbourhood",
    )(p, p, p, bias)


def _swa_kernel(sink_ref, q_ref, k_ref, v_ref, o_ref, kd_ref, vd_ref):
    kp = pl.program_id(1)
    s_len = q_ref.shape[0]
    seq = s_len - TM
    nblk = seq // SWA_BLOCK
    band = SWA_BLOCK + 2 * SWA_WINDOW
    group = SWA_HEADS // SWA_KV_HEADS
    heads_per_step = 2 * group
    m0t = lax.broadcasted_iota(jnp.int32, (TM, LANES), 1) < HEAD_DIM

    def stage(i, carry):
        r0 = pl.multiple_of(i * TM, TM)
        for src, dst in ((k_ref, kd_ref), (v_ref, vd_ref)):
            x = src[pl.ds(r0, TM), :]
            xr = pltpu.roll(x, HEAD_DIM, 1)
            dst[0, pl.ds(r0, TM), :] = jnp.where(m0t, x, xr).astype(BF16)
            dst[1, pl.ds(r0, TM), :] = jnp.where(m0t, xr, x).astype(BF16)
        return carry

    lax.fori_loop(0, s_len // TM, stage, 0)
    o_ref[0:TM, :] = jnp.zeros((TM, o_ref.shape[1]), F32)

    qi = lax.broadcasted_iota(jnp.int32, (2 * SWA_BLOCK, band), 0) % SWA_BLOCK
    ki = lax.broadcasted_iota(jnp.int32, (2 * SWA_BLOCK, band), 1)
    m0 = lax.broadcasted_iota(jnp.int32, (SWA_BLOCK, LANES), 1) < HEAD_DIM
    first_head = lax.broadcasted_iota(jnp.int32, (2 * SWA_BLOCK, 1), 0) < SWA_BLOCK

    def block(i, carry):
        start = jnp.clip((i - 1) * SWA_BLOCK, 0, seq - band)
        k0 = pl.multiple_of(TM + start, SWA_BLOCK)
        q0 = pl.multiple_of(TM + i * SWA_BLOCK, SWA_BLOCK)
        valid = jnp.abs(qi + (i * SWA_BLOCK - start) - ki) <= SWA_WINDOW
        for hh in range(2):
            kb = kd_ref[hh, pl.ds(k0, band), :]
            vb = vd_ref[hh, pl.ds(k0, band), :]
            kc = kd_ref[hh, 0:TM, :]
            vc = vd_ref[hh, 0:TM, :]
            for j in range(group // 2):
                pair = hh * (group // 2) + j
                qp = q_ref[pl.ds(q0, SWA_BLOCK), pair * LANES:(pair + 1) * LANES]
                q = jnp.concatenate([jnp.where(m0, qp, 0.0), jnp.where(m0, 0.0, qp)], axis=0).astype(BF16)
                s = jnp.where(valid, _dot_nt(q, kb), NEG_INF)
                s_ctx = _dot_nt(q, kc)
                h0 = kp * heads_per_step + 2 * pair
                sink = jnp.where(first_head, sink_ref[h0], sink_ref[h0 + 1])
                res = _softmax_pv([s, s_ctx], [vb, vc], extra=sink)
                o_ref[pl.ds(q0, SWA_BLOCK), pair * LANES:(pair + 1) * LANES] = jnp.where(
                    m0, res[:SWA_BLOCK], res[SWA_BLOCK:])
        return carry

    lax.fori_loop(0, nblk, block, 0, unroll=2)


def _swa(p, sink):
    b, s, _ = p.shape
    qw = SWA_HEADS * HEAD_DIM // 2
    kblk = SWA_HEADS * HEAD_DIM // LANES
    vblk = kblk + SWA_KV_HEADS * HEAD_DIM // LANES
    return pl.pallas_call(
        _swa_kernel,
        grid=(b, 2),
        in_specs=[
            pl.BlockSpec(memory_space=pltpu.SMEM),
            pl.BlockSpec((None, s, qw), lambda bi, kp: (bi, 0, kp)),
            pl.BlockSpec((None, s, LANES), lambda bi, kp: (bi, 0, kblk + kp)),
            pl.BlockSpec((None, s, LANES), lambda bi, kp: (bi, 0, vblk + kp)),
        ],
        out_specs=pl.BlockSpec((None, s, qw), lambda bi, kp: (bi, 0, kp)),
        out_shape=jax.ShapeDtypeStruct((b, s, SWA_HEADS * HEAD_DIM), F32),
        scratch_shapes=[pltpu.VMEM((2, s, LANES), BF16)] * 2,
        compiler_params=_cparams(("arbitrary", "arbitrary")),
        name="swa",
    )(sink, p, p, p)


def _outproj_kernel(*refs, n_mix):
    mix_refs = refs[:n_mix]
    (w_ref, x_ref, g1_ref, gn_ref, sh_ref, sc_ref, wr_ref, br_ref,
     xo_ref, h_ref, ri_ref, cnt_ref, carry_ref) = refs[n_mix:]
    first = (pl.program_id(0) == 0) & (pl.program_id(1) == 0)

    @pl.when(first)
    def _():
        carry_ref[...] = jnp.zeros_like(carry_ref)

    o = None
    off = 0
    for m_ref in mix_refs:
        kw = m_ref.shape[1]
        part = jnp.dot(m_ref[...].astype(BF16), w_ref[off:off + kw, :], preferred_element_type=F32)
        o = part if o is None else o + part
        off += kw
    xn = x_ref[...] + g1_ref[...] * o
    xo_ref[...] = xn
    h = _rms_mod(xn, gn_ref[...], sh_ref[...], sc_ref[...])
    h_ref[...] = h

    logits = _dot3(h, wr_ref[...]) + br_ref[...]
    tm = logits.shape[0]
    lane = lax.broadcasted_iota(jnp.int32, (tm, LANES), 1).astype(F32)
    big = 1e9
    gmask = lane < MOE_GROUPS
    mg = jnp.max(jnp.where(gmask, logits, -big), axis=-1, keepdims=True)
    sg = jnp.sum(jnp.where(gmask, jnp.exp(jnp.minimum(logits - mg, 0.0)), 0.0), axis=-1, keepdims=True)
    gw = 1.0 / sg
    gi = jnp.min(jnp.where(gmask & (logits == mg), lane, big), axis=-1, keepdims=True)
    lo = ROUTE_LANE0 + MOE_EPG * gi
    emask = (lane >= lo) & (lane < lo + MOE_EPG)
    l1 = jnp.max(jnp.where(emask, logits, -big), axis=-1, keepdims=True)
    i1 = jnp.min(jnp.where(emask & (logits == l1), lane, big), axis=-1, keepdims=True)
    emask2 = emask & (lane != i1)
    l2 = jnp.max(jnp.where(emask2, logits, -big), axis=-1, keepdims=True)
    i2 = jnp.min(jnp.where(emask2 & (logits == l2), lane, big), axis=-1, keepdims=True)
    e21 = jnp.exp(l2 - l1)
    w1 = gw / (1.0 + e21)
    w2 = gw * e21 / (1.0 + e21)

    oh = jnp.where((lane == i1) | (lane == i2), 1.0, 0.0)
    tri = (lax.broadcasted_iota(jnp.int32, (tm, tm), 0) > lax.broadcasted_iota(jnp.int32, (tm, tm), 1))
    cum = jnp.dot(jnp.where(tri, 1.0, 0.0).astype(BF16), oh.astype(BF16), preferred_element_type=F32) + carry_ref[...]
    r1 = jnp.sum(jnp.where(lane == i1, cum, 0.0), axis=-1, keepdims=True)
    r2 = jnp.sum(jnp.where(lane == i2, cum, 0.0), axis=-1, keepdims=True)
    carry_ref[...] = carry_ref[...] + jnp.sum(oh, axis=0, keepdims=True)
    cnt_ref[...] = carry_ref[...]

    ri = jnp.where(lane == 0, i1 - ROUTE_LANE0, 0.0)
    ri = jnp.where(lane == 1, i2 - ROUTE_LANE0, ri)
    ri = jnp.where(lane == 2, w1, ri)
    ri = jnp.where(lane == 3, w2, ri)
    ri = jnp.where(lane == 4, r1, ri)
    ri = jnp.where(lane == 5, r2, ri)
    ri_ref[...] = ri


def _outproj(mixes, w_bf16, xa, g1, gain2, sh2, sc2, wr, br, t0):
    b, s, d = xa.shape
    nt = s // TM - t0
    so = nt * TM
    mod_idx = lambda bi, t: (jnp.where(t + t0 == 0, b, bi), 0, 0)
    tile = lambda wdt: pl.BlockSpec((None, TM, wdt), lambda bi, t: (bi, t + t0, 0))
    otile = lambda wdt: pl.BlockSpec((None, TM, wdt), lambda bi, t: (bi, t, 0))
    const = lambda shape: pl.BlockSpec(shape, lambda bi, t: (0,) * len(shape))
    in_specs = [tile(m.shape[2]) for m in mixes] + [
        const(w_bf16.shape), tile(d),
        pl.BlockSpec((None, 1, d), mod_idx), const((1, d)),
        pl.BlockSpec((None, 1, d), mod_idx), pl.BlockSpec((None, 1, d), mod_idx),
        const((d, LANES)), const((1, LANES)),
    ]
    return pl.pallas_call(
        functools.partial(_outproj_kernel, n_mix=len(mixes)),
        grid=(b, nt),
        in_specs=in_specs,
        out_specs=[otile(d), otile(d), otile(LANES), const((1, LANES))],
        out_shape=[jax.ShapeDtypeStruct((b, so, d), F32), jax.ShapeDtypeStruct((b, so, d), F32),
                   jax.ShapeDtypeStruct((b, so, LANES), F32), jax.ShapeDtypeStruct((1, LANES), F32)],
        scratch_shapes=[pltpu.VMEM((1, LANES), F32)],
        compiler_params=_cparams(("arbitrary", "arbitrary")),
        name="outproj",
    )(*mixes, w_bf16, xa, g1, gain2, sh2, sc2, wr, br)


def _dispatch_kernel(dest_ref, h_ref, xs_in_ref, xs_ref, hbuf, sem):
    del xs_in_ref
    nt = pl.num_programs(1)
    step = pl.program_id(0) * nt + pl.program_id(1)
    last = pl.num_programs(0) * nt - 1
    slot = step % 2
    hbuf[slot] = h_ref[...]

    def copy(sl, i, dst_row):
        return pltpu.make_async_copy(hbuf.at[sl, pl.ds(i, 1)], xs_ref.at[pl.ds(dst_row, 1)], sem.at[sl])

    for i in range(TM):
        copy(slot, i, dest_ref[step, i]).start()
        copy(slot, i, dest_ref[step, TM + i]).start()

    def drain(sl):
        def one(i, c):
            copy(sl, 0, 0).wait()
            return c
        lax.fori_loop(0, 2 * TM, one, 0, unroll=8)

    @pl.when(step > 0)
    def _():
        drain(1 - slot)

    @pl.when(step == last)
    def _():
        drain(slot)


def _dispatch(dest, h2, n_pad):
    b, s, d = h2.shape
    nt = s // TM
    xs0 = jnp.zeros((n_pad, d), F32)
    return pl.pallas_call(
        _dispatch_kernel,
        grid_spec=pltpu.PrefetchScalarGridSpec(
            num_scalar_prefetch=1,
            grid=(b, nt),
            in_specs=[pl.BlockSpec((None, TM, d), lambda bi, t, dr: (bi, t, 0)),
                      pl.BlockSpec(memory_space=pl.ANY)],
            out_specs=pl.BlockSpec(memory_space=pl.ANY),
            scratch_shapes=[pltpu.VMEM((2, TM, d), F32), pltpu.SemaphoreType.DMA((2,))],
        ),
        out_shape=jax.ShapeDtypeStruct((n_pad, d), F32),
        input_output_aliases={2: 0},
        compiler_params=_cparams(("arbitrary", "arbitrary")),
        name="dispatch",
    )(dest, h2, xs0)


def _mlp_kernel(be_ref, nu_ref, x_ref, wgu_ref, wd_ref, y_ref, wgu_b, wd_b):
    i = pl.program_id(0)
    prev = be_ref[jnp.maximum(i - 1, 0)]
    used = i < nu_ref[0]

    @pl.when(used & ((i == 0) | (be_ref[i] != prev)))
    def _():
        wgu_b[...] = wgu_ref[...].astype(BF16)
        wd_b[...] = wd_ref[...].astype(BF16)

    @pl.when(used)
    def _():
        gu = jnp.dot(x_ref[...].astype(BF16), wgu_b[...], preferred_element_type=F32)
        act = _silu(gu[:, :MOE_FF]) * gu[:, MOE_FF:]
        y_ref[...] = jnp.dot(act.astype(BF16), wd_b[...], preferred_element_type=F32)

    @pl.when(jnp.logical_not(used))
    def _():
        y_ref[...] = jnp.zeros_like(y_ref)


def _expert_mlp(block_e, n_used, xs, w_gu, w_down, layer):
    n_pad, d = xs.shape
    ff2 = w_gu.shape[-1]
    return pl.pallas_call(
        _mlp_kernel,
        grid_spec=pltpu.PrefetchScalarGridSpec(
            num_scalar_prefetch=2,
            grid=(n_pad // MOE_ROWS,),
            in_specs=[
                pl.BlockSpec((MOE_ROWS, d), lambda i, be, nu: (i, 0)),
                pl.BlockSpec((None, None, d, ff2), lambda i, be, nu: (layer, be[i], 0, 0)),
                pl.BlockSpec((None, None, ff2 // 2, d), lambda i, be, nu: (layer, be[i], 0, 0)),
            ],
            out_specs=pl.BlockSpec((MOE_ROWS, d), lambda i, be, nu: (i, 0)),
            scratch_shapes=[pltpu.VMEM((d, ff2), BF16), pltpu.VMEM((ff2 // 2, d), BF16)],
        ),
        out_shape=jax.ShapeDtypeStruct((n_pad, d), F32),
        compiler_params=_cparams(("arbitrary",)),
        name="expert_mlp",
    )(block_e, n_used, xs, w_gu, w_down)


def _combine_kernel(dest_ref, ys_ref, x_ref, ri_ref, g2_ref, fg_ref, o_ref, buf, sem, *, final):
    bi = pl.program_id(0)
    t = pl.program_id(1)
    nt = pl.num_programs(1)
    step = bi * nt + t
    total = pl.num_programs(0) * nt

    def copy(src_row, slot, k, i):
        return pltpu.make_async_copy(ys_ref.at[pl.ds(src_row, 1)], buf.at[slot, k, pl.ds(i, 1)], sem.at[slot])

    def issue(st, slot):
        for i in range(TM):
            copy(dest_ref[st, i], slot, 0, i).start()
            copy(dest_ref[st, TM + i], slot, 1, i).start()

    slot = step % 2

    @pl.when(step == 0)
    def _():
        issue(0, 0)

    @pl.when(step + 1 < total)
    def _():
        issue(step + 1, 1 - slot)

    def drain(i, c):
        copy(0, slot, 0, 0).wait()
        return c

    lax.fori_loop(0, 2 * TM, drain, 0, unroll=8)

    lane = lax.broadcasted_iota(jnp.int32, (TM, LANES), 1)
    ri = ri_ref[...]
    w1 = jnp.sum(jnp.where(lane == 2, ri, 0.0), axis=-1, keepdims=True)
    w2 = jnp.sum(jnp.where(lane == 3, ri, 0.0), axis=-1, keepdims=True)
    y = buf[slot, 0] * w1 + buf[slot, 1] * w2
    xn = x_ref[...] + g2_ref[...] * y
    if final:
        ms = jnp.mean(xn * xn, axis=-1, keepdims=True)
        xn = xn * lax.rsqrt(ms + NORM_EPS) * fg_ref[...]
    o_ref[...] = xn


def _combine(dest, ys, xa, rinfo, g2, final_g, has_ctx, final):
    b, s, d = xa.shape
    nt = s // TM
    mod_idx = lambda bi, t, dr: (jnp.where(t == 0, b, bi) if has_ctx else bi, 0, 0)
    tile = lambda wdt: pl.BlockSpec((None, TM, wdt), lambda bi, t, dr: (bi, t, 0))
    out_spec = tile(d)
    out_shape = jax.ShapeDtypeStruct((b, s, d), F32)
    return pl.pallas_call(
        functools.partial(_combine_kernel, final=final),
        grid_spec=pltpu.PrefetchScalarGridSpec(
            num_scalar_prefetch=1,
            grid=(b, nt),
            in_specs=[
                pl.BlockSpec(memory_space=pl.ANY),
                tile(d), tile(LANES),
                pl.BlockSpec((None, 1, d), mod_idx),
                pl.BlockSpec((1, d), lambda bi, t, dr: (0, 0)),
            ],
            out_specs=out_spec,
            scratch_shapes=[pltpu.VMEM((2, 2, TM, d), F32), pltpu.SemaphoreType.DMA((2,))],
        ),
        out_shape=out_shape,
        compiler_params=_cparams(("arbitrary", "arbitrary")),
        name="combine",
    )(dest, ys, xa, rinfo, g2, final_g)


def _moe(h2, rinfo, counts, w_gu, w_down, layer):
    b, s, d = h2.shape
    nt = s // TM
    r = rinfo[:, :, :8]
    e = r[..., 0:2].astype(jnp.int32)
    rank = r[..., 4:6].astype(jnp.int32)
    cnt = counts[0, ROUTE_LANE0:ROUTE_LANE0 + MOE_EXPERTS].astype(jnp.int32)
    padded = (cnt + MOE_ROWS - 1) // MOE_ROWS * MOE_ROWS
    ends = jnp.cumsum(padded)
    starts = ends - padded
    eids = jnp.arange(MOE_EXPERTS, dtype=jnp.int32)
    dest = jnp.sum(jnp.where(e[..., None] == eids, starts, 0), axis=-1) + rank
    dest = dest.reshape(b, nt, TM, 2).transpose(0, 1, 3, 2).reshape(b * nt, 2 * TM)
    n_assign = b * nt * TM * 2
    n_blocks = (n_assign + MOE_EXPERTS * (MOE_ROWS - 1)) // MOE_ROWS + 1
    n_pad = n_blocks * MOE_ROWS
    blk_row = jnp.arange(n_blocks, dtype=jnp.int32) * MOE_ROWS
    block_e = jnp.minimum(jnp.sum((blk_row[:, None] >= ends[None, :]).astype(jnp.int32), axis=1),
                          MOE_EXPERTS - 1)
    n_used = (ends[-1] // MOE_ROWS).astype(jnp.int32).reshape(1)
    xs = _dispatch(dest, h2, n_pad)
    ys = _expert_mlp(block_e, n_used, xs, w_gu, w_down, layer)
    return dest, ys


def _router_weights(wg, bg, we, be):
    d = wg.shape[0]
    pad = LANES - MOE_GROUPS - MOE_EXPERTS
    assert ROUTE_LANE0 == MOE_GROUPS
    wr = jnp.concatenate([wg.astype(F32), we.astype(F32), jnp.zeros((d, pad), F32)], axis=1)
    br = jnp.concatenate([bg.astype(F32), be.astype(F32), jnp.zeros((pad,), F32)]).reshape(1, LANES)
    return wr, br


def kernel(x, c, ctx, c_ctx, ada_w, ada_b, norm_g, final_g, ab_w_in, ab_w_out, ret_decay, ret_gn, na_rpb,
           swa_w_in, swa_w_out, swa_sink, router_g_w, router_g_b, router_e_w, router_e_b,
           expert_w_gu, expert_w_down):
    b, seq, d = x.shape
    assert ctx.shape[1] == TM and seq % TM == 0 and d == D_MODEL
    xa = jnp.concatenate([ctx, x], axis=1)
    rope = _rope_tables(seq)

    cvec = jnp.concatenate([c, c_ctx[None, :], jnp.zeros((7, d), F32)], axis=0)
    mod = _adaln(cvec, ada_w, ada_b)
    mod = mod.reshape(DEPTH, b + 8, 6, 1, d)[:, :b + 1].transpose(0, 2, 1, 3, 4)

    for layer in range(DEPTH):
        last = layer == DEPTH - 1
        sh1, sc1, g1, sh2, sc2, g2 = (mod[layer, i] for i in range(6))
        gain1 = norm_g[layer, 0].reshape(1, d)
        gain2 = norm_g[layer, 1].reshape(1, d)
        j = layer // 2
        if layer % 2 == 0:
            nb = RET_W // LANES
            p = _proj(xa, gain1, sh1, sc1, ab_w_in[j].astype(BF16), rope,
                      rope_blocks=range(0, 2 * nb),
                      scale_blocks=list(range(nb, 2 * nb)) + list(range(4 * nb, 5 * nb)))
            log_gamma = jnp.log1p(-jnp.exp2(-ret_decay[j].astype(F32)))
            mixes = [_retention(p, log_gamma, ret_gn[j]), _neighbourhood(p, _na_bias(na_rpb[j]))]
            w_out = ab_w_out[j]
        else:
            qb = SWA_HEADS * HEAD_DIM // LANES
            kb = SWA_KV_HEADS * HEAD_DIM // LANES
            p = _proj(xa, gain1, sh1, sc1, swa_w_in[j].astype(BF16), rope,
                      rope_blocks=range(0, qb + kb), scale_blocks=range(0, qb))
            mixes = [_swa(p, swa_sink[j].astype(F32))]
            w_out = swa_w_out[j]
        t0 = 1 if last else 0
        wr, br = _router_weights(router_g_w[layer], router_g_b[layer], router_e_w[layer], router_e_b[layer])
        xa, h2, rinfo, counts = _outproj(mixes, w_out.astype(BF16), xa, g1, gain2, sh2, sc2, wr, br, t0)
        dest, ys = _moe(h2, rinfo, counts, expert_w_gu, expert_w_down, layer)
        xa = _combine(dest, ys, xa, rinfo, g2, final_g.reshape(1, d), not last, last)
    return xa
```

```python
import functools

import numpy as np
import jax
import jax.numpy as jnp
from jax import lax
from jax.experimental import pallas as pl
from jax.experimental.pallas import tpu as pltpu

F32 = jnp.float32
BF16 = jnp.bfloat16

D_MODEL = 1024
DEPTH = 2
GRID_W = 64
HEAD_DIM = 64
RET_HEADS = 8
NA_HEADS = 8
RET_W = 512
NA_W = 512
AB_IN = 4 * RET_W + 3 * NA_W
RET_CHUNK = 128
GN_EPS = 1e-5
NA_KH = 8
NA_KW = 16
SWA_HEADS = 16
SWA_KV_HEADS = 4
SWA_WINDOW = 128
SWA_BLOCK = 128
SWA_IN = (SWA_HEADS + 2 * SWA_KV_HEADS) * HEAD_DIM
ROPE_BASE = 10000.0
MOE_GROUPS = 4
MOE_EPG = 8
MOE_EXPERTS = 32
MOE_FF = 512
NORM_EPS = 1e-6
NEG_INF = -1e30

LANES = 128
TM = 256
MOE_ROWS = 256
ROUTE_LANE0 = 4
VMEM_LIMIT = 56 * 1024 * 1024


def _cparams(sem, vmem=VMEM_LIMIT):
    return pltpu.CompilerParams(dimension_semantics=sem, vmem_limit_bytes=vmem)


def _split_bf16(a):
    hi = a.astype(BF16)
    lo = (a - hi.astype(F32)).astype(BF16)
    return hi, lo


def _dot3(a, b):
    ah, al = _split_bf16(a)
    bh, bl = _split_bf16(b)
    d = lambda x, y: jnp.dot(x, y, preferred_element_type=F32)
    return d(ah, bh) + (d(ah, bl) + d(al, bh))


def _dot_nt(a, b):
    return lax.dot_general(a, b, (((1,), (1,)), ((), ())), preferred_element_type=F32)


def _dot_tn(a, b):
    return lax.dot_general(a, b, (((0,), (0,)), ((), ())), preferred_element_type=F32)


def _silu(x):
    return x / (1.0 + jnp.exp(-x))


def _adaln_kernel(c_ref, w_ref, b_ref, o_ref):
    o_ref[...] = _dot3(_silu(c_ref[...]), w_ref[...]) + b_ref[...]


def _adaln(cvec, ada_w, ada_b):
    depth, d, n6 = ada_w.shape
    rows = cvec.shape[0]
    tn = 1024
    return pl.pallas_call(
        _adaln_kernel,
        grid=(depth, n6 // tn),
        in_specs=[
            pl.BlockSpec((rows, d), lambda l, j: (0, 0)),
            pl.BlockSpec((None, d, tn), lambda l, j: (l, 0, j)),
            pl.BlockSpec((None, 1, tn), lambda l, j: (l, 0, j)),
        ],
        out_specs=pl.BlockSpec((None, rows, tn), lambda l, j: (l, 0, j)),
        out_shape=jax.ShapeDtypeStruct((depth, rows, n6), F32),
        compiler_params=_cparams(("arbitrary", "arbitrary")),
        name="adaln",
    )(cvec, ada_w, ada_b.reshape(depth, 1, n6))


def _rms_mod(x, g, sh, sc):
    ms = jnp.mean(x * x, axis=-1, keepdims=True)
    return (x * lax.rsqrt(ms + NORM_EPS) * g) * (1.0 + sc) + sh


def _proj_kernel(x_ref, g_ref, sh_ref, sc_ref, w_ref, rope_ref, o_ref, *, rope_blocks, scales, cn):
    is_lat = pl.program_id(1) > 0
    hb = _rms_mod(x_ref[...], g_ref[...], sh_ref[...], sc_ref[...]).astype(BF16)
    nout = w_ref.shape[1]
    for c in range(nout // cn):
        o = jnp.dot(hb, w_ref[:, c * cn:(c + 1) * cn], preferred_element_type=F32)
        for s in range(cn // LANES):
            blk = c * (cn // LANES) + s
            ob = o[:, s * LANES:(s + 1) * LANES]
            if blk in rope_blocks:
                r = (ob * rope_ref[0] + pltpu.roll(ob, 16, 1) * rope_ref[1]
                     + pltpu.roll(ob, LANES - 16, 1) * rope_ref[2])
                ob = jnp.where(is_lat, r, ob)
            if blk in scales:
                ob = ob * scales[blk]
            o_ref[:, blk * LANES:(blk + 1) * LANES] = ob


def _proj(xa, gain, sh, sc, w_bf16, rope, rope_blocks, scales):
    b, s, d = xa.shape
    nout = w_bf16.shape[1]
    nt = s // TM
    mod_idx = lambda bi, t: (jnp.where(t == 0, b, bi), 0, 0)
    kern = functools.partial(_proj_kernel, rope_blocks=frozenset(rope_blocks), scales=dict(scales), cn=512)
    return pl.pallas_call(
        kern,
        grid=(b, nt),
        in_specs=[
            pl.BlockSpec((None, TM, d), lambda bi, t: (bi, t, 0)),
            pl.BlockSpec((1, d), lambda bi, t: (0, 0)),
            pl.BlockSpec((None, 1, d), mod_idx),
            pl.BlockSpec((None, 1, d), mod_idx),
            pl.BlockSpec((d, nout), lambda bi, t: (0, 0)),
            pl.BlockSpec((3, TM, LANES), lambda bi, t: (0, jnp.maximum(t - 1, 0), 0)),
        ],
        out_specs=pl.BlockSpec((None, TM, nout), lambda bi, t: (bi, t, 0)),
        out_shape=jax.ShapeDtypeStruct((b, s, nout), F32),
        compiler_params=_cparams(("arbitrary", "arbitrary")),
        name="proj",
    )(xa, gain, sh, sc, w_bf16, rope)


def _rope_tables(seq):
    nf = HEAD_DIM // 4
    inv = ROPE_BASE ** (-jnp.arange(nf, dtype=F32) / nf)
    t = jnp.arange(seq)
    row = (t // GRID_W).astype(F32)
    col = (t % GRID_W).astype(F32)
    lane = np.arange(LANES)
    jj = lane % HEAD_DIM
    axis_is_col = (jj // 32) == 1
    second_half = (jj % 32) >= 16
    f = jj % 16
    pos = jnp.where(axis_is_col[None, :], col[:, None], row[:, None])
    ang = pos * inv[f][None, :]
    c, s = jnp.cos(ang), jnp.sin(ang)
    sa = jnp.where(second_half[None, :], s, 0.0)
    sb = jnp.where(second_half[None, :], 0.0, -s)
    return jnp.stack([c, sa, sb], axis=0)


def _ret_kernel(lg_ref, q_ref, k_ref, v_ref, g_ref, gn_ref, o_ref, accf_ref, accb_ref,
                intra_ref, qd_ref, kd_ref):
    hp = pl.program_id(1)
    c = RET_CHUNK
    s_len = q_ref.shape[0]
    n_chunks = s_len // c
    ctx_chunks = TM // c
    pos = lax.broadcasted_iota(jnp.int32, (c, LANES), 0).astype(F32)
    lane = lax.broadcasted_iota(jnp.int32, (c, LANES), 1)
    row = lax.broadcasted_iota(jnp.int32, (c, LANES), 0)
    m0 = lane < HEAD_DIM
    same_head = (row < HEAD_DIM) == m0
    rel = (row - lane).astype(F32)
    lgf = [lg_ref[0, hp * 2 + hh] for hh in range(2)]
    lgb = [lg_ref[1, hp * 2 + hh] for hh in range(2)]
    lgf_l = jnp.where(m0, lgf[0], lgf[1])
    lgb_l = jnp.where(m0, lgb[0], lgb[1])
    for hh in range(2):
        intra_ref[0, :, hh * c:(hh + 1) * c] = jnp.where(rel >= 0, jnp.exp(lgf[hh] * jnp.maximum(rel, 0.0)), 0.0)
        intra_ref[1, :, hh * c:(hh + 1) * c] = jnp.where(rel <= 0, jnp.exp(lgb[hh] * jnp.maximum(-rel, 0.0)), 0.0)
    qd_ref[0] = jnp.exp(lgf_l * (pos + 1.0))
    qd_ref[1] = jnp.exp(lgb_l * (c - pos))
    kd_ref[0] = jnp.exp(lgf_l * (c - 1.0 - pos))
    kd_ref[1] = jnp.exp(lgb_l * pos)
    cd = [jnp.exp(lgf_l[0:1] * float(c)), jnp.exp(lgb_l[0:1] * float(c))]

    def chunk(r0, state, d):
        q = q_ref[pl.ds(r0, c), :]
        k = k_ref[pl.ds(r0, c), :]
        v = v_ref[pl.ds(r0, c), :]
        qb = q.astype(BF16)
        kcat = jnp.concatenate([jnp.where(m0, k, 0.0), jnp.where(m0, 0.0, k)], axis=0).astype(BF16)
        vcat = jnp.concatenate([jnp.where(m0, v, 0.0), jnp.where(m0, 0.0, v)], axis=0).astype(BF16)
        s = _dot_nt(qb, kcat) * intra_ref[d]
        out = jnp.dot(s.astype(BF16), vcat, preferred_element_type=F32)
        out = out + jnp.dot(qb, state.astype(BF16), preferred_element_type=F32) * qd_ref[d]
        kv = _dot_tn((k * kd_ref[d]).astype(BF16), v.astype(BF16))
        return out, state * cd[d] + jnp.where(same_head, kv, 0.0)

    def body(i, states):
        sf, sb = states
        rf = pl.multiple_of(i * c, c)
        ib = jnp.where(i < ctx_chunks, ctx_chunks - 1 - i, n_chunks + ctx_chunks - 1 - i)
        rb = pl.multiple_of(ib * c, c)
        of, sf = chunk(rf, sf, 0)
        ob, sb = chunk(rb, sb, 1)
        accf_ref[pl.ds(rf, c), :] = of
        accb_ref[pl.ds(rb, c), :] = ob
        return sf, sb

    z = jnp.zeros((LANES, LANES), F32)
    lax.fori_loop(0, n_chunks, body, (z, z), unroll=2)

    avg = jnp.where(same_head, 1.0 / HEAD_DIM, 0.0).astype(BF16)

    def head_mean(x):
        hi, lo = _split_bf16(x)
        return (jnp.dot(hi, avg, preferred_element_type=F32) + jnp.dot(lo, avg, preferred_element_type=F32))

    def readout(i, carry):
        r0 = pl.multiple_of(i * c, c)
        o = accf_ref[pl.ds(r0, c), :] + accb_ref[pl.ds(r0, c), :]
        dlt = o - head_mean(o)
        var = head_mean(dlt * dlt)
        y = dlt * lax.rsqrt(var + GN_EPS) * gn_ref[...]
        o_ref[pl.ds(r0, c), :] = _silu(g_ref[pl.ds(r0, c), :]) * y
        return carry

    lax.fori_loop(0, n_chunks, readout, 0, unroll=2)


def _retention(p, log_gamma, ret_gn):
    b, s, _ = p.shape
    nb = RET_W // LANES
    blk = lambda off: pl.BlockSpec((None, s, LANES), lambda bi, hp: (bi, 0, off + hp))
    return pl.pallas_call(
        _ret_kernel,
        grid=(b, nb),
        in_specs=[
            pl.BlockSpec(memory_space=pltpu.SMEM),
            blk(0), blk(nb), blk(2 * nb), blk(3 * nb),
            pl.BlockSpec((1, LANES), lambda bi, hp: (0, hp)),
        ],
        out_specs=pl.BlockSpec((None, s, LANES), lambda bi, hp: (bi, 0, hp)),
        out_shape=jax.ShapeDtypeStruct((b, s, RET_W), F32),
        scratch_shapes=[pltpu.VMEM((s, LANES), F32), pltpu.VMEM((s, LANES), F32),
                        pltpu.VMEM((2, RET_CHUNK, 2 * RET_CHUNK), F32),
                        pltpu.VMEM((2, RET_CHUNK, LANES), F32), pltpu.VMEM((2, RET_CHUNK, LANES), F32)],
        compiler_params=_cparams(("arbitrary", "arbitrary")),
        name="retention",
    )(log_gamma, p, p, p, p, ret_gn.reshape(1, RET_W))


LOG2E = 1.4426950408889634


def _softmax_pv(s_list, v_list, extra=None):
    m = None
    for s in s_list:
        for j in range(s.shape[1] // LANES):
            blk = s[:, j * LANES:(j + 1) * LANES]
            m = blk if m is None else jnp.maximum(m, blk)
    m = m.max(axis=-1, keepdims=True)
    if extra is not None:
        m = jnp.maximum(m, extra)
    acc = None
    for s, v in zip(s_list, v_list):
        pv = jnp.dot(jnp.exp2(s - m).astype(BF16), v, preferred_element_type=F32)
        acc = pv if acc is None else acc + pv
    o, den = acc[:, :LANES], acc[:, LANES:]
    if extra is not None:
        den = den + jnp.exp2(extra - m)
    return o / den


def _stage_heads(q_ref, k_ref, v_ref, qm_ref, kb_ref, vb_ref):
    m0 = lax.broadcasted_iota(jnp.int32, (TM, LANES), 1) < HEAD_DIM

    def stage(i, carry):
        r0 = pl.multiple_of(i * TM, TM)
        q = q_ref[pl.ds(r0, TM), :]
        qm_ref[0, pl.ds(r0, TM), :] = jnp.where(m0, q, 0.0).astype(BF16)
        qm_ref[1, pl.ds(r0, TM), :] = jnp.where(m0, 0.0, q).astype(BF16)
        kb_ref[pl.ds(r0, TM), :] = k_ref[pl.ds(r0, TM), :].astype(BF16)
        vb_ref[pl.ds(r0, TM), 0:LANES] = v_ref[pl.ds(r0, TM), :].astype(BF16)
        vb_ref[pl.ds(r0, TM), LANES:2 * LANES] = jnp.ones((TM, LANES), BF16)
        return carry

    lax.fori_loop(0, q_ref.shape[0] // TM, stage, 0)


def _na_kernel(q_ref, k_ref, v_ref, bias_ref, o_ref, qm_ref, kb_ref, vb_ref):
    s_len = q_ref.shape[0]
    rows = (s_len - TM) // GRID_W
    nloc = NA_KH * GRID_W
    _stage_heads(q_ref, k_ref, v_ref, qm_ref, kb_ref, vb_ref)

    kc = kb_ref[0:TM, :]
    vc = vb_ref[0:TM, :]
    outs = [_softmax_pv([_dot_nt(qm_ref[hh, 0:TM, :], kc)], [vc]) for hh in range(2)]
    m0c = lax.broadcasted_iota(jnp.int32, (TM, LANES), 1) < HEAD_DIM
    o_ref[0:TM, :] = jnp.where(m0c, outs[0], outs[1])

    m0 = lax.broadcasted_iota(jnp.int32, (GRID_W, LANES), 1) < HEAD_DIM

    def row_block(r, carry):
        rs = jnp.clip(r - NA_KH // 2, 0, rows - NA_KH)
        pat = r - rs
        q0 = pl.multiple_of(TM + r * GRID_W, GRID_W)
        k0 = pl.multiple_of(TM + rs * GRID_W, GRID_W)
        kl = kb_ref[pl.ds(k0, nloc), :]
        vl = vb_ref[pl.ds(k0, nloc), :]
        kc = kb_ref[0:TM, :]
        vc = vb_ref[0:TM, :]
        q = jnp.concatenate([qm_ref[0, pl.ds(q0, GRID_W), :], qm_ref[1, pl.ds(q0, GRID_W), :]], axis=0)
        s_loc = _dot_nt(q, kl) + bias_ref[pat].reshape(2 * GRID_W, nloc)
        s_ctx = _dot_nt(q, kc)
        res = _softmax_pv([s_loc, s_ctx], [vl, vc])
        o_ref[pl.ds(q0, GRID_W), :] = jnp.where(m0, res[:GRID_W], res[GRID_W:])
        return carry

    lax.fori_loop(0, rows, row_block, 0, unroll=4)


def _na_bias(rpb):
    h = rpb.shape[0]
    qc = np.arange(GRID_W)[:, None]
    kc = np.arange(GRID_W)[None, :]
    win = np.clip(qc - NA_KW // 2, 0, GRID_W - NA_KW)
    valid = (kc >= win) & (kc < win + NA_KW)
    col_off = np.clip(kc - qc + NA_KW - 1, 0, 2 * NA_KW - 2)
    pat = np.arange(NA_KH)[:, None]
    row_off = np.arange(NA_KH)[None, :] - pat + NA_KH - 1
    bias = rpb[:, row_off][:, :, :, col_off]
    bias = jnp.where(valid[None, None, None], bias.astype(F32) * LOG2E, NEG_INF)
    bias = bias.transpose(1, 0, 3, 2, 4)
    return bias.reshape(NA_KH, h, GRID_W, NA_KH * GRID_W)


def _neighbourhood(p, bias):
    b, s, _ = p.shape
    nb = NA_W // LANES
    c0 = 4 * RET_W // LANES
    blk = lambda off: pl.BlockSpec((None, s, LANES), lambda bi, hp: (bi, 0, c0 + off + hp))
    return pl.pallas_call(
        _na_kernel,
        grid=(b, nb),
        in_specs=[
            blk(0), blk(nb), blk(2 * nb),
            pl.BlockSpec((NA_KH, 2, GRID_W, NA_KH * GRID_W), lambda bi, hp: (0, hp, 0, 0)),
        ],
        out_specs=pl.BlockSpec((None, s, LANES), lambda bi, hp: (bi, 0, hp)),
        out_shape=jax.ShapeDtypeStruct((b, s, NA_W), F32),
        scratch_shapes=[pltpu.VMEM((2, s, LANES), BF16), pltpu.VMEM((s, LANES), BF16),
                        pltpu.VMEM((s, 2 * LANES), BF16)],
        compiler_params=_cparams(("arbitrary", "arbitrary")),
        name="neighbourhood",
    )(p, p, p, bias)


def _swa_kernel(sink_ref, q_ref, k_ref, v_ref, o_ref, kd_ref, vd_ref):
    kp = pl.program_id(1)
    s_len = q_ref.shape[0]
    seq = s_len - TM
    nblk = seq // SWA_BLOCK
    band = SWA_BLOCK + 2 * SWA_WINDOW
    group = SWA_HEADS // SWA_KV_HEADS
    heads_per_step = 2 * group
    m0t = lax.broadcasted_iota(jnp.int32, (TM, LANES), 1) < HEAD_DIM

    def stage(i, carry):
        r0 = pl.multiple_of(i * TM, TM)
        for src, dst in ((k_ref, kd_ref), (v_ref, vd_ref)):
            x = src[pl.ds(r0, TM), :]
            xr = pltpu.roll(x, HEAD_DIM, 1)
            dst[0, pl.ds(r0, TM), 0:LANES] = jnp.where(m0t, x, xr).astype(BF16)
            dst[1, pl.ds(r0, TM), 0:LANES] = jnp.where(m0t, xr, x).astype(BF16)
        for h in range(2):
            vd_ref[h, pl.ds(r0, TM), LANES:2 * LANES] = jnp.ones((TM, LANES), BF16)
        return carry

    lax.fori_loop(0, s_len // TM, stage, 0)
    o_ref[0:TM, :] = jnp.zeros((TM, o_ref.shape[1]), F32)

    qi = lax.broadcasted_iota(jnp.int32, (2 * SWA_BLOCK, band), 0) % SWA_BLOCK
    ki = lax.broadcasted_iota(jnp.int32, (2 * SWA_BLOCK, band), 1)
    m0 = lax.broadcasted_iota(jnp.int32, (SWA_BLOCK, LANES), 1) < HEAD_DIM
    first_head = lax.broadcasted_iota(jnp.int32, (2 * SWA_BLOCK, 1), 0) < SWA_BLOCK

    def block(i, carry):
        start = jnp.clip((i - 1) * SWA_BLOCK, 0, seq - band)
        k0 = pl.multiple_of(TM + start, SWA_BLOCK)
        q0 = pl.multiple_of(TM + i * SWA_BLOCK, SWA_BLOCK)
        valid = jnp.abs(qi + (i * SWA_BLOCK - start) - ki) <= SWA_WINDOW
        for hh in range(2):
            kb = kd_ref[hh, pl.ds(k0, band), :]
            vb = vd_ref[hh, pl.ds(k0, band), :]
            kc = kd_ref[hh, 0:TM, :]
            vc = vd_ref[hh, 0:TM, :]
            for j in range(group // 2):
                pair = hh * (group // 2) + j
                qp = q_ref[pl.ds(q0, SWA_BLOCK), pair * LANES:(pair + 1) * LANES]
                q = jnp.concatenate([jnp.where(m0, qp, 0.0), jnp.where(m0, 0.0, qp)], axis=0).astype(BF16)
                s = jnp.where(valid, _dot_nt(q, kb), NEG_INF)
                s_ctx = _dot_nt(q, kc)
                h0 = kp * heads_per_step + 2 * pair
                sink = jnp.where(first_head, sink_ref[h0], sink_ref[h0 + 1]) * LOG2E
                res = _softmax_pv([s, s_ctx], [vb, vc], extra=sink)
                o_ref[pl.ds(q0, SWA_BLOCK), pair * LANES:(pair + 1) * LANES] = jnp.where(
                    m0, res[:SWA_BLOCK], res[SWA_BLOCK:])
        return carry

    lax.fori_loop(0, nblk, block, 0, unroll=2)


def _swa(p, sink):
    b, s, _ = p.shape
    qw = SWA_HEADS * HEAD_DIM // 2
    kblk = SWA_HEADS * HEAD_DIM // LANES
    vblk = kblk + SWA_KV_HEADS * HEAD_DIM // LANES
    return pl.pallas_call(
        _swa_kernel,
        grid=(b, 2),
        in_specs=[
            pl.BlockSpec(memory_space=pltpu.SMEM),
            pl.BlockSpec((None, s, qw), lambda bi, kp: (bi, 0, kp)),
            pl.BlockSpec((None, s, LANES), lambda bi, kp: (bi, 0, kblk + kp)),
            pl.BlockSpec((None, s, LANES), lambda bi, kp: (bi, 0, vblk + kp)),
        ],
        out_specs=pl.BlockSpec((None, s, qw), lambda bi, kp: (bi, 0, kp)),
        out_shape=jax.ShapeDtypeStruct((b, s, SWA_HEADS * HEAD_DIM), F32),
        scratch_shapes=[pltpu.VMEM((2, s, LANES), BF16), pltpu.VMEM((2, s, 2 * LANES), BF16)],
        compiler_params=_cparams(("arbitrary", "arbitrary")),
        name="swa",
    )(sink, p, p, p)


def _outproj_kernel(*refs, n_mix):
    mix_refs = refs[:n_mix]
    (w_ref, x_ref, g1_ref, gn_ref, sh_ref, sc_ref, wr_ref, br_ref,
     xo_ref, h_ref, ri_ref, cnt_ref, carry_ref) = refs[n_mix:]
    first = (pl.program_id(0) == 0) & (pl.program_id(1) == 0)

    @pl.when(first)
    def _():
        carry_ref[...] = jnp.zeros_like(carry_ref)

    o = None
    off = 0
    for m_ref in mix_refs:
        kw = m_ref.shape[1]
        part = jnp.dot(m_ref[...].astype(BF16), w_ref[off:off + kw, :], preferred_element_type=F32)
        o = part if o is None else o + part
        off += kw
    xn = x_ref[...] + g1_ref[...] * o
    xo_ref[...] = xn
    h = _rms_mod(xn, gn_ref[...], sh_ref[...], sc_ref[...])
    h_ref[...] = h

    logits = _dot3(h, wr_ref[...]) + br_ref[...]
    tm = logits.shape[0]
    lane = lax.broadcasted_iota(jnp.int32, (tm, LANES), 1).astype(F32)
    big = 1e9
    gmask = lane < MOE_GROUPS
    mg = jnp.max(jnp.where(gmask, logits, -big), axis=-1, keepdims=True)
    sg = jnp.sum(jnp.where(gmask, jnp.exp(jnp.minimum(logits - mg, 0.0)), 0.0), axis=-1, keepdims=True)
    gw = 1.0 / sg
    gi = jnp.min(jnp.where(gmask & (logits == mg), lane, big), axis=-1, keepdims=True)
    lo = ROUTE_LANE0 + MOE_EPG * gi
    emask = (lane >= lo) & (lane < lo + MOE_EPG)
    l1 = jnp.max(jnp.where(emask, logits, -big), axis=-1, keepdims=True)
    i1 = jnp.min(jnp.where(emask & (logits == l1), lane, big), axis=-1, keepdims=True)
    emask2 = emask & (lane != i1)
    l2 = jnp.max(jnp.where(emask2, logits, -big), axis=-1, keepdims=True)
    i2 = jnp.min(jnp.where(emask2 & (logits == l2), lane, big), axis=-1, keepdims=True)
    e21 = jnp.exp(l2 - l1)
    w1 = gw / (1.0 + e21)
    w2 = gw * e21 / (1.0 + e21)

    oh = jnp.where((lane == i1) | (lane == i2), 1.0, 0.0)
    tri = (lax.broadcasted_iota(jnp.int32, (tm, tm), 0) > lax.broadcasted_iota(jnp.int32, (tm, tm), 1))
    cum = jnp.dot(jnp.where(tri, 1.0, 0.0).astype(BF16), oh.astype(BF16), preferred_element_type=F32) + carry_ref[...]
    r1 = jnp.sum(jnp.where(lane == i1, cum, 0.0), axis=-1, keepdims=True)
    r2 = jnp.sum(jnp.where(lane == i2, cum, 0.0), axis=-1, keepdims=True)
    carry_ref[...] = carry_ref[...] + jnp.sum(oh, axis=0, keepdims=True)
    cnt_ref[...] = carry_ref[...]

    ri = jnp.where(lane == 0, i1 - ROUTE_LANE0, 0.0)
    ri = jnp.where(lane == 1, i2 - ROUTE_LANE0, ri)
    ri = jnp.where(lane == 2, w1, ri)
    ri = jnp.where(lane == 3, w2, ri)
    ri = jnp.where(lane == 4, r1, ri)
    ri = jnp.where(lane == 5, r2, ri)
    ri_ref[...] = ri


def _outproj(mixes, w_bf16, xa, g1, gain2, sh2, sc2, wr, br, t0):
    b, s, d = xa.shape
    nt = s // TM - t0
    so = nt * TM
    mod_idx = lambda bi, t: (jnp.where(t + t0 == 0, b, bi), 0, 0)
    tile = lambda wdt: pl.BlockSpec((None, TM, wdt), lambda bi, t: (bi, t + t0, 0))
    otile = lambda wdt: pl.BlockSpec((None, TM, wdt), lambda bi, t: (bi, t, 0))
    const = lambda shape: pl.BlockSpec(shape, lambda bi, t: (0,) * len(shape))
    in_specs = [tile(m.shape[2]) for m in mixes] + [
        const(w_bf16.shape), tile(d),
        pl.BlockSpec((None, 1, d), mod_idx), const((1, d)),
        pl.BlockSpec((None, 1, d), mod_idx), pl.BlockSpec((None, 1, d), mod_idx),
        const((d, LANES)), const((1, LANES)),
    ]
    return pl.pallas_call(
        functools.partial(_outproj_kernel, n_mix=len(mixes)),
        grid=(b, nt),
        in_specs=in_specs,
        out_specs=[otile(d), otile(d), otile(LANES), const((1, LANES))],
        out_shape=[jax.ShapeDtypeStruct((b, so, d), F32), jax.ShapeDtypeStruct((b, so, d), F32),
                   jax.ShapeDtypeStruct((b, so, LANES), F32), jax.ShapeDtypeStruct((1, LANES), F32)],
        scratch_shapes=[pltpu.VMEM((1, LANES), F32)],
        compiler_params=_cparams(("arbitrary", "arbitrary")),
        name="outproj",
    )(*mixes, w_bf16, xa, g1, gain2, sh2, sc2, wr, br)


def _dispatch_kernel(dest_ref, h_ref, xs_in_ref, xs_ref, hbuf, sem):
    del xs_in_ref
    nt = pl.num_programs(1)
    step = pl.program_id(0) * nt + pl.program_id(1)
    last = pl.num_programs(0) * nt - 1
    slot = step % 2
    hbuf[slot] = h_ref[...]

    def copy(sl, i, dst_row):
        return pltpu.make_async_copy(hbuf.at[sl, pl.ds(i, 1)], xs_ref.at[pl.ds(dst_row, 1)], sem.at[sl])

    for i in range(TM):
        copy(slot, i, dest_ref[step, i]).start()
        copy(slot, i, dest_ref[step, TM + i]).start()

    def drain(sl):
        def one(i, c):
            copy(sl, 0, 0).wait()
            return c
        lax.fori_loop(0, 2 * TM, one, 0, unroll=8)

    @pl.when(step > 0)
    def _():
        drain(1 - slot)

    @pl.when(step == last)
    def _():
        drain(slot)


def _dispatch(dest, h2, n_pad):
    b, s, d = h2.shape
    nt = s // TM
    xs0 = jnp.zeros((n_pad, d), F32)
    return pl.pallas_call(
        _dispatch_kernel,
        grid_spec=pltpu.PrefetchScalarGridSpec(
            num_scalar_prefetch=1,
            grid=(b, nt),
            in_specs=[pl.BlockSpec((None, TM, d), lambda bi, t, dr: (bi, t, 0)),
                      pl.BlockSpec(memory_space=pl.ANY)],
            out_specs=pl.BlockSpec(memory_space=pl.ANY),
            scratch_shapes=[pltpu.VMEM((2, TM, d), F32), pltpu.SemaphoreType.DMA((2,))],
        ),
        out_shape=jax.ShapeDtypeStruct((n_pad, d), F32),
        input_output_aliases={2: 0},
        compiler_params=_cparams(("arbitrary", "arbitrary")),
        name="dispatch",
    )(dest, h2, xs0)


def _mlp_kernel(be_ref, nu_ref, x_ref, wgu_ref, wd_ref, y_ref, wgu_b, wd_b):
    i = pl.program_id(0)
    prev = be_ref[jnp.maximum(i - 1, 0)]
    used = i < nu_ref[0]

    @pl.when(used & ((i == 0) | (be_ref[i] != prev)))
    def _():
        wgu_b[...] = wgu_ref[...].astype(BF16)
        wd_b[...] = wd_ref[...].astype(BF16)

    @pl.when(used)
    def _():
        gu = jnp.dot(x_ref[...].astype(BF16), wgu_b[...], preferred_element_type=F32)
        act = _silu(gu[:, :MOE_FF]) * gu[:, MOE_FF:]
        y_ref[...] = jnp.dot(act.astype(BF16), wd_b[...], preferred_element_type=F32)

    @pl.when(jnp.logical_not(used))
    def _():
        y_ref[...] = jnp.zeros_like(y_ref)


def _expert_mlp(block_e, n_used, xs, w_gu, w_down, layer):
    n_pad, d = xs.shape
    ff2 = w_gu.shape[-1]
    return pl.pallas_call(
        _mlp_kernel,
        grid_spec=pltpu.PrefetchScalarGridSpec(
            num_scalar_prefetch=2,
            grid=(n_pad // MOE_ROWS,),
            in_specs=[
                pl.BlockSpec((MOE_ROWS, d), lambda i, be, nu: (i, 0)),
                pl.BlockSpec((None, None, d, ff2), lambda i, be, nu: (layer, be[i], 0, 0)),
                pl.BlockSpec((None, None, ff2 // 2, d), lambda i, be, nu: (layer, be[i], 0, 0)),
            ],
            out_specs=pl.BlockSpec((MOE_ROWS, d), lambda i, be, nu: (i, 0)),
            scratch_shapes=[pltpu.VMEM((d, ff2), BF16), pltpu.VMEM((ff2 // 2, d), BF16)],
        ),
        out_shape=jax.ShapeDtypeStruct((n_pad, d), F32),
        compiler_params=_cparams(("arbitrary",)),
        name="expert_mlp",
    )(block_e, n_used, xs, w_gu, w_down)


def _combine_kernel(dest_ref, ys_ref, x_ref, ri_ref, g2_ref, fg_ref, o_ref, buf, sem, *, final):
    bi = pl.program_id(0)
    t = pl.program_id(1)
    nt = pl.num_programs(1)
    step = bi * nt + t
    total = pl.num_programs(0) * nt

    def copy(src_row, slot, k, i):
        return pltpu.make_async_copy(ys_ref.at[pl.ds(src_row, 1)], buf.at[slot, k, pl.ds(i, 1)], sem.at[slot])

    def issue(st, slot):
        for i in range(TM):
            copy(dest_ref[st, i], slot, 0, i).start()
            copy(dest_ref[st, TM + i], slot, 1, i).start()

    slot = step % 2

    @pl.when(step == 0)
    def _():
        issue(0, 0)

    @pl.when(step + 1 < total)
    def _():
        issue(step + 1, 1 - slot)

    def drain(i, c):
        copy(0, slot, 0, 0).wait()
        return c

    lax.fori_loop(0, 2 * TM, drain, 0, unroll=8)

    lane = lax.broadcasted_iota(jnp.int32, (TM, LANES), 1)
    ri = ri_ref[...]
    w1 = jnp.sum(jnp.where(lane == 2, ri, 0.0), axis=-1, keepdims=True)
    w2 = jnp.sum(jnp.where(lane == 3, ri, 0.0), axis=-1, keepdims=True)
    y = buf[slot, 0] * w1 + buf[slot, 1] * w2
    xn = x_ref[...] + g2_ref[...] * y
    if final:
        ms = jnp.mean(xn * xn, axis=-1, keepdims=True)
        xn = xn * lax.rsqrt(ms + NORM_EPS) * fg_ref[...]
    o_ref[...] = xn


def _combine(dest, ys, xa, rinfo, g2, final_g, has_ctx, final):
    b, s, d = xa.shape
    nt = s // TM
    mod_idx = lambda bi, t, dr: (jnp.where(t == 0, b, bi) if has_ctx else bi, 0, 0)
    tile = lambda wdt: pl.BlockSpec((None, TM, wdt), lambda bi, t, dr: (bi, t, 0))
    out_spec = tile(d)
    out_shape = jax.ShapeDtypeStruct((b, s, d), F32)
    return pl.pallas_call(
        functools.partial(_combine_kernel, final=final),
        grid_spec=pltpu.PrefetchScalarGridSpec(
            num_scalar_prefetch=1,
            grid=(b, nt),
            in_specs=[
                pl.BlockSpec(memory_space=pl.ANY),
                tile(d), tile(LANES),
                pl.BlockSpec((None, 1, d), mod_idx),
                pl.BlockSpec((1, d), lambda bi, t, dr: (0, 0)),
            ],
            out_specs=out_spec,
            scratch_shapes=[pltpu.VMEM((2, 2, TM, d), F32), pltpu.SemaphoreType.DMA((2,))],
        ),
        out_shape=out_shape,
        compiler_params=_cparams(("arbitrary", "arbitrary")),
        name="combine",
    )(dest, ys, xa, rinfo, g2, final_g)


def _moe(h2, rinfo, counts, w_gu, w_down, layer):
    b, s, d = h2.shape
    nt = s // TM
    r = rinfo[:, :, :8]
    e = r[..., 0:2].astype(jnp.int32)
    rank = r[..., 4:6].astype(jnp.int32)
    cnt = counts[0, ROUTE_LANE0:ROUTE_LANE0 + MOE_EXPERTS].astype(jnp.int32)
    padded = (cnt + MOE_ROWS - 1) // MOE_ROWS * MOE_ROWS
    ends = jnp.cumsum(padded)
    starts = ends - padded
    eids = jnp.arange(MOE_EXPERTS, dtype=jnp.int32)
    dest = jnp.sum(jnp.where(e[..., None] == eids, starts, 0), axis=-1) + rank
    dest = dest.reshape(b, nt, TM, 2).transpose(0, 1, 3, 2).reshape(b * nt, 2 * TM)
    n_assign = b * nt * TM * 2
    n_blocks = (n_assign + MOE_EXPERTS * (MOE_ROWS - 1)) // MOE_ROWS + 1
    n_pad = n_blocks * MOE_ROWS
    blk_row = jnp.arange(n_blocks, dtype=jnp.int32) * MOE_ROWS
    block_e = jnp.minimum(jnp.sum((blk_row[:, None] >= ends[None, :]).astype(jnp.int32), axis=1),
                          MOE_EXPERTS - 1)
    n_used = (ends[-1] // MOE_ROWS).astype(jnp.int32).reshape(1)
    xs = _dispatch(dest, h2, n_pad)
    ys = _expert_mlp(block_e, n_used, xs, w_gu, w_down, layer)
    return dest, ys


def _router_weights(wg, bg, we, be):
    d = wg.shape[0]
    pad = LANES - MOE_GROUPS - MOE_EXPERTS
    assert ROUTE_LANE0 == MOE_GROUPS
    wr = jnp.concatenate([wg.astype(F32), we.astype(F32), jnp.zeros((d, pad), F32)], axis=1)
    br = jnp.concatenate([bg.astype(F32), be.astype(F32), jnp.zeros((pad,), F32)]).reshape(1, LANES)
    return wr, br


def kernel(x, c, ctx, c_ctx, ada_w, ada_b, norm_g, final_g, ab_w_in, ab_w_out, ret_decay, ret_gn, na_rpb,
           swa_w_in, swa_w_out, swa_sink, router_g_w, router_g_b, router_e_w, router_e_b,
           expert_w_gu, expert_w_down):
    b, seq, d = x.shape
    assert ctx.shape[1] == TM and seq % TM == 0 and d == D_MODEL
    xa = jnp.concatenate([ctx, x], axis=1)
    rope = _rope_tables(seq)

    cvec = jnp.concatenate([c, c_ctx[None, :], jnp.zeros((7, d), F32)], axis=0)
    mod = _adaln(cvec, ada_w, ada_b)
    mod = mod.reshape(DEPTH, b + 8, 6, 1, d)[:, :b + 1].transpose(0, 2, 1, 3, 4)

    for layer in range(DEPTH):
        last = layer == DEPTH - 1
        sh1, sc1, g1, sh2, sc2, g2 = (mod[layer, i] for i in range(6))
        gain1 = norm_g[layer, 0].reshape(1, d)
        gain2 = norm_g[layer, 1].reshape(1, d)
        j = layer // 2
        if layer % 2 == 0:
            nb = RET_W // LANES
            qk_scale = HEAD_DIM ** -0.5
            scales = {blk: qk_scale for blk in range(nb, 2 * nb)}
            scales.update({blk: qk_scale * LOG2E for blk in range(4 * nb, 5 * nb)})
            p = _proj(xa, gain1, sh1, sc1, ab_w_in[j].astype(BF16), rope,
                      rope_blocks=range(0, 2 * nb), scales=scales)
            log_gamma = jnp.log1p(-jnp.exp2(-ret_decay[j].astype(F32)))
            mixes = [_retention(p, log_gamma, ret_gn[j]), _neighbourhood(p, _na_bias(na_rpb[j]))]
            w_out = ab_w_out[j]
        else:
            qb = SWA_HEADS * HEAD_DIM // LANES
            kb = SWA_KV_HEADS * HEAD_DIM // LANES
            p = _proj(xa, gain1, sh1, sc1, swa_w_in[j].astype(BF16), rope,
                      rope_blocks=range(0, qb + kb),
                      scales={blk: HEAD_DIM ** -0.5 * LOG2E for blk in range(qb)})
            mixes = [_swa(p, swa_sink[j].astype(F32))]
            w_out = swa_w_out[j]
        t0 = 1 if last else 0
        wr, br = _router_weights(router_g_w[layer], router_g_b[layer], router_e_w[layer], router_e_b[layer])
        xa, h2, rinfo, counts = _outproj(mixes, w_out.astype(BF16), xa, g1, gain2, sh2, sc2, wr, br, t0)
        dest, ys = _moe(h2, rinfo, counts, expert_w_gu, expert_w_down, layer)
        xa = _combine(dest, ys, xa, rinfo, g2, final_g.reshape(1, d), not last, last)
    return xa
```

```python
import functools

import numpy as np
import jax
import jax.numpy as jnp
from jax import lax
from jax.experimental import pallas as pl
from jax.experimental.pallas import tpu as pltpu

F32 = jnp.float32
BF16 = jnp.bfloat16

D_MODEL = 1024
DEPTH = 2
GRID_W = 64
HEAD_DIM = 64
RET_HEADS = 8
NA_HEADS = 8
RET_W = 512
NA_W = 512
AB_IN = 4 * RET_W + 3 * NA_W
RET_CHUNK = 128
GN_EPS = 1e-5
NA_KH = 8
NA_KW = 16
SWA_HEADS = 16
SWA_KV_HEADS = 4
SWA_WINDOW = 128
SWA_BLOCK = 128
SWA_IN = (SWA_HEADS + 2 * SWA_KV_HEADS) * HEAD_DIM
ROPE_BASE = 10000.0
MOE_GROUPS = 4
MOE_EPG = 8
MOE_EXPERTS = 32
MOE_FF = 512
NORM_EPS = 1e-6
NEG_INF = -1e30

LANES = 128
TM = 256
RET_BLOCK = 256
MOE_ROWS = 256
ROUTE_LANE0 = 4
VMEM_LIMIT = 56 * 1024 * 1024


def _cparams(sem, vmem=VMEM_LIMIT):
    return pltpu.CompilerParams(dimension_semantics=sem, vmem_limit_bytes=vmem)


def _split_bf16(a):
    hi = a.astype(BF16)
    lo = (a - hi.astype(F32)).astype(BF16)
    return hi, lo


def _dot3(a, b):
    ah, al = _split_bf16(a)
    bh, bl = _split_bf16(b)
    d = lambda x, y: jnp.dot(x, y, preferred_element_type=F32)
    return d(ah, bh) + (d(ah, bl) + d(al, bh))


def _dot_nt(a, b):
    return lax.dot_general(a, b, (((1,), (1,)), ((), ())), preferred_element_type=F32)


def _dot_tn(a, b):
    return lax.dot_general(a, b, (((0,), (0,)), ((), ())), preferred_element_type=F32)


def _silu(x):
    return x / (1.0 + jnp.exp(-x))


def _adaln_kernel(c_ref, w_ref, b_ref, o_ref):
    o_ref[...] = _dot3(_silu(c_ref[...]), w_ref[...]) + b_ref[...]


def _adaln(cvec, ada_w, ada_b):
    depth, d, n6 = ada_w.shape
    rows = cvec.shape[0]
    tn = 1024
    return pl.pallas_call(
        _adaln_kernel,
        grid=(depth, n6 // tn),
        in_specs=[
            pl.BlockSpec((rows, d), lambda l, j: (0, 0)),
            pl.BlockSpec((None, d, tn), lambda l, j: (l, 0, j)),
            pl.BlockSpec((None, 1, tn), lambda l, j: (l, 0, j)),
        ],
        out_specs=pl.BlockSpec((None, rows, tn), lambda l, j: (l, 0, j)),
        out_shape=jax.ShapeDtypeStruct((depth, rows, n6), F32),
        compiler_params=_cparams(("arbitrary", "arbitrary")),
        name="adaln",
    )(cvec, ada_w, ada_b.reshape(depth, 1, n6))


def _rms_mod(x, g, sh, sc):
    ms = jnp.mean(x * x, axis=-1, keepdims=True)
    return (x * lax.rsqrt(ms + NORM_EPS) * g) * (1.0 + sc) + sh


def _proj_kernel(x_ref, g_ref, sh_ref, sc_ref, w_ref, rope_ref, o_ref, *, rope_blocks, scales, cn):
    is_lat = pl.program_id(1) > 0
    hb = _rms_mod(x_ref[...], g_ref[...], sh_ref[...], sc_ref[...]).astype(BF16)
    nout = w_ref.shape[1]
    for c in range(nout // cn):
        o = jnp.dot(hb, w_ref[:, c * cn:(c + 1) * cn], preferred_element_type=F32)
        for s in range(cn // LANES):
            blk = c * (cn // LANES) + s
            ob = o[:, s * LANES:(s + 1) * LANES]
            if blk in rope_blocks:
                r = (ob * rope_ref[0] + pltpu.roll(ob, 16, 1) * rope_ref[1]
                     + pltpu.roll(ob, LANES - 16, 1) * rope_ref[2])
                ob = jnp.where(is_lat, r, ob)
            if blk in scales:
                ob = ob * scales[blk]
            o_ref[:, blk * LANES:(blk + 1) * LANES] = ob


def _proj(xa, gain, sh, sc, w_bf16, rope, rope_blocks, scales):
    b, s, d = xa.shape
    nout = w_bf16.shape[1]
    nt = s // TM
    mod_idx = lambda bi, t: (jnp.where(t == 0, b, bi), 0, 0)
    kern = functools.partial(_proj_kernel, rope_blocks=frozenset(rope_blocks), scales=dict(scales), cn=512)
    return pl.pallas_call(
        kern,
        grid=(b, nt),
        in_specs=[
            pl.BlockSpec((None, TM, d), lambda bi, t: (bi, t, 0)),
            pl.BlockSpec((1, d), lambda bi, t: (0, 0)),
            pl.BlockSpec((None, 1, d), mod_idx),
            pl.BlockSpec((None, 1, d), mod_idx),
            pl.BlockSpec((d, nout), lambda bi, t: (0, 0)),
            pl.BlockSpec((3, TM, LANES), lambda bi, t: (0, jnp.maximum(t - 1, 0), 0)),
        ],
        out_specs=pl.BlockSpec((None, TM, nout), lambda bi, t: (bi, t, 0)),
        out_shape=jax.ShapeDtypeStruct((b, s, nout), F32),
        compiler_params=_cparams(("arbitrary", "arbitrary")),
        name="proj",
    )(xa, gain, sh, sc, w_bf16, rope)


def _rope_tables(seq):
    nf = HEAD_DIM // 4
    inv = ROPE_BASE ** (-jnp.arange(nf, dtype=F32) / nf)
    t = jnp.arange(seq)
    row = (t // GRID_W).astype(F32)
    col = (t % GRID_W).astype(F32)
    lane = np.arange(LANES)
    jj = lane % HEAD_DIM
    axis_is_col = (jj // 32) == 1
    second_half = (jj % 32) >= 16
    f = jj % 16
    pos = jnp.where(axis_is_col[None, :], col[:, None], row[:, None])
    ang = pos * inv[f][None, :]
    c, s = jnp.cos(ang), jnp.sin(ang)
    sa = jnp.where(second_half[None, :], s, 0.0)
    sb = jnp.where(second_half[None, :], 0.0, -s)
    return jnp.stack([c, sa, sb], axis=0)


def _ret_kernel(lg_ref, q_ref, k_ref, v_ref, g_ref, gn_ref, o_ref, accf_ref, accb_ref,
                intra_ref, qd_ref, kd_ref):
    hp = pl.program_id(1)
    c = RET_BLOCK
    s_len = q_ref.shape[0]
    n_chunks = s_len // c
    ctx_chunks = TM // c
    pos = lax.broadcasted_iota(jnp.int32, (c, LANES), 0).astype(F32)
    m0 = lax.broadcasted_iota(jnp.int32, (c, LANES), 1) < HEAD_DIM
    same_head = ((lax.broadcasted_iota(jnp.int32, (LANES, LANES), 0) < HEAD_DIM)
                 == (lax.broadcasted_iota(jnp.int32, (LANES, LANES), 1) < HEAD_DIM))
    rel = (lax.broadcasted_iota(jnp.int32, (c, c), 0) - lax.broadcasted_iota(jnp.int32, (c, c), 1)).astype(F32)
    lgf = [lg_ref[0, hp * 2 + hh] for hh in range(2)]
    lgb = [lg_ref[1, hp * 2 + hh] for hh in range(2)]
    lgf_l = jnp.where(m0, lgf[0], lgf[1])
    lgb_l = jnp.where(m0, lgb[0], lgb[1])
    for hh in range(2):
        intra_ref[0, :, hh * c:(hh + 1) * c] = jnp.where(rel >= 0, jnp.exp(lgf[hh] * jnp.maximum(rel, 0.0)), 0.0)
        intra_ref[1, :, hh * c:(hh + 1) * c] = jnp.where(rel <= 0, jnp.exp(lgb[hh] * jnp.maximum(-rel, 0.0)), 0.0)
    qd_ref[0] = jnp.exp(lgf_l * (pos + 1.0))
    qd_ref[1] = jnp.exp(lgb_l * (c - pos))
    kd_ref[0] = jnp.exp(lgf_l * (c - 1.0 - pos))
    kd_ref[1] = jnp.exp(lgb_l * pos)
    cd = [jnp.exp(lgf_l[0:1] * float(c)), jnp.exp(lgb_l[0:1] * float(c))]

    def chunk(r0, state, d):
        q = q_ref[pl.ds(r0, c), :]
        k = k_ref[pl.ds(r0, c), :]
        v = v_ref[pl.ds(r0, c), :]
        qb = q.astype(BF16)
        kcat = jnp.concatenate([jnp.where(m0, k, 0.0), jnp.where(m0, 0.0, k)], axis=0).astype(BF16)
        vcat = jnp.concatenate([jnp.where(m0, v, 0.0), jnp.where(m0, 0.0, v)], axis=0).astype(BF16)
        s = _dot_nt(qb, kcat) * intra_ref[d]
        out = jnp.dot(s.astype(BF16), vcat, preferred_element_type=F32)
        out = out + jnp.dot(qb, state.astype(BF16), preferred_element_type=F32) * qd_ref[d]
        kv = _dot_tn((k * kd_ref[d]).astype(BF16), v.astype(BF16))
        return out, state * cd[d] + jnp.where(same_head, kv, 0.0)

    def body(i, states):
        sf, sb = states
        rf = pl.multiple_of(i * c, c)
        ib = jnp.where(i < ctx_chunks, ctx_chunks - 1 - i, n_chunks + ctx_chunks - 1 - i)
        rb = pl.multiple_of(ib * c, c)
        of, sf = chunk(rf, sf, 0)
        ob, sb = chunk(rb, sb, 1)
        accf_ref[pl.ds(rf, c), :] = of
        accb_ref[pl.ds(rb, c), :] = ob
        return sf, sb

    z = jnp.zeros((LANES, LANES), F32)
    lax.fori_loop(0, n_chunks, body, (z, z), unroll=3)

    avg = jnp.where(same_head, 1.0 / HEAD_DIM, 0.0).astype(BF16)

    def head_mean(x):
        hi, lo = _split_bf16(x)
        return (jnp.dot(hi, avg, preferred_element_type=F32) + jnp.dot(lo, avg, preferred_element_type=F32))

    def readout(i, carry):
        r0 = pl.multiple_of(i * c, c)
        o = accf_ref[pl.ds(r0, c), :] + accb_ref[pl.ds(r0, c), :]
        dlt = o - head_mean(o)
        var = head_mean(dlt * dlt)
        y = dlt * lax.rsqrt(var + GN_EPS) * gn_ref[...]
        o_ref[pl.ds(r0, c), :] = _silu(g_ref[pl.ds(r0, c), :]) * y
        return carry

    lax.fori_loop(0, n_chunks, readout, 0, unroll=2)


def _retention(p, log_gamma, ret_gn):
    b, s, _ = p.shape
    nb = RET_W // LANES
    blk = lambda off: pl.BlockSpec((None, s, LANES), lambda bi, hp: (bi, 0, off + hp))
    return pl.pallas_call(
        _ret_kernel,
        grid=(b, nb),
        in_specs=[
            pl.BlockSpec(memory_space=pltpu.SMEM),
            blk(0), blk(nb), blk(2 * nb), blk(3 * nb),
            pl.BlockSpec((1, LANES), lambda bi, hp: (0, hp)),
        ],
        out_specs=pl.BlockSpec((None, s, LANES), lambda bi, hp: (bi, 0, hp)),
        out_shape=jax.ShapeDtypeStruct((b, s, RET_W), F32),
        scratch_shapes=[pltpu.VMEM((s, LANES), F32), pltpu.VMEM((s, LANES), F32),
                        pltpu.VMEM((2, RET_BLOCK, 2 * RET_BLOCK), F32),
                        pltpu.VMEM((2, RET_BLOCK, LANES), F32), pltpu.VMEM((2, RET_BLOCK, LANES), F32)],
        compiler_params=_cparams(("arbitrary", "arbitrary")),
        name="retention",
    )(log_gamma, p, p, p, p, ret_gn.reshape(1, RET_W))


LOG2E = 1.4426950408889634


def _softmax_pv(s_list, v_list, extra=None):
    m = None
    for s in s_list:
        for j in range(s.shape[1] // LANES):
            blk = s[:, j * LANES:(j + 1) * LANES]
            m = blk if m is None else jnp.maximum(m, blk)
    m = m.max(axis=-1, keepdims=True)
    if extra is not None:
        m = jnp.maximum(m, extra)
    acc = None
    for s, v in zip(s_list, v_list):
        pv = jnp.dot(jnp.exp2(s - m).astype(BF16), v, preferred_element_type=F32)
        acc = pv if acc is None else acc + pv
    o, den = acc[:, :LANES], acc[:, LANES:]
    if extra is not None:
        den = den + jnp.exp2(extra - m)
    return o / den


def _stage_heads(q_ref, k_ref, v_ref, qm_ref, kb_ref, vb_ref):
    m0 = lax.broadcasted_iota(jnp.int32, (TM, LANES), 1) < HEAD_DIM

    def stage(i, carry):
        r0 = pl.multiple_of(i * TM, TM)
        q = q_ref[pl.ds(r0, TM), :]
        qm_ref[0, pl.ds(r0, TM), :] = jnp.where(m0, q, 0.0).astype(BF16)
        qm_ref[1, pl.ds(r0, TM), :] = jnp.where(m0, 0.0, q).astype(BF16)
        kb_ref[pl.ds(r0, TM), :] = k_ref[pl.ds(r0, TM), :].astype(BF16)
        vb_ref[pl.ds(r0, TM), 0:LANES] = v_ref[pl.ds(r0, TM), :].astype(BF16)
        vb_ref[pl.ds(r0, TM), LANES:2 * LANES] = jnp.ones((TM, LANES), BF16)
        return carry

    lax.fori_loop(0, q_ref.shape[0] // TM, stage, 0)


def _na_kernel(q_ref, k_ref, v_ref, bias_ref, o_ref, qm_ref, kb_ref, vb_ref):
    s_len = q_ref.shape[0]
    rows = (s_len - TM) // GRID_W
    nloc = NA_KH * GRID_W
    _stage_heads(q_ref, k_ref, v_ref, qm_ref, kb_ref, vb_ref)

    kc = kb_ref[0:TM, :]
    vc = vb_ref[0:TM, :]
    outs = [_softmax_pv([_dot_nt(qm_ref[hh, 0:TM, :], kc)], [vc]) for hh in range(2)]
    m0c = lax.broadcasted_iota(jnp.int32, (TM, LANES), 1) < HEAD_DIM
    o_ref[0:TM, :] = jnp.where(m0c, outs[0], outs[1])

    m0 = lax.broadcasted_iota(jnp.int32, (GRID_W, LANES), 1) < HEAD_DIM

    def row_block(r, carry):
        rs = jnp.clip(r - NA_KH // 2, 0, rows - NA_KH)
        pat = r - rs
        q0 = pl.multiple_of(TM + r * GRID_W, GRID_W)
        k0 = pl.multiple_of(TM + rs * GRID_W, GRID_W)
        kl = kb_ref[pl.ds(k0, nloc), :]
        vl = vb_ref[pl.ds(k0, nloc), :]
        kc = kb_ref[0:TM, :]
        vc = vb_ref[0:TM, :]
        q = jnp.concatenate([qm_ref[0, pl.ds(q0, GRID_W), :], qm_ref[1, pl.ds(q0, GRID_W), :]], axis=0)
        s_loc = _dot_nt(q, kl) + bias_ref[pat].reshape(2 * GRID_W, nloc)
        s_ctx = _dot_nt(q, kc)
        res = _softmax_pv([s_loc, s_ctx], [vl, vc])
        o_ref[pl.ds(q0, GRID_W), :] = jnp.where(m0, res[:GRID_W], res[GRID_W:])
        return carry

    lax.fori_loop(0, rows, row_block, 0, unroll=4)


def _na_bias(rpb):
    h = rpb.shape[0]
    qc = np.arange(GRID_W)[:, None]
    kc = np.arange(GRID_W)[None, :]
    win = np.clip(qc - NA_KW // 2, 0, GRID_W - NA_KW)
    valid = (kc >= win) & (kc < win + NA_KW)
    col_off = np.clip(kc - qc + NA_KW - 1, 0, 2 * NA_KW - 2)
    pat = np.arange(NA_KH)[:, None]
    row_off = np.arange(NA_KH)[None, :] - pat + NA_KH - 1
    bias = rpb[:, row_off][:, :, :, col_off]
    bias = jnp.where(valid[None, None, None], bias.astype(F32) * LOG2E, NEG_INF)
    bias = bias.transpose(1, 0, 3, 2, 4)
    return bias.reshape(NA_KH, h, GRID_W, NA_KH * GRID_W)


def _neighbourhood(p, bias):
    b, s, _ = p.shape
    nb = NA_W // LANES
    c0 = 4 * RET_W // LANES
    blk = lambda off: pl.BlockSpec((None, s, LANES), lambda bi, hp: (bi, 0, c0 + off + hp))
    return pl.pallas_call(
        _na_kernel,
        grid=(b, nb),
        in_specs=[
            blk(0), blk(nb), blk(2 * nb),
            pl.BlockSpec((NA_KH, 2, GRID_W, NA_KH * GRID_W), lambda bi, hp: (0, hp, 0, 0)),
        ],
        out_specs=pl.BlockSpec((None, s, LANES), lambda bi, hp: (bi, 0, hp)),
        out_shape=jax.ShapeDtypeStruct((b, s, NA_W), F32),
        scratch_shapes=[pltpu.VMEM((2, s, LANES), BF16), pltpu.VMEM((s, LANES), BF16),
                        pltpu.VMEM((s, 2 * LANES), BF16)],
        compiler_params=_cparams(("arbitrary", "arbitrary")),
        name="neighbourhood",
    )(p, p, p, bias)


def _swa_kernel(sink_ref, q_ref, k_ref, v_ref, o_ref, kd_ref, vd_ref):
    kp = pl.program_id(1)
    s_len = q_ref.shape[0]
    seq = s_len - TM
    nblk = seq // SWA_BLOCK
    band = SWA_BLOCK + 2 * SWA_WINDOW
    group = SWA_HEADS // SWA_KV_HEADS
    heads_per_step = 2 * group
    m0t = lax.broadcasted_iota(jnp.int32, (TM, LANES), 1) < HEAD_DIM

    def stage(i, carry):
        r0 = pl.multiple_of(i * TM, TM)
        for src, dst in ((k_ref, kd_ref), (v_ref, vd_ref)):
            x = src[pl.ds(r0, TM), :]
            xr = pltpu.roll(x, HEAD_DIM, 1)
            dst[0, pl.ds(r0, TM), 0:LANES] = jnp.where(m0t, x, xr).astype(BF16)
            dst[1, pl.ds(r0, TM), 0:LANES] = jnp.where(m0t, xr, x).astype(BF16)
        for h in range(2):
            vd_ref[h, pl.ds(r0, TM), LANES:2 * LANES] = jnp.ones((TM, LANES), BF16)
        return carry

    lax.fori_loop(0, s_len // TM, stage, 0)
    o_ref[0:TM, :] = jnp.zeros((TM, o_ref.shape[1]), F32)

    qi = lax.broadcasted_iota(jnp.int32, (2 * SWA_BLOCK, band), 0) % SWA_BLOCK
    ki = lax.broadcasted_iota(jnp.int32, (2 * SWA_BLOCK, band), 1)
    m0 = lax.broadcasted_iota(jnp.int32, (SWA_BLOCK, LANES), 1) < HEAD_DIM
    first_head = lax.broadcasted_iota(jnp.int32, (2 * SWA_BLOCK, 1), 0) < SWA_BLOCK

    def block(i, carry):
        start = jnp.clip((i - 1) * SWA_BLOCK, 0, seq - band)
        k0 = pl.multiple_of(TM + start, SWA_BLOCK)
        q0 = pl.multiple_of(TM + i * SWA_BLOCK, SWA_BLOCK)
        valid = jnp.abs(qi + (i * SWA_BLOCK - start) - ki) <= SWA_WINDOW
        for hh in range(2):
            kb = kd_ref[hh, pl.ds(k0, band), :]
            vb = vd_ref[hh, pl.ds(k0, band), :]
            kc = kd_ref[hh, 0:TM, :]
            vc = vd_ref[hh, 0:TM, :]
            for j in range(group // 2):
                pair = hh * (group // 2) + j
                qp = q_ref[pl.ds(q0, SWA_BLOCK), pair * LANES:(pair + 1) * LANES]
                q = jnp.concatenate([jnp.where(m0, qp, 0.0), jnp.where(m0, 0.0, qp)], axis=0).astype(BF16)
                s = jnp.where(valid, _dot_nt(q, kb), NEG_INF)
                s_ctx = _dot_nt(q, kc)
                h0 = kp * heads_per_step + 2 * pair
                sink = jnp.where(first_head, sink_ref[h0], sink_ref[h0 + 1]) * LOG2E
                res = _softmax_pv([s, s_ctx], [vb, vc], extra=sink)
                o_ref[pl.ds(q0, SWA_BLOCK), pair * LANES:(pair + 1) * LANES] = jnp.where(
                    m0, res[:SWA_BLOCK], res[SWA_BLOCK:])
        return carry

    lax.fori_loop(0, nblk, block, 0, unroll=2)


def _swa(p, sink):
    b, s, _ = p.shape
    qw = SWA_HEADS * HEAD_DIM // 2
    kblk = SWA_HEADS * HEAD_DIM // LANES
    vblk = kblk + SWA_KV_HEADS * HEAD_DIM // LANES
    return pl.pallas_call(
        _swa_kernel,
        grid=(b, 2),
        in_specs=[
            pl.BlockSpec(memory_space=pltpu.SMEM),
            pl.BlockSpec((None, s, qw), lambda bi, kp: (bi, 0, kp)),
            pl.BlockSpec((None, s, LANES), lambda bi, kp: (bi, 0, kblk + kp)),
            pl.BlockSpec((None, s, LANES), lambda bi, kp: (bi, 0, vblk + kp)),
        ],
        out_specs=pl.BlockSpec((None, s, qw), lambda bi, kp: (bi, 0, kp)),
        out_shape=jax.ShapeDtypeStruct((b, s, SWA_HEADS * HEAD_DIM), F32),
        scratch_shapes=[pltpu.VMEM((2, s, LANES), BF16), pltpu.VMEM((2, s, 2 * LANES), BF16)],
        compiler_params=_cparams(("arbitrary", "arbitrary")),
        name="swa",
    )(sink, p, p, p)


def _outproj_kernel(*refs, n_mix):
    mix_refs = refs[:n_mix]
    (w_ref, x_ref, g1_ref, gn_ref, sh_ref, sc_ref, wr_ref, br_ref,
     xo_ref, h_ref, ri_ref, cnt_ref, carry_ref) = refs[n_mix:]
    first = (pl.program_id(0) == 0) & (pl.program_id(1) == 0)

    @pl.when(first)
    def _():
        carry_ref[...] = jnp.zeros_like(carry_ref)

    o = None
    off = 0
    for m_ref in mix_refs:
        kw = m_ref.shape[1]
        part = jnp.dot(m_ref[...].astype(BF16), w_ref[off:off + kw, :], preferred_element_type=F32)
        o = part if o is None else o + part
        off += kw
    xn = x_ref[...] + g1_ref[...] * o
    xo_ref[...] = xn
    h = _rms_mod(xn, gn_ref[...], sh_ref[...], sc_ref[...])
    h_ref[...] = h

    logits = _dot3(h, wr_ref[...]) + br_ref[...]
    tm = logits.shape[0]
    lane = lax.broadcasted_iota(jnp.int32, (tm, LANES), 1).astype(F32)
    big = 1e9
    gmask = lane < MOE_GROUPS
    mg = jnp.max(jnp.where(gmask, logits, -big), axis=-1, keepdims=True)
    sg = jnp.sum(jnp.where(gmask, jnp.exp(jnp.minimum(logits - mg, 0.0)), 0.0), axis=-1, keepdims=True)
    gw = 1.0 / sg
    gi = jnp.min(jnp.where(gmask & (logits == mg), lane, big), axis=-1, keepdims=True)
    lo = ROUTE_LANE0 + MOE_EPG * gi
    emask = (lane >= lo) & (lane < lo + MOE_EPG)
    l1 = jnp.max(jnp.where(emask, logits, -big), axis=-1, keepdims=True)
    i1 = jnp.min(jnp.where(emask & (logits == l1), lane, big), axis=-1, keepdims=True)
    emask2 = emask & (lane != i1)
    l2 = jnp.max(jnp.where(emask2, logits, -big), axis=-1, keepdims=True)
    i2 = jnp.min(jnp.where(emask2 & (logits == l2), lane, big), axis=-1, keepdims=True)
    e21 = jnp.exp(l2 - l1)
    w1 = gw / (1.0 + e21)
    w2 = gw * e21 / (1.0 + e21)

    oh = jnp.where((lane == i1) | (lane == i2), 1.0, 0.0)
    tri = (lax.broadcasted_iota(jnp.int32, (tm, tm), 0) > lax.broadcasted_iota(jnp.int32, (tm, tm), 1))
    cum = jnp.dot(jnp.where(tri, 1.0, 0.0).astype(BF16), oh.astype(BF16), preferred_element_type=F32) + carry_ref[...]
    r1 = jnp.sum(jnp.where(lane == i1, cum, 0.0), axis=-1, keepdims=True)
    r2 = jnp.sum(jnp.where(lane == i2, cum, 0.0), axis=-1, keepdims=True)
    carry_ref[...] = carry_ref[...] + jnp.sum(oh, axis=0, keepdims=True)
    cnt_ref[...] = carry_ref[...]

    ri = jnp.where(lane == 0, i1 - ROUTE_LANE0, 0.0)
    ri = jnp.where(lane == 1, i2 - ROUTE_LANE0, ri)
    ri = jnp.where(lane == 2, w1, ri)
    ri = jnp.where(lane == 3, w2, ri)
    ri = jnp.where(lane == 4, r1, ri)
    ri = jnp.where(lane == 5, r2, ri)
    ri_ref[...] = ri


def _outproj(mixes, w_bf16, xa, g1, gain2, sh2, sc2, wr, br, t0):
    b, s, d = xa.shape
    nt = s // TM - t0
    so = nt * TM
    mod_idx = lambda bi, t: (jnp.where(t + t0 == 0, b, bi), 0, 0)
    tile = lambda wdt: pl.BlockSpec((None, TM, wdt), lambda bi, t: (bi, t + t0, 0))
    otile = lambda wdt: pl.BlockSpec((None, TM, wdt), lambda bi, t: (bi, t, 0))
    const = lambda shape: pl.BlockSpec(shape, lambda bi, t: (0,) * len(shape))
    in_specs = [tile(m.shape[2]) for m in mixes] + [
        const(w_bf16.shape), tile(d),
        pl.BlockSpec((None, 1, d), mod_idx), const((1, d)),
        pl.BlockSpec((None, 1, d), mod_idx), pl.BlockSpec((None, 1, d), mod_idx),
        const((d, LANES)), const((1, LANES)),
    ]
    return pl.pallas_call(
        functools.partial(_outproj_kernel, n_mix=len(mixes)),
        grid=(b, nt),
        in_specs=in_specs,
        out_specs=[otile(d), otile(d), otile(LANES), const((1, LANES))],
        out_shape=[jax.ShapeDtypeStruct((b, so, d), F32), jax.ShapeDtypeStruct((b, so, d), F32),
                   jax.ShapeDtypeStruct((b, so, LANES), F32), jax.ShapeDtypeStruct((1, LANES), F32)],
        scratch_shapes=[pltpu.VMEM((1, LANES), F32)],
        compiler_params=_cparams(("arbitrary", "arbitrary")),
        name="outproj",
    )(*mixes, w_bf16, xa, g1, gain2, sh2, sc2, wr, br)


def _dispatch_kernel(dest_ref, h_ref, xs_in_ref, xs_ref, hbuf, sem):
    del xs_in_ref
    nt = pl.num_programs(1)
    step = pl.program_id(0) * nt + pl.program_id(1)
    last = pl.num_programs(0) * nt - 1
    slot = step % 2
    hbuf[slot] = h_ref[...]

    def copy(sl, i, dst_row):
        return pltpu.make_async_copy(hbuf.at[sl, pl.ds(i, 1)], xs_ref.at[pl.ds(dst_row, 1)], sem.at[sl])

    for i in range(TM):
        copy(slot, i, dest_ref[step, i]).start()
        copy(slot, i, dest_ref[step, TM + i]).start()

    def drain(sl):
        def one(i, c):
            copy(sl, 0, 0).wait()
            return c
        lax.fori_loop(0, 2 * TM, one, 0, unroll=8)

    @pl.when(step > 0)
    def _():
        drain(1 - slot)

    @pl.when(step == last)
    def _():
        drain(slot)


def _dispatch(dest, h2, n_pad):
    b, s, d = h2.shape
    nt = s // TM
    xs0 = jnp.zeros((n_pad, d), F32)
    return pl.pallas_call(
        _dispatch_kernel,
        grid_spec=pltpu.PrefetchScalarGridSpec(
            num_scalar_prefetch=1,
            grid=(b, nt),
            in_specs=[pl.BlockSpec((None, TM, d), lambda bi, t, dr: (bi, t, 0)),
                      pl.BlockSpec(memory_space=pl.ANY)],
            out_specs=pl.BlockSpec(memory_space=pl.ANY),
            scratch_shapes=[pltpu.VMEM((2, TM, d), F32), pltpu.SemaphoreType.DMA((2,))],
        ),
        out_shape=jax.ShapeDtypeStruct((n_pad, d), F32),
        input_output_aliases={2: 0},
        compiler_params=_cparams(("arbitrary", "arbitrary")),
        name="dispatch",
    )(dest, h2, xs0)


def _mlp_kernel(be_ref, nu_ref, x_ref, wgu_ref, wd_ref, y_ref, wgu_b, wd_b):
    i = pl.program_id(0)
    prev = be_ref[jnp.maximum(i - 1, 0)]
    used = i < nu_ref[0]

    @pl.when(used & ((i == 0) | (be_ref[i] != prev)))
    def _():
        wgu_b[...] = wgu_ref[...].astype(BF16)
        wd_b[...] = wd_ref[...].astype(BF16)

    @pl.when(used)
    def _():
        gu = jnp.dot(x_ref[...].astype(BF16), wgu_b[...], preferred_element_type=F32)
        act = _silu(gu[:, :MOE_FF]) * gu[:, MOE_FF:]
        y_ref[...] = jnp.dot(act.astype(BF16), wd_b[...], preferred_element_type=F32)

    @pl.when(jnp.logical_not(used))
    def _():
        y_ref[...] = jnp.zeros_like(y_ref)


def _expert_mlp(block_e, n_used, xs, w_gu, w_down, layer):
    n_pad, d = xs.shape
    ff2 = w_gu.shape[-1]
    return pl.pallas_call(
        _mlp_kernel,
        grid_spec=pltpu.PrefetchScalarGridSpec(
            num_scalar_prefetch=2,
            grid=(n_pad // MOE_ROWS,),
            in_specs=[
                pl.BlockSpec((MOE_ROWS, d), lambda i, be, nu: (i, 0)),
                pl.BlockSpec((None, None, d, ff2), lambda i, be, nu: (layer, be[i], 0, 0)),
                pl.BlockSpec((None, None, ff2 // 2, d), lambda i, be, nu: (layer, be[i], 0, 0)),
            ],
            out_specs=pl.BlockSpec((MOE_ROWS, d), lambda i, be, nu: (i, 0)),
            scratch_shapes=[pltpu.VMEM((d, ff2), BF16), pltpu.VMEM((ff2 // 2, d), BF16)],
        ),
        out_shape=jax.ShapeDtypeStruct((n_pad, d), F32),
        compiler_params=_cparams(("arbitrary",)),
        name="expert_mlp",
    )(block_e, n_used, xs, w_gu, w_down)


def _combine_kernel(dest_ref, ys_ref, x_ref, ri_ref, g2_ref, fg_ref, o_ref, buf, sem, *, final):
    bi = pl.program_id(0)
    t = pl.program_id(1)
    nt = pl.num_programs(1)
    step = bi * nt + t
    total = pl.num_programs(0) * nt

    def copy(src_row, slot, k, i):
        return pltpu.make_async_copy(ys_ref.at[pl.ds(src_row, 1)], buf.at[slot, k, pl.ds(i, 1)], sem.at[slot])

    def issue(st, slot):
        for i in range(TM):
            copy(dest_ref[st, i], slot, 0, i).start()
            copy(dest_ref[st, TM + i], slot, 1, i).start()

    slot = step % 2

    @pl.when(step == 0)
    def _():
        issue(0, 0)

    @pl.when(step + 1 < total)
    def _():
        issue(step + 1, 1 - slot)

    def drain(i, c):
        copy(0, slot, 0, 0).wait()
        return c

    lax.fori_loop(0, 2 * TM, drain, 0, unroll=8)

    lane = lax.broadcasted_iota(jnp.int32, (TM, LANES), 1)
    ri = ri_ref[...]
    w1 = jnp.sum(jnp.where(lane == 2, ri, 0.0), axis=-1, keepdims=True)
    w2 = jnp.sum(jnp.where(lane == 3, ri, 0.0), axis=-1, keepdims=True)
    y = buf[slot, 0] * w1 + buf[slot, 1] * w2
    xn = x_ref[...] + g2_ref[...] * y
    if final:
        ms = jnp.mean(xn * xn, axis=-1, keepdims=True)
        xn = xn * lax.rsqrt(ms + NORM_EPS) * fg_ref[...]
    o_ref[...] = xn


def _combine(dest, ys, xa, rinfo, g2, final_g, has_ctx, final):
    b, s, d = xa.shape
    nt = s // TM
    mod_idx = lambda bi, t, dr: (jnp.where(t == 0, b, bi) if has_ctx else bi, 0, 0)
    tile = lambda wdt: pl.BlockSpec((None, TM, wdt), lambda bi, t, dr: (bi, t, 0))
    out_spec = tile(d)
    out_shape = jax.ShapeDtypeStruct((b, s, d), F32)
    return pl.pallas_call(
        functools.partial(_combine_kernel, final=final),
        grid_spec=pltpu.PrefetchScalarGridSpec(
            num_scalar_prefetch=1,
            grid=(b, nt),
            in_specs=[
                pl.BlockSpec(memory_space=pl.ANY),
                tile(d), tile(LANES),
                pl.BlockSpec((None, 1, d), mod_idx),
                pl.BlockSpec((1, d), lambda bi, t, dr: (0, 0)),
            ],
            out_specs=out_spec,
            scratch_shapes=[pltpu.VMEM((2, 2, TM, d), F32), pltpu.SemaphoreType.DMA((2,))],
        ),
        out_shape=out_shape,
        compiler_params=_cparams(("arbitrary", "arbitrary")),
        name="combine",
    )(dest, ys, xa, rinfo, g2, final_g)


def _moe(h2, rinfo, counts, w_gu, w_down, layer):
    b, s, d = h2.shape
    nt = s // TM
    r = rinfo[:, :, :8]
    e = r[..., 0:2].astype(jnp.int32)
    rank = r[..., 4:6].astype(jnp.int32)
    cnt = counts[0, ROUTE_LANE0:ROUTE_LANE0 + MOE_EXPERTS].astype(jnp.int32)
    padded = (cnt + MOE_ROWS - 1) // MOE_ROWS * MOE_ROWS
    ends = jnp.cumsum(padded)
    starts = ends - padded
    eids = jnp.arange(MOE_EXPERTS, dtype=jnp.int32)
    dest = jnp.sum(jnp.where(e[..., None] == eids, starts, 0), axis=-1) + rank
    dest = dest.reshape(b, nt, TM, 2).transpose(0, 1, 3, 2).reshape(b * nt, 2 * TM)
    n_assign = b * nt * TM * 2
    n_blocks = (n_assign + MOE_EXPERTS * (MOE_ROWS - 1)) // MOE_ROWS + 1
    n_pad = n_blocks * MOE_ROWS
    blk_row = jnp.arange(n_blocks, dtype=jnp.int32) * MOE_ROWS
    block_e = jnp.minimum(jnp.sum((blk_row[:, None] >= ends[None, :]).astype(jnp.int32), axis=1),
                          MOE_EXPERTS - 1)
    n_used = (ends[-1] // MOE_ROWS).astype(jnp.int32).reshape(1)
    xs = _dispatch(dest, h2, n_pad)
    ys = _expert_mlp(block_e, n_used, xs, w_gu, w_down, layer)
    return dest, ys


def _router_weights(wg, bg, we, be):
    d = wg.shape[0]
    pad = LANES - MOE_GROUPS - MOE_EXPERTS
    assert ROUTE_LANE0 == MOE_GROUPS
    wr = jnp.concatenate([wg.astype(F32), we.astype(F32), jnp.zeros((d, pad), F32)], axis=1)
    br = jnp.concatenate([bg.astype(F32), be.astype(F32), jnp.zeros((pad,), F32)]).reshape(1, LANES)
    return wr, br


def kernel(x, c, ctx, c_ctx, ada_w, ada_b, norm_g, final_g, ab_w_in, ab_w_out, ret_decay, ret_gn, na_rpb,
           swa_w_in, swa_w_out, swa_sink, router_g_w, router_g_b, router_e_w, router_e_b,
           expert_w_gu, expert_w_down):
    b, seq, d = x.shape
    assert ctx.shape[1] == TM and seq % TM == 0 and d == D_MODEL
    xa = jnp.concatenate([ctx, x], axis=1)
    rope = _rope_tables(seq)

    cvec = jnp.concatenate([c, c_ctx[None, :], jnp.zeros((7, d), F32)], axis=0)
    mod = _adaln(cvec, ada_w, ada_b)
    mod = mod.reshape(DEPTH, b + 8, 6, 1, d)[:, :b + 1].transpose(0, 2, 1, 3, 4)

    for layer in range(DEPTH):
        last = layer == DEPTH - 1
        sh1, sc1, g1, sh2, sc2, g2 = (mod[layer, i] for i in range(6))
        gain1 = norm_g[layer, 0].reshape(1, d)
        gain2 = norm_g[layer, 1].reshape(1, d)
        j = layer // 2
        if layer % 2 == 0:
            nb = RET_W // LANES
            qk_scale = HEAD_DIM ** -0.5
            scales = {blk: qk_scale for blk in range(nb, 2 * nb)}
            scales.update({blk: qk_scale * LOG2E for blk in range(4 * nb, 5 * nb)})
            p = _proj(xa, gain1, sh1, sc1, ab_w_in[j].astype(BF16), rope,
                      rope_blocks=range(0, 2 * nb), scales=scales)
            log_gamma = jnp.log1p(-jnp.exp2(-ret_decay[j].astype(F32)))
            mixes = [_retention(p, log_gamma, ret_gn[j]), _neighbourhood(p, _na_bias(na_rpb[j]))]
            w_out = ab_w_out[j]
        else:
            qb = SWA_HEADS * HEAD_DIM // LANES
            kb = SWA_KV_HEADS * HEAD_DIM // LANES
            p = _proj(xa, gain1, sh1, sc1, swa_w_in[j].astype(BF16), rope,
                      rope_blocks=range(0, qb + kb),
                      scales={blk: HEAD_DIM ** -0.5 * LOG2E for blk in range(qb)})
            mixes = [_swa(p, swa_sink[j].astype(F32))]
            w_out = swa_w_out[j]
        t0 = 1 if last else 0
        wr, br = _router_weights(router_g_w[layer], router_g_b[layer], router_e_w[layer], router_e_b[layer])
        xa, h2, rinfo, counts = _outproj(mixes, w_out.astype(BF16), xa, g1, gain2, sh2, sc2, wr, br, t0)
        dest, ys = _moe(h2, rinfo, counts, expert_w_gu, expert_w_down, layer)
        xa = _combine(dest, ys, xa, rinfo, g2, final_g.reshape(1, d), not last, last)
    return xa
```

```python
import functools

import numpy as np
import jax
import jax.numpy as jnp
from jax import lax
from jax.experimental import pallas as pl
from jax.experimental.pallas import tpu as pltpu

F32 = jnp.float32
BF16 = jnp.bfloat16

D_MODEL = 1024
DEPTH = 2
GRID_W = 64
HEAD_DIM = 64
RET_HEADS = 8
NA_HEADS = 8
RET_W = 512
NA_W = 512
AB_IN = 4 * RET_W + 3 * NA_W
RET_CHUNK = 128
GN_EPS = 1e-5
NA_KH = 8
NA_KW = 16
SWA_HEADS = 16
SWA_KV_HEADS = 4
SWA_WINDOW = 128
SWA_BLOCK = 128
SWA_IN = (SWA_HEADS + 2 * SWA_KV_HEADS) * HEAD_DIM
ROPE_BASE = 10000.0
MOE_GROUPS = 4
MOE_EPG = 8
MOE_EXPERTS = 32
MOE_FF = 512
NORM_EPS = 1e-6
NEG_INF = -1e30

LANES = 128
TM = 256
RET_BLOCK = 256
MOE_ROWS = 256
ROUTE_LANE0 = 4
VMEM_LIMIT = 56 * 1024 * 1024


def _cparams(sem, vmem=VMEM_LIMIT):
    return pltpu.CompilerParams(dimension_semantics=sem, vmem_limit_bytes=vmem)


def _split_bf16(a):
    hi = a.astype(BF16)
    lo = (a - hi.astype(F32)).astype(BF16)
    return hi, lo


def _dot3(a, b):
    ah, al = _split_bf16(a)
    bh, bl = _split_bf16(b)
    d = lambda x, y: jnp.dot(x, y, preferred_element_type=F32)
    return d(ah, bh) + (d(ah, bl) + d(al, bh))


def _dot_nt(a, b):
    return lax.dot_general(a, b, (((1,), (1,)), ((), ())), preferred_element_type=F32)


def _dot_tn(a, b):
    return lax.dot_general(a, b, (((0,), (0,)), ((), ())), preferred_element_type=F32)


def _silu(x):
    return x / (1.0 + jnp.exp(-x))


def _adaln_kernel(c_ref, w_ref, b_ref, o_ref):
    o_ref[...] = _dot3(_silu(c_ref[...]), w_ref[...]) + b_ref[...]


def _adaln(cvec, ada_w, ada_b):
    depth, d, n6 = ada_w.shape
    rows = cvec.shape[0]
    tn = 1024
    return pl.pallas_call(
        _adaln_kernel,
        grid=(depth, n6 // tn),
        in_specs=[
            pl.BlockSpec((rows, d), lambda l, j: (0, 0)),
            pl.BlockSpec((None, d, tn), lambda l, j: (l, 0, j)),
            pl.BlockSpec((None, 1, tn), lambda l, j: (l, 0, j)),
        ],
        out_specs=pl.BlockSpec((None, rows, tn), lambda l, j: (l, 0, j)),
        out_shape=jax.ShapeDtypeStruct((depth, rows, n6), F32),
        compiler_params=_cparams(("arbitrary", "arbitrary")),
        name="adaln",
    )(cvec, ada_w, ada_b.reshape(depth, 1, n6))


def _rms_mod(x, g, sh, sc):
    ms = jnp.mean(x * x, axis=-1, keepdims=True)
    return (x * lax.rsqrt(ms + NORM_EPS) * g) * (1.0 + sc) + sh


def _proj_kernel(x_ref, g_ref, sh_ref, sc_ref, w_ref, rope_ref, o_ref, *, rope_blocks, scales, cn):
    is_lat = pl.program_id(1) > 0
    hb = _rms_mod(x_ref[...], g_ref[...], sh_ref[...], sc_ref[...]).astype(BF16)
    nout = w_ref.shape[1]
    for c in range(nout // cn):
        o = jnp.dot(hb, w_ref[:, c * cn:(c + 1) * cn], preferred_element_type=F32)
        for s in range(cn // LANES):
            blk = c * (cn // LANES) + s
            ob = o[:, s * LANES:(s + 1) * LANES]
            if blk in rope_blocks:
                r = (ob * rope_ref[0] + pltpu.roll(ob, 16, 1) * rope_ref[1]
                     + pltpu.roll(ob, LANES - 16, 1) * rope_ref[2])
                ob = jnp.where(is_lat, r, ob)
            if blk in scales:
                ob = ob * scales[blk]
            o_ref[:, blk * LANES:(blk + 1) * LANES] = ob


def _proj(xa, gain, sh, sc, w_bf16, rope, rope_blocks, scales):
    b, s, d = xa.shape
    nout = w_bf16.shape[1]
    nt = s // TM
    mod_idx = lambda bi, t: (jnp.where(t == 0, b, bi), 0, 0)
    kern = functools.partial(_proj_kernel, rope_blocks=frozenset(rope_blocks), scales=dict(scales), cn=512)
    return pl.pallas_call(
        kern,
        grid=(b, nt),
        in_specs=[
            pl.BlockSpec((None, TM, d), lambda bi, t: (bi, t, 0)),
            pl.BlockSpec((1, d), lambda bi, t: (0, 0)),
            pl.BlockSpec((None, 1, d), mod_idx),
            pl.BlockSpec((None, 1, d), mod_idx),
            pl.BlockSpec((d, nout), lambda bi, t: (0, 0)),
            pl.BlockSpec((3, TM, LANES), lambda bi, t: (0, jnp.maximum(t - 1, 0), 0)),
        ],
        out_specs=pl.BlockSpec((None, TM, nout), lambda bi, t: (bi, t, 0)),
        out_shape=jax.ShapeDtypeStruct((b, s, nout), F32),
        compiler_params=_cparams(("arbitrary", "arbitrary")),
        name="proj",
    )(xa, gain, sh, sc, w_bf16, rope)


def _rope_tables(seq):
    nf = HEAD_DIM // 4
    inv = ROPE_BASE ** (-jnp.arange(nf, dtype=F32) / nf)
    t = jnp.arange(seq)
    row = (t // GRID_W).astype(F32)
    col = (t % GRID_W).astype(F32)
    lane = np.arange(LANES)
    jj = lane % HEAD_DIM
    axis_is_col = (jj // 32) == 1
    second_half = (jj % 32) >= 16
    f = jj % 16
    pos = jnp.where(axis_is_col[None, :], col[:, None], row[:, None])
    ang = pos * inv[f][None, :]
    c, s = jnp.cos(ang), jnp.sin(ang)
    sa = jnp.where(second_half[None, :], s, 0.0)
    sb = jnp.where(second_half[None, :], 0.0, -s)
    return jnp.stack([c, sa, sb], axis=0)


def _ret_kernel(lg_ref, q_ref, k_ref, v_ref, g_ref, gn_ref, o_ref, accf_ref, accb_ref,
                intra_ref, qd_ref, kd_ref):
    hp = pl.program_id(1)
    c = RET_BLOCK
    s_len = q_ref.shape[0]
    n_chunks = s_len // c
    ctx_chunks = TM // c
    pos = lax.broadcasted_iota(jnp.int32, (c, LANES), 0).astype(F32)
    m0 = lax.broadcasted_iota(jnp.int32, (c, LANES), 1) < HEAD_DIM
    same_head = ((lax.broadcasted_iota(jnp.int32, (LANES, LANES), 0) < HEAD_DIM)
                 == (lax.broadcasted_iota(jnp.int32, (LANES, LANES), 1) < HEAD_DIM))
    rel = (lax.broadcasted_iota(jnp.int32, (c, c), 0) - lax.broadcasted_iota(jnp.int32, (c, c), 1)).astype(F32)
    lgf = [lg_ref[0, hp * 2 + hh] for hh in range(2)]
    lgb = [lg_ref[1, hp * 2 + hh] for hh in range(2)]
    lgf_l = jnp.where(m0, lgf[0], lgf[1])
    lgb_l = jnp.where(m0, lgb[0], lgb[1])
    for hh in range(2):
        intra_ref[0, :, hh * c:(hh + 1) * c] = jnp.where(rel >= 0, jnp.exp(lgf[hh] * jnp.maximum(rel, 0.0)), 0.0)
        intra_ref[1, :, hh * c:(hh + 1) * c] = jnp.where(rel <= 0, jnp.exp(lgb[hh] * jnp.maximum(-rel, 0.0)), 0.0)
    qd_ref[0] = jnp.exp(lgf_l * (pos + 1.0))
    qd_ref[1] = jnp.exp(lgb_l * (c - pos))
    kd_ref[0] = jnp.exp(lgf_l * (c - 1.0 - pos))
    kd_ref[1] = jnp.exp(lgb_l * pos)
    cd = [jnp.exp(lgf_l[0:1] * float(c)), jnp.exp(lgb_l[0:1] * float(c))]

    def chunk(r0, state, d):
        q = q_ref[pl.ds(r0, c), :]
        k = k_ref[pl.ds(r0, c), :]
        v = v_ref[pl.ds(r0, c), :]
        qb = q.astype(BF16)
        kcat = jnp.concatenate([jnp.where(m0, k, 0.0), jnp.where(m0, 0.0, k)], axis=0).astype(BF16)
        vcat = jnp.concatenate([jnp.where(m0, v, 0.0), jnp.where(m0, 0.0, v)], axis=0).astype(BF16)
        s = _dot_nt(qb, kcat) * intra_ref[d]
        out = jnp.dot(s.astype(BF16), vcat, preferred_element_type=F32)
        out = out + jnp.dot(qb, state.astype(BF16), preferred_element_type=F32) * qd_ref[d]
        kv = _dot_tn((k * kd_ref[d]).astype(BF16), v.astype(BF16))
        return out, state * cd[d] + jnp.where(same_head, kv, 0.0)

    def body(i, states):
        sf, sb = states
        rf = pl.multiple_of(i * c, c)
        ib = jnp.where(i < ctx_chunks, ctx_chunks - 1 - i, n_chunks + ctx_chunks - 1 - i)
        rb = pl.multiple_of(ib * c, c)
        of, sf = chunk(rf, sf, 0)
        ob, sb = chunk(rb, sb, 1)
        accf_ref[pl.ds(rf, c), :] = of
        accb_ref[pl.ds(rb, c), :] = ob
        return sf, sb

    z = jnp.zeros((LANES, LANES), F32)
    lax.fori_loop(0, n_chunks, body, (z, z), unroll=3)

    avg = jnp.where(same_head, 1.0 / HEAD_DIM, 0.0).astype(BF16)

    def head_mean(x):
        hi, lo = _split_bf16(x)
        return (jnp.dot(hi, avg, preferred_element_type=F32) + jnp.dot(lo, avg, preferred_element_type=F32))

    def readout(i, carry):
        r0 = pl.multiple_of(i * c, c)
        o = accf_ref[pl.ds(r0, c), :] + accb_ref[pl.ds(r0, c), :]
        dlt = o - head_mean(o)
        var = head_mean(dlt * dlt)
        y = dlt * lax.rsqrt(var + GN_EPS) * gn_ref[...]
        o_ref[pl.ds(r0, c), :] = _silu(g_ref[pl.ds(r0, c), :]) * y
        return carry

    lax.fori_loop(0, n_chunks, readout, 0, unroll=2)


def _retention(p, log_gamma, ret_gn):
    b, s, _ = p.shape
    nb = RET_W // LANES
    blk = lambda off: pl.BlockSpec((None, s, LANES), lambda bi, hp: (bi, 0, off + hp))
    return pl.pallas_call(
        _ret_kernel,
        grid=(b, nb),
        in_specs=[
            pl.BlockSpec(memory_space=pltpu.SMEM),
            blk(0), blk(nb), blk(2 * nb), blk(3 * nb),
            pl.BlockSpec((1, LANES), lambda bi, hp: (0, hp)),
        ],
        out_specs=pl.BlockSpec((None, s, LANES), lambda bi, hp: (bi, 0, hp)),
        out_shape=jax.ShapeDtypeStruct((b, s, RET_W), F32),
        scratch_shapes=[pltpu.VMEM((s, LANES), F32), pltpu.VMEM((s, LANES), F32),
                        pltpu.VMEM((2, RET_BLOCK, 2 * RET_BLOCK), F32),
                        pltpu.VMEM((2, RET_BLOCK, LANES), F32), pltpu.VMEM((2, RET_BLOCK, LANES), F32)],
        compiler_params=_cparams(("arbitrary", "arbitrary")),
        name="retention",
    )(log_gamma, p, p, p, p, ret_gn.reshape(1, RET_W))


LOG2E = 1.4426950408889634


def _softmax_pv(s_list, v_list, extra=None):
    m = None
    for s in s_list:
        for j in range(s.shape[1] // LANES):
            blk = s[:, j * LANES:(j + 1) * LANES]
            m = blk if m is None else jnp.maximum(m, blk)
    m = m.max(axis=-1, keepdims=True)
    if extra is not None:
        m = jnp.maximum(m, extra)
    acc = None
    for s, v in zip(s_list, v_list):
        pv = jnp.dot(jnp.exp2(s - m).astype(BF16), v, preferred_element_type=F32)
        acc = pv if acc is None else acc + pv
    o, den = acc[:, :LANES], acc[:, LANES:]
    if extra is not None:
        den = den + jnp.exp2(extra - m)
    return o / den


def _stage_heads(q_ref, k_ref, v_ref, qm_ref, kb_ref, vb_ref):
    m0 = lax.broadcasted_iota(jnp.int32, (TM, LANES), 1) < HEAD_DIM

    def stage(i, carry):
        r0 = pl.multiple_of(i * TM, TM)
        q = q_ref[pl.ds(r0, TM), :]
        qm_ref[0, pl.ds(r0, TM), :] = jnp.where(m0, q, 0.0).astype(BF16)
        qm_ref[1, pl.ds(r0, TM), :] = jnp.where(m0, 0.0, q).astype(BF16)
        kb_ref[pl.ds(r0, TM), :] = k_ref[pl.ds(r0, TM), :].astype(BF16)
        vb_ref[pl.ds(r0, TM), 0:LANES] = v_ref[pl.ds(r0, TM), :].astype(BF16)
        vb_ref[pl.ds(r0, TM), LANES:2 * LANES] = jnp.ones((TM, LANES), BF16)
        return carry

    lax.fori_loop(0, q_ref.shape[0] // TM, stage, 0)


def _na_kernel(q_ref, k_ref, v_ref, bias_ref, o_ref, qm_ref, kb_ref, vb_ref):
    s_len = q_ref.shape[0]
    rows = (s_len - TM) // GRID_W
    nloc = NA_KH * GRID_W
    _stage_heads(q_ref, k_ref, v_ref, qm_ref, kb_ref, vb_ref)

    kc = kb_ref[0:TM, :]
    vc = vb_ref[0:TM, :]
    outs = [_softmax_pv([_dot_nt(qm_ref[hh, 0:TM, :], kc)], [vc]) for hh in range(2)]
    m0c = lax.broadcasted_iota(jnp.int32, (TM, LANES), 1) < HEAD_DIM
    o_ref[0:TM, :] = jnp.where(m0c, outs[0], outs[1])

    m0 = lax.broadcasted_iota(jnp.int32, (GRID_W, LANES), 1) < HEAD_DIM

    def row_block(r, carry):
        rs = jnp.clip(r - NA_KH // 2, 0, rows - NA_KH)
        pat = r - rs
        q0 = pl.multiple_of(TM + r * GRID_W, GRID_W)
        k0 = pl.multiple_of(TM + rs * GRID_W, GRID_W)
        kl = kb_ref[pl.ds(k0, nloc), :]
        vl = vb_ref[pl.ds(k0, nloc), :]
        kc = kb_ref[0:TM, :]
        vc = vb_ref[0:TM, :]
        q = jnp.concatenate([qm_ref[0, pl.ds(q0, GRID_W), :], qm_ref[1, pl.ds(q0, GRID_W), :]], axis=0)
        s_loc = _dot_nt(q, kl) + bias_ref[pat].reshape(2 * GRID_W, nloc)
        s_ctx = _dot_nt(q, kc)
        res = _softmax_pv([s_loc, s_ctx], [vl, vc])
        o_ref[pl.ds(q0, GRID_W), :] = jnp.where(m0, res[:GRID_W], res[GRID_W:])
        return carry

    lax.fori_loop(0, rows, row_block, 0, unroll=4)


def _na_bias(rpb):
    h = rpb.shape[0]
    qc = np.arange(GRID_W)[:, None]
    kc = np.arange(GRID_W)[None, :]
    win = np.clip(qc - NA_KW // 2, 0, GRID_W - NA_KW)
    valid = (kc >= win) & (kc < win + NA_KW)
    col_off = np.clip(kc - qc + NA_KW - 1, 0, 2 * NA_KW - 2)
    pat = np.arange(NA_KH)[:, None]
    row_off = np.arange(NA_KH)[None, :] - pat + NA_KH - 1
    bias = rpb[:, row_off][:, :, :, col_off]
    bias = jnp.where(valid[None, None, None], bias.astype(F32) * LOG2E, NEG_INF)
    bias = bias.transpose(1, 0, 3, 2, 4)
    return bias.reshape(NA_KH, h, GRID_W, NA_KH * GRID_W)


def _neighbourhood(p, bias):
    b, s, _ = p.shape
    nb = NA_W // LANES
    c0 = 4 * RET_W // LANES
    blk = lambda off: pl.BlockSpec((None, s, LANES), lambda bi, hp: (bi, 0, c0 + off + hp))
    return pl.pallas_call(
        _na_kernel,
        grid=(b, nb),
        in_specs=[
            blk(0), blk(nb), blk(2 * nb),
            pl.BlockSpec((NA_KH, 2, GRID_W, NA_KH * GRID_W), lambda bi, hp: (0, hp, 0, 0)),
        ],
        out_specs=pl.BlockSpec((None, s, LANES), lambda bi, hp: (bi, 0, hp)),
        out_shape=jax.ShapeDtypeStruct((b, s, NA_W), F32),
        scratch_shapes=[pltpu.VMEM((2, s, LANES), BF16), pltpu.VMEM((s, LANES), BF16),
                        pltpu.VMEM((s, 2 * LANES), BF16)],
        compiler_params=_cparams(("arbitrary", "arbitrary")),
        name="neighbourhood",
    )(p, p, p, bias)


def _swa_kernel(sink_ref, q_ref, k_ref, v_ref, o_ref, kd_ref, vd_ref):
    kp = pl.program_id(1)
    s_len = q_ref.shape[0]
    seq = s_len - TM
    nblk = seq // SWA_BLOCK
    band = SWA_BLOCK + 2 * SWA_WINDOW
    group = SWA_HEADS // SWA_KV_HEADS
    heads_per_step = 2 * group
    m0t = lax.broadcasted_iota(jnp.int32, (TM, LANES), 1) < HEAD_DIM

    def stage(i, carry):
        r0 = pl.multiple_of(i * TM, TM)
        for src, dst in ((k_ref, kd_ref), (v_ref, vd_ref)):
            x = src[pl.ds(r0, TM), :]
            xr = pltpu.roll(x, HEAD_DIM, 1)
            dst[0, pl.ds(r0, TM), 0:LANES] = jnp.where(m0t, x, xr).astype(BF16)
            dst[1, pl.ds(r0, TM), 0:LANES] = jnp.where(m0t, xr, x).astype(BF16)
        for h in range(2):
            vd_ref[h, pl.ds(r0, TM), LANES:2 * LANES] = jnp.ones((TM, LANES), BF16)
        return carry

    lax.fori_loop(0, s_len // TM, stage, 0)
    o_ref[0:TM, :] = jnp.zeros((TM, o_ref.shape[1]), F32)

    qi = lax.broadcasted_iota(jnp.int32, (2 * SWA_BLOCK, band), 0) % SWA_BLOCK
    ki = lax.broadcasted_iota(jnp.int32, (2 * SWA_BLOCK, band), 1)
    m0 = lax.broadcasted_iota(jnp.int32, (SWA_BLOCK, LANES), 1) < HEAD_DIM
    first_head = lax.broadcasted_iota(jnp.int32, (2 * SWA_BLOCK, 1), 0) < SWA_BLOCK

    def block(i, carry):
        start = jnp.clip((i - 1) * SWA_BLOCK, 0, seq - band)
        k0 = pl.multiple_of(TM + start, SWA_BLOCK)
        q0 = pl.multiple_of(TM + i * SWA_BLOCK, SWA_BLOCK)
        valid = jnp.abs(qi + (i * SWA_BLOCK - start) - ki) <= SWA_WINDOW
        for hh in range(2):
            kb = kd_ref[hh, pl.ds(k0, band), :]
            vb = vd_ref[hh, pl.ds(k0, band), :]
            kc = kd_ref[hh, 0:TM, :]
            vc = vd_ref[hh, 0:TM, :]
            for j in range(group // 2):
                pair = hh * (group // 2) + j
                qp = q_ref[pl.ds(q0, SWA_BLOCK), pair * LANES:(pair + 1) * LANES]
                q = jnp.concatenate([jnp.where(m0, qp, 0.0), jnp.where(m0, 0.0, qp)], axis=0).astype(BF16)
                s = jnp.where(valid, _dot_nt(q, kb), NEG_INF)
                s_ctx = _dot_nt(q, kc)
                h0 = kp * heads_per_step + 2 * pair
                sink = jnp.where(first_head, sink_ref[h0], sink_ref[h0 + 1]) * LOG2E
                res = _softmax_pv([s, s_ctx], [vb, vc], extra=sink)
                o_ref[pl.ds(q0, SWA_BLOCK), pair * LANES:(pair + 1) * LANES] = jnp.where(
                    m0, res[:SWA_BLOCK], res[SWA_BLOCK:])
        return carry

    lax.fori_loop(0, nblk, block, 0, unroll=2)


def _swa(p, sink):
    b, s, _ = p.shape
    qw = SWA_HEADS * HEAD_DIM // 2
    kblk = SWA_HEADS * HEAD_DIM // LANES
    vblk = kblk + SWA_KV_HEADS * HEAD_DIM // LANES
    return pl.pallas_call(
        _swa_kernel,
        grid=(b, 2),
        in_specs=[
            pl.BlockSpec(memory_space=pltpu.SMEM),
            pl.BlockSpec((None, s, qw), lambda bi, kp: (bi, 0, kp)),
            pl.BlockSpec((None, s, LANES), lambda bi, kp: (bi, 0, kblk + kp)),
            pl.BlockSpec((None, s, LANES), lambda bi, kp: (bi, 0, vblk + kp)),
        ],
        out_specs=pl.BlockSpec((None, s, qw), lambda bi, kp: (bi, 0, kp)),
        out_shape=jax.ShapeDtypeStruct((b, s, SWA_HEADS * HEAD_DIM), F32),
        scratch_shapes=[pltpu.VMEM((2, s, LANES), BF16), pltpu.VMEM((2, s, 2 * LANES), BF16)],
        compiler_params=_cparams(("arbitrary", "arbitrary")),
        name="swa",
    )(sink, p, p, p)


def _outproj_kernel(*refs, n_mix):
    mix_refs = refs[:n_mix]
    (w_ref, x_ref, g1_ref, gn_ref, sh_ref, sc_ref, wr_ref, br_ref,
     xo_ref, h_ref, ri_ref, cnt_ref, carry_ref) = refs[n_mix:]
    first = (pl.program_id(0) == 0) & (pl.program_id(1) == 0)

    @pl.when(first)
    def _():
        carry_ref[...] = jnp.zeros_like(carry_ref)

    o = None
    off = 0
    for m_ref in mix_refs:
        kw = m_ref.shape[1]
        part = jnp.dot(m_ref[...].astype(BF16), w_ref[off:off + kw, :], preferred_element_type=F32)
        o = part if o is None else o + part
        off += kw
    xn = x_ref[...] + g1_ref[...] * o
    xo_ref[...] = xn
    h = _rms_mod(xn, gn_ref[...], sh_ref[...], sc_ref[...])
    h_ref[...] = h

    logits = _dot3(h, wr_ref[...]) + br_ref[...]
    tm = logits.shape[0]
    lane = lax.broadcasted_iota(jnp.int32, (tm, LANES), 1).astype(F32)
    big = 1e9
    gmask = lane < MOE_GROUPS
    mg = jnp.max(jnp.where(gmask, logits, -big), axis=-1, keepdims=True)
    sg = jnp.sum(jnp.where(gmask, jnp.exp(jnp.minimum(logits - mg, 0.0)), 0.0), axis=-1, keepdims=True)
    gw = 1.0 / sg
    gi = jnp.min(jnp.where(gmask & (logits == mg), lane, big), axis=-1, keepdims=True)
    lo = ROUTE_LANE0 + MOE_EPG * gi
    emask = (lane >= lo) & (lane < lo + MOE_EPG)
    l1 = jnp.max(jnp.where(emask, logits, -big), axis=-1, keepdims=True)
    i1 = jnp.min(jnp.where(emask & (logits == l1), lane, big), axis=-1, keepdims=True)
    emask2 = emask & (lane != i1)
    l2 = jnp.max(jnp.where(emask2, logits, -big), axis=-1, keepdims=True)
    i2 = jnp.min(jnp.where(emask2 & (logits == l2), lane, big), axis=-1, keepdims=True)
    e21 = jnp.exp(l2 - l1)
    w1 = gw / (1.0 + e21)
    w2 = gw * e21 / (1.0 + e21)

    oh = jnp.where((lane == i1) | (lane == i2), 1.0, 0.0)
    tri = (lax.broadcasted_iota(jnp.int32, (tm, tm), 0) > lax.broadcasted_iota(jnp.int32, (tm, tm), 1))
    cum = jnp.dot(jnp.where(tri, 1.0, 0.0).astype(BF16), oh.astype(BF16), preferred_element_type=F32) + carry_ref[...]
    r1 = jnp.sum(jnp.where(lane == i1, cum, 0.0), axis=-1, keepdims=True)
    r2 = jnp.sum(jnp.where(lane == i2, cum, 0.0), axis=-1, keepdims=True)
    carry_ref[...] = carry_ref[...] + jnp.sum(oh, axis=0, keepdims=True)
    cnt_ref[...] = carry_ref[...]

    ri = jnp.where(lane == 0, i1 - ROUTE_LANE0, 0.0)
    ri = jnp.where(lane == 1, i2 - ROUTE_LANE0, ri)
    ri = jnp.where(lane == 2, w1, ri)
    ri = jnp.where(lane == 3, w2, ri)
    ri = jnp.where(lane == 4, r1, ri)
    ri = jnp.where(lane == 5, r2, ri)
    ri_ref[...] = ri


def _outproj(mixes, w_bf16, xa, g1, gain2, sh2, sc2, wr, br, t0):
    b, s, d = xa.shape
    nt = s // TM - t0
    so = nt * TM
    mod_idx = lambda bi, t: (jnp.where(t + t0 == 0, b, bi), 0, 0)
    tile = lambda wdt: pl.BlockSpec((None, TM, wdt), lambda bi, t: (bi, t + t0, 0))
    otile = lambda wdt: pl.BlockSpec((None, TM, wdt), lambda bi, t: (bi, t, 0))
    const = lambda shape: pl.BlockSpec(shape, lambda bi, t: (0,) * len(shape))
    in_specs = [tile(m.shape[2]) for m in mixes] + [
        const(w_bf16.shape), tile(d),
        pl.BlockSpec((None, 1, d), mod_idx), const((1, d)),
        pl.BlockSpec((None, 1, d), mod_idx), pl.BlockSpec((None, 1, d), mod_idx),
        const((d, LANES)), const((1, LANES)),
    ]
    return pl.pallas_call(
        functools.partial(_outproj_kernel, n_mix=len(mixes)),
        grid=(b, nt),
        in_specs=in_specs,
        out_specs=[otile(d), otile(d), otile(LANES), const((1, LANES))],
        out_shape=[jax.ShapeDtypeStruct((b, so, d), F32), jax.ShapeDtypeStruct((b, so, d), F32),
                   jax.ShapeDtypeStruct((b, so, LANES), F32), jax.ShapeDtypeStruct((1, LANES), F32)],
        scratch_shapes=[pltpu.VMEM((1, LANES), F32)],
        compiler_params=_cparams(("arbitrary", "arbitrary")),
        name="outproj",
    )(*mixes, w_bf16, xa, g1, gain2, sh2, sc2, wr, br)


SUBLANES = 8


def _to_token_tiles(x):
    return x.reshape(x.shape[0], SUBLANES, x.shape[1] // SUBLANES)


def _from_token_tiles(x3):
    return x3.reshape(x3.shape[0], x3.shape[1] * x3.shape[2])


def _dispatch_kernel(dest_ref, h_ref, xs_in_ref, xs_ref, hbuf, sem):
    del xs_in_ref
    nt = pl.num_programs(1)
    step = pl.program_id(0) * nt + pl.program_id(1)
    last = pl.num_programs(0) * nt - 1
    slot = step % 2
    hbuf[slot] = _to_token_tiles(h_ref[...])

    def copy(sl, i, dst_row):
        return pltpu.make_async_copy(hbuf.at[sl, i], xs_ref.at[dst_row], sem.at[sl])

    for i in range(TM):
        copy(slot, i, dest_ref[step, i]).start()
        copy(slot, i, dest_ref[step, TM + i]).start()

    def drain(sl):
        def one(i, c):
            copy(sl, 0, 0).wait()
            return c
        lax.fori_loop(0, 2 * TM, one, 0, unroll=8)

    @pl.when(step > 0)
    def _():
        drain(1 - slot)

    @pl.when(step == last)
    def _():
        drain(slot)


def _dispatch(dest, h2, n_pad):
    b, s, d = h2.shape
    nt = s // TM
    xs0 = jnp.zeros((n_pad, SUBLANES, d // SUBLANES), F32)
    return pl.pallas_call(
        _dispatch_kernel,
        grid_spec=pltpu.PrefetchScalarGridSpec(
            num_scalar_prefetch=1,
            grid=(b, nt),
            in_specs=[pl.BlockSpec((None, TM, d), lambda bi, t, dr: (bi, t, 0)),
                      pl.BlockSpec(memory_space=pl.ANY)],
            out_specs=pl.BlockSpec(memory_space=pl.ANY),
            scratch_shapes=[pltpu.VMEM((2, TM, SUBLANES, d // SUBLANES), F32), pltpu.SemaphoreType.DMA((2,))],
        ),
        out_shape=jax.ShapeDtypeStruct(xs0.shape, F32),
        input_output_aliases={2: 0},
        compiler_params=_cparams(("arbitrary", "arbitrary")),
        name="dispatch",
    )(dest, h2, xs0)


def _mlp_kernel(be_ref, nu_ref, x_ref, wgu_ref, wd_ref, y_ref, wgu_b, wd_b):
    i = pl.program_id(0)
    prev = be_ref[jnp.maximum(i - 1, 0)]
    used = i < nu_ref[0]

    @pl.when(used & ((i == 0) | (be_ref[i] != prev)))
    def _():
        wgu_b[...] = wgu_ref[...].astype(BF16)
        wd_b[...] = wd_ref[...].astype(BF16)

    @pl.when(used)
    def _():
        x = _from_token_tiles(x_ref[...])
        gu = jnp.dot(x.astype(BF16), wgu_b[...], preferred_element_type=F32)
        act = _silu(gu[:, :MOE_FF]) * gu[:, MOE_FF:]
        y_ref[...] = _to_token_tiles(jnp.dot(act.astype(BF16), wd_b[...], preferred_element_type=F32))

    @pl.when(jnp.logical_not(used))
    def _():
        y_ref[...] = jnp.zeros_like(y_ref)


def _expert_mlp(block_e, n_used, xs, w_gu, w_down, layer):
    n_pad, sub, dl = xs.shape
    d = sub * dl
    ff2 = w_gu.shape[-1]
    slots = pl.BlockSpec((MOE_ROWS, sub, dl), lambda i, be, nu: (i, 0, 0))
    return pl.pallas_call(
        _mlp_kernel,
        grid_spec=pltpu.PrefetchScalarGridSpec(
            num_scalar_prefetch=2,
            grid=(n_pad // MOE_ROWS,),
            in_specs=[
                slots,
                pl.BlockSpec((None, None, d, ff2), lambda i, be, nu: (layer, be[i], 0, 0)),
                pl.BlockSpec((None, None, ff2 // 2, d), lambda i, be, nu: (layer, be[i], 0, 0)),
            ],
            out_specs=slots,
            scratch_shapes=[pltpu.VMEM((d, ff2), BF16), pltpu.VMEM((ff2 // 2, d), BF16)],
        ),
        out_shape=jax.ShapeDtypeStruct(xs.shape, F32),
        compiler_params=_cparams(("arbitrary",)),
        name="expert_mlp",
    )(block_e, n_used, xs, w_gu, w_down)


def _combine_kernel(dest_ref, ys_ref, x_ref, ri_ref, g2_ref, fg_ref, o_ref, buf, sem, *, final):
    bi = pl.program_id(0)
    t = pl.program_id(1)
    nt = pl.num_programs(1)
    step = bi * nt + t
    total = pl.num_programs(0) * nt

    def copy(src_row, slot, k, i):
        return pltpu.make_async_copy(ys_ref.at[src_row], buf.at[slot, k, i], sem.at[slot])

    def issue(st, slot):
        for i in range(TM):
            copy(dest_ref[st, i], slot, 0, i).start()
            copy(dest_ref[st, TM + i], slot, 1, i).start()

    slot = step % 2

    @pl.when(step == 0)
    def _():
        issue(0, 0)

    @pl.when(step + 1 < total)
    def _():
        issue(step + 1, 1 - slot)

    def drain(i, c):
        copy(0, slot, 0, 0).wait()
        return c

    lax.fori_loop(0, 2 * TM, drain, 0, unroll=8)

    lane = lax.broadcasted_iota(jnp.int32, (TM, LANES), 1)
    ri = ri_ref[...]
    w1 = jnp.sum(jnp.where(lane == 2, ri, 0.0), axis=-1, keepdims=True)
    w2 = jnp.sum(jnp.where(lane == 3, ri, 0.0), axis=-1, keepdims=True)
    y = _from_token_tiles(buf[slot, 0]) * w1 + _from_token_tiles(buf[slot, 1]) * w2
    xn = x_ref[...] + g2_ref[...] * y
    if final:
        ms = jnp.mean(xn * xn, axis=-1, keepdims=True)
        xn = xn * lax.rsqrt(ms + NORM_EPS) * fg_ref[...]
    o_ref[...] = xn


def _combine(dest, ys, xa, rinfo, g2, final_g, has_ctx, final):
    b, s, d = xa.shape
    nt = s // TM
    mod_idx = lambda bi, t, dr: (jnp.where(t == 0, b, bi) if has_ctx else bi, 0, 0)
    tile = lambda wdt: pl.BlockSpec((None, TM, wdt), lambda bi, t, dr: (bi, t, 0))
    out_spec = tile(d)
    out_shape = jax.ShapeDtypeStruct((b, s, d), F32)
    return pl.pallas_call(
        functools.partial(_combine_kernel, final=final),
        grid_spec=pltpu.PrefetchScalarGridSpec(
            num_scalar_prefetch=1,
            grid=(b, nt),
            in_specs=[
                pl.BlockSpec(memory_space=pl.ANY),
                tile(d), tile(LANES),
                pl.BlockSpec((None, 1, d), mod_idx),
                pl.BlockSpec((1, d), lambda bi, t, dr: (0, 0)),
            ],
            out_specs=out_spec,
            scratch_shapes=[pltpu.VMEM((2, 2, TM, SUBLANES, d // SUBLANES), F32),
                            pltpu.SemaphoreType.DMA((2,))],
        ),
        out_shape=out_shape,
        compiler_params=_cparams(("arbitrary", "arbitrary")),
        name="combine",
    )(dest, ys, xa, rinfo, g2, final_g)


def _moe(h2, rinfo, counts, w_gu, w_down, layer):
    b, s, d = h2.shape
    nt = s // TM
    r = rinfo[:, :, :8]
    e = r[..., 0:2].astype(jnp.int32)
    rank = r[..., 4:6].astype(jnp.int32)
    cnt = counts[0, ROUTE_LANE0:ROUTE_LANE0 + MOE_EXPERTS].astype(jnp.int32)
    padded = (cnt + MOE_ROWS - 1) // MOE_ROWS * MOE_ROWS
    ends = jnp.cumsum(padded)
    starts = ends - padded
    eids = jnp.arange(MOE_EXPERTS, dtype=jnp.int32)
    dest = jnp.sum(jnp.where(e[..., None] == eids, starts, 0), axis=-1) + rank
    dest = dest.reshape(b, nt, TM, 2).transpose(0, 1, 3, 2).reshape(b * nt, 2 * TM)
    n_assign = b * nt * TM * 2
    n_blocks = (n_assign + MOE_EXPERTS * (MOE_ROWS - 1)) // MOE_ROWS + 1
    n_pad = n_blocks * MOE_ROWS
    blk_row = jnp.arange(n_blocks, dtype=jnp.int32) * MOE_ROWS
    block_e = jnp.minimum(jnp.sum((blk_row[:, None] >= ends[None, :]).astype(jnp.int32), axis=1),
                          MOE_EXPERTS - 1)
    n_used = (ends[-1] // MOE_ROWS).astype(jnp.int32).reshape(1)
    xs = _dispatch(dest, h2, n_pad)
    ys = _expert_mlp(block_e, n_used, xs, w_gu, w_down, layer)
    return dest, ys


def _router_weights(wg, bg, we, be):
    d = wg.shape[0]
    pad = LANES - MOE_GROUPS - MOE_EXPERTS
    assert ROUTE_LANE0 == MOE_GROUPS
    wr = jnp.concatenate([wg.astype(F32), we.astype(F32), jnp.zeros((d, pad), F32)], axis=1)
    br = jnp.concatenate([bg.astype(F32), be.astype(F32), jnp.zeros((pad,), F32)]).reshape(1, LANES)
    return wr, br


def kernel(x, c, ctx, c_ctx, ada_w, ada_b, norm_g, final_g, ab_w_in, ab_w_out, ret_decay, ret_gn, na_rpb,
           swa_w_in, swa_w_out, swa_sink, router_g_w, router_g_b, router_e_w, router_e_b,
           expert_w_gu, expert_w_down):
    b, seq, d = x.shape
    assert ctx.shape[1] == TM and seq % TM == 0 and d == D_MODEL
    xa = jnp.concatenate([ctx, x], axis=1)
    rope = _rope_tables(seq)

    cvec = jnp.concatenate([c, c_ctx[None, :], jnp.zeros((7, d), F32)], axis=0)
    mod = _adaln(cvec, ada_w, ada_b)
    mod = mod.reshape(DEPTH, b + 8, 6, 1, d)[:, :b + 1].transpose(0, 2, 1, 3, 4)

    for layer in range(DEPTH):
        last = layer == DEPTH - 1
        sh1, sc1, g1, sh2, sc2, g2 = (mod[layer, i] for i in range(6))
        gain1 = norm_g[layer, 0].reshape(1, d)
        gain2 = norm_g[layer, 1].reshape(1, d)
        j = layer // 2
        if layer % 2 == 0:
            nb = RET_W // LANES
            qk_scale = HEAD_DIM ** -0.5
            scales = {blk: qk_scale for blk in range(nb, 2 * nb)}
            scales.update({blk: qk_scale * LOG2E for blk in range(4 * nb, 5 * nb)})
            p = _proj(xa, gain1, sh1, sc1, ab_w_in[j].astype(BF16), rope,
                      rope_blocks=range(0, 2 * nb), scales=scales)
            log_gamma = jnp.log1p(-jnp.exp2(-ret_decay[j].astype(F32)))
            mixes = [_retention(p, log_gamma, ret_gn[j]), _neighbourhood(p, _na_bias(na_rpb[j]))]
            w_out = ab_w_out[j]
        else:
            qb = SWA_HEADS * HEAD_DIM // LANES
            kb = SWA_KV_HEADS * HEAD_DIM // LANES
            p = _proj(xa, gain1, sh1, sc1, swa_w_in[j].astype(BF16), rope,
                      rope_blocks=range(0, qb + kb),
                      scales={blk: HEAD_DIM ** -0.5 * LOG2E for blk in range(qb)})
            mixes = [_swa(p, swa_sink[j].astype(F32))]
            w_out = swa_w_out[j]
        t0 = 1 if last else 0
        wr, br = _router_weights(router_g_w[layer], router_g_b[layer], router_e_w[layer], router_e_b[layer])
        xa, h2, rinfo, counts = _outproj(mixes, w_out.astype(BF16), xa, g1, gain2, sh2, sc2, wr, br, t0)
        dest, ys = _moe(h2, rinfo, counts, expert_w_gu, expert_w_down, layer)
        xa = _combine(dest, ys, xa, rinfo, g2, final_g.reshape(1, d), not last, last)
    return xa
```

```python
import functools

import numpy as np
import jax
import jax.numpy as jnp
from jax import lax
from jax.experimental import pallas as pl
from jax.experimental.pallas import tpu as pltpu

F32 = jnp.float32
BF16 = jnp.bfloat16

D_MODEL = 1024
DEPTH = 2
GRID_W = 64
HEAD_DIM = 64
RET_HEADS = 8
NA_HEADS = 8
RET_W = 512
NA_W = 512
AB_IN = 4 * RET_W + 3 * NA_W
RET_CHUNK = 128
GN_EPS = 1e-5
NA_KH = 8
NA_KW = 16
SWA_HEADS = 16
SWA_KV_HEADS = 4
SWA_WINDOW = 128
SWA_BLOCK = 128
SWA_IN = (SWA_HEADS + 2 * SWA_KV_HEADS) * HEAD_DIM
ROPE_BASE = 10000.0
MOE_GROUPS = 4
MOE_EPG = 8
MOE_EXPERTS = 32
MOE_FF = 512
NORM_EPS = 1e-6
NEG_INF = -1e30

LANES = 128
TM = 256
RET_BLOCK = 256
MOE_ROWS = 512
ROUTE_LANE0 = 4
VMEM_LIMIT = 56 * 1024 * 1024


def _cparams(sem, vmem=VMEM_LIMIT):
    return pltpu.CompilerParams(dimension_semantics=sem, vmem_limit_bytes=vmem)


def _split_bf16(a):
    hi = a.astype(BF16)
    lo = (a - hi.astype(F32)).astype(BF16)
    return hi, lo


def _dot3(a, b):
    ah, al = _split_bf16(a)
    bh, bl = _split_bf16(b)
    d = lambda x, y: jnp.dot(x, y, preferred_element_type=F32)
    return d(ah, bh) + (d(ah, bl) + d(al, bh))


def _dot_nt(a, b):
    return lax.dot_general(a, b, (((1,), (1,)), ((), ())), preferred_element_type=F32)


def _dot_tn(a, b):
    return lax.dot_general(a, b, (((0,), (0,)), ((), ())), preferred_element_type=F32)


def _silu(x):
    return x / (1.0 + jnp.exp(-x))


def _adaln_kernel(c_ref, w_ref, b_ref, o_ref):
    o_ref[...] = _dot3(_silu(c_ref[...]), w_ref[...]) + b_ref[...]


def _adaln(cvec, ada_w, ada_b):
    depth, d, n6 = ada_w.shape
    rows = cvec.shape[0]
    tn = 1024
    return pl.pallas_call(
        _adaln_kernel,
        grid=(depth, n6 // tn),
        in_specs=[
            pl.BlockSpec((rows, d), lambda l, j: (0, 0)),
            pl.BlockSpec((None, d, tn), lambda l, j: (l, 0, j)),
            pl.BlockSpec((None, 1, tn), lambda l, j: (l, 0, j)),
        ],
        out_specs=pl.BlockSpec((None, rows, tn), lambda l, j: (l, 0, j)),
        out_shape=jax.ShapeDtypeStruct((depth, rows, n6), F32),
        compiler_params=_cparams(("arbitrary", "arbitrary")),
        name="adaln",
    )(cvec, ada_w, ada_b.reshape(depth, 1, n6))


def _rms_mod(x, g, sh, sc):
    ms = jnp.mean(x * x, axis=-1, keepdims=True)
    return (x * lax.rsqrt(ms + NORM_EPS) * g) * (1.0 + sc) + sh


def _proj_kernel(x_ref, g_ref, sh_ref, sc_ref, w_ref, rope_ref, o_ref, *, rope_blocks, scales, cn):
    is_lat = pl.program_id(1) > 0
    hb = _rms_mod(x_ref[...], g_ref[...], sh_ref[...], sc_ref[...]).astype(BF16)
    nout = w_ref.shape[1]
    for c in range(nout // cn):
        o = jnp.dot(hb, w_ref[:, c * cn:(c + 1) * cn], preferred_element_type=F32)
        for s in range(cn // LANES):
            blk = c * (cn // LANES) + s
            ob = o[:, s * LANES:(s + 1) * LANES]
            if blk in rope_blocks:
                r = (ob * rope_ref[0] + pltpu.roll(ob, 16, 1) * rope_ref[1]
                     + pltpu.roll(ob, LANES - 16, 1) * rope_ref[2])
                ob = jnp.where(is_lat, r, ob)
            if blk in scales:
                ob = ob * scales[blk]
            o_ref[:, blk * LANES:(blk + 1) * LANES] = ob


def _proj(xa, gain, sh, sc, w_bf16, rope, rope_blocks, scales):
    b, s, d = xa.shape
    nout = w_bf16.shape[1]
    nt = s // TM
    mod_idx = lambda bi, t: (jnp.where(t == 0, b, bi), 0, 0)
    kern = functools.partial(_proj_kernel, rope_blocks=frozenset(rope_blocks), scales=dict(scales), cn=512)
    return pl.pallas_call(
        kern,
        grid=(b, nt),
        in_specs=[
            pl.BlockSpec((None, TM, d), lambda bi, t: (bi, t, 0)),
            pl.BlockSpec((1, d), lambda bi, t: (0, 0)),
            pl.BlockSpec((None, 1, d), mod_idx),
            pl.BlockSpec((None, 1, d), mod_idx),
            pl.BlockSpec((d, nout), lambda bi, t: (0, 0)),
            pl.BlockSpec((3, TM, LANES), lambda bi, t: (0, jnp.maximum(t - 1, 0), 0)),
        ],
        out_specs=pl.BlockSpec((None, TM, nout), lambda bi, t: (bi, t, 0)),
        out_shape=jax.ShapeDtypeStruct((b, s, nout), F32),
        compiler_params=_cparams(("arbitrary", "arbitrary")),
        name="proj",
    )(xa, gain, sh, sc, w_bf16, rope)


def _rope_tables(seq):
    nf = HEAD_DIM // 4
    inv = ROPE_BASE ** (-jnp.arange(nf, dtype=F32) / nf)
    t = jnp.arange(seq)
    row = (t // GRID_W).astype(F32)
    col = (t % GRID_W).astype(F32)
    lane = np.arange(LANES)
    jj = lane % HEAD_DIM
    axis_is_col = (jj // 32) == 1
    second_half = (jj % 32) >= 16
    f = jj % 16
    pos = jnp.where(axis_is_col[None, :], col[:, None], row[:, None])
    ang = pos * inv[f][None, :]
    c, s = jnp.cos(ang), jnp.sin(ang)
    sa = jnp.where(second_half[None, :], s, 0.0)
    sb = jnp.where(second_half[None, :], 0.0, -s)
    return jnp.stack([c, sa, sb], axis=0)


def _ret_kernel(lg_ref, q_ref, k_ref, v_ref, g_ref, gn_ref, o_ref, accf_ref, accb_ref,
                intra_ref, qd_ref, kd_ref):
    hp = pl.program_id(1)
    c = RET_BLOCK
    s_len = q_ref.shape[0]
    n_chunks = s_len // c
    ctx_chunks = TM // c
    pos = lax.broadcasted_iota(jnp.int32, (c, LANES), 0).astype(F32)
    m0 = lax.broadcasted_iota(jnp.int32, (c, LANES), 1) < HEAD_DIM
    same_head = ((lax.broadcasted_iota(jnp.int32, (LANES, LANES), 0) < HEAD_DIM)
                 == (lax.broadcasted_iota(jnp.int32, (LANES, LANES), 1) < HEAD_DIM))
    rel = (lax.broadcasted_iota(jnp.int32, (c, c), 0) - lax.broadcasted_iota(jnp.int32, (c, c), 1)).astype(F32)
    lgf = [lg_ref[0, hp * 2 + hh] for hh in range(2)]
    lgb = [lg_ref[1, hp * 2 + hh] for hh in range(2)]
    lgf_l = jnp.where(m0, lgf[0], lgf[1])
    lgb_l = jnp.where(m0, lgb[0], lgb[1])
    for hh in range(2):
        intra_ref[0, :, hh * c:(hh + 1) * c] = jnp.where(rel >= 0, jnp.exp(lgf[hh] * jnp.maximum(rel, 0.0)), 0.0)
        intra_ref[1, :, hh * c:(hh + 1) * c] = jnp.where(rel <= 0, jnp.exp(lgb[hh] * jnp.maximum(-rel, 0.0)), 0.0)
    qd_ref[0] = jnp.exp(lgf_l * (pos + 1.0))
    qd_ref[1] = jnp.exp(lgb_l * (c - pos))
    kd_ref[0] = jnp.exp(lgf_l * (c - 1.0 - pos))
    kd_ref[1] = jnp.exp(lgb_l * pos)
    cd = [jnp.exp(lgf_l[0:1] * float(c)), jnp.exp(lgb_l[0:1] * float(c))]

    def chunk(r0, state, d):
        q = q_ref[pl.ds(r0, c), :]
        k = k_ref[pl.ds(r0, c), :]
        v = v_ref[pl.ds(r0, c), :]
        qb = q.astype(BF16)
        kcat = jnp.concatenate([jnp.where(m0, k, 0.0), jnp.where(m0, 0.0, k)], axis=0).astype(BF16)
        vcat = jnp.concatenate([jnp.where(m0, v, 0.0), jnp.where(m0, 0.0, v)], axis=0).astype(BF16)
        s = _dot_nt(qb, kcat) * intra_ref[d]
        out = jnp.dot(s.astype(BF16), vcat, preferred_element_type=F32)
        out = out + jnp.dot(qb, state.astype(BF16), preferred_element_type=F32) * qd_ref[d]
        kv = _dot_tn((k * kd_ref[d]).astype(BF16), v.astype(BF16))
        return out, state * cd[d] + jnp.where(same_head, kv, 0.0)

    def body(i, states):
        sf, sb = states
        rf = pl.multiple_of(i * c, c)
        ib = jnp.where(i < ctx_chunks, ctx_chunks - 1 - i, n_chunks + ctx_chunks - 1 - i)
        rb = pl.multiple_of(ib * c, c)
        of, sf = chunk(rf, sf, 0)
        ob, sb = chunk(rb, sb, 1)
        accf_ref[pl.ds(rf, c), :] = of
        accb_ref[pl.ds(rb, c), :] = ob
        return sf, sb

    z = jnp.zeros((LANES, LANES), F32)
    lax.fori_loop(0, n_chunks, body, (z, z), unroll=3)

    avg = jnp.where(same_head, 1.0 / HEAD_DIM, 0.0).astype(BF16)

    def head_mean(x):
        hi, lo = _split_bf16(x)
        return (jnp.dot(hi, avg, preferred_element_type=F32) + jnp.dot(lo, avg, preferred_element_type=F32))

    def readout(i, carry):
        r0 = pl.multiple_of(i * c, c)
        o = accf_ref[pl.ds(r0, c), :] + accb_ref[pl.ds(r0, c), :]
        dlt = o - head_mean(o)
        var = head_mean(dlt * dlt)
        y = dlt * lax.rsqrt(var + GN_EPS) * gn_ref[...]
        o_ref[pl.ds(r0, c), :] = _silu(g_ref[pl.ds(r0, c), :]) * y
        return carry

    lax.fori_loop(0, n_chunks, readout, 0, unroll=2)


def _retention(p, log_gamma, ret_gn):
    b, s, _ = p.shape
    nb = RET_W // LANES
    blk = lambda off: pl.BlockSpec((None, s, LANES), lambda bi, hp: (bi, 0, off + hp))
    return pl.pallas_call(
        _ret_kernel,
        grid=(b, nb),
        in_specs=[
            pl.BlockSpec(memory_space=pltpu.SMEM),
            blk(0), blk(nb), blk(2 * nb), blk(3 * nb),
            pl.BlockSpec((1, LANES), lambda bi, hp: (0, hp)),
        ],
        out_specs=pl.BlockSpec((None, s, LANES), lambda bi, hp: (bi, 0, hp)),
        out_shape=jax.ShapeDtypeStruct((b, s, RET_W), F32),
        scratch_shapes=[pltpu.VMEM((s, LANES), F32), pltpu.VMEM((s, LANES), F32),
                        pltpu.VMEM((2, RET_BLOCK, 2 * RET_BLOCK), F32),
                        pltpu.VMEM((2, RET_BLOCK, LANES), F32), pltpu.VMEM((2, RET_BLOCK, LANES), F32)],
        compiler_params=_cparams(("arbitrary", "arbitrary")),
        name="retention",
    )(log_gamma, p, p, p, p, ret_gn.reshape(1, RET_W))


LOG2E = 1.4426950408889634


def _softmax_pv(s_list, v_list, extra=None):
    m = None
    for s in s_list:
        for j in range(s.shape[1] // LANES):
            blk = s[:, j * LANES:(j + 1) * LANES]
            m = blk if m is None else jnp.maximum(m, blk)
    m = m.max(axis=-1, keepdims=True)
    if extra is not None:
        m = jnp.maximum(m, extra)
    acc = None
    for s, v in zip(s_list, v_list):
        pv = jnp.dot(jnp.exp2(s - m).astype(BF16), v, preferred_element_type=F32)
        acc = pv if acc is None else acc + pv
    o, den = acc[:, :LANES], acc[:, LANES:]
    if extra is not None:
        den = den + jnp.exp2(extra - m)
    return o / den


def _stage_heads(q_ref, k_ref, v_ref, qm_ref, kb_ref, vb_ref):
    m0 = lax.broadcasted_iota(jnp.int32, (TM, LANES), 1) < HEAD_DIM

    def stage(i, carry):
        r0 = pl.multiple_of(i * TM, TM)
        q = q_ref[pl.ds(r0, TM), :]
        qm_ref[0, pl.ds(r0, TM), :] = jnp.where(m0, q, 0.0).astype(BF16)
        qm_ref[1, pl.ds(r0, TM), :] = jnp.where(m0, 0.0, q).astype(BF16)
        kb_ref[pl.ds(r0, TM), :] = k_ref[pl.ds(r0, TM), :].astype(BF16)
        vb_ref[pl.ds(r0, TM), 0:LANES] = v_ref[pl.ds(r0, TM), :].astype(BF16)
        vb_ref[pl.ds(r0, TM), LANES:2 * LANES] = jnp.ones((TM, LANES), BF16)
        return carry

    lax.fori_loop(0, q_ref.shape[0] // TM, stage, 0)


def _na_kernel(q_ref, k_ref, v_ref, bias_ref, o_ref, qm_ref, kb_ref, vb_ref):
    s_len = q_ref.shape[0]
    rows = (s_len - TM) // GRID_W
    nloc = NA_KH * GRID_W
    _stage_heads(q_ref, k_ref, v_ref, qm_ref, kb_ref, vb_ref)

    kc = kb_ref[0:TM, :]
    vc = vb_ref[0:TM, :]
    outs = [_softmax_pv([_dot_nt(qm_ref[hh, 0:TM, :], kc)], [vc]) for hh in range(2)]
    m0c = lax.broadcasted_iota(jnp.int32, (TM, LANES), 1) < HEAD_DIM
    o_ref[0:TM, :] = jnp.where(m0c, outs[0], outs[1])

    m0 = lax.broadcasted_iota(jnp.int32, (GRID_W, LANES), 1) < HEAD_DIM

    def row_block(r, carry):
        rs = jnp.clip(r - NA_KH // 2, 0, rows - NA_KH)
        pat = r - rs
        q0 = pl.multiple_of(TM + r * GRID_W, GRID_W)
        k0 = pl.multiple_of(TM + rs * GRID_W, GRID_W)
        kl = kb_ref[pl.ds(k0, nloc), :]
        vl = vb_ref[pl.ds(k0, nloc), :]
        kc = kb_ref[0:TM, :]
        vc = vb_ref[0:TM, :]
        q = jnp.concatenate([qm_ref[0, pl.ds(q0, GRID_W), :], qm_ref[1, pl.ds(q0, GRID_W), :]], axis=0)
        s_loc = _dot_nt(q, kl) + bias_ref[pat].reshape(2 * GRID_W, nloc)
        s_ctx = _dot_nt(q, kc)
        res = _softmax_pv([s_loc, s_ctx], [vl, vc])
        o_ref[pl.ds(q0, GRID_W), :] = jnp.where(m0, res[:GRID_W], res[GRID_W:])
        return carry

    lax.fori_loop(0, rows, row_block, 0, unroll=4)


def _na_bias(rpb):
    h = rpb.shape[0]
    qc = np.arange(GRID_W)[:, None]
    kc = np.arange(GRID_W)[None, :]
    win = np.clip(qc - NA_KW // 2, 0, GRID_W - NA_KW)
    valid = (kc >= win) & (kc < win + NA_KW)
    col_off = np.clip(kc - qc + NA_KW - 1, 0, 2 * NA_KW - 2)
    pat = np.arange(NA_KH)[:, None]
    row_off = np.arange(NA_KH)[None, :] - pat + NA_KH - 1
    bias = rpb[:, row_off][:, :, :, col_off]
    bias = jnp.where(valid[None, None, None], bias.astype(F32) * LOG2E, NEG_INF)
    bias = bias.transpose(1, 0, 3, 2, 4)
    return bias.reshape(NA_KH, h, GRID_W, NA_KH * GRID_W)


def _neighbourhood(p, bias):
    b, s, _ = p.shape
    nb = NA_W // LANES
    c0 = 4 * RET_W // LANES
    blk = lambda off: pl.BlockSpec((None, s, LANES), lambda bi, hp: (bi, 0, c0 + off + hp))
    return pl.pallas_call(
        _na_kernel,
        grid=(b, nb),
        in_specs=[
            blk(0), blk(nb), blk(2 * nb),
            pl.BlockSpec((NA_KH, 2, GRID_W, NA_KH * GRID_W), lambda bi, hp: (0, hp, 0, 0)),
        ],
        out_specs=pl.BlockSpec((None, s, LANES), lambda bi, hp: (bi, 0, hp)),
        out_shape=jax.ShapeDtypeStruct((b, s, NA_W), F32),
        scratch_shapes=[pltpu.VMEM((2, s, LANES), BF16), pltpu.VMEM((s, LANES), BF16),
                        pltpu.VMEM((s, 2 * LANES), BF16)],
        compiler_params=_cparams(("arbitrary", "arbitrary")),
        name="neighbourhood",
    )(p, p, p, bias)


def _swa_kernel(sink_ref, q_ref, k_ref, v_ref, o_ref, kd_ref, vd_ref):
    kp = pl.program_id(1)
    s_len = q_ref.shape[0]
    seq = s_len - TM
    nblk = seq // SWA_BLOCK
    band = SWA_BLOCK + 2 * SWA_WINDOW
    group = SWA_HEADS // SWA_KV_HEADS
    heads_per_step = 2 * group
    m0t = lax.broadcasted_iota(jnp.int32, (TM, LANES), 1) < HEAD_DIM

    def stage(i, carry):
        r0 = pl.multiple_of(i * TM, TM)
        for src, dst in ((k_ref, kd_ref), (v_ref, vd_ref)):
            x = src[pl.ds(r0, TM), :]
            xr = pltpu.roll(x, HEAD_DIM, 1)
            dst[0, pl.ds(r0, TM), 0:LANES] = jnp.where(m0t, x, xr).astype(BF16)
            dst[1, pl.ds(r0, TM), 0:LANES] = jnp.where(m0t, xr, x).astype(BF16)
        for h in range(2):
            vd_ref[h, pl.ds(r0, TM), LANES:2 * LANES] = jnp.ones((TM, LANES), BF16)
        return carry

    lax.fori_loop(0, s_len // TM, stage, 0)
    o_ref[0:TM, :] = jnp.zeros((TM, o_ref.shape[1]), F32)

    rows = group * SWA_BLOCK
    qi = lax.broadcasted_iota(jnp.int32, (rows, band), 0) % SWA_BLOCK
    ki = lax.broadcasted_iota(jnp.int32, (rows, band), 1)
    m0 = lax.broadcasted_iota(jnp.int32, (SWA_BLOCK, LANES), 1) < HEAD_DIM
    head_of_row = lax.broadcasted_iota(jnp.int32, (rows, 1), 0) // SWA_BLOCK

    def block(i, carry):
        start = jnp.clip((i - 1) * SWA_BLOCK, 0, seq - band)
        k0 = pl.multiple_of(TM + start, SWA_BLOCK)
        q0 = pl.multiple_of(TM + i * SWA_BLOCK, SWA_BLOCK)
        valid = jnp.abs(qi + (i * SWA_BLOCK - start) - ki) <= SWA_WINDOW
        for hh in range(2):
            kb = kd_ref[hh, pl.ds(k0, band), :]
            vb = vd_ref[hh, pl.ds(k0, band), :]
            kc = kd_ref[hh, 0:TM, :]
            vc = vd_ref[hh, 0:TM, :]
            parts = []
            for j in range(group // 2):
                pair = hh * (group // 2) + j
                qp = q_ref[pl.ds(q0, SWA_BLOCK), pair * LANES:(pair + 1) * LANES]
                parts += [jnp.where(m0, qp, 0.0), jnp.where(m0, 0.0, qp)]
            q = jnp.concatenate(parts, axis=0).astype(BF16)
            s = jnp.where(valid, _dot_nt(q, kb), NEG_INF)
            s_ctx = _dot_nt(q, kc)
            h0 = kp * heads_per_step + hh * group
            sink = jnp.full((rows, 1), sink_ref[h0], F32)
            for g in range(1, group):
                sink = jnp.where(head_of_row == g, sink_ref[h0 + g], sink)
            res = _softmax_pv([s, s_ctx], [vb, vc], extra=sink * LOG2E)
            for j in range(group // 2):
                pair = hh * (group // 2) + j
                r0 = 2 * j * SWA_BLOCK
                o_ref[pl.ds(q0, SWA_BLOCK), pair * LANES:(pair + 1) * LANES] = jnp.where(
                    m0, res[r0:r0 + SWA_BLOCK], res[r0 + SWA_BLOCK:r0 + 2 * SWA_BLOCK])
        return carry

    lax.fori_loop(0, nblk, block, 0, unroll=2)


def _swa(p, sink):
    b, s, _ = p.shape
    qw = SWA_HEADS * HEAD_DIM // 2
    kblk = SWA_HEADS * HEAD_DIM // LANES
    vblk = kblk + SWA_KV_HEADS * HEAD_DIM // LANES
    return pl.pallas_call(
        _swa_kernel,
        grid=(b, 2),
        in_specs=[
            pl.BlockSpec(memory_space=pltpu.SMEM),
            pl.BlockSpec((None, s, qw), lambda bi, kp: (bi, 0, kp)),
            pl.BlockSpec((None, s, LANES), lambda bi, kp: (bi, 0, kblk + kp)),
            pl.BlockSpec((None, s, LANES), lambda bi, kp: (bi, 0, vblk + kp)),
        ],
        out_specs=pl.BlockSpec((None, s, qw), lambda bi, kp: (bi, 0, kp)),
        out_shape=jax.ShapeDtypeStruct((b, s, SWA_HEADS * HEAD_DIM), F32),
        scratch_shapes=[pltpu.VMEM((2, s, LANES), BF16), pltpu.VMEM((2, s, 2 * LANES), BF16)],
        compiler_params=_cparams(("arbitrary", "arbitrary")),
        name="swa",
    )(sink, p, p, p)


def _outproj_kernel(*refs, n_mix):
    mix_refs = refs[:n_mix]
    (w_ref, x_ref, g1_ref, gn_ref, sh_ref, sc_ref, wr_ref, br_ref,
     xo_ref, h_ref, ri_ref, cnt_ref, carry_ref) = refs[n_mix:]
    first = (pl.program_id(0) == 0) & (pl.program_id(1) == 0)

    @pl.when(first)
    def _():
        carry_ref[...] = jnp.zeros_like(carry_ref)

    o = None
    off = 0
    for m_ref in mix_refs:
        kw = m_ref.shape[1]
        part = jnp.dot(m_ref[...].astype(BF16), w_ref[off:off + kw, :], preferred_element_type=F32)
        o = part if o is None else o + part
        off += kw
    xn = x_ref[...] + g1_ref[...] * o
    xo_ref[...] = xn
    h = _rms_mod(xn, gn_ref[...], sh_ref[...], sc_ref[...])
    h_ref[...] = h

    logits = _dot3(h, wr_ref[...]) + br_ref[...]
    tm = logits.shape[0]
    lane = lax.broadcasted_iota(jnp.int32, (tm, LANES), 1).astype(F32)
    big = 1e9
    gmask = lane < MOE_GROUPS
    mg = jnp.max(jnp.where(gmask, logits, -big), axis=-1, keepdims=True)
    sg = jnp.sum(jnp.where(gmask, jnp.exp(jnp.minimum(logits - mg, 0.0)), 0.0), axis=-1, keepdims=True)
    gw = 1.0 / sg
    gi = jnp.min(jnp.where(gmask & (logits == mg), lane, big), axis=-1, keepdims=True)
    lo = ROUTE_LANE0 + MOE_EPG * gi
    emask = (lane >= lo) & (lane < lo + MOE_EPG)
    l1 = jnp.max(jnp.where(emask, logits, -big), axis=-1, keepdims=True)
    i1 = jnp.min(jnp.where(emask & (logits == l1), lane, big), axis=-1, keepdims=True)
    emask2 = emask & (lane != i1)
    l2 = jnp.max(jnp.where(emask2, logits, -big), axis=-1, keepdims=True)
    i2 = jnp.min(jnp.where(emask2 & (logits == l2), lane, big), axis=-1, keepdims=True)
    e21 = jnp.exp(l2 - l1)
    w1 = gw / (1.0 + e21)
    w2 = gw * e21 / (1.0 + e21)

    oh = jnp.where((lane == i1) | (lane == i2), 1.0, 0.0)
    tri = (lax.broadcasted_iota(jnp.int32, (tm, tm), 0) > lax.broadcasted_iota(jnp.int32, (tm, tm), 1))
    cum = jnp.dot(jnp.where(tri, 1.0, 0.0).astype(BF16), oh.astype(BF16), preferred_element_type=F32) + carry_ref[...]
    r1 = jnp.sum(jnp.where(lane == i1, cum, 0.0), axis=-1, keepdims=True)
    r2 = jnp.sum(jnp.where(lane == i2, cum, 0.0), axis=-1, keepdims=True)
    carry_ref[...] = carry_ref[...] + jnp.sum(oh, axis=0, keepdims=True)
    cnt_ref[...] = carry_ref[...]

    ri = jnp.where(lane == 0, i1 - ROUTE_LANE0, 0.0)
    ri = jnp.where(lane == 1, i2 - ROUTE_LANE0, ri)
    ri = jnp.where(lane == 2, w1, ri)
    ri = jnp.where(lane == 3, w2, ri)
    ri = jnp.where(lane == 4, r1, ri)
    ri = jnp.where(lane == 5, r2, ri)
    ri_ref[...] = ri


def _outproj(mixes, w_bf16, xa, g1, gain2, sh2, sc2, wr, br, t0):
    b, s, d = xa.shape
    nt = s // TM - t0
    so = nt * TM
    mod_idx = lambda bi, t: (jnp.where(t + t0 == 0, b, bi), 0, 0)
    tile = lambda wdt: pl.BlockSpec((None, TM, wdt), lambda bi, t: (bi, t + t0, 0))
    otile = lambda wdt: pl.BlockSpec((None, TM, wdt), lambda bi, t: (bi, t, 0))
    const = lambda shape: pl.BlockSpec(shape, lambda bi, t: (0,) * len(shape))
    in_specs = [tile(m.shape[2]) for m in mixes] + [
        const(w_bf16.shape), tile(d),
        pl.BlockSpec((None, 1, d), mod_idx), const((1, d)),
        pl.BlockSpec((None, 1, d), mod_idx), pl.BlockSpec((None, 1, d), mod_idx),
        const((d, LANES)), const((1, LANES)),
    ]
    return pl.pallas_call(
        functools.partial(_outproj_kernel, n_mix=len(mixes)),
        grid=(b, nt),
        in_specs=in_specs,
        out_specs=[otile(d), otile(d), otile(LANES), const((1, LANES))],
        out_shape=[jax.ShapeDtypeStruct((b, so, d), F32), jax.ShapeDtypeStruct((b, so, d), F32),
                   jax.ShapeDtypeStruct((b, so, LANES), F32), jax.ShapeDtypeStruct((1, LANES), F32)],
        scratch_shapes=[pltpu.VMEM((1, LANES), F32)],
        compiler_params=_cparams(("arbitrary", "arbitrary")),
        name="outproj",
    )(*mixes, w_bf16, xa, g1, gain2, sh2, sc2, wr, br)


SUBLANES = 8


def _to_token_tiles(x):
    return x.reshape(x.shape[0], SUBLANES, x.shape[1] // SUBLANES)


def _from_token_tiles(x3):
    return x3.reshape(x3.shape[0], x3.shape[1] * x3.shape[2])


def _dispatch_kernel(dest_ref, h_ref, xs_in_ref, xs_ref, hbuf, sem):
    del xs_in_ref
    nt = pl.num_programs(1)
    step = pl.program_id(0) * nt + pl.program_id(1)
    last = pl.num_programs(0) * nt - 1
    slot = step % 2
    hbuf[slot] = _to_token_tiles(h_ref[...])

    def copy(sl, i, dst_row):
        return pltpu.make_async_copy(hbuf.at[sl, i], xs_ref.at[dst_row], sem.at[sl])

    for i in range(TM):
        copy(slot, i, dest_ref[step, i]).start()
        copy(slot, i, dest_ref[step, TM + i]).start()

    def drain(sl):
        def one(i, c):
            copy(sl, 0, 0).wait()
            return c
        lax.fori_loop(0, 2 * TM, one, 0, unroll=8)

    @pl.when(step > 0)
    def _():
        drain(1 - slot)

    @pl.when(step == last)
    def _():
        drain(slot)


def _dispatch(dest, h2, n_pad):
    b, s, d = h2.shape
    nt = s // TM
    xs0 = jnp.zeros((n_pad, SUBLANES, d // SUBLANES), F32)
    return pl.pallas_call(
        _dispatch_kernel,
        grid_spec=pltpu.PrefetchScalarGridSpec(
            num_scalar_prefetch=1,
            grid=(b, nt),
            in_specs=[pl.BlockSpec((None, TM, d), lambda bi, t, dr: (bi, t, 0)),
                      pl.BlockSpec(memory_space=pl.ANY)],
            out_specs=pl.BlockSpec(memory_space=pl.ANY),
            scratch_shapes=[pltpu.VMEM((2, TM, SUBLANES, d // SUBLANES), F32), pltpu.SemaphoreType.DMA((2,))],
        ),
        out_shape=jax.ShapeDtypeStruct(xs0.shape, F32),
        input_output_aliases={2: 0},
        compiler_params=_cparams(("arbitrary", "arbitrary")),
        name="dispatch",
    )(dest, h2, xs0)


def _mlp_kernel(be_ref, nu_ref, x_ref, wgu_ref, wd_ref, y_ref, wgu_b, wd_b):
    i = pl.program_id(0)
    prev = be_ref[jnp.maximum(i - 1, 0)]
    used = i < nu_ref[0]

    @pl.when(used & ((i == 0) | (be_ref[i] != prev)))
    def _():
        wgu_b[...] = wgu_ref[...].astype(BF16)
        wd_b[...] = wd_ref[...].astype(BF16)

    @pl.when(used)
    def _():
        x = _from_token_tiles(x_ref[...])
        gu = jnp.dot(x.astype(BF16), wgu_b[...], preferred_element_type=F32)
        act = _silu(gu[:, :MOE_FF]) * gu[:, MOE_FF:]
        y_ref[...] = _to_token_tiles(jnp.dot(act.astype(BF16), wd_b[...], preferred_element_type=F32))

    @pl.when(jnp.logical_not(used))
    def _():
        y_ref[...] = jnp.zeros_like(y_ref)


def _expert_mlp(block_e, n_used, xs, w_gu, w_down, layer):
    n_pad, sub, dl = xs.shape
    d = sub * dl
    ff2 = w_gu.shape[-1]
    slots = pl.BlockSpec((MOE_ROWS, sub, dl), lambda i, be, nu: (i, 0, 0))
    return pl.pallas_call(
        _mlp_kernel,
        grid_spec=pltpu.PrefetchScalarGridSpec(
            num_scalar_prefetch=2,
            grid=(n_pad // MOE_ROWS,),
            in_specs=[
                slots,
                pl.BlockSpec((None, None, d, ff2), lambda i, be, nu: (layer, be[i], 0, 0)),
                pl.BlockSpec((None, None, ff2 // 2, d), lambda i, be, nu: (layer, be[i], 0, 0)),
            ],
            out_specs=slots,
            scratch_shapes=[pltpu.VMEM((d, ff2), BF16), pltpu.VMEM((ff2 // 2, d), BF16)],
        ),
        out_shape=jax.ShapeDtypeStruct(xs.shape, F32),
        compiler_params=_cparams(("arbitrary",)),
        name="expert_mlp",
    )(block_e, n_used, xs, w_gu, w_down)


def _combine_kernel(dest_ref, ys_ref, x_ref, ri_ref, g2_ref, fg_ref, o_ref, buf, sem, *, final):
    bi = pl.program_id(0)
    t = pl.program_id(1)
    nt = pl.num_programs(1)
    step = bi * nt + t
    total = pl.num_programs(0) * nt

    def copy(src_row, slot, k, i):
        return pltpu.make_async_copy(ys_ref.at[src_row], buf.at[slot, k, i], sem.at[slot])

    def issue(st, slot):
        for i in range(TM):
            copy(dest_ref[st, i], slot, 0, i).start()
            copy(dest_ref[st, TM + i], slot, 1, i).start()

    slot = step % 2

    @pl.when(step == 0)
    def _():
        issue(0, 0)

    @pl.when(step + 1 < total)
    def _():
        issue(step + 1, 1 - slot)

    def drain(i, c):
        copy(0, slot, 0, 0).wait()
        return c

    lax.fori_loop(0, 2 * TM, drain, 0, unroll=8)

    lane = lax.broadcasted_iota(jnp.int32, (TM, LANES), 1)
    ri = ri_ref[...]
    w1 = jnp.sum(jnp.where(lane == 2, ri, 0.0), axis=-1, keepdims=True)
    w2 = jnp.sum(jnp.where(lane == 3, ri, 0.0), axis=-1, keepdims=True)
    y = _from_token_tiles(buf[slot, 0]) * w1 + _from_token_tiles(buf[slot, 1]) * w2
    xn = x_ref[...] + g2_ref[...] * y
    if final:
        ms = jnp.mean(xn * xn, axis=-1, keepdims=True)
        xn = xn * lax.rsqrt(ms + NORM_EPS) * fg_ref[...]
    o_ref[...] = xn


def _combine(dest, ys, xa, rinfo, g2, final_g, has_ctx, final):
    b, s, d = xa.shape
    nt = s // TM
    mod_idx = lambda bi, t, dr: (jnp.where(t == 0, b, bi) if has_ctx else bi, 0, 0)
    tile = lambda wdt: pl.BlockSpec((None, TM, wdt), lambda bi, t, dr: (bi, t, 0))
    out_spec = tile(d)
    out_shape = jax.ShapeDtypeStruct((b, s, d), F32)
    return pl.pallas_call(
        functools.partial(_combine_kernel, final=final),
        grid_spec=pltpu.PrefetchScalarGridSpec(
            num_scalar_prefetch=1,
            grid=(b, nt),
            in_specs=[
                pl.BlockSpec(memory_space=pl.ANY),
                tile(d), tile(LANES),
                pl.BlockSpec((None, 1, d), mod_idx),
                pl.BlockSpec((1, d), lambda bi, t, dr: (0, 0)),
            ],
            out_specs=out_spec,
            scratch_shapes=[pltpu.VMEM((2, 2, TM, SUBLANES, d // SUBLANES), F32),
                            pltpu.SemaphoreType.DMA((2,))],
        ),
        out_shape=out_shape,
        compiler_params=_cparams(("arbitrary", "arbitrary")),
        name="combine",
    )(dest, ys, xa, rinfo, g2, final_g)


def _moe(h2, rinfo, counts, w_gu, w_down, layer):
    b, s, d = h2.shape
    nt = s // TM
    r = rinfo[:, :, :8]
    e = r[..., 0:2].astype(jnp.int32)
    rank = r[..., 4:6].astype(jnp.int32)
    cnt = counts[0, ROUTE_LANE0:ROUTE_LANE0 + MOE_EXPERTS].astype(jnp.int32)
    padded = (cnt + MOE_ROWS - 1) // MOE_ROWS * MOE_ROWS
    ends = jnp.cumsum(padded)
    starts = ends - padded
    eids = jnp.arange(MOE_EXPERTS, dtype=jnp.int32)
    dest = jnp.sum(jnp.where(e[..., None] == eids, starts, 0), axis=-1) + rank
    dest = dest.reshape(b, nt, TM, 2).transpose(0, 1, 3, 2).reshape(b * nt, 2 * TM)
    n_assign = b * nt * TM * 2
    n_blocks = (n_assign + MOE_EXPERTS * (MOE_ROWS - 1)) // MOE_ROWS + 1
    n_pad = n_blocks * MOE_ROWS
    blk_row = jnp.arange(n_blocks, dtype=jnp.int32) * MOE_ROWS
    block_e = jnp.minimum(jnp.sum((blk_row[:, None] >= ends[None, :]).astype(jnp.int32), axis=1),
                          MOE_EXPERTS - 1)
    n_used = (ends[-1] // MOE_ROWS).astype(jnp.int32).reshape(1)
    xs = _dispatch(dest, h2, n_pad)
    ys = _expert_mlp(block_e, n_used, xs, w_gu, w_down, layer)
    return dest, ys


def _router_weights(wg, bg, we, be):
    d = wg.shape[0]
    pad = LANES - MOE_GROUPS - MOE_EXPERTS
    assert ROUTE_LANE0 == MOE_GROUPS
    wr = jnp.concatenate([wg.astype(F32), we.astype(F32), jnp.zeros((d, pad), F32)], axis=1)
    br = jnp.concatenate([bg.astype(F32), be.astype(F32), jnp.zeros((pad,), F32)]).reshape(1, LANES)
    return wr, br


def kernel(x, c, ctx, c_ctx, ada_w, ada_b, norm_g, final_g, ab_w_in, ab_w_out, ret_decay, ret_gn, na_rpb,
           swa_w_in, swa_w_out, swa_sink, router_g_w, router_g_b, router_e_w, router_e_b,
           expert_w_gu, expert_w_down):
    b, seq, d = x.shape
    assert ctx.shape[1] == TM and seq % TM == 0 and d == D_MODEL
    xa = jnp.concatenate([ctx, x], axis=1)
    rope = _rope_tables(seq)

    cvec = jnp.concatenate([c, c_ctx[None, :], jnp.zeros((7, d), F32)], axis=0)
    mod = _adaln(cvec, ada_w, ada_b)
    mod = mod.reshape(DEPTH, b + 8, 6, 1, d)[:, :b + 1].transpose(0, 2, 1, 3, 4)

    for layer in range(DEPTH):
        last = layer == DEPTH - 1
        sh1, sc1, g1, sh2, sc2, g2 = (mod[layer, i] for i in range(6))
        gain1 = norm_g[layer, 0].reshape(1, d)
        gain2 = norm_g[layer, 1].reshape(1, d)
        j = layer // 2
        if layer % 2 == 0:
            nb = RET_W // LANES
            qk_scale = HEAD_DIM ** -0.5
            scales = {blk: qk_scale for blk in range(nb, 2 * nb)}
            scales.update({blk: qk_scale * LOG2E for blk in range(4 * nb, 5 * nb)})
            p = _proj(xa, gain1, sh1, sc1, ab_w_in[j].astype(BF16), rope,
                      rope_blocks=range(0, 2 * nb), scales=scales)
            log_gamma = jnp.log1p(-jnp.exp2(-ret_decay[j].astype(F32)))
            mixes = [_retention(p, log_gamma, ret_gn[j]), _neighbourhood(p, _na_bias(na_rpb[j]))]
            w_out = ab_w_out[j]
        else:
            qb = SWA_HEADS * HEAD_DIM // LANES
            kb = SWA_KV_HEADS * HEAD_DIM // LANES
            p = _proj(xa, gain1, sh1, sc1, swa_w_in[j].astype(BF16), rope,
                      rope_blocks=range(0, qb + kb),
                      scales={blk: HEAD_DIM ** -0.5 * LOG2E for blk in range(qb)})
            mixes = [_swa(p, swa_sink[j].astype(F32))]
            w_out = swa_w_out[j]
        t0 = 1 if last else 0
        wr, br = _router_weights(router_g_w[layer], router_g_b[layer], router_e_w[layer], router_e_b[layer])
        xa, h2, rinfo, counts = _outproj(mixes, w_out.astype(BF16), xa, g1, gain2, sh2, sc2, wr, br, t0)
        dest, ys = _moe(h2, rinfo, counts, expert_w_gu, expert_w_down, layer)
        xa = _combine(dest, ys, xa, rinfo, g2, final_g.reshape(1, d), not last, last)
    return xa
```

```python
import functools

import numpy as np
import jax
import jax.numpy as jnp
from jax import lax
from jax.experimental import pallas as pl
from jax.experimental.pallas import tpu as pltpu

F32 = jnp.float32
BF16 = jnp.bfloat16

D_MODEL = 1024
DEPTH = 2
GRID_W = 64
HEAD_DIM = 64
RET_HEADS = 8
NA_HEADS = 8
RET_W = 512
NA_W = 512
AB_IN = 4 * RET_W + 3 * NA_W
RET_CHUNK = 128
GN_EPS = 1e-5
NA_KH = 8
NA_KW = 16
SWA_HEADS = 16
SWA_KV_HEADS = 4
SWA_WINDOW = 128
SWA_BLOCK = 128
SWA_IN = (SWA_HEADS + 2 * SWA_KV_HEADS) * HEAD_DIM
ROPE_BASE = 10000.0
MOE_GROUPS = 4
MOE_EPG = 8
MOE_EXPERTS = 32
MOE_FF = 512
NORM_EPS = 1e-6
NEG_INF = -1e30

LANES = 128
TM = 256
RET_BLOCK = 256
MOE_ROWS = 512
ROUTE_LANE0 = 4
VMEM_LIMIT = 56 * 1024 * 1024


def _cparams(sem, vmem=VMEM_LIMIT):
    return pltpu.CompilerParams(dimension_semantics=sem, vmem_limit_bytes=vmem)


def _split_bf16(a):
    hi = a.astype(BF16)
    lo = (a - hi.astype(F32)).astype(BF16)
    return hi, lo


def _dot3(a, b):
    ah, al = _split_bf16(a)
    bh, bl = _split_bf16(b)
    d = lambda x, y: jnp.dot(x, y, preferred_element_type=F32)
    return d(ah, bh) + (d(ah, bl) + d(al, bh))


def _dot_nt(a, b):
    return lax.dot_general(a, b, (((1,), (1,)), ((), ())), preferred_element_type=F32)


def _dot_tn(a, b):
    return lax.dot_general(a, b, (((0,), (0,)), ((), ())), preferred_element_type=F32)


def _silu(x):
    return x / (1.0 + jnp.exp(-x))


def _adaln_kernel(c_ref, w_ref, b_ref, o_ref):
    o_ref[...] = _dot3(_silu(c_ref[...]), w_ref[...]) + b_ref[...]


def _adaln(cvec, ada_w, ada_b):
    depth, d, n6 = ada_w.shape
    rows = cvec.shape[0]
    tn = 1024
    return pl.pallas_call(
        _adaln_kernel,
        grid=(depth, n6 // tn),
        in_specs=[
            pl.BlockSpec((rows, d), lambda l, j: (0, 0)),
            pl.BlockSpec((None, d, tn), lambda l, j: (l, 0, j)),
            pl.BlockSpec((None, 1, tn), lambda l, j: (l, 0, j)),
        ],
        out_specs=pl.BlockSpec((None, rows, tn), lambda l, j: (l, 0, j)),
        out_shape=jax.ShapeDtypeStruct((depth, rows, n6), F32),
        compiler_params=_cparams(("arbitrary", "arbitrary")),
        name="adaln",
    )(cvec, ada_w, ada_b.reshape(depth, 1, n6))


def _rms_mod(x, g, sh, sc):
    ms = jnp.mean(x * x, axis=-1, keepdims=True)
    return (x * lax.rsqrt(ms + NORM_EPS) * g) * (1.0 + sc) + sh


def _stream_tile(refs, is_lat):
    if len(refs) == 1:
        return refs[0][...]
    return jnp.where(is_lat, refs[1][...], refs[0][...])


def _stream_specs(stream, t0, extra_args=0):
    def im(f):
        return (lambda bi, t, *_: f(bi, t + t0))
    if not isinstance(stream, tuple):
        return [pl.BlockSpec((None, TM, stream.shape[2]), im(lambda bi, t: (bi, t, 0)))], [stream]
    ctx, x = stream
    d = x.shape[2]
    return ([pl.BlockSpec((None, TM, d), im(lambda bi, t: (bi, 0, 0))),
             pl.BlockSpec((None, TM, d), im(lambda bi, t: (bi, jnp.maximum(t - 1, 0), 0)))], [ctx, x])


def _stream_shape(stream):
    if not isinstance(stream, tuple):
        return stream.shape
    ctx, x = stream
    return (x.shape[0], ctx.shape[1] + x.shape[1], x.shape[2])


def _proj_kernel(*refs, rope_blocks, scales, cn, n_stream):
    x_refs = refs[:n_stream]
    g_ref, sh_ref, sc_ref, w_ref, rope_ref, o_ref = refs[n_stream:]
    is_lat = pl.program_id(1) > 0
    hb = _rms_mod(_stream_tile(x_refs, is_lat), g_ref[...], sh_ref[...], sc_ref[...]).astype(BF16)
    nout = w_ref.shape[1]
    for c in range(nout // cn):
        o = jnp.dot(hb, w_ref[:, c * cn:(c + 1) * cn], preferred_element_type=F32)
        for s in range(cn // LANES):
            blk = c * (cn // LANES) + s
            ob = o[:, s * LANES:(s + 1) * LANES]
            if blk in rope_blocks:
                r = (ob * rope_ref[0] + pltpu.roll(ob, 16, 1) * rope_ref[1]
                     + pltpu.roll(ob, LANES - 16, 1) * rope_ref[2])
                ob = jnp.where(is_lat, r, ob)
            if blk in scales:
                ob = ob * scales[blk]
            o_ref[:, blk * LANES:(blk + 1) * LANES] = ob


def _proj(xa, gain, sh, sc, w_bf16, rope, rope_blocks, scales):
    b, s, d = _stream_shape(xa)
    nout = w_bf16.shape[1]
    nt = s // TM
    mod_idx = lambda bi, t: (jnp.where(t == 0, b, bi), 0, 0)
    x_specs, x_args = _stream_specs(xa, 0)
    kern = functools.partial(_proj_kernel, rope_blocks=frozenset(rope_blocks), scales=dict(scales), cn=512,
                             n_stream=len(x_args))
    return pl.pallas_call(
        kern,
        grid=(b, nt),
        in_specs=x_specs + [
            pl.BlockSpec((1, d), lambda bi, t: (0, 0)),
            pl.BlockSpec((None, 1, d), mod_idx),
            pl.BlockSpec((None, 1, d), mod_idx),
            pl.BlockSpec((d, nout), lambda bi, t: (0, 0)),
            pl.BlockSpec((3, TM, LANES), lambda bi, t: (0, jnp.maximum(t - 1, 0), 0)),
        ],
        out_specs=pl.BlockSpec((None, TM, nout), lambda bi, t: (bi, t, 0)),
        out_shape=jax.ShapeDtypeStruct((b, s, nout), F32),
        compiler_params=_cparams(("arbitrary", "arbitrary")),
        name="proj",
    )(*x_args, gain, sh, sc, w_bf16, rope)


def _rope_tables(seq):
    nf = HEAD_DIM // 4
    inv = ROPE_BASE ** (-jnp.arange(nf, dtype=F32) / nf)
    t = jnp.arange(seq)
    row = (t // GRID_W).astype(F32)
    col = (t % GRID_W).astype(F32)
    lane = np.arange(LANES)
    jj = lane % HEAD_DIM
    axis_is_col = (jj // 32) == 1
    second_half = (jj % 32) >= 16
    f = jj % 16
    pos = jnp.where(axis_is_col[None, :], col[:, None], row[:, None])
    ang = pos * inv[f][None, :]
    c, s = jnp.cos(ang), jnp.sin(ang)
    sa = jnp.where(second_half[None, :], s, 0.0)
    sb = jnp.where(second_half[None, :], 0.0, -s)
    return jnp.stack([c, sa, sb], axis=0)


def _ret_kernel(lg_ref, q_ref, k_ref, v_ref, g_ref, gn_ref, o_ref, accf_ref, accb_ref,
                intra_ref, qd_ref, kd_ref):
    hp = pl.program_id(1)
    c = RET_BLOCK
    s_len = q_ref.shape[0]
    n_chunks = s_len // c
    ctx_chunks = TM // c
    pos = lax.broadcasted_iota(jnp.int32, (c, LANES), 0).astype(F32)
    m0 = lax.broadcasted_iota(jnp.int32, (c, LANES), 1) < HEAD_DIM
    same_head = ((lax.broadcasted_iota(jnp.int32, (LANES, LANES), 0) < HEAD_DIM)
                 == (lax.broadcasted_iota(jnp.int32, (LANES, LANES), 1) < HEAD_DIM))
    rel = (lax.broadcasted_iota(jnp.int32, (c, c), 0) - lax.broadcasted_iota(jnp.int32, (c, c), 1)).astype(F32)
    lgf = [lg_ref[0, hp * 2 + hh] for hh in range(2)]
    lgb = [lg_ref[1, hp * 2 + hh] for hh in range(2)]
    lgf_l = jnp.where(m0, lgf[0], lgf[1])
    lgb_l = jnp.where(m0, lgb[0], lgb[1])
    for hh in range(2):
        intra_ref[0, :, hh * c:(hh + 1) * c] = jnp.where(rel >= 0, jnp.exp(lgf[hh] * jnp.maximum(rel, 0.0)), 0.0)
        intra_ref[1, :, hh * c:(hh + 1) * c] = jnp.where(rel <= 0, jnp.exp(lgb[hh] * jnp.maximum(-rel, 0.0)), 0.0)
    qd_ref[0] = jnp.exp(lgf_l * (pos + 1.0))
    qd_ref[1] = jnp.exp(lgb_l * (c - pos))
    kd_ref[0] = jnp.exp(lgf_l * (c - 1.0 - pos))
    kd_ref[1] = jnp.exp(lgb_l * pos)
    cd = [jnp.exp(lgf_l[0:1] * float(c)), jnp.exp(lgb_l[0:1] * float(c))]

    def chunk(r0, state, d):
        q = q_ref[pl.ds(r0, c), :]
        k = k_ref[pl.ds(r0, c), :]
        v = v_ref[pl.ds(r0, c), :]
        qb = q.astype(BF16)
        kcat = jnp.concatenate([jnp.where(m0, k, 0.0), jnp.where(m0, 0.0, k)], axis=0).astype(BF16)
        vcat = jnp.concatenate([jnp.where(m0, v, 0.0), jnp.where(m0, 0.0, v)], axis=0).astype(BF16)
        s = _dot_nt(qb, kcat) * intra_ref[d]
        out = jnp.dot(s.astype(BF16), vcat, preferred_element_type=F32)
        out = out + jnp.dot(qb, state.astype(BF16), preferred_element_type=F32) * qd_ref[d]
        kv = _dot_tn((k * kd_ref[d]).astype(BF16), v.astype(BF16))
        return out, state * cd[d] + jnp.where(same_head, kv, 0.0)

    def body(i, states):
        sf, sb = states
        rf = pl.multiple_of(i * c, c)
        ib = jnp.where(i < ctx_chunks, ctx_chunks - 1 - i, n_chunks + ctx_chunks - 1 - i)
        rb = pl.multiple_of(ib * c, c)
        of, sf = chunk(rf, sf, 0)
        ob, sb = chunk(rb, sb, 1)
        accf_ref[pl.ds(rf, c), :] = of
        accb_ref[pl.ds(rb, c), :] = ob
        return sf, sb

    z = jnp.zeros((LANES, LANES), F32)
    lax.fori_loop(0, n_chunks, body, (z, z), unroll=3)

    avg = jnp.where(same_head, 1.0 / HEAD_DIM, 0.0).astype(BF16)

    def head_mean(x):
        hi, lo = _split_bf16(x)
        return (jnp.dot(hi, avg, preferred_element_type=F32) + jnp.dot(lo, avg, preferred_element_type=F32))

    def readout(i, carry):
        r0 = pl.multiple_of(i * c, c)
        o = accf_ref[pl.ds(r0, c), :] + accb_ref[pl.ds(r0, c), :]
        dlt = o - head_mean(o)
        var = head_mean(dlt * dlt)
        y = dlt * lax.rsqrt(var + GN_EPS) * gn_ref[...]
        o_ref[pl.ds(r0, c), :] = _silu(g_ref[pl.ds(r0, c), :]) * y
        return carry

    lax.fori_loop(0, n_chunks, readout, 0, unroll=2)


def _retention(p, log_gamma, ret_gn):
    b, s, _ = p.shape
    nb = RET_W // LANES
    blk = lambda off: pl.BlockSpec((None, s, LANES), lambda bi, hp: (bi, 0, off + hp))
    return pl.pallas_call(
        _ret_kernel,
        grid=(b, nb),
        in_specs=[
            pl.BlockSpec(memory_space=pltpu.SMEM),
            blk(0), blk(nb), blk(2 * nb), blk(3 * nb),
            pl.BlockSpec((1, LANES), lambda bi, hp: (0, hp)),
        ],
        out_specs=pl.BlockSpec((None, s, LANES), lambda bi, hp: (bi, 0, hp)),
        out_shape=jax.ShapeDtypeStruct((b, s, RET_W), F32),
        scratch_shapes=[pltpu.VMEM((s, LANES), F32), pltpu.VMEM((s, LANES), F32),
                        pltpu.VMEM((2, RET_BLOCK, 2 * RET_BLOCK), F32),
                        pltpu.VMEM((2, RET_BLOCK, LANES), F32), pltpu.VMEM((2, RET_BLOCK, LANES), F32)],
        compiler_params=_cparams(("arbitrary", "arbitrary")),
        name="retention",
    )(log_gamma, p, p, p, p, ret_gn.reshape(1, RET_W))


LOG2E = 1.4426950408889634


def _softmax_pv(s_list, v_list, extra=None):
    m = None
    for s in s_list:
        for j in range(s.shape[1] // LANES):
            blk = s[:, j * LANES:(j + 1) * LANES]
            m = blk if m is None else jnp.maximum(m, blk)
    m = m.max(axis=-1, keepdims=True)
    if extra is not None:
        m = jnp.maximum(m, extra)
    acc = None
    for s, v in zip(s_list, v_list):
        pv = jnp.dot(jnp.exp2(s - m).astype(BF16), v, preferred_element_type=F32)
        acc = pv if acc is None else acc + pv
    o, den = acc[:, :LANES], acc[:, LANES:]
    if extra is not None:
        den = den + jnp.exp2(extra - m)
    return o / den


def _stage_heads(q_ref, k_ref, v_ref, qm_ref, kb_ref, vb_ref):
    m0 = lax.broadcasted_iota(jnp.int32, (TM, LANES), 1) < HEAD_DIM

    def stage(i, carry):
        r0 = pl.multiple_of(i * TM, TM)
        q = q_ref[pl.ds(r0, TM), :]
        qm_ref[0, pl.ds(r0, TM), :] = jnp.where(m0, q, 0.0).astype(BF16)
        qm_ref[1, pl.ds(r0, TM), :] = jnp.where(m0, 0.0, q).astype(BF16)
        kb_ref[pl.ds(r0, TM), :] = k_ref[pl.ds(r0, TM), :].astype(BF16)
        vb_ref[pl.ds(r0, TM), 0:LANES] = v_ref[pl.ds(r0, TM), :].astype(BF16)
        vb_ref[pl.ds(r0, TM), LANES:2 * LANES] = jnp.ones((TM, LANES), BF16)
        return carry

    lax.fori_loop(0, q_ref.shape[0] // TM, stage, 0)


def _na_kernel(q_ref, k_ref, v_ref, bias_ref, o_ref, qm_ref, kb_ref, vb_ref):
    s_len = q_ref.shape[0]
    rows = (s_len - TM) // GRID_W
    nloc = NA_KH * GRID_W
    _stage_heads(q_ref, k_ref, v_ref, qm_ref, kb_ref, vb_ref)

    kc = kb_ref[0:TM, :]
    vc = vb_ref[0:TM, :]
    outs = [_softmax_pv([_dot_nt(qm_ref[hh, 0:TM, :], kc)], [vc]) for hh in range(2)]
    m0c = lax.broadcasted_iota(jnp.int32, (TM, LANES), 1) < HEAD_DIM
    o_ref[0:TM, :] = jnp.where(m0c, outs[0], outs[1])

    m0 = lax.broadcasted_iota(jnp.int32, (GRID_W, LANES), 1) < HEAD_DIM

    def row_block(r, carry):
        rs = jnp.clip(r - NA_KH // 2, 0, rows - NA_KH)
        pat = r - rs
        q0 = pl.multiple_of(TM + r * GRID_W, GRID_W)
        k0 = pl.multiple_of(TM + rs * GRID_W, GRID_W)
        kl = kb_ref[pl.ds(k0, nloc), :]
        vl = vb_ref[pl.ds(k0, nloc), :]
        kc = kb_ref[0:TM, :]
        vc = vb_ref[0:TM, :]
        q = jnp.concatenate([qm_ref[0, pl.ds(q0, GRID_W), :], qm_ref[1, pl.ds(q0, GRID_W), :]], axis=0)
        s_loc = _dot_nt(q, kl) + bias_ref[pat].reshape(2 * GRID_W, nloc)
        s_ctx = _dot_nt(q, kc)
        res = _softmax_pv([s_loc, s_ctx], [vl, vc])
        o_ref[pl.ds(q0, GRID_W), :] = jnp.where(m0, res[:GRID_W], res[GRID_W:])
        return carry

    lax.fori_loop(0, rows, row_block, 0, unroll=4)


def _na_bias(rpb):
    h = rpb.shape[0]
    qc = np.arange(GRID_W)[:, None]
    kc = np.arange(GRID_W)[None, :]
    win = np.clip(qc - NA_KW // 2, 0, GRID_W - NA_KW)
    valid = (kc >= win) & (kc < win + NA_KW)
    col_off = np.clip(kc - qc + NA_KW - 1, 0, 2 * NA_KW - 2)
    onehot = (col_off[None] == np.arange(2 * NA_KW - 1)[:, None, None]).astype(np.float32)
    cols = jnp.einsum("hrc,cqk->hrqk", rpb.astype(F32), onehot, precision=lax.Precision.HIGHEST)
    cols = jnp.where(valid[None, None], cols * LOG2E, NEG_INF)
    bias = jnp.stack([cols[:, NA_KH - 1 - p:2 * NA_KH - 1 - p] for p in range(NA_KH)])
    bias = bias.transpose(0, 1, 3, 2, 4)
    return bias.reshape(NA_KH, h, GRID_W, NA_KH * GRID_W)


def _neighbourhood(p, bias):
    b, s, _ = p.shape
    nb = NA_W // LANES
    c0 = 4 * RET_W // LANES
    blk = lambda off: pl.BlockSpec((None, s, LANES), lambda bi, hp: (bi, 0, c0 + off + hp))
    return pl.pallas_call(
        _na_kernel,
        grid=(b, nb),
        in_specs=[
            blk(0), blk(nb), blk(2 * nb),
            pl.BlockSpec((NA_KH, 2, GRID_W, NA_KH * GRID_W), lambda bi, hp: (0, hp, 0, 0)),
        ],
        out_specs=pl.BlockSpec((None, s, LANES), lambda bi, hp: (bi, 0, hp)),
        out_shape=jax.ShapeDtypeStruct((b, s, NA_W), F32),
        scratch_shapes=[pltpu.VMEM((2, s, LANES), BF16), pltpu.VMEM((s, LANES), BF16),
                        pltpu.VMEM((s, 2 * LANES), BF16)],
        compiler_params=_cparams(("arbitrary", "arbitrary")),
        name="neighbourhood",
    )(p, p, p, bias)


def _swa_kernel(sink_ref, q_ref, k_ref, v_ref, o_ref, kd_ref, vd_ref):
    kp = pl.program_id(1)
    s_len = q_ref.shape[0]
    seq = s_len - TM
    nblk = seq // SWA_BLOCK
    band = SWA_BLOCK + 2 * SWA_WINDOW
    group = SWA_HEADS // SWA_KV_HEADS
    heads_per_step = 2 * group
    m0t = lax.broadcasted_iota(jnp.int32, (TM, LANES), 1) < HEAD_DIM

    def stage(i, carry):
        r0 = pl.multiple_of(i * TM, TM)
        for src, dst in ((k_ref, kd_ref), (v_ref, vd_ref)):
            x = src[pl.ds(r0, TM), :]
            xr = pltpu.roll(x, HEAD_DIM, 1)
            dst[0, pl.ds(r0, TM), 0:LANES] = jnp.where(m0t, x, xr).astype(BF16)
            dst[1, pl.ds(r0, TM), 0:LANES] = jnp.where(m0t, xr, x).astype(BF16)
        for h in range(2):
            vd_ref[h, pl.ds(r0, TM), LANES:2 * LANES] = jnp.ones((TM, LANES), BF16)
        return carry

    lax.fori_loop(0, s_len // TM, stage, 0)
    o_ref[0:TM, :] = jnp.zeros((TM, o_ref.shape[1]), F32)

    rows = group * SWA_BLOCK
    qi = lax.broadcasted_iota(jnp.int32, (rows, band), 0) % SWA_BLOCK
    ki = lax.broadcasted_iota(jnp.int32, (rows, band), 1)
    m0 = lax.broadcasted_iota(jnp.int32, (SWA_BLOCK, LANES), 1) < HEAD_DIM
    head_of_row = lax.broadcasted_iota(jnp.int32, (rows, 1), 0) // SWA_BLOCK

    def block(i, carry):
        start = jnp.clip((i - 1) * SWA_BLOCK, 0, seq - band)
        k0 = pl.multiple_of(TM + start, SWA_BLOCK)
        q0 = pl.multiple_of(TM + i * SWA_BLOCK, SWA_BLOCK)
        valid = jnp.abs(qi + (i * SWA_BLOCK - start) - ki) <= SWA_WINDOW
        for hh in range(2):
            kb = kd_ref[hh, pl.ds(k0, band), :]
            vb = vd_ref[hh, pl.ds(k0, band), :]
            kc = kd_ref[hh, 0:TM, :]
            vc = vd_ref[hh, 0:TM, :]
            parts = []
            for j in range(group // 2):
                pair = hh * (group // 2) + j
                qp = q_ref[pl.ds(q0, SWA_BLOCK), pair * LANES:(pair + 1) * LANES]
                parts += [jnp.where(m0, qp, 0.0), jnp.where(m0, 0.0, qp)]
            q = jnp.concatenate(parts, axis=0).astype(BF16)
            s = jnp.where(valid, _dot_nt(q, kb), NEG_INF)
            s_ctx = _dot_nt(q, kc)
            h0 = kp * heads_per_step + hh * group
            sink = jnp.full((rows, 1), sink_ref[h0], F32)
            for g in range(1, group):
                sink = jnp.where(head_of_row == g, sink_ref[h0 + g], sink)
            res = _softmax_pv([s, s_ctx], [vb, vc], extra=sink * LOG2E)
            for j in range(group // 2):
                pair = hh * (group // 2) + j
                r0 = 2 * j * SWA_BLOCK
                o_ref[pl.ds(q0, SWA_BLOCK), pair * LANES:(pair + 1) * LANES] = jnp.where(
                    m0, res[r0:r0 + SWA_BLOCK], res[r0 + SWA_BLOCK:r0 + 2 * SWA_BLOCK])
        return carry

    lax.fori_loop(0, nblk, block, 0, unroll=2)


def _swa(p, sink):
    b, s, _ = p.shape
    qw = SWA_HEADS * HEAD_DIM // 2
    kblk = SWA_HEADS * HEAD_DIM // LANES
    vblk = kblk + SWA_KV_HEADS * HEAD_DIM // LANES
    return pl.pallas_call(
        _swa_kernel,
        grid=(b, 2),
        in_specs=[
            pl.BlockSpec(memory_space=pltpu.SMEM),
            pl.BlockSpec((None, s, qw), lambda bi, kp: (bi, 0, kp)),
            pl.BlockSpec((None, s, LANES), lambda bi, kp: (bi, 0, kblk + kp)),
            pl.BlockSpec((None, s, LANES), lambda bi, kp: (bi, 0, vblk + kp)),
        ],
        out_specs=pl.BlockSpec((None, s, qw), lambda bi, kp: (bi, 0, kp)),
        out_shape=jax.ShapeDtypeStruct((b, s, SWA_HEADS * HEAD_DIM), F32),
        scratch_shapes=[pltpu.VMEM((2, s, LANES), BF16), pltpu.VMEM((2, s, 2 * LANES), BF16)],
        compiler_params=_cparams(("arbitrary", "arbitrary")),
        name="swa",
    )(sink, p, p, p)


def _outproj_kernel(*refs, n_mix, n_stream, t0):
    mix_refs = refs[:n_mix]
    x_refs = refs[n_mix + 1:n_mix + 1 + n_stream]
    w_ref = refs[n_mix]
    (g1_ref, gn_ref, sh_ref, sc_ref, wr_ref, br_ref,
     xo_ref, h_ref, ri_ref, cnt_ref, carry_ref) = refs[n_mix + 1 + n_stream:]
    first = (pl.program_id(0) == 0) & (pl.program_id(1) == 0)

    @pl.when(first)
    def _():
        carry_ref[...] = jnp.zeros_like(carry_ref)

    o = None
    off = 0
    for m_ref in mix_refs:
        kw = m_ref.shape[1]
        part = jnp.dot(m_ref[...].astype(BF16), w_ref[off:off + kw, :], preferred_element_type=F32)
        o = part if o is None else o + part
        off += kw
    xn = _stream_tile(x_refs, pl.program_id(1) + t0 > 0) + g1_ref[...] * o
    xo_ref[...] = xn
    h = _rms_mod(xn, gn_ref[...], sh_ref[...], sc_ref[...])
    h_ref[...] = h

    logits = _dot3(h, wr_ref[...]) + br_ref[...]
    tm = logits.shape[0]
    lane = lax.broadcasted_iota(jnp.int32, (tm, LANES), 1).astype(F32)
    big = 1e9
    gmask = lane < MOE_GROUPS
    mg = jnp.max(jnp.where(gmask, logits, -big), axis=-1, keepdims=True)
    sg = jnp.sum(jnp.where(gmask, jnp.exp(jnp.minimum(logits - mg, 0.0)), 0.0), axis=-1, keepdims=True)
    gw = 1.0 / sg
    gi = jnp.min(jnp.where(gmask & (logits == mg), lane, big), axis=-1, keepdims=True)
    lo = ROUTE_LANE0 + MOE_EPG * gi
    emask = (lane >= lo) & (lane < lo + MOE_EPG)
    l1 = jnp.max(jnp.where(emask, logits, -big), axis=-1, keepdims=True)
    i1 = jnp.min(jnp.where(emask & (logits == l1), lane, big), axis=-1, keepdims=True)
    emask2 = emask & (lane != i1)
    l2 = jnp.max(jnp.where(emask2, logits, -big), axis=-1, keepdims=True)
    i2 = jnp.min(jnp.where(emask2 & (logits == l2), lane, big), axis=-1, keepdims=True)
    e21 = jnp.exp(l2 - l1)
    w1 = gw / (1.0 + e21)
    w2 = gw * e21 / (1.0 + e21)

    oh = jnp.where((lane == i1) | (lane == i2), 1.0, 0.0)
    tri = (lax.broadcasted_iota(jnp.int32, (tm, tm), 0) > lax.broadcasted_iota(jnp.int32, (tm, tm), 1))
    cum = jnp.dot(jnp.where(tri, 1.0, 0.0).astype(BF16), oh.astype(BF16), preferred_element_type=F32) + carry_ref[...]
    r1 = jnp.sum(jnp.where(lane == i1, cum, 0.0), axis=-1, keepdims=True)
    r2 = jnp.sum(jnp.where(lane == i2, cum, 0.0), axis=-1, keepdims=True)
    carry_ref[...] = carry_ref[...] + jnp.sum(oh, axis=0, keepdims=True)
    cnt_ref[...] = carry_ref[...]

    ri = jnp.where(lane == 0, i1 - ROUTE_LANE0, 0.0)
    ri = jnp.where(lane == 1, i2 - ROUTE_LANE0, ri)
    ri = jnp.where(lane == 2, w1, ri)
    ri = jnp.where(lane == 3, w2, ri)
    ri = jnp.where(lane == 4, r1, ri)
    ri = jnp.where(lane == 5, r2, ri)
    ri_ref[...] = ri


def _outproj(mixes, w_bf16, xa, g1, gain2, sh2, sc2, wr, br, t0):
    b, s, d = _stream_shape(xa)
    nt = s // TM - t0
    so = nt * TM
    mod_idx = lambda bi, t: (jnp.where(t + t0 == 0, b, bi), 0, 0)
    tile = lambda wdt: pl.BlockSpec((None, TM, wdt), lambda bi, t: (bi, t + t0, 0))
    otile = lambda wdt: pl.BlockSpec((None, TM, wdt), lambda bi, t: (bi, t, 0))
    const = lambda shape: pl.BlockSpec(shape, lambda bi, t: (0,) * len(shape))
    x_specs, x_args = _stream_specs(xa, t0)
    in_specs = [tile(m.shape[2]) for m in mixes] + [const(w_bf16.shape)] + x_specs + [
        pl.BlockSpec((None, 1, d), mod_idx), const((1, d)),
        pl.BlockSpec((None, 1, d), mod_idx), pl.BlockSpec((None, 1, d), mod_idx),
        const((d, LANES)), const((1, LANES)),
    ]
    return pl.pallas_call(
        functools.partial(_outproj_kernel, n_mix=len(mixes), n_stream=len(x_args), t0=t0),
        grid=(b, nt),
        in_specs=in_specs,
        out_specs=[otile(d), otile(d), otile(LANES), const((1, LANES))],
        out_shape=[jax.ShapeDtypeStruct((b, so, d), F32), jax.ShapeDtypeStruct((b, so, d), F32),
                   jax.ShapeDtypeStruct((b, so, LANES), F32), jax.ShapeDtypeStruct((1, LANES), F32)],
        scratch_shapes=[pltpu.VMEM((1, LANES), F32)],
        compiler_params=_cparams(("arbitrary", "arbitrary")),
        name="outproj",
    )(*mixes, w_bf16, *x_args, g1, gain2, sh2, sc2, wr, br)


SUBLANES = 8


def _to_token_tiles(x):
    return x.reshape(x.shape[0], SUBLANES, x.shape[1] // SUBLANES)


def _from_token_tiles(x3):
    return x3.reshape(x3.shape[0], x3.shape[1] * x3.shape[2])


PAD_BITS = tuple(1 << k for k in reversed(range(MOE_ROWS.bit_length() - 1)))


def _dispatch_kernel(dest_ref, pads_ref, h_ref, xs_ref, hbuf, zbuf, sem, zsem):
    nt = pl.num_programs(1)
    step = pl.program_id(0) * nt + pl.program_id(1)
    last = pl.num_programs(0) * nt - 1
    slot = step % 2

    def zero_pads(wait):
        def expert(e, c):
            pos = pads_ref[0, e]
            n = pads_ref[1, e]
            for bit in PAD_BITS:
                take = (n & bit) != 0
                cp = pltpu.make_async_copy(zbuf.at[pl.ds(0, bit)], xs_ref.at[pl.ds(pos, bit)], zsem)

                @pl.when(take)
                def _():
                    cp.wait() if wait else cp.start()

                pos = pos + jnp.where(take, bit, 0)
            return c
        lax.fori_loop(0, MOE_EXPERTS, expert, 0)

    @pl.when(step == 0)
    def _():
        zbuf[...] = jnp.zeros_like(zbuf)
        zero_pads(wait=False)
    hbuf[slot] = _to_token_tiles(h_ref[...])

    def copy(sl, i, dst_row):
        return pltpu.make_async_copy(hbuf.at[sl, i], xs_ref.at[dst_row], sem.at[sl])

    for i in range(TM):
        copy(slot, i, dest_ref[step, i]).start()
        copy(slot, i, dest_ref[step, TM + i]).start()

    def drain(sl):
        def one(i, c):
            copy(sl, 0, 0).wait()
            return c
        lax.fori_loop(0, 2 * TM, one, 0, unroll=8)

    @pl.when(step > 0)
    def _():
        drain(1 - slot)

    @pl.when(step == last)
    def _():
        drain(slot)
        zero_pads(wait=True)


def _dispatch(dest, pads, h2, n_pad):
    b, s, d = h2.shape
    nt = s // TM
    tile = (SUBLANES, d // SUBLANES)
    return pl.pallas_call(
        _dispatch_kernel,
        grid_spec=pltpu.PrefetchScalarGridSpec(
            num_scalar_prefetch=2,
            grid=(b, nt),
            in_specs=[pl.BlockSpec((None, TM, d), lambda bi, t, dr, pd: (bi, t, 0))],
            out_specs=pl.BlockSpec(memory_space=pl.ANY),
            scratch_shapes=[pltpu.VMEM((2, TM) + tile, F32), pltpu.VMEM((PAD_BITS[0],) + tile, F32),
                            pltpu.SemaphoreType.DMA((2,)), pltpu.SemaphoreType.DMA],
        ),
        out_shape=jax.ShapeDtypeStruct((n_pad,) + tile, F32),
        compiler_params=_cparams(("arbitrary", "arbitrary")),
        name="dispatch",
    )(dest, pads, h2)


def _mlp_kernel(be_ref, nu_ref, x_ref, wgu_ref, wd_ref, y_ref, wgu_b, wd_b):
    i = pl.program_id(0)
    prev = be_ref[jnp.maximum(i - 1, 0)]
    used = i < nu_ref[0]

    @pl.when(used & ((i == 0) | (be_ref[i] != prev)))
    def _():
        wgu_b[...] = wgu_ref[...].astype(BF16)
        wd_b[...] = wd_ref[...].astype(BF16)

    @pl.when(used)
    def _():
        x = _from_token_tiles(x_ref[...])
        gu = jnp.dot(x.astype(BF16), wgu_b[...], preferred_element_type=F32)
        act = _silu(gu[:, :MOE_FF]) * gu[:, MOE_FF:]
        y_ref[...] = _to_token_tiles(jnp.dot(act.astype(BF16), wd_b[...], preferred_element_type=F32))

    @pl.when(jnp.logical_not(used))
    def _():
        y_ref[...] = jnp.zeros_like(y_ref)


def _expert_mlp(block_e, n_used, xs, w_gu, w_down, layer):
    n_pad, sub, dl = xs.shape
    d = sub * dl
    ff2 = w_gu.shape[-1]
    slots = pl.BlockSpec((MOE_ROWS, sub, dl), lambda i, be, nu: (i, 0, 0))
    used_slots = pl.BlockSpec((MOE_ROWS, sub, dl), lambda i, be, nu: (jnp.minimum(i, nu[0] - 1), 0, 0))
    return pl.pallas_call(
        _mlp_kernel,
        grid_spec=pltpu.PrefetchScalarGridSpec(
            num_scalar_prefetch=2,
            grid=(n_pad // MOE_ROWS,),
            in_specs=[
                used_slots,
                pl.BlockSpec((None, None, d, ff2), lambda i, be, nu: (layer, be[i], 0, 0)),
                pl.BlockSpec((None, None, ff2 // 2, d), lambda i, be, nu: (layer, be[i], 0, 0)),
            ],
            out_specs=slots,
            scratch_shapes=[pltpu.VMEM((d, ff2), BF16), pltpu.VMEM((ff2 // 2, d), BF16)],
        ),
        out_shape=jax.ShapeDtypeStruct(xs.shape, F32),
        compiler_params=_cparams(("arbitrary",)),
        name="expert_mlp",
    )(block_e, n_used, xs, w_gu, w_down)


def _combine_kernel(dest_ref, ys_ref, x_ref, ri_ref, g2_ref, fg_ref, o_ref, buf, sem, *, final):
    bi = pl.program_id(0)
    t = pl.program_id(1)
    nt = pl.num_programs(1)
    step = bi * nt + t
    total = pl.num_programs(0) * nt

    def copy(src_row, slot, k, i):
        return pltpu.make_async_copy(ys_ref.at[src_row], buf.at[slot, k, i], sem.at[slot])

    def issue(st, slot):
        for i in range(TM):
            copy(dest_ref[st, i], slot, 0, i).start()
            copy(dest_ref[st, TM + i], slot, 1, i).start()

    slot = step % 2

    @pl.when(step == 0)
    def _():
        issue(0, 0)

    @pl.when(step + 1 < total)
    def _():
        issue(step + 1, 1 - slot)

    def drain(i, c):
        copy(0, slot, 0, 0).wait()
        return c

    lax.fori_loop(0, 2 * TM, drain, 0, unroll=8)

    lane = lax.broadcasted_iota(jnp.int32, (TM, LANES), 1)
    ri = ri_ref[...]
    w1 = jnp.sum(jnp.where(lane == 2, ri, 0.0), axis=-1, keepdims=True)
    w2 = jnp.sum(jnp.where(lane == 3, ri, 0.0), axis=-1, keepdims=True)
    y = _from_token_tiles(buf[slot, 0]) * w1 + _from_token_tiles(buf[slot, 1]) * w2
    xn = x_ref[...] + g2_ref[...] * y
    if final:
        ms = jnp.mean(xn * xn, axis=-1, keepdims=True)
        xn = xn * lax.rsqrt(ms + NORM_EPS) * fg_ref[...]
    o_ref[...] = xn


def _combine(dest, ys, xa, rinfo, g2, final_g, has_ctx, final):
    b, s, d = xa.shape
    nt = s // TM
    mod_idx = lambda bi, t, dr: (jnp.where(t == 0, b, bi) if has_ctx else bi, 0, 0)
    tile = lambda wdt: pl.BlockSpec((None, TM, wdt), lambda bi, t, dr: (bi, t, 0))
    out_spec = tile(d)
    out_shape = jax.ShapeDtypeStruct((b, s, d), F32)
    return pl.pallas_call(
        functools.partial(_combine_kernel, final=final),
        grid_spec=pltpu.PrefetchScalarGridSpec(
            num_scalar_prefetch=1,
            grid=(b, nt),
            in_specs=[
                pl.BlockSpec(memory_space=pl.ANY),
                tile(d), tile(LANES),
                pl.BlockSpec((None, 1, d), mod_idx),
                pl.BlockSpec((1, d), lambda bi, t, dr: (0, 0)),
            ],
            out_specs=out_spec,
            scratch_shapes=[pltpu.VMEM((2, 2, TM, SUBLANES, d // SUBLANES), F32),
                            pltpu.SemaphoreType.DMA((2,))],
        ),
        out_shape=out_shape,
        compiler_params=_cparams(("arbitrary", "arbitrary")),
        name="combine",
    )(dest, ys, xa, rinfo, g2, final_g)


def _moe(h2, rinfo, counts, w_gu, w_down, layer):
    b, s, d = h2.shape
    nt = s // TM
    r = rinfo[:, :, :8].reshape(b * nt, TM, 8)
    lanes = lambda i: jnp.concatenate([r[..., i], r[..., i + 1]], axis=1).astype(jnp.int32)
    e = lanes(0)
    rank = lanes(4)
    cnt = counts[0, ROUTE_LANE0:ROUTE_LANE0 + MOE_EXPERTS].astype(jnp.int32)
    padded = (cnt + MOE_ROWS - 1) // MOE_ROWS * MOE_ROWS
    ends = jnp.cumsum(padded)
    starts = ends - padded
    eids = jnp.arange(MOE_EXPERTS, dtype=jnp.int32)
    dest = jnp.sum(jnp.where(e[..., None] == eids, starts, 0), axis=-1) + rank
    n_assign = b * nt * TM * 2
    n_blocks = (n_assign + MOE_EXPERTS * (MOE_ROWS - 1)) // MOE_ROWS + 1
    n_pad = n_blocks * MOE_ROWS
    blk_row = jnp.arange(n_blocks, dtype=jnp.int32) * MOE_ROWS
    block_e = jnp.minimum(jnp.sum((blk_row[:, None] >= ends[None, :]).astype(jnp.int32), axis=1),
                          MOE_EXPERTS - 1)
    n_used = (ends[-1] // MOE_ROWS).astype(jnp.int32).reshape(1)
    pads = jnp.stack([starts + cnt, padded - cnt]).astype(jnp.int32)
    xs = _dispatch(dest, pads, h2, n_pad)
    ys = _expert_mlp(block_e, n_used, xs, w_gu, w_down, layer)
    return dest, ys


def _router_weights(wg, bg, we, be):
    d = wg.shape[0]
    pad = LANES - MOE_GROUPS - MOE_EXPERTS
    assert ROUTE_LANE0 == MOE_GROUPS
    wr = jnp.concatenate([wg.astype(F32), we.astype(F32), jnp.zeros((d, pad), F32)], axis=1)
    br = jnp.concatenate([bg.astype(F32), be.astype(F32), jnp.zeros((pad,), F32)]).reshape(1, LANES)
    return wr, br


def kernel(x, c, ctx, c_ctx, ada_w, ada_b, norm_g, final_g, ab_w_in, ab_w_out, ret_decay, ret_gn, na_rpb,
           swa_w_in, swa_w_out, swa_sink, router_g_w, router_g_b, router_e_w, router_e_b,
           expert_w_gu, expert_w_down):
    b, seq, d = x.shape
    assert ctx.shape[1] == TM and seq % TM == 0 and d == D_MODEL
    xa = (ctx, x)
    rope = _rope_tables(seq)

    cvec = jnp.concatenate([c, c_ctx[None, :], jnp.zeros((7, d), F32)], axis=0)
    mod = _adaln(cvec, ada_w, ada_b)
    mod = mod.reshape(DEPTH, b + 8, 6, 1, d)[:, :b + 1].transpose(0, 2, 1, 3, 4)

    for layer in range(DEPTH):
        last = layer == DEPTH - 1
        sh1, sc1, g1, sh2, sc2, g2 = (mod[layer, i] for i in range(6))
        gain1 = norm_g[layer, 0].reshape(1, d)
        gain2 = norm_g[layer, 1].reshape(1, d)
        j = layer // 2
        if layer % 2 == 0:
            nb = RET_W // LANES
            qk_scale = HEAD_DIM ** -0.5
            scales = {blk: qk_scale for blk in range(nb, 2 * nb)}
            scales.update({blk: qk_scale * LOG2E for blk in range(4 * nb, 5 * nb)})
            p = _proj(xa, gain1, sh1, sc1, ab_w_in[j].astype(BF16), rope,
                      rope_blocks=range(0, 2 * nb), scales=scales)
            log_gamma = jnp.log1p(-jnp.exp2(-ret_decay[j].astype(F32)))
            mixes = [_retention(p, log_gamma, ret_gn[j]), _neighbourhood(p, _na_bias(na_rpb[j]))]
            w_out = ab_w_out[j]
        else:
            qb = SWA_HEADS * HEAD_DIM // LANES
            kb = SWA_KV_HEADS * HEAD_DIM // LANES
            p = _proj(xa, gain1, sh1, sc1, swa_w_in[j].astype(BF16), rope,
                      rope_blocks=range(0, qb + kb),
                      scales={blk: HEAD_DIM ** -0.5 * LOG2E for blk in range(qb)})
            mixes = [_swa(p, swa_sink[j].astype(F32))]
            w_out = swa_w_out[j]
        t0 = 1 if last else 0
        wr, br = _router_weights(router_g_w[layer], router_g_b[layer], router_e_w[layer], router_e_b[layer])
        xa, h2, rinfo, counts = _outproj(mixes, w_out.astype(BF16), xa, g1, gain2, sh2, sc2, wr, br, t0)
        dest, ys = _moe(h2, rinfo, counts, expert_w_gu, expert_w_down, layer)
        xa = _combine(dest, ys, xa, rinfo, g2, final_g.reshape(1, d), not last, last)
    return xa
```

```python
import functools

import numpy as np
import jax
import jax.numpy as jnp
from jax import lax
from jax.experimental import pallas as pl
from jax.experimental.pallas import tpu as pltpu

F32 = jnp.float32
BF16 = jnp.bfloat16

D_MODEL = 1024
DEPTH = 2
GRID_W = 64
HEAD_DIM = 64
RET_HEADS = 8
NA_HEADS = 8
RET_W = 512
NA_W = 512
AB_IN = 4 * RET_W + 3 * NA_W
RET_CHUNK = 128
GN_EPS = 1e-5
NA_KH = 8
NA_KW = 16
SWA_HEADS = 16
SWA_KV_HEADS = 4
SWA_WINDOW = 128
SWA_BLOCK = 128
SWA_IN = (SWA_HEADS + 2 * SWA_KV_HEADS) * HEAD_DIM
ROPE_BASE = 10000.0
MOE_GROUPS = 4
MOE_EPG = 8
MOE_EXPERTS = 32
MOE_FF = 512
NORM_EPS = 1e-6
NEG_INF = -1e30

LANES = 128
TM = 256
RET_BLOCK = 256
MOE_ROWS = 512
ROUTE_LANE0 = 4
VMEM_LIMIT = 56 * 1024 * 1024


def _cparams(sem, vmem=VMEM_LIMIT):
    return pltpu.CompilerParams(dimension_semantics=sem, vmem_limit_bytes=vmem)


def _split_bf16(a):
    hi = a.astype(BF16)
    lo = (a - hi.astype(F32)).astype(BF16)
    return hi, lo


def _dot3_split(a, bh, bl):
    ah, al = _split_bf16(a)
    d = lambda x, y: jnp.dot(x, y, preferred_element_type=F32)
    return d(ah, bh) + (d(ah, bl) + d(al, bh))


def _dot3(a, b):
    return _dot3_split(a, *_split_bf16(b))


def _dot_nt(a, b):
    return lax.dot_general(a, b, (((1,), (1,)), ((), ())), preferred_element_type=F32)


def _dot_tn(a, b):
    return lax.dot_general(a, b, (((0,), (0,)), ((), ())), preferred_element_type=F32)


def _silu(x):
    return x / (1.0 + jnp.exp(-x))


def _adaln_kernel(c_ref, w_ref, b_ref, o_ref):
    o_ref[...] = _dot3(_silu(c_ref[...]), w_ref[...]) + b_ref[...]


def _adaln(cvec, ada_w, ada_b):
    depth, d, n6 = ada_w.shape
    rows = cvec.shape[0]
    tn = 1024
    return pl.pallas_call(
        _adaln_kernel,
        grid=(depth, n6 // tn),
        in_specs=[
            pl.BlockSpec((rows, d), lambda l, j: (0, 0)),
            pl.BlockSpec((None, d, tn), lambda l, j: (l, 0, j)),
            pl.BlockSpec((None, 1, tn), lambda l, j: (l, 0, j)),
        ],
        out_specs=pl.BlockSpec((None, rows, tn), lambda l, j: (l, 0, j)),
        out_shape=jax.ShapeDtypeStruct((depth, rows, n6), F32),
        compiler_params=_cparams(("arbitrary", "arbitrary")),
        name="adaln",
    )(cvec, ada_w, ada_b.reshape(depth, 1, n6))


def _rms_mod(x, g, sh, sc):
    ms = jnp.mean(x * x, axis=-1, keepdims=True)
    return (x * lax.rsqrt(ms + NORM_EPS) * g) * (1.0 + sc) + sh


def _stream_tile(refs, is_lat):
    if len(refs) == 1:
        return refs[0][...]
    return jnp.where(is_lat, refs[1][...], refs[0][...])


def _stream_specs(stream, t0, extra_args=0):
    def im(f):
        return (lambda bi, t, *_: f(bi, t + t0))
    if not isinstance(stream, tuple):
        return [pl.BlockSpec((None, TM, stream.shape[2]), im(lambda bi, t: (bi, t, 0)))], [stream]
    ctx, x = stream
    d = x.shape[2]
    return ([pl.BlockSpec((None, TM, d), im(lambda bi, t: (bi, 0, 0))),
             pl.BlockSpec((None, TM, d), im(lambda bi, t: (bi, jnp.maximum(t - 1, 0), 0)))], [ctx, x])


def _stream_shape(stream):
    if not isinstance(stream, tuple):
        return stream.shape
    ctx, x = stream
    return (x.shape[0], ctx.shape[1] + x.shape[1], x.shape[2])


def _proj_kernel(*refs, rope_blocks, scales, cn, n_stream):
    x_refs = refs[:n_stream]
    g_ref, sh_ref, sc_ref, w_ref, rope_ref, o_ref = refs[n_stream:]
    is_lat = pl.program_id(1) > 0
    hb = _rms_mod(_stream_tile(x_refs, is_lat), g_ref[...], sh_ref[...], sc_ref[...]).astype(BF16)
    nout = w_ref.shape[1]
    for c in range(nout // cn):
        o = jnp.dot(hb, w_ref[:, c * cn:(c + 1) * cn], preferred_element_type=F32)
        for s in range(cn // LANES):
            blk = c * (cn // LANES) + s
            ob = o[:, s * LANES:(s + 1) * LANES]
            if blk in rope_blocks:
                r = (ob * rope_ref[0] + pltpu.roll(ob, 16, 1) * rope_ref[1]
                     + pltpu.roll(ob, LANES - 16, 1) * rope_ref[2])
                ob = jnp.where(is_lat, r, ob)
            if blk in scales:
                ob = ob * scales[blk]
            o_ref[:, blk * LANES:(blk + 1) * LANES] = ob.astype(o_ref.dtype)


def _proj(xa, gain, sh, sc, w_bf16, rope, rope_blocks, scales):
    b, s, d = _stream_shape(xa)
    nout = w_bf16.shape[1]
    nt = s // TM
    mod_idx = lambda bi, t: (jnp.where(t == 0, b, bi), 0, 0)
    x_specs, x_args = _stream_specs(xa, 0)
    kern = functools.partial(_proj_kernel, rope_blocks=frozenset(rope_blocks), scales=dict(scales), cn=512,
                             n_stream=len(x_args))
    return pl.pallas_call(
        kern,
        grid=(b, nt),
        in_specs=x_specs + [
            pl.BlockSpec((1, d), lambda bi, t: (0, 0)),
            pl.BlockSpec((None, 1, d), mod_idx),
            pl.BlockSpec((None, 1, d), mod_idx),
            pl.BlockSpec((d, nout), lambda bi, t: (0, 0)),
            pl.BlockSpec((3, TM, LANES), lambda bi, t: (0, jnp.maximum(t - 1, 0), 0)),
        ],
        out_specs=pl.BlockSpec((None, TM, nout), lambda bi, t: (bi, t, 0)),
        out_shape=jax.ShapeDtypeStruct((b, s, nout), BF16),
        compiler_params=_cparams(("arbitrary", "arbitrary")),
        name="proj",
    )(*x_args, gain, sh, sc, w_bf16, rope)


def _rope_tables(seq):
    nf = HEAD_DIM // 4
    inv = ROPE_BASE ** (-jnp.arange(nf, dtype=F32) / nf)
    t = jnp.arange(seq)
    row = (t // GRID_W).astype(F32)
    col = (t % GRID_W).astype(F32)
    lane = np.arange(LANES)
    jj = lane % HEAD_DIM
    axis_is_col = (jj // 32) == 1
    second_half = (jj % 32) >= 16
    f = jj % 16
    pos = jnp.where(axis_is_col[None, :], col[:, None], row[:, None])
    ang = pos * inv[f][None, :]
    c, s = jnp.cos(ang), jnp.sin(ang)
    sa = jnp.where(second_half[None, :], s, 0.0)
    sb = jnp.where(second_half[None, :], 0.0, -s)
    return jnp.stack([c, sa, sb], axis=0)


def _ret_kernel(lg_ref, q_ref, k_ref, v_ref, g_ref, gn_ref, o_ref, accf_ref, accb_ref,
                intra_ref, qd_ref, kd_ref):
    hp = pl.program_id(1)
    c = RET_BLOCK
    s_len = q_ref.shape[0]
    n_chunks = s_len // c
    ctx_chunks = TM // c
    pos = lax.broadcasted_iota(jnp.int32, (c, LANES), 0).astype(F32)
    m0 = lax.broadcasted_iota(jnp.int32, (c, LANES), 1) < HEAD_DIM
    same_head = ((lax.broadcasted_iota(jnp.int32, (LANES, LANES), 0) < HEAD_DIM)
                 == (lax.broadcasted_iota(jnp.int32, (LANES, LANES), 1) < HEAD_DIM))
    rel = (lax.broadcasted_iota(jnp.int32, (c, c), 0) - lax.broadcasted_iota(jnp.int32, (c, c), 1)).astype(F32)
    lgf = [lg_ref[0, hp * 2 + hh] for hh in range(2)]
    lgb = [lg_ref[1, hp * 2 + hh] for hh in range(2)]
    lgf_l = jnp.where(m0, lgf[0], lgf[1])
    lgb_l = jnp.where(m0, lgb[0], lgb[1])
    for hh in range(2):
        intra_ref[0, :, hh * c:(hh + 1) * c] = jnp.where(rel >= 0, jnp.exp(lgf[hh] * jnp.maximum(rel, 0.0)), 0.0)
        intra_ref[1, :, hh * c:(hh + 1) * c] = jnp.where(rel <= 0, jnp.exp(lgb[hh] * jnp.maximum(-rel, 0.0)), 0.0)
    qd_ref[0] = jnp.exp(lgf_l * (pos + 1.0))
    qd_ref[1] = jnp.exp(lgb_l * (c - pos))
    kd_ref[0] = jnp.exp(lgf_l * (c - 1.0 - pos))
    kd_ref[1] = jnp.exp(lgb_l * pos)
    cd = [jnp.exp(lgf_l[0:1] * float(c)), jnp.exp(lgb_l[0:1] * float(c))]

    def chunk(r0, state, d):
        q = q_ref[pl.ds(r0, c), :]
        k = k_ref[pl.ds(r0, c), :]
        v = v_ref[pl.ds(r0, c), :]
        qb = q.astype(BF16)
        kcat = jnp.concatenate([jnp.where(m0, k, 0.0), jnp.where(m0, 0.0, k)], axis=0).astype(BF16)
        vcat = jnp.concatenate([jnp.where(m0, v, 0.0), jnp.where(m0, 0.0, v)], axis=0).astype(BF16)
        s = _dot_nt(qb, kcat) * intra_ref[d]
        out = jnp.dot(s.astype(BF16), vcat, preferred_element_type=F32)
        out = out + jnp.dot(qb, state.astype(BF16), preferred_element_type=F32) * qd_ref[d]
        kv = _dot_tn((k * kd_ref[d]).astype(BF16), v.astype(BF16))
        return out, state * cd[d] + jnp.where(same_head, kv, 0.0)

    def body(i, states):
        sf, sb = states
        rf = pl.multiple_of(i * c, c)
        ib = jnp.where(i < ctx_chunks, ctx_chunks - 1 - i, n_chunks + ctx_chunks - 1 - i)
        rb = pl.multiple_of(ib * c, c)
        of, sf = chunk(rf, sf, 0)
        ob, sb = chunk(rb, sb, 1)
        accf_ref[pl.ds(rf, c), :] = of
        accb_ref[pl.ds(rb, c), :] = ob
        return sf, sb

    z = jnp.zeros((LANES, LANES), F32)
    lax.fori_loop(0, n_chunks, body, (z, z), unroll=3)

    avg = jnp.where(same_head, 1.0 / HEAD_DIM, 0.0).astype(BF16)

    def head_mean(x):
        hi, lo = _split_bf16(x)
        return (jnp.dot(hi, avg, preferred_element_type=F32) + jnp.dot(lo, avg, preferred_element_type=F32))

    def readout(i, carry):
        r0 = pl.multiple_of(i * c, c)
        o = accf_ref[pl.ds(r0, c), :] + accb_ref[pl.ds(r0, c), :]
        dlt = o - head_mean(o)
        var = head_mean(dlt * dlt)
        y = dlt * lax.rsqrt(var + GN_EPS) * gn_ref[...]
        o_ref[pl.ds(r0, c), :] = _silu(g_ref[pl.ds(r0, c), :].astype(F32)) * y
        return carry

    lax.fori_loop(0, n_chunks, readout, 0, unroll=2)


def _retention(p, log_gamma, ret_gn):
    b, s, _ = p.shape
    nb = RET_W // LANES
    blk = lambda off: pl.BlockSpec((None, s, LANES), lambda bi, hp: (bi, 0, off + hp))
    return pl.pallas_call(
        _ret_kernel,
        grid=(b, nb),
        in_specs=[
            pl.BlockSpec(memory_space=pltpu.SMEM),
            blk(0), blk(nb), blk(2 * nb), blk(3 * nb),
            pl.BlockSpec((1, LANES), lambda bi, hp: (0, hp)),
        ],
        out_specs=pl.BlockSpec((None, s, LANES), lambda bi, hp: (bi, 0, hp)),
        out_shape=jax.ShapeDtypeStruct((b, s, RET_W), F32),
        scratch_shapes=[pltpu.VMEM((s, LANES), F32), pltpu.VMEM((s, LANES), F32),
                        pltpu.VMEM((2, RET_BLOCK, 2 * RET_BLOCK), F32),
                        pltpu.VMEM((2, RET_BLOCK, LANES), F32), pltpu.VMEM((2, RET_BLOCK, LANES), F32)],
        compiler_params=_cparams(("arbitrary", "arbitrary")),
        name="retention",
    )(log_gamma, p, p, p, p, ret_gn.reshape(1, RET_W))


LOG2E = 1.4426950408889634


def _softmax_pv(s_list, v_list, extra=None):
    m = None
    for s in s_list:
        for j in range(s.shape[1] // LANES):
            blk = s[:, j * LANES:(j + 1) * LANES]
            m = blk if m is None else jnp.maximum(m, blk)
    m = m.max(axis=-1, keepdims=True)
    if extra is not None:
        m = jnp.maximum(m, extra)
    acc = None
    for s, v in zip(s_list, v_list):
        pv = jnp.dot(jnp.exp2(s - m).astype(BF16), v, preferred_element_type=F32)
        acc = pv if acc is None else acc + pv
    o, den = acc[:, :LANES], acc[:, LANES:]
    if extra is not None:
        den = den + jnp.exp2(extra - m)
    return o / den


def _stage_heads(q_ref, k_ref, v_ref, qm_ref, kb_ref, vb_ref):
    m0 = lax.broadcasted_iota(jnp.int32, (TM, LANES), 1) < HEAD_DIM

    def stage(i, carry):
        r0 = pl.multiple_of(i * TM, TM)
        q = q_ref[pl.ds(r0, TM), :]
        qm_ref[0, pl.ds(r0, TM), :] = jnp.where(m0, q, 0.0).astype(BF16)
        qm_ref[1, pl.ds(r0, TM), :] = jnp.where(m0, 0.0, q).astype(BF16)
        kb_ref[pl.ds(r0, TM), :] = k_ref[pl.ds(r0, TM), :].astype(BF16)
        vb_ref[pl.ds(r0, TM), 0:LANES] = v_ref[pl.ds(r0, TM), :].astype(BF16)
        vb_ref[pl.ds(r0, TM), LANES:2 * LANES] = jnp.ones((TM, LANES), BF16)
        return carry

    lax.fori_loop(0, q_ref.shape[0] // TM, stage, 0)


def _na_kernel(q_ref, k_ref, v_ref, bias_ref, o_ref, qm_ref, kb_ref, vb_ref):
    s_len = q_ref.shape[0]
    rows = (s_len - TM) // GRID_W
    nloc = NA_KH * GRID_W
    _stage_heads(q_ref, k_ref, v_ref, qm_ref, kb_ref, vb_ref)

    kc = kb_ref[0:TM, :]
    vc = vb_ref[0:TM, :]
    outs = [_softmax_pv([_dot_nt(qm_ref[hh, 0:TM, :], kc)], [vc]) for hh in range(2)]
    m0c = lax.broadcasted_iota(jnp.int32, (TM, LANES), 1) < HEAD_DIM
    o_ref[0:TM, :] = jnp.where(m0c, outs[0], outs[1])

    m0 = lax.broadcasted_iota(jnp.int32, (GRID_W, LANES), 1) < HEAD_DIM

    def row_block(r, carry):
        rs = jnp.clip(r - NA_KH // 2, 0, rows - NA_KH)
        pat = r - rs
        q0 = pl.multiple_of(TM + r * GRID_W, GRID_W)
        k0 = pl.multiple_of(TM + rs * GRID_W, GRID_W)
        kl = kb_ref[pl.ds(k0, nloc), :]
        vl = vb_ref[pl.ds(k0, nloc), :]
        kc = kb_ref[0:TM, :]
        vc = vb_ref[0:TM, :]
        q = jnp.concatenate([qm_ref[0, pl.ds(q0, GRID_W), :], qm_ref[1, pl.ds(q0, GRID_W), :]], axis=0)
        s_loc = _dot_nt(q, kl) + bias_ref[pat].reshape(2 * GRID_W, nloc)
        s_ctx = _dot_nt(q, kc)
        res = _softmax_pv([s_loc, s_ctx], [vl, vc])
        o_ref[pl.ds(q0, GRID_W), :] = jnp.where(m0, res[:GRID_W], res[GRID_W:])
        return carry

    lax.fori_loop(0, rows, row_block, 0, unroll=4)


def _na_bias(rpb):
    h = rpb.shape[0]
    qc = np.arange(GRID_W)[:, None]
    kc = np.arange(GRID_W)[None, :]
    win = np.clip(qc - NA_KW // 2, 0, GRID_W - NA_KW)
    valid = (kc >= win) & (kc < win + NA_KW)
    col_off = np.clip(kc - qc + NA_KW - 1, 0, 2 * NA_KW - 2)
    onehot = (col_off[None] == np.arange(2 * NA_KW - 1)[:, None, None]).astype(np.float32)
    cols = jnp.einsum("hrc,cqk->hrqk", rpb.astype(F32), onehot, precision=lax.Precision.HIGHEST)
    cols = jnp.where(valid[None, None], cols * LOG2E, NEG_INF)
    bias = jnp.stack([cols[:, NA_KH - 1 - p:2 * NA_KH - 1 - p] for p in range(NA_KH)])
    bias = bias.transpose(0, 1, 3, 2, 4)
    return bias.reshape(NA_KH, h, GRID_W, NA_KH * GRID_W)


def _neighbourhood(p, bias):
    b, s, _ = p.shape
    nb = NA_W // LANES
    c0 = 4 * RET_W // LANES
    blk = lambda off: pl.BlockSpec((None, s, LANES), lambda bi, hp: (bi, 0, c0 + off + hp))
    return pl.pallas_call(
        _na_kernel,
        grid=(b, nb),
        in_specs=[
            blk(0), blk(nb), blk(2 * nb),
            pl.BlockSpec((NA_KH, 2, GRID_W, NA_KH * GRID_W), lambda bi, hp: (0, hp, 0, 0)),
        ],
        out_specs=pl.BlockSpec((None, s, LANES), lambda bi, hp: (bi, 0, hp)),
        out_shape=jax.ShapeDtypeStruct((b, s, NA_W), F32),
        scratch_shapes=[pltpu.VMEM((2, s, LANES), BF16), pltpu.VMEM((s, LANES), BF16),
                        pltpu.VMEM((s, 2 * LANES), BF16)],
        compiler_params=_cparams(("arbitrary", "arbitrary")),
        name="neighbourhood",
    )(p, p, p, bias)


def _swa_kernel(sink_ref, q_ref, k_ref, v_ref, o_ref, kd_ref, vd_ref):
    kp = pl.program_id(1)
    s_len = q_ref.shape[0]
    seq = s_len - TM
    nblk = seq // SWA_BLOCK
    band = SWA_BLOCK + 2 * SWA_WINDOW
    group = SWA_HEADS // SWA_KV_HEADS
    heads_per_step = 2 * group
    m0t = lax.broadcasted_iota(jnp.int32, (TM, LANES), 1) < HEAD_DIM

    def stage(i, carry):
        r0 = pl.multiple_of(i * TM, TM)
        for src, dst in ((k_ref, kd_ref), (v_ref, vd_ref)):
            x = src[pl.ds(r0, TM), :].astype(F32)
            xr = pltpu.roll(x, HEAD_DIM, 1)
            dst[0, pl.ds(r0, TM), 0:LANES] = jnp.where(m0t, x, xr).astype(BF16)
            dst[1, pl.ds(r0, TM), 0:LANES] = jnp.where(m0t, xr, x).astype(BF16)
        for h in range(2):
            vd_ref[h, pl.ds(r0, TM), LANES:2 * LANES] = jnp.ones((TM, LANES), BF16)
        return carry

    lax.fori_loop(0, s_len // TM, stage, 0)
    o_ref[0:TM, :] = jnp.zeros((TM, o_ref.shape[1]), F32)

    rows = group * SWA_BLOCK
    qi = lax.broadcasted_iota(jnp.int32, (rows, band), 0) % SWA_BLOCK
    ki = lax.broadcasted_iota(jnp.int32, (rows, band), 1)
    m0 = lax.broadcasted_iota(jnp.int32, (SWA_BLOCK, LANES), 1) < HEAD_DIM
    head_of_row = lax.broadcasted_iota(jnp.int32, (rows, 1), 0) // SWA_BLOCK

    def block(i, carry):
        start = jnp.clip((i - 1) * SWA_BLOCK, 0, seq - band)
        k0 = pl.multiple_of(TM + start, SWA_BLOCK)
        q0 = pl.multiple_of(TM + i * SWA_BLOCK, SWA_BLOCK)
        valid = jnp.abs(qi + (i * SWA_BLOCK - start) - ki) <= SWA_WINDOW
        for hh in range(2):
            kb = kd_ref[hh, pl.ds(k0, band), :]
            vb = vd_ref[hh, pl.ds(k0, band), :]
            kc = kd_ref[hh, 0:TM, :]
            vc = vd_ref[hh, 0:TM, :]
            parts = []
            for j in range(group // 2):
                pair = hh * (group // 2) + j
                qp = q_ref[pl.ds(q0, SWA_BLOCK), pair * LANES:(pair + 1) * LANES]
                parts += [jnp.where(m0, qp, 0.0), jnp.where(m0, 0.0, qp)]
            q = jnp.concatenate(parts, axis=0).astype(BF16)
            s = jnp.where(valid, _dot_nt(q, kb), NEG_INF)
            s_ctx = _dot_nt(q, kc)
            h0 = kp * heads_per_step + hh * group
            sink = jnp.full((rows, 1), sink_ref[h0], F32)
            for g in range(1, group):
                sink = jnp.where(head_of_row == g, sink_ref[h0 + g], sink)
            res = _softmax_pv([s, s_ctx], [vb, vc], extra=sink * LOG2E)
            for j in range(group // 2):
                pair = hh * (group // 2) + j
                r0 = 2 * j * SWA_BLOCK
                o_ref[pl.ds(q0, SWA_BLOCK), pair * LANES:(pair + 1) * LANES] = jnp.where(
                    m0, res[r0:r0 + SWA_BLOCK], res[r0 + SWA_BLOCK:r0 + 2 * SWA_BLOCK])
        return carry

    lax.fori_loop(0, nblk, block, 0, unroll=2)


def _swa(p, sink):
    b, s, _ = p.shape
    qw = SWA_HEADS * HEAD_DIM // 2
    kblk = SWA_HEADS * HEAD_DIM // LANES
    vblk = kblk + SWA_KV_HEADS * HEAD_DIM // LANES
    return pl.pallas_call(
        _swa_kernel,
        grid=(b, 2),
        in_specs=[
            pl.BlockSpec(memory_space=pltpu.SMEM),
            pl.BlockSpec((None, s, qw), lambda bi, kp: (bi, 0, kp)),
            pl.BlockSpec((None, s, LANES), lambda bi, kp: (bi, 0, kblk + kp)),
            pl.BlockSpec((None, s, LANES), lambda bi, kp: (bi, 0, vblk + kp)),
        ],
        out_specs=pl.BlockSpec((None, s, qw), lambda bi, kp: (bi, 0, kp)),
        out_shape=jax.ShapeDtypeStruct((b, s, SWA_HEADS * HEAD_DIM), F32),
        scratch_shapes=[pltpu.VMEM((2, s, LANES), BF16), pltpu.VMEM((2, s, 2 * LANES), BF16)],
        compiler_params=_cparams(("arbitrary", "arbitrary")),
        name="swa",
    )(sink, p, p, p)


def _outproj_kernel(*refs, n_mix, n_stream, t0):
    mix_refs = refs[:n_mix]
    x_refs = refs[n_mix + 1:n_mix + 1 + n_stream]
    w_ref = refs[n_mix]
    (g1_ref, gn_ref, sh_ref, sc_ref, wr_ref, br_ref,
     xo_ref, h_ref, ri_ref, cnt_ref, carry_ref) = refs[n_mix + 1 + n_stream:]
    first = (pl.program_id(0) == 0) & (pl.program_id(1) == 0)

    @pl.when(first)
    def _():
        carry_ref[...] = jnp.zeros_like(carry_ref)

    o = None
    off = 0
    for m_ref in mix_refs:
        kw = m_ref.shape[1]
        part = jnp.dot(m_ref[...].astype(BF16), w_ref[off:off + kw, :], preferred_element_type=F32)
        o = part if o is None else o + part
        off += kw
    xn = _stream_tile(x_refs, pl.program_id(1) + t0 > 0) + g1_ref[...] * o
    xo_ref[...] = xn
    h = _rms_mod(xn, gn_ref[...], sh_ref[...], sc_ref[...])
    h_ref[...] = h

    logits = _dot3_split(h, wr_ref[0], wr_ref[1]) + br_ref[...]
    tm = logits.shape[0]
    lane = lax.broadcasted_iota(jnp.int32, (tm, LANES), 1).astype(F32)
    big = 1e9
    gmask = lane < MOE_GROUPS
    mg = jnp.max(jnp.where(gmask, logits, -big), axis=-1, keepdims=True)
    sg = jnp.sum(jnp.where(gmask, jnp.exp(jnp.minimum(logits - mg, 0.0)), 0.0), axis=-1, keepdims=True)
    gw = 1.0 / sg
    gi = jnp.min(jnp.where(gmask & (logits == mg), lane, big), axis=-1, keepdims=True)
    lo = ROUTE_LANE0 + MOE_EPG * gi
    emask = (lane >= lo) & (lane < lo + MOE_EPG)
    l1 = jnp.max(jnp.where(emask, logits, -big), axis=-1, keepdims=True)
    i1 = jnp.min(jnp.where(emask & (logits == l1), lane, big), axis=-1, keepdims=True)
    emask2 = emask & (lane != i1)
    l2 = jnp.max(jnp.where(emask2, logits, -big), axis=-1, keepdims=True)
    i2 = jnp.min(jnp.where(emask2 & (logits == l2), lane, big), axis=-1, keepdims=True)
    e21 = jnp.exp(l2 - l1)
    w1 = gw / (1.0 + e21)
    w2 = gw * e21 / (1.0 + e21)

    oh = jnp.where((lane == i1) | (lane == i2), 1.0, 0.0)
    tri = (lax.broadcasted_iota(jnp.int32, (tm, tm), 0) > lax.broadcasted_iota(jnp.int32, (tm, tm), 1))
    cum = jnp.dot(jnp.where(tri, 1.0, 0.0).astype(BF16), oh.astype(BF16), preferred_element_type=F32) + carry_ref[...]
    r1 = jnp.sum(jnp.where(lane == i1, cum, 0.0), axis=-1, keepdims=True)
    r2 = jnp.sum(jnp.where(lane == i2, cum, 0.0), axis=-1, keepdims=True)
    carry_ref[...] = carry_ref[...] + jnp.sum(oh, axis=0, keepdims=True)
    cnt_ref[...] = carry_ref[...]

    ri = jnp.where(lane == 0, i1 - ROUTE_LANE0, 0.0)
    ri = jnp.where(lane == 1, i2 - ROUTE_LANE0, ri)
    ri = jnp.where(lane == 2, w1, ri)
    ri = jnp.where(lane == 3, w2, ri)
    ri = jnp.where(lane == 4, r1, ri)
    ri = jnp.where(lane == 5, r2, ri)
    ri_ref[...] = ri


def _outproj(mixes, w_bf16, xa, g1, gain2, sh2, sc2, wr, br, t0):
    b, s, d = _stream_shape(xa)
    nt = s // TM - t0
    so = nt * TM
    mod_idx = lambda bi, t: (jnp.where(t + t0 == 0, b, bi), 0, 0)
    tile = lambda wdt: pl.BlockSpec((None, TM, wdt), lambda bi, t: (bi, t + t0, 0))
    otile = lambda wdt: pl.BlockSpec((None, TM, wdt), lambda bi, t: (bi, t, 0))
    const = lambda shape: pl.BlockSpec(shape, lambda bi, t: (0,) * len(shape))
    x_specs, x_args = _stream_specs(xa, t0)
    in_specs = [tile(m.shape[2]) for m in mixes] + [const(w_bf16.shape)] + x_specs + [
        pl.BlockSpec((None, 1, d), mod_idx), const((1, d)),
        pl.BlockSpec((None, 1, d), mod_idx), pl.BlockSpec((None, 1, d), mod_idx),
        const((2, d, LANES)), const((1, LANES)),
    ]
    return pl.pallas_call(
        functools.partial(_outproj_kernel, n_mix=len(mixes), n_stream=len(x_args), t0=t0),
        grid=(b, nt),
        in_specs=in_specs,
        out_specs=[otile(d), otile(d), otile(LANES), const((1, LANES))],
        out_shape=[jax.ShapeDtypeStruct((b, so, d), F32), jax.ShapeDtypeStruct((b, so, d), F32),
                   jax.ShapeDtypeStruct((b, so, LANES), F32), jax.ShapeDtypeStruct((1, LANES), F32)],
        scratch_shapes=[pltpu.VMEM((1, LANES), F32)],
        compiler_params=_cparams(("arbitrary", "arbitrary")),
        name="outproj",
    )(*mixes, w_bf16, *x_args, g1, gain2, sh2, sc2, wr, br)


SUBLANES = 8


def _to_token_tiles(x):
    return x.reshape(x.shape[0], SUBLANES, x.shape[1] // SUBLANES)


def _from_token_tiles(x3):
    return x3.reshape(x3.shape[0], x3.shape[1] * x3.shape[2])


PAD_BITS = tuple(1 << k for k in reversed(range(MOE_ROWS.bit_length() - 1)))


def _dispatch_kernel(dest_ref, pads_ref, h_ref, xs_ref, hbuf, zbuf, sem, zsem):
    nt = pl.num_programs(1)
    step = pl.program_id(0) * nt + pl.program_id(1)
    last = pl.num_programs(0) * nt - 1
    slot = step % 2

    def zero_pads(wait):
        def expert(e, c):
            pos = pads_ref[0, e]
            n = pads_ref[1, e]
            for bit in PAD_BITS:
                take = (n & bit) != 0
                cp = pltpu.make_async_copy(zbuf.at[pl.ds(0, bit)], xs_ref.at[pl.ds(pos, bit)], zsem)

                @pl.when(take)
                def _():
                    cp.wait() if wait else cp.start()

                pos = pos + jnp.where(take, bit, 0)
            return c
        lax.fori_loop(0, MOE_EXPERTS, expert, 0)

        def tail(j, c):
            cp = pltpu.make_async_copy(
                zbuf, xs_ref.at[pl.ds(pads_ref[0, MOE_EXPERTS] + j * PAD_BITS[0], PAD_BITS[0])], zsem)
            cp.wait() if wait else cp.start()
            return c
        lax.fori_loop(0, pads_ref[1, MOE_EXPERTS], tail, 0)

    @pl.when(step == 0)
    def _():
        zbuf[...] = jnp.zeros_like(zbuf)
        zero_pads(wait=False)
    hbuf[slot] = _to_token_tiles(h_ref[...])

    def copy(sl, i, dst_row):
        return pltpu.make_async_copy(hbuf.at[sl, i], xs_ref.at[dst_row], sem.at[sl])

    for i in range(TM):
        copy(slot, i, dest_ref[step, i]).start()
        copy(slot, i, dest_ref[step, TM + i]).start()

    def drain(sl):
        def one(i, c):
            copy(sl, 0, 0).wait()
            return c
        lax.fori_loop(0, 2 * TM, one, 0, unroll=8)

    @pl.when(step > 0)
    def _():
        drain(1 - slot)

    @pl.when(step == last)
    def _():
        drain(slot)
        zero_pads(wait=True)


def _dispatch(dest, pads, h2, n_pad):
    b, s, d = h2.shape
    nt = s // TM
    tile = (SUBLANES, d // SUBLANES)
    return pl.pallas_call(
        _dispatch_kernel,
        grid_spec=pltpu.PrefetchScalarGridSpec(
            num_scalar_prefetch=2,
            grid=(b, nt),
            in_specs=[pl.BlockSpec((None, TM, d), lambda bi, t, dr, pd: (bi, t, 0))],
            out_specs=pl.BlockSpec(memory_space=pl.ANY),
            scratch_shapes=[pltpu.VMEM((2, TM) + tile, F32), pltpu.VMEM((PAD_BITS[0],) + tile, F32),
                            pltpu.SemaphoreType.DMA((2,)), pltpu.SemaphoreType.DMA],
        ),
        out_shape=jax.ShapeDtypeStruct((n_pad,) + tile, F32),
        compiler_params=_cparams(("arbitrary", "arbitrary")),
        name="dispatch",
    )(dest, pads, h2)


def _mlp_kernel(be_ref, nu_ref, x_ref, wgu_ref, wd_ref, y_ref, wgu_b, wd_b):
    i = pl.program_id(0)
    prev = be_ref[jnp.maximum(i - 1, 0)]
    used = i < nu_ref[0]

    @pl.when(used & ((i == 0) | (be_ref[i] != prev)))
    def _():
        wgu_b[...] = wgu_ref[...].astype(BF16)
        wd_b[...] = wd_ref[...].astype(BF16)

    @pl.when(used)
    def _():
        x = _from_token_tiles(x_ref[...])
        gu = jnp.dot(x.astype(BF16), wgu_b[...], preferred_element_type=F32)
        act = _silu(gu[:, :MOE_FF]) * gu[:, MOE_FF:]
        y_ref[...] = _to_token_tiles(jnp.dot(act.astype(BF16), wd_b[...], preferred_element_type=F32))

    @pl.when(jnp.logical_not(used))
    def _():
        y_ref[...] = jnp.zeros_like(y_ref)


def _expert_mlp(block_e, n_used, xs, w_gu, w_down, layer):
    n_pad, sub, dl = xs.shape
    d = sub * dl
    ff2 = w_gu.shape[-1]
    slots = pl.BlockSpec((MOE_ROWS, sub, dl), lambda i, be, nu: (i, 0, 0))
    used_slots = pl.BlockSpec((MOE_ROWS, sub, dl), lambda i, be, nu: (jnp.minimum(i, nu[0] - 1), 0, 0))
    return pl.pallas_call(
        _mlp_kernel,
        grid_spec=pltpu.PrefetchScalarGridSpec(
            num_scalar_prefetch=2,
            grid=(n_pad // MOE_ROWS,),
            in_specs=[
                used_slots,
                pl.BlockSpec((None, None, d, ff2), lambda i, be, nu: (layer, be[i], 0, 0)),
                pl.BlockSpec((None, None, ff2 // 2, d), lambda i, be, nu: (layer, be[i], 0, 0)),
            ],
            out_specs=slots,
            scratch_shapes=[pltpu.VMEM((d, ff2), BF16), pltpu.VMEM((ff2 // 2, d), BF16)],
        ),
        out_shape=jax.ShapeDtypeStruct(xs.shape, F32),
        compiler_params=_cparams(("arbitrary",)),
        name="expert_mlp",
    )(block_e, n_used, xs, w_gu, w_down)


def _combine_kernel(dest_ref, ys_ref, x_ref, ri_ref, g2_ref, fg_ref, o_ref, buf, sem, *, final):
    bi = pl.program_id(0)
    t = pl.program_id(1)
    nt = pl.num_programs(1)
    step = bi * nt + t
    total = pl.num_programs(0) * nt

    def copy(src_row, slot, k, i):
        return pltpu.make_async_copy(ys_ref.at[src_row], buf.at[slot, k, i], sem.at[slot])

    def issue(st, slot):
        for i in range(TM):
            copy(dest_ref[st, i], slot, 0, i).start()
            copy(dest_ref[st, TM + i], slot, 1, i).start()

    slot = step % 2

    @pl.when(step == 0)
    def _():
        issue(0, 0)

    @pl.when(step + 1 < total)
    def _():
        issue(step + 1, 1 - slot)

    def drain(i, c):
        copy(0, slot, 0, 0).wait()
        return c

    lax.fori_loop(0, 2 * TM, drain, 0, unroll=8)

    lane = lax.broadcasted_iota(jnp.int32, (TM, LANES), 1)
    ri = ri_ref[...]
    w1 = jnp.sum(jnp.where(lane == 2, ri, 0.0), axis=-1, keepdims=True)
    w2 = jnp.sum(jnp.where(lane == 3, ri, 0.0), axis=-1, keepdims=True)
    y = _from_token_tiles(buf[slot, 0]) * w1 + _from_token_tiles(buf[slot, 1]) * w2
    xn = x_ref[...] + g2_ref[...] * y
    if final:
        ms = jnp.mean(xn * xn, axis=-1, keepdims=True)
        xn = xn * lax.rsqrt(ms + NORM_EPS) * fg_ref[...]
    o_ref[...] = xn


def _combine(dest, ys, xa, rinfo, g2, final_g, has_ctx, final):
    b, s, d = xa.shape
    nt = s // TM
    mod_idx = lambda bi, t, dr: (jnp.where(t == 0, b, bi) if has_ctx else bi, 0, 0)
    tile = lambda wdt: pl.BlockSpec((None, TM, wdt), lambda bi, t, dr: (bi, t, 0))
    out_spec = tile(d)
    out_shape = jax.ShapeDtypeStruct((b, s, d), F32)
    return pl.pallas_call(
        functools.partial(_combine_kernel, final=final),
        grid_spec=pltpu.PrefetchScalarGridSpec(
            num_scalar_prefetch=1,
            grid=(b, nt),
            in_specs=[
                pl.BlockSpec(memory_space=pl.ANY),
                tile(d), tile(LANES),
                pl.BlockSpec((None, 1, d), mod_idx),
                pl.BlockSpec((1, d), lambda bi, t, dr: (0, 0)),
            ],
            out_specs=out_spec,
            scratch_shapes=[pltpu.VMEM((2, 2, TM, SUBLANES, d // SUBLANES), F32),
                            pltpu.SemaphoreType.DMA((2,))],
        ),
        out_shape=out_shape,
        compiler_params=_cparams(("arbitrary", "arbitrary")),
        name="combine",
    )(dest, ys, xa, rinfo, g2, final_g)


def _moe(h2, rinfo, counts, w_gu, w_down, layer):
    b, s, d = h2.shape
    nt = s // TM
    r = rinfo[:, :, :8].reshape(b * nt, TM, 8)
    lanes = lambda i: jnp.concatenate([r[..., i], r[..., i + 1]], axis=1).astype(jnp.int32)
    e = lanes(0)
    rank = lanes(4)
    cnt = counts[0, ROUTE_LANE0:ROUTE_LANE0 + MOE_EXPERTS].astype(jnp.int32)
    padded = (cnt + MOE_ROWS - 1) // MOE_ROWS * MOE_ROWS
    ends = jnp.cumsum(padded)
    starts = ends - padded
    eids = jnp.arange(MOE_EXPERTS, dtype=jnp.int32)
    dest = jnp.sum(jnp.where(e[..., None] == eids, starts, 0), axis=-1) + rank
    n_assign = b * nt * TM * 2
    n_blocks = (n_assign + MOE_EXPERTS * (MOE_ROWS - 1)) // MOE_ROWS + 1
    n_pad = n_blocks * MOE_ROWS
    blk_row = jnp.arange(n_blocks, dtype=jnp.int32) * MOE_ROWS
    block_e = jnp.minimum(jnp.sum((blk_row[:, None] >= ends[None, :]).astype(jnp.int32), axis=1),
                          MOE_EXPERTS - 1)
    n_used = (ends[-1] // MOE_ROWS).astype(jnp.int32).reshape(1)
    pads = jnp.stack([jnp.append(starts + cnt, ends[-1]),
                      jnp.append(padded - cnt, (n_pad - ends[-1]) // PAD_BITS[0])]).astype(jnp.int32)
    xs = _dispatch(dest, pads, h2, n_pad)
    ys = _expert_mlp(block_e, n_used, xs, w_gu, w_down, layer)
    return dest, ys


def _router_weights(wg, bg, we, be):
    d = wg.shape[0]
    pad = LANES - MOE_GROUPS - MOE_EXPERTS
    assert ROUTE_LANE0 == MOE_GROUPS
    wr = jnp.concatenate([wg.astype(F32), we.astype(F32), jnp.zeros((d, pad), F32)], axis=1)
    br = jnp.concatenate([bg.astype(F32), be.astype(F32), jnp.zeros((pad,), F32)]).reshape(1, LANES)
    return jnp.stack(_split_bf16(wr)), br


def kernel(x, c, ctx, c_ctx, ada_w, ada_b, norm_g, final_g, ab_w_in, ab_w_out, ret_decay, ret_gn, na_rpb,
           swa_w_in, swa_w_out, swa_sink, router_g_w, router_g_b, router_e_w, router_e_b,
           expert_w_gu, expert_w_down):
    b, seq, d = x.shape
    assert ctx.shape[1] == TM and seq % TM == 0 and d == D_MODEL
    xa = (ctx, x)
    rope = _rope_tables(seq)

    cvec = jnp.concatenate([c, c_ctx[None, :], jnp.zeros((7, d), F32)], axis=0)
    mod = _adaln(cvec, ada_w, ada_b)
    mod = mod.reshape(DEPTH, b + 8, 6, 1, d)[:, :b + 1].transpose(0, 2, 1, 3, 4)

    for layer in range(DEPTH):
        last = layer == DEPTH - 1
        sh1, sc1, g1, sh2, sc2, g2 = (mod[layer, i] for i in range(6))
        gain1 = norm_g[layer, 0].reshape(1, d)
        gain2 = norm_g[layer, 1].reshape(1, d)
        j = layer // 2
        if layer % 2 == 0:
            nb = RET_W // LANES
            qk_scale = HEAD_DIM ** -0.5
            scales = {blk: qk_scale for blk in range(nb, 2 * nb)}
            scales.update({blk: qk_scale * LOG2E for blk in range(4 * nb, 5 * nb)})
            p = _proj(xa, gain1, sh1, sc1, ab_w_in[j].astype(BF16), rope,
                      rope_blocks=range(0, 2 * nb), scales=scales)
            log_gamma = jnp.log1p(-jnp.exp2(-ret_decay[j].astype(F32)))
            mixes = [_retention(p, log_gamma, ret_gn[j]), _neighbourhood(p, _na_bias(na_rpb[j]))]
            w_out = ab_w_out[j]
        else:
            qb = SWA_HEADS * HEAD_DIM // LANES
            kb = SWA_KV_HEADS * HEAD_DIM // LANES
            p = _proj(xa, gain1, sh1, sc1, swa_w_in[j].astype(BF16), rope,
                      rope_blocks=range(0, qb + kb),
                      scales={blk: HEAD_DIM ** -0.5 * LOG2E for blk in range(qb)})
            mixes = [_swa(p, swa_sink[j].astype(F32))]
            w_out = swa_w_out[j]
        t0 = 1 if last else 0
        wr, br = _router_weights(router_g_w[layer], router_g_b[layer], router_e_w[layer], router_e_b[layer])
        xa, h2, rinfo, counts = _outproj(mixes, w_out.astype(BF16), xa, g1, gain2, sh2, sc2, wr, br, t0)
        dest, ys = _moe(h2, rinfo, counts, expert_w_gu, expert_w_down, layer)
        xa = _combine(dest, ys, xa, rinfo, g2, final_g.reshape(1, d), not last, last)
    return xa
```

```python
import functools

import numpy as np
import jax
import jax.numpy as jnp
from jax import lax
from jax.experimental import pallas as pl
from jax.experimental.pallas import tpu as pltpu

F32 = jnp.float32
BF16 = jnp.bfloat16

D_MODEL = 1024
DEPTH = 2
GRID_W = 64
HEAD_DIM = 64
RET_HEADS = 8
NA_HEADS = 8
RET_W = 512
NA_W = 512
AB_IN = 4 * RET_W + 3 * NA_W
RET_CHUNK = 128
GN_EPS = 1e-5
NA_KH = 8
NA_KW = 16
SWA_HEADS = 16
SWA_KV_HEADS = 4
SWA_WINDOW = 128
SWA_BLOCK = 128
SWA_IN = (SWA_HEADS + 2 * SWA_KV_HEADS) * HEAD_DIM
ROPE_BASE = 10000.0
MOE_GROUPS = 4
MOE_EPG = 8
MOE_EXPERTS = 32
MOE_FF = 512
NORM_EPS = 1e-6
NEG_INF = -1e30

LANES = 128
TM = 256
RET_BLOCK = 256
MOE_ROWS = 1024
ROUTE_LANE0 = 4
VMEM_LIMIT = 56 * 1024 * 1024


def _cparams(sem, vmem=VMEM_LIMIT):
    return pltpu.CompilerParams(dimension_semantics=sem, vmem_limit_bytes=vmem)


def _split_bf16(a):
    hi = a.astype(BF16)
    lo = (a - hi.astype(F32)).astype(BF16)
    return hi, lo


def _dot3_split(a, bh, bl):
    ah, al = _split_bf16(a)
    d = lambda x, y: jnp.dot(x, y, preferred_element_type=F32)
    return d(ah, bh) + (d(ah, bl) + d(al, bh))


def _dot3(a, b):
    return _dot3_split(a, *_split_bf16(b))


def _dot_nt(a, b):
    return lax.dot_general(a, b, (((1,), (1,)), ((), ())), preferred_element_type=F32)


def _dot_tn(a, b):
    return lax.dot_general(a, b, (((0,), (0,)), ((), ())), preferred_element_type=F32)


def _silu(x):
    return x / (1.0 + jnp.exp(-x))


def _adaln_kernel(c_ref, w_ref, b_ref, o_ref):
    o_ref[...] = _dot3(_silu(c_ref[...]), w_ref[...]) + b_ref[...]


def _adaln(cvec, ada_w, ada_b):
    depth, d, n6 = ada_w.shape
    rows = cvec.shape[0]
    tn = 1024
    return pl.pallas_call(
        _adaln_kernel,
        grid=(depth, n6 // tn),
        in_specs=[
            pl.BlockSpec((rows, d), lambda l, j: (0, 0)),
            pl.BlockSpec((None, d, tn), lambda l, j: (l, 0, j)),
            pl.BlockSpec((None, 1, tn), lambda l, j: (l, 0, j)),
        ],
        out_specs=pl.BlockSpec((None, rows, tn), lambda l, j: (l, 0, j)),
        out_shape=jax.ShapeDtypeStruct((depth, rows, n6), F32),
        compiler_params=_cparams(("arbitrary", "arbitrary")),
        name="adaln",
    )(cvec, ada_w, ada_b.reshape(depth, 1, n6))


def _rms_mod(x, g, sh, sc):
    ms = jnp.mean(x * x, axis=-1, keepdims=True)
    return (x * lax.rsqrt(ms + NORM_EPS) * g) * (1.0 + sc) + sh


def _stream_tile(refs, is_lat):
    if len(refs) == 1:
        return refs[0][...]
    return jnp.where(is_lat, refs[1][...], refs[0][...])


def _stream_specs(stream, t0, extra_args=0):
    def im(f):
        return (lambda bi, t, *_: f(bi, t + t0))
    if not isinstance(stream, tuple):
        return [pl.BlockSpec((None, TM, stream.shape[2]), im(lambda bi, t: (bi, t, 0)))], [stream]
    ctx, x = stream
    d = x.shape[2]
    return ([pl.BlockSpec((None, TM, d), im(lambda bi, t: (bi, 0, 0))),
             pl.BlockSpec((None, TM, d), im(lambda bi, t: (bi, jnp.maximum(t - 1, 0), 0)))], [ctx, x])


def _stream_shape(stream):
    if not isinstance(stream, tuple):
        return stream.shape
    ctx, x = stream
    return (x.shape[0], ctx.shape[1] + x.shape[1], x.shape[2])


def _proj_kernel(*refs, rope_blocks, scales, cn, n_stream):
    x_refs = refs[:n_stream]
    g_ref, sh_ref, sc_ref, w_ref, rope_ref, o_ref = refs[n_stream:]
    is_lat = pl.program_id(1) > 0
    hb = _rms_mod(_stream_tile(x_refs, is_lat), g_ref[...], sh_ref[...], sc_ref[...]).astype(BF16)
    nout = w_ref.shape[1]
    for c in range(nout // cn):
        o = jnp.dot(hb, w_ref[:, c * cn:(c + 1) * cn], preferred_element_type=F32)
        for s in range(cn // LANES):
            blk = c * (cn // LANES) + s
            ob = o[:, s * LANES:(s + 1) * LANES]
            if blk in rope_blocks:
                r = (ob * rope_ref[0] + pltpu.roll(ob, 16, 1) * rope_ref[1]
                     + pltpu.roll(ob, LANES - 16, 1) * rope_ref[2])
                ob = jnp.where(is_lat, r, ob)
            if blk in scales:
                ob = ob * scales[blk]
            o_ref[:, blk * LANES:(blk + 1) * LANES] = ob.astype(o_ref.dtype)


def _proj(xa, gain, sh, sc, w_bf16, rope, rope_blocks, scales):
    b, s, d = _stream_shape(xa)
    nout = w_bf16.shape[1]
    nt = s // TM
    mod_idx = lambda bi, t: (jnp.where(t == 0, b, bi), 0, 0)
    x_specs, x_args = _stream_specs(xa, 0)
    kern = functools.partial(_proj_kernel, rope_blocks=frozenset(rope_blocks), scales=dict(scales), cn=512,
                             n_stream=len(x_args))
    return pl.pallas_call(
        kern,
        grid=(b, nt),
        in_specs=x_specs + [
            pl.BlockSpec((1, d), lambda bi, t: (0, 0)),
            pl.BlockSpec((None, 1, d), mod_idx),
            pl.BlockSpec((None, 1, d), mod_idx),
            pl.BlockSpec((d, nout), lambda bi, t: (0, 0)),
            pl.BlockSpec((3, TM, LANES), lambda bi, t: (0, jnp.maximum(t - 1, 0), 0)),
        ],
        out_specs=pl.BlockSpec((None, TM, nout), lambda bi, t: (bi, t, 0)),
        out_shape=jax.ShapeDtypeStruct((b, s, nout), BF16),
        compiler_params=_cparams(("arbitrary", "arbitrary")),
        name="proj",
    )(*x_args, gain, sh, sc, w_bf16, rope)


def _rope_tables(seq):
    nf = HEAD_DIM // 4
    inv = ROPE_BASE ** (-jnp.arange(nf, dtype=F32) / nf)
    t = jnp.arange(seq)
    row = (t // GRID_W).astype(F32)
    col = (t % GRID_W).astype(F32)
    lane = np.arange(LANES)
    jj = lane % HEAD_DIM
    axis_is_col = (jj // 32) == 1
    second_half = (jj % 32) >= 16
    f = jj % 16
    pos = jnp.where(axis_is_col[None, :], col[:, None], row[:, None])
    ang = pos * inv[f][None, :]
    c, s = jnp.cos(ang), jnp.sin(ang)
    sa = jnp.where(second_half[None, :], s, 0.0)
    sb = jnp.where(second_half[None, :], 0.0, -s)
    return jnp.stack([c, sa, sb], axis=0)


def _ret_kernel(lg_ref, q_ref, k_ref, v_ref, g_ref, gn_ref, o_ref, accf_ref, accb_ref,
                intra_ref, qd_ref, kd_ref):
    hp = pl.program_id(1)
    c = RET_BLOCK
    s_len = q_ref.shape[0]
    n_chunks = s_len // c
    ctx_chunks = TM // c
    pos = lax.broadcasted_iota(jnp.int32, (c, LANES), 0).astype(F32)
    m0 = lax.broadcasted_iota(jnp.int32, (c, LANES), 1) < HEAD_DIM
    same_head = ((lax.broadcasted_iota(jnp.int32, (LANES, LANES), 0) < HEAD_DIM)
                 == (lax.broadcasted_iota(jnp.int32, (LANES, LANES), 1) < HEAD_DIM))
    rel = (lax.broadcasted_iota(jnp.int32, (c, c), 0) - lax.broadcasted_iota(jnp.int32, (c, c), 1)).astype(F32)
    lgf = [lg_ref[0, hp * 2 + hh] for hh in range(2)]
    lgb = [lg_ref[1, hp * 2 + hh] for hh in range(2)]
    lgf_l = jnp.where(m0, lgf[0], lgf[1])
    lgb_l = jnp.where(m0, lgb[0], lgb[1])
    for hh in range(2):
        intra_ref[0, :, hh * c:(hh + 1) * c] = jnp.where(rel >= 0, jnp.exp(lgf[hh] * jnp.maximum(rel, 0.0)), 0.0)
        intra_ref[1, :, hh * c:(hh + 1) * c] = jnp.where(rel <= 0, jnp.exp(lgb[hh] * jnp.maximum(-rel, 0.0)), 0.0)
    qd_ref[0] = jnp.exp(lgf_l * (pos + 1.0))
    qd_ref[1] = jnp.exp(lgb_l * (c - pos))
    kd_ref[0] = jnp.exp(lgf_l * (c - 1.0 - pos))
    kd_ref[1] = jnp.exp(lgb_l * pos)
    cd = [jnp.exp(lgf_l[0:1] * float(c)), jnp.exp(lgb_l[0:1] * float(c))]

    def chunk(r0, state, d):
        q = q_ref[pl.ds(r0, c), :]
        k = k_ref[pl.ds(r0, c), :]
        v = v_ref[pl.ds(r0, c), :]
        qb = q.astype(BF16)
        kcat = jnp.concatenate([jnp.where(m0, k, 0.0), jnp.where(m0, 0.0, k)], axis=0).astype(BF16)
        vcat = jnp.concatenate([jnp.where(m0, v, 0.0), jnp.where(m0, 0.0, v)], axis=0).astype(BF16)
        s = _dot_nt(qb, kcat) * intra_ref[d]
        out = jnp.dot(s.astype(BF16), vcat, preferred_element_type=F32)
        out = out + jnp.dot(qb, state.astype(BF16), preferred_element_type=F32) * qd_ref[d]
        kv = _dot_tn((k * kd_ref[d]).astype(BF16), v.astype(BF16))
        return out, state * cd[d] + jnp.where(same_head, kv, 0.0)

    def body(i, states):
        sf, sb = states
        rf = pl.multiple_of(i * c, c)
        ib = jnp.where(i < ctx_chunks, ctx_chunks - 1 - i, n_chunks + ctx_chunks - 1 - i)
        rb = pl.multiple_of(ib * c, c)
        of, sf = chunk(rf, sf, 0)
        ob, sb = chunk(rb, sb, 1)
        accf_ref[pl.ds(rf, c), :] = of
        accb_ref[pl.ds(rb, c), :] = ob
        return sf, sb

    z = jnp.zeros((LANES, LANES), F32)
    lax.fori_loop(0, n_chunks, body, (z, z), unroll=3)

    avg = jnp.where(same_head, 1.0 / HEAD_DIM, 0.0).astype(BF16)

    def head_mean(x):
        hi, lo = _split_bf16(x)
        return (jnp.dot(hi, avg, preferred_element_type=F32) + jnp.dot(lo, avg, preferred_element_type=F32))

    def readout(i, carry):
        r0 = pl.multiple_of(i * c, c)
        o = accf_ref[pl.ds(r0, c), :] + accb_ref[pl.ds(r0, c), :]
        dlt = o - head_mean(o)
        var = head_mean(dlt * dlt)
        y = dlt * lax.rsqrt(var + GN_EPS) * gn_ref[...]
        o_ref[pl.ds(r0, c), :] = _silu(g_ref[pl.ds(r0, c), :].astype(F32)) * y
        return carry

    lax.fori_loop(0, n_chunks, readout, 0, unroll=2)


def _retention(p, log_gamma, ret_gn):
    b, s, _ = p.shape
    nb = RET_W // LANES
    blk = lambda off: pl.BlockSpec((None, s, LANES), lambda bi, hp: (bi, 0, off + hp))
    return pl.pallas_call(
        _ret_kernel,
        grid=(b, nb),
        in_specs=[
            pl.BlockSpec(memory_space=pltpu.SMEM),
            blk(0), blk(nb), blk(2 * nb), blk(3 * nb),
            pl.BlockSpec((1, LANES), lambda bi, hp: (0, hp)),
        ],
        out_specs=pl.BlockSpec((None, s, LANES), lambda bi, hp: (bi, 0, hp)),
        out_shape=jax.ShapeDtypeStruct((b, s, RET_W), F32),
        scratch_shapes=[pltpu.VMEM((s, LANES), F32), pltpu.VMEM((s, LANES), F32),
                        pltpu.VMEM((2, RET_BLOCK, 2 * RET_BLOCK), F32),
                        pltpu.VMEM((2, RET_BLOCK, LANES), F32), pltpu.VMEM((2, RET_BLOCK, LANES), F32)],
        compiler_params=_cparams(("arbitrary", "arbitrary")),
        name="retention",
    )(log_gamma, p, p, p, p, ret_gn.reshape(1, RET_W))


LOG2E = 1.4426950408889634


def _softmax_pv(s_list, v_list, extra=None):
    m = None
    for s in s_list:
        for j in range(s.shape[1] // LANES):
            blk = s[:, j * LANES:(j + 1) * LANES]
            m = blk if m is None else jnp.maximum(m, blk)
    m = m.max(axis=-1, keepdims=True)
    if extra is not None:
        m = jnp.maximum(m, extra)
    acc = None
    for s, v in zip(s_list, v_list):
        pv = jnp.dot(jnp.exp2(s - m).astype(BF16), v, preferred_element_type=F32)
        acc = pv if acc is None else acc + pv
    o, den = acc[:, :LANES], acc[:, LANES:]
    if extra is not None:
        den = den + jnp.exp2(extra - m)
    return o / den


def _stage_heads(q_ref, k_ref, v_ref, qm_ref, kb_ref, vb_ref):
    m0 = lax.broadcasted_iota(jnp.int32, (TM, LANES), 1) < HEAD_DIM

    def stage(i, carry):
        r0 = pl.multiple_of(i * TM, TM)
        q = q_ref[pl.ds(r0, TM), :]
        qm_ref[0, pl.ds(r0, TM), :] = jnp.where(m0, q, 0.0).astype(BF16)
        qm_ref[1, pl.ds(r0, TM), :] = jnp.where(m0, 0.0, q).astype(BF16)
        kb_ref[pl.ds(r0, TM), :] = k_ref[pl.ds(r0, TM), :].astype(BF16)
        vb_ref[pl.ds(r0, TM), 0:LANES] = v_ref[pl.ds(r0, TM), :].astype(BF16)
        vb_ref[pl.ds(r0, TM), LANES:2 * LANES] = jnp.ones((TM, LANES), BF16)
        return carry

    lax.fori_loop(0, q_ref.shape[0] // TM, stage, 0)


def _na_kernel(q_ref, k_ref, v_ref, bias_ref, o_ref, qm_ref, kb_ref, vb_ref):
    s_len = q_ref.shape[0]
    rows = (s_len - TM) // GRID_W
    nloc = NA_KH * GRID_W
    _stage_heads(q_ref, k_ref, v_ref, qm_ref, kb_ref, vb_ref)

    kc = kb_ref[0:TM, :]
    vc = vb_ref[0:TM, :]
    outs = [_softmax_pv([_dot_nt(qm_ref[hh, 0:TM, :], kc)], [vc]) for hh in range(2)]
    m0c = lax.broadcasted_iota(jnp.int32, (TM, LANES), 1) < HEAD_DIM
    o_ref[0:TM, :] = jnp.where(m0c, outs[0], outs[1])

    m0 = lax.broadcasted_iota(jnp.int32, (GRID_W, LANES), 1) < HEAD_DIM

    def row_block(r, carry):
        rs = jnp.clip(r - NA_KH // 2, 0, rows - NA_KH)
        pat = r - rs
        q0 = pl.multiple_of(TM + r * GRID_W, GRID_W)
        k0 = pl.multiple_of(TM + rs * GRID_W, GRID_W)
        kl = kb_ref[pl.ds(k0, nloc), :]
        vl = vb_ref[pl.ds(k0, nloc), :]
        kc = kb_ref[0:TM, :]
        vc = vb_ref[0:TM, :]
        q = jnp.concatenate([qm_ref[0, pl.ds(q0, GRID_W), :], qm_ref[1, pl.ds(q0, GRID_W), :]], axis=0)
        s_loc = _dot_nt(q, kl) + bias_ref[pat].reshape(2 * GRID_W, nloc)
        s_ctx = _dot_nt(q, kc)
        res = _softmax_pv([s_loc, s_ctx], [vl, vc])
        o_ref[pl.ds(q0, GRID_W), :] = jnp.where(m0, res[:GRID_W], res[GRID_W:])
        return carry

    lax.fori_loop(0, rows, row_block, 0, unroll=4)


def _na_bias(rpb):
    h = rpb.shape[0]
    qc = np.arange(GRID_W)[:, None]
    kc = np.arange(GRID_W)[None, :]
    win = np.clip(qc - NA_KW // 2, 0, GRID_W - NA_KW)
    valid = (kc >= win) & (kc < win + NA_KW)
    col_off = np.clip(kc - qc + NA_KW - 1, 0, 2 * NA_KW - 2)
    onehot = (col_off[None] == np.arange(2 * NA_KW - 1)[:, None, None]).astype(np.float32)
    cols = jnp.einsum("hrc,cqk->hrqk", rpb.astype(F32), onehot, precision=lax.Precision.HIGHEST)
    cols = jnp.where(valid[None, None], cols * LOG2E, NEG_INF)
    bias = jnp.stack([cols[:, NA_KH - 1 - p:2 * NA_KH - 1 - p] for p in range(NA_KH)])
    bias = bias.transpose(0, 1, 3, 2, 4)
    return bias.reshape(NA_KH, h, GRID_W, NA_KH * GRID_W)


def _neighbourhood(p, bias):
    b, s, _ = p.shape
    nb = NA_W // LANES
    c0 = 4 * RET_W // LANES
    blk = lambda off: pl.BlockSpec((None, s, LANES), lambda bi, hp: (bi, 0, c0 + off + hp))
    return pl.pallas_call(
        _na_kernel,
        grid=(b, nb),
        in_specs=[
            blk(0), blk(nb), blk(2 * nb),
            pl.BlockSpec((NA_KH, 2, GRID_W, NA_KH * GRID_W), lambda bi, hp: (0, hp, 0, 0)),
        ],
        out_specs=pl.BlockSpec((None, s, LANES), lambda bi, hp: (bi, 0, hp)),
        out_shape=jax.ShapeDtypeStruct((b, s, NA_W), F32),
        scratch_shapes=[pltpu.VMEM((2, s, LANES), BF16), pltpu.VMEM((s, LANES), BF16),
                        pltpu.VMEM((s, 2 * LANES), BF16)],
        compiler_params=_cparams(("arbitrary", "arbitrary")),
        name="neighbourhood",
    )(p, p, p, bias)


def _swa_kernel(sink_ref, q_ref, k_ref, v_ref, o_ref, kd_ref, vd_ref):
    kp = pl.program_id(1)
    s_len = q_ref.shape[0]
    seq = s_len - TM
    nblk = seq // SWA_BLOCK
    band = SWA_BLOCK + 2 * SWA_WINDOW
    group = SWA_HEADS // SWA_KV_HEADS
    heads_per_step = 2 * group
    m0t = lax.broadcasted_iota(jnp.int32, (TM, LANES), 1) < HEAD_DIM

    def stage(i, carry):
        r0 = pl.multiple_of(i * TM, TM)
        for src, dst in ((k_ref, kd_ref), (v_ref, vd_ref)):
            x = src[pl.ds(r0, TM), :].astype(F32)
            xr = pltpu.roll(x, HEAD_DIM, 1)
            dst[0, pl.ds(r0, TM), 0:LANES] = jnp.where(m0t, x, xr).astype(BF16)
            dst[1, pl.ds(r0, TM), 0:LANES] = jnp.where(m0t, xr, x).astype(BF16)
        for h in range(2):
            vd_ref[h, pl.ds(r0, TM), LANES:2 * LANES] = jnp.ones((TM, LANES), BF16)
        return carry

    lax.fori_loop(0, s_len // TM, stage, 0)
    o_ref[0:TM, :] = jnp.zeros((TM, o_ref.shape[1]), F32)

    rows = group * SWA_BLOCK
    qi = lax.broadcasted_iota(jnp.int32, (rows, band), 0) % SWA_BLOCK
    ki = lax.broadcasted_iota(jnp.int32, (rows, band), 1)
    m0 = lax.broadcasted_iota(jnp.int32, (SWA_BLOCK, LANES), 1) < HEAD_DIM
    head_of_row = lax.broadcasted_iota(jnp.int32, (rows, 1), 0) // SWA_BLOCK

    def block(i, carry):
        start = jnp.clip((i - 1) * SWA_BLOCK, 0, seq - band)
        k0 = pl.multiple_of(TM + start, SWA_BLOCK)
        q0 = pl.multiple_of(TM + i * SWA_BLOCK, SWA_BLOCK)
        valid = jnp.abs(qi + (i * SWA_BLOCK - start) - ki) <= SWA_WINDOW
        for hh in range(2):
            kb = kd_ref[hh, pl.ds(k0, band), :]
            vb = vd_ref[hh, pl.ds(k0, band), :]
            kc = kd_ref[hh, 0:TM, :]
            vc = vd_ref[hh, 0:TM, :]
            parts = []
            for j in range(group // 2):
                pair = hh * (group // 2) + j
                qp = q_ref[pl.ds(q0, SWA_BLOCK), pair * LANES:(pair + 1) * LANES]
                parts += [jnp.where(m0, qp, 0.0), jnp.where(m0, 0.0, qp)]
            q = jnp.concatenate(parts, axis=0).astype(BF16)
            s = jnp.where(valid, _dot_nt(q, kb), NEG_INF)
            s_ctx = _dot_nt(q, kc)
            h0 = kp * heads_per_step + hh * group
            sink = jnp.full((rows, 1), sink_ref[h0], F32)
            for g in range(1, group):
                sink = jnp.where(head_of_row == g, sink_ref[h0 + g], sink)
            res = _softmax_pv([s, s_ctx], [vb, vc], extra=sink * LOG2E)
            for j in range(group // 2):
                pair = hh * (group // 2) + j
                r0 = 2 * j * SWA_BLOCK
                o_ref[pl.ds(q0, SWA_BLOCK), pair * LANES:(pair + 1) * LANES] = jnp.where(
                    m0, res[r0:r0 + SWA_BLOCK], res[r0 + SWA_BLOCK:r0 + 2 * SWA_BLOCK])
        return carry

    lax.fori_loop(0, nblk, block, 0, unroll=2)


def _swa(p, sink):
    b, s, _ = p.shape
    qw = SWA_HEADS * HEAD_DIM // 2
    kblk = SWA_HEADS * HEAD_DIM // LANES
    vblk = kblk + SWA_KV_HEADS * HEAD_DIM // LANES
    return pl.pallas_call(
        _swa_kernel,
        grid=(b, 2),
        in_specs=[
            pl.BlockSpec(memory_space=pltpu.SMEM),
            pl.BlockSpec((None, s, qw), lambda bi, kp: (bi, 0, kp)),
            pl.BlockSpec((None, s, LANES), lambda bi, kp: (bi, 0, kblk + kp)),
            pl.BlockSpec((None, s, LANES), lambda bi, kp: (bi, 0, vblk + kp)),
        ],
        out_specs=pl.BlockSpec((None, s, qw), lambda bi, kp: (bi, 0, kp)),
        out_shape=jax.ShapeDtypeStruct((b, s, SWA_HEADS * HEAD_DIM), F32),
        scratch_shapes=[pltpu.VMEM((2, s, LANES), BF16), pltpu.VMEM((2, s, 2 * LANES), BF16)],
        compiler_params=_cparams(("arbitrary", "arbitrary")),
        name="swa",
    )(sink, p, p, p)


def _outproj_kernel(*refs, n_mix, n_stream, t0):
    mix_refs = refs[:n_mix]
    x_refs = refs[n_mix + 1:n_mix + 1 + n_stream]
    w_ref = refs[n_mix]
    (g1_ref, gn_ref, sh_ref, sc_ref, wr_ref, br_ref,
     xo_ref, h_ref, ri_ref, cnt_ref, carry_ref) = refs[n_mix + 1 + n_stream:]
    first = (pl.program_id(0) == 0) & (pl.program_id(1) == 0)

    @pl.when(first)
    def _():
        carry_ref[...] = jnp.zeros_like(carry_ref)

    o = None
    off = 0
    for m_ref in mix_refs:
        kw = m_ref.shape[1]
        part = jnp.dot(m_ref[...].astype(BF16), w_ref[off:off + kw, :], preferred_element_type=F32)
        o = part if o is None else o + part
        off += kw
    xn = _stream_tile(x_refs, pl.program_id(1) + t0 > 0) + g1_ref[...] * o
    xo_ref[...] = xn
    h = _rms_mod(xn, gn_ref[...], sh_ref[...], sc_ref[...])
    h_ref[...] = h

    logits = _dot3_split(h, wr_ref[0], wr_ref[1]) + br_ref[...]
    tm = logits.shape[0]
    lane = lax.broadcasted_iota(jnp.int32, (tm, LANES), 1).astype(F32)
    big = 1e9
    gmask = lane < MOE_GROUPS
    mg = jnp.max(jnp.where(gmask, logits, -big), axis=-1, keepdims=True)
    sg = jnp.sum(jnp.where(gmask, jnp.exp(jnp.minimum(logits - mg, 0.0)), 0.0), axis=-1, keepdims=True)
    gw = 1.0 / sg
    gi = jnp.min(jnp.where(gmask & (logits == mg), lane, big), axis=-1, keepdims=True)
    lo = ROUTE_LANE0 + MOE_EPG * gi
    emask = (lane >= lo) & (lane < lo + MOE_EPG)
    l1 = jnp.max(jnp.where(emask, logits, -big), axis=-1, keepdims=True)
    i1 = jnp.min(jnp.where(emask & (logits == l1), lane, big), axis=-1, keepdims=True)
    emask2 = emask & (lane != i1)
    l2 = jnp.max(jnp.where(emask2, logits, -big), axis=-1, keepdims=True)
    i2 = jnp.min(jnp.where(emask2 & (logits == l2), lane, big), axis=-1, keepdims=True)
    e21 = jnp.exp(l2 - l1)
    w1 = gw / (1.0 + e21)
    w2 = gw * e21 / (1.0 + e21)

    oh = jnp.where((lane == i1) | (lane == i2), 1.0, 0.0)
    tri = (lax.broadcasted_iota(jnp.int32, (tm, tm), 0) > lax.broadcasted_iota(jnp.int32, (tm, tm), 1))
    cum = jnp.dot(jnp.where(tri, 1.0, 0.0).astype(BF16), oh.astype(BF16), preferred_element_type=F32) + carry_ref[...]
    r1 = jnp.sum(jnp.where(lane == i1, cum, 0.0), axis=-1, keepdims=True)
    r2 = jnp.sum(jnp.where(lane == i2, cum, 0.0), axis=-1, keepdims=True)
    carry_ref[...] = carry_ref[...] + jnp.sum(oh, axis=0, keepdims=True)
    cnt_ref[...] = carry_ref[...]

    ri = jnp.where(lane == 0, i1 - ROUTE_LANE0, 0.0)
    ri = jnp.where(lane == 1, i2 - ROUTE_LANE0, ri)
    ri = jnp.where(lane == 2, w1, ri)
    ri = jnp.where(lane == 3, w2, ri)
    ri = jnp.where(lane == 4, r1, ri)
    ri = jnp.where(lane == 5, r2, ri)
    ri_ref[...] = ri


def _outproj(mixes, w_bf16, xa, g1, gain2, sh2, sc2, wr, br, t0):
    b, s, d = _stream_shape(xa)
    nt = s // TM - t0
    so = nt * TM
    mod_idx = lambda bi, t: (jnp.where(t + t0 == 0, b, bi), 0, 0)
    tile = lambda wdt: pl.BlockSpec((None, TM, wdt), lambda bi, t: (bi, t + t0, 0))
    otile = lambda wdt: pl.BlockSpec((None, TM, wdt), lambda bi, t: (bi, t, 0))
    const = lambda shape: pl.BlockSpec(shape, lambda bi, t: (0,) * len(shape))
    x_specs, x_args = _stream_specs(xa, t0)
    in_specs = [tile(m.shape[2]) for m in mixes] + [const(w_bf16.shape)] + x_specs + [
        pl.BlockSpec((None, 1, d), mod_idx), const((1, d)),
        pl.BlockSpec((None, 1, d), mod_idx), pl.BlockSpec((None, 1, d), mod_idx),
        const((2, d, LANES)), const((1, LANES)),
    ]
    return pl.pallas_call(
        functools.partial(_outproj_kernel, n_mix=len(mixes), n_stream=len(x_args), t0=t0),
        grid=(b, nt),
        in_specs=in_specs,
        out_specs=[otile(d), otile(d), otile(LANES), const((1, LANES))],
        out_shape=[jax.ShapeDtypeStruct((b, so, d), F32), jax.ShapeDtypeStruct((b, so, d), F32),
                   jax.ShapeDtypeStruct((b, so, LANES), F32), jax.ShapeDtypeStruct((1, LANES), F32)],
        scratch_shapes=[pltpu.VMEM((1, LANES), F32)],
        compiler_params=_cparams(("arbitrary", "arbitrary")),
        name="outproj",
    )(*mixes, w_bf16, *x_args, g1, gain2, sh2, sc2, wr, br)


SUBLANES = 8


def _to_token_tiles(x):
    return x.reshape(x.shape[0], SUBLANES, x.shape[1] // SUBLANES)


def _from_token_tiles(x3):
    return x3.reshape(x3.shape[0], x3.shape[1] * x3.shape[2])


PAD_BITS = tuple(1 << k for k in reversed(range(MOE_ROWS.bit_length() - 1)))


def _dispatch_kernel(dest_ref, pads_ref, h_ref, xs_ref, hbuf, zbuf, sem, zsem):
    nt = pl.num_programs(1)
    step = pl.program_id(0) * nt + pl.program_id(1)
    last = pl.num_programs(0) * nt - 1
    slot = step % 2

    def zero_pads(wait):
        def expert(e, c):
            pos = pads_ref[0, e]
            n = pads_ref[1, e]
            for bit in PAD_BITS:
                take = (n & bit) != 0
                cp = pltpu.make_async_copy(zbuf.at[pl.ds(0, bit)], xs_ref.at[pl.ds(pos, bit)], zsem)

                @pl.when(take)
                def _():
                    cp.wait() if wait else cp.start()

                pos = pos + jnp.where(take, bit, 0)
            return c
        lax.fori_loop(0, MOE_EXPERTS, expert, 0)

        def tail(j, c):
            cp = pltpu.make_async_copy(
                zbuf, xs_ref.at[pl.ds(pads_ref[0, MOE_EXPERTS] + j * PAD_BITS[0], PAD_BITS[0])], zsem)
            cp.wait() if wait else cp.start()
            return c
        lax.fori_loop(0, pads_ref[1, MOE_EXPERTS], tail, 0)

    @pl.when(step == 0)
    def _():
        zbuf[...] = jnp.zeros_like(zbuf)
        zero_pads(wait=False)
    hbuf[slot] = _to_token_tiles(h_ref[...])

    def copy(sl, i, dst_row):
        return pltpu.make_async_copy(hbuf.at[sl, i], xs_ref.at[dst_row], sem.at[sl])

    for i in range(TM):
        copy(slot, i, dest_ref[step, i]).start()
        copy(slot, i, dest_ref[step, TM + i]).start()

    def drain(sl):
        def one(i, c):
            copy(sl, 0, 0).wait()
            return c
        lax.fori_loop(0, 2 * TM, one, 0, unroll=8)

    @pl.when(step > 0)
    def _():
        drain(1 - slot)

    @pl.when(step == last)
    def _():
        drain(slot)
        zero_pads(wait=True)


def _dispatch(dest, pads, h2, n_pad):
    b, s, d = h2.shape
    nt = s // TM
    tile = (SUBLANES, d // SUBLANES)
    return pl.pallas_call(
        _dispatch_kernel,
        grid_spec=pltpu.PrefetchScalarGridSpec(
            num_scalar_prefetch=2,
            grid=(b, nt),
            in_specs=[pl.BlockSpec((None, TM, d), lambda bi, t, dr, pd: (bi, t, 0))],
            out_specs=pl.BlockSpec(memory_space=pl.ANY),
            scratch_shapes=[pltpu.VMEM((2, TM) + tile, F32), pltpu.VMEM((PAD_BITS[0],) + tile, F32),
                            pltpu.SemaphoreType.DMA((2,)), pltpu.SemaphoreType.DMA],
        ),
        out_shape=jax.ShapeDtypeStruct((n_pad,) + tile, F32),
        compiler_params=_cparams(("arbitrary", "arbitrary")),
        name="dispatch",
    )(dest, pads, h2)


def _mlp_kernel(be_ref, nu_ref, x_ref, wgu_ref, wd_ref, y_ref, wgu_b, wd_b):
    i = pl.program_id(0)
    prev = be_ref[jnp.maximum(i - 1, 0)]
    used = i < nu_ref[0]

    @pl.when(used & ((i == 0) | (be_ref[i] != prev)))
    def _():
        wgu_b[...] = wgu_ref[...].astype(BF16)
        wd_b[...] = wd_ref[...].astype(BF16)

    @pl.when(used)
    def _():
        x = _from_token_tiles(x_ref[...])
        gu = jnp.dot(x.astype(BF16), wgu_b[...], preferred_element_type=F32)
        act = _silu(gu[:, :MOE_FF]) * gu[:, MOE_FF:]
        y_ref[...] = _to_token_tiles(jnp.dot(act.astype(BF16), wd_b[...], preferred_element_type=F32))

    @pl.when(jnp.logical_not(used))
    def _():
        y_ref[...] = jnp.zeros_like(y_ref)


def _expert_mlp(block_e, n_used, xs, w_gu, w_down, layer):
    n_pad, sub, dl = xs.shape
    d = sub * dl
    ff2 = w_gu.shape[-1]
    slots = pl.BlockSpec((MOE_ROWS, sub, dl), lambda i, be, nu: (i, 0, 0))
    used_slots = pl.BlockSpec((MOE_ROWS, sub, dl), lambda i, be, nu: (jnp.minimum(i, nu[0] - 1), 0, 0))
    return pl.pallas_call(
        _mlp_kernel,
        grid_spec=pltpu.PrefetchScalarGridSpec(
            num_scalar_prefetch=2,
            grid=(n_pad // MOE_ROWS,),
            in_specs=[
                used_slots,
                pl.BlockSpec((None, None, d, ff2), lambda i, be, nu: (layer, be[i], 0, 0)),
                pl.BlockSpec((None, None, ff2 // 2, d), lambda i, be, nu: (layer, be[i], 0, 0)),
            ],
            out_specs=slots,
            scratch_shapes=[pltpu.VMEM((d, ff2), BF16), pltpu.VMEM((ff2 // 2, d), BF16)],
        ),
        out_shape=jax.ShapeDtypeStruct(xs.shape, F32),
        compiler_params=_cparams(("arbitrary",)),
        name="expert_mlp",
    )(block_e, n_used, xs, w_gu, w_down)


def _combine_kernel(dest_ref, ys_ref, x_ref, ri_ref, g2_ref, fg_ref, o_ref, buf, sem, *, final):
    bi = pl.program_id(0)
    t = pl.program_id(1)
    nt = pl.num_programs(1)
    step = bi * nt + t
    total = pl.num_programs(0) * nt

    def copy(src_row, slot, k, i):
        return pltpu.make_async_copy(ys_ref.at[src_row], buf.at[slot, k, i], sem.at[slot])

    def issue(st, slot):
        for i in range(TM):
            copy(dest_ref[st, i], slot, 0, i).start()
            copy(dest_ref[st, TM + i], slot, 1, i).start()

    slot = step % 2

    @pl.when(step == 0)
    def _():
        issue(0, 0)

    @pl.when(step + 1 < total)
    def _():
        issue(step + 1, 1 - slot)

    def drain(i, c):
        copy(0, slot, 0, 0).wait()
        return c

    lax.fori_loop(0, 2 * TM, drain, 0, unroll=8)

    lane = lax.broadcasted_iota(jnp.int32, (TM, LANES), 1)
    ri = ri_ref[...]
    w1 = jnp.sum(jnp.where(lane == 2, ri, 0.0), axis=-1, keepdims=True)
    w2 = jnp.sum(jnp.where(lane == 3, ri, 0.0), axis=-1, keepdims=True)
    y = _from_token_tiles(buf[slot, 0]) * w1 + _from_token_tiles(buf[slot, 1]) * w2
    xn = x_ref[...] + g2_ref[...] * y
    if final:
        ms = jnp.mean(xn * xn, axis=-1, keepdims=True)
        xn = xn * lax.rsqrt(ms + NORM_EPS) * fg_ref[...]
    o_ref[...] = xn


def _combine(dest, ys, xa, rinfo, g2, final_g, has_ctx, final):
    b, s, d = xa.shape
    nt = s // TM
    mod_idx = lambda bi, t, dr: (jnp.where(t == 0, b, bi) if has_ctx else bi, 0, 0)
    tile = lambda wdt: pl.BlockSpec((None, TM, wdt), lambda bi, t, dr: (bi, t, 0))
    out_spec = tile(d)
    out_shape = jax.ShapeDtypeStruct((b, s, d), F32)
    return pl.pallas_call(
        functools.partial(_combine_kernel, final=final),
        grid_spec=pltpu.PrefetchScalarGridSpec(
            num_scalar_prefetch=1,
            grid=(b, nt),
            in_specs=[
                pl.BlockSpec(memory_space=pl.ANY),
                tile(d), tile(LANES),
                pl.BlockSpec((None, 1, d), mod_idx),
                pl.BlockSpec((1, d), lambda bi, t, dr: (0, 0)),
            ],
            out_specs=out_spec,
            scratch_shapes=[pltpu.VMEM((2, 2, TM, SUBLANES, d // SUBLANES), F32),
                            pltpu.SemaphoreType.DMA((2,))],
        ),
        out_shape=out_shape,
        compiler_params=_cparams(("arbitrary", "arbitrary")),
        name="combine",
    )(dest, ys, xa, rinfo, g2, final_g)


def _moe(h2, rinfo, counts, w_gu, w_down, layer):
    b, s, d = h2.shape
    nt = s // TM
    r = rinfo[:, :, :8].reshape(b * nt, TM, 8)
    lanes = lambda i: jnp.concatenate([r[..., i], r[..., i + 1]], axis=1).astype(jnp.int32)
    e = lanes(0)
    rank = lanes(4)
    cnt = counts[0, ROUTE_LANE0:ROUTE_LANE0 + MOE_EXPERTS].astype(jnp.int32)
    padded = (cnt + MOE_ROWS - 1) // MOE_ROWS * MOE_ROWS
    ends = jnp.cumsum(padded)
    starts = ends - padded
    eids = jnp.arange(MOE_EXPERTS, dtype=jnp.int32)
    dest = jnp.sum(jnp.where(e[..., None] == eids, starts, 0), axis=-1) + rank
    n_assign = b * nt * TM * 2
    n_blocks = (n_assign + MOE_EXPERTS * (MOE_ROWS - 1)) // MOE_ROWS + 1
    n_pad = n_blocks * MOE_ROWS
    blk_row = jnp.arange(n_blocks, dtype=jnp.int32) * MOE_ROWS
    block_e = jnp.minimum(jnp.sum((blk_row[:, None] >= ends[None, :]).astype(jnp.int32), axis=1),
                          MOE_EXPERTS - 1)
    n_used = (ends[-1] // MOE_ROWS).astype(jnp.int32).reshape(1)
    pads = jnp.stack([jnp.append(starts + cnt, ends[-1]),
                      jnp.append(padded - cnt, (n_pad - ends[-1]) // PAD_BITS[0])]).astype(jnp.int32)
    xs = _dispatch(dest, pads, h2, n_pad)
    ys = _expert_mlp(block_e, n_used, xs, w_gu, w_down, layer)
    return dest, ys


def _router_weights(wg, bg, we, be):
    d = wg.shape[0]
    pad = LANES - MOE_GROUPS - MOE_EXPERTS
    assert ROUTE_LANE0 == MOE_GROUPS
    wr = jnp.concatenate([wg.astype(F32), we.astype(F32), jnp.zeros((d, pad), F32)], axis=1)
    br = jnp.concatenate([bg.astype(F32), be.astype(F32), jnp.zeros((pad,), F32)]).reshape(1, LANES)
    return jnp.stack(_split_bf16(wr)), br


def kernel(x, c, ctx, c_ctx, ada_w, ada_b, norm_g, final_g, ab_w_in, ab_w_out, ret_decay, ret_gn, na_rpb,
           swa_w_in, swa_w_out, swa_sink, router_g_w, router_g_b, router_e_w, router_e_b,
           expert_w_gu, expert_w_down):
    b, seq, d = x.shape
    assert ctx.shape[1] == TM and seq % TM == 0 and d == D_MODEL
    xa = (ctx, x)
    rope = _rope_tables(seq)

    cvec = jnp.concatenate([c, c_ctx[None, :], jnp.zeros((7, d), F32)], axis=0)
    mod = _adaln(cvec, ada_w, ada_b)
    mod = mod.reshape(DEPTH, b + 8, 6, 1, d)[:, :b + 1].transpose(0, 2, 1, 3, 4)

    for layer in range(DEPTH):
        last = layer == DEPTH - 1
        sh1, sc1, g1, sh2, sc2, g2 = (mod[layer, i] for i in range(6))
        gain1 = norm_g[layer, 0].reshape(1, d)
        gain2 = norm_g[layer, 1].reshape(1, d)
        j = layer // 2
        if layer % 2 == 0:
            nb = RET_W // LANES
            qk_scale = HEAD_DIM ** -0.5
            scales = {blk: qk_scale for blk in range(nb, 2 * nb)}
            scales.update({blk: qk_scale * LOG2E for blk in range(4 * nb, 5 * nb)})
            p = _proj(xa, gain1, sh1, sc1, ab_w_in[j].astype(BF16), rope,
                      rope_blocks=range(0, 2 * nb), scales=scales)
            log_gamma = jnp.log1p(-jnp.exp2(-ret_decay[j].astype(F32)))
            mixes = [_retention(p, log_gamma, ret_gn[j]), _neighbourhood(p, _na_bias(na_rpb[j]))]
            w_out = ab_w_out[j]
        else:
            qb = SWA_HEADS * HEAD_DIM // LANES
            kb = SWA_KV_HEADS * HEAD_DIM // LANES
            p = _proj(xa, gain1, sh1, sc1, swa_w_in[j].astype(BF16), rope,
                      rope_blocks=range(0, qb + kb),
                      scales={blk: HEAD_DIM ** -0.5 * LOG2E for blk in range(qb)})
            mixes = [_swa(p, swa_sink[j].astype(F32))]
            w_out = swa_w_out[j]
        t0 = 1 if last else 0
        wr, br = _router_weights(router_g_w[layer], router_g_b[layer], router_e_w[layer], router_e_b[layer])
        xa, h2, rinfo, counts = _outproj(mixes, w_out.astype(BF16), xa, g1, gain2, sh2, sc2, wr, br, t0)
        dest, ys = _moe(h2, rinfo, counts, expert_w_gu, expert_w_down, layer)
        xa = _combine(dest, ys, xa, rinfo, g2, final_g.reshape(1, d), not last, last)
    return xa
```

```python
import functools

import numpy as np
import jax
import jax.numpy as jnp
from jax import lax
from jax.experimental import pallas as pl
from jax.experimental.pallas import tpu as pltpu

F32 = jnp.float32
BF16 = jnp.bfloat16

D_MODEL = 1024
DEPTH = 2
GRID_W = 64
HEAD_DIM = 64
RET_HEADS = 8
NA_HEADS = 8
RET_W = 512
NA_W = 512
AB_IN = 4 * RET_W + 3 * NA_W
RET_CHUNK = 128
GN_EPS = 1e-5
NA_KH = 8
NA_KW = 16
SWA_HEADS = 16
SWA_KV_HEADS = 4
SWA_WINDOW = 128
SWA_BLOCK = 128
SWA_IN = (SWA_HEADS + 2 * SWA_KV_HEADS) * HEAD_DIM
ROPE_BASE = 10000.0
MOE_GROUPS = 4
MOE_EPG = 8
MOE_EXPERTS = 32
MOE_FF = 512
NORM_EPS = 1e-6
NEG_INF = -1e30

LANES = 128
TM = 256
RET_BLOCK = 256
MOE_ROWS = 512
ROUTE_LANE0 = 4
VMEM_LIMIT = 56 * 1024 * 1024


def _cparams(sem, vmem=VMEM_LIMIT):
    return pltpu.CompilerParams(dimension_semantics=sem, vmem_limit_bytes=vmem)


def _split_bf16(a):
    hi = a.astype(BF16)
    lo = (a - hi.astype(F32)).astype(BF16)
    return hi, lo


def _dot3_split(a, bh, bl):
    ah, al = _split_bf16(a)
    d = lambda x, y: jnp.dot(x, y, preferred_element_type=F32)
    return d(ah, bh) + (d(ah, bl) + d(al, bh))


def _dot3(a, b):
    return _dot3_split(a, *_split_bf16(b))


def _dot_nt(a, b):
    return lax.dot_general(a, b, (((1,), (1,)), ((), ())), preferred_element_type=F32)


def _dot_tn(a, b):
    return lax.dot_general(a, b, (((0,), (0,)), ((), ())), preferred_element_type=F32)


def _silu(x):
    return x / (1.0 + jnp.exp(-x))


def _adaln_kernel(c_ref, w_ref, b_ref, o_ref):
    o_ref[...] = _dot3(_silu(c_ref[...]), w_ref[...]) + b_ref[...]


def _adaln(cvec, ada_w, ada_b):
    depth, d, n6 = ada_w.shape
    rows = cvec.shape[0]
    tn = 1024
    return pl.pallas_call(
        _adaln_kernel,
        grid=(depth, n6 // tn),
        in_specs=[
            pl.BlockSpec((rows, d), lambda l, j: (0, 0)),
            pl.BlockSpec((None, d, tn), lambda l, j: (l, 0, j)),
            pl.BlockSpec((None, 1, tn), lambda l, j: (l, 0, j)),
        ],
        out_specs=pl.BlockSpec((None, rows, tn), lambda l, j: (l, 0, j)),
        out_shape=jax.ShapeDtypeStruct((depth, rows, n6), F32),
        compiler_params=_cparams(("arbitrary", "arbitrary")),
        name="adaln",
    )(cvec, ada_w, ada_b.reshape(depth, 1, n6))


def _rms_mod(x, g, sh, sc):
    ms = jnp.mean(x * x, axis=-1, keepdims=True)
    return (x * lax.rsqrt(ms + NORM_EPS) * g) * (1.0 + sc) + sh


def _stream_tile(refs, is_lat):
    if len(refs) == 1:
        return refs[0][...]
    return jnp.where(is_lat, refs[1][...], refs[0][...])


def _stream_specs(stream, t0, extra_args=0):
    def im(f):
        return (lambda bi, t, *_: f(bi, t + t0))
    if not isinstance(stream, tuple):
        return [pl.BlockSpec((None, TM, stream.shape[2]), im(lambda bi, t: (bi, t, 0)))], [stream]
    ctx, x = stream
    d = x.shape[2]
    return ([pl.BlockSpec((None, TM, d), im(lambda bi, t: (bi, 0, 0))),
             pl.BlockSpec((None, TM, d), im(lambda bi, t: (bi, jnp.maximum(t - 1, 0), 0)))], [ctx, x])


def _stream_shape(stream):
    if not isinstance(stream, tuple):
        return stream.shape
    ctx, x = stream
    return (x.shape[0], ctx.shape[1] + x.shape[1], x.shape[2])


def _proj_kernel(*refs, rope_blocks, scales, cn, n_stream):
    x_refs = refs[:n_stream]
    g_ref, sh_ref, sc_ref, w_ref, rope_ref, o_ref = refs[n_stream:]
    is_lat = pl.program_id(1) > 0
    hb = _rms_mod(_stream_tile(x_refs, is_lat), g_ref[...], sh_ref[...], sc_ref[...]).astype(BF16)
    nout = w_ref.shape[1]
    for c in range(nout // cn):
        o = jnp.dot(hb, w_ref[:, c * cn:(c + 1) * cn], preferred_element_type=F32)
        for s in range(cn // LANES):
            blk = c * (cn // LANES) + s
            ob = o[:, s * LANES:(s + 1) * LANES]
            if blk in rope_blocks:
                r = (ob * rope_ref[0] + pltpu.roll(ob, 16, 1) * rope_ref[1]
                     + pltpu.roll(ob, LANES - 16, 1) * rope_ref[2])
                ob = jnp.where(is_lat, r, ob)
            if blk in scales:
                ob = ob * scales[blk]
            o_ref[:, blk * LANES:(blk + 1) * LANES] = ob.astype(o_ref.dtype)


def _proj(xa, gain, sh, sc, w_bf16, rope, rope_blocks, scales):
    b, s, d = _stream_shape(xa)
    nout = w_bf16.shape[1]
    nt = s // TM
    mod_idx = lambda bi, t: (jnp.where(t == 0, b, bi), 0, 0)
    x_specs, x_args = _stream_specs(xa, 0)
    kern = functools.partial(_proj_kernel, rope_blocks=frozenset(rope_blocks), scales=dict(scales), cn=512,
                             n_stream=len(x_args))
    return pl.pallas_call(
        kern,
        grid=(b, nt),
        in_specs=x_specs + [
            pl.BlockSpec((1, d), lambda bi, t: (0, 0)),
            pl.BlockSpec((None, 1, d), mod_idx),
            pl.BlockSpec((None, 1, d), mod_idx),
            pl.BlockSpec((d, nout), lambda bi, t: (0, 0)),
            pl.BlockSpec((3, TM, LANES), lambda bi, t: (0, jnp.maximum(t - 1, 0), 0)),
        ],
        out_specs=pl.BlockSpec((None, TM, nout), lambda bi, t: (bi, t, 0)),
        out_shape=jax.ShapeDtypeStruct((b, s, nout), BF16),
        compiler_params=_cparams(("arbitrary", "arbitrary")),
        name="proj",
    )(*x_args, gain, sh, sc, w_bf16, rope)


def _rope_tables(seq):
    nf = HEAD_DIM // 4
    inv = ROPE_BASE ** (-jnp.arange(nf, dtype=F32) / nf)
    t = jnp.arange(seq)
    row = (t // GRID_W).astype(F32)
    col = (t % GRID_W).astype(F32)
    lane = np.arange(LANES)
    jj = lane % HEAD_DIM
    axis_is_col = (jj // 32) == 1
    second_half = (jj % 32) >= 16
    f = jj % 16
    pos = jnp.where(axis_is_col[None, :], col[:, None], row[:, None])
    ang = pos * inv[f][None, :]
    c, s = jnp.cos(ang), jnp.sin(ang)
    sa = jnp.where(second_half[None, :], s, 0.0)
    sb = jnp.where(second_half[None, :], 0.0, -s)
    return jnp.stack([c, sa, sb], axis=0)


def _ret_kernel(lg_ref, q_ref, k_ref, v_ref, g_ref, gn_ref, o_ref, accf_ref, accb_ref,
                intra_ref, qd_ref, kd_ref):
    hp = pl.program_id(1)
    c = RET_BLOCK
    s_len = q_ref.shape[0]
    n_chunks = s_len // c
    ctx_chunks = TM // c
    pos = lax.broadcasted_iota(jnp.int32, (c, LANES), 0).astype(F32)
    m0 = lax.broadcasted_iota(jnp.int32, (c, LANES), 1) < HEAD_DIM
    same_head = ((lax.broadcasted_iota(jnp.int32, (LANES, LANES), 0) < HEAD_DIM)
                 == (lax.broadcasted_iota(jnp.int32, (LANES, LANES), 1) < HEAD_DIM))
    rel = (lax.broadcasted_iota(jnp.int32, (c, c), 0) - lax.broadcasted_iota(jnp.int32, (c, c), 1)).astype(F32)
    lgf = [lg_ref[0, hp * 2 + hh] for hh in range(2)]
    lgb = [lg_ref[1, hp * 2 + hh] for hh in range(2)]
    lgf_l = jnp.where(m0, lgf[0], lgf[1])
    lgb_l = jnp.where(m0, lgb[0], lgb[1])
    for hh in range(2):
        intra_ref[0, :, hh * c:(hh + 1) * c] = jnp.where(rel >= 0, jnp.exp(lgf[hh] * jnp.maximum(rel, 0.0)), 0.0)
        intra_ref[1, :, hh * c:(hh + 1) * c] = jnp.where(rel <= 0, jnp.exp(lgb[hh] * jnp.maximum(-rel, 0.0)), 0.0)
    qd_ref[0] = jnp.exp(lgf_l * (pos + 1.0))
    qd_ref[1] = jnp.exp(lgb_l * (c - pos))
    kd_ref[0] = jnp.exp(lgf_l * (c - 1.0 - pos))
    kd_ref[1] = jnp.exp(lgb_l * pos)
    cd = [jnp.exp(lgf_l[0:1] * float(c)), jnp.exp(lgb_l[0:1] * float(c))]

    def chunk(r0, state, d):
        q = q_ref[pl.ds(r0, c), :]
        k = k_ref[pl.ds(r0, c), :]
        v = v_ref[pl.ds(r0, c), :]
        qb = q.astype(BF16)
        kcat = jnp.concatenate([jnp.where(m0, k, 0.0), jnp.where(m0, 0.0, k)], axis=0).astype(BF16)
        vcat = jnp.concatenate([jnp.where(m0, v, 0.0), jnp.where(m0, 0.0, v)], axis=0).astype(BF16)
        s = _dot_nt(qb, kcat) * intra_ref[d]
        out = jnp.dot(s.astype(BF16), vcat, preferred_element_type=F32)
        out = out + jnp.dot(qb, state.astype(BF16), preferred_element_type=F32) * qd_ref[d]
        kv = _dot_tn((k * kd_ref[d]).astype(BF16), v.astype(BF16))
        return out, state * cd[d] + jnp.where(same_head, kv, 0.0)

    def body(i, states):
        sf, sb = states
        rf = pl.multiple_of(i * c, c)
        ib = jnp.where(i < ctx_chunks, ctx_chunks - 1 - i, n_chunks + ctx_chunks - 1 - i)
        rb = pl.multiple_of(ib * c, c)
        of, sf = chunk(rf, sf, 0)
        ob, sb = chunk(rb, sb, 1)
        accf_ref[pl.ds(rf, c), :] = of
        accb_ref[pl.ds(rb, c), :] = ob
        return sf, sb

    z = jnp.zeros((LANES, LANES), F32)
    lax.fori_loop(0, n_chunks, body, (z, z), unroll=3)

    avg = jnp.where(same_head, 1.0 / HEAD_DIM, 0.0).astype(BF16)

    def head_mean(x):
        hi, lo = _split_bf16(x)
        return (jnp.dot(hi, avg, preferred_element_type=F32) + jnp.dot(lo, avg, preferred_element_type=F32))

    def readout(i, carry):
        r0 = pl.multiple_of(i * c, c)
        o = accf_ref[pl.ds(r0, c), :] + accb_ref[pl.ds(r0, c), :]
        dlt = o - head_mean(o)
        var = head_mean(dlt * dlt)
        y = dlt * lax.rsqrt(var + GN_EPS) * gn_ref[...]
        o_ref[pl.ds(r0, c), :] = _silu(g_ref[pl.ds(r0, c), :].astype(F32)) * y
        return carry

    lax.fori_loop(0, n_chunks, readout, 0, unroll=2)


def _retention(p, log_gamma, ret_gn):
    b, s, _ = p.shape
    nb = RET_W // LANES
    blk = lambda off: pl.BlockSpec((None, s, LANES), lambda bi, hp: (bi, 0, off + hp))
    return pl.pallas_call(
        _ret_kernel,
        grid=(b, nb),
        in_specs=[
            pl.BlockSpec(memory_space=pltpu.SMEM),
            blk(0), blk(nb), blk(2 * nb), blk(3 * nb),
            pl.BlockSpec((1, LANES), lambda bi, hp: (0, hp)),
        ],
        out_specs=pl.BlockSpec((None, s, LANES), lambda bi, hp: (bi, 0, hp)),
        out_shape=jax.ShapeDtypeStruct((b, s, RET_W), F32),
        scratch_shapes=[pltpu.VMEM((s, LANES), F32), pltpu.VMEM((s, LANES), F32),
                        pltpu.VMEM((2, RET_BLOCK, 2 * RET_BLOCK), F32),
                        pltpu.VMEM((2, RET_BLOCK, LANES), F32), pltpu.VMEM((2, RET_BLOCK, LANES), F32)],
        compiler_params=_cparams(("arbitrary", "arbitrary")),
        name="retention",
    )(log_gamma, p, p, p, p, ret_gn.reshape(1, RET_W))


LOG2E = 1.4426950408889634


def _softmax_pv(s_list, v_list, extra=None):
    m = None
    for s in s_list:
        for j in range(s.shape[1] // LANES):
            blk = s[:, j * LANES:(j + 1) * LANES]
            m = blk if m is None else jnp.maximum(m, blk)
    m = m.max(axis=-1, keepdims=True)
    if extra is not None:
        m = jnp.maximum(m, extra)
    acc = None
    for s, v in zip(s_list, v_list):
        pv = jnp.dot(jnp.exp2(s - m).astype(BF16), v, preferred_element_type=F32)
        acc = pv if acc is None else acc + pv
    o, den = acc[:, :LANES], acc[:, LANES:]
    if extra is not None:
        den = den + jnp.exp2(extra - m)
    return o / den


def _stage_heads(q_ref, k_ref, v_ref, qm_ref, kb_ref, vb_ref):
    m0 = lax.broadcasted_iota(jnp.int32, (TM, LANES), 1) < HEAD_DIM

    def stage(i, carry):
        r0 = pl.multiple_of(i * TM, TM)
        q = q_ref[pl.ds(r0, TM), :]
        qm_ref[0, pl.ds(r0, TM), :] = jnp.where(m0, q, 0.0).astype(BF16)
        qm_ref[1, pl.ds(r0, TM), :] = jnp.where(m0, 0.0, q).astype(BF16)
        kb_ref[pl.ds(r0, TM), :] = k_ref[pl.ds(r0, TM), :].astype(BF16)
        vb_ref[pl.ds(r0, TM), 0:LANES] = v_ref[pl.ds(r0, TM), :].astype(BF16)
        vb_ref[pl.ds(r0, TM), LANES:2 * LANES] = jnp.ones((TM, LANES), BF16)
        return carry

    lax.fori_loop(0, q_ref.shape[0] // TM, stage, 0)


def _na_kernel(q_ref, k_ref, v_ref, bias_ref, o_ref, qm_ref, kb_ref, vb_ref):
    s_len = q_ref.shape[0]
    rows = (s_len - TM) // GRID_W
    nloc = NA_KH * GRID_W
    _stage_heads(q_ref, k_ref, v_ref, qm_ref, kb_ref, vb_ref)

    kc = kb_ref[0:TM, :]
    vc = vb_ref[0:TM, :]
    outs = [_softmax_pv([_dot_nt(qm_ref[hh, 0:TM, :], kc)], [vc]) for hh in range(2)]
    m0c = lax.broadcasted_iota(jnp.int32, (TM, LANES), 1) < HEAD_DIM
    o_ref[0:TM, :] = jnp.where(m0c, outs[0], outs[1])

    m0 = lax.broadcasted_iota(jnp.int32, (GRID_W, LANES), 1) < HEAD_DIM

    def row_block(r, carry):
        rs = jnp.clip(r - NA_KH // 2, 0, rows - NA_KH)
        pat = r - rs
        q0 = pl.multiple_of(TM + r * GRID_W, GRID_W)
        k0 = pl.multiple_of(TM + rs * GRID_W, GRID_W)
        kl = kb_ref[pl.ds(k0, nloc), :]
        vl = vb_ref[pl.ds(k0, nloc), :]
        kc = kb_ref[0:TM, :]
        vc = vb_ref[0:TM, :]
        q = jnp.concatenate([qm_ref[0, pl.ds(q0, GRID_W), :], qm_ref[1, pl.ds(q0, GRID_W), :]], axis=0)
        s_loc = _dot_nt(q, kl) + bias_ref[pat].reshape(2 * GRID_W, nloc)
        s_ctx = _dot_nt(q, kc)
        res = _softmax_pv([s_loc, s_ctx], [vl, vc])
        o_ref[pl.ds(q0, GRID_W), :] = jnp.where(m0, res[:GRID_W], res[GRID_W:])
        return carry

    lax.fori_loop(0, rows, row_block, 0, unroll=4)


def _na_bias(rpb):
    h = rpb.shape[0]
    qc = np.arange(GRID_W)[:, None]
    kc = np.arange(GRID_W)[None, :]
    win = np.clip(qc - NA_KW // 2, 0, GRID_W - NA_KW)
    valid = (kc >= win) & (kc < win + NA_KW)
    col_off = np.clip(kc - qc + NA_KW - 1, 0, 2 * NA_KW - 2)
    onehot = (col_off[None] == np.arange(2 * NA_KW - 1)[:, None, None]).astype(np.float32)
    cols = jnp.einsum("hrc,cqk->hrqk", rpb.astype(F32), onehot, precision=lax.Precision.HIGHEST)
    cols = jnp.where(valid[None, None], cols * LOG2E, NEG_INF)
    bias = jnp.stack([cols[:, NA_KH - 1 - p:2 * NA_KH - 1 - p] for p in range(NA_KH)])
    bias = bias.transpose(0, 1, 3, 2, 4)
    return bias.reshape(NA_KH, h, GRID_W, NA_KH * GRID_W)


def _neighbourhood(p, bias):
    b, s, _ = p.shape
    nb = NA_W // LANES
    c0 = 4 * RET_W // LANES
    blk = lambda off: pl.BlockSpec((None, s, LANES), lambda bi, hp: (bi, 0, c0 + off + hp))
    return pl.pallas_call(
        _na_kernel,
        grid=(b, nb),
        in_specs=[
            blk(0), blk(nb), blk(2 * nb),
            pl.BlockSpec((NA_KH, 2, GRID_W, NA_KH * GRID_W), lambda bi, hp: (0, hp, 0, 0)),
        ],
        out_specs=pl.BlockSpec((None, s, LANES), lambda bi, hp: (bi, 0, hp)),
        out_shape=jax.ShapeDtypeStruct((b, s, NA_W), F32),
        scratch_shapes=[pltpu.VMEM((2, s, LANES), BF16), pltpu.VMEM((s, LANES), BF16),
                        pltpu.VMEM((s, 2 * LANES), BF16)],
        compiler_params=_cparams(("arbitrary", "arbitrary")),
        name="neighbourhood",
    )(p, p, p, bias)


def _swa_kernel(sink_ref, q_ref, k_ref, v_ref, o_ref, kd_ref, vd_ref):
    kp = pl.program_id(1)
    s_len = q_ref.shape[0]
    seq = s_len - TM
    nblk = seq // SWA_BLOCK
    band = SWA_BLOCK + 2 * SWA_WINDOW
    group = SWA_HEADS // SWA_KV_HEADS
    heads_per_step = 2 * group
    m0t = lax.broadcasted_iota(jnp.int32, (TM, LANES), 1) < HEAD_DIM

    def stage(i, carry):
        r0 = pl.multiple_of(i * TM, TM)
        for src, dst in ((k_ref, kd_ref), (v_ref, vd_ref)):
            x = src[pl.ds(r0, TM), :].astype(F32)
            xr = pltpu.roll(x, HEAD_DIM, 1)
            dst[0, pl.ds(r0, TM), 0:LANES] = jnp.where(m0t, x, xr).astype(BF16)
            dst[1, pl.ds(r0, TM), 0:LANES] = jnp.where(m0t, xr, x).astype(BF16)
        for h in range(2):
            vd_ref[h, pl.ds(r0, TM), LANES:2 * LANES] = jnp.ones((TM, LANES), BF16)
        return carry

    lax.fori_loop(0, s_len // TM, stage, 0)
    o_ref[0:TM, :] = jnp.zeros((TM, o_ref.shape[1]), F32)

    rows = group * SWA_BLOCK
    qi = lax.broadcasted_iota(jnp.int32, (rows, band), 0) % SWA_BLOCK
    ki = lax.broadcasted_iota(jnp.int32, (rows, band), 1)
    m0 = lax.broadcasted_iota(jnp.int32, (SWA_BLOCK, LANES), 1) < HEAD_DIM
    head_of_row = lax.broadcasted_iota(jnp.int32, (rows, 1), 0) // SWA_BLOCK

    def block(i, carry):
        start = jnp.clip((i - 1) * SWA_BLOCK, 0, seq - band)
        k0 = pl.multiple_of(TM + start, SWA_BLOCK)
        q0 = pl.multiple_of(TM + i * SWA_BLOCK, SWA_BLOCK)
        valid = jnp.abs(qi + (i * SWA_BLOCK - start) - ki) <= SWA_WINDOW
        for hh in range(2):
            kb = kd_ref[hh, pl.ds(k0, band), :]
            vb = vd_ref[hh, pl.ds(k0, band), :]
            kc = kd_ref[hh, 0:TM, :]
            vc = vd_ref[hh, 0:TM, :]
            parts = []
            for j in range(group // 2):
                pair = hh * (group // 2) + j
                qp = q_ref[pl.ds(q0, SWA_BLOCK), pair * LANES:(pair + 1) * LANES]
                parts += [jnp.where(m0, qp, 0.0), jnp.where(m0, 0.0, qp)]
            q = jnp.concatenate(parts, axis=0).astype(BF16)
            s = jnp.where(valid, _dot_nt(q, kb), NEG_INF)
            s_ctx = _dot_nt(q, kc)
            h0 = kp * heads_per_step + hh * group
            sink = jnp.full((rows, 1), sink_ref[h0], F32)
            for g in range(1, group):
                sink = jnp.where(head_of_row == g, sink_ref[h0 + g], sink)
            res = _softmax_pv([s, s_ctx], [vb, vc], extra=sink * LOG2E)
            for j in range(group // 2):
                pair = hh * (group // 2) + j
                r0 = 2 * j * SWA_BLOCK
                o_ref[pl.ds(q0, SWA_BLOCK), pair * LANES:(pair + 1) * LANES] = jnp.where(
                    m0, res[r0:r0 + SWA_BLOCK], res[r0 + SWA_BLOCK:r0 + 2 * SWA_BLOCK])
        return carry

    lax.fori_loop(0, nblk, block, 0, unroll=2)


def _swa(p, sink):
    b, s, _ = p.shape
    qw = SWA_HEADS * HEAD_DIM // 2
    kblk = SWA_HEADS * HEAD_DIM // LANES
    vblk = kblk + SWA_KV_HEADS * HEAD_DIM // LANES
    return pl.pallas_call(
        _swa_kernel,
        grid=(b, 2),
        in_specs=[
            pl.BlockSpec(memory_space=pltpu.SMEM),
            pl.BlockSpec((None, s, qw), lambda bi, kp: (bi, 0, kp)),
            pl.BlockSpec((None, s, LANES), lambda bi, kp: (bi, 0, kblk + kp)),
            pl.BlockSpec((None, s, LANES), lambda bi, kp: (bi, 0, vblk + kp)),
        ],
        out_specs=pl.BlockSpec((None, s, qw), lambda bi, kp: (bi, 0, kp)),
        out_shape=jax.ShapeDtypeStruct((b, s, SWA_HEADS * HEAD_DIM), F32),
        scratch_shapes=[pltpu.VMEM((2, s, LANES), BF16), pltpu.VMEM((2, s, 2 * LANES), BF16)],
        compiler_params=_cparams(("arbitrary", "arbitrary")),
        name="swa",
    )(sink, p, p, p)


def _outproj_kernel(*refs, n_mix, n_stream, t0):
    mix_refs = refs[:n_mix]
    x_refs = refs[n_mix + 1:n_mix + 1 + n_stream]
    w_ref = refs[n_mix]
    (g1_ref, gn_ref, sh_ref, sc_ref, wr_ref, br_ref,
     xo_ref, h_ref, ri_ref, cnt_ref, carry_ref) = refs[n_mix + 1 + n_stream:]
    first = (pl.program_id(0) == 0) & (pl.program_id(1) == 0)

    @pl.when(first)
    def _():
        carry_ref[...] = jnp.zeros_like(carry_ref)

    o = None
    off = 0
    for m_ref in mix_refs:
        kw = m_ref.shape[1]
        part = jnp.dot(m_ref[...].astype(BF16), w_ref[off:off + kw, :], preferred_element_type=F32)
        o = part if o is None else o + part
        off += kw
    xn = _stream_tile(x_refs, pl.program_id(1) + t0 > 0) + g1_ref[...] * o
    xo_ref[...] = xn
    h = _rms_mod(xn, gn_ref[...], sh_ref[...], sc_ref[...])
    h_ref[...] = h

    logits = _dot3_split(h, wr_ref[0], wr_ref[1]) + br_ref[...]
    tm = logits.shape[0]
    lane = lax.broadcasted_iota(jnp.int32, (tm, LANES), 1).astype(F32)
    big = 1e9
    gmask = lane < MOE_GROUPS
    mg = jnp.max(jnp.where(gmask, logits, -big), axis=-1, keepdims=True)
    sg = jnp.sum(jnp.where(gmask, jnp.exp(jnp.minimum(logits - mg, 0.0)), 0.0), axis=-1, keepdims=True)
    gw = 1.0 / sg
    gi = jnp.min(jnp.where(gmask & (logits == mg), lane, big), axis=-1, keepdims=True)
    lo = ROUTE_LANE0 + MOE_EPG * gi
    emask = (lane >= lo) & (lane < lo + MOE_EPG)
    l1 = jnp.max(jnp.where(emask, logits, -big), axis=-1, keepdims=True)
    i1 = jnp.min(jnp.where(emask & (logits == l1), lane, big), axis=-1, keepdims=True)
    emask2 = emask & (lane != i1)
    l2 = jnp.max(jnp.where(emask2, logits, -big), axis=-1, keepdims=True)
    i2 = jnp.min(jnp.where(emask2 & (logits == l2), lane, big), axis=-1, keepdims=True)
    e21 = jnp.exp(l2 - l1)
    w1 = gw / (1.0 + e21)
    w2 = gw * e21 / (1.0 + e21)

    oh = jnp.where((lane == i1) | (lane == i2), 1.0, 0.0)
    tri = (lax.broadcasted_iota(jnp.int32, (tm, tm), 0) > lax.broadcasted_iota(jnp.int32, (tm, tm), 1))
    cum = jnp.dot(jnp.where(tri, 1.0, 0.0).astype(BF16), oh.astype(BF16), preferred_element_type=F32) + carry_ref[...]
    r1 = jnp.sum(jnp.where(lane == i1, cum, 0.0), axis=-1, keepdims=True)
    r2 = jnp.sum(jnp.where(lane == i2, cum, 0.0), axis=-1, keepdims=True)
    carry_ref[...] = carry_ref[...] + jnp.sum(oh, axis=0, keepdims=True)
    cnt_ref[...] = carry_ref[...]

    ri = jnp.where(lane == 0, i1 - ROUTE_LANE0, 0.0)
    ri = jnp.where(lane == 1, i2 - ROUTE_LANE0, ri)
    ri = jnp.where(lane == 2, w1, ri)
    ri = jnp.where(lane == 3, w2, ri)
    ri = jnp.where(lane == 4, r1, ri)
    ri = jnp.where(lane == 5, r2, ri)
    ri_ref[...] = ri


def _outproj(mixes, w_bf16, xa, g1, gain2, sh2, sc2, wr, br, t0):
    b, s, d = _stream_shape(xa)
    nt = s // TM - t0
    so = nt * TM
    mod_idx = lambda bi, t: (jnp.where(t + t0 == 0, b, bi), 0, 0)
    tile = lambda wdt: pl.BlockSpec((None, TM, wdt), lambda bi, t: (bi, t + t0, 0))
    otile = lambda wdt: pl.BlockSpec((None, TM, wdt), lambda bi, t: (bi, t, 0))
    const = lambda shape: pl.BlockSpec(shape, lambda bi, t: (0,) * len(shape))
    x_specs, x_args = _stream_specs(xa, t0)
    in_specs = [tile(m.shape[2]) for m in mixes] + [const(w_bf16.shape)] + x_specs + [
        pl.BlockSpec((None, 1, d), mod_idx), const((1, d)),
        pl.BlockSpec((None, 1, d), mod_idx), pl.BlockSpec((None, 1, d), mod_idx),
        const((2, d, LANES)), const((1, LANES)),
    ]
    return pl.pallas_call(
        functools.partial(_outproj_kernel, n_mix=len(mixes), n_stream=len(x_args), t0=t0),
        grid=(b, nt),
        in_specs=in_specs,
        out_specs=[otile(d), otile(d), otile(LANES), const((1, LANES))],
        out_shape=[jax.ShapeDtypeStruct((b, so, d), F32), jax.ShapeDtypeStruct((b, so, d), F32),
                   jax.ShapeDtypeStruct((b, so, LANES), F32), jax.ShapeDtypeStruct((1, LANES), F32)],
        scratch_shapes=[pltpu.VMEM((1, LANES), F32)],
        compiler_params=_cparams(("arbitrary", "arbitrary")),
        name="outproj",
    )(*mixes, w_bf16, *x_args, g1, gain2, sh2, sc2, wr, br)


SUBLANES = 8


def _to_token_tiles(x):
    return x.reshape(x.shape[0], SUBLANES, x.shape[1] // SUBLANES)


def _from_token_tiles(x3):
    return x3.reshape(x3.shape[0], x3.shape[1] * x3.shape[2])


PAD_BITS = tuple(1 << k for k in reversed(range(MOE_ROWS.bit_length() - 1)))


def _dispatch_kernel(dest_ref, pads_ref, h_ref, xs_ref, hbuf, zbuf, sem, zsem):
    nt = pl.num_programs(1)
    step = pl.program_id(0) * nt + pl.program_id(1)
    last = pl.num_programs(0) * nt - 1
    slot = step % 2

    def zero_pads(wait):
        def expert(e, c):
            pos = pads_ref[0, e]
            n = pads_ref[1, e]
            for bit in PAD_BITS:
                take = (n & bit) != 0
                cp = pltpu.make_async_copy(zbuf.at[pl.ds(0, bit)], xs_ref.at[pl.ds(pos, bit)], zsem)

                @pl.when(take)
                def _():
                    cp.wait() if wait else cp.start()

                pos = pos + jnp.where(take, bit, 0)
            return c
        lax.fori_loop(0, MOE_EXPERTS, expert, 0)

        def tail(j, c):
            cp = pltpu.make_async_copy(
                zbuf, xs_ref.at[pl.ds(pads_ref[0, MOE_EXPERTS] + j * PAD_BITS[0], PAD_BITS[0])], zsem)
            cp.wait() if wait else cp.start()
            return c
        lax.fori_loop(0, pads_ref[1, MOE_EXPERTS], tail, 0)

    @pl.when(step == 0)
    def _():
        zbuf[...] = jnp.zeros_like(zbuf)
        zero_pads(wait=False)
    hbuf[slot] = _to_token_tiles(h_ref[...])

    def copy(sl, i, dst_row):
        return pltpu.make_async_copy(hbuf.at[sl, i], xs_ref.at[dst_row], sem.at[sl])

    for i in range(TM):
        copy(slot, i, dest_ref[step, i]).start()
        copy(slot, i, dest_ref[step, TM + i]).start()

    def drain(sl):
        def one(i, c):
            copy(sl, 0, 0).wait()
            return c
        lax.fori_loop(0, 2 * TM, one, 0, unroll=8)

    @pl.when(step > 0)
    def _():
        drain(1 - slot)

    @pl.when(step == last)
    def _():
        drain(slot)
        zero_pads(wait=True)


def _dispatch(dest, pads, h2, n_pad):
    b, s, d = h2.shape
    nt = s // TM
    tile = (SUBLANES, d // SUBLANES)
    return pl.pallas_call(
        _dispatch_kernel,
        grid_spec=pltpu.PrefetchScalarGridSpec(
            num_scalar_prefetch=2,
            grid=(b, nt),
            in_specs=[pl.BlockSpec((None, TM, d), lambda bi, t, dr, pd: (bi, t, 0))],
            out_specs=pl.BlockSpec(memory_space=pl.ANY),
            scratch_shapes=[pltpu.VMEM((2, TM) + tile, F32), pltpu.VMEM((PAD_BITS[0],) + tile, F32),
                            pltpu.SemaphoreType.DMA((2,)), pltpu.SemaphoreType.DMA],
        ),
        out_shape=jax.ShapeDtypeStruct((n_pad,) + tile, F32),
        compiler_params=_cparams(("arbitrary", "arbitrary")),
        name="dispatch",
    )(dest, pads, h2)


def _mlp_kernel(be_ref, nu_ref, x_ref, wgu_ref, wd_ref, y_ref, wgu_b, wd_b):
    i = pl.program_id(0)
    prev = be_ref[jnp.maximum(i - 1, 0)]
    used = i < nu_ref[0]

    @pl.when(used & ((i == 0) | (be_ref[i] != prev)))
    def _():
        wgu_b[...] = wgu_ref[...].astype(BF16)
        wd_b[...] = wd_ref[...].astype(BF16)

    @pl.when(used)
    def _():
        x = _from_token_tiles(x_ref[...])
        gu = jnp.dot(x.astype(BF16), wgu_b[...], preferred_element_type=F32)
        act = _silu(gu[:, :MOE_FF]) * gu[:, MOE_FF:]
        y_ref[...] = _to_token_tiles(jnp.dot(act.astype(BF16), wd_b[...], preferred_element_type=F32))

    @pl.when(jnp.logical_not(used))
    def _():
        y_ref[...] = jnp.zeros_like(y_ref)


def _expert_mlp(block_e, n_used, xs, w_gu, w_down, layer):
    n_pad, sub, dl = xs.shape
    d = sub * dl
    ff2 = w_gu.shape[-1]
    slots = pl.BlockSpec((MOE_ROWS, sub, dl), lambda i, be, nu: (i, 0, 0))
    used_slots = pl.BlockSpec((MOE_ROWS, sub, dl), lambda i, be, nu: (jnp.minimum(i, nu[0] - 1), 0, 0))
    return pl.pallas_call(
        _mlp_kernel,
        grid_spec=pltpu.PrefetchScalarGridSpec(
            num_scalar_prefetch=2,
            grid=(n_pad // MOE_ROWS,),
            in_specs=[
                used_slots,
                pl.BlockSpec((None, None, d, ff2), lambda i, be, nu: (layer, be[i], 0, 0)),
                pl.BlockSpec((None, None, ff2 // 2, d), lambda i, be, nu: (layer, be[i], 0, 0)),
            ],
            out_specs=slots,
            scratch_shapes=[pltpu.VMEM((d, ff2), BF16), pltpu.VMEM((ff2 // 2, d), BF16)],
        ),
        out_shape=jax.ShapeDtypeStruct(xs.shape, F32),
        compiler_params=_cparams(("arbitrary",)),
        name="expert_mlp",
    )(block_e, n_used, xs, w_gu, w_down)


def _combine_kernel(dest_ref, ys_ref, x_ref, ri_ref, g2_ref, fg_ref, o_ref, buf, sem, *, final):
    bi = pl.program_id(0)
    t = pl.program_id(1)
    nt = pl.num_programs(1)
    step = bi * nt + t
    total = pl.num_programs(0) * nt

    def copy(src_row, slot, k, i):
        return pltpu.make_async_copy(ys_ref.at[src_row], buf.at[slot, k, i], sem.at[slot])

    def issue(st, slot):
        for i in range(TM):
            copy(dest_ref[st, i], slot, 0, i).start(priority=0)
            copy(dest_ref[st, TM + i], slot, 1, i).start(priority=1)

    slot = step % 2

    @pl.when(step == 0)
    def _():
        issue(0, 0)

    @pl.when(step + 1 < total)
    def _():
        issue(step + 1, 1 - slot)

    def drain(i, c):
        copy(0, slot, 0, 0).wait()
        return c

    lax.fori_loop(0, 2 * TM, drain, 0, unroll=8)

    lane = lax.broadcasted_iota(jnp.int32, (TM, LANES), 1)
    ri = ri_ref[...]
    w1 = jnp.sum(jnp.where(lane == 2, ri, 0.0), axis=-1, keepdims=True)
    w2 = jnp.sum(jnp.where(lane == 3, ri, 0.0), axis=-1, keepdims=True)
    y = _from_token_tiles(buf[slot, 0]) * w1 + _from_token_tiles(buf[slot, 1]) * w2
    xn = x_ref[...] + g2_ref[...] * y
    if final:
        ms = jnp.mean(xn * xn, axis=-1, keepdims=True)
        xn = xn * lax.rsqrt(ms + NORM_EPS) * fg_ref[...]
    o_ref[...] = xn


def _combine(dest, ys, xa, rinfo, g2, final_g, has_ctx, final):
    b, s, d = xa.shape
    nt = s // TM
    mod_idx = lambda bi, t, dr: (jnp.where(t == 0, b, bi) if has_ctx else bi, 0, 0)
    tile = lambda wdt: pl.BlockSpec((None, TM, wdt), lambda bi, t, dr: (bi, t, 0))
    out_spec = tile(d)
    out_shape = jax.ShapeDtypeStruct((b, s, d), F32)
    return pl.pallas_call(
        functools.partial(_combine_kernel, final=final),
        grid_spec=pltpu.PrefetchScalarGridSpec(
            num_scalar_prefetch=1,
            grid=(b, nt),
            in_specs=[
                pl.BlockSpec(memory_space=pl.ANY),
                tile(d), tile(LANES),
                pl.BlockSpec((None, 1, d), mod_idx),
                pl.BlockSpec((1, d), lambda bi, t, dr: (0, 0)),
            ],
            out_specs=out_spec,
            scratch_shapes=[pltpu.VMEM((2, 2, TM, SUBLANES, d // SUBLANES), F32),
                            pltpu.SemaphoreType.DMA((2,))],
        ),
        out_shape=out_shape,
        compiler_params=_cparams(("arbitrary", "arbitrary")),
        name="combine",
    )(dest, ys, xa, rinfo, g2, final_g)


def _moe(h2, rinfo, counts, w_gu, w_down, layer):
    b, s, d = h2.shape
    nt = s // TM
    r = rinfo[:, :, :8].reshape(b * nt, TM, 8)
    lanes = lambda i: jnp.concatenate([r[..., i], r[..., i + 1]], axis=1).astype(jnp.int32)
    e = lanes(0)
    rank = lanes(4)
    cnt = counts[0, ROUTE_LANE0:ROUTE_LANE0 + MOE_EXPERTS].astype(jnp.int32)
    padded = (cnt + MOE_ROWS - 1) // MOE_ROWS * MOE_ROWS
    ends = jnp.cumsum(padded)
    starts = ends - padded
    eids = jnp.arange(MOE_EXPERTS, dtype=jnp.int32)
    dest = jnp.sum(jnp.where(e[..., None] == eids, starts, 0), axis=-1) + rank
    n_assign = b * nt * TM * 2
    n_blocks = (n_assign + MOE_EXPERTS * (MOE_ROWS - 1)) // MOE_ROWS + 1
    n_pad = n_blocks * MOE_ROWS
    blk_row = jnp.arange(n_blocks, dtype=jnp.int32) * MOE_ROWS
    block_e = jnp.minimum(jnp.sum((blk_row[:, None] >= ends[None, :]).astype(jnp.int32), axis=1),
                          MOE_EXPERTS - 1)
    n_used = (ends[-1] // MOE_ROWS).astype(jnp.int32).reshape(1)
    pads = jnp.stack([jnp.append(starts + cnt, ends[-1]),
                      jnp.append(padded - cnt, (n_pad - ends[-1]) // PAD_BITS[0])]).astype(jnp.int32)
    xs = _dispatch(dest, pads, h2, n_pad)
    ys = _expert_mlp(block_e, n_used, xs, w_gu, w_down, layer)
    return dest, ys


def _router_weights(wg, bg, we, be):
    d = wg.shape[0]
    pad = LANES - MOE_GROUPS - MOE_EXPERTS
    assert ROUTE_LANE0 == MOE_GROUPS
    wr = jnp.concatenate([wg.astype(F32), we.astype(F32), jnp.zeros((d, pad), F32)], axis=1)
    br = jnp.concatenate([bg.astype(F32), be.astype(F32), jnp.zeros((pad,), F32)]).reshape(1, LANES)
    return jnp.stack(_split_bf16(wr)), br


def kernel(x, c, ctx, c_ctx, ada_w, ada_b, norm_g, final_g, ab_w_in, ab_w_out, ret_decay, ret_gn, na_rpb,
           swa_w_in, swa_w_out, swa_sink, router_g_w, router_g_b, router_e_w, router_e_b,
           expert_w_gu, expert_w_down):
    b, seq, d = x.shape
    assert ctx.shape[1] == TM and seq % TM == 0 and d == D_MODEL
    xa = (ctx, x)
    rope = _rope_tables(seq)

    cvec = jnp.concatenate([c, c_ctx[None, :], jnp.zeros((7, d), F32)], axis=0)
    mod = _adaln(cvec, ada_w, ada_b)
    mod = mod.reshape(DEPTH, b + 8, 6, 1, d)[:, :b + 1].transpose(0, 2, 1, 3, 4)

    for layer in range(DEPTH):
        last = layer == DEPTH - 1
        sh1, sc1, g1, sh2, sc2, g2 = (mod[layer, i] for i in range(6))
        gain1 = norm_g[layer, 0].reshape(1, d)
        gain2 = norm_g[layer, 1].reshape(1, d)
        j = layer // 2
        if layer % 2 == 0:
            nb = RET_W // LANES
            qk_scale = HEAD_DIM ** -0.5
            scales = {blk: qk_scale for blk in range(nb, 2 * nb)}
            scales.update({blk: qk_scale * LOG2E for blk in range(4 * nb, 5 * nb)})
            p = _proj(xa, gain1, sh1, sc1, ab_w_in[j].astype(BF16), rope,
                      rope_blocks=range(0, 2 * nb), scales=scales)
            log_gamma = jnp.log1p(-jnp.exp2(-ret_decay[j].astype(F32)))
            mixes = [_retention(p, log_gamma, ret_gn[j]), _neighbourhood(p, _na_bias(na_rpb[j]))]
            w_out = ab_w_out[j]
        else:
            qb = SWA_HEADS * HEAD_DIM // LANES
            kb = SWA_KV_HEADS * HEAD_DIM // LANES
            p = _proj(xa, gain1, sh1, sc1, swa_w_in[j].astype(BF16), rope,
                      rope_blocks=range(0, qb + kb),
                      scales={blk: HEAD_DIM ** -0.5 * LOG2E for blk in range(qb)})
            mixes = [_swa(p, swa_sink[j].astype(F32))]
            w_out = swa_w_out[j]
        t0 = 1 if last else 0
        wr, br = _router_weights(router_g_w[layer], router_g_b[layer], router_e_w[layer], router_e_b[layer])
        xa, h2, rinfo, counts = _outproj(mixes, w_out.astype(BF16), xa, g1, gain2, sh2, sc2, wr, br, t0)
        dest, ys = _moe(h2, rinfo, counts, expert_w_gu, expert_w_down, layer)
        xa = _combine(dest, ys, xa, rinfo, g2, final_g.reshape(1, d), not last, last)
    return xa
```

```python
import functools

import numpy as np
import jax
import jax.numpy as jnp
from jax import lax
from jax.experimental import pallas as pl
from jax.experimental.pallas import tpu as pltpu

F32 = jnp.float32
BF16 = jnp.bfloat16

D_MODEL = 1024
DEPTH = 2
GRID_W = 64
HEAD_DIM = 64
RET_HEADS = 8
NA_HEADS = 8
RET_W = 512
NA_W = 512
AB_IN = 4 * RET_W + 3 * NA_W
RET_CHUNK = 128
GN_EPS = 1e-5
NA_KH = 8
NA_KW = 16
SWA_HEADS = 16
SWA_KV_HEADS = 4
SWA_WINDOW = 128
SWA_BLOCK = 128
SWA_IN = (SWA_HEADS + 2 * SWA_KV_HEADS) * HEAD_DIM
ROPE_BASE = 10000.0
MOE_GROUPS = 4
MOE_EPG = 8
MOE_EXPERTS = 32
MOE_FF = 512
NORM_EPS = 1e-6
NEG_INF = -1e30

LANES = 128
TM = 256
RET_BLOCK = 256
MOE_ROWS = 512
ROUTE_LANE0 = 4
VMEM_LIMIT = 56 * 1024 * 1024


def _cparams(sem, vmem=VMEM_LIMIT, flags=None):
    return pltpu.CompilerParams(dimension_semantics=sem, vmem_limit_bytes=vmem, flags=flags)


def _split_bf16(a):
    hi = a.astype(BF16)
    lo = (a - hi.astype(F32)).astype(BF16)
    return hi, lo


def _dot3_split(a, bh, bl):
    ah, al = _split_bf16(a)
    d = lambda x, y: jnp.dot(x, y, preferred_element_type=F32)
    return d(ah, bh) + (d(ah, bl) + d(al, bh))


def _dot3(a, b):
    return _dot3_split(a, *_split_bf16(b))


def _dot_nt(a, b):
    return lax.dot_general(a, b, (((1,), (1,)), ((), ())), preferred_element_type=F32)


def _dot_tn(a, b):
    return lax.dot_general(a, b, (((0,), (0,)), ((), ())), preferred_element_type=F32)


def _silu(x):
    return x / (1.0 + jnp.exp(-x))


def _adaln_kernel(c_ref, w_ref, b_ref, o_ref):
    o_ref[...] = _dot3(_silu(c_ref[...]), w_ref[...]) + b_ref[...]


def _adaln(cvec, ada_w, ada_b):
    depth, d, n6 = ada_w.shape
    rows = cvec.shape[0]
    tn = 1024
    return pl.pallas_call(
        _adaln_kernel,
        grid=(depth, n6 // tn),
        in_specs=[
            pl.BlockSpec((rows, d), lambda l, j: (0, 0)),
            pl.BlockSpec((None, d, tn), lambda l, j: (l, 0, j)),
            pl.BlockSpec((None, 1, tn), lambda l, j: (l, 0, j)),
        ],
        out_specs=pl.BlockSpec((None, rows, tn), lambda l, j: (l, 0, j)),
        out_shape=jax.ShapeDtypeStruct((depth, rows, n6), F32),
        compiler_params=_cparams(("arbitrary", "arbitrary")),
        name="adaln",
    )(cvec, ada_w, ada_b.reshape(depth, 1, n6))


def _rms_mod(x, g, sh, sc):
    ms = jnp.mean(x * x, axis=-1, keepdims=True)
    return (x * lax.rsqrt(ms + NORM_EPS) * g) * (1.0 + sc) + sh


def _stream_tile(refs, is_lat):
    if len(refs) == 1:
        return refs[0][...]
    return jnp.where(is_lat, refs[1][...], refs[0][...])


def _stream_specs(stream, t0, extra_args=0):
    def im(f):
        return (lambda bi, t, *_: f(bi, t + t0))
    if not isinstance(stream, tuple):
        return [pl.BlockSpec((None, TM, stream.shape[2]), im(lambda bi, t: (bi, t, 0)))], [stream]
    ctx, x = stream
    d = x.shape[2]
    return ([pl.BlockSpec((None, TM, d), im(lambda bi, t: (bi, 0, 0))),
             pl.BlockSpec((None, TM, d), im(lambda bi, t: (bi, jnp.maximum(t - 1, 0), 0)))], [ctx, x])


def _stream_shape(stream):
    if not isinstance(stream, tuple):
        return stream.shape
    ctx, x = stream
    return (x.shape[0], ctx.shape[1] + x.shape[1], x.shape[2])


def _proj_kernel(*refs, rope_blocks, scales, cn, n_stream):
    x_refs = refs[:n_stream]
    g_ref, sh_ref, sc_ref, w_ref, rope_ref, o_ref = refs[n_stream:]
    is_lat = pl.program_id(1) > 0
    hb = _rms_mod(_stream_tile(x_refs, is_lat), g_ref[...], sh_ref[...], sc_ref[...]).astype(BF16)
    nout = w_ref.shape[1]
    for c in range(nout // cn):
        o = jnp.dot(hb, w_ref[:, c * cn:(c + 1) * cn], preferred_element_type=F32)
        for s in range(cn // LANES):
            blk = c * (cn // LANES) + s
            ob = o[:, s * LANES:(s + 1) * LANES]
            if blk in rope_blocks:
                r = (ob * rope_ref[0] + pltpu.roll(ob, 16, 1) * rope_ref[1]
                     + pltpu.roll(ob, LANES - 16, 1) * rope_ref[2])
                ob = jnp.where(is_lat, r, ob)
            if blk in scales:
                ob = ob * scales[blk]
            o_ref[:, blk * LANES:(blk + 1) * LANES] = ob.astype(o_ref.dtype)


def _proj(xa, gain, sh, sc, w_bf16, rope, rope_blocks, scales):
    b, s, d = _stream_shape(xa)
    nout = w_bf16.shape[1]
    nt = s // TM
    mod_idx = lambda bi, t: (jnp.where(t == 0, b, bi), 0, 0)
    x_specs, x_args = _stream_specs(xa, 0)
    kern = functools.partial(_proj_kernel, rope_blocks=frozenset(rope_blocks), scales=dict(scales), cn=512,
                             n_stream=len(x_args))
    return pl.pallas_call(
        kern,
        grid=(b, nt),
        in_specs=x_specs + [
            pl.BlockSpec((1, d), lambda bi, t: (0, 0)),
            pl.BlockSpec((None, 1, d), mod_idx),
            pl.BlockSpec((None, 1, d), mod_idx),
            pl.BlockSpec((d, nout), lambda bi, t: (0, 0)),
            pl.BlockSpec((3, TM, LANES), lambda bi, t: (0, jnp.maximum(t - 1, 0), 0)),
        ],
        out_specs=pl.BlockSpec((None, TM, nout), lambda bi, t: (bi, t, 0)),
        out_shape=jax.ShapeDtypeStruct((b, s, nout), BF16),
        compiler_params=_cparams(("arbitrary", "arbitrary")),
        name="proj",
    )(*x_args, gain, sh, sc, w_bf16, rope)


def _rope_tables(seq):
    nf = HEAD_DIM // 4
    inv = ROPE_BASE ** (-jnp.arange(nf, dtype=F32) / nf)
    t = jnp.arange(seq)
    row = (t // GRID_W).astype(F32)
    col = (t % GRID_W).astype(F32)
    lane = np.arange(LANES)
    jj = lane % HEAD_DIM
    axis_is_col = (jj // 32) == 1
    second_half = (jj % 32) >= 16
    f = jj % 16
    pos = jnp.where(axis_is_col[None, :], col[:, None], row[:, None])
    ang = pos * inv[f][None, :]
    c, s = jnp.cos(ang), jnp.sin(ang)
    sa = jnp.where(second_half[None, :], s, 0.0)
    sb = jnp.where(second_half[None, :], 0.0, -s)
    return jnp.stack([c, sa, sb], axis=0)


def _ret_kernel(lg_ref, q_ref, k_ref, v_ref, g_ref, gn_ref, o_ref, accf_ref, accb_ref,
                intra_ref, qd_ref, kd_ref):
    hp = pl.program_id(1)
    c = RET_BLOCK
    s_len = q_ref.shape[0]
    n_chunks = s_len // c
    ctx_chunks = TM // c
    pos = lax.broadcasted_iota(jnp.int32, (c, LANES), 0).astype(F32)
    m0 = lax.broadcasted_iota(jnp.int32, (c, LANES), 1) < HEAD_DIM
    same_head = ((lax.broadcasted_iota(jnp.int32, (LANES, LANES), 0) < HEAD_DIM)
                 == (lax.broadcasted_iota(jnp.int32, (LANES, LANES), 1) < HEAD_DIM))
    rel = (lax.broadcasted_iota(jnp.int32, (c, c), 0) - lax.broadcasted_iota(jnp.int32, (c, c), 1)).astype(F32)
    lgf = [lg_ref[0, hp * 2 + hh] for hh in range(2)]
    lgb = [lg_ref[1, hp * 2 + hh] for hh in range(2)]
    lgf_l = jnp.where(m0, lgf[0], lgf[1])
    lgb_l = jnp.where(m0, lgb[0], lgb[1])
    for hh in range(2):
        intra_ref[0, :, hh * c:(hh + 1) * c] = jnp.where(rel >= 0, jnp.exp(lgf[hh] * jnp.maximum(rel, 0.0)), 0.0)
        intra_ref[1, :, hh * c:(hh + 1) * c] = jnp.where(rel <= 0, jnp.exp(lgb[hh] * jnp.maximum(-rel, 0.0)), 0.0)
    qd_ref[0] = jnp.exp(lgf_l * (pos + 1.0))
    qd_ref[1] = jnp.exp(lgb_l * (c - pos))
    kd_ref[0] = jnp.exp(lgf_l * (c - 1.0 - pos))
    kd_ref[1] = jnp.exp(lgb_l * pos)
    cd = [jnp.exp(lgf_l[0:1] * float(c)), jnp.exp(lgb_l[0:1] * float(c))]

    def chunk(r0, state, d):
        q = q_ref[pl.ds(r0, c), :]
        k = k_ref[pl.ds(r0, c), :]
        v = v_ref[pl.ds(r0, c), :]
        qb = q.astype(BF16)
        kcat = jnp.concatenate([jnp.where(m0, k, 0.0), jnp.where(m0, 0.0, k)], axis=0).astype(BF16)
        vcat = jnp.concatenate([jnp.where(m0, v, 0.0), jnp.where(m0, 0.0, v)], axis=0).astype(BF16)
        s = _dot_nt(qb, kcat) * intra_ref[d]
        out = jnp.dot(s.astype(BF16), vcat, preferred_element_type=F32)
        out = out + jnp.dot(qb, state.astype(BF16), preferred_element_type=F32) * qd_ref[d]
        kv = _dot_tn((k * kd_ref[d]).astype(BF16), v.astype(BF16))
        return out, state * cd[d] + jnp.where(same_head, kv, 0.0)

    def body(i, states):
        sf, sb = states
        rf = pl.multiple_of(i * c, c)
        ib = jnp.where(i < ctx_chunks, ctx_chunks - 1 - i, n_chunks + ctx_chunks - 1 - i)
        rb = pl.multiple_of(ib * c, c)
        of, sf = chunk(rf, sf, 0)
        ob, sb = chunk(rb, sb, 1)
        accf_ref[pl.ds(rf, c), :] = of
        accb_ref[pl.ds(rb, c), :] = ob
        return sf, sb

    z = jnp.zeros((LANES, LANES), F32)
    lax.fori_loop(0, n_chunks, body, (z, z), unroll=3)

    avg = jnp.where(same_head, 1.0 / HEAD_DIM, 0.0).astype(BF16)

    def head_mean(x):
        hi, lo = _split_bf16(x)
        return (jnp.dot(hi, avg, preferred_element_type=F32) + jnp.dot(lo, avg, preferred_element_type=F32))

    def readout(i, carry):
        r0 = pl.multiple_of(i * c, c)
        o = accf_ref[pl.ds(r0, c), :] + accb_ref[pl.ds(r0, c), :]
        dlt = o - head_mean(o)
        var = head_mean(dlt * dlt)
        y = dlt * lax.rsqrt(var + GN_EPS) * gn_ref[...]
        o_ref[pl.ds(r0, c), :] = _silu(g_ref[pl.ds(r0, c), :].astype(F32)) * y
        return carry

    lax.fori_loop(0, n_chunks, readout, 0, unroll=2)


def _retention(p, log_gamma, ret_gn):
    b, s, _ = p.shape
    nb = RET_W // LANES
    blk = lambda off: pl.BlockSpec((None, s, LANES), lambda bi, hp: (bi, 0, off + hp))
    return pl.pallas_call(
        _ret_kernel,
        grid=(b, nb),
        in_specs=[
            pl.BlockSpec(memory_space=pltpu.SMEM),
            blk(0), blk(nb), blk(2 * nb), blk(3 * nb),
            pl.BlockSpec((1, LANES), lambda bi, hp: (0, hp)),
        ],
        out_specs=pl.BlockSpec((None, s, LANES), lambda bi, hp: (bi, 0, hp)),
        out_shape=jax.ShapeDtypeStruct((b, s, RET_W), F32),
        scratch_shapes=[pltpu.VMEM((s, LANES), F32), pltpu.VMEM((s, LANES), F32),
                        pltpu.VMEM((2, RET_BLOCK, 2 * RET_BLOCK), F32),
                        pltpu.VMEM((2, RET_BLOCK, LANES), F32), pltpu.VMEM((2, RET_BLOCK, LANES), F32)],
        compiler_params=_cparams(("arbitrary", "arbitrary")),
        name="retention",
    )(log_gamma, p, p, p, p, ret_gn.reshape(1, RET_W))


LOG2E = 1.4426950408889634


def _softmax_pv(s_list, v_list, extra=None):
    m = None
    for s in s_list:
        for j in range(s.shape[1] // LANES):
            blk = s[:, j * LANES:(j + 1) * LANES]
            m = blk if m is None else jnp.maximum(m, blk)
    m = m.max(axis=-1, keepdims=True)
    if extra is not None:
        m = jnp.maximum(m, extra)
    acc = None
    for s, v in zip(s_list, v_list):
        pv = jnp.dot(jnp.exp2(s - m).astype(BF16), v, preferred_element_type=F32)
        acc = pv if acc is None else acc + pv
    o, den = acc[:, :LANES], acc[:, LANES:]
    if extra is not None:
        den = den + jnp.exp2(extra - m)
    return o / den


def _stage_heads(q_ref, k_ref, v_ref, qm_ref, kb_ref, vb_ref):
    m0 = lax.broadcasted_iota(jnp.int32, (TM, LANES), 1) < HEAD_DIM

    def stage(i, carry):
        r0 = pl.multiple_of(i * TM, TM)
        q = q_ref[pl.ds(r0, TM), :]
        qm_ref[0, pl.ds(r0, TM), :] = jnp.where(m0, q, 0.0).astype(BF16)
        qm_ref[1, pl.ds(r0, TM), :] = jnp.where(m0, 0.0, q).astype(BF16)
        kb_ref[pl.ds(r0, TM), :] = k_ref[pl.ds(r0, TM), :].astype(BF16)
        vb_ref[pl.ds(r0, TM), 0:LANES] = v_ref[pl.ds(r0, TM), :].astype(BF16)
        vb_ref[pl.ds(r0, TM), LANES:2 * LANES] = jnp.ones((TM, LANES), BF16)
        return carry

    lax.fori_loop(0, q_ref.shape[0] // TM, stage, 0)


def _na_kernel(q_ref, k_ref, v_ref, bias_ref, o_ref, qm_ref, kb_ref, vb_ref):
    s_len = q_ref.shape[0]
    rows = (s_len - TM) // GRID_W
    nloc = NA_KH * GRID_W
    _stage_heads(q_ref, k_ref, v_ref, qm_ref, kb_ref, vb_ref)

    kc = kb_ref[0:TM, :]
    vc = vb_ref[0:TM, :]
    outs = [_softmax_pv([_dot_nt(qm_ref[hh, 0:TM, :], kc)], [vc]) for hh in range(2)]
    m0c = lax.broadcasted_iota(jnp.int32, (TM, LANES), 1) < HEAD_DIM
    o_ref[0:TM, :] = jnp.where(m0c, outs[0], outs[1])

    m0 = lax.broadcasted_iota(jnp.int32, (GRID_W, LANES), 1) < HEAD_DIM

    def row_block(r, carry):
        rs = jnp.clip(r - NA_KH // 2, 0, rows - NA_KH)
        pat = r - rs
        q0 = pl.multiple_of(TM + r * GRID_W, GRID_W)
        k0 = pl.multiple_of(TM + rs * GRID_W, GRID_W)
        kl = kb_ref[pl.ds(k0, nloc), :]
        vl = vb_ref[pl.ds(k0, nloc), :]
        kc = kb_ref[0:TM, :]
        vc = vb_ref[0:TM, :]
        q = jnp.concatenate([qm_ref[0, pl.ds(q0, GRID_W), :], qm_ref[1, pl.ds(q0, GRID_W), :]], axis=0)
        s_loc = _dot_nt(q, kl) + bias_ref[pat].reshape(2 * GRID_W, nloc)
        s_ctx = _dot_nt(q, kc)
        res = _softmax_pv([s_loc, s_ctx], [vl, vc])
        o_ref[pl.ds(q0, GRID_W), :] = jnp.where(m0, res[:GRID_W], res[GRID_W:])
        return carry

    lax.fori_loop(0, rows, row_block, 0, unroll=8)


def _na_bias(rpb):
    h = rpb.shape[0]
    qc = np.arange(GRID_W)[:, None]
    kc = np.arange(GRID_W)[None, :]
    win = np.clip(qc - NA_KW // 2, 0, GRID_W - NA_KW)
    valid = (kc >= win) & (kc < win + NA_KW)
    col_off = np.clip(kc - qc + NA_KW - 1, 0, 2 * NA_KW - 2)
    onehot = (col_off[None] == np.arange(2 * NA_KW - 1)[:, None, None]).astype(np.float32)
    cols = jnp.einsum("hrc,cqk->hrqk", rpb.astype(F32), onehot, precision=lax.Precision.HIGHEST)
    cols = jnp.where(valid[None, None], cols * LOG2E, NEG_INF)
    bias = jnp.stack([cols[:, NA_KH - 1 - p:2 * NA_KH - 1 - p] for p in range(NA_KH)])
    bias = bias.transpose(0, 1, 3, 2, 4)
    return bias.reshape(NA_KH, h, GRID_W, NA_KH * GRID_W)


def _neighbourhood(p, bias):
    b, s, _ = p.shape
    nb = NA_W // LANES
    c0 = 4 * RET_W // LANES
    blk = lambda off: pl.BlockSpec((None, s, LANES), lambda bi, hp: (bi, 0, c0 + off + hp))
    return pl.pallas_call(
        _na_kernel,
        grid=(b, nb),
        in_specs=[
            blk(0), blk(nb), blk(2 * nb),
            pl.BlockSpec((NA_KH, 2, GRID_W, NA_KH * GRID_W), lambda bi, hp: (0, hp, 0, 0)),
        ],
        out_specs=pl.BlockSpec((None, s, LANES), lambda bi, hp: (bi, 0, hp)),
        out_shape=jax.ShapeDtypeStruct((b, s, NA_W), F32),
        scratch_shapes=[pltpu.VMEM((2, s, LANES), BF16), pltpu.VMEM((s, LANES), BF16),
                        pltpu.VMEM((s, 2 * LANES), BF16)],
        compiler_params=_cparams(("arbitrary", "arbitrary")),
        name="neighbourhood",
    )(p, p, p, bias)


def _swa_kernel(sink_ref, q_ref, k_ref, v_ref, o_ref, kd_ref, vd_ref):
    kp = pl.program_id(1)
    s_len = q_ref.shape[0]
    seq = s_len - TM
    nblk = seq // SWA_BLOCK
    band = SWA_BLOCK + 2 * SWA_WINDOW
    group = SWA_HEADS // SWA_KV_HEADS
    heads_per_step = 2 * group
    m0t = lax.broadcasted_iota(jnp.int32, (TM, LANES), 1) < HEAD_DIM

    def stage(i, carry):
        r0 = pl.multiple_of(i * TM, TM)
        for src, dst in ((k_ref, kd_ref), (v_ref, vd_ref)):
            x = src[pl.ds(r0, TM), :].astype(F32)
            xr = pltpu.roll(x, HEAD_DIM, 1)
            dst[0, pl.ds(r0, TM), 0:LANES] = jnp.where(m0t, x, xr).astype(BF16)
            dst[1, pl.ds(r0, TM), 0:LANES] = jnp.where(m0t, xr, x).astype(BF16)
        for h in range(2):
            vd_ref[h, pl.ds(r0, TM), LANES:2 * LANES] = jnp.ones((TM, LANES), BF16)
        return carry

    lax.fori_loop(0, s_len // TM, stage, 0)
    o_ref[0:TM, :] = jnp.zeros((TM, o_ref.shape[1]), F32)

    rows = group * SWA_BLOCK
    qi = lax.broadcasted_iota(jnp.int32, (rows, band), 0) % SWA_BLOCK
    ki = lax.broadcasted_iota(jnp.int32, (rows, band), 1)
    m0 = lax.broadcasted_iota(jnp.int32, (SWA_BLOCK, LANES), 1) < HEAD_DIM
    head_of_row = lax.broadcasted_iota(jnp.int32, (rows, 1), 0) // SWA_BLOCK

    def block(i, carry):
        start = jnp.clip((i - 1) * SWA_BLOCK, 0, seq - band)
        k0 = pl.multiple_of(TM + start, SWA_BLOCK)
        q0 = pl.multiple_of(TM + i * SWA_BLOCK, SWA_BLOCK)
        valid = jnp.abs(qi + (i * SWA_BLOCK - start) - ki) <= SWA_WINDOW
        for hh in range(2):
            kb = kd_ref[hh, pl.ds(k0, band), :]
            vb = vd_ref[hh, pl.ds(k0, band), :]
            kc = kd_ref[hh, 0:TM, :]
            vc = vd_ref[hh, 0:TM, :]
            parts = []
            for j in range(group // 2):
                pair = hh * (group // 2) + j
                qp = q_ref[pl.ds(q0, SWA_BLOCK), pair * LANES:(pair + 1) * LANES]
                parts += [jnp.where(m0, qp, 0.0), jnp.where(m0, 0.0, qp)]
            q = jnp.concatenate(parts, axis=0).astype(BF16)
            s = jnp.where(valid, _dot_nt(q, kb), NEG_INF)
            s_ctx = _dot_nt(q, kc)
            h0 = kp * heads_per_step + hh * group
            sink = jnp.full((rows, 1), sink_ref[h0], F32)
            for g in range(1, group):
                sink = jnp.where(head_of_row == g, sink_ref[h0 + g], sink)
            res = _softmax_pv([s, s_ctx], [vb, vc], extra=sink * LOG2E)
            for j in range(group // 2):
                pair = hh * (group // 2) + j
                r0 = 2 * j * SWA_BLOCK
                o_ref[pl.ds(q0, SWA_BLOCK), pair * LANES:(pair + 1) * LANES] = jnp.where(
                    m0, res[r0:r0 + SWA_BLOCK], res[r0 + SWA_BLOCK:r0 + 2 * SWA_BLOCK])
        return carry

    lax.fori_loop(0, nblk, block, 0, unroll=4)


def _swa(p, sink):
    b, s, _ = p.shape
    qw = SWA_HEADS * HEAD_DIM // 2
    kblk = SWA_HEADS * HEAD_DIM // LANES
    vblk = kblk + SWA_KV_HEADS * HEAD_DIM // LANES
    return pl.pallas_call(
        _swa_kernel,
        grid=(b, 2),
        in_specs=[
            pl.BlockSpec(memory_space=pltpu.SMEM),
            pl.BlockSpec((None, s, qw), lambda bi, kp: (bi, 0, kp)),
            pl.BlockSpec((None, s, LANES), lambda bi, kp: (bi, 0, kblk + kp)),
            pl.BlockSpec((None, s, LANES), lambda bi, kp: (bi, 0, vblk + kp)),
        ],
        out_specs=pl.BlockSpec((None, s, qw), lambda bi, kp: (bi, 0, kp)),
        out_shape=jax.ShapeDtypeStruct((b, s, SWA_HEADS * HEAD_DIM), F32),
        scratch_shapes=[pltpu.VMEM((2, s, LANES), BF16), pltpu.VMEM((2, s, 2 * LANES), BF16)],
        compiler_params=_cparams(("arbitrary", "arbitrary")),
        name="swa",
    )(sink, p, p, p)


def _outproj_kernel(*refs, n_mix, n_stream, t0):
    mix_refs = refs[:n_mix]
    x_refs = refs[n_mix + 1:n_mix + 1 + n_stream]
    w_ref = refs[n_mix]
    (g1_ref, gn_ref, sh_ref, sc_ref, wr_ref, br_ref,
     xo_ref, h_ref, ri_ref, cnt_ref, carry_ref) = refs[n_mix + 1 + n_stream:]
    first = (pl.program_id(0) == 0) & (pl.program_id(1) == 0)

    @pl.when(first)
    def _():
        carry_ref[...] = jnp.zeros_like(carry_ref)

    o = None
    off = 0
    for m_ref in mix_refs:
        kw = m_ref.shape[1]
        part = jnp.dot(m_ref[...].astype(BF16), w_ref[off:off + kw, :], preferred_element_type=F32)
        o = part if o is None else o + part
        off += kw
    xn = _stream_tile(x_refs, pl.program_id(1) + t0 > 0) + g1_ref[...] * o
    xo_ref[...] = xn
    h = _rms_mod(xn, gn_ref[...], sh_ref[...], sc_ref[...])
    h_ref[...] = h

    logits = _dot3_split(h, wr_ref[0], wr_ref[1]) + br_ref[...]
    tm = logits.shape[0]
    lane = lax.broadcasted_iota(jnp.int32, (tm, LANES), 1).astype(F32)
    big = 1e9
    gmask = lane < MOE_GROUPS
    mg = jnp.max(jnp.where(gmask, logits, -big), axis=-1, keepdims=True)
    sg = jnp.sum(jnp.where(gmask, jnp.exp(jnp.minimum(logits - mg, 0.0)), 0.0), axis=-1, keepdims=True)
    gw = 1.0 / sg
    gi = jnp.min(jnp.where(gmask & (logits == mg), lane, big), axis=-1, keepdims=True)
    lo = ROUTE_LANE0 + MOE_EPG * gi
    emask = (lane >= lo) & (lane < lo + MOE_EPG)
    l1 = jnp.max(jnp.where(emask, logits, -big), axis=-1, keepdims=True)
    i1 = jnp.min(jnp.where(emask & (logits == l1), lane, big), axis=-1, keepdims=True)
    emask2 = emask & (lane != i1)
    l2 = jnp.max(jnp.where(emask2, logits, -big), axis=-1, keepdims=True)
    i2 = jnp.min(jnp.where(emask2 & (logits == l2), lane, big), axis=-1, keepdims=True)
    e21 = jnp.exp(l2 - l1)
    w1 = gw / (1.0 + e21)
    w2 = gw * e21 / (1.0 + e21)

    oh = jnp.where((lane == i1) | (lane == i2), 1.0, 0.0)
    tri = (lax.broadcasted_iota(jnp.int32, (tm, tm), 0) > lax.broadcasted_iota(jnp.int32, (tm, tm), 1))
    cum = jnp.dot(jnp.where(tri, 1.0, 0.0).astype(BF16), oh.astype(BF16), preferred_element_type=F32) + carry_ref[...]
    r1 = jnp.sum(jnp.where(lane == i1, cum, 0.0), axis=-1, keepdims=True)
    r2 = jnp.sum(jnp.where(lane == i2, cum, 0.0), axis=-1, keepdims=True)
    carry_ref[...] = carry_ref[...] + jnp.sum(oh, axis=0, keepdims=True)
    cnt_ref[...] = carry_ref[...]

    ri = jnp.where(lane == 0, i1 - ROUTE_LANE0, 0.0)
    ri = jnp.where(lane == 1, i2 - ROUTE_LANE0, ri)
    ri = jnp.where(lane == 2, w1, ri)
    ri = jnp.where(lane == 3, w2, ri)
    ri = jnp.where(lane == 4, r1, ri)
    ri = jnp.where(lane == 5, r2, ri)
    ri_ref[...] = ri


def _outproj(mixes, w_bf16, xa, g1, gain2, sh2, sc2, wr, br, t0):
    b, s, d = _stream_shape(xa)
    nt = s // TM - t0
    so = nt * TM
    mod_idx = lambda bi, t: (jnp.where(t + t0 == 0, b, bi), 0, 0)
    tile = lambda wdt: pl.BlockSpec((None, TM, wdt), lambda bi, t: (bi, t + t0, 0))
    otile = lambda wdt: pl.BlockSpec((None, TM, wdt), lambda bi, t: (bi, t, 0))
    const = lambda shape: pl.BlockSpec(shape, lambda bi, t: (0,) * len(shape))
    x_specs, x_args = _stream_specs(xa, t0)
    in_specs = [tile(m.shape[2]) for m in mixes] + [const(w_bf16.shape)] + x_specs + [
        pl.BlockSpec((None, 1, d), mod_idx), const((1, d)),
        pl.BlockSpec((None, 1, d), mod_idx), pl.BlockSpec((None, 1, d), mod_idx),
        const((2, d, LANES)), const((1, LANES)),
    ]
    return pl.pallas_call(
        functools.partial(_outproj_kernel, n_mix=len(mixes), n_stream=len(x_args), t0=t0),
        grid=(b, nt),
        in_specs=in_specs,
        out_specs=[otile(d), otile(d), otile(LANES), const((1, LANES))],
        out_shape=[jax.ShapeDtypeStruct((b, so, d), F32), jax.ShapeDtypeStruct((b, so, d), F32),
                   jax.ShapeDtypeStruct((b, so, LANES), F32), jax.ShapeDtypeStruct((1, LANES), F32)],
        scratch_shapes=[pltpu.VMEM((1, LANES), F32)],
        compiler_params=_cparams(("arbitrary", "arbitrary")),
        name="outproj",
    )(*mixes, w_bf16, *x_args, g1, gain2, sh2, sc2, wr, br)


SUBLANES = 8


def _to_token_tiles(x):
    return x.reshape(x.shape[0], SUBLANES, x.shape[1] // SUBLANES)


def _from_token_tiles(x3):
    return x3.reshape(x3.shape[0], x3.shape[1] * x3.shape[2])


PAD_BITS = tuple(1 << k for k in reversed(range(MOE_ROWS.bit_length() - 1)))


def _dispatch_kernel(dest_ref, pads_ref, h_ref, xs_ref, hbuf, zbuf, sem, zsem):
    nt = pl.num_programs(1)
    step = pl.program_id(0) * nt + pl.program_id(1)
    last = pl.num_programs(0) * nt - 1
    slot = step % 2

    def zero_pads(wait):
        def expert(e, c):
            pos = pads_ref[0, e]
            n = pads_ref[1, e]
            for bit in PAD_BITS:
                take = (n & bit) != 0
                cp = pltpu.make_async_copy(zbuf.at[pl.ds(0, bit)], xs_ref.at[pl.ds(pos, bit)], zsem)

                @pl.when(take)
                def _():
                    cp.wait() if wait else cp.start()

                pos = pos + jnp.where(take, bit, 0)
            return c
        lax.fori_loop(0, MOE_EXPERTS, expert, 0)

        def tail(j, c):
            cp = pltpu.make_async_copy(
                zbuf, xs_ref.at[pl.ds(pads_ref[0, MOE_EXPERTS] + j * PAD_BITS[0], PAD_BITS[0])], zsem)
            cp.wait() if wait else cp.start()
            return c
        lax.fori_loop(0, pads_ref[1, MOE_EXPERTS], tail, 0)

    @pl.when(step == 0)
    def _():
        zbuf[...] = jnp.zeros_like(zbuf)
        zero_pads(wait=False)
    hbuf[slot] = _to_token_tiles(h_ref[...])

    def copy(sl, i, dst_row):
        return pltpu.make_async_copy(hbuf.at[sl, i], xs_ref.at[dst_row], sem.at[sl])

    for i in range(TM):
        copy(slot, i, dest_ref[step, i]).start()
        copy(slot, i, dest_ref[step, TM + i]).start()

    def drain(sl):
        def one(i, c):
            copy(sl, 0, 0).wait()
            return c
        lax.fori_loop(0, 2 * TM, one, 0, unroll=8)

    @pl.when(step > 0)
    def _():
        drain(1 - slot)

    @pl.when(step == last)
    def _():
        drain(slot)
        zero_pads(wait=True)


def _dispatch(dest, pads, h2, n_pad):
    b, s, d = h2.shape
    nt = s // TM
    tile = (SUBLANES, d // SUBLANES)
    return pl.pallas_call(
        _dispatch_kernel,
        grid_spec=pltpu.PrefetchScalarGridSpec(
            num_scalar_prefetch=2,
            grid=(b, nt),
            in_specs=[pl.BlockSpec((None, TM, d), lambda bi, t, dr, pd: (bi, t, 0))],
            out_specs=pl.BlockSpec(memory_space=pl.ANY),
            scratch_shapes=[pltpu.VMEM((2, TM) + tile, F32), pltpu.VMEM((PAD_BITS[0],) + tile, F32),
                            pltpu.SemaphoreType.DMA((2,)), pltpu.SemaphoreType.DMA],
        ),
        out_shape=jax.ShapeDtypeStruct((n_pad,) + tile, F32),
        compiler_params=_cparams(("arbitrary", "arbitrary")),
        name="dispatch",
    )(dest, pads, h2)


def _mlp_kernel(be_ref, nu_ref, x_ref, wgu_ref, wd_ref, y_ref, wgu_b, wd_b):
    i = pl.program_id(0)
    prev = be_ref[jnp.maximum(i - 1, 0)]
    used = i < nu_ref[0]

    @pl.when(used & ((i == 0) | (be_ref[i] != prev)))
    def _():
        wgu_b[...] = wgu_ref[...].astype(BF16)
        wd_b[...] = wd_ref[...].astype(BF16)

    @pl.when(used)
    def _():
        x = _from_token_tiles(x_ref[...])
        gu = jnp.dot(x.astype(BF16), wgu_b[...], preferred_element_type=F32)
        act = _silu(gu[:, :MOE_FF]) * gu[:, MOE_FF:]
        y_ref[...] = _to_token_tiles(jnp.dot(act.astype(BF16), wd_b[...], preferred_element_type=F32))

    @pl.when(jnp.logical_not(used))
    def _():
        y_ref[...] = jnp.zeros_like(y_ref)


def _expert_mlp(block_e, n_used, xs, w_gu, w_down, layer):
    n_pad, sub, dl = xs.shape
    d = sub * dl
    ff2 = w_gu.shape[-1]
    slots = pl.BlockSpec((MOE_ROWS, sub, dl), lambda i, be, nu: (i, 0, 0))
    used_slots = pl.BlockSpec((MOE_ROWS, sub, dl), lambda i, be, nu: (jnp.minimum(i, nu[0] - 1), 0, 0))
    return pl.pallas_call(
        _mlp_kernel,
        grid_spec=pltpu.PrefetchScalarGridSpec(
            num_scalar_prefetch=2,
            grid=(n_pad // MOE_ROWS,),
            in_specs=[
                used_slots,
                pl.BlockSpec((None, None, d, ff2), lambda i, be, nu: (layer, be[i], 0, 0)),
                pl.BlockSpec((None, None, ff2 // 2, d), lambda i, be, nu: (layer, be[i], 0, 0)),
            ],
            out_specs=slots,
            scratch_shapes=[pltpu.VMEM((d, ff2), BF16), pltpu.VMEM((ff2 // 2, d), BF16)],
        ),
        out_shape=jax.ShapeDtypeStruct(xs.shape, F32),
        compiler_params=_cparams(("arbitrary",)),
        name="expert_mlp",
    )(block_e, n_used, xs, w_gu, w_down)


def _combine_kernel(dest_ref, ys_ref, x_ref, ri_ref, g2_ref, fg_ref, o_ref, buf, sem, *, final):
    bi = pl.program_id(0)
    t = pl.program_id(1)
    nt = pl.num_programs(1)
    step = bi * nt + t
    total = pl.num_programs(0) * nt

    def copy(src_row, slot, k, i):
        return pltpu.make_async_copy(ys_ref.at[src_row], buf.at[slot, k, i], sem.at[slot])

    def issue(st, slot):
        for i in range(TM):
            copy(dest_ref[st, i], slot, 0, i).start()
            copy(dest_ref[st, TM + i], slot, 1, i).start()

    slot = step % 2

    @pl.when(step == 0)
    def _():
        issue(0, 0)

    @pl.when(step + 1 < total)
    def _():
        issue(step + 1, 1 - slot)

    def drain(i, c):
        copy(0, slot, 0, 0).wait()
        return c

    lax.fori_loop(0, 2 * TM, drain, 0, unroll=8)

    lane = lax.broadcasted_iota(jnp.int32, (TM, LANES), 1)
    ri = ri_ref[...]
    w1 = jnp.sum(jnp.where(lane == 2, ri, 0.0), axis=-1, keepdims=True)
    w2 = jnp.sum(jnp.where(lane == 3, ri, 0.0), axis=-1, keepdims=True)
    y = _from_token_tiles(buf[slot, 0]) * w1 + _from_token_tiles(buf[slot, 1]) * w2
    xn = x_ref[...] + g2_ref[...] * y
    if final:
        ms = jnp.mean(xn * xn, axis=-1, keepdims=True)
        xn = xn * lax.rsqrt(ms + NORM_EPS) * fg_ref[...]
    o_ref[...] = xn


def _combine(dest, ys, xa, rinfo, g2, final_g, has_ctx, final):
    b, s, d = xa.shape
    nt = s // TM
    mod_idx = lambda bi, t, dr: (jnp.where(t == 0, b, bi) if has_ctx else bi, 0, 0)
    tile = lambda wdt: pl.BlockSpec((None, TM, wdt), lambda bi, t, dr: (bi, t, 0))
    out_spec = tile(d)
    out_shape = jax.ShapeDtypeStruct((b, s, d), F32)
    return pl.pallas_call(
        functools.partial(_combine_kernel, final=final),
        grid_spec=pltpu.PrefetchScalarGridSpec(
            num_scalar_prefetch=1,
            grid=(b, nt),
            in_specs=[
                pl.BlockSpec(memory_space=pl.ANY),
                tile(d), tile(LANES),
                pl.BlockSpec((None, 1, d), mod_idx),
                pl.BlockSpec((1, d), lambda bi, t, dr: (0, 0)),
            ],
            out_specs=out_spec,
            scratch_shapes=[pltpu.VMEM((2, 2, TM, SUBLANES, d // SUBLANES), F32),
                            pltpu.SemaphoreType.DMA((2,))],
        ),
        out_shape=out_shape,
        compiler_params=_cparams(("arbitrary", "arbitrary")),
        name="combine",
    )(dest, ys, xa, rinfo, g2, final_g)


def _moe(h2, rinfo, counts, w_gu, w_down, layer):
    b, s, d = h2.shape
    nt = s // TM
    r = rinfo[:, :, :8].reshape(b * nt, TM, 8)
    lanes = lambda i: jnp.concatenate([r[..., i], r[..., i + 1]], axis=1).astype(jnp.int32)
    e = lanes(0)
    rank = lanes(4)
    cnt = counts[0, ROUTE_LANE0:ROUTE_LANE0 + MOE_EXPERTS].astype(jnp.int32)
    padded = (cnt + MOE_ROWS - 1) // MOE_ROWS * MOE_ROWS
    ends = jnp.cumsum(padded)
    starts = ends - padded
    eids = jnp.arange(MOE_EXPERTS, dtype=jnp.int32)
    dest = jnp.sum(jnp.where(e[..., None] == eids, starts, 0), axis=-1) + rank
    n_assign = b * nt * TM * 2
    n_blocks = (n_assign + MOE_EXPERTS * (MOE_ROWS - 1)) // MOE_ROWS + 1
    n_pad = n_blocks * MOE_ROWS
    blk_row = jnp.arange(n_blocks, dtype=jnp.int32) * MOE_ROWS
    block_e = jnp.minimum(jnp.sum((blk_row[:, None] >= ends[None, :]).astype(jnp.int32), axis=1),
                          MOE_EXPERTS - 1)
    n_used = (ends[-1] // MOE_ROWS).astype(jnp.int32).reshape(1)
    pads = jnp.stack([jnp.append(starts + cnt, ends[-1]),
                      jnp.append(padded - cnt, (n_pad - ends[-1]) // PAD_BITS[0])]).astype(jnp.int32)
    xs = _dispatch(dest, pads, h2, n_pad)
    ys = _expert_mlp(block_e, n_used, xs, w_gu, w_down, layer)
    return dest, ys


def _router_weights(wg, bg, we, be):
    d = wg.shape[0]
    pad = LANES - MOE_GROUPS - MOE_EXPERTS
    assert ROUTE_LANE0 == MOE_GROUPS
    wr = jnp.concatenate([wg.astype(F32), we.astype(F32), jnp.zeros((d, pad), F32)], axis=1)
    br = jnp.concatenate([bg.astype(F32), be.astype(F32), jnp.zeros((pad,), F32)]).reshape(1, LANES)
    return jnp.stack(_split_bf16(wr)), br


def kernel(x, c, ctx, c_ctx, ada_w, ada_b, norm_g, final_g, ab_w_in, ab_w_out, ret_decay, ret_gn, na_rpb,
           swa_w_in, swa_w_out, swa_sink, router_g_w, router_g_b, router_e_w, router_e_b,
           expert_w_gu, expert_w_down):
    b, seq, d = x.shape
    assert ctx.shape[1] == TM and seq % TM == 0 and d == D_MODEL
    xa = (ctx, x)
    rope = _rope_tables(seq)

    cvec = jnp.concatenate([c, c_ctx[None, :], jnp.zeros((7, d), F32)], axis=0)
    mod = _adaln(cvec, ada_w, ada_b)
    mod = mod.reshape(DEPTH, b + 8, 6, 1, d)[:, :b + 1].transpose(0, 2, 1, 3, 4)

    for layer in range(DEPTH):
        last = layer == DEPTH - 1
        sh1, sc1, g1, sh2, sc2, g2 = (mod[layer, i] for i in range(6))
        gain1 = norm_g[layer, 0].reshape(1, d)
        gain2 = norm_g[layer, 1].reshape(1, d)
        j = layer // 2
        if layer % 2 == 0:
            nb = RET_W // LANES
            qk_scale = HEAD_DIM ** -0.5
            scales = {blk: qk_scale for blk in range(nb, 2 * nb)}
            scales.update({blk: qk_scale * LOG2E for blk in range(4 * nb, 5 * nb)})
            p = _proj(xa, gain1, sh1, sc1, ab_w_in[j].astype(BF16), rope,
                      rope_blocks=range(0, 2 * nb), scales=scales)
            log_gamma = jnp.log1p(-jnp.exp2(-ret_decay[j].astype(F32)))
            mixes = [_retention(p, log_gamma, ret_gn[j]), _neighbourhood(p, _na_bias(na_rpb[j]))]
            w_out = ab_w_out[j]
        else:
            qb = SWA_HEADS * HEAD_DIM // LANES
            kb = SWA_KV_HEADS * HEAD_DIM // LANES
            p = _proj(xa, gain1, sh1, sc1, swa_w_in[j].astype(BF16), rope,
                      rope_blocks=range(0, qb + kb),
                      scales={blk: HEAD_DIM ** -0.5 * LOG2E for blk in range(qb)})
            mixes = [_swa(p, swa_sink[j].astype(F32))]
            w_out = swa_w_out[j]
        t0 = 1 if last else 0
        wr, br = _router_weights(router_g_w[layer], router_g_b[layer], router_e_w[layer], router_e_b[layer])
        xa, h2, rinfo, counts = _outproj(mixes, w_out.astype(BF16), xa, g1, gain2, sh2, sc2, wr, br, t0)
        dest, ys = _moe(h2, rinfo, counts, expert_w_gu, expert_w_down, layer)
        xa = _combine(dest, ys, xa, rinfo, g2, final_g.reshape(1, d), not last, last)
    return xa
```

```python
import functools

import numpy as np
import jax
import jax.numpy as jnp
from jax import lax
from jax.experimental import pallas as pl
from jax.experimental.pallas import tpu as pltpu

F32 = jnp.float32
BF16 = jnp.bfloat16

D_MODEL = 1024
DEPTH = 2
GRID_W = 64
HEAD_DIM = 64
RET_HEADS = 8
NA_HEADS = 8
RET_W = 512
NA_W = 512
AB_IN = 4 * RET_W + 3 * NA_W
RET_CHUNK = 128
GN_EPS = 1e-5
NA_KH = 8
NA_KW = 16
SWA_HEADS = 16
SWA_KV_HEADS = 4
SWA_WINDOW = 128
SWA_BLOCK = 128
SWA_IN = (SWA_HEADS + 2 * SWA_KV_HEADS) * HEAD_DIM
ROPE_BASE = 10000.0
MOE_GROUPS = 4
MOE_EPG = 8
MOE_EXPERTS = 32
MOE_FF = 512
NORM_EPS = 1e-6
NEG_INF = -1e30

LANES = 128
TM = 256
RET_BLOCK = 256
MOE_ROWS = 512
ROUTE_LANE0 = 4
VMEM_LIMIT = 56 * 1024 * 1024


def _cparams(sem, vmem=VMEM_LIMIT, flags=None):
    return pltpu.CompilerParams(dimension_semantics=sem, vmem_limit_bytes=vmem, flags=flags)


def _split_bf16(a):
    hi = a.astype(BF16)
    lo = (a - hi.astype(F32)).astype(BF16)
    return hi, lo


def _dot3_split(a, bh, bl):
    ah, al = _split_bf16(a)
    d = lambda x, y: jnp.dot(x, y, preferred_element_type=F32)
    return d(ah, bh) + (d(ah, bl) + d(al, bh))


def _dot3(a, b):
    return _dot3_split(a, *_split_bf16(b))


def _dot_nt(a, b):
    return lax.dot_general(a, b, (((1,), (1,)), ((), ())), preferred_element_type=F32)


def _dot_tn(a, b):
    return lax.dot_general(a, b, (((0,), (0,)), ((), ())), preferred_element_type=F32)


def _silu(x):
    return x / (1.0 + jnp.exp(-x))


def _adaln_kernel(c_ref, w_ref, b_ref, o_ref):
    o_ref[...] = _dot3(_silu(c_ref[...]), w_ref[...]) + b_ref[...]


def _adaln(cvec, ada_w, ada_b):
    depth, d, n6 = ada_w.shape
    rows = cvec.shape[0]
    tn = 1024
    return pl.pallas_call(
        _adaln_kernel,
        grid=(depth, n6 // tn),
        in_specs=[
            pl.BlockSpec((rows, d), lambda l, j: (0, 0)),
            pl.BlockSpec((None, d, tn), lambda l, j: (l, 0, j)),
            pl.BlockSpec((None, 1, tn), lambda l, j: (l, 0, j)),
        ],
        out_specs=pl.BlockSpec((None, rows, tn), lambda l, j: (l, 0, j)),
        out_shape=jax.ShapeDtypeStruct((depth, rows, n6), F32),
        compiler_params=_cparams(("arbitrary", "arbitrary")),
        name="adaln",
    )(cvec, ada_w, ada_b.reshape(depth, 1, n6))


def _rms_mod(x, g, sh, sc):
    ms = jnp.mean(x * x, axis=-1, keepdims=True)
    return (x * lax.rsqrt(ms + NORM_EPS) * g) * (1.0 + sc) + sh


def _stream_tile(refs, is_lat):
    if len(refs) == 1:
        return refs[0][...]
    return jnp.where(is_lat, refs[1][...], refs[0][...])


def _stream_specs(stream, t0, extra_args=0):
    def im(f):
        return (lambda bi, t, *_: f(bi, t + t0))
    if not isinstance(stream, tuple):
        return [pl.BlockSpec((None, TM, stream.shape[2]), im(lambda bi, t: (bi, t, 0)))], [stream]
    ctx, x = stream
    d = x.shape[2]
    return ([pl.BlockSpec((None, TM, d), im(lambda bi, t: (bi, 0, 0))),
             pl.BlockSpec((None, TM, d), im(lambda bi, t: (bi, jnp.maximum(t - 1, 0), 0)))], [ctx, x])


def _stream_shape(stream):
    if not isinstance(stream, tuple):
        return stream.shape
    ctx, x = stream
    return (x.shape[0], ctx.shape[1] + x.shape[1], x.shape[2])


def _proj_kernel(*refs, rope_blocks, scales, cn, n_stream):
    x_refs = refs[:n_stream]
    g_ref, sh_ref, sc_ref, w_ref, rope_ref, o_ref = refs[n_stream:]
    is_lat = pl.program_id(1) > 0
    hb = _rms_mod(_stream_tile(x_refs, is_lat), g_ref[...], sh_ref[...], sc_ref[...]).astype(BF16)
    nout = w_ref.shape[1]
    for c in range(nout // cn):
        o = jnp.dot(hb, w_ref[:, c * cn:(c + 1) * cn], preferred_element_type=F32)
        for s in range(cn // LANES):
            blk = c * (cn // LANES) + s
            ob = o[:, s * LANES:(s + 1) * LANES]
            if blk in rope_blocks:
                r = (ob * rope_ref[0] + pltpu.roll(ob, 16, 1) * rope_ref[1]
                     + pltpu.roll(ob, LANES - 16, 1) * rope_ref[2])
                ob = jnp.where(is_lat, r, ob)
            if blk in scales:
                ob = ob * scales[blk]
            o_ref[:, blk * LANES:(blk + 1) * LANES] = ob.astype(o_ref.dtype)


def _proj(xa, gain, sh, sc, w_bf16, rope, rope_blocks, scales):
    b, s, d = _stream_shape(xa)
    nout = w_bf16.shape[1]
    nt = s // TM
    mod_idx = lambda bi, t: (jnp.where(t == 0, b, bi), 0, 0)
    x_specs, x_args = _stream_specs(xa, 0)
    kern = functools.partial(_proj_kernel, rope_blocks=frozenset(rope_blocks), scales=dict(scales), cn=512,
                             n_stream=len(x_args))
    return pl.pallas_call(
        kern,
        grid=(b, nt),
        in_specs=x_specs + [
            pl.BlockSpec((1, d), lambda bi, t: (0, 0)),
            pl.BlockSpec((None, 1, d), mod_idx),
            pl.BlockSpec((None, 1, d), mod_idx),
            pl.BlockSpec((d, nout), lambda bi, t: (0, 0)),
            pl.BlockSpec((3, TM, LANES), lambda bi, t: (0, jnp.maximum(t - 1, 0), 0)),
        ],
        out_specs=pl.BlockSpec((None, TM, nout), lambda bi, t: (bi, t, 0)),
        out_shape=jax.ShapeDtypeStruct((b, s, nout), BF16),
        compiler_params=_cparams(("arbitrary", "arbitrary")),
        name="proj",
    )(*x_args, gain, sh, sc, w_bf16, rope)


def _rope_tables(seq):
    nf = HEAD_DIM // 4
    inv = ROPE_BASE ** (-jnp.arange(nf, dtype=F32) / nf)
    t = jnp.arange(seq)
    row = (t // GRID_W).astype(F32)
    col = (t % GRID_W).astype(F32)
    lane = np.arange(LANES)
    jj = lane % HEAD_DIM
    axis_is_col = (jj // 32) == 1
    second_half = (jj % 32) >= 16
    f = jj % 16
    pos = jnp.where(axis_is_col[None, :], col[:, None], row[:, None])
    ang = pos * inv[f][None, :]
    c, s = jnp.cos(ang), jnp.sin(ang)
    sa = jnp.where(second_half[None, :], s, 0.0)
    sb = jnp.where(second_half[None, :], 0.0, -s)
    return jnp.stack([c, sa, sb], axis=0)


def _ret_kernel(lg_ref, q_ref, k_ref, v_ref, g_ref, gn_ref, o_ref, accf_ref, accb_ref,
                intra_ref, qd_ref, kd_ref):
    hp = pl.program_id(1)
    c = RET_BLOCK
    s_len = q_ref.shape[0]
    n_chunks = s_len // c
    ctx_chunks = TM // c
    pos = lax.broadcasted_iota(jnp.int32, (c, LANES), 0).astype(F32)
    m0 = lax.broadcasted_iota(jnp.int32, (c, LANES), 1) < HEAD_DIM
    same_head = ((lax.broadcasted_iota(jnp.int32, (LANES, LANES), 0) < HEAD_DIM)
                 == (lax.broadcasted_iota(jnp.int32, (LANES, LANES), 1) < HEAD_DIM))
    rel = (lax.broadcasted_iota(jnp.int32, (c, c), 0) - lax.broadcasted_iota(jnp.int32, (c, c), 1)).astype(F32)
    lgf = [lg_ref[0, hp * 2 + hh] for hh in range(2)]
    lgb = [lg_ref[1, hp * 2 + hh] for hh in range(2)]
    lgf_l = jnp.where(m0, lgf[0], lgf[1])
    lgb_l = jnp.where(m0, lgb[0], lgb[1])
    for hh in range(2):
        intra_ref[0, :, hh * c:(hh + 1) * c] = jnp.where(rel >= 0, jnp.exp(lgf[hh] * jnp.maximum(rel, 0.0)), 0.0)
        intra_ref[1, :, hh * c:(hh + 1) * c] = jnp.where(rel <= 0, jnp.exp(lgb[hh] * jnp.maximum(-rel, 0.0)), 0.0)
    qd_ref[0] = jnp.exp(lgf_l * (pos + 1.0))
    qd_ref[1] = jnp.exp(lgb_l * (c - pos))
    kd_ref[0] = jnp.exp(lgf_l * (c - 1.0 - pos))
    kd_ref[1] = jnp.exp(lgb_l * pos)
    cd = [jnp.exp(lgf_l[0:1] * float(c)), jnp.exp(lgb_l[0:1] * float(c))]

    def chunk(r0, state, d):
        q = q_ref[pl.ds(r0, c), :]
        k = k_ref[pl.ds(r0, c), :]
        v = v_ref[pl.ds(r0, c), :]
        qb = q.astype(BF16)
        kcat = jnp.concatenate([jnp.where(m0, k, 0.0), jnp.where(m0, 0.0, k)], axis=0).astype(BF16)
        vcat = jnp.concatenate([jnp.where(m0, v, 0.0), jnp.where(m0, 0.0, v)], axis=0).astype(BF16)
        s = _dot_nt(qb, kcat) * intra_ref[d]
        out = jnp.dot(s.astype(BF16), vcat, preferred_element_type=F32)
        out = out + jnp.dot(qb, state.astype(BF16), preferred_element_type=F32) * qd_ref[d]
        kv = _dot_tn((k * kd_ref[d]).astype(BF16), v.astype(BF16))
        return out, state * cd[d] + jnp.where(same_head, kv, 0.0)

    def body(i, states):
        sf, sb = states
        rf = pl.multiple_of(i * c, c)
        ib = jnp.where(i < ctx_chunks, ctx_chunks - 1 - i, n_chunks + ctx_chunks - 1 - i)
        rb = pl.multiple_of(ib * c, c)
        of, sf = chunk(rf, sf, 0)
        ob, sb = chunk(rb, sb, 1)
        accf_ref[pl.ds(rf, c), :] = of
        accb_ref[pl.ds(rb, c), :] = ob
        return sf, sb

    z = jnp.zeros((LANES, LANES), F32)
    lax.fori_loop(0, n_chunks, body, (z, z), unroll=True)

    avg = jnp.where(same_head, 1.0 / HEAD_DIM, 0.0).astype(BF16)

    def head_mean(x):
        hi, lo = _split_bf16(x)
        return (jnp.dot(hi, avg, preferred_element_type=F32) + jnp.dot(lo, avg, preferred_element_type=F32))

    def readout(i, carry):
        r0 = pl.multiple_of(i * c, c)
        o = accf_ref[pl.ds(r0, c), :] + accb_ref[pl.ds(r0, c), :]
        dlt = o - head_mean(o)
        var = head_mean(dlt * dlt)
        y = dlt * lax.rsqrt(var + GN_EPS) * gn_ref[...]
        o_ref[pl.ds(r0, c), :] = _silu(g_ref[pl.ds(r0, c), :].astype(F32)) * y
        return carry

    lax.fori_loop(0, n_chunks, readout, 0, unroll=2)


def _retention(p, log_gamma, ret_gn):
    b, s, _ = p.shape
    nb = RET_W // LANES
    blk = lambda off: pl.BlockSpec((None, s, LANES), lambda bi, hp: (bi, 0, off + hp))
    return pl.pallas_call(
        _ret_kernel,
        grid=(b, nb),
        in_specs=[
            pl.BlockSpec(memory_space=pltpu.SMEM),
            blk(0), blk(nb), blk(2 * nb), blk(3 * nb),
            pl.BlockSpec((1, LANES), lambda bi, hp: (0, hp)),
        ],
        out_specs=pl.BlockSpec((None, s, LANES), lambda bi, hp: (bi, 0, hp)),
        out_shape=jax.ShapeDtypeStruct((b, s, RET_W), F32),
        scratch_shapes=[pltpu.VMEM((s, LANES), F32), pltpu.VMEM((s, LANES), F32),
                        pltpu.VMEM((2, RET_BLOCK, 2 * RET_BLOCK), F32),
                        pltpu.VMEM((2, RET_BLOCK, LANES), F32), pltpu.VMEM((2, RET_BLOCK, LANES), F32)],
        compiler_params=_cparams(("arbitrary", "arbitrary")),
        name="retention",
    )(log_gamma, p, p, p, p, ret_gn.reshape(1, RET_W))


LOG2E = 1.4426950408889634


def _softmax_pv(s_list, v_list, extra=None):
    m = None
    for s in s_list:
        for j in range(s.shape[1] // LANES):
            blk = s[:, j * LANES:(j + 1) * LANES]
            m = blk if m is None else jnp.maximum(m, blk)
    m = m.max(axis=-1, keepdims=True)
    if extra is not None:
        m = jnp.maximum(m, extra)
    acc = None
    for s, v in zip(s_list, v_list):
        pv = jnp.dot(jnp.exp2(s - m).astype(BF16), v, preferred_element_type=F32)
        acc = pv if acc is None else acc + pv
    o, den = acc[:, :LANES], acc[:, LANES:]
    if extra is not None:
        den = den + jnp.exp2(extra - m)
    return o / den


def _stage_heads(q_ref, k_ref, v_ref, qm_ref, kb_ref, vb_ref):
    m0 = lax.broadcasted_iota(jnp.int32, (TM, LANES), 1) < HEAD_DIM

    def stage(i, carry):
        r0 = pl.multiple_of(i * TM, TM)
        q = q_ref[pl.ds(r0, TM), :]
        qm_ref[0, pl.ds(r0, TM), :] = jnp.where(m0, q, 0.0).astype(BF16)
        qm_ref[1, pl.ds(r0, TM), :] = jnp.where(m0, 0.0, q).astype(BF16)
        kb_ref[pl.ds(r0, TM), :] = k_ref[pl.ds(r0, TM), :].astype(BF16)
        vb_ref[pl.ds(r0, TM), 0:LANES] = v_ref[pl.ds(r0, TM), :].astype(BF16)
        vb_ref[pl.ds(r0, TM), LANES:2 * LANES] = jnp.ones((TM, LANES), BF16)
        return carry

    lax.fori_loop(0, q_ref.shape[0] // TM, stage, 0)


def _na_kernel(q_ref, k_ref, v_ref, bias_ref, o_ref, qm_ref, kb_ref, vb_ref):
    s_len = q_ref.shape[0]
    rows = (s_len - TM) // GRID_W
    nloc = NA_KH * GRID_W
    _stage_heads(q_ref, k_ref, v_ref, qm_ref, kb_ref, vb_ref)

    kc = kb_ref[0:TM, :]
    vc = vb_ref[0:TM, :]
    outs = [_softmax_pv([_dot_nt(qm_ref[hh, 0:TM, :], kc)], [vc]) for hh in range(2)]
    m0c = lax.broadcasted_iota(jnp.int32, (TM, LANES), 1) < HEAD_DIM
    o_ref[0:TM, :] = jnp.where(m0c, outs[0], outs[1])

    m0 = lax.broadcasted_iota(jnp.int32, (GRID_W, LANES), 1) < HEAD_DIM

    def row_block(r, carry):
        rs = jnp.clip(r - NA_KH // 2, 0, rows - NA_KH)
        pat = r - rs
        q0 = pl.multiple_of(TM + r * GRID_W, GRID_W)
        k0 = pl.multiple_of(TM + rs * GRID_W, GRID_W)
        kl = kb_ref[pl.ds(k0, nloc), :]
        vl = vb_ref[pl.ds(k0, nloc), :]
        kc = kb_ref[0:TM, :]
        vc = vb_ref[0:TM, :]
        q = jnp.concatenate([qm_ref[0, pl.ds(q0, GRID_W), :], qm_ref[1, pl.ds(q0, GRID_W), :]], axis=0)
        s_loc = _dot_nt(q, kl) + bias_ref[pat].reshape(2 * GRID_W, nloc)
        s_ctx = _dot_nt(q, kc)
        res = _softmax_pv([s_loc, s_ctx], [vl, vc])
        o_ref[pl.ds(q0, GRID_W), :] = jnp.where(m0, res[:GRID_W], res[GRID_W:])
        return carry

    lax.fori_loop(0, rows, row_block, 0, unroll=16)


def _na_bias(rpb):
    h = rpb.shape[0]
    qc = np.arange(GRID_W)[:, None]
    kc = np.arange(GRID_W)[None, :]
    win = np.clip(qc - NA_KW // 2, 0, GRID_W - NA_KW)
    valid = (kc >= win) & (kc < win + NA_KW)
    col_off = np.clip(kc - qc + NA_KW - 1, 0, 2 * NA_KW - 2)
    onehot = (col_off[None] == np.arange(2 * NA_KW - 1)[:, None, None]).astype(np.float32)
    cols = jnp.einsum("hrc,cqk->hrqk", rpb.astype(F32), onehot, precision=lax.Precision.HIGHEST)
    cols = jnp.where(valid[None, None], cols * LOG2E, NEG_INF)
    bias = jnp.stack([cols[:, NA_KH - 1 - p:2 * NA_KH - 1 - p] for p in range(NA_KH)])
    bias = bias.transpose(0, 1, 3, 2, 4)
    return bias.reshape(NA_KH, h, GRID_W, NA_KH * GRID_W)


def _neighbourhood(p, bias):
    b, s, _ = p.shape
    nb = NA_W // LANES
    c0 = 4 * RET_W // LANES
    blk = lambda off: pl.BlockSpec((None, s, LANES), lambda bi, hp: (bi, 0, c0 + off + hp))
    return pl.pallas_call(
        _na_kernel,
        grid=(b, nb),
        in_specs=[
            blk(0), blk(nb), blk(2 * nb),
            pl.BlockSpec((NA_KH, 2, GRID_W, NA_KH * GRID_W), lambda bi, hp: (0, hp, 0, 0)),
        ],
        out_specs=pl.BlockSpec((None, s, LANES), lambda bi, hp: (bi, 0, hp)),
        out_shape=jax.ShapeDtypeStruct((b, s, NA_W), F32),
        scratch_shapes=[pltpu.VMEM((2, s, LANES), BF16), pltpu.VMEM((s, LANES), BF16),
                        pltpu.VMEM((s, 2 * LANES), BF16)],
        compiler_params=_cparams(("arbitrary", "arbitrary")),
        name="neighbourhood",
    )(p, p, p, bias)


def _swa_kernel(sink_ref, q_ref, k_ref, v_ref, o_ref, kd_ref, vd_ref):
    kp = pl.program_id(1)
    s_len = q_ref.shape[0]
    seq = s_len - TM
    nblk = seq // SWA_BLOCK
    band = SWA_BLOCK + 2 * SWA_WINDOW
    group = SWA_HEADS // SWA_KV_HEADS
    heads_per_step = 2 * group
    m0t = lax.broadcasted_iota(jnp.int32, (TM, LANES), 1) < HEAD_DIM

    def stage(i, carry):
        r0 = pl.multiple_of(i * TM, TM)
        for src, dst in ((k_ref, kd_ref), (v_ref, vd_ref)):
            x = src[pl.ds(r0, TM), :].astype(F32)
            xr = pltpu.roll(x, HEAD_DIM, 1)
            dst[0, pl.ds(r0, TM), 0:LANES] = jnp.where(m0t, x, xr).astype(BF16)
            dst[1, pl.ds(r0, TM), 0:LANES] = jnp.where(m0t, xr, x).astype(BF16)
        for h in range(2):
            vd_ref[h, pl.ds(r0, TM), LANES:2 * LANES] = jnp.ones((TM, LANES), BF16)
        return carry

    lax.fori_loop(0, s_len // TM, stage, 0)
    o_ref[0:TM, :] = jnp.zeros((TM, o_ref.shape[1]), F32)

    rows = group * SWA_BLOCK
    qi = lax.broadcasted_iota(jnp.int32, (rows, band), 0) % SWA_BLOCK
    ki = lax.broadcasted_iota(jnp.int32, (rows, band), 1)
    m0 = lax.broadcasted_iota(jnp.int32, (SWA_BLOCK, LANES), 1) < HEAD_DIM
    head_of_row = lax.broadcasted_iota(jnp.int32, (rows, 1), 0) // SWA_BLOCK

    def block(i, carry):
        start = jnp.clip((i - 1) * SWA_BLOCK, 0, seq - band)
        k0 = pl.multiple_of(TM + start, SWA_BLOCK)
        q0 = pl.multiple_of(TM + i * SWA_BLOCK, SWA_BLOCK)
        valid = jnp.abs(qi + (i * SWA_BLOCK - start) - ki) <= SWA_WINDOW
        for hh in range(2):
            kb = kd_ref[hh, pl.ds(k0, band), :]
            vb = vd_ref[hh, pl.ds(k0, band), :]
            kc = kd_ref[hh, 0:TM, :]
            vc = vd_ref[hh, 0:TM, :]
            parts = []
            for j in range(group // 2):
                pair = hh * (group // 2) + j
                qp = q_ref[pl.ds(q0, SWA_BLOCK), pair * LANES:(pair + 1) * LANES]
                parts += [jnp.where(m0, qp, 0.0), jnp.where(m0, 0.0, qp)]
            q = jnp.concatenate(parts, axis=0).astype(BF16)
            s = jnp.where(valid, _dot_nt(q, kb), NEG_INF)
            s_ctx = _dot_nt(q, kc)
            h0 = kp * heads_per_step + hh * group
            sink = jnp.full((rows, 1), sink_ref[h0], F32)
            for g in range(1, group):
                sink = jnp.where(head_of_row == g, sink_ref[h0 + g], sink)
            res = _softmax_pv([s, s_ctx], [vb, vc], extra=sink * LOG2E)
            for j in range(group // 2):
                pair = hh * (group // 2) + j
                r0 = 2 * j * SWA_BLOCK
                o_ref[pl.ds(q0, SWA_BLOCK), pair * LANES:(pair + 1) * LANES] = jnp.where(
                    m0, res[r0:r0 + SWA_BLOCK], res[r0 + SWA_BLOCK:r0 + 2 * SWA_BLOCK])
        return carry

    lax.fori_loop(0, nblk, block, 0, unroll=8)


def _swa(p, sink):
    b, s, _ = p.shape
    qw = SWA_HEADS * HEAD_DIM // 2
    kblk = SWA_HEADS * HEAD_DIM // LANES
    vblk = kblk + SWA_KV_HEADS * HEAD_DIM // LANES
    return pl.pallas_call(
        _swa_kernel,
        grid=(b, 2),
        in_specs=[
            pl.BlockSpec(memory_space=pltpu.SMEM),
            pl.BlockSpec((None, s, qw), lambda bi, kp: (bi, 0, kp)),
            pl.BlockSpec((None, s, LANES), lambda bi, kp: (bi, 0, kblk + kp)),
            pl.BlockSpec((None, s, LANES), lambda bi, kp: (bi, 0, vblk + kp)),
        ],
        out_specs=pl.BlockSpec((None, s, qw), lambda bi, kp: (bi, 0, kp)),
        out_shape=jax.ShapeDtypeStruct((b, s, SWA_HEADS * HEAD_DIM), F32),
        scratch_shapes=[pltpu.VMEM((2, s, LANES), BF16), pltpu.VMEM((2, s, 2 * LANES), BF16)],
        compiler_params=_cparams(("arbitrary", "arbitrary")),
        name="swa",
    )(sink, p, p, p)


def _outproj_kernel(*refs, n_mix, n_stream, t0):
    mix_refs = refs[:n_mix]
    x_refs = refs[n_mix + 1:n_mix + 1 + n_stream]
    w_ref = refs[n_mix]
    (g1_ref, gn_ref, sh_ref, sc_ref, wr_ref, br_ref,
     xo_ref, h_ref, ri_ref, cnt_ref, carry_ref) = refs[n_mix + 1 + n_stream:]
    first = (pl.program_id(0) == 0) & (pl.program_id(1) == 0)

    @pl.when(first)
    def _():
        carry_ref[...] = jnp.zeros_like(carry_ref)

    o = None
    off = 0
    for m_ref in mix_refs:
        kw = m_ref.shape[1]
        part = jnp.dot(m_ref[...].astype(BF16), w_ref[off:off + kw, :], preferred_element_type=F32)
        o = part if o is None else o + part
        off += kw
    xn = _stream_tile(x_refs, pl.program_id(1) + t0 > 0) + g1_ref[...] * o
    xo_ref[...] = xn
    h = _rms_mod(xn, gn_ref[...], sh_ref[...], sc_ref[...])
    h_ref[...] = h

    logits = _dot3_split(h, wr_ref[0], wr_ref[1]) + br_ref[...]
    tm = logits.shape[0]
    lane = lax.broadcasted_iota(jnp.int32, (tm, LANES), 1).astype(F32)
    big = 1e9
    gmask = lane < MOE_GROUPS
    mg = jnp.max(jnp.where(gmask, logits, -big), axis=-1, keepdims=True)
    sg = jnp.sum(jnp.where(gmask, jnp.exp(jnp.minimum(logits - mg, 0.0)), 0.0), axis=-1, keepdims=True)
    gw = 1.0 / sg
    gi = jnp.min(jnp.where(gmask & (logits == mg), lane, big), axis=-1, keepdims=True)
    lo = ROUTE_LANE0 + MOE_EPG * gi
    emask = (lane >= lo) & (lane < lo + MOE_EPG)
    l1 = jnp.max(jnp.where(emask, logits, -big), axis=-1, keepdims=True)
    i1 = jnp.min(jnp.where(emask & (logits == l1), lane, big), axis=-1, keepdims=True)
    emask2 = emask & (lane != i1)
    l2 = jnp.max(jnp.where(emask2, logits, -big), axis=-1, keepdims=True)
    i2 = jnp.min(jnp.where(emask2 & (logits == l2), lane, big), axis=-1, keepdims=True)
    e21 = jnp.exp(l2 - l1)
    w1 = gw / (1.0 + e21)
    w2 = gw * e21 / (1.0 + e21)

    oh = jnp.where((lane == i1) | (lane == i2), 1.0, 0.0)
    tri = (lax.broadcasted_iota(jnp.int32, (tm, tm), 0) > lax.broadcasted_iota(jnp.int32, (tm, tm), 1))
    cum = jnp.dot(jnp.where(tri, 1.0, 0.0).astype(BF16), oh.astype(BF16), preferred_element_type=F32) + carry_ref[...]
    r1 = jnp.sum(jnp.where(lane == i1, cum, 0.0), axis=-1, keepdims=True)
    r2 = jnp.sum(jnp.where(lane == i2, cum, 0.0), axis=-1, keepdims=True)
    carry_ref[...] = carry_ref[...] + jnp.sum(oh, axis=0, keepdims=True)
    cnt_ref[...] = carry_ref[...]

    ri = jnp.where(lane == 0, i1 - ROUTE_LANE0, 0.0)
    ri = jnp.where(lane == 1, i2 - ROUTE_LANE0, ri)
    ri = jnp.where(lane == 2, w1, ri)
    ri = jnp.where(lane == 3, w2, ri)
    ri = jnp.where(lane == 4, r1, ri)
    ri = jnp.where(lane == 5, r2, ri)
    ri_ref[...] = ri


def _outproj(mixes, w_bf16, xa, g1, gain2, sh2, sc2, wr, br, t0):
    b, s, d = _stream_shape(xa)
    nt = s // TM - t0
    so = nt * TM
    mod_idx = lambda bi, t: (jnp.where(t + t0 == 0, b, bi), 0, 0)
    tile = lambda wdt: pl.BlockSpec((None, TM, wdt), lambda bi, t: (bi, t + t0, 0))
    otile = lambda wdt: pl.BlockSpec((None, TM, wdt), lambda bi, t: (bi, t, 0))
    const = lambda shape: pl.BlockSpec(shape, lambda bi, t: (0,) * len(shape))
    x_specs, x_args = _stream_specs(xa, t0)
    in_specs = [tile(m.shape[2]) for m in mixes] + [const(w_bf16.shape)] + x_specs + [
        pl.BlockSpec((None, 1, d), mod_idx), const((1, d)),
        pl.BlockSpec((None, 1, d), mod_idx), pl.BlockSpec((None, 1, d), mod_idx),
        const((2, d, LANES)), const((1, LANES)),
    ]
    return pl.pallas_call(
        functools.partial(_outproj_kernel, n_mix=len(mixes), n_stream=len(x_args), t0=t0),
        grid=(b, nt),
        in_specs=in_specs,
        out_specs=[otile(d), otile(d), otile(LANES), const((1, LANES))],
        out_shape=[jax.ShapeDtypeStruct((b, so, d), F32), jax.ShapeDtypeStruct((b, so, d), F32),
                   jax.ShapeDtypeStruct((b, so, LANES), F32), jax.ShapeDtypeStruct((1, LANES), F32)],
        scratch_shapes=[pltpu.VMEM((1, LANES), F32)],
        compiler_params=_cparams(("arbitrary", "arbitrary")),
        name="outproj",
    )(*mixes, w_bf16, *x_args, g1, gain2, sh2, sc2, wr, br)


SUBLANES = 8


def _to_token_tiles(x):
    return x.reshape(x.shape[0], SUBLANES, x.shape[1] // SUBLANES)


def _from_token_tiles(x3):
    return x3.reshape(x3.shape[0], x3.shape[1] * x3.shape[2])


PAD_BITS = tuple(1 << k for k in reversed(range(MOE_ROWS.bit_length() - 1)))


def _dispatch_kernel(dest_ref, pads_ref, h_ref, xs_ref, hbuf, zbuf, sem, zsem):
    nt = pl.num_programs(1)
    step = pl.program_id(0) * nt + pl.program_id(1)
    last = pl.num_programs(0) * nt - 1
    slot = step % 2

    def zero_pads(wait):
        def expert(e, c):
            pos = pads_ref[0, e]
            n = pads_ref[1, e]
            for bit in PAD_BITS:
                take = (n & bit) != 0
                cp = pltpu.make_async_copy(zbuf.at[pl.ds(0, bit)], xs_ref.at[pl.ds(pos, bit)], zsem)

                @pl.when(take)
                def _():
                    cp.wait() if wait else cp.start()

                pos = pos + jnp.where(take, bit, 0)
            return c
        lax.fori_loop(0, MOE_EXPERTS, expert, 0)

        def tail(j, c):
            cp = pltpu.make_async_copy(
                zbuf, xs_ref.at[pl.ds(pads_ref[0, MOE_EXPERTS] + j * PAD_BITS[0], PAD_BITS[0])], zsem)
            cp.wait() if wait else cp.start()
            return c
        lax.fori_loop(0, pads_ref[1, MOE_EXPERTS], tail, 0)

    @pl.when(step == 0)
    def _():
        zbuf[...] = jnp.zeros_like(zbuf)
        zero_pads(wait=False)
    hbuf[slot] = _to_token_tiles(h_ref[...])

    def copy(sl, i, dst_row):
        return pltpu.make_async_copy(hbuf.at[sl, i], xs_ref.at[dst_row], sem.at[sl])

    for i in range(TM):
        copy(slot, i, dest_ref[step, i]).start()
        copy(slot, i, dest_ref[step, TM + i]).start()

    def drain(sl):
        def one(i, c):
            copy(sl, 0, 0).wait()
            return c
        lax.fori_loop(0, 2 * TM, one, 0, unroll=8)

    @pl.when(step > 0)
    def _():
        drain(1 - slot)

    @pl.when(step == last)
    def _():
        drain(slot)
        zero_pads(wait=True)


def _dispatch(dest, pads, h2, n_pad):
    b, s, d = h2.shape
    nt = s // TM
    tile = (SUBLANES, d // SUBLANES)
    return pl.pallas_call(
        _dispatch_kernel,
        grid_spec=pltpu.PrefetchScalarGridSpec(
            num_scalar_prefetch=2,
            grid=(b, nt),
            in_specs=[pl.BlockSpec((None, TM, d), lambda bi, t, dr, pd: (bi, t, 0))],
            out_specs=pl.BlockSpec(memory_space=pl.ANY),
            scratch_shapes=[pltpu.VMEM((2, TM) + tile, F32), pltpu.VMEM((PAD_BITS[0],) + tile, F32),
                            pltpu.SemaphoreType.DMA((2,)), pltpu.SemaphoreType.DMA],
        ),
        out_shape=jax.ShapeDtypeStruct((n_pad,) + tile, F32),
        compiler_params=_cparams(("arbitrary", "arbitrary")),
        name="dispatch",
    )(dest, pads, h2)


def _mlp_kernel(be_ref, nu_ref, x_ref, wgu_ref, wd_ref, y_ref, wgu_b, wd_b):
    i = pl.program_id(0)
    prev = be_ref[jnp.maximum(i - 1, 0)]
    used = i < nu_ref[0]

    @pl.when(used & ((i == 0) | (be_ref[i] != prev)))
    def _():
        wgu_b[...] = wgu_ref[...].astype(BF16)
        wd_b[...] = wd_ref[...].astype(BF16)

    @pl.when(used)
    def _():
        x = _from_token_tiles(x_ref[...])
        gu = jnp.dot(x.astype(BF16), wgu_b[...], preferred_element_type=F32)
        act = _silu(gu[:, :MOE_FF]) * gu[:, MOE_FF:]
        y_ref[...] = _to_token_tiles(jnp.dot(act.astype(BF16), wd_b[...], preferred_element_type=F32))

    @pl.when(jnp.logical_not(used))
    def _():
        y_ref[...] = jnp.zeros_like(y_ref)


def _expert_mlp(block_e, n_used, xs, w_gu, w_down, layer):
    n_pad, sub, dl = xs.shape
    d = sub * dl
    ff2 = w_gu.shape[-1]
    slots = pl.BlockSpec((MOE_ROWS, sub, dl), lambda i, be, nu: (i, 0, 0))
    used_slots = pl.BlockSpec((MOE_ROWS, sub, dl), lambda i, be, nu: (jnp.minimum(i, nu[0] - 1), 0, 0))
    return pl.pallas_call(
        _mlp_kernel,
        grid_spec=pltpu.PrefetchScalarGridSpec(
            num_scalar_prefetch=2,
            grid=(n_pad // MOE_ROWS,),
            in_specs=[
                used_slots,
                pl.BlockSpec((None, None, d, ff2), lambda i, be, nu: (layer, be[i], 0, 0)),
                pl.BlockSpec((None, None, ff2 // 2, d), lambda i, be, nu: (layer, be[i], 0, 0)),
            ],
            out_specs=slots,
            scratch_shapes=[pltpu.VMEM((d, ff2), BF16), pltpu.VMEM((ff2 // 2, d), BF16)],
        ),
        out_shape=jax.ShapeDtypeStruct(xs.shape, F32),
        compiler_params=_cparams(("arbitrary",)),
        name="expert_mlp",
    )(block_e, n_used, xs, w_gu, w_down)


def _combine_kernel(dest_ref, ys_ref, x_ref, ri_ref, g2_ref, fg_ref, o_ref, buf, sem, *, final):
    bi = pl.program_id(0)
    t = pl.program_id(1)
    nt = pl.num_programs(1)
    step = bi * nt + t
    total = pl.num_programs(0) * nt

    def copy(src_row, slot, k, i):
        return pltpu.make_async_copy(ys_ref.at[src_row], buf.at[slot, k, i], sem.at[slot])

    def issue(st, slot):
        for i in range(TM):
            copy(dest_ref[st, i], slot, 0, i).start()
            copy(dest_ref[st, TM + i], slot, 1, i).start()

    slot = step % 2

    @pl.when(step == 0)
    def _():
        issue(0, 0)

    @pl.when(step + 1 < total)
    def _():
        issue(step + 1, 1 - slot)

    def drain(i, c):
        copy(0, slot, 0, 0).wait()
        return c

    lax.fori_loop(0, 2 * TM, drain, 0, unroll=8)

    lane = lax.broadcasted_iota(jnp.int32, (TM, LANES), 1)
    ri = ri_ref[...]
    w1 = jnp.sum(jnp.where(lane == 2, ri, 0.0), axis=-1, keepdims=True)
    w2 = jnp.sum(jnp.where(lane == 3, ri, 0.0), axis=-1, keepdims=True)
    y = _from_token_tiles(buf[slot, 0]) * w1 + _from_token_tiles(buf[slot, 1]) * w2
    xn = x_ref[...] + g2_ref[...] * y
    if final:
        ms = jnp.mean(xn * xn, axis=-1, keepdims=True)
        xn = xn * lax.rsqrt(ms + NORM_EPS) * fg_ref[...]
    o_ref[...] = xn


def _combine(dest, ys, xa, rinfo, g2, final_g, has_ctx, final):
    b, s, d = xa.shape
    nt = s // TM
    mod_idx = lambda bi, t, dr: (jnp.where(t == 0, b, bi) if has_ctx else bi, 0, 0)
    tile = lambda wdt: pl.BlockSpec((None, TM, wdt), lambda bi, t, dr: (bi, t, 0))
    out_spec = tile(d)
    out_shape = jax.ShapeDtypeStruct((b, s, d), F32)
    return pl.pallas_call(
        functools.partial(_combine_kernel, final=final),
        grid_spec=pltpu.PrefetchScalarGridSpec(
            num_scalar_prefetch=1,
            grid=(b, nt),
            in_specs=[
                pl.BlockSpec(memory_space=pl.ANY),
                tile(d), tile(LANES),
                pl.BlockSpec((None, 1, d), mod_idx),
                pl.BlockSpec((1, d), lambda bi, t, dr: (0, 0)),
            ],
            out_specs=out_spec,
            scratch_shapes=[pltpu.VMEM((2, 2, TM, SUBLANES, d // SUBLANES), F32),
                            pltpu.SemaphoreType.DMA((2,))],
        ),
        out_shape=out_shape,
        compiler_params=_cparams(("arbitrary", "arbitrary")),
        name="combine",
    )(dest, ys, xa, rinfo, g2, final_g)


def _moe(h2, rinfo, counts, w_gu, w_down, layer):
    b, s, d = h2.shape
    nt = s // TM
    r = rinfo[:, :, :8].reshape(b * nt, TM, 8)
    lanes = lambda i: jnp.concatenate([r[..., i], r[..., i + 1]], axis=1).astype(jnp.int32)
    e = lanes(0)
    rank = lanes(4)
    cnt = counts[0, ROUTE_LANE0:ROUTE_LANE0 + MOE_EXPERTS].astype(jnp.int32)
    padded = (cnt + MOE_ROWS - 1) // MOE_ROWS * MOE_ROWS
    ends = jnp.cumsum(padded)
    starts = ends - padded
    eids = jnp.arange(MOE_EXPERTS, dtype=jnp.int32)
    dest = jnp.sum(jnp.where(e[..., None] == eids, starts, 0), axis=-1) + rank
    n_assign = b * nt * TM * 2
    n_blocks = (n_assign + MOE_EXPERTS * (MOE_ROWS - 1)) // MOE_ROWS + 1
    n_pad = n_blocks * MOE_ROWS
    blk_row = jnp.arange(n_blocks, dtype=jnp.int32) * MOE_ROWS
    block_e = jnp.minimum(jnp.sum((blk_row[:, None] >= ends[None, :]).astype(jnp.int32), axis=1),
                          MOE_EXPERTS - 1)
    n_used = (ends[-1] // MOE_ROWS).astype(jnp.int32).reshape(1)
    pads = jnp.stack([jnp.append(starts + cnt, ends[-1]),
                      jnp.append(padded - cnt, (n_pad - ends[-1]) // PAD_BITS[0])]).astype(jnp.int32)
    xs = _dispatch(dest, pads, h2, n_pad)
    ys = _expert_mlp(block_e, n_used, xs, w_gu, w_down, layer)
    return dest, ys


def _router_weights(wg, bg, we, be):
    d = wg.shape[0]
    pad = LANES - MOE_GROUPS - MOE_EXPERTS
    assert ROUTE_LANE0 == MOE_GROUPS
    wr = jnp.concatenate([wg.astype(F32), we.astype(F32), jnp.zeros((d, pad), F32)], axis=1)
    br = jnp.concatenate([bg.astype(F32), be.astype(F32), jnp.zeros((pad,), F32)]).reshape(1, LANES)
    return jnp.stack(_split_bf16(wr)), br


def kernel(x, c, ctx, c_ctx, ada_w, ada_b, norm_g, final_g, ab_w_in, ab_w_out, ret_decay, ret_gn, na_rpb,
           swa_w_in, swa_w_out, swa_sink, router_g_w, router_g_b, router_e_w, router_e_b,
           expert_w_gu, expert_w_down):
    b, seq, d = x.shape
    assert ctx.shape[1] == TM and seq % TM == 0 and d == D_MODEL
    xa = (ctx, x)
    rope = _rope_tables(seq)

    cvec = jnp.concatenate([c, c_ctx[None, :], jnp.zeros((7, d), F32)], axis=0)
    mod = _adaln(cvec, ada_w, ada_b)
    mod = mod.reshape(DEPTH, b + 8, 6, 1, d)[:, :b + 1].transpose(0, 2, 1, 3, 4)

    for layer in range(DEPTH):
        last = layer == DEPTH - 1
        sh1, sc1, g1, sh2, sc2, g2 = (mod[layer, i] for i in range(6))
        gain1 = norm_g[layer, 0].reshape(1, d)
        gain2 = norm_g[layer, 1].reshape(1, d)
        j = layer // 2
        if layer % 2 == 0:
            nb = RET_W // LANES
            qk_scale = HEAD_DIM ** -0.5
            scales = {blk: qk_scale for blk in range(nb, 2 * nb)}
            scales.update({blk: qk_scale * LOG2E for blk in range(4 * nb, 5 * nb)})
            p = _proj(xa, gain1, sh1, sc1, ab_w_in[j].astype(BF16), rope,
                      rope_blocks=range(0, 2 * nb), scales=scales)
            log_gamma = jnp.log1p(-jnp.exp2(-ret_decay[j].astype(F32)))
            mixes = [_retention(p, log_gamma, ret_gn[j]), _neighbourhood(p, _na_bias(na_rpb[j]))]
            w_out = ab_w_out[j]
        else:
            qb = SWA_HEADS * HEAD_DIM // LANES
            kb = SWA_KV_HEADS * HEAD_DIM // LANES
            p = _proj(xa, gain1, sh1, sc1, swa_w_in[j].astype(BF16), rope,
                      rope_blocks=range(0, qb + kb),
                      scales={blk: HEAD_DIM ** -0.5 * LOG2E for blk in range(qb)})
            mixes = [_swa(p, swa_sink[j].astype(F32))]
            w_out = swa_w_out[j]
        t0 = 1 if last else 0
        wr, br = _router_weights(router_g_w[layer], router_g_b[layer], router_e_w[layer], router_e_b[layer])
        xa, h2, rinfo, counts = _outproj(mixes, w_out.astype(BF16), xa, g1, gain2, sh2, sc2, wr, br, t0)
        dest, ys = _moe(h2, rinfo, counts, expert_w_gu, expert_w_down, layer)
        xa = _combine(dest, ys, xa, rinfo, g2, final_g.reshape(1, d), not last, last)
    return xa
```

```python
import functools

import numpy as np
import jax
import jax.numpy as jnp
from jax import lax
from jax.experimental import pallas as pl
from jax.experimental.pallas import tpu as pltpu

F32 = jnp.float32
BF16 = jnp.bfloat16

D_MODEL = 1024
DEPTH = 2
GRID_W = 64
HEAD_DIM = 64
RET_HEADS = 8
NA_HEADS = 8
RET_W = 512
NA_W = 512
AB_IN = 4 * RET_W + 3 * NA_W
RET_CHUNK = 128
GN_EPS = 1e-5
NA_KH = 8
NA_KW = 16
SWA_HEADS = 16
SWA_KV_HEADS = 4
SWA_WINDOW = 128
SWA_BLOCK = 128
SWA_IN = (SWA_HEADS + 2 * SWA_KV_HEADS) * HEAD_DIM
ROPE_BASE = 10000.0
MOE_GROUPS = 4
MOE_EPG = 8
MOE_EXPERTS = 32
MOE_FF = 512
NORM_EPS = 1e-6
NEG_INF = -1e30

LANES = 128
TM = 256
RET_BLOCK = 256
MOE_ROWS = 512
ROUTE_LANE0 = 4
VMEM_LIMIT = 56 * 1024 * 1024


def _cparams(sem, vmem=VMEM_LIMIT, flags=None):
    return pltpu.CompilerParams(dimension_semantics=sem, vmem_limit_bytes=vmem, flags=flags)


def _split_bf16(a):
    hi = a.astype(BF16)
    lo = (a - hi.astype(F32)).astype(BF16)
    return hi, lo


def _dot3_split(a, bh, bl):
    ah, al = _split_bf16(a)
    d = lambda x, y: jnp.dot(x, y, preferred_element_type=F32)
    return d(ah, bh) + (d(ah, bl) + d(al, bh))


def _dot3(a, b):
    return _dot3_split(a, *_split_bf16(b))


def _dot_nt(a, b):
    return lax.dot_general(a, b, (((1,), (1,)), ((), ())), preferred_element_type=F32)


def _dot_tn(a, b):
    return lax.dot_general(a, b, (((0,), (0,)), ((), ())), preferred_element_type=F32)


def _silu(x):
    return x / (1.0 + jnp.exp(-x))


def _adaln_kernel(c_ref, w_ref, b_ref, o_ref):
    o_ref[...] = _dot3(_silu(c_ref[...]), w_ref[...]) + b_ref[...]


def _adaln(cvec, ada_w, ada_b):
    depth, d, n6 = ada_w.shape
    rows = cvec.shape[0]
    tn = 1024
    return pl.pallas_call(
        _adaln_kernel,
        grid=(depth, n6 // tn),
        in_specs=[
            pl.BlockSpec((rows, d), lambda l, j: (0, 0)),
            pl.BlockSpec((None, d, tn), lambda l, j: (l, 0, j)),
            pl.BlockSpec((None, 1, tn), lambda l, j: (l, 0, j)),
        ],
        out_specs=pl.BlockSpec((None, rows, tn), lambda l, j: (l, 0, j)),
        out_shape=jax.ShapeDtypeStruct((depth, rows, n6), F32),
        compiler_params=_cparams(("arbitrary", "arbitrary")),
        name="adaln",
    )(cvec, ada_w, ada_b.reshape(depth, 1, n6))


def _rms_mod(x, g, sh, sc):
    ms = jnp.mean(x * x, axis=-1, keepdims=True)
    return (x * lax.rsqrt(ms + NORM_EPS) * g) * (1.0 + sc) + sh


def _stream_tile(refs, is_lat):
    if len(refs) == 1:
        return refs[0][...]
    return jnp.where(is_lat, refs[1][...], refs[0][...])


def _stream_specs(stream, t0, extra_args=0):
    def im(f):
        return (lambda bi, t, *_: f(bi, t + t0))
    if not isinstance(stream, tuple):
        return [pl.BlockSpec((None, TM, stream.shape[2]), im(lambda bi, t: (bi, t, 0)))], [stream]
    ctx, x = stream
    d = x.shape[2]
    return ([pl.BlockSpec((None, TM, d), im(lambda bi, t: (bi, 0, 0))),
             pl.BlockSpec((None, TM, d), im(lambda bi, t: (bi, jnp.maximum(t - 1, 0), 0)))], [ctx, x])


def _stream_shape(stream):
    if not isinstance(stream, tuple):
        return stream.shape
    ctx, x = stream
    return (x.shape[0], ctx.shape[1] + x.shape[1], x.shape[2])


def _proj_kernel(*refs, rope_blocks, scales, cn, n_stream):
    x_refs = refs[:n_stream]
    g_ref, sh_ref, sc_ref, w_ref, rope_ref, o_ref = refs[n_stream:]
    is_lat = pl.program_id(1) > 0
    hb = _rms_mod(_stream_tile(x_refs, is_lat), g_ref[...], sh_ref[...], sc_ref[...]).astype(BF16)
    nout = w_ref.shape[1]
    for c in range(nout // cn):
        o = jnp.dot(hb, w_ref[:, c * cn:(c + 1) * cn], preferred_element_type=F32)
        for s in range(cn // LANES):
            blk = c * (cn // LANES) + s
            ob = o[:, s * LANES:(s + 1) * LANES]
            if blk in rope_blocks:
                r = (ob * rope_ref[0] + pltpu.roll(ob, 16, 1) * rope_ref[1]
                     + pltpu.roll(ob, LANES - 16, 1) * rope_ref[2])
                ob = jnp.where(is_lat, r, ob)
            if blk in scales:
                ob = ob * scales[blk]
            o_ref[:, blk * LANES:(blk + 1) * LANES] = ob.astype(o_ref.dtype)


def _proj(xa, gain, sh, sc, w_bf16, rope, rope_blocks, scales):
    b, s, d = _stream_shape(xa)
    nout = w_bf16.shape[1]
    nt = s // TM
    mod_idx = lambda bi, t: (jnp.where(t == 0, b, bi), 0, 0)
    x_specs, x_args = _stream_specs(xa, 0)
    kern = functools.partial(_proj_kernel, rope_blocks=frozenset(rope_blocks), scales=dict(scales), cn=512,
                             n_stream=len(x_args))
    return pl.pallas_call(
        kern,
        grid=(b, nt),
        in_specs=x_specs + [
            pl.BlockSpec((1, d), lambda bi, t: (0, 0)),
            pl.BlockSpec((None, 1, d), mod_idx),
            pl.BlockSpec((None, 1, d), mod_idx),
            pl.BlockSpec((d, nout), lambda bi, t: (0, 0)),
            pl.BlockSpec((3, TM, LANES), lambda bi, t: (0, jnp.maximum(t - 1, 0), 0)),
        ],
        out_specs=pl.BlockSpec((None, TM, nout), lambda bi, t: (bi, t, 0)),
        out_shape=jax.ShapeDtypeStruct((b, s, nout), BF16),
        compiler_params=_cparams(("arbitrary", "arbitrary")),
        name="proj",
    )(*x_args, gain, sh, sc, w_bf16, rope)


def _rope_tables(seq):
    nf = HEAD_DIM // 4
    inv = ROPE_BASE ** (-jnp.arange(nf, dtype=F32) / nf)
    t = jnp.arange(seq)
    row = (t // GRID_W).astype(F32)
    col = (t % GRID_W).astype(F32)
    lane = np.arange(LANES)
    jj = lane % HEAD_DIM
    axis_is_col = (jj // 32) == 1
    second_half = (jj % 32) >= 16
    f = jj % 16
    pos = jnp.where(axis_is_col[None, :], col[:, None], row[:, None])
    ang = pos * inv[f][None, :]
    c, s = jnp.cos(ang), jnp.sin(ang)
    sa = jnp.where(second_half[None, :], s, 0.0)
    sb = jnp.where(second_half[None, :], 0.0, -s)
    return jnp.stack([c, sa, sb], axis=0)


def _ret_kernel(lg_ref, q_ref, k_ref, v_ref, g_ref, gn_ref, o_ref, accf_ref, accb_ref,
                intra_ref, qd_ref, kd_ref):
    hp = pl.program_id(1)
    c = RET_BLOCK
    s_len = q_ref.shape[0]
    n_chunks = s_len // c
    ctx_chunks = TM // c
    pos = lax.broadcasted_iota(jnp.int32, (c, LANES), 0).astype(F32)
    m0 = lax.broadcasted_iota(jnp.int32, (c, LANES), 1) < HEAD_DIM
    same_head = ((lax.broadcasted_iota(jnp.int32, (LANES, LANES), 0) < HEAD_DIM)
                 == (lax.broadcasted_iota(jnp.int32, (LANES, LANES), 1) < HEAD_DIM))
    rel = (lax.broadcasted_iota(jnp.int32, (c, c), 0) - lax.broadcasted_iota(jnp.int32, (c, c), 1)).astype(F32)
    lgf = [lg_ref[0, hp * 2 + hh] for hh in range(2)]
    lgb = [lg_ref[1, hp * 2 + hh] for hh in range(2)]
    lgf_l = jnp.where(m0, lgf[0], lgf[1])
    lgb_l = jnp.where(m0, lgb[0], lgb[1])
    for hh in range(2):
        intra_ref[0, :, hh * c:(hh + 1) * c] = jnp.where(rel >= 0, jnp.exp(lgf[hh] * jnp.maximum(rel, 0.0)), 0.0)
        intra_ref[1, :, hh * c:(hh + 1) * c] = jnp.where(rel <= 0, jnp.exp(lgb[hh] * jnp.maximum(-rel, 0.0)), 0.0)
    qd_ref[0] = jnp.exp(lgf_l * (pos + 1.0))
    qd_ref[1] = jnp.exp(lgb_l * (c - pos))
    kd_ref[0] = jnp.exp(lgf_l * (c - 1.0 - pos))
    kd_ref[1] = jnp.exp(lgb_l * pos)
    cd = [jnp.exp(lgf_l[0:1] * float(c)), jnp.exp(lgb_l[0:1] * float(c))]

    def chunk(r0, state, d):
        q = q_ref[pl.ds(r0, c), :]
        k = k_ref[pl.ds(r0, c), :]
        v = v_ref[pl.ds(r0, c), :]
        qb = q.astype(BF16)
        kcat = jnp.concatenate([jnp.where(m0, k, 0.0), jnp.where(m0, 0.0, k)], axis=0).astype(BF16)
        vcat = jnp.concatenate([jnp.where(m0, v, 0.0), jnp.where(m0, 0.0, v)], axis=0).astype(BF16)
        s = _dot_nt(qb, kcat) * intra_ref[d]
        out = jnp.dot(s.astype(BF16), vcat, preferred_element_type=F32)
        out = out + jnp.dot(qb, state.astype(BF16), preferred_element_type=F32) * qd_ref[d]
        kv = _dot_tn((k * kd_ref[d]).astype(BF16), v.astype(BF16))
        return out, state * cd[d] + jnp.where(same_head, kv, 0.0)

    def body(i, states):
        sf, sb = states
        rf = pl.multiple_of(i * c, c)
        ib = jnp.where(i < ctx_chunks, ctx_chunks - 1 - i, n_chunks + ctx_chunks - 1 - i)
        rb = pl.multiple_of(ib * c, c)
        of, sf = chunk(rf, sf, 0)
        ob, sb = chunk(rb, sb, 1)
        accf_ref[pl.ds(rf, c), :] = of
        accb_ref[pl.ds(rb, c), :] = ob
        return sf, sb

    z = jnp.zeros((LANES, LANES), F32)
    lax.fori_loop(0, n_chunks, body, (z, z), unroll=True)

    avg = jnp.where(same_head, 1.0 / HEAD_DIM, 0.0).astype(BF16)

    def head_mean(x):
        hi, lo = _split_bf16(x)
        return (jnp.dot(hi, avg, preferred_element_type=F32) + jnp.dot(lo, avg, preferred_element_type=F32))

    def readout(i, carry):
        r0 = pl.multiple_of(i * c, c)
        o = accf_ref[pl.ds(r0, c), :] + accb_ref[pl.ds(r0, c), :]
        dlt = o - head_mean(o)
        var = head_mean(dlt * dlt)
        y = dlt * lax.rsqrt(var + GN_EPS) * gn_ref[...]
        o_ref[pl.ds(r0, c), :] = _silu(g_ref[pl.ds(r0, c), :].astype(F32)) * y
        return carry

    lax.fori_loop(0, n_chunks, readout, 0, unroll=True)


def _retention(p, log_gamma, ret_gn):
    b, s, _ = p.shape
    nb = RET_W // LANES
    blk = lambda off: pl.BlockSpec((None, s, LANES), lambda bi, hp: (bi, 0, off + hp))
    return pl.pallas_call(
        _ret_kernel,
        grid=(b, nb),
        in_specs=[
            pl.BlockSpec(memory_space=pltpu.SMEM),
            blk(0), blk(nb), blk(2 * nb), blk(3 * nb),
            pl.BlockSpec((1, LANES), lambda bi, hp: (0, hp)),
        ],
        out_specs=pl.BlockSpec((None, s, LANES), lambda bi, hp: (bi, 0, hp)),
        out_shape=jax.ShapeDtypeStruct((b, s, RET_W), F32),
        scratch_shapes=[pltpu.VMEM((s, LANES), F32), pltpu.VMEM((s, LANES), F32),
                        pltpu.VMEM((2, RET_BLOCK, 2 * RET_BLOCK), F32),
                        pltpu.VMEM((2, RET_BLOCK, LANES), F32), pltpu.VMEM((2, RET_BLOCK, LANES), F32)],
        compiler_params=_cparams(("arbitrary", "arbitrary")),
        name="retention",
    )(log_gamma, p, p, p, p, ret_gn.reshape(1, RET_W))


LOG2E = 1.4426950408889634


def _softmax_pv(s_list, v_list, extra=None):
    m = None
    for s in s_list:
        for j in range(s.shape[1] // LANES):
            blk = s[:, j * LANES:(j + 1) * LANES]
            m = blk if m is None else jnp.maximum(m, blk)
    m = m.max(axis=-1, keepdims=True)
    if extra is not None:
        m = jnp.maximum(m, extra)
    acc = None
    for s, v in zip(s_list, v_list):
        pv = jnp.dot(jnp.exp2(s - m).astype(BF16), v, preferred_element_type=F32)
        acc = pv if acc is None else acc + pv
    o, den = acc[:, :LANES], acc[:, LANES:]
    if extra is not None:
        den = den + jnp.exp2(extra - m)
    return o / den


def _stage_heads(q_ref, k_ref, v_ref, qm_ref, kb_ref, vb_ref):
    m0 = lax.broadcasted_iota(jnp.int32, (TM, LANES), 1) < HEAD_DIM

    def stage(i, carry):
        r0 = pl.multiple_of(i * TM, TM)
        q = q_ref[pl.ds(r0, TM), :]
        qm_ref[0, pl.ds(r0, TM), :] = jnp.where(m0, q, 0.0).astype(BF16)
        qm_ref[1, pl.ds(r0, TM), :] = jnp.where(m0, 0.0, q).astype(BF16)
        kb_ref[pl.ds(r0, TM), :] = k_ref[pl.ds(r0, TM), :].astype(BF16)
        vb_ref[pl.ds(r0, TM), 0:LANES] = v_ref[pl.ds(r0, TM), :].astype(BF16)
        vb_ref[pl.ds(r0, TM), LANES:2 * LANES] = jnp.ones((TM, LANES), BF16)
        return carry

    lax.fori_loop(0, q_ref.shape[0] // TM, stage, 0)


def _na_kernel(q_ref, k_ref, v_ref, bias_ref, o_ref, qm_ref, kb_ref, vb_ref):
    s_len = q_ref.shape[0]
    rows = (s_len - TM) // GRID_W
    nloc = NA_KH * GRID_W
    _stage_heads(q_ref, k_ref, v_ref, qm_ref, kb_ref, vb_ref)

    kc = kb_ref[0:TM, :]
    vc = vb_ref[0:TM, :]
    outs = [_softmax_pv([_dot_nt(qm_ref[hh, 0:TM, :], kc)], [vc]) for hh in range(2)]
    m0c = lax.broadcasted_iota(jnp.int32, (TM, LANES), 1) < HEAD_DIM
    o_ref[0:TM, :] = jnp.where(m0c, outs[0], outs[1])

    m0 = lax.broadcasted_iota(jnp.int32, (GRID_W, LANES), 1) < HEAD_DIM

    def row_block(r, carry):
        rs = jnp.clip(r - NA_KH // 2, 0, rows - NA_KH)
        pat = r - rs
        q0 = pl.multiple_of(TM + r * GRID_W, GRID_W)
        k0 = pl.multiple_of(TM + rs * GRID_W, GRID_W)
        kl = kb_ref[pl.ds(k0, nloc), :]
        vl = vb_ref[pl.ds(k0, nloc), :]
        kc = kb_ref[0:TM, :]
        vc = vb_ref[0:TM, :]
        q = jnp.concatenate([qm_ref[0, pl.ds(q0, GRID_W), :], qm_ref[1, pl.ds(q0, GRID_W), :]], axis=0)
        s_loc = _dot_nt(q, kl) + bias_ref[pat].reshape(2 * GRID_W, nloc)
        s_ctx = _dot_nt(q, kc)
        res = _softmax_pv([s_loc, s_ctx], [vl, vc])
        o_ref[pl.ds(q0, GRID_W), :] = jnp.where(m0, res[:GRID_W], res[GRID_W:])
        return carry

    lax.fori_loop(0, rows, row_block, 0, unroll=True)


def _na_bias(rpb):
    h = rpb.shape[0]
    qc = np.arange(GRID_W)[:, None]
    kc = np.arange(GRID_W)[None, :]
    win = np.clip(qc - NA_KW // 2, 0, GRID_W - NA_KW)
    valid = (kc >= win) & (kc < win + NA_KW)
    col_off = np.clip(kc - qc + NA_KW - 1, 0, 2 * NA_KW - 2)
    onehot = (col_off[None] == np.arange(2 * NA_KW - 1)[:, None, None]).astype(np.float32)
    cols = jnp.einsum("hrc,cqk->hrqk", rpb.astype(F32), onehot, precision=lax.Precision.HIGHEST)
    cols = jnp.where(valid[None, None], cols * LOG2E, NEG_INF)
    bias = jnp.stack([cols[:, NA_KH - 1 - p:2 * NA_KH - 1 - p] for p in range(NA_KH)])
    bias = bias.transpose(0, 1, 3, 2, 4)
    return bias.reshape(NA_KH, h, GRID_W, NA_KH * GRID_W)


def _neighbourhood(p, bias):
    b, s, _ = p.shape
    nb = NA_W // LANES
    c0 = 4 * RET_W // LANES
    blk = lambda off: pl.BlockSpec((None, s, LANES), lambda bi, hp: (bi, 0, c0 + off + hp))
    return pl.pallas_call(
        _na_kernel,
        grid=(b, nb),
        in_specs=[
            blk(0), blk(nb), blk(2 * nb),
            pl.BlockSpec((NA_KH, 2, GRID_W, NA_KH * GRID_W), lambda bi, hp: (0, hp, 0, 0)),
        ],
        out_specs=pl.BlockSpec((None, s, LANES), lambda bi, hp: (bi, 0, hp)),
        out_shape=jax.ShapeDtypeStruct((b, s, NA_W), F32),
        scratch_shapes=[pltpu.VMEM((2, s, LANES), BF16), pltpu.VMEM((s, LANES), BF16),
                        pltpu.VMEM((s, 2 * LANES), BF16)],
        compiler_params=_cparams(("arbitrary", "arbitrary")),
        name="neighbourhood",
    )(p, p, p, bias)


def _swa_kernel(sink_ref, q_ref, k_ref, v_ref, o_ref, kd_ref, vd_ref):
    kp = pl.program_id(1)
    s_len = q_ref.shape[0]
    seq = s_len - TM
    nblk = seq // SWA_BLOCK
    band = SWA_BLOCK + 2 * SWA_WINDOW
    group = SWA_HEADS // SWA_KV_HEADS
    heads_per_step = 2 * group
    m0t = lax.broadcasted_iota(jnp.int32, (TM, LANES), 1) < HEAD_DIM

    def stage(i, carry):
        r0 = pl.multiple_of(i * TM, TM)
        for src, dst in ((k_ref, kd_ref), (v_ref, vd_ref)):
            x = src[pl.ds(r0, TM), :].astype(F32)
            xr = pltpu.roll(x, HEAD_DIM, 1)
            dst[0, pl.ds(r0, TM), 0:LANES] = jnp.where(m0t, x, xr).astype(BF16)
            dst[1, pl.ds(r0, TM), 0:LANES] = jnp.where(m0t, xr, x).astype(BF16)
        for h in range(2):
            vd_ref[h, pl.ds(r0, TM), LANES:2 * LANES] = jnp.ones((TM, LANES), BF16)
        return carry

    lax.fori_loop(0, s_len // TM, stage, 0)
    o_ref[0:TM, :] = jnp.zeros((TM, o_ref.shape[1]), F32)

    rows = group * SWA_BLOCK
    qi = lax.broadcasted_iota(jnp.int32, (rows, band), 0) % SWA_BLOCK
    ki = lax.broadcasted_iota(jnp.int32, (rows, band), 1)
    m0 = lax.broadcasted_iota(jnp.int32, (SWA_BLOCK, LANES), 1) < HEAD_DIM
    head_of_row = lax.broadcasted_iota(jnp.int32, (rows, 1), 0) // SWA_BLOCK

    def block(i, carry):
        start = jnp.clip((i - 1) * SWA_BLOCK, 0, seq - band)
        k0 = pl.multiple_of(TM + start, SWA_BLOCK)
        q0 = pl.multiple_of(TM + i * SWA_BLOCK, SWA_BLOCK)
        valid = jnp.abs(qi + (i * SWA_BLOCK - start) - ki) <= SWA_WINDOW
        for hh in range(2):
            kb = kd_ref[hh, pl.ds(k0, band), :]
            vb = vd_ref[hh, pl.ds(k0, band), :]
            kc = kd_ref[hh, 0:TM, :]
            vc = vd_ref[hh, 0:TM, :]
            parts = []
            for j in range(group // 2):
                pair = hh * (group // 2) + j
                qp = q_ref[pl.ds(q0, SWA_BLOCK), pair * LANES:(pair + 1) * LANES]
                parts += [jnp.where(m0, qp, 0.0), jnp.where(m0, 0.0, qp)]
            q = jnp.concatenate(parts, axis=0).astype(BF16)
            s = jnp.where(valid, _dot_nt(q, kb), NEG_INF)
            s_ctx = _dot_nt(q, kc)
            h0 = kp * heads_per_step + hh * group
            sink = jnp.full((rows, 1), sink_ref[h0], F32)
            for g in range(1, group):
                sink = jnp.where(head_of_row == g, sink_ref[h0 + g], sink)
            res = _softmax_pv([s, s_ctx], [vb, vc], extra=sink * LOG2E)
            for j in range(group // 2):
                pair = hh * (group // 2) + j
                r0 = 2 * j * SWA_BLOCK
                o_ref[pl.ds(q0, SWA_BLOCK), pair * LANES:(pair + 1) * LANES] = jnp.where(
                    m0, res[r0:r0 + SWA_BLOCK], res[r0 + SWA_BLOCK:r0 + 2 * SWA_BLOCK])
        return carry

    lax.fori_loop(0, nblk, block, 0, unroll=True)


def _swa(p, sink):
    b, s, _ = p.shape
    qw = SWA_HEADS * HEAD_DIM // 2
    kblk = SWA_HEADS * HEAD_DIM // LANES
    vblk = kblk + SWA_KV_HEADS * HEAD_DIM // LANES
    return pl.pallas_call(
        _swa_kernel,
        grid=(b, 2),
        in_specs=[
            pl.BlockSpec(memory_space=pltpu.SMEM),
            pl.BlockSpec((None, s, qw), lambda bi, kp: (bi, 0, kp)),
            pl.BlockSpec((None, s, LANES), lambda bi, kp: (bi, 0, kblk + kp)),
            pl.BlockSpec((None, s, LANES), lambda bi, kp: (bi, 0, vblk + kp)),
        ],
        out_specs=pl.BlockSpec((None, s, qw), lambda bi, kp: (bi, 0, kp)),
        out_shape=jax.ShapeDtypeStruct((b, s, SWA_HEADS * HEAD_DIM), F32),
        scratch_shapes=[pltpu.VMEM((2, s, LANES), BF16), pltpu.VMEM((2, s, 2 * LANES), BF16)],
        compiler_params=_cparams(("arbitrary", "arbitrary")),
        name="swa",
    )(sink, p, p, p)


def _outproj_kernel(*refs, n_mix, n_stream, t0):
    mix_refs = refs[:n_mix]
    x_refs = refs[n_mix + 1:n_mix + 1 + n_stream]
    w_ref = refs[n_mix]
    (g1_ref, gn_ref, sh_ref, sc_ref, wr_ref, br_ref,
     xo_ref, h_ref, ri_ref, cnt_ref, carry_ref) = refs[n_mix + 1 + n_stream:]
    first = (pl.program_id(0) == 0) & (pl.program_id(1) == 0)

    @pl.when(first)
    def _():
        carry_ref[...] = jnp.zeros_like(carry_ref)

    o = None
    off = 0
    for m_ref in mix_refs:
        kw = m_ref.shape[1]
        part = jnp.dot(m_ref[...].astype(BF16), w_ref[off:off + kw, :], preferred_element_type=F32)
        o = part if o is None else o + part
        off += kw
    xn = _stream_tile(x_refs, pl.program_id(1) + t0 > 0) + g1_ref[...] * o
    xo_ref[...] = xn
    h = _rms_mod(xn, gn_ref[...], sh_ref[...], sc_ref[...])
    h_ref[...] = h

    logits = _dot3_split(h, wr_ref[0], wr_ref[1]) + br_ref[...]
    tm = logits.shape[0]
    lane = lax.broadcasted_iota(jnp.int32, (tm, LANES), 1).astype(F32)
    big = 1e9
    gmask = lane < MOE_GROUPS
    mg = jnp.max(jnp.where(gmask, logits, -big), axis=-1, keepdims=True)
    sg = jnp.sum(jnp.where(gmask, jnp.exp(jnp.minimum(logits - mg, 0.0)), 0.0), axis=-1, keepdims=True)
    gw = 1.0 / sg
    gi = jnp.min(jnp.where(gmask & (logits == mg), lane, big), axis=-1, keepdims=True)
    lo = ROUTE_LANE0 + MOE_EPG * gi
    emask = (lane >= lo) & (lane < lo + MOE_EPG)
    l1 = jnp.max(jnp.where(emask, logits, -big), axis=-1, keepdims=True)
    i1 = jnp.min(jnp.where(emask & (logits == l1), lane, big), axis=-1, keepdims=True)
    emask2 = emask & (lane != i1)
    l2 = jnp.max(jnp.where(emask2, logits, -big), axis=-1, keepdims=True)
    i2 = jnp.min(jnp.where(emask2 & (logits == l2), lane, big), axis=-1, keepdims=True)
    e21 = jnp.exp(l2 - l1)
    w1 = gw / (1.0 + e21)
    w2 = gw * e21 / (1.0 + e21)

    oh = jnp.where((lane == i1) | (lane == i2), 1.0, 0.0)
    tri = (lax.broadcasted_iota(jnp.int32, (tm, tm), 0) > lax.broadcasted_iota(jnp.int32, (tm, tm), 1))
    cum = jnp.dot(jnp.where(tri, 1.0, 0.0).astype(BF16), oh.astype(BF16), preferred_element_type=F32) + carry_ref[...]
    r1 = jnp.sum(jnp.where(lane == i1, cum, 0.0), axis=-1, keepdims=True)
    r2 = jnp.sum(jnp.where(lane == i2, cum, 0.0), axis=-1, keepdims=True)
    carry_ref[...] = carry_ref[...] + jnp.sum(oh, axis=0, keepdims=True)
    cnt_ref[...] = carry_ref[...]

    ri = jnp.where(lane == 0, i1 - ROUTE_LANE0, 0.0)
    ri = jnp.where(lane == 1, i2 - ROUTE_LANE0, ri)
    ri = jnp.where(lane == 2, w1, ri)
    ri = jnp.where(lane == 3, w2, ri)
    ri = jnp.where(lane == 4, r1, ri)
    ri = jnp.where(lane == 5, r2, ri)
    ri_ref[...] = ri


def _outproj(mixes, w_bf16, xa, g1, gain2, sh2, sc2, wr, br, t0):
    b, s, d = _stream_shape(xa)
    nt = s // TM - t0
    so = nt * TM
    mod_idx = lambda bi, t: (jnp.where(t + t0 == 0, b, bi), 0, 0)
    tile = lambda wdt: pl.BlockSpec((None, TM, wdt), lambda bi, t: (bi, t + t0, 0))
    otile = lambda wdt: pl.BlockSpec((None, TM, wdt), lambda bi, t: (bi, t, 0))
    const = lambda shape: pl.BlockSpec(shape, lambda bi, t: (0,) * len(shape))
    x_specs, x_args = _stream_specs(xa, t0)
    in_specs = [tile(m.shape[2]) for m in mixes] + [const(w_bf16.shape)] + x_specs + [
        pl.BlockSpec((None, 1, d), mod_idx), const((1, d)),
        pl.BlockSpec((None, 1, d), mod_idx), pl.BlockSpec((None, 1, d), mod_idx),
        const((2, d, LANES)), const((1, LANES)),
    ]
    return pl.pallas_call(
        functools.partial(_outproj_kernel, n_mix=len(mixes), n_stream=len(x_args), t0=t0),
        grid=(b, nt),
        in_specs=in_specs,
        out_specs=[otile(d), otile(d), otile(LANES), const((1, LANES))],
        out_shape=[jax.ShapeDtypeStruct((b, so, d), F32), jax.ShapeDtypeStruct((b, so, d), F32),
                   jax.ShapeDtypeStruct((b, so, LANES), F32), jax.ShapeDtypeStruct((1, LANES), F32)],
        scratch_shapes=[pltpu.VMEM((1, LANES), F32)],
        compiler_params=_cparams(("arbitrary", "arbitrary")),
        name="outproj",
    )(*mixes, w_bf16, *x_args, g1, gain2, sh2, sc2, wr, br)


SUBLANES = 8


def _to_token_tiles(x):
    return x.reshape(x.shape[0], SUBLANES, x.shape[1] // SUBLANES)


def _from_token_tiles(x3):
    return x3.reshape(x3.shape[0], x3.shape[1] * x3.shape[2])


PAD_BITS = tuple(1 << k for k in reversed(range(MOE_ROWS.bit_length() - 1)))


def _dispatch_kernel(dest_ref, pads_ref, h_ref, xs_ref, hbuf, zbuf, sem, zsem):
    nt = pl.num_programs(1)
    step = pl.program_id(0) * nt + pl.program_id(1)
    last = pl.num_programs(0) * nt - 1
    slot = step % 2

    def zero_pads(wait):
        def expert(e, c):
            pos = pads_ref[0, e]
            n = pads_ref[1, e]
            for bit in PAD_BITS:
                take = (n & bit) != 0
                cp = pltpu.make_async_copy(zbuf.at[pl.ds(0, bit)], xs_ref.at[pl.ds(pos, bit)], zsem)

                @pl.when(take)
                def _():
                    cp.wait() if wait else cp.start()

                pos = pos + jnp.where(take, bit, 0)
            return c
        lax.fori_loop(0, MOE_EXPERTS, expert, 0)

        def tail(j, c):
            cp = pltpu.make_async_copy(
                zbuf, xs_ref.at[pl.ds(pads_ref[0, MOE_EXPERTS] + j * PAD_BITS[0], PAD_BITS[0])], zsem)
            cp.wait() if wait else cp.start()
            return c
        lax.fori_loop(0, pads_ref[1, MOE_EXPERTS], tail, 0)

    @pl.when(step == 0)
    def _():
        zbuf[...] = jnp.zeros_like(zbuf)
        zero_pads(wait=False)
    hbuf[slot] = _to_token_tiles(h_ref[...])

    def copy(sl, i, dst_row):
        return pltpu.make_async_copy(hbuf.at[sl, i], xs_ref.at[dst_row], sem.at[sl])

    for i in range(TM):
        copy(slot, i, dest_ref[step, i]).start()
        copy(slot, i, dest_ref[step, TM + i]).start()

    def drain(sl):
        def one(i, c):
            copy(sl, 0, 0).wait()
            return c
        lax.fori_loop(0, 2 * TM, one, 0, unroll=8)

    @pl.when(step > 0)
    def _():
        drain(1 - slot)

    @pl.when(step == last)
    def _():
        drain(slot)
        zero_pads(wait=True)


def _dispatch(dest, pads, h2, n_pad):
    b, s, d = h2.shape
    nt = s // TM
    tile = (SUBLANES, d // SUBLANES)
    return pl.pallas_call(
        _dispatch_kernel,
        grid_spec=pltpu.PrefetchScalarGridSpec(
            num_scalar_prefetch=2,
            grid=(b, nt),
            in_specs=[pl.BlockSpec((None, TM, d), lambda bi, t, dr, pd: (bi, t, 0))],
            out_specs=pl.BlockSpec(memory_space=pl.ANY),
            scratch_shapes=[pltpu.VMEM((2, TM) + tile, F32), pltpu.VMEM((PAD_BITS[0],) + tile, F32),
                            pltpu.SemaphoreType.DMA((2,)), pltpu.SemaphoreType.DMA],
        ),
        out_shape=jax.ShapeDtypeStruct((n_pad,) + tile, F32),
        compiler_params=_cparams(("arbitrary", "arbitrary")),
        name="dispatch",
    )(dest, pads, h2)


def _mlp_kernel(be_ref, nu_ref, x_ref, wgu_ref, wd_ref, y_ref, wgu_b, wd_b):
    i = pl.program_id(0)
    prev = be_ref[jnp.maximum(i - 1, 0)]
    used = i < nu_ref[0]

    @pl.when(used & ((i == 0) | (be_ref[i] != prev)))
    def _():
        wgu_b[...] = wgu_ref[...].astype(BF16)
        wd_b[...] = wd_ref[...].astype(BF16)

    @pl.when(used)
    def _():
        x = _from_token_tiles(x_ref[...])
        gu = jnp.dot(x.astype(BF16), wgu_b[...], preferred_element_type=F32)
        act = _silu(gu[:, :MOE_FF]) * gu[:, MOE_FF:]
        y_ref[...] = _to_token_tiles(jnp.dot(act.astype(BF16), wd_b[...], preferred_element_type=F32))

    @pl.when(jnp.logical_not(used))
    def _():
        y_ref[...] = jnp.zeros_like(y_ref)


def _expert_mlp(block_e, n_used, xs, w_gu, w_down, layer):
    n_pad, sub, dl = xs.shape
    d = sub * dl
    ff2 = w_gu.shape[-1]
    slots = pl.BlockSpec((MOE_ROWS, sub, dl), lambda i, be, nu: (i, 0, 0))
    used_slots = pl.BlockSpec((MOE_ROWS, sub, dl), lambda i, be, nu: (jnp.minimum(i, nu[0] - 1), 0, 0))
    return pl.pallas_call(
        _mlp_kernel,
        grid_spec=pltpu.PrefetchScalarGridSpec(
            num_scalar_prefetch=2,
            grid=(n_pad // MOE_ROWS,),
            in_specs=[
                used_slots,
                pl.BlockSpec((None, None, d, ff2), lambda i, be, nu: (layer, be[i], 0, 0)),
                pl.BlockSpec((None, None, ff2 // 2, d), lambda i, be, nu: (layer, be[i], 0, 0)),
            ],
            out_specs=slots,
            scratch_shapes=[pltpu.VMEM((d, ff2), BF16), pltpu.VMEM((ff2 // 2, d), BF16)],
        ),
        out_shape=jax.ShapeDtypeStruct(xs.shape, F32),
        compiler_params=_cparams(("arbitrary",)),
        name="expert_mlp",
    )(block_e, n_used, xs, w_gu, w_down)


def _combine_kernel(dest_ref, ys_ref, x_ref, ri_ref, g2_ref, fg_ref, o_ref, buf, sem, *, final):
    bi = pl.program_id(0)
    t = pl.program_id(1)
    nt = pl.num_programs(1)
    step = bi * nt + t
    total = pl.num_programs(0) * nt

    def copy(src_row, slot, k, i):
        return pltpu.make_async_copy(ys_ref.at[src_row], buf.at[slot, k, i], sem.at[slot])

    def issue(st, slot):
        for i in range(TM):
            copy(dest_ref[st, i], slot, 0, i).start()
            copy(dest_ref[st, TM + i], slot, 1, i).start()

    slot = step % 2

    @pl.when(step == 0)
    def _():
        issue(0, 0)

    @pl.when(step + 1 < total)
    def _():
        issue(step + 1, 1 - slot)

    def drain(i, c):
        copy(0, slot, 0, 0).wait()
        return c

    lax.fori_loop(0, 2 * TM, drain, 0, unroll=8)

    lane = lax.broadcasted_iota(jnp.int32, (TM, LANES), 1)
    ri = ri_ref[...]
    w1 = jnp.sum(jnp.where(lane == 2, ri, 0.0), axis=-1, keepdims=True)
    w2 = jnp.sum(jnp.where(lane == 3, ri, 0.0), axis=-1, keepdims=True)
    y = _from_token_tiles(buf[slot, 0]) * w1 + _from_token_tiles(buf[slot, 1]) * w2
    xn = x_ref[...] + g2_ref[...] * y
    if final:
        ms = jnp.mean(xn * xn, axis=-1, keepdims=True)
        xn = xn * lax.rsqrt(ms + NORM_EPS) * fg_ref[...]
    o_ref[...] = xn


def _combine(dest, ys, xa, rinfo, g2, final_g, has_ctx, final):
    b, s, d = xa.shape
    nt = s // TM
    mod_idx = lambda bi, t, dr: (jnp.where(t == 0, b, bi) if has_ctx else bi, 0, 0)
    tile = lambda wdt: pl.BlockSpec((None, TM, wdt), lambda bi, t, dr: (bi, t, 0))
    out_spec = tile(d)
    out_shape = jax.ShapeDtypeStruct((b, s, d), F32)
    return pl.pallas_call(
        functools.partial(_combine_kernel, final=final),
        grid_spec=pltpu.PrefetchScalarGridSpec(
            num_scalar_prefetch=1,
            grid=(b, nt),
            in_specs=[
                pl.BlockSpec(memory_space=pl.ANY),
                tile(d), tile(LANES),
                pl.BlockSpec((None, 1, d), mod_idx),
                pl.BlockSpec((1, d), lambda bi, t, dr: (0, 0)),
            ],
            out_specs=out_spec,
            scratch_shapes=[pltpu.VMEM((2, 2, TM, SUBLANES, d // SUBLANES), F32),
                            pltpu.SemaphoreType.DMA((2,))],
        ),
        out_shape=out_shape,
        compiler_params=_cparams(("arbitrary", "arbitrary")),
        name="combine",
    )(dest, ys, xa, rinfo, g2, final_g)


def _moe(h2, rinfo, counts, w_gu, w_down, layer):
    b, s, d = h2.shape
    nt = s // TM
    r = rinfo[:, :, :8].reshape(b * nt, TM, 8)
    lanes = lambda i: jnp.concatenate([r[..., i], r[..., i + 1]], axis=1).astype(jnp.int32)
    e = lanes(0)
    rank = lanes(4)
    cnt = counts[0, ROUTE_LANE0:ROUTE_LANE0 + MOE_EXPERTS].astype(jnp.int32)
    padded = (cnt + MOE_ROWS - 1) // MOE_ROWS * MOE_ROWS
    ends = jnp.cumsum(padded)
    starts = ends - padded
    eids = jnp.arange(MOE_EXPERTS, dtype=jnp.int32)
    dest = jnp.sum(jnp.where(e[..., None] == eids, starts, 0), axis=-1) + rank
    n_assign = b * nt * TM * 2
    n_blocks = (n_assign + MOE_EXPERTS * (MOE_ROWS - 1)) // MOE_ROWS + 1
    n_pad = n_blocks * MOE_ROWS
    blk_row = jnp.arange(n_blocks, dtype=jnp.int32) * MOE_ROWS
    block_e = jnp.minimum(jnp.sum((blk_row[:, None] >= ends[None, :]).astype(jnp.int32), axis=1),
                          MOE_EXPERTS - 1)
    n_used = (ends[-1] // MOE_ROWS).astype(jnp.int32).reshape(1)
    pads = jnp.stack([jnp.append(starts + cnt, ends[-1]),
                      jnp.append(padded - cnt, (n_pad - ends[-1]) // PAD_BITS[0])]).astype(jnp.int32)
    xs = _dispatch(dest, pads, h2, n_pad)
    ys = _expert_mlp(block_e, n_used, xs, w_gu, w_down, layer)
    return dest, ys


def _router_weights(wg, bg, we, be):
    d = wg.shape[0]
    pad = LANES - MOE_GROUPS - MOE_EXPERTS
    assert ROUTE_LANE0 == MOE_GROUPS
    wr = jnp.concatenate([wg.astype(F32), we.astype(F32), jnp.zeros((d, pad), F32)], axis=1)
    br = jnp.concatenate([bg.astype(F32), be.astype(F32), jnp.zeros((pad,), F32)]).reshape(1, LANES)
    return jnp.stack(_split_bf16(wr)), br


def kernel(x, c, ctx, c_ctx, ada_w, ada_b, norm_g, final_g, ab_w_in, ab_w_out, ret_decay, ret_gn, na_rpb,
           swa_w_in, swa_w_out, swa_sink, router_g_w, router_g_b, router_e_w, router_e_b,
           expert_w_gu, expert_w_down):
    b, seq, d = x.shape
    assert ctx.shape[1] == TM and seq % TM == 0 and d == D_MODEL
    xa = (ctx, x)
    rope = _rope_tables(seq)

    cvec = jnp.concatenate([c, c_ctx[None, :], jnp.zeros((7, d), F32)], axis=0)
    mod = _adaln(cvec, ada_w, ada_b)
    mod = mod.reshape(DEPTH, b + 8, 6, 1, d)[:, :b + 1].transpose(0, 2, 1, 3, 4)

    for layer in range(DEPTH):
        last = layer == DEPTH - 1
        sh1, sc1, g1, sh2, sc2, g2 = (mod[layer, i] for i in range(6))
        gain1 = norm_g[layer, 0].reshape(1, d)
        gain2 = norm_g[layer, 1].reshape(1, d)
        j = layer // 2
        if layer % 2 == 0:
            nb = RET_W // LANES
            qk_scale = HEAD_DIM ** -0.5
            scales = {blk: qk_scale for blk in range(nb, 2 * nb)}
            scales.update({blk: qk_scale * LOG2E for blk in range(4 * nb, 5 * nb)})
            p = _proj(xa, gain1, sh1, sc1, ab_w_in[j].astype(BF16), rope,
                      rope_blocks=range(0, 2 * nb), scales=scales)
            log_gamma = jnp.log1p(-jnp.exp2(-ret_decay[j].astype(F32)))
            mixes = [_retention(p, log_gamma, ret_gn[j]), _neighbourhood(p, _na_bias(na_rpb[j]))]
            w_out = ab_w_out[j]
        else:
            qb = SWA_HEADS * HEAD_DIM // LANES
            kb = SWA_KV_HEADS * HEAD_DIM // LANES
            p = _proj(xa, gain1, sh1, sc1, swa_w_in[j].astype(BF16), rope,
                      rope_blocks=range(0, qb + kb),
                      scales={blk: HEAD_DIM ** -0.5 * LOG2E for blk in range(qb)})
            mixes = [_swa(p, swa_sink[j].astype(F32))]
            w_out = swa_w_out[j]
        t0 = 1 if last else 0
        wr, br = _router_weights(router_g_w[layer], router_g_b[layer], router_e_w[layer], router_e_b[layer])
        xa, h2, rinfo, counts = _outproj(mixes, w_out.astype(BF16), xa, g1, gain2, sh2, sc2, wr, br, t0)
        dest, ys = _moe(h2, rinfo, counts, expert_w_gu, expert_w_down, layer)
        xa = _combine(dest, ys, xa, rinfo, g2, final_g.reshape(1, d), not last, last)
    return xa
```

```python
import functools

import numpy as np
import jax
import jax.numpy as jnp
from jax import lax
from jax.experimental import pallas as pl
from jax.experimental.pallas import tpu as pltpu

F32 = jnp.float32
BF16 = jnp.bfloat16

D_MODEL = 1024
DEPTH = 2
GRID_W = 64
HEAD_DIM = 64
RET_HEADS = 8
NA_HEADS = 8
RET_W = 512
NA_W = 512
AB_IN = 4 * RET_W + 3 * NA_W
RET_CHUNK = 128
GN_EPS = 1e-5
NA_KH = 8
NA_KW = 16
SWA_HEADS = 16
SWA_KV_HEADS = 4
SWA_WINDOW = 128
SWA_BLOCK = 128
SWA_IN = (SWA_HEADS + 2 * SWA_KV_HEADS) * HEAD_DIM
ROPE_BASE = 10000.0
MOE_GROUPS = 4
MOE_EPG = 8
MOE_EXPERTS = 32
MOE_FF = 512
NORM_EPS = 1e-6
NEG_INF = -1e30

LANES = 128
TM = 256
RET_BLOCK = 256
MOE_ROWS = 512
ROUTE_LANE0 = 4
VMEM_LIMIT = 56 * 1024 * 1024


def _cparams(sem, vmem=VMEM_LIMIT, flags=None):
    return pltpu.CompilerParams(dimension_semantics=sem, vmem_limit_bytes=vmem, flags=flags)


def _split_bf16(a):
    hi = a.astype(BF16)
    lo = (a - hi.astype(F32)).astype(BF16)
    return hi, lo


def _dot3_split(a, bh, bl):
    ah, al = _split_bf16(a)
    d = lambda x, y: jnp.dot(x, y, preferred_element_type=F32)
    return d(ah, bh) + (d(ah, bl) + d(al, bh))


def _dot3(a, b):
    return _dot3_split(a, *_split_bf16(b))


def _dot_nt(a, b):
    return lax.dot_general(a, b, (((1,), (1,)), ((), ())), preferred_element_type=F32)


def _dot_tn(a, b):
    return lax.dot_general(a, b, (((0,), (0,)), ((), ())), preferred_element_type=F32)


def _silu(x):
    return x / (1.0 + jnp.exp(-x))


def _adaln_kernel(c_ref, w_ref, b_ref, o_ref):
    o_ref[...] = _dot3(_silu(c_ref[...]), w_ref[...]) + b_ref[...]


def _adaln(cvec, ada_w, ada_b):
    depth, d, n6 = ada_w.shape
    rows = cvec.shape[0]
    tn = 1024
    return pl.pallas_call(
        _adaln_kernel,
        grid=(depth, n6 // tn),
        in_specs=[
            pl.BlockSpec((rows, d), lambda l, j: (0, 0)),
            pl.BlockSpec((None, d, tn), lambda l, j: (l, 0, j)),
            pl.BlockSpec((None, 1, tn), lambda l, j: (l, 0, j)),
        ],
        out_specs=pl.BlockSpec((None, rows, tn), lambda l, j: (l, 0, j)),
        out_shape=jax.ShapeDtypeStruct((depth, rows, n6), F32),
        compiler_params=_cparams(("arbitrary", "arbitrary")),
        name="adaln",
    )(cvec, ada_w, ada_b.reshape(depth, 1, n6))


def _rms_mod(x, g, sh, sc):
    ms = jnp.mean(x * x, axis=-1, keepdims=True)
    return (x * lax.rsqrt(ms + NORM_EPS) * g) * (1.0 + sc) + sh


def _stream_tile(refs, is_lat):
    if len(refs) == 1:
        return refs[0][...]
    return jnp.where(is_lat, refs[1][...], refs[0][...])


def _stream_specs(stream, t0, extra_args=0):
    def im(f):
        return (lambda bi, t, *_: f(bi, t + t0))
    if not isinstance(stream, tuple):
        return [pl.BlockSpec((None, TM, stream.shape[2]), im(lambda bi, t: (bi, t, 0)))], [stream]
    ctx, x = stream
    d = x.shape[2]
    return ([pl.BlockSpec((None, TM, d), im(lambda bi, t: (bi, 0, 0))),
             pl.BlockSpec((None, TM, d), im(lambda bi, t: (bi, jnp.maximum(t - 1, 0), 0)))], [ctx, x])


def _stream_shape(stream):
    if not isinstance(stream, tuple):
        return stream.shape
    ctx, x = stream
    return (x.shape[0], ctx.shape[1] + x.shape[1], x.shape[2])


def _proj_kernel(*refs, rope_blocks, scales, cn, n_stream):
    x_refs = refs[:n_stream]
    g_ref, sh_ref, sc_ref, w_ref, rope_ref, o_ref = refs[n_stream:]
    is_lat = pl.program_id(1) > 0
    hb = _rms_mod(_stream_tile(x_refs, is_lat), g_ref[...], sh_ref[...], sc_ref[...]).astype(BF16)
    nout = w_ref.shape[1]
    for c in range(nout // cn):
        o = jnp.dot(hb, w_ref[:, c * cn:(c + 1) * cn], preferred_element_type=F32)
        for s in range(cn // LANES):
            blk = c * (cn // LANES) + s
            ob = o[:, s * LANES:(s + 1) * LANES]
            if blk in rope_blocks:
                r = (ob * rope_ref[0] + pltpu.roll(ob, 16, 1) * rope_ref[1]
                     + pltpu.roll(ob, LANES - 16, 1) * rope_ref[2])
                ob = jnp.where(is_lat, r, ob)
            if blk in scales:
                ob = ob * scales[blk]
            o_ref[:, blk * LANES:(blk + 1) * LANES] = ob.astype(o_ref.dtype)


def _proj(xa, gain, sh, sc, w_bf16, rope, rope_blocks, scales):
    b, s, d = _stream_shape(xa)
    nout = w_bf16.shape[1]
    nt = s // TM
    mod_idx = lambda bi, t: (jnp.where(t == 0, b, bi), 0, 0)
    x_specs, x_args = _stream_specs(xa, 0)
    kern = functools.partial(_proj_kernel, rope_blocks=frozenset(rope_blocks), scales=dict(scales), cn=512,
                             n_stream=len(x_args))
    return pl.pallas_call(
        kern,
        grid=(b, nt),
        in_specs=x_specs + [
            pl.BlockSpec((1, d), lambda bi, t: (0, 0)),
            pl.BlockSpec((None, 1, d), mod_idx),
            pl.BlockSpec((None, 1, d), mod_idx),
            pl.BlockSpec((d, nout), lambda bi, t: (0, 0)),
            pl.BlockSpec((3, TM, LANES), lambda bi, t: (0, jnp.maximum(t - 1, 0), 0)),
        ],
        out_specs=pl.BlockSpec((None, TM, nout), lambda bi, t: (bi, t, 0)),
        out_shape=jax.ShapeDtypeStruct((b, s, nout), BF16),
        compiler_params=_cparams(("arbitrary", "arbitrary")),
        name="proj",
    )(*x_args, gain, sh, sc, w_bf16, rope)


def _rope_tables(seq):
    nf = HEAD_DIM // 4
    inv = ROPE_BASE ** (-jnp.arange(nf, dtype=F32) / nf)
    t = jnp.arange(seq)
    row = (t // GRID_W).astype(F32)
    col = (t % GRID_W).astype(F32)
    lane = np.arange(LANES)
    jj = lane % HEAD_DIM
    axis_is_col = (jj // 32) == 1
    second_half = (jj % 32) >= 16
    f = jj % 16
    pos = jnp.where(axis_is_col[None, :], col[:, None], row[:, None])
    ang = pos * inv[f][None, :]
    c, s = jnp.cos(ang), jnp.sin(ang)
    sa = jnp.where(second_half[None, :], s, 0.0)
    sb = jnp.where(second_half[None, :], 0.0, -s)
    return jnp.stack([c, sa, sb], axis=0)


def _ret_kernel(lg_ref, q_ref, k_ref, v_ref, g_ref, gn_ref, o_ref, accf_ref, accb_ref,
                intra_ref, qd_ref, kd_ref):
    hp = pl.program_id(1)
    c = RET_BLOCK
    s_len = q_ref.shape[0]
    n_chunks = s_len // c
    ctx_chunks = TM // c
    pos = lax.broadcasted_iota(jnp.int32, (c, LANES), 0).astype(F32)
    m0 = lax.broadcasted_iota(jnp.int32, (c, LANES), 1) < HEAD_DIM
    same_head = ((lax.broadcasted_iota(jnp.int32, (LANES, LANES), 0) < HEAD_DIM)
                 == (lax.broadcasted_iota(jnp.int32, (LANES, LANES), 1) < HEAD_DIM))
    rel = (lax.broadcasted_iota(jnp.int32, (c, c), 0) - lax.broadcasted_iota(jnp.int32, (c, c), 1)).astype(F32)
    lgf = [lg_ref[0, hp * 2 + hh] for hh in range(2)]
    lgb = [lg_ref[1, hp * 2 + hh] for hh in range(2)]
    lgf_l = jnp.where(m0, lgf[0], lgf[1])
    lgb_l = jnp.where(m0, lgb[0], lgb[1])
    for hh in range(2):
        intra_ref[0, :, hh * c:(hh + 1) * c] = jnp.where(rel >= 0, jnp.exp(lgf[hh] * jnp.maximum(rel, 0.0)), 0.0)
        intra_ref[1, :, hh * c:(hh + 1) * c] = jnp.where(rel <= 0, jnp.exp(lgb[hh] * jnp.maximum(-rel, 0.0)), 0.0)
    qd_ref[0] = jnp.exp(lgf_l * (pos + 1.0))
    qd_ref[1] = jnp.exp(lgb_l * (c - pos))
    kd_ref[0] = jnp.exp(lgf_l * (c - 1.0 - pos))
    kd_ref[1] = jnp.exp(lgb_l * pos)
    cd = [jnp.exp(lgf_l[0:1] * float(c)), jnp.exp(lgb_l[0:1] * float(c))]

    def chunk(r0, state, d):
        q = q_ref[pl.ds(r0, c), :]
        k = k_ref[pl.ds(r0, c), :]
        v = v_ref[pl.ds(r0, c), :]
        qb = q.astype(BF16)
        kcat = jnp.concatenate([jnp.where(m0, k, 0.0), jnp.where(m0, 0.0, k)], axis=0).astype(BF16)
        vcat = jnp.concatenate([jnp.where(m0, v, 0.0), jnp.where(m0, 0.0, v)], axis=0).astype(BF16)
        s = _dot_nt(qb, kcat) * intra_ref[d]
        out = jnp.dot(s.astype(BF16), vcat, preferred_element_type=F32)
        out = out + jnp.dot(qb, state.astype(BF16), preferred_element_type=F32) * qd_ref[d]
        kv = _dot_tn((k * kd_ref[d]).astype(BF16), v.astype(BF16))
        return out, state * cd[d] + jnp.where(same_head, kv, 0.0)

    def body(i, states):
        sf, sb = states
        rf = pl.multiple_of(i * c, c)
        ib = jnp.where(i < ctx_chunks, ctx_chunks - 1 - i, n_chunks + ctx_chunks - 1 - i)
        rb = pl.multiple_of(ib * c, c)
        of, sf = chunk(rf, sf, 0)
        ob, sb = chunk(rb, sb, 1)
        accf_ref[pl.ds(rf, c), :] = of
        accb_ref[pl.ds(rb, c), :] = ob
        return sf, sb

    z = jnp.zeros((LANES, LANES), F32)
    lax.fori_loop(0, n_chunks, body, (z, z), unroll=True)

    avg = jnp.where(same_head, 1.0 / HEAD_DIM, 0.0).astype(BF16)

    def head_mean(x):
        hi, lo = _split_bf16(x)
        return (jnp.dot(hi, avg, preferred_element_type=F32) + jnp.dot(lo, avg, preferred_element_type=F32))

    def readout(i, carry):
        r0 = pl.multiple_of(i * c, c)
        o = accf_ref[pl.ds(r0, c), :] + accb_ref[pl.ds(r0, c), :]
        dlt = o - head_mean(o)
        var = head_mean(dlt * dlt)
        y = dlt * lax.rsqrt(var + GN_EPS) * gn_ref[...]
        o_ref[pl.ds(r0, c), :] = _silu(g_ref[pl.ds(r0, c), :].astype(F32)) * y
        return carry

    lax.fori_loop(0, n_chunks, readout, 0, unroll=True)


def _retention(p, log_gamma, ret_gn):
    b, s, _ = p.shape
    nb = RET_W // LANES
    blk = lambda off: pl.BlockSpec((None, s, LANES), lambda bi, hp: (bi, 0, off + hp))
    return pl.pallas_call(
        _ret_kernel,
        grid=(b, nb),
        in_specs=[
            pl.BlockSpec(memory_space=pltpu.SMEM),
            blk(0), blk(nb), blk(2 * nb), blk(3 * nb),
            pl.BlockSpec((1, LANES), lambda bi, hp: (0, hp)),
        ],
        out_specs=pl.BlockSpec((None, s, LANES), lambda bi, hp: (bi, 0, hp)),
        out_shape=jax.ShapeDtypeStruct((b, s, RET_W), F32),
        scratch_shapes=[pltpu.VMEM((s, LANES), F32), pltpu.VMEM((s, LANES), F32),
                        pltpu.VMEM((2, RET_BLOCK, 2 * RET_BLOCK), F32),
                        pltpu.VMEM((2, RET_BLOCK, LANES), F32), pltpu.VMEM((2, RET_BLOCK, LANES), F32)],
        compiler_params=_cparams(("arbitrary", "arbitrary")),
        name="retention",
    )(log_gamma, p, p, p, p, ret_gn.reshape(1, RET_W))


LOG2E = 1.4426950408889634


def _softmax_pv(s_list, v_list, extra=None):
    m = None
    for s in s_list:
        for j in range(s.shape[1] // LANES):
            blk = s[:, j * LANES:(j + 1) * LANES]
            m = blk if m is None else jnp.maximum(m, blk)
    m = m.max(axis=-1, keepdims=True)
    if extra is not None:
        m = jnp.maximum(m, extra)
    acc = None
    for s, v in zip(s_list, v_list):
        pv = jnp.dot(jnp.exp2(s - m).astype(BF16), v, preferred_element_type=F32)
        acc = pv if acc is None else acc + pv
    o, den = acc[:, :LANES], acc[:, LANES:]
    if extra is not None:
        den = den + jnp.exp2(extra - m)
    return o / den


def _stage_heads(q_ref, k_ref, v_ref, qm_ref, kb_ref, vb_ref):
    m0 = lax.broadcasted_iota(jnp.int32, (TM, LANES), 1) < HEAD_DIM

    def stage(i, carry):
        r0 = pl.multiple_of(i * TM, TM)
        q = q_ref[pl.ds(r0, TM), :]
        qm_ref[0, pl.ds(r0, TM), :] = jnp.where(m0, q, 0.0).astype(BF16)
        qm_ref[1, pl.ds(r0, TM), :] = jnp.where(m0, 0.0, q).astype(BF16)
        kb_ref[pl.ds(r0, TM), :] = k_ref[pl.ds(r0, TM), :].astype(BF16)
        vb_ref[pl.ds(r0, TM), 0:LANES] = v_ref[pl.ds(r0, TM), :].astype(BF16)
        vb_ref[pl.ds(r0, TM), LANES:2 * LANES] = jnp.ones((TM, LANES), BF16)
        return carry

    lax.fori_loop(0, q_ref.shape[0] // TM, stage, 0)


def _na_kernel(q_ref, k_ref, v_ref, bias_ref, o_ref, qm_ref, kb_ref, vb_ref):
    s_len = q_ref.shape[0]
    rows = (s_len - TM) // GRID_W
    nloc = NA_KH * GRID_W
    _stage_heads(q_ref, k_ref, v_ref, qm_ref, kb_ref, vb_ref)

    kc = kb_ref[0:TM, :]
    vc = vb_ref[0:TM, :]
    outs = [_softmax_pv([_dot_nt(qm_ref[hh, 0:TM, :], kc)], [vc]) for hh in range(2)]
    m0c = lax.broadcasted_iota(jnp.int32, (TM, LANES), 1) < HEAD_DIM
    o_ref[0:TM, :] = jnp.where(m0c, outs[0], outs[1])

    m0 = lax.broadcasted_iota(jnp.int32, (GRID_W, LANES), 1) < HEAD_DIM

    def row_block(r, carry):
        rs = jnp.clip(r - NA_KH // 2, 0, rows - NA_KH)
        pat = r - rs
        q0 = pl.multiple_of(TM + r * GRID_W, GRID_W)
        k0 = pl.multiple_of(TM + rs * GRID_W, GRID_W)
        kl = kb_ref[pl.ds(k0, nloc), :]
        vl = vb_ref[pl.ds(k0, nloc), :]
        kc = kb_ref[0:TM, :]
        vc = vb_ref[0:TM, :]
        q = jnp.concatenate([qm_ref[0, pl.ds(q0, GRID_W), :], qm_ref[1, pl.ds(q0, GRID_W), :]], axis=0)
        s_loc = _dot_nt(q, kl) + bias_ref[pat].reshape(2 * GRID_W, nloc)
        s_ctx = _dot_nt(q, kc)
        res = _softmax_pv([s_loc, s_ctx], [vl, vc])
        o_ref[pl.ds(q0, GRID_W), :] = jnp.where(m0, res[:GRID_W], res[GRID_W:])
        return carry

    lax.fori_loop(0, rows, row_block, 0, unroll=True)


def _na_bias(rpb):
    h = rpb.shape[0]
    qc = np.arange(GRID_W)[:, None]
    kc = np.arange(GRID_W)[None, :]
    win = np.clip(qc - NA_KW // 2, 0, GRID_W - NA_KW)
    valid = (kc >= win) & (kc < win + NA_KW)
    col_off = np.clip(kc - qc + NA_KW - 1, 0, 2 * NA_KW - 2)
    onehot = (col_off[None] == np.arange(2 * NA_KW - 1)[:, None, None]).astype(np.float32)
    cols = jnp.einsum("hrc,cqk->hrqk", rpb.astype(F32), onehot, precision=lax.Precision.HIGHEST)
    cols = jnp.where(valid[None, None], cols * LOG2E, NEG_INF)
    bias = jnp.stack([cols[:, NA_KH - 1 - p:2 * NA_KH - 1 - p] for p in range(NA_KH)])
    bias = bias.transpose(0, 1, 3, 2, 4)
    return bias.reshape(NA_KH, h, GRID_W, NA_KH * GRID_W)


def _neighbourhood(p, bias):
    b, s, _ = p.shape
    nb = NA_W // LANES
    c0 = 4 * RET_W // LANES
    blk = lambda off: pl.BlockSpec((None, s, LANES), lambda bi, hp: (bi, 0, c0 + off + hp))
    return pl.pallas_call(
        _na_kernel,
        grid=(b, nb),
        in_specs=[
            blk(0), blk(nb), blk(2 * nb),
            pl.BlockSpec((NA_KH, 2, GRID_W, NA_KH * GRID_W), lambda bi, hp: (0, hp, 0, 0)),
        ],
        out_specs=pl.BlockSpec((None, s, LANES), lambda bi, hp: (bi, 0, hp)),
        out_shape=jax.ShapeDtypeStruct((b, s, NA_W), F32),
        scratch_shapes=[pltpu.VMEM((2, s, LANES), BF16), pltpu.VMEM((s, LANES), BF16),
                        pltpu.VMEM((s, 2 * LANES), BF16)],
        compiler_params=_cparams(("arbitrary", "arbitrary")),
        name="neighbourhood",
    )(p, p, p, bias)


def _swa_kernel(sink_ref, q_ref, k_ref, v_ref, o_ref, kd_ref, vd_ref):
    kp = pl.program_id(1)
    s_len = q_ref.shape[0]
    seq = s_len - TM
    nblk = seq // SWA_BLOCK
    band = SWA_BLOCK + 2 * SWA_WINDOW
    group = SWA_HEADS // SWA_KV_HEADS
    heads_per_step = 2 * group
    m0t = lax.broadcasted_iota(jnp.int32, (TM, LANES), 1) < HEAD_DIM

    def stage(i, carry):
        r0 = pl.multiple_of(i * TM, TM)
        for src, dst in ((k_ref, kd_ref), (v_ref, vd_ref)):
            x = src[pl.ds(r0, TM), :].astype(F32)
            xr = pltpu.roll(x, HEAD_DIM, 1)
            dst[0, pl.ds(r0, TM), 0:LANES] = jnp.where(m0t, x, xr).astype(BF16)
            dst[1, pl.ds(r0, TM), 0:LANES] = jnp.where(m0t, xr, x).astype(BF16)
        for h in range(2):
            vd_ref[h, pl.ds(r0, TM), LANES:2 * LANES] = jnp.ones((TM, LANES), BF16)
        return carry

    lax.fori_loop(0, s_len // TM, stage, 0)
    o_ref[0:TM, :] = jnp.zeros((TM, o_ref.shape[1]), F32)

    rows = group * SWA_BLOCK
    qi = lax.broadcasted_iota(jnp.int32, (rows, band), 0) % SWA_BLOCK
    ki = lax.broadcasted_iota(jnp.int32, (rows, band), 1)
    m0 = lax.broadcasted_iota(jnp.int32, (SWA_BLOCK, LANES), 1) < HEAD_DIM
    head_of_row = lax.broadcasted_iota(jnp.int32, (rows, 1), 0) // SWA_BLOCK

    def block(i, carry):
        start = jnp.clip((i - 1) * SWA_BLOCK, 0, seq - band)
        k0 = pl.multiple_of(TM + start, SWA_BLOCK)
        q0 = pl.multiple_of(TM + i * SWA_BLOCK, SWA_BLOCK)
        valid = jnp.abs(qi + (i * SWA_BLOCK - start) - ki) <= SWA_WINDOW
        for hh in range(2):
            kb = kd_ref[hh, pl.ds(k0, band), :]
            vb = vd_ref[hh, pl.ds(k0, band), :]
            kc = kd_ref[hh, 0:TM, :]
            vc = vd_ref[hh, 0:TM, :]
            parts = []
            for j in range(group // 2):
                pair = hh * (group // 2) + j
                qp = q_ref[pl.ds(q0, SWA_BLOCK), pair * LANES:(pair + 1) * LANES]
                parts += [jnp.where(m0, qp, 0.0), jnp.where(m0, 0.0, qp)]
            q = jnp.concatenate(parts, axis=0).astype(BF16)
            s = jnp.where(valid, _dot_nt(q, kb), NEG_INF)
            s_ctx = _dot_nt(q, kc)
            h0 = kp * heads_per_step + hh * group
            sink = jnp.full((rows, 1), sink_ref[h0], F32)
            for g in range(1, group):
                sink = jnp.where(head_of_row == g, sink_ref[h0 + g], sink)
            res = _softmax_pv([s, s_ctx], [vb, vc], extra=sink * LOG2E)
            for j in range(group // 2):
                pair = hh * (group // 2) + j
                r0 = 2 * j * SWA_BLOCK
                o_ref[pl.ds(q0, SWA_BLOCK), pair * LANES:(pair + 1) * LANES] = jnp.where(
                    m0, res[r0:r0 + SWA_BLOCK], res[r0 + SWA_BLOCK:r0 + 2 * SWA_BLOCK])
        return carry

    lax.fori_loop(0, nblk, block, 0, unroll=True)


def _swa(p, sink):
    b, s, _ = p.shape
    qw = SWA_HEADS * HEAD_DIM // 2
    kblk = SWA_HEADS * HEAD_DIM // LANES
    vblk = kblk + SWA_KV_HEADS * HEAD_DIM // LANES
    return pl.pallas_call(
        _swa_kernel,
        grid=(b, 2),
        in_specs=[
            pl.BlockSpec(memory_space=pltpu.SMEM),
            pl.BlockSpec((None, s, qw), lambda bi, kp: (bi, 0, kp)),
            pl.BlockSpec((None, s, LANES), lambda bi, kp: (bi, 0, kblk + kp)),
            pl.BlockSpec((None, s, LANES), lambda bi, kp: (bi, 0, vblk + kp)),
        ],
        out_specs=pl.BlockSpec((None, s, qw), lambda bi, kp: (bi, 0, kp)),
        out_shape=jax.ShapeDtypeStruct((b, s, SWA_HEADS * HEAD_DIM), F32),
        scratch_shapes=[pltpu.VMEM((2, s, LANES), BF16), pltpu.VMEM((2, s, 2 * LANES), BF16)],
        compiler_params=_cparams(("arbitrary", "arbitrary")),
        name="swa",
    )(sink, p, p, p)


def _outproj_kernel(*refs, n_mix, n_stream, t0):
    mix_refs = refs[:n_mix]
    x_refs = refs[n_mix + 1:n_mix + 1 + n_stream]
    w_ref = refs[n_mix]
    (g1_ref, gn_ref, sh_ref, sc_ref, wr_ref, br_ref,
     xo_ref, h_ref, ri_ref, cnt_ref, carry_ref) = refs[n_mix + 1 + n_stream:]
    first = (pl.program_id(0) == 0) & (pl.program_id(1) == 0)

    @pl.when(first)
    def _():
        carry_ref[...] = jnp.zeros_like(carry_ref)

    o = None
    off = 0
    for m_ref in mix_refs:
        kw = m_ref.shape[1]
        part = jnp.dot(m_ref[...].astype(BF16), w_ref[off:off + kw, :], preferred_element_type=F32)
        o = part if o is None else o + part
        off += kw
    xn = _stream_tile(x_refs, pl.program_id(1) + t0 > 0) + g1_ref[...] * o
    xo_ref[...] = xn
    h = _rms_mod(xn, gn_ref[...], sh_ref[...], sc_ref[...])
    h_ref[...] = h

    logits = _dot3_split(h, wr_ref[0], wr_ref[1]) + br_ref[...]
    tm = logits.shape[0]
    lane = lax.broadcasted_iota(jnp.int32, (tm, LANES), 1).astype(F32)
    big = 1e9
    gmask = lane < MOE_GROUPS
    mg = jnp.max(jnp.where(gmask, logits, -big), axis=-1, keepdims=True)
    sg = jnp.sum(jnp.where(gmask, jnp.exp(jnp.minimum(logits - mg, 0.0)), 0.0), axis=-1, keepdims=True)
    gw = 1.0 / sg
    gi = jnp.min(jnp.where(gmask & (logits == mg), lane, big), axis=-1, keepdims=True)
    lo = ROUTE_LANE0 + MOE_EPG * gi
    emask = (lane >= lo) & (lane < lo + MOE_EPG)
    l1 = jnp.max(jnp.where(emask, logits, -big), axis=-1, keepdims=True)
    i1 = jnp.min(jnp.where(emask & (logits == l1), lane, big), axis=-1, keepdims=True)
    emask2 = emask & (lane != i1)
    l2 = jnp.max(jnp.where(emask2, logits, -big), axis=-1, keepdims=True)
    i2 = jnp.min(jnp.where(emask2 & (logits == l2), lane, big), axis=-1, keepdims=True)
    e21 = jnp.exp(l2 - l1)
    w1 = gw / (1.0 + e21)
    w2 = gw * e21 / (1.0 + e21)

    oh = jnp.where((lane == i1) | (lane == i2), 1.0, 0.0)
    tri = (lax.broadcasted_iota(jnp.int32, (tm, tm), 0) > lax.broadcasted_iota(jnp.int32, (tm, tm), 1))
    cum = jnp.dot(jnp.where(tri, 1.0, 0.0).astype(BF16), oh.astype(BF16), preferred_element_type=F32) + carry_ref[...]
    r1 = jnp.sum(jnp.where(lane == i1, cum, 0.0), axis=-1, keepdims=True)
    r2 = jnp.sum(jnp.where(lane == i2, cum, 0.0), axis=-1, keepdims=True)
    carry_ref[...] = carry_ref[...] + jnp.sum(oh, axis=0, keepdims=True)
    cnt_ref[...] = carry_ref[...]

    ri = jnp.where(lane == 0, i1 - ROUTE_LANE0, 0.0)
    ri = jnp.where(lane == 1, i2 - ROUTE_LANE0, ri)
    ri = jnp.where(lane == 2, w1, ri)
    ri = jnp.where(lane == 3, w2, ri)
    ri = jnp.where(lane == 4, r1, ri)
    ri = jnp.where(lane == 5, r2, ri)
    ri_ref[...] = ri


def _outproj(mixes, w_bf16, xa, g1, gain2, sh2, sc2, wr, br, t0):
    b, s, d = _stream_shape(xa)
    nt = s // TM - t0
    so = nt * TM
    mod_idx = lambda bi, t: (jnp.where(t + t0 == 0, b, bi), 0, 0)
    tile = lambda wdt: pl.BlockSpec((None, TM, wdt), lambda bi, t: (bi, t + t0, 0))
    otile = lambda wdt: pl.BlockSpec((None, TM, wdt), lambda bi, t: (bi, t, 0))
    const = lambda shape: pl.BlockSpec(shape, lambda bi, t: (0,) * len(shape))
    x_specs, x_args = _stream_specs(xa, t0)
    in_specs = [tile(m.shape[2]) for m in mixes] + [const(w_bf16.shape)] + x_specs + [
        pl.BlockSpec((None, 1, d), mod_idx), const((1, d)),
        pl.BlockSpec((None, 1, d), mod_idx), pl.BlockSpec((None, 1, d), mod_idx),
        const((2, d, LANES)), const((1, LANES)),
    ]
    return pl.pallas_call(
        functools.partial(_outproj_kernel, n_mix=len(mixes), n_stream=len(x_args), t0=t0),
        grid=(b, nt),
        in_specs=in_specs,
        out_specs=[otile(d), otile(d), otile(LANES), const((1, LANES))],
        out_shape=[jax.ShapeDtypeStruct((b, so, d), F32), jax.ShapeDtypeStruct((b, so, d), F32),
                   jax.ShapeDtypeStruct((b, so, LANES), F32), jax.ShapeDtypeStruct((1, LANES), F32)],
        scratch_shapes=[pltpu.VMEM((1, LANES), F32)],
        compiler_params=_cparams(("arbitrary", "arbitrary")),
        name="outproj",
    )(*mixes, w_bf16, *x_args, g1, gain2, sh2, sc2, wr, br)


SUBLANES = 8


def _to_token_tiles(x):
    return x.reshape(x.shape[0], SUBLANES, x.shape[1] // SUBLANES)


def _from_token_tiles(x3):
    return x3.reshape(x3.shape[0], x3.shape[1] * x3.shape[2])


PAD_BITS = tuple(1 << k for k in reversed(range(MOE_ROWS.bit_length() - 1)))


SLOT_EXPERT_BITS = 5


def _slot(route_ref, starts_ref, step, col):
    packed = route_ref[step, col]
    return starts_ref[packed & (MOE_EXPERTS - 1)] + (packed >> SLOT_EXPERT_BITS)


def _dispatch_kernel(route_ref, starts_ref, pads_ref, h_ref, xs_ref, hbuf, zbuf, sem, zsem):
    nt = pl.num_programs(1)
    step = pl.program_id(0) * nt + pl.program_id(1)
    last = pl.num_programs(0) * nt - 1
    slot = step % 2

    def zero_pads(wait):
        def expert(e, c):
            pos = pads_ref[0, e]
            n = pads_ref[1, e]
            for bit in PAD_BITS:
                take = (n & bit) != 0
                cp = pltpu.make_async_copy(zbuf.at[pl.ds(0, bit)], xs_ref.at[pl.ds(pos, bit)], zsem)

                @pl.when(take)
                def _():
                    cp.wait() if wait else cp.start()

                pos = pos + jnp.where(take, bit, 0)
            return c
        lax.fori_loop(0, MOE_EXPERTS, expert, 0)

        def tail(j, c):
            cp = pltpu.make_async_copy(
                zbuf, xs_ref.at[pl.ds(pads_ref[0, MOE_EXPERTS] + j * PAD_BITS[0], PAD_BITS[0])], zsem)
            cp.wait() if wait else cp.start()
            return c
        lax.fori_loop(0, pads_ref[1, MOE_EXPERTS], tail, 0)

    @pl.when(step == 0)
    def _():
        zbuf[...] = jnp.zeros_like(zbuf)
        zero_pads(wait=False)
    hbuf[slot] = _to_token_tiles(h_ref[...])

    def copy(sl, i, dst_row):
        return pltpu.make_async_copy(hbuf.at[sl, i], xs_ref.at[dst_row], sem.at[sl])

    for i in range(TM):
        copy(slot, i, _slot(route_ref, starts_ref, step, i)).start()
        copy(slot, i, _slot(route_ref, starts_ref, step, TM + i)).start()

    def drain(sl):
        def one(i, c):
            copy(sl, 0, 0).wait()
            return c
        lax.fori_loop(0, 2 * TM, one, 0, unroll=8)

    @pl.when(step > 0)
    def _():
        drain(1 - slot)

    @pl.when(step == last)
    def _():
        drain(slot)
        zero_pads(wait=True)


def _dispatch(route, starts, pads, h2, n_pad):
    b, s, d = h2.shape
    nt = s // TM
    tile = (SUBLANES, d // SUBLANES)
    return pl.pallas_call(
        _dispatch_kernel,
        grid_spec=pltpu.PrefetchScalarGridSpec(
            num_scalar_prefetch=3,
            grid=(b, nt),
            in_specs=[pl.BlockSpec((None, TM, d), lambda bi, t, *_: (bi, t, 0))],
            out_specs=pl.BlockSpec(memory_space=pl.ANY),
            scratch_shapes=[pltpu.VMEM((2, TM) + tile, F32), pltpu.VMEM((PAD_BITS[0],) + tile, F32),
                            pltpu.SemaphoreType.DMA((2,)), pltpu.SemaphoreType.DMA],
        ),
        out_shape=jax.ShapeDtypeStruct((n_pad,) + tile, F32),
        compiler_params=_cparams(("arbitrary", "arbitrary")),
        name="dispatch",
    )(route, starts, pads, h2)


def _mlp_kernel(be_ref, nu_ref, x_ref, wgu_ref, wd_ref, y_ref, wgu_b, wd_b):
    i = pl.program_id(0)
    prev = be_ref[jnp.maximum(i - 1, 0)]
    used = i < nu_ref[0]

    @pl.when(used & ((i == 0) | (be_ref[i] != prev)))
    def _():
        wgu_b[...] = wgu_ref[...].astype(BF16)
        wd_b[...] = wd_ref[...].astype(BF16)

    @pl.when(used)
    def _():
        x = _from_token_tiles(x_ref[...])
        gu = jnp.dot(x.astype(BF16), wgu_b[...], preferred_element_type=F32)
        act = _silu(gu[:, :MOE_FF]) * gu[:, MOE_FF:]
        y_ref[...] = _to_token_tiles(jnp.dot(act.astype(BF16), wd_b[...], preferred_element_type=F32))

    @pl.when(jnp.logical_not(used))
    def _():
        y_ref[...] = jnp.zeros_like(y_ref)


def _expert_mlp(block_e, n_used, xs, w_gu, w_down, layer):
    n_pad, sub, dl = xs.shape
    d = sub * dl
    ff2 = w_gu.shape[-1]
    slots = pl.BlockSpec((MOE_ROWS, sub, dl), lambda i, be, nu: (i, 0, 0))
    used_slots = pl.BlockSpec((MOE_ROWS, sub, dl), lambda i, be, nu: (jnp.minimum(i, nu[0] - 1), 0, 0))
    return pl.pallas_call(
        _mlp_kernel,
        grid_spec=pltpu.PrefetchScalarGridSpec(
            num_scalar_prefetch=2,
            grid=(n_pad // MOE_ROWS,),
            in_specs=[
                used_slots,
                pl.BlockSpec((None, None, d, ff2), lambda i, be, nu: (layer, be[i], 0, 0)),
                pl.BlockSpec((None, None, ff2 // 2, d), lambda i, be, nu: (layer, be[i], 0, 0)),
            ],
            out_specs=slots,
            scratch_shapes=[pltpu.VMEM((d, ff2), BF16), pltpu.VMEM((ff2 // 2, d), BF16)],
        ),
        out_shape=jax.ShapeDtypeStruct(xs.shape, F32),
        compiler_params=_cparams(("arbitrary",)),
        name="expert_mlp",
    )(block_e, n_used, xs, w_gu, w_down)


def _combine_kernel(route_ref, starts_ref, ys_ref, x_ref, ri_ref, g2_ref, fg_ref, o_ref, buf, sem, *, final):
    bi = pl.program_id(0)
    t = pl.program_id(1)
    nt = pl.num_programs(1)
    step = bi * nt + t
    total = pl.num_programs(0) * nt

    def copy(src_row, slot, k, i):
        return pltpu.make_async_copy(ys_ref.at[src_row], buf.at[slot, k, i], sem.at[slot])

    def issue(st, slot):
        for i in range(TM):
            copy(_slot(route_ref, starts_ref, st, i), slot, 0, i).start()
            copy(_slot(route_ref, starts_ref, st, TM + i), slot, 1, i).start()

    slot = step % 2

    @pl.when(step == 0)
    def _():
        issue(0, 0)

    @pl.when(step + 1 < total)
    def _():
        issue(step + 1, 1 - slot)

    def drain(i, c):
        copy(0, slot, 0, 0).wait()
        return c

    lax.fori_loop(0, 2 * TM, drain, 0, unroll=8)

    lane = lax.broadcasted_iota(jnp.int32, (TM, LANES), 1)
    ri = ri_ref[...]
    w1 = jnp.sum(jnp.where(lane == 2, ri, 0.0), axis=-1, keepdims=True)
    w2 = jnp.sum(jnp.where(lane == 3, ri, 0.0), axis=-1, keepdims=True)
    y = _from_token_tiles(buf[slot, 0]) * w1 + _from_token_tiles(buf[slot, 1]) * w2
    xn = x_ref[...] + g2_ref[...] * y
    if final:
        ms = jnp.mean(xn * xn, axis=-1, keepdims=True)
        xn = xn * lax.rsqrt(ms + NORM_EPS) * fg_ref[...]
    o_ref[...] = xn


def _combine(route, starts, ys, xa, rinfo, g2, final_g, has_ctx, final):
    b, s, d = xa.shape
    nt = s // TM
    mod_idx = lambda bi, t, *_: (jnp.where(t == 0, b, bi) if has_ctx else bi, 0, 0)
    tile = lambda wdt: pl.BlockSpec((None, TM, wdt), lambda bi, t, *_: (bi, t, 0))
    out_spec = tile(d)
    out_shape = jax.ShapeDtypeStruct((b, s, d), F32)
    return pl.pallas_call(
        functools.partial(_combine_kernel, final=final),
        grid_spec=pltpu.PrefetchScalarGridSpec(
            num_scalar_prefetch=2,
            grid=(b, nt),
            in_specs=[
                pl.BlockSpec(memory_space=pl.ANY),
                tile(d), tile(LANES),
                pl.BlockSpec((None, 1, d), mod_idx),
                pl.BlockSpec((1, d), lambda bi, t, *_: (0, 0)),
            ],
            out_specs=out_spec,
            scratch_shapes=[pltpu.VMEM((2, 2, TM, SUBLANES, d // SUBLANES), F32),
                            pltpu.SemaphoreType.DMA((2,))],
        ),
        out_shape=out_shape,
        compiler_params=_cparams(("arbitrary", "arbitrary")),
        name="combine",
    )(route, starts, ys, xa, rinfo, g2, final_g)


def _moe(h2, rinfo, counts, w_gu, w_down, layer):
    b, s, d = h2.shape
    nt = s // TM
    r = rinfo[:, :, :8].reshape(b * nt, TM, 8)
    lanes = lambda i: jnp.concatenate([r[..., i], r[..., i + 1]], axis=1).astype(jnp.int32)
    assert MOE_EXPERTS == 1 << SLOT_EXPERT_BITS
    route = lanes(4) * MOE_EXPERTS + lanes(0)
    cnt = counts[0, ROUTE_LANE0:ROUTE_LANE0 + MOE_EXPERTS].astype(jnp.int32)
    padded = (cnt + MOE_ROWS - 1) // MOE_ROWS * MOE_ROWS
    ends = jnp.cumsum(padded)
    starts = (ends - padded).astype(jnp.int32)
    n_assign = b * nt * TM * 2
    n_blocks = (n_assign + MOE_EXPERTS * (MOE_ROWS - 1)) // MOE_ROWS + 1
    n_pad = n_blocks * MOE_ROWS
    blk_row = jnp.arange(n_blocks, dtype=jnp.int32) * MOE_ROWS
    block_e = jnp.minimum(jnp.sum((blk_row[:, None] >= ends[None, :]).astype(jnp.int32), axis=1),
                          MOE_EXPERTS - 1)
    n_used = (ends[-1] // MOE_ROWS).astype(jnp.int32).reshape(1)
    pads = jnp.stack([jnp.append(starts + cnt, ends[-1]),
                      jnp.append(padded - cnt, (n_pad - ends[-1]) // PAD_BITS[0])]).astype(jnp.int32)
    xs = _dispatch(route, starts, pads, h2, n_pad)
    ys = _expert_mlp(block_e, n_used, xs, w_gu, w_down, layer)
    return route, starts, ys


def _router_weights(wg, bg, we, be):
    d = wg.shape[0]
    pad = LANES - MOE_GROUPS - MOE_EXPERTS
    assert ROUTE_LANE0 == MOE_GROUPS
    wr = jnp.concatenate([wg.astype(F32), we.astype(F32), jnp.zeros((d, pad), F32)], axis=1)
    br = jnp.concatenate([bg.astype(F32), be.astype(F32), jnp.zeros((pad,), F32)]).reshape(1, LANES)
    return jnp.stack(_split_bf16(wr)), br


def kernel(x, c, ctx, c_ctx, ada_w, ada_b, norm_g, final_g, ab_w_in, ab_w_out, ret_decay, ret_gn, na_rpb,
           swa_w_in, swa_w_out, swa_sink, router_g_w, router_g_b, router_e_w, router_e_b,
           expert_w_gu, expert_w_down):
    b, seq, d = x.shape
    assert ctx.shape[1] == TM and seq % TM == 0 and d == D_MODEL
    xa = (ctx, x)
    rope = _rope_tables(seq)

    cvec = jnp.concatenate([c, c_ctx[None, :], jnp.zeros((7, d), F32)], axis=0)
    mod = _adaln(cvec, ada_w, ada_b)
    mod = mod.reshape(DEPTH, b + 8, 6, 1, d)[:, :b + 1].transpose(0, 2, 1, 3, 4)

    for layer in range(DEPTH):
        last = layer == DEPTH - 1
        sh1, sc1, g1, sh2, sc2, g2 = (mod[layer, i] for i in range(6))
        gain1 = norm_g[layer, 0].reshape(1, d)
        gain2 = norm_g[layer, 1].reshape(1, d)
        j = layer // 2
        if layer % 2 == 0:
            nb = RET_W // LANES
            qk_scale = HEAD_DIM ** -0.5
            scales = {blk: qk_scale for blk in range(nb, 2 * nb)}
            scales.update({blk: qk_scale * LOG2E for blk in range(4 * nb, 5 * nb)})
            p = _proj(xa, gain1, sh1, sc1, ab_w_in[j].astype(BF16), rope,
                      rope_blocks=range(0, 2 * nb), scales=scales)
            log_gamma = jnp.log1p(-jnp.exp2(-ret_decay[j].astype(F32)))
            mixes = [_retention(p, log_gamma, ret_gn[j]), _neighbourhood(p, _na_bias(na_rpb[j]))]
            w_out = ab_w_out[j]
        else:
            qb = SWA_HEADS * HEAD_DIM // LANES
            kb = SWA_KV_HEADS * HEAD_DIM // LANES
            p = _proj(xa, gain1, sh1, sc1, swa_w_in[j].astype(BF16), rope,
                      rope_blocks=range(0, qb + kb),
                      scales={blk: HEAD_DIM ** -0.5 * LOG2E for blk in range(qb)})
            mixes = [_swa(p, swa_sink[j].astype(F32))]
            w_out = swa_w_out[j]
        t0 = 1 if last else 0
        wr, br = _router_weights(router_g_w[layer], router_g_b[layer], router_e_w[layer], router_e_b[layer])
        xa, h2, rinfo, counts = _outproj(mixes, w_out.astype(BF16), xa, g1, gain2, sh2, sc2, wr, br, t0)
        route, starts, ys = _moe(h2, rinfo, counts, expert_w_gu, expert_w_down, layer)
        xa = _combine(route, starts, ys, xa, rinfo, g2, final_g.reshape(1, d), not last, last)
    return xa
```

```python
import functools

import numpy as np
import jax
import jax.numpy as jnp
from jax import lax
from jax.experimental import pallas as pl
from jax.experimental.pallas import tpu as pltpu

F32 = jnp.float32
BF16 = jnp.bfloat16

D_MODEL = 1024
DEPTH = 2
GRID_W = 64
HEAD_DIM = 64
RET_HEADS = 8
NA_HEADS = 8
RET_W = 512
NA_W = 512
AB_IN = 4 * RET_W + 3 * NA_W
RET_CHUNK = 128
GN_EPS = 1e-5
NA_KH = 8
NA_KW = 16
SWA_HEADS = 16
SWA_KV_HEADS = 4
SWA_WINDOW = 128
SWA_BLOCK = 128
SWA_IN = (SWA_HEADS + 2 * SWA_KV_HEADS) * HEAD_DIM
ROPE_BASE = 10000.0
MOE_GROUPS = 4
MOE_EPG = 8
MOE_EXPERTS = 32
MOE_FF = 512
NORM_EPS = 1e-6
NEG_INF = -1e30

LANES = 128
TM = 256
RET_BLOCK = 256
MOE_ROWS = 512
ROUTE_LANE0 = 4
VMEM_LIMIT = 56 * 1024 * 1024


def _cparams(sem, vmem=VMEM_LIMIT, flags=None):
    return pltpu.CompilerParams(dimension_semantics=sem, vmem_limit_bytes=vmem, flags=flags)


def _split_bf16(a):
    hi = a.astype(BF16)
    lo = (a - hi.astype(F32)).astype(BF16)
    return hi, lo


def _dot3_split(a, bh, bl):
    ah, al = _split_bf16(a)
    d = lambda x, y: jnp.dot(x, y, preferred_element_type=F32)
    return d(ah, bh) + (d(ah, bl) + d(al, bh))


def _dot3(a, b):
    return _dot3_split(a, *_split_bf16(b))


def _dot_nt(a, b):
    return lax.dot_general(a, b, (((1,), (1,)), ((), ())), preferred_element_type=F32)


def _dot_tn(a, b):
    return lax.dot_general(a, b, (((0,), (0,)), ((), ())), preferred_element_type=F32)


def _silu(x):
    return x / (1.0 + jnp.exp(-x))


def _adaln_kernel(c_ref, w_ref, b_ref, o_ref):
    o_ref[...] = _dot3(_silu(c_ref[...]), w_ref[...]) + b_ref[...]


def _adaln(cvec, ada_w, ada_b):
    depth, d, n6 = ada_w.shape
    rows = cvec.shape[0]
    tn = 1024
    return pl.pallas_call(
        _adaln_kernel,
        grid=(depth, n6 // tn),
        in_specs=[
            pl.BlockSpec((rows, d), lambda l, j: (0, 0)),
            pl.BlockSpec((None, d, tn), lambda l, j: (l, 0, j)),
            pl.BlockSpec((None, 1, tn), lambda l, j: (l, 0, j)),
        ],
        out_specs=pl.BlockSpec((None, rows, tn), lambda l, j: (l, 0, j)),
        out_shape=jax.ShapeDtypeStruct((depth, rows, n6), F32),
        compiler_params=_cparams(("arbitrary", "arbitrary")),
        name="adaln",
    )(cvec, ada_w, ada_b.reshape(depth, 1, n6))


def _rms_mod(x, g, sh, sc):
    ms = jnp.mean(x * x, axis=-1, keepdims=True)
    return (x * lax.rsqrt(ms + NORM_EPS) * g) * (1.0 + sc) + sh


def _stream_tile(refs, is_lat):
    if len(refs) == 1:
        return refs[0][...]
    return jnp.where(is_lat, refs[1][...], refs[0][...])


def _stream_specs(stream, t0, extra_args=0):
    def im(f):
        return (lambda bi, t, *_: f(bi, t + t0))
    if not isinstance(stream, tuple):
        return [pl.BlockSpec((None, TM, stream.shape[2]), im(lambda bi, t: (bi, t, 0)))], [stream]
    ctx, x = stream
    d = x.shape[2]
    return ([pl.BlockSpec((None, TM, d), im(lambda bi, t: (bi, 0, 0))),
             pl.BlockSpec((None, TM, d), im(lambda bi, t: (bi, jnp.maximum(t - 1, 0), 0)))], [ctx, x])


def _stream_shape(stream):
    if not isinstance(stream, tuple):
        return stream.shape
    ctx, x = stream
    return (x.shape[0], ctx.shape[1] + x.shape[1], x.shape[2])


def _proj_kernel(*refs, rope_blocks, scales, cn, n_stream):
    x_refs = refs[:n_stream]
    g_ref, sh_ref, sc_ref, w_ref, rope_ref, o_ref = refs[n_stream:]
    is_lat = pl.program_id(1) > 0
    hb = _rms_mod(_stream_tile(x_refs, is_lat), g_ref[...], sh_ref[...], sc_ref[...]).astype(BF16)
    nout = w_ref.shape[1]
    for c in range(nout // cn):
        o = jnp.dot(hb, w_ref[:, c * cn:(c + 1) * cn], preferred_element_type=F32)
        for s in range(cn // LANES):
            blk = c * (cn // LANES) + s
            ob = o[:, s * LANES:(s + 1) * LANES]
            if blk in rope_blocks:
                r = (ob * rope_ref[0] + pltpu.roll(ob, 16, 1) * rope_ref[1]
                     + pltpu.roll(ob, LANES - 16, 1) * rope_ref[2])
                ob = jnp.where(is_lat, r, ob)
            if blk in scales:
                ob = ob * scales[blk]
            o_ref[:, blk * LANES:(blk + 1) * LANES] = ob.astype(o_ref.dtype)


def _proj(xa, gain, sh, sc, w_bf16, rope, rope_blocks, scales):
    b, s, d = _stream_shape(xa)
    nout = w_bf16.shape[1]
    nt = s // TM
    mod_idx = lambda bi, t: (jnp.where(t == 0, b, bi), 0, 0)
    x_specs, x_args = _stream_specs(xa, 0)
    kern = functools.partial(_proj_kernel, rope_blocks=frozenset(rope_blocks), scales=dict(scales), cn=512,
                             n_stream=len(x_args))
    return pl.pallas_call(
        kern,
        grid=(b, nt),
        in_specs=x_specs + [
            pl.BlockSpec((1, d), lambda bi, t: (0, 0)),
            pl.BlockSpec((None, 1, d), mod_idx),
            pl.BlockSpec((None, 1, d), mod_idx),
            pl.BlockSpec((d, nout), lambda bi, t: (0, 0)),
            pl.BlockSpec((3, TM, LANES), lambda bi, t: (0, jnp.maximum(t - 1, 0), 0)),
        ],
        out_specs=pl.BlockSpec((None, TM, nout), lambda bi, t: (bi, t, 0)),
        out_shape=jax.ShapeDtypeStruct((b, s, nout), BF16),
        compiler_params=_cparams(("arbitrary", "arbitrary")),
        name="proj",
    )(*x_args, gain, sh, sc, w_bf16, rope)


def _rope_tables(seq):
    nf = HEAD_DIM // 4
    inv = ROPE_BASE ** (-jnp.arange(nf, dtype=F32) / nf)
    t = jnp.arange(seq)
    row = (t // GRID_W).astype(F32)
    col = (t % GRID_W).astype(F32)
    lane = np.arange(LANES)
    jj = lane % HEAD_DIM
    axis_is_col = (jj // 32) == 1
    second_half = (jj % 32) >= 16
    f = jj % 16
    pos = jnp.where(axis_is_col[None, :], col[:, None], row[:, None])
    ang = pos * inv[f][None, :]
    c, s = jnp.cos(ang), jnp.sin(ang)
    sa = jnp.where(second_half[None, :], s, 0.0)
    sb = jnp.where(second_half[None, :], 0.0, -s)
    return jnp.stack([c, sa, sb], axis=0)


def _ret_kernel(lg_ref, q_ref, k_ref, v_ref, g_ref, gn_ref, o_ref, accf_ref, accb_ref,
                intra_ref, qd_ref, kd_ref):
    hp = pl.program_id(1)
    c = RET_BLOCK
    s_len = q_ref.shape[0]
    n_chunks = s_len // c
    ctx_chunks = TM // c
    pos = lax.broadcasted_iota(jnp.int32, (c, LANES), 0).astype(F32)
    m0 = lax.broadcasted_iota(jnp.int32, (c, LANES), 1) < HEAD_DIM
    same_head = ((lax.broadcasted_iota(jnp.int32, (LANES, LANES), 0) < HEAD_DIM)
                 == (lax.broadcasted_iota(jnp.int32, (LANES, LANES), 1) < HEAD_DIM))
    rel = (lax.broadcasted_iota(jnp.int32, (c, c), 0) - lax.broadcasted_iota(jnp.int32, (c, c), 1)).astype(F32)
    lgf = [lg_ref[0, hp * 2 + hh] for hh in range(2)]
    lgb = [lg_ref[1, hp * 2 + hh] for hh in range(2)]
    lgf_l = jnp.where(m0, lgf[0], lgf[1])
    lgb_l = jnp.where(m0, lgb[0], lgb[1])
    for hh in range(2):
        intra_ref[0, :, hh * c:(hh + 1) * c] = jnp.where(rel >= 0, jnp.exp(lgf[hh] * jnp.maximum(rel, 0.0)), 0.0)
        intra_ref[1, :, hh * c:(hh + 1) * c] = jnp.where(rel <= 0, jnp.exp(lgb[hh] * jnp.maximum(-rel, 0.0)), 0.0)
    qd_ref[0] = jnp.exp(lgf_l * (pos + 1.0))
    qd_ref[1] = jnp.exp(lgb_l * (c - pos))
    kd_ref[0] = jnp.exp(lgf_l * (c - 1.0 - pos))
    kd_ref[1] = jnp.exp(lgb_l * pos)
    cd = [jnp.exp(lgf_l[0:1] * float(c)), jnp.exp(lgb_l[0:1] * float(c))]

    def chunk(r0, state, d):
        q = q_ref[pl.ds(r0, c), :]
        k = k_ref[pl.ds(r0, c), :]
        v = v_ref[pl.ds(r0, c), :]
        qb = q.astype(BF16)
        kcat = jnp.concatenate([jnp.where(m0, k, 0.0), jnp.where(m0, 0.0, k)], axis=0).astype(BF16)
        vcat = jnp.concatenate([jnp.where(m0, v, 0.0), jnp.where(m0, 0.0, v)], axis=0).astype(BF16)
        s = _dot_nt(qb, kcat) * intra_ref[d]
        out = jnp.dot(s.astype(BF16), vcat, preferred_element_type=F32)
        out = out + jnp.dot(qb, state.astype(BF16), preferred_element_type=F32) * qd_ref[d]
        kv = _dot_tn((k * kd_ref[d]).astype(BF16), v.astype(BF16))
        return out, state * cd[d] + jnp.where(same_head, kv, 0.0)

    def body(i, states):
        sf, sb = states
        rf = pl.multiple_of(i * c, c)
        ib = jnp.where(i < ctx_chunks, ctx_chunks - 1 - i, n_chunks + ctx_chunks - 1 - i)
        rb = pl.multiple_of(ib * c, c)
        of, sf = chunk(rf, sf, 0)
        ob, sb = chunk(rb, sb, 1)
        accf_ref[pl.ds(rf, c), :] = of
        accb_ref[pl.ds(rb, c), :] = ob
        return sf, sb

    z = jnp.zeros((LANES, LANES), F32)
    lax.fori_loop(0, n_chunks, body, (z, z), unroll=True)

    avg = jnp.where(same_head, 1.0 / HEAD_DIM, 0.0).astype(BF16)

    def head_mean(x):
        hi, lo = _split_bf16(x)
        return (jnp.dot(hi, avg, preferred_element_type=F32) + jnp.dot(lo, avg, preferred_element_type=F32))

    def readout(i, carry):
        r0 = pl.multiple_of(i * c, c)
        o = accf_ref[pl.ds(r0, c), :] + accb_ref[pl.ds(r0, c), :]
        dlt = o - head_mean(o)
        var = head_mean(dlt * dlt)
        y = dlt * lax.rsqrt(var + GN_EPS) * gn_ref[...]
        o_ref[pl.ds(r0, c), :] = _silu(g_ref[pl.ds(r0, c), :].astype(F32)) * y
        return carry

    lax.fori_loop(0, n_chunks, readout, 0, unroll=True)


def _retention(p, log_gamma, ret_gn):
    b, s, _ = p.shape
    nb = RET_W // LANES
    blk = lambda off: pl.BlockSpec((None, s, LANES), lambda bi, hp: (bi, 0, off + hp))
    return pl.pallas_call(
        _ret_kernel,
        grid=(b, nb),
        in_specs=[
            pl.BlockSpec(memory_space=pltpu.SMEM),
            blk(0), blk(nb), blk(2 * nb), blk(3 * nb),
            pl.BlockSpec((1, LANES), lambda bi, hp: (0, hp)),
        ],
        out_specs=pl.BlockSpec((None, s, LANES), lambda bi, hp: (bi, 0, hp)),
        out_shape=jax.ShapeDtypeStruct((b, s, RET_W), F32),
        scratch_shapes=[pltpu.VMEM((s, LANES), F32), pltpu.VMEM((s, LANES), F32),
                        pltpu.VMEM((2, RET_BLOCK, 2 * RET_BLOCK), F32),
                        pltpu.VMEM((2, RET_BLOCK, LANES), F32), pltpu.VMEM((2, RET_BLOCK, LANES), F32)],
        compiler_params=_cparams(("arbitrary", "arbitrary")),
        name="retention",
    )(log_gamma, p, p, p, p, ret_gn.reshape(1, RET_W))


LOG2E = 1.4426950408889634


def _softmax_pv(s_list, v_list, extra=None):
    m = None
    for s in s_list:
        for j in range(s.shape[1] // LANES):
            blk = s[:, j * LANES:(j + 1) * LANES]
            m = blk if m is None else jnp.maximum(m, blk)
    m = m.max(axis=-1, keepdims=True)
    if extra is not None:
        m = jnp.maximum(m, extra)
    acc = None
    for s, v in zip(s_list, v_list):
        pv = jnp.dot(jnp.exp2(s - m).astype(BF16), v, preferred_element_type=F32)
        acc = pv if acc is None else acc + pv
    o, den = acc[:, :LANES], acc[:, LANES:]
    if extra is not None:
        den = den + jnp.exp2(extra - m)
    return o / den


def _stage_heads(q_ref, k_ref, v_ref, qm_ref, kb_ref, vb_ref):
    m0 = lax.broadcasted_iota(jnp.int32, (TM, LANES), 1) < HEAD_DIM

    def stage(i, carry):
        r0 = pl.multiple_of(i * TM, TM)
        q = q_ref[pl.ds(r0, TM), :]
        qm_ref[0, pl.ds(r0, TM), :] = jnp.where(m0, q, 0.0).astype(BF16)
        qm_ref[1, pl.ds(r0, TM), :] = jnp.where(m0, 0.0, q).astype(BF16)
        kb_ref[pl.ds(r0, TM), :] = k_ref[pl.ds(r0, TM), :].astype(BF16)
        vb_ref[pl.ds(r0, TM), 0:LANES] = v_ref[pl.ds(r0, TM), :].astype(BF16)
        vb_ref[pl.ds(r0, TM), LANES:2 * LANES] = jnp.ones((TM, LANES), BF16)
        return carry

    lax.fori_loop(0, q_ref.shape[0] // TM, stage, 0)


def _na_kernel(q_ref, k_ref, v_ref, bias_ref, o_ref, qm_ref, kb_ref, vb_ref):
    s_len = q_ref.shape[0]
    rows = (s_len - TM) // GRID_W
    nloc = NA_KH * GRID_W
    _stage_heads(q_ref, k_ref, v_ref, qm_ref, kb_ref, vb_ref)

    kc = kb_ref[0:TM, :]
    vc = vb_ref[0:TM, :]
    outs = [_softmax_pv([_dot_nt(qm_ref[hh, 0:TM, :], kc)], [vc]) for hh in range(2)]
    m0c = lax.broadcasted_iota(jnp.int32, (TM, LANES), 1) < HEAD_DIM
    o_ref[0:TM, :] = jnp.where(m0c, outs[0], outs[1])

    m0 = lax.broadcasted_iota(jnp.int32, (GRID_W, LANES), 1) < HEAD_DIM

    def row_block(r, carry):
        rs = jnp.clip(r - NA_KH // 2, 0, rows - NA_KH)
        pat = r - rs
        q0 = pl.multiple_of(TM + r * GRID_W, GRID_W)
        k0 = pl.multiple_of(TM + rs * GRID_W, GRID_W)
        kl = kb_ref[pl.ds(k0, nloc), :]
        vl = vb_ref[pl.ds(k0, nloc), :]
        kc = kb_ref[0:TM, :]
        vc = vb_ref[0:TM, :]
        q = jnp.concatenate([qm_ref[0, pl.ds(q0, GRID_W), :], qm_ref[1, pl.ds(q0, GRID_W), :]], axis=0)
        s_loc = _dot_nt(q, kl) + bias_ref[pat].reshape(2 * GRID_W, nloc)
        s_ctx = _dot_nt(q, kc)
        res = _softmax_pv([s_loc, s_ctx], [vl, vc])
        o_ref[pl.ds(q0, GRID_W), :] = jnp.where(m0, res[:GRID_W], res[GRID_W:])
        return carry

    lax.fori_loop(0, rows, row_block, 0, unroll=True)


def _na_bias(rpb):
    h = rpb.shape[0]
    qc = np.arange(GRID_W)[:, None]
    kc = np.arange(GRID_W)[None, :]
    win = np.clip(qc - NA_KW // 2, 0, GRID_W - NA_KW)
    valid = (kc >= win) & (kc < win + NA_KW)
    col_off = np.clip(kc - qc + NA_KW - 1, 0, 2 * NA_KW - 2)
    onehot = (col_off[None] == np.arange(2 * NA_KW - 1)[:, None, None]).astype(np.float32)
    cols = jnp.einsum("hrc,cqk->hrqk", rpb.astype(F32), onehot, precision=lax.Precision.HIGHEST)
    cols = jnp.where(valid[None, None], cols * LOG2E, NEG_INF)
    bias = jnp.stack([cols[:, NA_KH - 1 - p:2 * NA_KH - 1 - p] for p in range(NA_KH)])
    bias = bias.transpose(0, 1, 3, 2, 4)
    return bias.reshape(NA_KH, h, GRID_W, NA_KH * GRID_W)


def _neighbourhood(p, bias):
    b, s, _ = p.shape
    nb = NA_W // LANES
    c0 = 4 * RET_W // LANES
    blk = lambda off: pl.BlockSpec((None, s, LANES), lambda bi, hp: (bi, 0, c0 + off + hp))
    return pl.pallas_call(
        _na_kernel,
        grid=(b, nb),
        in_specs=[
            blk(0), blk(nb), blk(2 * nb),
            pl.BlockSpec((NA_KH, 2, GRID_W, NA_KH * GRID_W), lambda bi, hp: (0, hp, 0, 0)),
        ],
        out_specs=pl.BlockSpec((None, s, LANES), lambda bi, hp: (bi, 0, hp)),
        out_shape=jax.ShapeDtypeStruct((b, s, NA_W), F32),
        scratch_shapes=[pltpu.VMEM((2, s, LANES), BF16), pltpu.VMEM((s, LANES), BF16),
                        pltpu.VMEM((s, 2 * LANES), BF16)],
        compiler_params=_cparams(("arbitrary", "arbitrary")),
        name="neighbourhood",
    )(p, p, p, bias)


def _swa_kernel(sink_ref, q_ref, k_ref, v_ref, o_ref, kd_ref, vd_ref):
    kp = pl.program_id(1)
    s_len = q_ref.shape[0]
    seq = s_len - TM
    nblk = seq // SWA_BLOCK
    band = SWA_BLOCK + 2 * SWA_WINDOW
    group = SWA_HEADS // SWA_KV_HEADS
    heads_per_step = 2 * group
    m0t = lax.broadcasted_iota(jnp.int32, (TM, LANES), 1) < HEAD_DIM

    def stage(i, carry):
        r0 = pl.multiple_of(i * TM, TM)
        for src, dst in ((k_ref, kd_ref), (v_ref, vd_ref)):
            x = src[pl.ds(r0, TM), :].astype(F32)
            xr = pltpu.roll(x, HEAD_DIM, 1)
            dst[0, pl.ds(r0, TM), 0:LANES] = jnp.where(m0t, x, xr).astype(BF16)
            dst[1, pl.ds(r0, TM), 0:LANES] = jnp.where(m0t, xr, x).astype(BF16)
        for h in range(2):
            vd_ref[h, pl.ds(r0, TM), LANES:2 * LANES] = jnp.ones((TM, LANES), BF16)
        return carry

    lax.fori_loop(0, s_len // TM, stage, 0)
    o_ref[0:TM, :] = jnp.zeros((TM, o_ref.shape[1]), F32)

    rows = group * SWA_BLOCK
    qi = lax.broadcasted_iota(jnp.int32, (rows, band), 0) % SWA_BLOCK
    ki = lax.broadcasted_iota(jnp.int32, (rows, band), 1)
    m0 = lax.broadcasted_iota(jnp.int32, (SWA_BLOCK, LANES), 1) < HEAD_DIM
    head_of_row = lax.broadcasted_iota(jnp.int32, (rows, 1), 0) // SWA_BLOCK

    def block(i, carry):
        start = jnp.clip((i - 1) * SWA_BLOCK, 0, seq - band)
        k0 = pl.multiple_of(TM + start, SWA_BLOCK)
        q0 = pl.multiple_of(TM + i * SWA_BLOCK, SWA_BLOCK)
        valid = jnp.abs(qi + (i * SWA_BLOCK - start) - ki) <= SWA_WINDOW
        for hh in range(2):
            kb = kd_ref[hh, pl.ds(k0, band), :]
            vb = vd_ref[hh, pl.ds(k0, band), :]
            kc = kd_ref[hh, 0:TM, :]
            vc = vd_ref[hh, 0:TM, :]
            parts = []
            for j in range(group // 2):
                pair = hh * (group // 2) + j
                qp = q_ref[pl.ds(q0, SWA_BLOCK), pair * LANES:(pair + 1) * LANES]
                parts += [jnp.where(m0, qp, 0.0), jnp.where(m0, 0.0, qp)]
            q = jnp.concatenate(parts, axis=0).astype(BF16)
            s = jnp.where(valid, _dot_nt(q, kb), NEG_INF)
            s_ctx = _dot_nt(q, kc)
            h0 = kp * heads_per_step + hh * group
            sink = jnp.full((rows, 1), sink_ref[h0], F32)
            for g in range(1, group):
                sink = jnp.where(head_of_row == g, sink_ref[h0 + g], sink)
            res = _softmax_pv([s, s_ctx], [vb, vc], extra=sink * LOG2E)
            for j in range(group // 2):
                pair = hh * (group // 2) + j
                r0 = 2 * j * SWA_BLOCK
                o_ref[pl.ds(q0, SWA_BLOCK), pair * LANES:(pair + 1) * LANES] = jnp.where(
                    m0, res[r0:r0 + SWA_BLOCK], res[r0 + SWA_BLOCK:r0 + 2 * SWA_BLOCK])
        return carry

    lax.fori_loop(0, nblk, block, 0, unroll=True)


def _swa(p, sink):
    b, s, _ = p.shape
    qw = SWA_HEADS * HEAD_DIM // 2
    kblk = SWA_HEADS * HEAD_DIM // LANES
    vblk = kblk + SWA_KV_HEADS * HEAD_DIM // LANES
    return pl.pallas_call(
        _swa_kernel,
        grid=(b, 2),
        in_specs=[
            pl.BlockSpec(memory_space=pltpu.SMEM),
            pl.BlockSpec((None, s, qw), lambda bi, kp: (bi, 0, kp)),
            pl.BlockSpec((None, s, LANES), lambda bi, kp: (bi, 0, kblk + kp)),
            pl.BlockSpec((None, s, LANES), lambda bi, kp: (bi, 0, vblk + kp)),
        ],
        out_specs=pl.BlockSpec((None, s, qw), lambda bi, kp: (bi, 0, kp)),
        out_shape=jax.ShapeDtypeStruct((b, s, SWA_HEADS * HEAD_DIM), F32),
        scratch_shapes=[pltpu.VMEM((2, s, LANES), BF16), pltpu.VMEM((2, s, 2 * LANES), BF16)],
        compiler_params=_cparams(("arbitrary", "arbitrary")),
        name="swa",
    )(sink, p, p, p)


def _outproj_kernel(*refs, n_mix, n_stream, t0):
    mix_refs = refs[:n_mix]
    x_refs = refs[n_mix + 1:n_mix + 1 + n_stream]
    w_ref = refs[n_mix]
    (g1_ref, gn_ref, sh_ref, sc_ref, wr_ref, br_ref,
     xo_ref, h_ref, ri_ref, rt_ref, cnt_ref, carry_ref) = refs[n_mix + 1 + n_stream:]
    first = (pl.program_id(0) == 0) & (pl.program_id(1) == 0)

    @pl.when(first)
    def _():
        carry_ref[...] = jnp.zeros_like(carry_ref)

    o = None
    off = 0
    for m_ref in mix_refs:
        kw = m_ref.shape[1]
        part = jnp.dot(m_ref[...].astype(BF16), w_ref[off:off + kw, :], preferred_element_type=F32)
        o = part if o is None else o + part
        off += kw
    xn = _stream_tile(x_refs, pl.program_id(1) + t0 > 0) + g1_ref[...] * o
    xo_ref[...] = xn
    h = _rms_mod(xn, gn_ref[...], sh_ref[...], sc_ref[...])
    h_ref[...] = h

    logits = _dot3_split(h, wr_ref[0], wr_ref[1]) + br_ref[...]
    tm = logits.shape[0]
    lane = lax.broadcasted_iota(jnp.int32, (tm, LANES), 1).astype(F32)
    big = 1e9
    gmask = lane < MOE_GROUPS
    mg = jnp.max(jnp.where(gmask, logits, -big), axis=-1, keepdims=True)
    sg = jnp.sum(jnp.where(gmask, jnp.exp(jnp.minimum(logits - mg, 0.0)), 0.0), axis=-1, keepdims=True)
    gw = 1.0 / sg
    gi = jnp.min(jnp.where(gmask & (logits == mg), lane, big), axis=-1, keepdims=True)
    lo = ROUTE_LANE0 + MOE_EPG * gi
    emask = (lane >= lo) & (lane < lo + MOE_EPG)
    l1 = jnp.max(jnp.where(emask, logits, -big), axis=-1, keepdims=True)
    i1 = jnp.min(jnp.where(emask & (logits == l1), lane, big), axis=-1, keepdims=True)
    emask2 = emask & (lane != i1)
    l2 = jnp.max(jnp.where(emask2, logits, -big), axis=-1, keepdims=True)
    i2 = jnp.min(jnp.where(emask2 & (logits == l2), lane, big), axis=-1, keepdims=True)
    e21 = jnp.exp(l2 - l1)
    w1 = gw / (1.0 + e21)
    w2 = gw * e21 / (1.0 + e21)

    oh = jnp.where((lane == i1) | (lane == i2), 1.0, 0.0)
    tri = (lax.broadcasted_iota(jnp.int32, (tm, tm), 0) > lax.broadcasted_iota(jnp.int32, (tm, tm), 1))
    cum = jnp.dot(jnp.where(tri, 1.0, 0.0).astype(BF16), oh.astype(BF16), preferred_element_type=F32) + carry_ref[...]
    r1 = jnp.sum(jnp.where(lane == i1, cum, 0.0), axis=-1, keepdims=True)
    r2 = jnp.sum(jnp.where(lane == i2, cum, 0.0), axis=-1, keepdims=True)
    carry_ref[...] = carry_ref[...] + jnp.sum(oh, axis=0, keepdims=True)
    cnt_ref[...] = carry_ref[...]

    ri = jnp.where(lane == 0, i1 - ROUTE_LANE0, 0.0)
    ri = jnp.where(lane == 1, i2 - ROUTE_LANE0, ri)
    ri = jnp.where(lane == 2, w1, ri)
    ri = jnp.where(lane == 3, w2, ri)
    ri = jnp.where(lane == 4, r1, ri)
    ri = jnp.where(lane == 5, r2, ri)
    ri_ref[...] = ri
    rt_ref[...] = ri.T[0:SUBLANES, :]


def _outproj(mixes, w_bf16, xa, g1, gain2, sh2, sc2, wr, br, t0):
    b, s, d = _stream_shape(xa)
    nt = s // TM - t0
    so = nt * TM
    mod_idx = lambda bi, t: (jnp.where(t + t0 == 0, b, bi), 0, 0)
    tile = lambda wdt: pl.BlockSpec((None, TM, wdt), lambda bi, t: (bi, t + t0, 0))
    otile = lambda wdt: pl.BlockSpec((None, TM, wdt), lambda bi, t: (bi, t, 0))
    const = lambda shape: pl.BlockSpec(shape, lambda bi, t: (0,) * len(shape))
    x_specs, x_args = _stream_specs(xa, t0)
    in_specs = [tile(m.shape[2]) for m in mixes] + [const(w_bf16.shape)] + x_specs + [
        pl.BlockSpec((None, 1, d), mod_idx), const((1, d)),
        pl.BlockSpec((None, 1, d), mod_idx), pl.BlockSpec((None, 1, d), mod_idx),
        const((2, d, LANES)), const((1, LANES)),
    ]
    return pl.pallas_call(
        functools.partial(_outproj_kernel, n_mix=len(mixes), n_stream=len(x_args), t0=t0),
        grid=(b, nt),
        in_specs=in_specs,
        out_specs=[otile(d), otile(d), otile(LANES),
                   pl.BlockSpec((None, None, SUBLANES, TM), lambda bi, t: (bi, t, 0, 0)), const((1, LANES))],
        out_shape=[jax.ShapeDtypeStruct((b, so, d), F32), jax.ShapeDtypeStruct((b, so, d), F32),
                   jax.ShapeDtypeStruct((b, so, LANES), F32), jax.ShapeDtypeStruct((b, nt, SUBLANES, TM), F32),
                   jax.ShapeDtypeStruct((1, LANES), F32)],
        scratch_shapes=[pltpu.VMEM((1, LANES), F32)],
        compiler_params=_cparams(("arbitrary", "arbitrary")),
        name="outproj",
    )(*mixes, w_bf16, *x_args, g1, gain2, sh2, sc2, wr, br)


SUBLANES = 8


def _to_token_tiles(x):
    return x.reshape(x.shape[0], SUBLANES, x.shape[1] // SUBLANES)


def _from_token_tiles(x3):
    return x3.reshape(x3.shape[0], x3.shape[1] * x3.shape[2])


PAD_BITS = tuple(1 << k for k in reversed(range(MOE_ROWS.bit_length() - 1)))


def _dispatch_kernel(dest_ref, pads_ref, h_ref, xs_ref, hbuf, zbuf, sem, zsem):
    nt = pl.num_programs(1)
    step = pl.program_id(0) * nt + pl.program_id(1)
    last = pl.num_programs(0) * nt - 1
    slot = step % 2

    def zero_pads(wait):
        def expert(e, c):
            pos = pads_ref[0, e]
            n = pads_ref[1, e]
            for bit in PAD_BITS:
                take = (n & bit) != 0
                cp = pltpu.make_async_copy(zbuf.at[pl.ds(0, bit)], xs_ref.at[pl.ds(pos, bit)], zsem)

                @pl.when(take)
                def _():
                    cp.wait() if wait else cp.start()

                pos = pos + jnp.where(take, bit, 0)
            return c
        lax.fori_loop(0, MOE_EXPERTS, expert, 0)

        def tail(j, c):
            cp = pltpu.make_async_copy(
                zbuf, xs_ref.at[pl.ds(pads_ref[0, MOE_EXPERTS] + j * PAD_BITS[0], PAD_BITS[0])], zsem)
            cp.wait() if wait else cp.start()
            return c
        lax.fori_loop(0, pads_ref[1, MOE_EXPERTS], tail, 0)

    @pl.when(step == 0)
    def _():
        zbuf[...] = jnp.zeros_like(zbuf)
        zero_pads(wait=False)
    hbuf[slot] = _to_token_tiles(h_ref[...])

    def copy(sl, i, dst_row):
        return pltpu.make_async_copy(hbuf.at[sl, i], xs_ref.at[dst_row], sem.at[sl])

    for i in range(TM):
        copy(slot, i, dest_ref[step, i]).start()
        copy(slot, i, dest_ref[step, TM + i]).start()

    def drain(sl):
        def one(i, c):
            copy(sl, 0, 0).wait()
            return c
        lax.fori_loop(0, 2 * TM, one, 0, unroll=8)

    @pl.when(step > 0)
    def _():
        drain(1 - slot)

    @pl.when(step == last)
    def _():
        drain(slot)
        zero_pads(wait=True)


def _dispatch(dest, pads, h2, n_pad):
    b, s, d = h2.shape
    nt = s // TM
    tile = (SUBLANES, d // SUBLANES)
    return pl.pallas_call(
        _dispatch_kernel,
        grid_spec=pltpu.PrefetchScalarGridSpec(
            num_scalar_prefetch=2,
            grid=(b, nt),
            in_specs=[pl.BlockSpec((None, TM, d), lambda bi, t, dr, pd: (bi, t, 0))],
            out_specs=pl.BlockSpec(memory_space=pl.ANY),
            scratch_shapes=[pltpu.VMEM((2, TM) + tile, F32), pltpu.VMEM((PAD_BITS[0],) + tile, F32),
                            pltpu.SemaphoreType.DMA((2,)), pltpu.SemaphoreType.DMA],
        ),
        out_shape=jax.ShapeDtypeStruct((n_pad,) + tile, F32),
        compiler_params=_cparams(("arbitrary", "arbitrary")),
        name="dispatch",
    )(dest, pads, h2)


def _mlp_kernel(be_ref, nu_ref, x_ref, wgu_ref, wd_ref, y_ref, wgu_b, wd_b):
    i = pl.program_id(0)
    prev = be_ref[jnp.maximum(i - 1, 0)]
    used = i < nu_ref[0]

    @pl.when(used & ((i == 0) | (be_ref[i] != prev)))
    def _():
        wgu_b[...] = wgu_ref[...].astype(BF16)
        wd_b[...] = wd_ref[...].astype(BF16)

    @pl.when(used)
    def _():
        x = _from_token_tiles(x_ref[...])
        gu = jnp.dot(x.astype(BF16), wgu_b[...], preferred_element_type=F32)
        act = _silu(gu[:, :MOE_FF]) * gu[:, MOE_FF:]
        y_ref[...] = _to_token_tiles(jnp.dot(act.astype(BF16), wd_b[...], preferred_element_type=F32))

    @pl.when(jnp.logical_not(used))
    def _():
        y_ref[...] = jnp.zeros_like(y_ref)


def _expert_mlp(block_e, n_used, xs, w_gu, w_down, layer):
    n_pad, sub, dl = xs.shape
    d = sub * dl
    ff2 = w_gu.shape[-1]
    slots = pl.BlockSpec((MOE_ROWS, sub, dl), lambda i, be, nu: (i, 0, 0))
    used_slots = pl.BlockSpec((MOE_ROWS, sub, dl), lambda i, be, nu: (jnp.minimum(i, nu[0] - 1), 0, 0))
    return pl.pallas_call(
        _mlp_kernel,
        grid_spec=pltpu.PrefetchScalarGridSpec(
            num_scalar_prefetch=2,
            grid=(n_pad // MOE_ROWS,),
            in_specs=[
                used_slots,
                pl.BlockSpec((None, None, d, ff2), lambda i, be, nu: (layer, be[i], 0, 0)),
                pl.BlockSpec((None, None, ff2 // 2, d), lambda i, be, nu: (layer, be[i], 0, 0)),
            ],
            out_specs=slots,
            scratch_shapes=[pltpu.VMEM((d, ff2), BF16), pltpu.VMEM((ff2 // 2, d), BF16)],
        ),
        out_shape=jax.ShapeDtypeStruct(xs.shape, F32),
        compiler_params=_cparams(("arbitrary",)),
        name="expert_mlp",
    )(block_e, n_used, xs, w_gu, w_down)


def _combine_kernel(dest_ref, ys_ref, x_ref, ri_ref, g2_ref, fg_ref, o_ref, buf, sem, *, final):
    bi = pl.program_id(0)
    t = pl.program_id(1)
    nt = pl.num_programs(1)
    step = bi * nt + t
    total = pl.num_programs(0) * nt

    def copy(src_row, slot, k, i):
        return pltpu.make_async_copy(ys_ref.at[src_row], buf.at[slot, k, i], sem.at[slot])

    def issue(st, slot):
        for i in range(TM):
            copy(dest_ref[st, i], slot, 0, i).start()
            copy(dest_ref[st, TM + i], slot, 1, i).start()

    slot = step % 2

    @pl.when(step == 0)
    def _():
        issue(0, 0)

    @pl.when(step + 1 < total)
    def _():
        issue(step + 1, 1 - slot)

    def drain(i, c):
        copy(0, slot, 0, 0).wait()
        return c

    lax.fori_loop(0, 2 * TM, drain, 0, unroll=8)

    lane = lax.broadcasted_iota(jnp.int32, (TM, LANES), 1)
    ri = ri_ref[...]
    w1 = jnp.sum(jnp.where(lane == 2, ri, 0.0), axis=-1, keepdims=True)
    w2 = jnp.sum(jnp.where(lane == 3, ri, 0.0), axis=-1, keepdims=True)
    y = _from_token_tiles(buf[slot, 0]) * w1 + _from_token_tiles(buf[slot, 1]) * w2
    xn = x_ref[...] + g2_ref[...] * y
    if final:
        ms = jnp.mean(xn * xn, axis=-1, keepdims=True)
        xn = xn * lax.rsqrt(ms + NORM_EPS) * fg_ref[...]
    o_ref[...] = xn


def _combine(dest, ys, xa, rinfo, g2, final_g, has_ctx, final):
    b, s, d = xa.shape
    nt = s // TM
    mod_idx = lambda bi, t, dr: (jnp.where(t == 0, b, bi) if has_ctx else bi, 0, 0)
    tile = lambda wdt: pl.BlockSpec((None, TM, wdt), lambda bi, t, dr: (bi, t, 0))
    out_spec = tile(d)
    out_shape = jax.ShapeDtypeStruct((b, s, d), F32)
    return pl.pallas_call(
        functools.partial(_combine_kernel, final=final),
        grid_spec=pltpu.PrefetchScalarGridSpec(
            num_scalar_prefetch=1,
            grid=(b, nt),
            in_specs=[
                pl.BlockSpec(memory_space=pl.ANY),
                tile(d), tile(LANES),
                pl.BlockSpec((None, 1, d), mod_idx),
                pl.BlockSpec((1, d), lambda bi, t, dr: (0, 0)),
            ],
            out_specs=out_spec,
            scratch_shapes=[pltpu.VMEM((2, 2, TM, SUBLANES, d // SUBLANES), F32),
                            pltpu.SemaphoreType.DMA((2,))],
        ),
        out_shape=out_shape,
        compiler_params=_cparams(("arbitrary", "arbitrary")),
        name="combine",
    )(dest, ys, xa, rinfo, g2, final_g)


def _moe(h2, rt, counts, w_gu, w_down, layer):
    b, s, d = h2.shape
    nt = s // TM
    r = rt.reshape(b * nt, SUBLANES, TM)
    fields = lambda i: jnp.concatenate([r[:, i], r[:, i + 1]], axis=1).astype(jnp.int32)
    e = fields(0)
    rank = fields(4)
    cnt = counts[0, ROUTE_LANE0:ROUTE_LANE0 + MOE_EXPERTS].astype(jnp.int32)
    padded = (cnt + MOE_ROWS - 1) // MOE_ROWS * MOE_ROWS
    ends = jnp.cumsum(padded)
    starts = ends - padded
    eids = jnp.arange(MOE_EXPERTS, dtype=jnp.int32)
    dest = jnp.sum(jnp.where(e[..., None] == eids, starts, 0), axis=-1) + rank
    n_assign = b * nt * TM * 2
    n_blocks = (n_assign + MOE_EXPERTS * (MOE_ROWS - 1)) // MOE_ROWS + 1
    n_pad = n_blocks * MOE_ROWS
    blk_row = jnp.arange(n_blocks, dtype=jnp.int32) * MOE_ROWS
    block_e = jnp.minimum(jnp.sum((blk_row[:, None] >= ends[None, :]).astype(jnp.int32), axis=1),
                          MOE_EXPERTS - 1)
    n_used = (ends[-1] // MOE_ROWS).astype(jnp.int32).reshape(1)
    pads = jnp.stack([jnp.append(starts + cnt, ends[-1]),
                      jnp.append(padded - cnt, (n_pad - ends[-1]) // PAD_BITS[0])]).astype(jnp.int32)
    xs = _dispatch(dest, pads, h2, n_pad)
    ys = _expert_mlp(block_e, n_used, xs, w_gu, w_down, layer)
    return dest, ys


def _router_weights(wg, bg, we, be):
    d = wg.shape[0]
    pad = LANES - MOE_GROUPS - MOE_EXPERTS
    assert ROUTE_LANE0 == MOE_GROUPS
    wr = jnp.concatenate([wg.astype(F32), we.astype(F32), jnp.zeros((d, pad), F32)], axis=1)
    br = jnp.concatenate([bg.astype(F32), be.astype(F32), jnp.zeros((pad,), F32)]).reshape(1, LANES)
    return jnp.stack(_split_bf16(wr)), br


def kernel(x, c, ctx, c_ctx, ada_w, ada_b, norm_g, final_g, ab_w_in, ab_w_out, ret_decay, ret_gn, na_rpb,
           swa_w_in, swa_w_out, swa_sink, router_g_w, router_g_b, router_e_w, router_e_b,
           expert_w_gu, expert_w_down):
    b, seq, d = x.shape
    assert ctx.shape[1] == TM and seq % TM == 0 and d == D_MODEL
    xa = (ctx, x)
    rope = _rope_tables(seq)

    cvec = jnp.concatenate([c, c_ctx[None, :], jnp.zeros((7, d), F32)], axis=0)
    mod = _adaln(cvec, ada_w, ada_b)
    mod = mod.reshape(DEPTH, b + 8, 6, 1, d)[:, :b + 1].transpose(0, 2, 1, 3, 4)

    for layer in range(DEPTH):
        last = layer == DEPTH - 1
        sh1, sc1, g1, sh2, sc2, g2 = (mod[layer, i] for i in range(6))
        gain1 = norm_g[layer, 0].reshape(1, d)
        gain2 = norm_g[layer, 1].reshape(1, d)
        j = layer // 2
        if layer % 2 == 0:
            nb = RET_W // LANES
            qk_scale = HEAD_DIM ** -0.5
            scales = {blk: qk_scale for blk in range(nb, 2 * nb)}
            scales.update({blk: qk_scale * LOG2E for blk in range(4 * nb, 5 * nb)})
            p = _proj(xa, gain1, sh1, sc1, ab_w_in[j].astype(BF16), rope,
                      rope_blocks=range(0, 2 * nb), scales=scales)
            log_gamma = jnp.log1p(-jnp.exp2(-ret_decay[j].astype(F32)))
            mixes = [_retention(p, log_gamma, ret_gn[j]), _neighbourhood(p, _na_bias(na_rpb[j]))]
            w_out = ab_w_out[j]
        else:
            qb = SWA_HEADS * HEAD_DIM // LANES
            kb = SWA_KV_HEADS * HEAD_DIM // LANES
            p = _proj(xa, gain1, sh1, sc1, swa_w_in[j].astype(BF16), rope,
                      rope_blocks=range(0, qb + kb),
                      scales={blk: HEAD_DIM ** -0.5 * LOG2E for blk in range(qb)})
            mixes = [_swa(p, swa_sink[j].astype(F32))]
            w_out = swa_w_out[j]
        t0 = 1 if last else 0
        wr, br = _router_weights(router_g_w[layer], router_g_b[layer], router_e_w[layer], router_e_b[layer])
        xa, h2, rinfo, rt, counts = _outproj(mixes, w_out.astype(BF16), xa, g1, gain2, sh2, sc2, wr, br, t0)
        dest, ys = _moe(h2, rt, counts, expert_w_gu, expert_w_down, layer)
        xa = _combine(dest, ys, xa, rinfo, g2, final_g.reshape(1, d), not last, last)
    return xa
```

```python
import functools

import numpy as np
import jax
import jax.numpy as jnp
from jax import lax
from jax.experimental import pallas as pl
from jax.experimental.pallas import tpu as pltpu

F32 = jnp.float32
BF16 = jnp.bfloat16

D_MODEL = 1024
DEPTH = 2
GRID_W = 64
HEAD_DIM = 64
RET_HEADS = 8
NA_HEADS = 8
RET_W = 512
NA_W = 512
AB_IN = 4 * RET_W + 3 * NA_W
RET_CHUNK = 128
GN_EPS = 1e-5
NA_KH = 8
NA_KW = 16
SWA_HEADS = 16
SWA_KV_HEADS = 4
SWA_WINDOW = 128
SWA_BLOCK = 128
SWA_IN = (SWA_HEADS + 2 * SWA_KV_HEADS) * HEAD_DIM
ROPE_BASE = 10000.0
MOE_GROUPS = 4
MOE_EPG = 8
MOE_EXPERTS = 32
MOE_FF = 512
NORM_EPS = 1e-6
NEG_INF = -1e30

LANES = 128
TM = 256
RET_BLOCK = 256
MOE_ROWS = 512
ROUTE_LANE0 = 4
VMEM_LIMIT = 56 * 1024 * 1024


def _cparams(sem, vmem=VMEM_LIMIT, flags=None):
    return pltpu.CompilerParams(dimension_semantics=sem, vmem_limit_bytes=vmem, flags=flags)


def _split_bf16(a):
    hi = a.astype(BF16)
    lo = (a - hi.astype(F32)).astype(BF16)
    return hi, lo


def _dot3_split(a, bh, bl):
    ah, al = _split_bf16(a)
    d = lambda x, y: jnp.dot(x, y, preferred_element_type=F32)
    return d(ah, bh) + (d(ah, bl) + d(al, bh))


def _dot3(a, b):
    return _dot3_split(a, *_split_bf16(b))


def _dot_nt(a, b):
    return lax.dot_general(a, b, (((1,), (1,)), ((), ())), preferred_element_type=F32)


def _dot_tn(a, b):
    return lax.dot_general(a, b, (((0,), (0,)), ((), ())), preferred_element_type=F32)


def _silu(x):
    return x / (1.0 + jnp.exp(-x))


def _adaln_kernel(c_ref, w_ref, b_ref, o_ref):
    o_ref[...] = _dot3(_silu(c_ref[...]), w_ref[...]) + b_ref[...]


def _adaln(cvec, ada_w, ada_b):
    depth, d, n6 = ada_w.shape
    rows = cvec.shape[0]
    tn = 1024
    return pl.pallas_call(
        _adaln_kernel,
        grid=(depth, n6 // tn),
        in_specs=[
            pl.BlockSpec((rows, d), lambda l, j: (0, 0)),
            pl.BlockSpec((None, d, tn), lambda l, j: (l, 0, j)),
            pl.BlockSpec((None, 1, tn), lambda l, j: (l, 0, j)),
        ],
        out_specs=pl.BlockSpec((None, rows, tn), lambda l, j: (l, 0, j)),
        out_shape=jax.ShapeDtypeStruct((depth, rows, n6), F32),
        compiler_params=_cparams(("arbitrary", "arbitrary")),
        name="adaln",
    )(cvec, ada_w, ada_b.reshape(depth, 1, n6))


def _rms_mod(x, g, sh, sc):
    ms = jnp.mean(x * x, axis=-1, keepdims=True)
    return (x * lax.rsqrt(ms + NORM_EPS) * g) * (1.0 + sc) + sh


def _stream_tile(refs, is_lat):
    if len(refs) == 1:
        return refs[0][...]
    return jnp.where(is_lat, refs[1][...], refs[0][...])


def _stream_specs(stream, t0, extra_args=0):
    def im(f):
        return (lambda bi, t, *_: f(bi, t + t0))
    if not isinstance(stream, tuple):
        return [pl.BlockSpec((None, TM, stream.shape[2]), im(lambda bi, t: (bi, t, 0)))], [stream]
    ctx, x = stream
    d = x.shape[2]
    return ([pl.BlockSpec((None, TM, d), im(lambda bi, t: (bi, 0, 0))),
             pl.BlockSpec((None, TM, d), im(lambda bi, t: (bi, jnp.maximum(t - 1, 0), 0)))], [ctx, x])


def _stream_shape(stream):
    if not isinstance(stream, tuple):
        return stream.shape
    ctx, x = stream
    return (x.shape[0], ctx.shape[1] + x.shape[1], x.shape[2])


def _proj_kernel(*refs, rope_blocks, scales, cn, n_stream):
    x_refs = refs[:n_stream]
    g_ref, sh_ref, sc_ref, w_ref, rope_ref, o_ref = refs[n_stream:]
    is_lat = pl.program_id(1) > 0
    hb = _rms_mod(_stream_tile(x_refs, is_lat), g_ref[...], sh_ref[...], sc_ref[...]).astype(BF16)
    nout = w_ref.shape[1]
    for c in range(nout // cn):
        o = jnp.dot(hb, w_ref[:, c * cn:(c + 1) * cn], preferred_element_type=F32)
        for s in range(cn // LANES):
            blk = c * (cn // LANES) + s
            ob = o[:, s * LANES:(s + 1) * LANES]
            if blk in rope_blocks:
                r = (ob * rope_ref[0] + pltpu.roll(ob, 16, 1) * rope_ref[1]
                     + pltpu.roll(ob, LANES - 16, 1) * rope_ref[2])
                ob = jnp.where(is_lat, r, ob)
            if blk in scales:
                ob = ob * scales[blk]
            o_ref[:, blk * LANES:(blk + 1) * LANES] = ob.astype(o_ref.dtype)


def _proj(xa, gain, sh, sc, w_bf16, rope, rope_blocks, scales):
    b, s, d = _stream_shape(xa)
    nout = w_bf16.shape[1]
    nt = s // TM
    mod_idx = lambda bi, t: (jnp.where(t == 0, b, bi), 0, 0)
    x_specs, x_args = _stream_specs(xa, 0)
    kern = functools.partial(_proj_kernel, rope_blocks=frozenset(rope_blocks), scales=dict(scales), cn=512,
                             n_stream=len(x_args))
    return pl.pallas_call(
        kern,
        grid=(b, nt),
        in_specs=x_specs + [
            pl.BlockSpec((1, d), lambda bi, t: (0, 0)),
            pl.BlockSpec((None, 1, d), mod_idx),
            pl.BlockSpec((None, 1, d), mod_idx),
            pl.BlockSpec((d, nout), lambda bi, t: (0, 0)),
            pl.BlockSpec((3, TM, LANES), lambda bi, t: (0, jnp.maximum(t - 1, 0), 0)),
        ],
        out_specs=pl.BlockSpec((None, TM, nout), lambda bi, t: (bi, t, 0)),
        out_shape=jax.ShapeDtypeStruct((b, s, nout), BF16),
        compiler_params=_cparams(("arbitrary", "arbitrary")),
        name="proj",
    )(*x_args, gain, sh, sc, w_bf16, rope)


def _rope_tables(seq):
    nf = HEAD_DIM // 4
    inv = ROPE_BASE ** (-jnp.arange(nf, dtype=F32) / nf)
    t = jnp.arange(seq)
    row = (t // GRID_W).astype(F32)
    col = (t % GRID_W).astype(F32)
    lane = np.arange(LANES)
    jj = lane % HEAD_DIM
    axis_is_col = (jj // 32) == 1
    second_half = (jj % 32) >= 16
    f = jj % 16
    pos = jnp.where(axis_is_col[None, :], col[:, None], row[:, None])
    ang = pos * inv[f][None, :]
    c, s = jnp.cos(ang), jnp.sin(ang)
    sa = jnp.where(second_half[None, :], s, 0.0)
    sb = jnp.where(second_half[None, :], 0.0, -s)
    return jnp.stack([c, sa, sb], axis=0)


def _ret_kernel(lg_ref, q_ref, k_ref, v_ref, g_ref, gn_ref, o_ref, accf_ref, accb_ref,
                intra_ref, qd_ref, kd_ref):
    hp = pl.program_id(1)
    c = RET_BLOCK
    s_len = q_ref.shape[0]
    n_chunks = s_len // c
    ctx_chunks = TM // c
    pos = lax.broadcasted_iota(jnp.int32, (c, LANES), 0).astype(F32)
    m0 = lax.broadcasted_iota(jnp.int32, (c, LANES), 1) < HEAD_DIM
    same_head = ((lax.broadcasted_iota(jnp.int32, (LANES, LANES), 0) < HEAD_DIM)
                 == (lax.broadcasted_iota(jnp.int32, (LANES, LANES), 1) < HEAD_DIM))
    rel = (lax.broadcasted_iota(jnp.int32, (c, c), 0) - lax.broadcasted_iota(jnp.int32, (c, c), 1)).astype(F32)
    lgf = [lg_ref[0, hp * 2 + hh] for hh in range(2)]
    lgb = [lg_ref[1, hp * 2 + hh] for hh in range(2)]
    lgf_l = jnp.where(m0, lgf[0], lgf[1])
    lgb_l = jnp.where(m0, lgb[0], lgb[1])
    for hh in range(2):
        intra_ref[0, :, hh * c:(hh + 1) * c] = jnp.where(rel >= 0, jnp.exp(lgf[hh] * jnp.maximum(rel, 0.0)), 0.0)
        intra_ref[1, :, hh * c:(hh + 1) * c] = jnp.where(rel <= 0, jnp.exp(lgb[hh] * jnp.maximum(-rel, 0.0)), 0.0)
    qd_ref[0] = jnp.exp(lgf_l * (pos + 1.0))
    qd_ref[1] = jnp.exp(lgb_l * (c - pos))
    kd_ref[0] = jnp.exp(lgf_l * (c - 1.0 - pos))
    kd_ref[1] = jnp.exp(lgb_l * pos)
    cd = [jnp.exp(lgf_l[0:1] * float(c)), jnp.exp(lgb_l[0:1] * float(c))]

    def chunk(r0, state, d):
        q = q_ref[pl.ds(r0, c), :]
        k = k_ref[pl.ds(r0, c), :]
        v = v_ref[pl.ds(r0, c), :]
        qb = q.astype(BF16)
        kcat = jnp.concatenate([jnp.where(m0, k, 0.0), jnp.where(m0, 0.0, k)], axis=0).astype(BF16)
        vcat = jnp.concatenate([jnp.where(m0, v, 0.0), jnp.where(m0, 0.0, v)], axis=0).astype(BF16)
        s = _dot_nt(qb, kcat) * intra_ref[d]
        out = jnp.dot(s.astype(BF16), vcat, preferred_element_type=F32)
        out = out + jnp.dot(qb, state.astype(BF16), preferred_element_type=F32) * qd_ref[d]
        kv = _dot_tn((k * kd_ref[d]).astype(BF16), v.astype(BF16))
        return out, state * cd[d] + jnp.where(same_head, kv, 0.0)

    def body(i, states):
        sf, sb = states
        rf = pl.multiple_of(i * c, c)
        ib = jnp.where(i < ctx_chunks, ctx_chunks - 1 - i, n_chunks + ctx_chunks - 1 - i)
        rb = pl.multiple_of(ib * c, c)
        of, sf = chunk(rf, sf, 0)
        ob, sb = chunk(rb, sb, 1)
        accf_ref[pl.ds(rf, c), :] = of
        accb_ref[pl.ds(rb, c), :] = ob
        return sf, sb

    z = jnp.zeros((LANES, LANES), F32)
    lax.fori_loop(0, n_chunks, body, (z, z), unroll=True)

    avg = jnp.where(same_head, 1.0 / HEAD_DIM, 0.0).astype(BF16)

    def head_mean(x):
        hi, lo = _split_bf16(x)
        return (jnp.dot(hi, avg, preferred_element_type=F32) + jnp.dot(lo, avg, preferred_element_type=F32))

    def readout(i, carry):
        r0 = pl.multiple_of(i * c, c)
        o = accf_ref[pl.ds(r0, c), :] + accb_ref[pl.ds(r0, c), :]
        dlt = o - head_mean(o)
        var = head_mean(dlt * dlt)
        y = dlt * lax.rsqrt(var + GN_EPS) * gn_ref[...]
        o_ref[pl.ds(r0, c), :] = _silu(g_ref[pl.ds(r0, c), :].astype(F32)) * y
        return carry

    lax.fori_loop(0, n_chunks, readout, 0, unroll=True)


def _retention(p, log_gamma, ret_gn):
    b, s, _ = p.shape
    nb = RET_W // LANES
    blk = lambda off: pl.BlockSpec((None, s, LANES), lambda bi, hp: (bi, 0, off + hp))
    return pl.pallas_call(
        _ret_kernel,
        grid=(b, nb),
        in_specs=[
            pl.BlockSpec(memory_space=pltpu.SMEM),
            blk(0), blk(nb), blk(2 * nb), blk(3 * nb),
            pl.BlockSpec((1, LANES), lambda bi, hp: (0, hp)),
        ],
        out_specs=pl.BlockSpec((None, s, LANES), lambda bi, hp: (bi, 0, hp)),
        out_shape=jax.ShapeDtypeStruct((b, s, RET_W), F32),
        scratch_shapes=[pltpu.VMEM((s, LANES), F32), pltpu.VMEM((s, LANES), F32),
                        pltpu.VMEM((2, RET_BLOCK, 2 * RET_BLOCK), F32),
                        pltpu.VMEM((2, RET_BLOCK, LANES), F32), pltpu.VMEM((2, RET_BLOCK, LANES), F32)],
        compiler_params=_cparams(("arbitrary", "arbitrary")),
        name="retention",
    )(log_gamma, p, p, p, p, ret_gn.reshape(1, RET_W))


LOG2E = 1.4426950408889634


def _softmax_pv(s_list, v_list, extra=None):
    m = None
    for s in s_list:
        for j in range(s.shape[1] // LANES):
            blk = s[:, j * LANES:(j + 1) * LANES]
            m = blk if m is None else jnp.maximum(m, blk)
    m = m.max(axis=-1, keepdims=True)
    if extra is not None:
        m = jnp.maximum(m, extra)
    acc = None
    for s, v in zip(s_list, v_list):
        pv = jnp.dot(jnp.exp2(s - m).astype(BF16), v, preferred_element_type=F32)
        acc = pv if acc is None else acc + pv
    o, den = acc[:, :LANES], acc[:, LANES:]
    if extra is not None:
        den = den + jnp.exp2(extra - m)
    return o / den


def _stage_heads(q_ref, k_ref, v_ref, qm_ref, kb_ref, vb_ref):
    m0 = lax.broadcasted_iota(jnp.int32, (TM, LANES), 1) < HEAD_DIM

    def stage(i, carry):
        r0 = pl.multiple_of(i * TM, TM)
        q = q_ref[pl.ds(r0, TM), :]
        qm_ref[0, pl.ds(r0, TM), :] = jnp.where(m0, q, 0.0).astype(BF16)
        qm_ref[1, pl.ds(r0, TM), :] = jnp.where(m0, 0.0, q).astype(BF16)
        kb_ref[pl.ds(r0, TM), :] = k_ref[pl.ds(r0, TM), :].astype(BF16)
        vb_ref[pl.ds(r0, TM), 0:LANES] = v_ref[pl.ds(r0, TM), :].astype(BF16)
        vb_ref[pl.ds(r0, TM), LANES:2 * LANES] = jnp.ones((TM, LANES), BF16)
        return carry

    lax.fori_loop(0, q_ref.shape[0] // TM, stage, 0)


def _na_kernel(q_ref, k_ref, v_ref, bias_ref, o_ref, qm_ref, kb_ref, vb_ref):
    s_len = q_ref.shape[0]
    rows = (s_len - TM) // GRID_W
    nloc = NA_KH * GRID_W
    _stage_heads(q_ref, k_ref, v_ref, qm_ref, kb_ref, vb_ref)

    kc = kb_ref[0:TM, :]
    vc = vb_ref[0:TM, :]
    outs = [_softmax_pv([_dot_nt(qm_ref[hh, 0:TM, :], kc)], [vc]) for hh in range(2)]
    m0c = lax.broadcasted_iota(jnp.int32, (TM, LANES), 1) < HEAD_DIM
    o_ref[0:TM, :] = jnp.where(m0c, outs[0], outs[1])

    m0 = lax.broadcasted_iota(jnp.int32, (GRID_W, LANES), 1) < HEAD_DIM

    def row_block(r, carry):
        rs = jnp.clip(r - NA_KH // 2, 0, rows - NA_KH)
        pat = r - rs
        q0 = pl.multiple_of(TM + r * GRID_W, GRID_W)
        k0 = pl.multiple_of(TM + rs * GRID_W, GRID_W)
        kl = kb_ref[pl.ds(k0, nloc), :]
        vl = vb_ref[pl.ds(k0, nloc), :]
        kc = kb_ref[0:TM, :]
        vc = vb_ref[0:TM, :]
        q = jnp.concatenate([qm_ref[0, pl.ds(q0, GRID_W), :], qm_ref[1, pl.ds(q0, GRID_W), :]], axis=0)
        s_loc = _dot_nt(q, kl) + bias_ref[pat].reshape(2 * GRID_W, nloc)
        s_ctx = _dot_nt(q, kc)
        res = _softmax_pv([s_loc, s_ctx], [vl, vc])
        o_ref[pl.ds(q0, GRID_W), :] = jnp.where(m0, res[:GRID_W], res[GRID_W:])
        return carry

    lax.fori_loop(0, rows, row_block, 0, unroll=True)


def _na_bias(rpb):
    h = rpb.shape[0]
    qc = np.arange(GRID_W)[:, None]
    kc = np.arange(GRID_W)[None, :]
    win = np.clip(qc - NA_KW // 2, 0, GRID_W - NA_KW)
    valid = (kc >= win) & (kc < win + NA_KW)
    col_off = np.clip(kc - qc + NA_KW - 1, 0, 2 * NA_KW - 2)
    onehot = (col_off[None] == np.arange(2 * NA_KW - 1)[:, None, None]).astype(np.float32)
    cols = jnp.einsum("hrc,cqk->hrqk", rpb.astype(F32), onehot, precision=lax.Precision.HIGHEST)
    cols = jnp.where(valid[None, None], cols * LOG2E, NEG_INF)
    bias = jnp.stack([cols[:, NA_KH - 1 - p:2 * NA_KH - 1 - p] for p in range(NA_KH)])
    bias = bias.transpose(0, 1, 3, 2, 4)
    return bias.reshape(NA_KH, h, GRID_W, NA_KH * GRID_W)


def _neighbourhood(p, bias):
    b, s, _ = p.shape
    nb = NA_W // LANES
    c0 = 4 * RET_W // LANES
    blk = lambda off: pl.BlockSpec((None, s, LANES), lambda bi, hp: (bi, 0, c0 + off + hp))
    return pl.pallas_call(
        _na_kernel,
        grid=(b, nb),
        in_specs=[
            blk(0), blk(nb), blk(2 * nb),
            pl.BlockSpec((NA_KH, 2, GRID_W, NA_KH * GRID_W), lambda bi, hp: (0, hp, 0, 0)),
        ],
        out_specs=pl.BlockSpec((None, s, LANES), lambda bi, hp: (bi, 0, hp)),
        out_shape=jax.ShapeDtypeStruct((b, s, NA_W), F32),
        scratch_shapes=[pltpu.VMEM((2, s, LANES), BF16), pltpu.VMEM((s, LANES), BF16),
                        pltpu.VMEM((s, 2 * LANES), BF16)],
        compiler_params=_cparams(("arbitrary", "arbitrary")),
        name="neighbourhood",
    )(p, p, p, bias)


def _swa_kernel(sink_ref, q_ref, k_ref, v_ref, o_ref, kd_ref, vd_ref):
    kp = pl.program_id(1)
    s_len = q_ref.shape[0]
    seq = s_len - TM
    nblk = seq // SWA_BLOCK
    band = SWA_BLOCK + 2 * SWA_WINDOW
    group = SWA_HEADS // SWA_KV_HEADS
    heads_per_step = 2 * group
    m0t = lax.broadcasted_iota(jnp.int32, (TM, LANES), 1) < HEAD_DIM

    def stage(i, carry):
        r0 = pl.multiple_of(i * TM, TM)
        for src, dst in ((k_ref, kd_ref), (v_ref, vd_ref)):
            x = src[pl.ds(r0, TM), :].astype(F32)
            xr = pltpu.roll(x, HEAD_DIM, 1)
            dst[0, pl.ds(r0, TM), 0:LANES] = jnp.where(m0t, x, xr).astype(BF16)
            dst[1, pl.ds(r0, TM), 0:LANES] = jnp.where(m0t, xr, x).astype(BF16)
        for h in range(2):
            vd_ref[h, pl.ds(r0, TM), LANES:2 * LANES] = jnp.ones((TM, LANES), BF16)
        return carry

    lax.fori_loop(0, s_len // TM, stage, 0)
    o_ref[0:TM, :] = jnp.zeros((TM, o_ref.shape[1]), F32)

    rows = group * SWA_BLOCK
    qi = lax.broadcasted_iota(jnp.int32, (rows, band), 0) % SWA_BLOCK
    ki = lax.broadcasted_iota(jnp.int32, (rows, band), 1)
    m0 = lax.broadcasted_iota(jnp.int32, (SWA_BLOCK, LANES), 1) < HEAD_DIM
    head_of_row = lax.broadcasted_iota(jnp.int32, (rows, 1), 0) // SWA_BLOCK

    def block(i, carry):
        start = jnp.clip((i - 1) * SWA_BLOCK, 0, seq - band)
        k0 = pl.multiple_of(TM + start, SWA_BLOCK)
        q0 = pl.multiple_of(TM + i * SWA_BLOCK, SWA_BLOCK)
        valid = jnp.abs(qi + (i * SWA_BLOCK - start) - ki) <= SWA_WINDOW
        for hh in range(2):
            kb = kd_ref[hh, pl.ds(k0, band), :]
            vb = vd_ref[hh, pl.ds(k0, band), :]
            kc = kd_ref[hh, 0:TM, :]
            vc = vd_ref[hh, 0:TM, :]
            parts = []
            for j in range(group // 2):
                pair = hh * (group // 2) + j
                qp = q_ref[pl.ds(q0, SWA_BLOCK), pair * LANES:(pair + 1) * LANES]
                parts += [jnp.where(m0, qp, 0.0), jnp.where(m0, 0.0, qp)]
            q = jnp.concatenate(parts, axis=0).astype(BF16)
            s = jnp.where(valid, _dot_nt(q, kb), NEG_INF)
            s_ctx = _dot_nt(q, kc)
            h0 = kp * heads_per_step + hh * group
            sink = jnp.full((rows, 1), sink_ref[h0], F32)
            for g in range(1, group):
                sink = jnp.where(head_of_row == g, sink_ref[h0 + g], sink)
            res = _softmax_pv([s, s_ctx], [vb, vc], extra=sink * LOG2E)
            for j in range(group // 2):
                pair = hh * (group // 2) + j
                r0 = 2 * j * SWA_BLOCK
                o_ref[pl.ds(q0, SWA_BLOCK), pair * LANES:(pair + 1) * LANES] = jnp.where(
                    m0, res[r0:r0 + SWA_BLOCK], res[r0 + SWA_BLOCK:r0 + 2 * SWA_BLOCK])
        return carry

    lax.fori_loop(0, nblk, block, 0, unroll=True)


def _swa(p, sink):
    b, s, _ = p.shape
    qw = SWA_HEADS * HEAD_DIM // 2
    kblk = SWA_HEADS * HEAD_DIM // LANES
    vblk = kblk + SWA_KV_HEADS * HEAD_DIM // LANES
    return pl.pallas_call(
        _swa_kernel,
        grid=(b, 2),
        in_specs=[
            pl.BlockSpec(memory_space=pltpu.SMEM),
            pl.BlockSpec((None, s, qw), lambda bi, kp: (bi, 0, kp)),
            pl.BlockSpec((None, s, LANES), lambda bi, kp: (bi, 0, kblk + kp)),
            pl.BlockSpec((None, s, LANES), lambda bi, kp: (bi, 0, vblk + kp)),
        ],
        out_specs=pl.BlockSpec((None, s, qw), lambda bi, kp: (bi, 0, kp)),
        out_shape=jax.ShapeDtypeStruct((b, s, SWA_HEADS * HEAD_DIM), F32),
        scratch_shapes=[pltpu.VMEM((2, s, LANES), BF16), pltpu.VMEM((2, s, 2 * LANES), BF16)],
        compiler_params=_cparams(("arbitrary", "arbitrary")),
        name="swa",
    )(sink, p, p, p)


def _outproj_kernel(*refs, n_mix, n_stream, t0):
    mix_refs = refs[:n_mix]
    x_refs = refs[n_mix + 1:n_mix + 1 + n_stream]
    w_ref = refs[n_mix]
    (g1_ref, gn_ref, sh_ref, sc_ref, wr_ref, br_ref,
     xo_ref, h_ref, ri_ref, rt_ref, cnt_ref, carry_ref) = refs[n_mix + 1 + n_stream:]
    first = (pl.program_id(0) == 0) & (pl.program_id(1) == 0)

    @pl.when(first)
    def _():
        carry_ref[...] = jnp.zeros_like(carry_ref)

    o = None
    off = 0
    for m_ref in mix_refs:
        kw = m_ref.shape[1]
        part = jnp.dot(m_ref[...].astype(BF16), w_ref[off:off + kw, :], preferred_element_type=F32)
        o = part if o is None else o + part
        off += kw
    xn = _stream_tile(x_refs, pl.program_id(1) + t0 > 0) + g1_ref[...] * o
    xo_ref[...] = xn
    h = _rms_mod(xn, gn_ref[...], sh_ref[...], sc_ref[...])
    h_ref[...] = h

    logits = _dot3_split(h, wr_ref[0], wr_ref[1]) + br_ref[...]
    tm = logits.shape[0]
    lane = lax.broadcasted_iota(jnp.int32, (tm, LANES), 1).astype(F32)
    big = 1e9
    gmask = lane < MOE_GROUPS
    mg = jnp.max(jnp.where(gmask, logits, -big), axis=-1, keepdims=True)
    sg = jnp.sum(jnp.where(gmask, jnp.exp(jnp.minimum(logits - mg, 0.0)), 0.0), axis=-1, keepdims=True)
    gw = 1.0 / sg
    gi = jnp.min(jnp.where(gmask & (logits == mg), lane, big), axis=-1, keepdims=True)
    lo = ROUTE_LANE0 + MOE_EPG * gi
    emask = (lane >= lo) & (lane < lo + MOE_EPG)
    l1 = jnp.max(jnp.where(emask, logits, -big), axis=-1, keepdims=True)
    i1 = jnp.min(jnp.where(emask & (logits == l1), lane, big), axis=-1, keepdims=True)
    emask2 = emask & (lane != i1)
    l2 = jnp.max(jnp.where(emask2, logits, -big), axis=-1, keepdims=True)
    i2 = jnp.min(jnp.where(emask2 & (logits == l2), lane, big), axis=-1, keepdims=True)
    e21 = jnp.exp(l2 - l1)
    w1 = gw / (1.0 + e21)
    w2 = gw * e21 / (1.0 + e21)

    oh = jnp.where((lane == i1) | (lane == i2), 1.0, 0.0)
    tri = (lax.broadcasted_iota(jnp.int32, (tm, tm), 0) > lax.broadcasted_iota(jnp.int32, (tm, tm), 1))
    cum = jnp.dot(jnp.where(tri, 1.0, 0.0).astype(BF16), oh.astype(BF16), preferred_element_type=F32) + carry_ref[...]
    r1 = jnp.sum(jnp.where(lane == i1, cum, 0.0), axis=-1, keepdims=True)
    r2 = jnp.sum(jnp.where(lane == i2, cum, 0.0), axis=-1, keepdims=True)
    carry_ref[...] = carry_ref[...] + jnp.sum(oh, axis=0, keepdims=True)
    cnt_ref[...] = carry_ref[...]

    ri = jnp.where(lane == 0, i1 - ROUTE_LANE0, 0.0)
    ri = jnp.where(lane == 1, i2 - ROUTE_LANE0, ri)
    ri = jnp.where(lane == 2, w1, ri)
    ri = jnp.where(lane == 3, w2, ri)
    ri = jnp.where(lane == 4, r1, ri)
    ri = jnp.where(lane == 5, r2, ri)
    ri_ref[...] = ri
    rt_ref[...] = ri.T[0:SUBLANES, :]


def _outproj(mixes, w_bf16, xa, g1, gain2, sh2, sc2, wr, br, t0):
    b, s, d = _stream_shape(xa)
    nt = s // TM - t0
    so = nt * TM
    mod_idx = lambda bi, t: (jnp.where(t + t0 == 0, b, bi), 0, 0)
    tile = lambda wdt: pl.BlockSpec((None, TM, wdt), lambda bi, t: (bi, t + t0, 0))
    otile = lambda wdt: pl.BlockSpec((None, TM, wdt), lambda bi, t: (bi, t, 0))
    const = lambda shape: pl.BlockSpec(shape, lambda bi, t: (0,) * len(shape))
    x_specs, x_args = _stream_specs(xa, t0)
    in_specs = [tile(m.shape[2]) for m in mixes] + [const(w_bf16.shape)] + x_specs + [
        pl.BlockSpec((None, 1, d), mod_idx), const((1, d)),
        pl.BlockSpec((None, 1, d), mod_idx), pl.BlockSpec((None, 1, d), mod_idx),
        const((2, d, LANES)), const((1, LANES)),
    ]
    return pl.pallas_call(
        functools.partial(_outproj_kernel, n_mix=len(mixes), n_stream=len(x_args), t0=t0),
        grid=(b, nt),
        in_specs=in_specs,
        out_specs=[otile(d), otile(d), otile(LANES),
                   pl.BlockSpec((None, None, SUBLANES, TM), lambda bi, t: (bi, t, 0, 0)), const((1, LANES))],
        out_shape=[jax.ShapeDtypeStruct((b, so, d), F32), jax.ShapeDtypeStruct((b, so, d), F32),
                   jax.ShapeDtypeStruct((b, so, LANES), F32), jax.ShapeDtypeStruct((b, nt, SUBLANES, TM), F32),
                   jax.ShapeDtypeStruct((1, LANES), F32)],
        scratch_shapes=[pltpu.VMEM((1, LANES), F32)],
        compiler_params=_cparams(("arbitrary", "arbitrary")),
        name="outproj",
    )(*mixes, w_bf16, *x_args, g1, gain2, sh2, sc2, wr, br)


SUBLANES = 8


def _to_token_tiles(x):
    return x.reshape(x.shape[0], SUBLANES, x.shape[1] // SUBLANES)


def _from_token_tiles(x3):
    return x3.reshape(x3.shape[0], x3.shape[1] * x3.shape[2])


PAD_BITS = tuple(1 << k for k in reversed(range(MOE_ROWS.bit_length() - 1)))


def _dispatch_kernel(dest_ref, pads_ref, h_ref, xs_ref, hbuf, zbuf, sem, zsem):
    nt = pl.num_programs(1)
    step = pl.program_id(0) * nt + pl.program_id(1)
    last = pl.num_programs(0) * nt - 1
    slot = step % 2

    def zero_pads(wait):
        def expert(e, c):
            pos = pads_ref[0, e]
            n = pads_ref[1, e]
            for bit in PAD_BITS:
                take = (n & bit) != 0
                cp = pltpu.make_async_copy(zbuf.at[pl.ds(0, bit)], xs_ref.at[pl.ds(pos, bit)], zsem)

                @pl.when(take)
                def _():
                    cp.wait() if wait else cp.start()

                pos = pos + jnp.where(take, bit, 0)
            return c
        lax.fori_loop(0, MOE_EXPERTS, expert, 0)

        def tail(j, c):
            cp = pltpu.make_async_copy(
                zbuf, xs_ref.at[pl.ds(pads_ref[0, MOE_EXPERTS] + j * PAD_BITS[0], PAD_BITS[0])], zsem)
            cp.wait() if wait else cp.start()
            return c
        lax.fori_loop(0, pads_ref[1, MOE_EXPERTS], tail, 0)

    @pl.when(step == 0)
    def _():
        zbuf[...] = jnp.zeros_like(zbuf)
        zero_pads(wait=False)
    hbuf[slot] = _to_token_tiles(h_ref[...])

    def copy(sl, i, dst_row):
        return pltpu.make_async_copy(hbuf.at[sl, i], xs_ref.at[dst_row], sem.at[sl])

    for i in range(TM):
        copy(slot, i, dest_ref[step, i]).start()
        copy(slot, i, dest_ref[step, TM + i]).start()

    def drain(sl):
        for _ in range(2):
            pltpu.make_async_copy(hbuf.at[sl], xs_ref.at[pl.ds(0, TM)], sem.at[sl]).wait()

    @pl.when(step > 0)
    def _():
        drain(1 - slot)

    @pl.when(step == last)
    def _():
        drain(slot)
        zero_pads(wait=True)


def _dispatch(dest, pads, h2, n_pad):
    b, s, d = h2.shape
    nt = s // TM
    tile = (SUBLANES, d // SUBLANES)
    return pl.pallas_call(
        _dispatch_kernel,
        grid_spec=pltpu.PrefetchScalarGridSpec(
            num_scalar_prefetch=2,
            grid=(b, nt),
            in_specs=[pl.BlockSpec((None, TM, d), lambda bi, t, dr, pd: (bi, t, 0))],
            out_specs=pl.BlockSpec(memory_space=pl.ANY),
            scratch_shapes=[pltpu.VMEM((2, TM) + tile, F32), pltpu.VMEM((PAD_BITS[0],) + tile, F32),
                            pltpu.SemaphoreType.DMA((2,)), pltpu.SemaphoreType.DMA],
        ),
        out_shape=jax.ShapeDtypeStruct((n_pad,) + tile, F32),
        compiler_params=_cparams(("arbitrary", "arbitrary")),
        name="dispatch",
    )(dest, pads, h2)


def _mlp_kernel(be_ref, nu_ref, x_ref, wgu_ref, wd_ref, y_ref, wgu_b, wd_b):
    i = pl.program_id(0)
    prev = be_ref[jnp.maximum(i - 1, 0)]
    used = i < nu_ref[0]

    @pl.when(used & ((i == 0) | (be_ref[i] != prev)))
    def _():
        wgu_b[...] = wgu_ref[...].astype(BF16)
        wd_b[...] = wd_ref[...].astype(BF16)

    @pl.when(used)
    def _():
        x = _from_token_tiles(x_ref[...])
        gu = jnp.dot(x.astype(BF16), wgu_b[...], preferred_element_type=F32)
        act = _silu(gu[:, :MOE_FF]) * gu[:, MOE_FF:]
        y_ref[...] = _to_token_tiles(jnp.dot(act.astype(BF16), wd_b[...], preferred_element_type=F32))

    @pl.when(jnp.logical_not(used))
    def _():
        y_ref[...] = jnp.zeros_like(y_ref)


def _expert_mlp(block_e, n_used, xs, w_gu, w_down, layer):
    n_pad, sub, dl = xs.shape
    d = sub * dl
    ff2 = w_gu.shape[-1]
    slots = pl.BlockSpec((MOE_ROWS, sub, dl), lambda i, be, nu: (i, 0, 0))
    used_slots = pl.BlockSpec((MOE_ROWS, sub, dl), lambda i, be, nu: (jnp.minimum(i, nu[0] - 1), 0, 0))
    return pl.pallas_call(
        _mlp_kernel,
        grid_spec=pltpu.PrefetchScalarGridSpec(
            num_scalar_prefetch=2,
            grid=(n_pad // MOE_ROWS,),
            in_specs=[
                used_slots,
                pl.BlockSpec((None, None, d, ff2), lambda i, be, nu: (layer, be[i], 0, 0)),
                pl.BlockSpec((None, None, ff2 // 2, d), lambda i, be, nu: (layer, be[i], 0, 0)),
            ],
            out_specs=slots,
            scratch_shapes=[pltpu.VMEM((d, ff2), BF16), pltpu.VMEM((ff2 // 2, d), BF16)],
        ),
        out_shape=jax.ShapeDtypeStruct(xs.shape, F32),
        compiler_params=_cparams(("arbitrary",)),
        name="expert_mlp",
    )(block_e, n_used, xs, w_gu, w_down)


def _combine_kernel(dest_ref, ys_ref, x_ref, ri_ref, g2_ref, fg_ref, o_ref, buf, sem, *, final):
    bi = pl.program_id(0)
    t = pl.program_id(1)
    nt = pl.num_programs(1)
    step = bi * nt + t
    total = pl.num_programs(0) * nt

    def copy(src_row, slot, k, i):
        return pltpu.make_async_copy(ys_ref.at[src_row], buf.at[slot, k * TM + i], sem.at[slot])

    def issue(st, slot):
        for i in range(TM):
            copy(dest_ref[st, i], slot, 0, i).start()
            copy(dest_ref[st, TM + i], slot, 1, i).start()

    slot = step % 2

    @pl.when(step == 0)
    def _():
        issue(0, 0)

    @pl.when(step + 1 < total)
    def _():
        issue(step + 1, 1 - slot)

    pltpu.make_async_copy(ys_ref.at[pl.ds(0, 2 * TM)], buf.at[slot], sem.at[slot]).wait()

    lane = lax.broadcasted_iota(jnp.int32, (TM, LANES), 1)
    ri = ri_ref[...]
    w1 = jnp.sum(jnp.where(lane == 2, ri, 0.0), axis=-1, keepdims=True)
    w2 = jnp.sum(jnp.where(lane == 3, ri, 0.0), axis=-1, keepdims=True)
    y = (_from_token_tiles(buf[slot, 0:TM]) * w1 + _from_token_tiles(buf[slot, TM:2 * TM]) * w2)
    xn = x_ref[...] + g2_ref[...] * y
    if final:
        ms = jnp.mean(xn * xn, axis=-1, keepdims=True)
        xn = xn * lax.rsqrt(ms + NORM_EPS) * fg_ref[...]
    o_ref[...] = xn


def _combine(dest, ys, xa, rinfo, g2, final_g, has_ctx, final):
    b, s, d = xa.shape
    nt = s // TM
    mod_idx = lambda bi, t, dr: (jnp.where(t == 0, b, bi) if has_ctx else bi, 0, 0)
    tile = lambda wdt: pl.BlockSpec((None, TM, wdt), lambda bi, t, dr: (bi, t, 0))
    out_spec = tile(d)
    out_shape = jax.ShapeDtypeStruct((b, s, d), F32)
    return pl.pallas_call(
        functools.partial(_combine_kernel, final=final),
        grid_spec=pltpu.PrefetchScalarGridSpec(
            num_scalar_prefetch=1,
            grid=(b, nt),
            in_specs=[
                pl.BlockSpec(memory_space=pl.ANY),
                tile(d), tile(LANES),
                pl.BlockSpec((None, 1, d), mod_idx),
                pl.BlockSpec((1, d), lambda bi, t, dr: (0, 0)),
            ],
            out_specs=out_spec,
            scratch_shapes=[pltpu.VMEM((2, 2 * TM, SUBLANES, d // SUBLANES), F32),
                            pltpu.SemaphoreType.DMA((2,))],
        ),
        out_shape=out_shape,
        compiler_params=_cparams(("arbitrary", "arbitrary")),
        name="combine",
    )(dest, ys, xa, rinfo, g2, final_g)


def _moe(h2, rt, counts, w_gu, w_down, layer):
    b, s, d = h2.shape
    nt = s // TM
    r = rt.reshape(b * nt, SUBLANES, TM)
    fields = lambda i: jnp.concatenate([r[:, i], r[:, i + 1]], axis=1).astype(jnp.int32)
    e = fields(0)
    rank = fields(4)
    cnt = counts[0, ROUTE_LANE0:ROUTE_LANE0 + MOE_EXPERTS].astype(jnp.int32)
    padded = (cnt + MOE_ROWS - 1) // MOE_ROWS * MOE_ROWS
    ends = jnp.cumsum(padded)
    starts = ends - padded
    eids = jnp.arange(MOE_EXPERTS, dtype=jnp.int32)
    dest = jnp.sum(jnp.where(e[..., None] == eids, starts, 0), axis=-1) + rank
    n_assign = b * nt * TM * 2
    n_blocks = (n_assign + MOE_EXPERTS * (MOE_ROWS - 1)) // MOE_ROWS + 1
    n_pad = n_blocks * MOE_ROWS
    blk_row = jnp.arange(n_blocks, dtype=jnp.int32) * MOE_ROWS
    block_e = jnp.minimum(jnp.sum((blk_row[:, None] >= ends[None, :]).astype(jnp.int32), axis=1),
                          MOE_EXPERTS - 1)
    n_used = (ends[-1] // MOE_ROWS).astype(jnp.int32).reshape(1)
    pads = jnp.stack([jnp.append(starts + cnt, ends[-1]),
                      jnp.append(padded - cnt, (n_pad - ends[-1]) // PAD_BITS[0])]).astype(jnp.int32)
    xs = _dispatch(dest, pads, h2, n_pad)
    ys = _expert_mlp(block_e, n_used, xs, w_gu, w_down, layer)
    return dest, ys


def _router_weights(wg, bg, we, be):
    d = wg.shape[0]
    pad = LANES - MOE_GROUPS - MOE_EXPERTS
    assert ROUTE_LANE0 == MOE_GROUPS
    wr = jnp.concatenate([wg.astype(F32), we.astype(F32), jnp.zeros((d, pad), F32)], axis=1)
    br = jnp.concatenate([bg.astype(F32), be.astype(F32), jnp.zeros((pad,), F32)]).reshape(1, LANES)
    return jnp.stack(_split_bf16(wr)), br


def kernel(x, c, ctx, c_ctx, ada_w, ada_b, norm_g, final_g, ab_w_in, ab_w_out, ret_decay, ret_gn, na_rpb,
           swa_w_in, swa_w_out, swa_sink, router_g_w, router_g_b, router_e_w, router_e_b,
           expert_w_gu, expert_w_down):
    b, seq, d = x.shape
    assert ctx.shape[1] == TM and seq % TM == 0 and d == D_MODEL
    xa = (ctx, x)
    rope = _rope_tables(seq)

    cvec = jnp.concatenate([c, c_ctx[None, :], jnp.zeros((7, d), F32)], axis=0)
    mod = _adaln(cvec, ada_w, ada_b)
    mod = mod.reshape(DEPTH, b + 8, 6, 1, d)[:, :b + 1].transpose(0, 2, 1, 3, 4)

    for layer in range(DEPTH):
        last = layer == DEPTH - 1
        sh1, sc1, g1, sh2, sc2, g2 = (mod[layer, i] for i in range(6))
        gain1 = norm_g[layer, 0].reshape(1, d)
        gain2 = norm_g[layer, 1].reshape(1, d)
        j = layer // 2
        if layer % 2 == 0:
            nb = RET_W // LANES
            qk_scale = HEAD_DIM ** -0.5
            scales = {blk: qk_scale for blk in range(nb, 2 * nb)}
            scales.update({blk: qk_scale * LOG2E for blk in range(4 * nb, 5 * nb)})
            p = _proj(xa, gain1, sh1, sc1, ab_w_in[j].astype(BF16), rope,
                      rope_blocks=range(0, 2 * nb), scales=scales)
            log_gamma = jnp.log1p(-jnp.exp2(-ret_decay[j].astype(F32)))
            mixes = [_retention(p, log_gamma, ret_gn[j]), _neighbourhood(p, _na_bias(na_rpb[j]))]
            w_out = ab_w_out[j]
        else:
            qb = SWA_HEADS * HEAD_DIM // LANES
            kb = SWA_KV_HEADS * HEAD_DIM // LANES
            p = _proj(xa, gain1, sh1, sc1, swa_w_in[j].astype(BF16), rope,
                      rope_blocks=range(0, qb + kb),
                      scales={blk: HEAD_DIM ** -0.5 * LOG2E for blk in range(qb)})
            mixes = [_swa(p, swa_sink[j].astype(F32))]
            w_out = swa_w_out[j]
        t0 = 1 if last else 0
        wr, br = _router_weights(router_g_w[layer], router_g_b[layer], router_e_w[layer], router_e_b[layer])
        xa, h2, rinfo, rt, counts = _outproj(mixes, w_out.astype(BF16), xa, g1, gain2, sh2, sc2, wr, br, t0)
        dest, ys = _moe(h2, rt, counts, expert_w_gu, expert_w_down, layer)
        xa = _combine(dest, ys, xa, rinfo, g2, final_g.reshape(1, d), not last, last)
    return xa
```

```python
import functools

import numpy as np
import jax
import jax.numpy as jnp
from jax import lax
from jax.experimental import pallas as pl
from jax.experimental.pallas import tpu as pltpu

F32 = jnp.float32
BF16 = jnp.bfloat16

D_MODEL = 1024
DEPTH = 2
GRID_W = 64
HEAD_DIM = 64
RET_HEADS = 8
NA_HEADS = 8
RET_W = 512
NA_W = 512
AB_IN = 4 * RET_W + 3 * NA_W
RET_CHUNK = 128
GN_EPS = 1e-5
NA_KH = 8
NA_KW = 16
SWA_HEADS = 16
SWA_KV_HEADS = 4
SWA_WINDOW = 128
SWA_BLOCK = 128
SWA_IN = (SWA_HEADS + 2 * SWA_KV_HEADS) * HEAD_DIM
ROPE_BASE = 10000.0
MOE_GROUPS = 4
MOE_EPG = 8
MOE_EXPERTS = 32
MOE_FF = 512
NORM_EPS = 1e-6
NEG_INF = -1e30

LANES = 128
TM = 256
RET_BLOCK = 256
MOE_ROWS = 512
ROUTE_LANE0 = 4
ROUTE_ROWS = 48
VMEM_LIMIT = 56 * 1024 * 1024


def _cparams(sem, vmem=VMEM_LIMIT, flags=None):
    return pltpu.CompilerParams(dimension_semantics=sem, vmem_limit_bytes=vmem, flags=flags)


def _split_bf16(a):
    hi = a.astype(BF16)
    lo = (a - hi.astype(F32)).astype(BF16)
    return hi, lo


def _dot3_split(a, bh, bl):
    ah, al = _split_bf16(a)
    d = lambda x, y: jnp.dot(x, y, preferred_element_type=F32)
    return d(ah, bh) + (d(ah, bl) + d(al, bh))


def _dot3(a, b):
    return _dot3_split(a, *_split_bf16(b))


def _dot_nt(a, b):
    return lax.dot_general(a, b, (((1,), (1,)), ((), ())), preferred_element_type=F32)


def _dot_tn(a, b):
    return lax.dot_general(a, b, (((0,), (0,)), ((), ())), preferred_element_type=F32)


def _silu(x):
    return x / (1.0 + jnp.exp(-x))


def _adaln_kernel(c_ref, w_ref, b_ref, o_ref):
    o_ref[...] = _dot3(_silu(c_ref[...]), w_ref[...]) + b_ref[...]


def _adaln(cvec, ada_w, ada_b):
    depth, d, n6 = ada_w.shape
    rows = cvec.shape[0]
    tn = 1024
    return pl.pallas_call(
        _adaln_kernel,
        grid=(depth, n6 // tn),
        in_specs=[
            pl.BlockSpec((rows, d), lambda l, j: (0, 0)),
            pl.BlockSpec((None, d, tn), lambda l, j: (l, 0, j)),
            pl.BlockSpec((None, 1, tn), lambda l, j: (l, 0, j)),
        ],
        out_specs=pl.BlockSpec((None, rows, tn), lambda l, j: (l, 0, j)),
        out_shape=jax.ShapeDtypeStruct((depth, rows, n6), F32),
        compiler_params=_cparams(("arbitrary", "arbitrary")),
        name="adaln",
    )(cvec, ada_w, ada_b.reshape(depth, 1, n6))


def _rms_mod(x, g, sh, sc):
    ms = jnp.mean(x * x, axis=-1, keepdims=True)
    return (x * lax.rsqrt(ms + NORM_EPS) * g) * (1.0 + sc) + sh


def _stream_tile(refs, is_lat):
    if len(refs) == 1:
        return refs[0][...]
    return jnp.where(is_lat, refs[1][...], refs[0][...])


def _stream_specs(stream, t0, extra_args=0):
    def im(f):
        return (lambda bi, t, *_: f(bi, t + t0))
    if not isinstance(stream, tuple):
        return [pl.BlockSpec((None, TM, stream.shape[2]), im(lambda bi, t: (bi, t, 0)))], [stream]
    ctx, x = stream
    d = x.shape[2]
    return ([pl.BlockSpec((None, TM, d), im(lambda bi, t: (bi, 0, 0))),
             pl.BlockSpec((None, TM, d), im(lambda bi, t: (bi, jnp.maximum(t - 1, 0), 0)))], [ctx, x])


def _stream_shape(stream):
    if not isinstance(stream, tuple):
        return stream.shape
    ctx, x = stream
    return (x.shape[0], ctx.shape[1] + x.shape[1], x.shape[2])


def _proj_kernel(*refs, rope_blocks, scales, cn, n_stream):
    x_refs = refs[:n_stream]
    g_ref, sh_ref, sc_ref, w_ref, rope_ref, o_ref = refs[n_stream:]
    is_lat = pl.program_id(1) > 0
    hb = _rms_mod(_stream_tile(x_refs, is_lat), g_ref[...], sh_ref[...], sc_ref[...]).astype(BF16)
    nout = w_ref.shape[1]
    for c in range(nout // cn):
        o = jnp.dot(hb, w_ref[:, c * cn:(c + 1) * cn], preferred_element_type=F32)
        for s in range(cn // LANES):
            blk = c * (cn // LANES) + s
            ob = o[:, s * LANES:(s + 1) * LANES]
            if blk in rope_blocks:
                r = (ob * rope_ref[0] + pltpu.roll(ob, 16, 1) * rope_ref[1]
                     + pltpu.roll(ob, LANES - 16, 1) * rope_ref[2])
                ob = jnp.where(is_lat, r, ob)
            if blk in scales:
                ob = ob * scales[blk]
            o_ref[:, blk * LANES:(blk + 1) * LANES] = ob.astype(o_ref.dtype)


def _proj(xa, gain, sh, sc, w_bf16, rope, rope_blocks, scales):
    b, s, d = _stream_shape(xa)
    nout = w_bf16.shape[1]
    nt = s // TM
    mod_idx = lambda bi, t: (jnp.where(t == 0, b, bi), 0, 0)
    x_specs, x_args = _stream_specs(xa, 0)
    kern = functools.partial(_proj_kernel, rope_blocks=frozenset(rope_blocks), scales=dict(scales), cn=512,
                             n_stream=len(x_args))
    return pl.pallas_call(
        kern,
        grid=(b, nt),
        in_specs=x_specs + [
            pl.BlockSpec((1, d), lambda bi, t: (0, 0)),
            pl.BlockSpec((None, 1, d), mod_idx),
            pl.BlockSpec((None, 1, d), mod_idx),
            pl.BlockSpec((d, nout), lambda bi, t: (0, 0)),
            pl.BlockSpec((3, TM, LANES), lambda bi, t: (0, jnp.maximum(t - 1, 0), 0)),
        ],
        out_specs=pl.BlockSpec((None, TM, nout), lambda bi, t: (bi, t, 0)),
        out_shape=jax.ShapeDtypeStruct((b, s, nout), BF16),
        compiler_params=_cparams(("arbitrary", "arbitrary")),
        name="proj",
    )(*x_args, gain, sh, sc, w_bf16, rope)


def _rope_tables(seq):
    nf = HEAD_DIM // 4
    inv = ROPE_BASE ** (-jnp.arange(nf, dtype=F32) / nf)
    t = jnp.arange(seq)
    row = (t // GRID_W).astype(F32)
    col = (t % GRID_W).astype(F32)
    lane = np.arange(LANES)
    jj = lane % HEAD_DIM
    axis_is_col = (jj // 32) == 1
    second_half = (jj % 32) >= 16
    f = jj % 16
    pos = jnp.where(axis_is_col[None, :], col[:, None], row[:, None])
    ang = pos * inv[f][None, :]
    c, s = jnp.cos(ang), jnp.sin(ang)
    sa = jnp.where(second_half[None, :], s, 0.0)
    sb = jnp.where(second_half[None, :], 0.0, -s)
    return jnp.stack([c, sa, sb], axis=0)


def _ret_kernel(lg_ref, q_ref, k_ref, v_ref, g_ref, gn_ref, o_ref, accf_ref, accb_ref,
                intra_ref, qd_ref, kd_ref):
    hp = pl.program_id(1)
    c = RET_BLOCK
    s_len = q_ref.shape[0]
    n_chunks = s_len // c
    ctx_chunks = TM // c
    pos = lax.broadcasted_iota(jnp.int32, (c, LANES), 0).astype(F32)
    m0 = lax.broadcasted_iota(jnp.int32, (c, LANES), 1) < HEAD_DIM
    same_head = ((lax.broadcasted_iota(jnp.int32, (LANES, LANES), 0) < HEAD_DIM)
                 == (lax.broadcasted_iota(jnp.int32, (LANES, LANES), 1) < HEAD_DIM))
    rel = (lax.broadcasted_iota(jnp.int32, (c, c), 0) - lax.broadcasted_iota(jnp.int32, (c, c), 1)).astype(F32)
    lgf = [lg_ref[0, hp * 2 + hh] for hh in range(2)]
    lgb = [lg_ref[1, hp * 2 + hh] for hh in range(2)]
    lgf_l = jnp.where(m0, lgf[0], lgf[1])
    lgb_l = jnp.where(m0, lgb[0], lgb[1])
    for hh in range(2):
        intra_ref[0, :, hh * c:(hh + 1) * c] = jnp.where(rel >= 0, jnp.exp(lgf[hh] * jnp.maximum(rel, 0.0)), 0.0)
        intra_ref[1, :, hh * c:(hh + 1) * c] = jnp.where(rel <= 0, jnp.exp(lgb[hh] * jnp.maximum(-rel, 0.0)), 0.0)
    qd_ref[0] = jnp.exp(lgf_l * (pos + 1.0))
    qd_ref[1] = jnp.exp(lgb_l * (c - pos))
    kd_ref[0] = jnp.exp(lgf_l * (c - 1.0 - pos))
    kd_ref[1] = jnp.exp(lgb_l * pos)
    cd = [jnp.exp(lgf_l[0:1] * float(c)), jnp.exp(lgb_l[0:1] * float(c))]

    def chunk(r0, state, d):
        q = q_ref[pl.ds(r0, c), :]
        k = k_ref[pl.ds(r0, c), :]
        v = v_ref[pl.ds(r0, c), :]
        qb = q.astype(BF16)
        kcat = jnp.concatenate([jnp.where(m0, k, 0.0), jnp.where(m0, 0.0, k)], axis=0).astype(BF16)
        vcat = jnp.concatenate([jnp.where(m0, v, 0.0), jnp.where(m0, 0.0, v)], axis=0).astype(BF16)
        s = _dot_nt(qb, kcat) * intra_ref[d]
        out = jnp.dot(s.astype(BF16), vcat, preferred_element_type=F32)
        out = out + jnp.dot(qb, state.astype(BF16), preferred_element_type=F32) * qd_ref[d]
        kv = _dot_tn((k * kd_ref[d]).astype(BF16), v.astype(BF16))
        return out, state * cd[d] + jnp.where(same_head, kv, 0.0)

    def body(i, states):
        sf, sb = states
        rf = pl.multiple_of(i * c, c)
        ib = jnp.where(i < ctx_chunks, ctx_chunks - 1 - i, n_chunks + ctx_chunks - 1 - i)
        rb = pl.multiple_of(ib * c, c)
        of, sf = chunk(rf, sf, 0)
        ob, sb = chunk(rb, sb, 1)
        accf_ref[pl.ds(rf, c), :] = of
        accb_ref[pl.ds(rb, c), :] = ob
        return sf, sb

    z = jnp.zeros((LANES, LANES), F32)
    lax.fori_loop(0, n_chunks, body, (z, z), unroll=True)

    avg = jnp.where(same_head, 1.0 / HEAD_DIM, 0.0).astype(BF16)

    def head_mean(x):
        hi, lo = _split_bf16(x)
        return (jnp.dot(hi, avg, preferred_element_type=F32) + jnp.dot(lo, avg, preferred_element_type=F32))

    def readout(i, carry):
        r0 = pl.multiple_of(i * c, c)
        o = accf_ref[pl.ds(r0, c), :] + accb_ref[pl.ds(r0, c), :]
        dlt = o - head_mean(o)
        var = head_mean(dlt * dlt)
        y = dlt * lax.rsqrt(var + GN_EPS) * gn_ref[...]
        o_ref[pl.ds(r0, c), :] = _silu(g_ref[pl.ds(r0, c), :].astype(F32)) * y
        return carry

    lax.fori_loop(0, n_chunks, readout, 0, unroll=True)


def _retention(p, log_gamma, ret_gn):
    b, s, _ = p.shape
    nb = RET_W // LANES
    blk = lambda off: pl.BlockSpec((None, s, LANES), lambda bi, hp: (bi, 0, off + hp))
    return pl.pallas_call(
        _ret_kernel,
        grid=(b, nb),
        in_specs=[
            pl.BlockSpec(memory_space=pltpu.SMEM),
            blk(0), blk(nb), blk(2 * nb), blk(3 * nb),
            pl.BlockSpec((1, LANES), lambda bi, hp: (0, hp)),
        ],
        out_specs=pl.BlockSpec((None, s, LANES), lambda bi, hp: (bi, 0, hp)),
        out_shape=jax.ShapeDtypeStruct((b, s, RET_W), F32),
        scratch_shapes=[pltpu.VMEM((s, LANES), F32), pltpu.VMEM((s, LANES), F32),
                        pltpu.VMEM((2, RET_BLOCK, 2 * RET_BLOCK), F32),
                        pltpu.VMEM((2, RET_BLOCK, LANES), F32), pltpu.VMEM((2, RET_BLOCK, LANES), F32)],
        compiler_params=_cparams(("arbitrary", "arbitrary")),
        name="retention",
    )(log_gamma, p, p, p, p, ret_gn.reshape(1, RET_W))


LOG2E = 1.4426950408889634


def _softmax_pv(s_list, v_list, extra=None):
    m = None
    for s in s_list:
        for j in range(s.shape[1] // LANES):
            blk = s[:, j * LANES:(j + 1) * LANES]
            m = blk if m is None else jnp.maximum(m, blk)
    m = m.max(axis=-1, keepdims=True)
    if extra is not None:
        m = jnp.maximum(m, extra)
    acc = None
    for s, v in zip(s_list, v_list):
        pv = jnp.dot(jnp.exp2(s - m).astype(BF16), v, preferred_element_type=F32)
        acc = pv if acc is None else acc + pv
    o, den = acc[:, :LANES], acc[:, LANES:]
    if extra is not None:
        den = den + jnp.exp2(extra - m)
    return o / den


def _stage_heads(q_ref, k_ref, v_ref, qm_ref, kb_ref, vb_ref):
    m0 = lax.broadcasted_iota(jnp.int32, (TM, LANES), 1) < HEAD_DIM

    def stage(i, carry):
        r0 = pl.multiple_of(i * TM, TM)
        q = q_ref[pl.ds(r0, TM), :]
        qm_ref[0, pl.ds(r0, TM), :] = jnp.where(m0, q, 0.0).astype(BF16)
        qm_ref[1, pl.ds(r0, TM), :] = jnp.where(m0, 0.0, q).astype(BF16)
        kb_ref[pl.ds(r0, TM), :] = k_ref[pl.ds(r0, TM), :].astype(BF16)
        vb_ref[pl.ds(r0, TM), 0:LANES] = v_ref[pl.ds(r0, TM), :].astype(BF16)
        vb_ref[pl.ds(r0, TM), LANES:2 * LANES] = jnp.ones((TM, LANES), BF16)
        return carry

    lax.fori_loop(0, q_ref.shape[0] // TM, stage, 0)


def _na_kernel(q_ref, k_ref, v_ref, bias_ref, o_ref, qm_ref, kb_ref, vb_ref):
    s_len = q_ref.shape[0]
    rows = (s_len - TM) // GRID_W
    nloc = NA_KH * GRID_W
    _stage_heads(q_ref, k_ref, v_ref, qm_ref, kb_ref, vb_ref)

    kc = kb_ref[0:TM, :]
    vc = vb_ref[0:TM, :]
    outs = [_softmax_pv([_dot_nt(qm_ref[hh, 0:TM, :], kc)], [vc]) for hh in range(2)]
    m0c = lax.broadcasted_iota(jnp.int32, (TM, LANES), 1) < HEAD_DIM
    o_ref[0:TM, :] = jnp.where(m0c, outs[0], outs[1])

    m0 = lax.broadcasted_iota(jnp.int32, (GRID_W, LANES), 1) < HEAD_DIM

    def row_block(r, carry):
        rs = jnp.clip(r - NA_KH // 2, 0, rows - NA_KH)
        pat = r - rs
        q0 = pl.multiple_of(TM + r * GRID_W, GRID_W)
        k0 = pl.multiple_of(TM + rs * GRID_W, GRID_W)
        kl = kb_ref[pl.ds(k0, nloc), :]
        vl = vb_ref[pl.ds(k0, nloc), :]
        kc = kb_ref[0:TM, :]
        vc = vb_ref[0:TM, :]
        q = jnp.concatenate([qm_ref[0, pl.ds(q0, GRID_W), :], qm_ref[1, pl.ds(q0, GRID_W), :]], axis=0)
        s_loc = _dot_nt(q, kl) + bias_ref[pat].reshape(2 * GRID_W, nloc)
        s_ctx = _dot_nt(q, kc)
        res = _softmax_pv([s_loc, s_ctx], [vl, vc])
        o_ref[pl.ds(q0, GRID_W), :] = jnp.where(m0, res[:GRID_W], res[GRID_W:])
        return carry

    lax.fori_loop(0, rows, row_block, 0, unroll=True)


def _na_bias(rpb):
    h = rpb.shape[0]
    qc = np.arange(GRID_W)[:, None]
    kc = np.arange(GRID_W)[None, :]
    win = np.clip(qc - NA_KW // 2, 0, GRID_W - NA_KW)
    valid = (kc >= win) & (kc < win + NA_KW)
    col_off = np.clip(kc - qc + NA_KW - 1, 0, 2 * NA_KW - 2)
    onehot = (col_off[None] == np.arange(2 * NA_KW - 1)[:, None, None]).astype(np.float32)
    cols = jnp.einsum("hrc,cqk->hrqk", rpb.astype(F32), onehot, precision=lax.Precision.HIGHEST)
    cols = jnp.where(valid[None, None], cols * LOG2E, NEG_INF)
    bias = jnp.stack([cols[:, NA_KH - 1 - p:2 * NA_KH - 1 - p] for p in range(NA_KH)])
    bias = bias.transpose(0, 1, 3, 2, 4)
    return bias.reshape(NA_KH, h, GRID_W, NA_KH * GRID_W)


def _neighbourhood(p, bias):
    b, s, _ = p.shape
    nb = NA_W // LANES
    c0 = 4 * RET_W // LANES
    blk = lambda off: pl.BlockSpec((None, s, LANES), lambda bi, hp: (bi, 0, c0 + off + hp))
    return pl.pallas_call(
        _na_kernel,
        grid=(b, nb),
        in_specs=[
            blk(0), blk(nb), blk(2 * nb),
            pl.BlockSpec((NA_KH, 2, GRID_W, NA_KH * GRID_W), lambda bi, hp: (0, hp, 0, 0)),
        ],
        out_specs=pl.BlockSpec((None, s, LANES), lambda bi, hp: (bi, 0, hp)),
        out_shape=jax.ShapeDtypeStruct((b, s, NA_W), F32),
        scratch_shapes=[pltpu.VMEM((2, s, LANES), BF16), pltpu.VMEM((s, LANES), BF16),
                        pltpu.VMEM((s, 2 * LANES), BF16)],
        compiler_params=_cparams(("arbitrary", "arbitrary")),
        name="neighbourhood",
    )(p, p, p, bias)


def _swa_kernel(sink_ref, q_ref, k_ref, v_ref, o_ref, kd_ref, vd_ref):
    kp = pl.program_id(1)
    s_len = q_ref.shape[0]
    seq = s_len - TM
    nblk = seq // SWA_BLOCK
    band = SWA_BLOCK + 2 * SWA_WINDOW
    group = SWA_HEADS // SWA_KV_HEADS
    heads_per_step = 2 * group
    m0t = lax.broadcasted_iota(jnp.int32, (TM, LANES), 1) < HEAD_DIM

    def stage(i, carry):
        r0 = pl.multiple_of(i * TM, TM)
        for src, dst in ((k_ref, kd_ref), (v_ref, vd_ref)):
            x = src[pl.ds(r0, TM), :].astype(F32)
            xr = pltpu.roll(x, HEAD_DIM, 1)
            dst[0, pl.ds(r0, TM), 0:LANES] = jnp.where(m0t, x, xr).astype(BF16)
            dst[1, pl.ds(r0, TM), 0:LANES] = jnp.where(m0t, xr, x).astype(BF16)
        for h in range(2):
            vd_ref[h, pl.ds(r0, TM), LANES:2 * LANES] = jnp.ones((TM, LANES), BF16)
        return carry

    lax.fori_loop(0, s_len // TM, stage, 0)
    o_ref[0:TM, :] = jnp.zeros((TM, o_ref.shape[1]), F32)

    rows = group * SWA_BLOCK
    qi = lax.broadcasted_iota(jnp.int32, (rows, band), 0) % SWA_BLOCK
    ki = lax.broadcasted_iota(jnp.int32, (rows, band), 1)
    m0 = lax.broadcasted_iota(jnp.int32, (SWA_BLOCK, LANES), 1) < HEAD_DIM
    head_of_row = lax.broadcasted_iota(jnp.int32, (rows, 1), 0) // SWA_BLOCK

    def block(i, carry):
        start = jnp.clip((i - 1) * SWA_BLOCK, 0, seq - band)
        k0 = pl.multiple_of(TM + start, SWA_BLOCK)
        q0 = pl.multiple_of(TM + i * SWA_BLOCK, SWA_BLOCK)
        valid = jnp.abs(qi + (i * SWA_BLOCK - start) - ki) <= SWA_WINDOW
        for hh in range(2):
            kb = kd_ref[hh, pl.ds(k0, band), :]
            vb = vd_ref[hh, pl.ds(k0, band), :]
            kc = kd_ref[hh, 0:TM, :]
            vc = vd_ref[hh, 0:TM, :]
            parts = []
            for j in range(group // 2):
                pair = hh * (group // 2) + j
                qp = q_ref[pl.ds(q0, SWA_BLOCK), pair * LANES:(pair + 1) * LANES]
                parts += [jnp.where(m0, qp, 0.0), jnp.where(m0, 0.0, qp)]
            q = jnp.concatenate(parts, axis=0).astype(BF16)
            s = jnp.where(valid, _dot_nt(q, kb), NEG_INF)
            s_ctx = _dot_nt(q, kc)
            h0 = kp * heads_per_step + hh * group
            sink = jnp.full((rows, 1), sink_ref[h0], F32)
            for g in range(1, group):
                sink = jnp.where(head_of_row == g, sink_ref[h0 + g], sink)
            res = _softmax_pv([s, s_ctx], [vb, vc], extra=sink * LOG2E)
            for j in range(group // 2):
                pair = hh * (group // 2) + j
                r0 = 2 * j * SWA_BLOCK
                o_ref[pl.ds(q0, SWA_BLOCK), pair * LANES:(pair + 1) * LANES] = jnp.where(
                    m0, res[r0:r0 + SWA_BLOCK], res[r0 + SWA_BLOCK:r0 + 2 * SWA_BLOCK])
        return carry

    lax.fori_loop(0, nblk, block, 0, unroll=True)


def _swa(p, sink):
    b, s, _ = p.shape
    qw = SWA_HEADS * HEAD_DIM // 2
    kblk = SWA_HEADS * HEAD_DIM // LANES
    vblk = kblk + SWA_KV_HEADS * HEAD_DIM // LANES
    return pl.pallas_call(
        _swa_kernel,
        grid=(b, 2),
        in_specs=[
            pl.BlockSpec(memory_space=pltpu.SMEM),
            pl.BlockSpec((None, s, qw), lambda bi, kp: (bi, 0, kp)),
            pl.BlockSpec((None, s, LANES), lambda bi, kp: (bi, 0, kblk + kp)),
            pl.BlockSpec((None, s, LANES), lambda bi, kp: (bi, 0, vblk + kp)),
        ],
        out_specs=pl.BlockSpec((None, s, qw), lambda bi, kp: (bi, 0, kp)),
        out_shape=jax.ShapeDtypeStruct((b, s, SWA_HEADS * HEAD_DIM), F32),
        scratch_shapes=[pltpu.VMEM((2, s, LANES), BF16), pltpu.VMEM((2, s, 2 * LANES), BF16)],
        compiler_params=_cparams(("arbitrary", "arbitrary")),
        name="swa",
    )(sink, p, p, p)


def _outproj_kernel(*refs, n_mix, n_stream, t0):
    mix_refs = refs[:n_mix]
    x_refs = refs[n_mix + 1:n_mix + 1 + n_stream]
    w_ref = refs[n_mix]
    (g1_ref, gn_ref, sh_ref, sc_ref, wr_ref, br_ref,
     xo_ref, h_ref, ri_ref, rt_ref, cnt_ref, carry_ref) = refs[n_mix + 1 + n_stream:]
    first = (pl.program_id(0) == 0) & (pl.program_id(1) == 0)

    @pl.when(first)
    def _():
        carry_ref[...] = jnp.zeros_like(carry_ref)

    o = None
    off = 0
    for m_ref in mix_refs:
        kw = m_ref.shape[1]
        part = jnp.dot(m_ref[...].astype(BF16), w_ref[off:off + kw, :], preferred_element_type=F32)
        o = part if o is None else o + part
        off += kw
    xn = _stream_tile(x_refs, pl.program_id(1) + t0 > 0) + g1_ref[...] * o
    xo_ref[...] = xn
    h = _rms_mod(xn, gn_ref[...], sh_ref[...], sc_ref[...])
    h_ref[...] = h

    logits = _dot3_split(h, wr_ref[0], wr_ref[1]) + br_ref[...]
    tm = logits.shape[0]
    lt = logits.T[0:ROUTE_ROWS, :]
    row = lax.broadcasted_iota(jnp.int32, (ROUTE_ROWS, tm), 0).astype(F32)
    big = 1e9
    gmask = row < MOE_GROUPS
    mg = jnp.max(jnp.where(gmask, lt, -big), axis=0, keepdims=True)
    sg = jnp.sum(jnp.where(gmask, jnp.exp(jnp.minimum(lt - mg, 0.0)), 0.0), axis=0, keepdims=True)
    gw = 1.0 / sg
    gi = jnp.min(jnp.where(gmask & (lt == mg), row, big), axis=0, keepdims=True)
    lo = ROUTE_LANE0 + MOE_EPG * gi
    emask = (row >= lo) & (row < lo + MOE_EPG)
    l1 = jnp.max(jnp.where(emask, lt, -big), axis=0, keepdims=True)
    i1 = jnp.min(jnp.where(emask & (lt == l1), row, big), axis=0, keepdims=True)
    emask2 = emask & (row != i1)
    l2 = jnp.max(jnp.where(emask2, lt, -big), axis=0, keepdims=True)
    i2 = jnp.min(jnp.where(emask2 & (lt == l2), row, big), axis=0, keepdims=True)
    e21 = jnp.exp(l2 - l1)
    w1 = gw / (1.0 + e21)
    w2 = gw * e21 / (1.0 + e21)

    oh = jnp.where((row == i1) | (row == i2), 1.0, 0.0)
    before = (lax.broadcasted_iota(jnp.int32, (tm, tm), 0) < lax.broadcasted_iota(jnp.int32, (tm, tm), 1))
    cum = (jnp.dot(oh.astype(BF16), jnp.where(before, 1.0, 0.0).astype(BF16), preferred_element_type=F32)
           + carry_ref[:, 0:1])
    r1 = jnp.sum(jnp.where(row == i1, cum, 0.0), axis=0, keepdims=True)
    r2 = jnp.sum(jnp.where(row == i2, cum, 0.0), axis=0, keepdims=True)
    carry_ref[...] = carry_ref[...] + jnp.sum(oh, axis=1, keepdims=True)
    cnt_ref[...] = carry_ref[...]

    frow = lax.broadcasted_iota(jnp.int32, (LANES, tm), 0)
    rt = jnp.where(frow == 0, i1 - ROUTE_LANE0, 0.0)
    rt = jnp.where(frow == 1, i2 - ROUTE_LANE0, rt)
    rt = jnp.where(frow == 2, w1, rt)
    rt = jnp.where(frow == 3, w2, rt)
    rt = jnp.where(frow == 4, r1, rt)
    rt = jnp.where(frow == 5, r2, rt)
    rt_ref[...] = rt[0:SUBLANES, :]
    ri_ref[...] = rt.T


def _outproj(mixes, w_bf16, xa, g1, gain2, sh2, sc2, wr, br, t0):
    b, s, d = _stream_shape(xa)
    nt = s // TM - t0
    so = nt * TM
    mod_idx = lambda bi, t: (jnp.where(t + t0 == 0, b, bi), 0, 0)
    tile = lambda wdt: pl.BlockSpec((None, TM, wdt), lambda bi, t: (bi, t + t0, 0))
    otile = lambda wdt: pl.BlockSpec((None, TM, wdt), lambda bi, t: (bi, t, 0))
    const = lambda shape: pl.BlockSpec(shape, lambda bi, t: (0,) * len(shape))
    x_specs, x_args = _stream_specs(xa, t0)
    in_specs = [tile(m.shape[2]) for m in mixes] + [const(w_bf16.shape)] + x_specs + [
        pl.BlockSpec((None, 1, d), mod_idx), const((1, d)),
        pl.BlockSpec((None, 1, d), mod_idx), pl.BlockSpec((None, 1, d), mod_idx),
        const((2, d, LANES)), const((1, LANES)),
    ]
    return pl.pallas_call(
        functools.partial(_outproj_kernel, n_mix=len(mixes), n_stream=len(x_args), t0=t0),
        grid=(b, nt),
        in_specs=in_specs,
        out_specs=[otile(d), otile(d), otile(LANES),
                   pl.BlockSpec((None, None, SUBLANES, TM), lambda bi, t: (bi, t, 0, 0)),
                   const((ROUTE_ROWS, LANES))],
        out_shape=[jax.ShapeDtypeStruct((b, so, d), F32), jax.ShapeDtypeStruct((b, so, d), F32),
                   jax.ShapeDtypeStruct((b, so, LANES), F32), jax.ShapeDtypeStruct((b, nt, SUBLANES, TM), F32),
                   jax.ShapeDtypeStruct((ROUTE_ROWS, LANES), F32)],
        scratch_shapes=[pltpu.VMEM((ROUTE_ROWS, LANES), F32)],
        compiler_params=_cparams(("arbitrary", "arbitrary")),
        name="outproj",
    )(*mixes, w_bf16, *x_args, g1, gain2, sh2, sc2, wr, br)


SUBLANES = 8


def _to_token_tiles(x):
    return x.reshape(x.shape[0], SUBLANES, x.shape[1] // SUBLANES)


def _from_token_tiles(x3):
    return x3.reshape(x3.shape[0], x3.shape[1] * x3.shape[2])


PAD_BITS = tuple(1 << k for k in reversed(range(MOE_ROWS.bit_length() - 1)))


def _dispatch_kernel(dest_ref, pads_ref, h_ref, xs_ref, hbuf, zbuf, sem, zsem):
    nt = pl.num_programs(1)
    step = pl.program_id(0) * nt + pl.program_id(1)
    last = pl.num_programs(0) * nt - 1
    slot = step % 2

    def zero_pads(wait):
        def expert(e, c):
            pos = pads_ref[0, e]
            n = pads_ref[1, e]
            for bit in PAD_BITS:
                take = (n & bit) != 0
                cp = pltpu.make_async_copy(zbuf.at[pl.ds(0, bit)], xs_ref.at[pl.ds(pos, bit)], zsem)

                @pl.when(take)
                def _():
                    cp.wait() if wait else cp.start()

                pos = pos + jnp.where(take, bit, 0)
            return c
        lax.fori_loop(0, MOE_EXPERTS, expert, 0)

        def tail(j, c):
            cp = pltpu.make_async_copy(
                zbuf, xs_ref.at[pl.ds(pads_ref[0, MOE_EXPERTS] + j * PAD_BITS[0], PAD_BITS[0])], zsem)
            cp.wait() if wait else cp.start()
            return c
        lax.fori_loop(0, pads_ref[1, MOE_EXPERTS], tail, 0)

    @pl.when(step == 0)
    def _():
        zbuf[...] = jnp.zeros_like(zbuf)
        zero_pads(wait=False)
    hbuf[slot] = _to_token_tiles(h_ref[...])

    def copy(sl, i, dst_row):
        return pltpu.make_async_copy(hbuf.at[sl, i], xs_ref.at[dst_row], sem.at[sl])

    for i in range(TM):
        copy(slot, i, dest_ref[step, i]).start()
        copy(slot, i, dest_ref[step, TM + i]).start()

    def drain(sl):
        for _ in range(2):
            pltpu.make_async_copy(hbuf.at[sl], xs_ref.at[pl.ds(0, TM)], sem.at[sl]).wait()

    @pl.when(step > 0)
    def _():
        drain(1 - slot)

    @pl.when(step == last)
    def _():
        drain(slot)
        zero_pads(wait=True)


def _dispatch(dest, pads, h2, n_pad):
    b, s, d = h2.shape
    nt = s // TM
    tile = (SUBLANES, d // SUBLANES)
    return pl.pallas_call(
        _dispatch_kernel,
        grid_spec=pltpu.PrefetchScalarGridSpec(
            num_scalar_prefetch=2,
            grid=(b, nt),
            in_specs=[pl.BlockSpec((None, TM, d), lambda bi, t, dr, pd: (bi, t, 0))],
            out_specs=pl.BlockSpec(memory_space=pl.ANY),
            scratch_shapes=[pltpu.VMEM((2, TM) + tile, F32), pltpu.VMEM((PAD_BITS[0],) + tile, F32),
                            pltpu.SemaphoreType.DMA((2,)), pltpu.SemaphoreType.DMA],
        ),
        out_shape=jax.ShapeDtypeStruct((n_pad,) + tile, F32),
        compiler_params=_cparams(("arbitrary", "arbitrary")),
        name="dispatch",
    )(dest, pads, h2)


def _mlp_kernel(be_ref, nu_ref, x_ref, wgu_ref, wd_ref, y_ref, wgu_b, wd_b):
    i = pl.program_id(0)
    prev = be_ref[jnp.maximum(i - 1, 0)]
    used = i < nu_ref[0]

    @pl.when(used & ((i == 0) | (be_ref[i] != prev)))
    def _():
        wgu_b[...] = wgu_ref[...].astype(BF16)
        wd_b[...] = wd_ref[...].astype(BF16)

    @pl.when(used)
    def _():
        x = _from_token_tiles(x_ref[...])
        gu = jnp.dot(x.astype(BF16), wgu_b[...], preferred_element_type=F32)
        act = _silu(gu[:, :MOE_FF]) * gu[:, MOE_FF:]
        y_ref[...] = _to_token_tiles(jnp.dot(act.astype(BF16), wd_b[...], preferred_element_type=F32))

    @pl.when(jnp.logical_not(used))
    def _():
        y_ref[...] = jnp.zeros_like(y_ref)


def _expert_mlp(block_e, n_used, xs, w_gu, w_down, layer):
    n_pad, sub, dl = xs.shape
    d = sub * dl
    ff2 = w_gu.shape[-1]
    slots = pl.BlockSpec((MOE_ROWS, sub, dl), lambda i, be, nu: (i, 0, 0))
    used_slots = pl.BlockSpec((MOE_ROWS, sub, dl), lambda i, be, nu: (jnp.minimum(i, nu[0] - 1), 0, 0))
    return pl.pallas_call(
        _mlp_kernel,
        grid_spec=pltpu.PrefetchScalarGridSpec(
            num_scalar_prefetch=2,
            grid=(n_pad // MOE_ROWS,),
            in_specs=[
                used_slots,
                pl.BlockSpec((None, None, d, ff2), lambda i, be, nu: (layer, be[i], 0, 0)),
                pl.BlockSpec((None, None, ff2 // 2, d), lambda i, be, nu: (layer, be[i], 0, 0)),
            ],
            out_specs=slots,
            scratch_shapes=[pltpu.VMEM((d, ff2), BF16), pltpu.VMEM((ff2 // 2, d), BF16)],
        ),
        out_shape=jax.ShapeDtypeStruct(xs.shape, F32),
        compiler_params=_cparams(("arbitrary",)),
        name="expert_mlp",
    )(block_e, n_used, xs, w_gu, w_down)


def _combine_kernel(dest_ref, ys_ref, x_ref, ri_ref, g2_ref, fg_ref, o_ref, buf, sem, *, final):
    bi = pl.program_id(0)
    t = pl.program_id(1)
    nt = pl.num_programs(1)
    step = bi * nt + t
    total = pl.num_programs(0) * nt

    def copy(src_row, slot, k, i):
        return pltpu.make_async_copy(ys_ref.at[src_row], buf.at[slot, k * TM + i], sem.at[slot])

    def issue(st, slot):
        for i in range(TM):
            copy(dest_ref[st, i], slot, 0, i).start()
            copy(dest_ref[st, TM + i], slot, 1, i).start()

    slot = step % 2

    @pl.when(step == 0)
    def _():
        issue(0, 0)

    @pl.when(step + 1 < total)
    def _():
        issue(step + 1, 1 - slot)

    pltpu.make_async_copy(ys_ref.at[pl.ds(0, 2 * TM)], buf.at[slot], sem.at[slot]).wait()

    lane = lax.broadcasted_iota(jnp.int32, (TM, LANES), 1)
    ri = ri_ref[...]
    w1 = jnp.sum(jnp.where(lane == 2, ri, 0.0), axis=-1, keepdims=True)
    w2 = jnp.sum(jnp.where(lane == 3, ri, 0.0), axis=-1, keepdims=True)
    y = (_from_token_tiles(buf[slot, 0:TM]) * w1 + _from_token_tiles(buf[slot, TM:2 * TM]) * w2)
    xn = x_ref[...] + g2_ref[...] * y
    if final:
        ms = jnp.mean(xn * xn, axis=-1, keepdims=True)
        xn = xn * lax.rsqrt(ms + NORM_EPS) * fg_ref[...]
    o_ref[...] = xn


def _combine(dest, ys, xa, rinfo, g2, final_g, has_ctx, final):
    b, s, d = xa.shape
    nt = s // TM
    mod_idx = lambda bi, t, dr: (jnp.where(t == 0, b, bi) if has_ctx else bi, 0, 0)
    tile = lambda wdt: pl.BlockSpec((None, TM, wdt), lambda bi, t, dr: (bi, t, 0))
    out_spec = tile(d)
    out_shape = jax.ShapeDtypeStruct((b, s, d), F32)
    return pl.pallas_call(
        functools.partial(_combine_kernel, final=final),
        grid_spec=pltpu.PrefetchScalarGridSpec(
            num_scalar_prefetch=1,
            grid=(b, nt),
            in_specs=[
                pl.BlockSpec(memory_space=pl.ANY),
                tile(d), tile(LANES),
                pl.BlockSpec((None, 1, d), mod_idx),
                pl.BlockSpec((1, d), lambda bi, t, dr: (0, 0)),
            ],
            out_specs=out_spec,
            scratch_shapes=[pltpu.VMEM((2, 2 * TM, SUBLANES, d // SUBLANES), F32),
                            pltpu.SemaphoreType.DMA((2,))],
        ),
        out_shape=out_shape,
        compiler_params=_cparams(("arbitrary", "arbitrary")),
        name="combine",
    )(dest, ys, xa, rinfo, g2, final_g)


def _moe(h2, rt, counts, w_gu, w_down, layer):
    b, s, d = h2.shape
    nt = s // TM
    r = rt.reshape(b * nt, SUBLANES, TM)
    fields = lambda i: jnp.concatenate([r[:, i], r[:, i + 1]], axis=1).astype(jnp.int32)
    e = fields(0)
    rank = fields(4)
    cnt = counts[ROUTE_LANE0:ROUTE_LANE0 + MOE_EXPERTS, 0].astype(jnp.int32)
    padded = (cnt + MOE_ROWS - 1) // MOE_ROWS * MOE_ROWS
    ends = jnp.cumsum(padded)
    starts = ends - padded
    eids = jnp.arange(MOE_EXPERTS, dtype=jnp.int32)
    dest = jnp.sum(jnp.where(e[..., None] == eids, starts, 0), axis=-1) + rank
    n_assign = b * nt * TM * 2
    n_blocks = (n_assign + MOE_EXPERTS * (MOE_ROWS - 1)) // MOE_ROWS + 1
    n_pad = n_blocks * MOE_ROWS
    blk_row = jnp.arange(n_blocks, dtype=jnp.int32) * MOE_ROWS
    block_e = jnp.minimum(jnp.sum((blk_row[:, None] >= ends[None, :]).astype(jnp.int32), axis=1),
                          MOE_EXPERTS - 1)
    n_used = (ends[-1] // MOE_ROWS).astype(jnp.int32).reshape(1)
    pads = jnp.stack([jnp.append(starts + cnt, ends[-1]),
                      jnp.append(padded - cnt, (n_pad - ends[-1]) // PAD_BITS[0])]).astype(jnp.int32)
    xs = _dispatch(dest, pads, h2, n_pad)
    ys = _expert_mlp(block_e, n_used, xs, w_gu, w_down, layer)
    return dest, ys


def _router_weights(wg, bg, we, be):
    d = wg.shape[0]
    pad = LANES - MOE_GROUPS - MOE_EXPERTS
    assert ROUTE_LANE0 == MOE_GROUPS
    wr = jnp.concatenate([wg.astype(F32), we.astype(F32), jnp.zeros((d, pad), F32)], axis=1)
    br = jnp.concatenate([bg.astype(F32), be.astype(F32), jnp.zeros((pad,), F32)]).reshape(1, LANES)
    return jnp.stack(_split_bf16(wr)), br


def kernel(x, c, ctx, c_ctx, ada_w, ada_b, norm_g, final_g, ab_w_in, ab_w_out, ret_decay, ret_gn, na_rpb,
           swa_w_in, swa_w_out, swa_sink, router_g_w, router_g_b, router_e_w, router_e_b,
           expert_w_gu, expert_w_down):
    b, seq, d = x.shape
    assert ctx.shape[1] == TM and seq % TM == 0 and d == D_MODEL
    xa = (ctx, x)
    rope = _rope_tables(seq)

    cvec = jnp.concatenate([c, c_ctx[None, :], jnp.zeros((7, d), F32)], axis=0)
    mod = _adaln(cvec, ada_w, ada_b)
    mod = mod.reshape(DEPTH, b + 8, 6, 1, d)[:, :b + 1].transpose(0, 2, 1, 3, 4)

    for layer in range(DEPTH):
        last = layer == DEPTH - 1
        sh1, sc1, g1, sh2, sc2, g2 = (mod[layer, i] for i in range(6))
        gain1 = norm_g[layer, 0].reshape(1, d)
        gain2 = norm_g[layer, 1].reshape(1, d)
        j = layer // 2
        if layer % 2 == 0:
            nb = RET_W // LANES
            qk_scale = HEAD_DIM ** -0.5
            scales = {blk: qk_scale for blk in range(nb, 2 * nb)}
            scales.update({blk: qk_scale * LOG2E for blk in range(4 * nb, 5 * nb)})
            p = _proj(xa, gain1, sh1, sc1, ab_w_in[j].astype(BF16), rope,
                      rope_blocks=range(0, 2 * nb), scales=scales)
            log_gamma = jnp.log1p(-jnp.exp2(-ret_decay[j].astype(F32)))
            mixes = [_retention(p, log_gamma, ret_gn[j]), _neighbourhood(p, _na_bias(na_rpb[j]))]
            w_out = ab_w_out[j]
        else:
            qb = SWA_HEADS * HEAD_DIM // LANES
            kb = SWA_KV_HEADS * HEAD_DIM // LANES
            p = _proj(xa, gain1, sh1, sc1, swa_w_in[j].astype(BF16), rope,
                      rope_blocks=range(0, qb + kb),
                      scales={blk: HEAD_DIM ** -0.5 * LOG2E for blk in range(qb)})
            mixes = [_swa(p, swa_sink[j].astype(F32))]
            w_out = swa_w_out[j]
        t0 = 1 if last else 0
        wr, br = _router_weights(router_g_w[layer], router_g_b[layer], router_e_w[layer], router_e_b[layer])
        xa, h2, rinfo, rt, counts = _outproj(mixes, w_out.astype(BF16), xa, g1, gain2, sh2, sc2, wr, br, t0)
        dest, ys = _moe(h2, rt, counts, expert_w_gu, expert_w_down, layer)
        xa = _combine(dest, ys, xa, rinfo, g2, final_g.reshape(1, d), not last, last)
    return xa
```

```python
import functools

import numpy as np
import jax
import jax.numpy as jnp
from jax import lax
from jax.experimental import pallas as pl
from jax.experimental.pallas import tpu as pltpu

F32 = jnp.float32
BF16 = jnp.bfloat16

D_MODEL = 1024
DEPTH = 2
GRID_W = 64
HEAD_DIM = 64
RET_HEADS = 8
NA_HEADS = 8
RET_W = 512
NA_W = 512
AB_IN = 4 * RET_W + 3 * NA_W
RET_CHUNK = 128
GN_EPS = 1e-5
NA_KH = 8
NA_KW = 16
SWA_HEADS = 16
SWA_KV_HEADS = 4
SWA_WINDOW = 128
SWA_BLOCK = 128
SWA_IN = (SWA_HEADS + 2 * SWA_KV_HEADS) * HEAD_DIM
ROPE_BASE = 10000.0
MOE_GROUPS = 4
MOE_EPG = 8
MOE_EXPERTS = 32
MOE_FF = 512
NORM_EPS = 1e-6
NEG_INF = -1e30

LANES = 128
TM = 256
RET_BLOCK = 256
MOE_ROWS = 512
ROUTE_LANE0 = 4
ROUTE_ROWS = 48
VMEM_LIMIT = 56 * 1024 * 1024


def _cparams(sem, vmem=VMEM_LIMIT, flags=None):
    return pltpu.CompilerParams(dimension_semantics=sem, vmem_limit_bytes=vmem, flags=flags)


def _split_bf16(a):
    hi = a.astype(BF16)
    lo = (a - hi.astype(F32)).astype(BF16)
    return hi, lo


def _dot3_split(a, bh, bl):
    ah, al = _split_bf16(a)
    d = lambda x, y: jnp.dot(x, y, preferred_element_type=F32)
    return d(ah, bh) + (d(ah, bl) + d(al, bh))


def _dot3(a, b):
    return _dot3_split(a, *_split_bf16(b))


def _dot_nt(a, b):
    return lax.dot_general(a, b, (((1,), (1,)), ((), ())), preferred_element_type=F32)


def _dot_tn(a, b):
    return lax.dot_general(a, b, (((0,), (0,)), ((), ())), preferred_element_type=F32)


def _silu(x):
    return x / (1.0 + jnp.exp(-x))


def _adaln_kernel(c_ref, w_ref, b_ref, o_ref):
    o_ref[...] = _dot3(_silu(c_ref[...]), w_ref[...]) + b_ref[...]


def _adaln(cvec, ada_w, ada_b):
    depth, d, n6 = ada_w.shape
    rows = cvec.shape[0]
    tn = 1024
    return pl.pallas_call(
        _adaln_kernel,
        grid=(depth, n6 // tn),
        in_specs=[
            pl.BlockSpec((rows, d), lambda l, j: (0, 0)),
            pl.BlockSpec((None, d, tn), lambda l, j: (l, 0, j)),
            pl.BlockSpec((None, 1, tn), lambda l, j: (l, 0, j)),
        ],
        out_specs=pl.BlockSpec((None, rows, tn), lambda l, j: (l, 0, j)),
        out_shape=jax.ShapeDtypeStruct((depth, rows, n6), F32),
        compiler_params=_cparams(("arbitrary", "arbitrary")),
        name="adaln",
    )(cvec, ada_w, ada_b.reshape(depth, 1, n6))


def _rms_mod(x, g, sh, sc):
    ms = jnp.mean(x * x, axis=-1, keepdims=True)
    return (x * lax.rsqrt(ms + NORM_EPS) * g) * (1.0 + sc) + sh


def _stream_tile(refs, is_lat):
    if len(refs) == 1:
        return refs[0][...]
    return jnp.where(is_lat, refs[1][...], refs[0][...])


def _stream_specs(stream, t0, extra_args=0):
    def im(f):
        return (lambda bi, t, *_: f(bi, t + t0))
    if not isinstance(stream, tuple):
        return [pl.BlockSpec((None, TM, stream.shape[2]), im(lambda bi, t: (bi, t, 0)))], [stream]
    ctx, x = stream
    d = x.shape[2]
    return ([pl.BlockSpec((None, TM, d), im(lambda bi, t: (bi, 0, 0))),
             pl.BlockSpec((None, TM, d), im(lambda bi, t: (bi, jnp.maximum(t - 1, 0), 0)))], [ctx, x])


def _stream_shape(stream):
    if not isinstance(stream, tuple):
        return stream.shape
    ctx, x = stream
    return (x.shape[0], ctx.shape[1] + x.shape[1], x.shape[2])


def _proj_kernel(*refs, rope_blocks, scales, cn, n_stream):
    x_refs = refs[:n_stream]
    g_ref, sh_ref, sc_ref, w_ref, rope_ref, o_ref = refs[n_stream:]
    is_lat = pl.program_id(1) > 0
    hb = _rms_mod(_stream_tile(x_refs, is_lat), g_ref[...], sh_ref[...], sc_ref[...]).astype(BF16)
    nout = w_ref.shape[1]
    for c in range(nout // cn):
        o = jnp.dot(hb, w_ref[:, c * cn:(c + 1) * cn], preferred_element_type=F32)
        for s in range(cn // LANES):
            blk = c * (cn // LANES) + s
            ob = o[:, s * LANES:(s + 1) * LANES]
            if blk in rope_blocks:
                r = (ob * rope_ref[0] + pltpu.roll(ob, 16, 1) * rope_ref[1]
                     + pltpu.roll(ob, LANES - 16, 1) * rope_ref[2])
                ob = jnp.where(is_lat, r, ob)
            if blk in scales:
                ob = ob * scales[blk]
            o_ref[:, blk * LANES:(blk + 1) * LANES] = ob.astype(o_ref.dtype)


def _proj(xa, gain, sh, sc, w_bf16, rope, rope_blocks, scales):
    b, s, d = _stream_shape(xa)
    nout = w_bf16.shape[1]
    nt = s // TM
    mod_idx = lambda bi, t: (jnp.where(t == 0, b, bi), 0, 0)
    x_specs, x_args = _stream_specs(xa, 0)
    kern = functools.partial(_proj_kernel, rope_blocks=frozenset(rope_blocks), scales=dict(scales), cn=512,
                             n_stream=len(x_args))
    return pl.pallas_call(
        kern,
        grid=(b, nt),
        in_specs=x_specs + [
            pl.BlockSpec((1, d), lambda bi, t: (0, 0)),
            pl.BlockSpec((None, 1, d), mod_idx),
            pl.BlockSpec((None, 1, d), mod_idx),
            pl.BlockSpec((d, nout), lambda bi, t: (0, 0)),
            pl.BlockSpec((3, TM, LANES), lambda bi, t: (0, jnp.maximum(t - 1, 0), 0)),
        ],
        out_specs=pl.BlockSpec((None, TM, nout), lambda bi, t: (bi, t, 0)),
        out_shape=jax.ShapeDtypeStruct((b, s, nout), BF16),
        compiler_params=_cparams(("arbitrary", "arbitrary")),
        name="proj",
    )(*x_args, gain, sh, sc, w_bf16, rope)


def _rope_tables(seq):
    nf = HEAD_DIM // 4
    inv = ROPE_BASE ** (-jnp.arange(nf, dtype=F32) / nf)
    t = jnp.arange(seq)
    row = (t // GRID_W).astype(F32)
    col = (t % GRID_W).astype(F32)
    lane = np.arange(LANES)
    jj = lane % HEAD_DIM
    axis_is_col = (jj // 32) == 1
    second_half = (jj % 32) >= 16
    f = jj % 16
    pos = jnp.where(axis_is_col[None, :], col[:, None], row[:, None])
    ang = pos * inv[f][None, :]
    c, s = jnp.cos(ang), jnp.sin(ang)
    sa = jnp.where(second_half[None, :], s, 0.0)
    sb = jnp.where(second_half[None, :], 0.0, -s)
    return jnp.stack([c, sa, sb], axis=0)


def _ret_kernel(lg_ref, q_ref, k_ref, v_ref, g_ref, gn_ref, o_ref, accf_ref, accb_ref,
                intra_ref, qd_ref, kd_ref):
    hp = pl.program_id(0)
    c = RET_BLOCK
    s_len = q_ref.shape[0]
    n_chunks = s_len // c
    ctx_chunks = TM // c
    pos = lax.broadcasted_iota(jnp.int32, (c, LANES), 0).astype(F32)
    m0 = lax.broadcasted_iota(jnp.int32, (c, LANES), 1) < HEAD_DIM
    same_head = ((lax.broadcasted_iota(jnp.int32, (LANES, LANES), 0) < HEAD_DIM)
                 == (lax.broadcasted_iota(jnp.int32, (LANES, LANES), 1) < HEAD_DIM))
    rel = (lax.broadcasted_iota(jnp.int32, (c, c), 0) - lax.broadcasted_iota(jnp.int32, (c, c), 1)).astype(F32)
    lgf = [lg_ref[0, hp * 2 + hh] for hh in range(2)]
    lgb = [lg_ref[1, hp * 2 + hh] for hh in range(2)]
    lgf_l = jnp.where(m0, lgf[0], lgf[1])
    lgb_l = jnp.where(m0, lgb[0], lgb[1])
    @pl.when(pl.program_id(1) == 0)
    def _():
        for hh in range(2):
            intra_ref[0, :, hh * c:(hh + 1) * c] = jnp.where(
                rel >= 0, jnp.exp(lgf[hh] * jnp.maximum(rel, 0.0)), 0.0)
            intra_ref[1, :, hh * c:(hh + 1) * c] = jnp.where(
                rel <= 0, jnp.exp(lgb[hh] * jnp.maximum(-rel, 0.0)), 0.0)
        qd_ref[0] = jnp.exp(lgf_l * (pos + 1.0))
        qd_ref[1] = jnp.exp(lgb_l * (c - pos))
        kd_ref[0] = jnp.exp(lgf_l * (c - 1.0 - pos))
        kd_ref[1] = jnp.exp(lgb_l * pos)
    cd = [jnp.exp(lgf_l[0:1] * float(c)), jnp.exp(lgb_l[0:1] * float(c))]

    def chunk(r0, state, d):
        q = q_ref[pl.ds(r0, c), :]
        k = k_ref[pl.ds(r0, c), :]
        v = v_ref[pl.ds(r0, c), :]
        qb = q.astype(BF16)
        kcat = jnp.concatenate([jnp.where(m0, k, 0.0), jnp.where(m0, 0.0, k)], axis=0).astype(BF16)
        vcat = jnp.concatenate([jnp.where(m0, v, 0.0), jnp.where(m0, 0.0, v)], axis=0).astype(BF16)
        s = _dot_nt(qb, kcat) * intra_ref[d]
        out = jnp.dot(s.astype(BF16), vcat, preferred_element_type=F32)
        out = out + jnp.dot(qb, state.astype(BF16), preferred_element_type=F32) * qd_ref[d]
        kv = _dot_tn((k * kd_ref[d]).astype(BF16), v.astype(BF16))
        return out, state * cd[d] + jnp.where(same_head, kv, 0.0)

    def body(i, states):
        sf, sb = states
        rf = pl.multiple_of(i * c, c)
        ib = jnp.where(i < ctx_chunks, ctx_chunks - 1 - i, n_chunks + ctx_chunks - 1 - i)
        rb = pl.multiple_of(ib * c, c)
        of, sf = chunk(rf, sf, 0)
        ob, sb = chunk(rb, sb, 1)
        accf_ref[pl.ds(rf, c), :] = of
        accb_ref[pl.ds(rb, c), :] = ob
        return sf, sb

    z = jnp.zeros((LANES, LANES), F32)
    lax.fori_loop(0, n_chunks, body, (z, z), unroll=True)

    avg = jnp.where(same_head, 1.0 / HEAD_DIM, 0.0).astype(BF16)

    def head_mean(x):
        hi, lo = _split_bf16(x)
        return (jnp.dot(hi, avg, preferred_element_type=F32) + jnp.dot(lo, avg, preferred_element_type=F32))

    def readout(i, carry):
        r0 = pl.multiple_of(i * c, c)
        o = accf_ref[pl.ds(r0, c), :] + accb_ref[pl.ds(r0, c), :]
        dlt = o - head_mean(o)
        var = head_mean(dlt * dlt)
        y = dlt * lax.rsqrt(var + GN_EPS) * gn_ref[...]
        o_ref[pl.ds(r0, c), :] = _silu(g_ref[pl.ds(r0, c), :].astype(F32)) * y
        return carry

    lax.fori_loop(0, n_chunks, readout, 0, unroll=True)


def _retention(p, log_gamma, ret_gn):
    b, s, _ = p.shape
    nb = RET_W // LANES
    blk = lambda off: pl.BlockSpec((None, s, LANES), lambda hp, bi: (bi, 0, off + hp))
    return pl.pallas_call(
        _ret_kernel,
        grid=(nb, b),
        in_specs=[
            pl.BlockSpec(memory_space=pltpu.SMEM),
            blk(0), blk(nb), blk(2 * nb), blk(3 * nb),
            pl.BlockSpec((1, LANES), lambda hp, bi: (0, hp)),
        ],
        out_specs=pl.BlockSpec((None, s, LANES), lambda hp, bi: (bi, 0, hp)),
        out_shape=jax.ShapeDtypeStruct((b, s, RET_W), F32),
        scratch_shapes=[pltpu.VMEM((s, LANES), F32), pltpu.VMEM((s, LANES), F32),
                        pltpu.VMEM((2, RET_BLOCK, 2 * RET_BLOCK), F32),
                        pltpu.VMEM((2, RET_BLOCK, LANES), F32), pltpu.VMEM((2, RET_BLOCK, LANES), F32)],
        compiler_params=_cparams(("arbitrary", "arbitrary")),
        name="retention",
    )(log_gamma, p, p, p, p, ret_gn.reshape(1, RET_W))


LOG2E = 1.4426950408889634


def _softmax_pv(s_list, v_list, extra=None):
    m = None
    for s in s_list:
        for j in range(s.shape[1] // LANES):
            blk = s[:, j * LANES:(j + 1) * LANES]
            m = blk if m is None else jnp.maximum(m, blk)
    m = m.max(axis=-1, keepdims=True)
    if extra is not None:
        m = jnp.maximum(m, extra)
    acc = None
    for s, v in zip(s_list, v_list):
        pv = jnp.dot(jnp.exp2(s - m).astype(BF16), v, preferred_element_type=F32)
        acc = pv if acc is None else acc + pv
    o, den = acc[:, :LANES], acc[:, LANES:]
    if extra is not None:
        den = den + jnp.exp2(extra - m)
    return o / den


def _stage_heads(q_ref, k_ref, v_ref, qm_ref, kb_ref, vb_ref):
    m0 = lax.broadcasted_iota(jnp.int32, (TM, LANES), 1) < HEAD_DIM

    def stage(i, carry):
        r0 = pl.multiple_of(i * TM, TM)
        q = q_ref[pl.ds(r0, TM), :]
        qm_ref[0, pl.ds(r0, TM), :] = jnp.where(m0, q, 0.0).astype(BF16)
        qm_ref[1, pl.ds(r0, TM), :] = jnp.where(m0, 0.0, q).astype(BF16)
        kb_ref[pl.ds(r0, TM), :] = k_ref[pl.ds(r0, TM), :].astype(BF16)
        vb_ref[pl.ds(r0, TM), 0:LANES] = v_ref[pl.ds(r0, TM), :].astype(BF16)
        vb_ref[pl.ds(r0, TM), LANES:2 * LANES] = jnp.ones((TM, LANES), BF16)
        return carry

    lax.fori_loop(0, q_ref.shape[0] // TM, stage, 0)


def _na_kernel(q_ref, k_ref, v_ref, bias_ref, o_ref, qm_ref, kb_ref, vb_ref):
    s_len = q_ref.shape[0]
    rows = (s_len - TM) // GRID_W
    nloc = NA_KH * GRID_W
    _stage_heads(q_ref, k_ref, v_ref, qm_ref, kb_ref, vb_ref)

    kc = kb_ref[0:TM, :]
    vc = vb_ref[0:TM, :]
    outs = [_softmax_pv([_dot_nt(qm_ref[hh, 0:TM, :], kc)], [vc]) for hh in range(2)]
    m0c = lax.broadcasted_iota(jnp.int32, (TM, LANES), 1) < HEAD_DIM
    o_ref[0:TM, :] = jnp.where(m0c, outs[0], outs[1])

    m0 = lax.broadcasted_iota(jnp.int32, (GRID_W, LANES), 1) < HEAD_DIM

    def row_block(r, carry):
        rs = jnp.clip(r - NA_KH // 2, 0, rows - NA_KH)
        pat = r - rs
        q0 = pl.multiple_of(TM + r * GRID_W, GRID_W)
        k0 = pl.multiple_of(TM + rs * GRID_W, GRID_W)
        kl = kb_ref[pl.ds(k0, nloc), :]
        vl = vb_ref[pl.ds(k0, nloc), :]
        kc = kb_ref[0:TM, :]
        vc = vb_ref[0:TM, :]
        q = jnp.concatenate([qm_ref[0, pl.ds(q0, GRID_W), :], qm_ref[1, pl.ds(q0, GRID_W), :]], axis=0)
        s_loc = _dot_nt(q, kl) + bias_ref[pat].reshape(2 * GRID_W, nloc)
        s_ctx = _dot_nt(q, kc)
        res = _softmax_pv([s_loc, s_ctx], [vl, vc])
        o_ref[pl.ds(q0, GRID_W), :] = jnp.where(m0, res[:GRID_W], res[GRID_W:])
        return carry

    lax.fori_loop(0, rows, row_block, 0, unroll=True)


def _na_bias(rpb):
    h = rpb.shape[0]
    qc = np.arange(GRID_W)[:, None]
    kc = np.arange(GRID_W)[None, :]
    win = np.clip(qc - NA_KW // 2, 0, GRID_W - NA_KW)
    valid = (kc >= win) & (kc < win + NA_KW)
    col_off = np.clip(kc - qc + NA_KW - 1, 0, 2 * NA_KW - 2)
    onehot = (col_off[None] == np.arange(2 * NA_KW - 1)[:, None, None]).astype(np.float32)
    cols = jnp.einsum("hrc,cqk->hrqk", rpb.astype(F32), onehot, precision=lax.Precision.HIGHEST)
    cols = jnp.where(valid[None, None], cols * LOG2E, NEG_INF)
    bias = jnp.stack([cols[:, NA_KH - 1 - p:2 * NA_KH - 1 - p] for p in range(NA_KH)])
    bias = bias.transpose(0, 1, 3, 2, 4)
    return bias.reshape(NA_KH, h, GRID_W, NA_KH * GRID_W)


def _neighbourhood(p, bias):
    b, s, _ = p.shape
    nb = NA_W // LANES
    c0 = 4 * RET_W // LANES
    blk = lambda off: pl.BlockSpec((None, s, LANES), lambda hp, bi: (bi, 0, c0 + off + hp))
    return pl.pallas_call(
        _na_kernel,
        grid=(nb, b),
        in_specs=[
            blk(0), blk(nb), blk(2 * nb),
            pl.BlockSpec((NA_KH, 2, GRID_W, NA_KH * GRID_W), lambda hp, bi: (0, hp, 0, 0)),
        ],
        out_specs=pl.BlockSpec((None, s, LANES), lambda hp, bi: (bi, 0, hp)),
        out_shape=jax.ShapeDtypeStruct((b, s, NA_W), F32),
        scratch_shapes=[pltpu.VMEM((2, s, LANES), BF16), pltpu.VMEM((s, LANES), BF16),
                        pltpu.VMEM((s, 2 * LANES), BF16)],
        compiler_params=_cparams(("arbitrary", "arbitrary")),
        name="neighbourhood",
    )(p, p, p, bias)


def _swa_kernel(sink_ref, q_ref, k_ref, v_ref, o_ref, kd_ref, vd_ref):
    kp = pl.program_id(1)
    s_len = q_ref.shape[0]
    seq = s_len - TM
    nblk = seq // SWA_BLOCK
    band = SWA_BLOCK + 2 * SWA_WINDOW
    group = SWA_HEADS // SWA_KV_HEADS
    heads_per_step = 2 * group
    m0t = lax.broadcasted_iota(jnp.int32, (TM, LANES), 1) < HEAD_DIM

    def stage(i, carry):
        r0 = pl.multiple_of(i * TM, TM)
        for src, dst in ((k_ref, kd_ref), (v_ref, vd_ref)):
            x = src[pl.ds(r0, TM), :].astype(F32)
            xr = pltpu.roll(x, HEAD_DIM, 1)
            dst[0, pl.ds(r0, TM), 0:LANES] = jnp.where(m0t, x, xr).astype(BF16)
            dst[1, pl.ds(r0, TM), 0:LANES] = jnp.where(m0t, xr, x).astype(BF16)
        for h in range(2):
            vd_ref[h, pl.ds(r0, TM), LANES:2 * LANES] = jnp.ones((TM, LANES), BF16)
        return carry

    lax.fori_loop(0, s_len // TM, stage, 0)
    o_ref[0:TM, :] = jnp.zeros((TM, o_ref.shape[1]), F32)

    rows = group * SWA_BLOCK
    qi = lax.broadcasted_iota(jnp.int32, (rows, band), 0) % SWA_BLOCK
    ki = lax.broadcasted_iota(jnp.int32, (rows, band), 1)
    m0 = lax.broadcasted_iota(jnp.int32, (SWA_BLOCK, LANES), 1) < HEAD_DIM
    head_of_row = lax.broadcasted_iota(jnp.int32, (rows, 1), 0) // SWA_BLOCK

    def block(i, carry):
        start = jnp.clip((i - 1) * SWA_BLOCK, 0, seq - band)
        k0 = pl.multiple_of(TM + start, SWA_BLOCK)
        q0 = pl.multiple_of(TM + i * SWA_BLOCK, SWA_BLOCK)
        valid = jnp.abs(qi + (i * SWA_BLOCK - start) - ki) <= SWA_WINDOW
        for hh in range(2):
            kb = kd_ref[hh, pl.ds(k0, band), :]
            vb = vd_ref[hh, pl.ds(k0, band), :]
            kc = kd_ref[hh, 0:TM, :]
            vc = vd_ref[hh, 0:TM, :]
            parts = []
            for j in range(group // 2):
                pair = hh * (group // 2) + j
                qp = q_ref[pl.ds(q0, SWA_BLOCK), pair * LANES:(pair + 1) * LANES]
                parts += [jnp.where(m0, qp, 0.0), jnp.where(m0, 0.0, qp)]
            q = jnp.concatenate(parts, axis=0).astype(BF16)
            s = jnp.where(valid, _dot_nt(q, kb), NEG_INF)
            s_ctx = _dot_nt(q, kc)
            h0 = kp * heads_per_step + hh * group
            sink = jnp.full((rows, 1), sink_ref[h0], F32)
            for g in range(1, group):
                sink = jnp.where(head_of_row == g, sink_ref[h0 + g], sink)
            res = _softmax_pv([s, s_ctx], [vb, vc], extra=sink * LOG2E)
            for j in range(group // 2):
                pair = hh * (group // 2) + j
                r0 = 2 * j * SWA_BLOCK
                o_ref[pl.ds(q0, SWA_BLOCK), pair * LANES:(pair + 1) * LANES] = jnp.where(
                    m0, res[r0:r0 + SWA_BLOCK], res[r0 + SWA_BLOCK:r0 + 2 * SWA_BLOCK])
        return carry

    lax.fori_loop(0, nblk, block, 0, unroll=True)


def _swa(p, sink):
    b, s, _ = p.shape
    qw = SWA_HEADS * HEAD_DIM // 2
    kblk = SWA_HEADS * HEAD_DIM // LANES
    vblk = kblk + SWA_KV_HEADS * HEAD_DIM // LANES
    return pl.pallas_call(
        _swa_kernel,
        grid=(b, 2),
        in_specs=[
            pl.BlockSpec(memory_space=pltpu.SMEM),
            pl.BlockSpec((None, s, qw), lambda bi, kp: (bi, 0, kp)),
            pl.BlockSpec((None, s, LANES), lambda bi, kp: (bi, 0, kblk + kp)),
            pl.BlockSpec((None, s, LANES), lambda bi, kp: (bi, 0, vblk + kp)),
        ],
        out_specs=pl.BlockSpec((None, s, qw), lambda bi, kp: (bi, 0, kp)),
        out_shape=jax.ShapeDtypeStruct((b, s, SWA_HEADS * HEAD_DIM), F32),
        scratch_shapes=[pltpu.VMEM((2, s, LANES), BF16), pltpu.VMEM((2, s, 2 * LANES), BF16)],
        compiler_params=_cparams(("arbitrary", "arbitrary")),
        name="swa",
    )(sink, p, p, p)


def _outproj_kernel(*refs, n_mix, n_stream, t0):
    mix_refs = refs[:n_mix]
    x_refs = refs[n_mix + 1:n_mix + 1 + n_stream]
    w_ref = refs[n_mix]
    (g1_ref, gn_ref, sh_ref, sc_ref, wr_ref, br_ref,
     xo_ref, h_ref, ri_ref, rt_ref, cnt_ref, carry_ref) = refs[n_mix + 1 + n_stream:]
    first = (pl.program_id(0) == 0) & (pl.program_id(1) == 0)

    @pl.when(first)
    def _():
        carry_ref[...] = jnp.zeros_like(carry_ref)

    o = None
    off = 0
    for m_ref in mix_refs:
        kw = m_ref.shape[1]
        part = jnp.dot(m_ref[...].astype(BF16), w_ref[off:off + kw, :], preferred_element_type=F32)
        o = part if o is None else o + part
        off += kw
    xn = _stream_tile(x_refs, pl.program_id(1) + t0 > 0) + g1_ref[...] * o
    xo_ref[...] = xn
    h = _rms_mod(xn, gn_ref[...], sh_ref[...], sc_ref[...])
    h_ref[...] = h

    logits = _dot3_split(h, wr_ref[0], wr_ref[1]) + br_ref[...]
    tm = logits.shape[0]
    lt = logits.T[0:ROUTE_ROWS, :]
    row = lax.broadcasted_iota(jnp.int32, (ROUTE_ROWS, tm), 0).astype(F32)
    big = 1e9
    gmask = row < MOE_GROUPS
    mg = jnp.max(jnp.where(gmask, lt, -big), axis=0, keepdims=True)
    sg = jnp.sum(jnp.where(gmask, jnp.exp(jnp.minimum(lt - mg, 0.0)), 0.0), axis=0, keepdims=True)
    gw = 1.0 / sg
    gi = jnp.min(jnp.where(gmask & (lt == mg), row, big), axis=0, keepdims=True)
    lo = ROUTE_LANE0 + MOE_EPG * gi
    emask = (row >= lo) & (row < lo + MOE_EPG)
    l1 = jnp.max(jnp.where(emask, lt, -big), axis=0, keepdims=True)
    i1 = jnp.min(jnp.where(emask & (lt == l1), row, big), axis=0, keepdims=True)
    emask2 = emask & (row != i1)
    l2 = jnp.max(jnp.where(emask2, lt, -big), axis=0, keepdims=True)
    i2 = jnp.min(jnp.where(emask2 & (lt == l2), row, big), axis=0, keepdims=True)
    e21 = jnp.exp(l2 - l1)
    w1 = gw / (1.0 + e21)
    w2 = gw * e21 / (1.0 + e21)

    oh = jnp.where((row == i1) | (row == i2), 1.0, 0.0)
    before = (lax.broadcasted_iota(jnp.int32, (tm, tm), 0) < lax.broadcasted_iota(jnp.int32, (tm, tm), 1))
    cum = (jnp.dot(oh.astype(BF16), jnp.where(before, 1.0, 0.0).astype(BF16), preferred_element_type=F32)
           + carry_ref[:, 0:1])
    r1 = jnp.sum(jnp.where(row == i1, cum, 0.0), axis=0, keepdims=True)
    r2 = jnp.sum(jnp.where(row == i2, cum, 0.0), axis=0, keepdims=True)
    carry_ref[...] = carry_ref[...] + jnp.sum(oh, axis=1, keepdims=True)
    cnt_ref[...] = carry_ref[...]

    frow = lax.broadcasted_iota(jnp.int32, (LANES, tm), 0)
    rt = jnp.where(frow == 0, i1 - ROUTE_LANE0, 0.0)
    rt = jnp.where(frow == 1, i2 - ROUTE_LANE0, rt)
    rt = jnp.where(frow == 2, w1, rt)
    rt = jnp.where(frow == 3, w2, rt)
    rt = jnp.where(frow == 4, r1, rt)
    rt = jnp.where(frow == 5, r2, rt)
    rt_ref[...] = rt[0:SUBLANES, :]
    ri_ref[...] = rt.T


def _outproj(mixes, w_bf16, xa, g1, gain2, sh2, sc2, wr, br, t0):
    b, s, d = _stream_shape(xa)
    nt = s // TM - t0
    so = nt * TM
    mod_idx = lambda bi, t: (jnp.where(t + t0 == 0, b, bi), 0, 0)
    tile = lambda wdt: pl.BlockSpec((None, TM, wdt), lambda bi, t: (bi, t + t0, 0))
    otile = lambda wdt: pl.BlockSpec((None, TM, wdt), lambda bi, t: (bi, t, 0))
    const = lambda shape: pl.BlockSpec(shape, lambda bi, t: (0,) * len(shape))
    x_specs, x_args = _stream_specs(xa, t0)
    in_specs = [tile(m.shape[2]) for m in mixes] + [const(w_bf16.shape)] + x_specs + [
        pl.BlockSpec((None, 1, d), mod_idx), const((1, d)),
        pl.BlockSpec((None, 1, d), mod_idx), pl.BlockSpec((None, 1, d), mod_idx),
        const((2, d, LANES)), const((1, LANES)),
    ]
    return pl.pallas_call(
        functools.partial(_outproj_kernel, n_mix=len(mixes), n_stream=len(x_args), t0=t0),
        grid=(b, nt),
        in_specs=in_specs,
        out_specs=[otile(d), otile(d), otile(LANES),
                   pl.BlockSpec((None, None, SUBLANES, TM), lambda bi, t: (bi, t, 0, 0)),
                   const((ROUTE_ROWS, LANES))],
        out_shape=[jax.ShapeDtypeStruct((b, so, d), F32), jax.ShapeDtypeStruct((b, so, d), F32),
                   jax.ShapeDtypeStruct((b, so, LANES), F32), jax.ShapeDtypeStruct((b, nt, SUBLANES, TM), F32),
                   jax.ShapeDtypeStruct((ROUTE_ROWS, LANES), F32)],
        scratch_shapes=[pltpu.VMEM((ROUTE_ROWS, LANES), F32)],
        compiler_params=_cparams(("arbitrary", "arbitrary")),
        name="outproj",
    )(*mixes, w_bf16, *x_args, g1, gain2, sh2, sc2, wr, br)


SUBLANES = 8


def _to_token_tiles(x):
    return x.reshape(x.shape[0], SUBLANES, x.shape[1] // SUBLANES)


def _from_token_tiles(x3):
    return x3.reshape(x3.shape[0], x3.shape[1] * x3.shape[2])


PAD_BITS = tuple(1 << k for k in reversed(range(MOE_ROWS.bit_length() - 1)))


def _dispatch_kernel(dest_ref, pads_ref, h_ref, xs_ref, hbuf, zbuf, sem, zsem):
    nt = pl.num_programs(1)
    step = pl.program_id(0) * nt + pl.program_id(1)
    last = pl.num_programs(0) * nt - 1
    slot = step % 2

    def zero_pads(wait):
        def expert(e, c):
            pos = pads_ref[0, e]
            n = pads_ref[1, e]
            for bit in PAD_BITS:
                take = (n & bit) != 0
                cp = pltpu.make_async_copy(zbuf.at[pl.ds(0, bit)], xs_ref.at[pl.ds(pos, bit)], zsem)

                @pl.when(take)
                def _():
                    cp.wait() if wait else cp.start()

                pos = pos + jnp.where(take, bit, 0)
            return c
        lax.fori_loop(0, MOE_EXPERTS, expert, 0)

        def tail(j, c):
            cp = pltpu.make_async_copy(
                zbuf, xs_ref.at[pl.ds(pads_ref[0, MOE_EXPERTS] + j * PAD_BITS[0], PAD_BITS[0])], zsem)
            cp.wait() if wait else cp.start()
            return c
        lax.fori_loop(0, pads_ref[1, MOE_EXPERTS], tail, 0)

    @pl.when(step == 0)
    def _():
        zbuf[...] = jnp.zeros_like(zbuf)
        zero_pads(wait=False)
    hbuf[slot] = _to_token_tiles(h_ref[...])

    def copy(sl, i, dst_row):
        return pltpu.make_async_copy(hbuf.at[sl, i], xs_ref.at[dst_row], sem.at[sl])

    for i in range(TM):
        copy(slot, i, dest_ref[step, i]).start()
        copy(slot, i, dest_ref[step, TM + i]).start()

    def drain(sl):
        for _ in range(2):
            pltpu.make_async_copy(hbuf.at[sl], xs_ref.at[pl.ds(0, TM)], sem.at[sl]).wait()

    @pl.when(step > 0)
    def _():
        drain(1 - slot)

    @pl.when(step == last)
    def _():
        drain(slot)
        zero_pads(wait=True)


def _dispatch(dest, pads, h2, n_pad):
    b, s, d = h2.shape
    nt = s // TM
    tile = (SUBLANES, d // SUBLANES)
    return pl.pallas_call(
        _dispatch_kernel,
        grid_spec=pltpu.PrefetchScalarGridSpec(
            num_scalar_prefetch=2,
            grid=(b, nt),
            in_specs=[pl.BlockSpec((None, TM, d), lambda bi, t, dr, pd: (bi, t, 0))],
            out_specs=pl.BlockSpec(memory_space=pl.ANY),
            scratch_shapes=[pltpu.VMEM((2, TM) + tile, F32), pltpu.VMEM((PAD_BITS[0],) + tile, F32),
                            pltpu.SemaphoreType.DMA((2,)), pltpu.SemaphoreType.DMA],
        ),
        out_shape=jax.ShapeDtypeStruct((n_pad,) + tile, F32),
        compiler_params=_cparams(("arbitrary", "arbitrary")),
        name="dispatch",
    )(dest, pads, h2)


def _mlp_kernel(be_ref, nu_ref, x_ref, wgu_ref, wd_ref, y_ref, wgu_b, wd_b):
    i = pl.program_id(0)
    prev = be_ref[jnp.maximum(i - 1, 0)]
    used = i < nu_ref[0]

    @pl.when(used & ((i == 0) | (be_ref[i] != prev)))
    def _():
        wgu_b[...] = wgu_ref[...].astype(BF16)
        wd_b[...] = wd_ref[...].astype(BF16)

    @pl.when(used)
    def _():
        x = _from_token_tiles(x_ref[...])
        gu = jnp.dot(x.astype(BF16), wgu_b[...], preferred_element_type=F32)
        act = _silu(gu[:, :MOE_FF]) * gu[:, MOE_FF:]
        y_ref[...] = _to_token_tiles(jnp.dot(act.astype(BF16), wd_b[...], preferred_element_type=F32))

    @pl.when(jnp.logical_not(used))
    def _():
        y_ref[...] = jnp.zeros_like(y_ref)


def _expert_mlp(block_e, n_used, xs, w_gu, w_down, layer):
    n_pad, sub, dl = xs.shape
    d = sub * dl
    ff2 = w_gu.shape[-1]
    slots = pl.BlockSpec((MOE_ROWS, sub, dl), lambda i, be, nu: (i, 0, 0))
    used_slots = pl.BlockSpec((MOE_ROWS, sub, dl), lambda i, be, nu: (jnp.minimum(i, nu[0] - 1), 0, 0))
    return pl.pallas_call(
        _mlp_kernel,
        grid_spec=pltpu.PrefetchScalarGridSpec(
            num_scalar_prefetch=2,
            grid=(n_pad // MOE_ROWS,),
            in_specs=[
                used_slots,
                pl.BlockSpec((None, None, d, ff2), lambda i, be, nu: (layer, be[i], 0, 0)),
                pl.BlockSpec((None, None, ff2 // 2, d), lambda i, be, nu: (layer, be[i], 0, 0)),
            ],
            out_specs=slots,
            scratch_shapes=[pltpu.VMEM((d, ff2), BF16), pltpu.VMEM((ff2 // 2, d), BF16)],
        ),
        out_shape=jax.ShapeDtypeStruct(xs.shape, F32),
        compiler_params=_cparams(("arbitrary",)),
        name="expert_mlp",
    )(block_e, n_used, xs, w_gu, w_down)


def _combine_kernel(dest_ref, ys_ref, x_ref, ri_ref, g2_ref, fg_ref, o_ref, buf, sem, *, final):
    bi = pl.program_id(0)
    t = pl.program_id(1)
    nt = pl.num_programs(1)
    step = bi * nt + t
    total = pl.num_programs(0) * nt

    def copy(src_row, slot, k, i):
        return pltpu.make_async_copy(ys_ref.at[src_row], buf.at[slot, k * TM + i], sem.at[slot])

    def issue(st, slot):
        for i in range(TM):
            copy(dest_ref[st, i], slot, 0, i).start()
            copy(dest_ref[st, TM + i], slot, 1, i).start()

    slot = step % 2

    @pl.when(step == 0)
    def _():
        issue(0, 0)

    @pl.when(step + 1 < total)
    def _():
        issue(step + 1, 1 - slot)

    pltpu.make_async_copy(ys_ref.at[pl.ds(0, 2 * TM)], buf.at[slot], sem.at[slot]).wait()

    lane = lax.broadcasted_iota(jnp.int32, (TM, LANES), 1)
    ri = ri_ref[...]
    w1 = jnp.sum(jnp.where(lane == 2, ri, 0.0), axis=-1, keepdims=True)
    w2 = jnp.sum(jnp.where(lane == 3, ri, 0.0), axis=-1, keepdims=True)
    y = (_from_token_tiles(buf[slot, 0:TM]) * w1 + _from_token_tiles(buf[slot, TM:2 * TM]) * w2)
    xn = x_ref[...] + g2_ref[...] * y
    if final:
        ms = jnp.mean(xn * xn, axis=-1, keepdims=True)
        xn = xn * lax.rsqrt(ms + NORM_EPS) * fg_ref[...]
    o_ref[...] = xn


def _combine(dest, ys, xa, rinfo, g2, final_g, has_ctx, final):
    b, s, d = xa.shape
    nt = s // TM
    mod_idx = lambda bi, t, dr: (jnp.where(t == 0, b, bi) if has_ctx else bi, 0, 0)
    tile = lambda wdt: pl.BlockSpec((None, TM, wdt), lambda bi, t, dr: (bi, t, 0))
    out_spec = tile(d)
    out_shape = jax.ShapeDtypeStruct((b, s, d), F32)
    return pl.pallas_call(
        functools.partial(_combine_kernel, final=final),
        grid_spec=pltpu.PrefetchScalarGridSpec(
            num_scalar_prefetch=1,
            grid=(b, nt),
            in_specs=[
                pl.BlockSpec(memory_space=pl.ANY),
                tile(d), tile(LANES),
                pl.BlockSpec((None, 1, d), mod_idx),
                pl.BlockSpec((1, d), lambda bi, t, dr: (0, 0)),
            ],
            out_specs=out_spec,
            scratch_shapes=[pltpu.VMEM((2, 2 * TM, SUBLANES, d // SUBLANES), F32),
                            pltpu.SemaphoreType.DMA((2,))],
        ),
        out_shape=out_shape,
        compiler_params=_cparams(("arbitrary", "arbitrary")),
        name="combine",
    )(dest, ys, xa, rinfo, g2, final_g)


def _moe(h2, rt, counts, w_gu, w_down, layer):
    b, s, d = h2.shape
    nt = s // TM
    r = rt.reshape(b * nt, SUBLANES, TM)
    fields = lambda i: jnp.concatenate([r[:, i], r[:, i + 1]], axis=1).astype(jnp.int32)
    e = fields(0)
    rank = fields(4)
    cnt = counts[ROUTE_LANE0:ROUTE_LANE0 + MOE_EXPERTS, 0].astype(jnp.int32)
    padded = (cnt + MOE_ROWS - 1) // MOE_ROWS * MOE_ROWS
    ends = jnp.cumsum(padded)
    starts = ends - padded
    eids = jnp.arange(MOE_EXPERTS, dtype=jnp.int32)
    dest = jnp.sum(jnp.where(e[..., None] == eids, starts, 0), axis=-1) + rank
    n_assign = b * nt * TM * 2
    n_blocks = (n_assign + MOE_EXPERTS * (MOE_ROWS - 1)) // MOE_ROWS + 1
    n_pad = n_blocks * MOE_ROWS
    blk_row = jnp.arange(n_blocks, dtype=jnp.int32) * MOE_ROWS
    block_e = jnp.minimum(jnp.sum((blk_row[:, None] >= ends[None, :]).astype(jnp.int32), axis=1),
                          MOE_EXPERTS - 1)
    n_used = (ends[-1] // MOE_ROWS).astype(jnp.int32).reshape(1)
    pads = jnp.stack([jnp.append(starts + cnt, ends[-1]),
                      jnp.append(padded - cnt, (n_pad - ends[-1]) // PAD_BITS[0])]).astype(jnp.int32)
    xs = _dispatch(dest, pads, h2, n_pad)
    ys = _expert_mlp(block_e, n_used, xs, w_gu, w_down, layer)
    return dest, ys


def _router_weights(wg, bg, we, be):
    d = wg.shape[0]
    pad = LANES - MOE_GROUPS - MOE_EXPERTS
    assert ROUTE_LANE0 == MOE_GROUPS
    wr = jnp.concatenate([wg.astype(F32), we.astype(F32), jnp.zeros((d, pad), F32)], axis=1)
    br = jnp.concatenate([bg.astype(F32), be.astype(F32), jnp.zeros((pad,), F32)]).reshape(1, LANES)
    return jnp.stack(_split_bf16(wr)), br


def kernel(x, c, ctx, c_ctx, ada_w, ada_b, norm_g, final_g, ab_w_in, ab_w_out, ret_decay, ret_gn, na_rpb,
           swa_w_in, swa_w_out, swa_sink, router_g_w, router_g_b, router_e_w, router_e_b,
           expert_w_gu, expert_w_down):
    b, seq, d = x.shape
    assert ctx.shape[1] == TM and seq % TM == 0 and d == D_MODEL
    xa = (ctx, x)
    rope = _rope_tables(seq)

    cvec = jnp.concatenate([c, c_ctx[None, :], jnp.zeros((7, d), F32)], axis=0)
    mod = _adaln(cvec, ada_w, ada_b)
    mod = mod.reshape(DEPTH, b + 8, 6, 1, d)[:, :b + 1].transpose(0, 2, 1, 3, 4)

    for layer in range(DEPTH):
        last = layer == DEPTH - 1
        sh1, sc1, g1, sh2, sc2, g2 = (mod[layer, i] for i in range(6))
        gain1 = norm_g[layer, 0].reshape(1, d)
        gain2 = norm_g[layer, 1].reshape(1, d)
        j = layer // 2
        if layer % 2 == 0:
            nb = RET_W // LANES
            qk_scale = HEAD_DIM ** -0.5
            scales = {blk: qk_scale for blk in range(nb, 2 * nb)}
            scales.update({blk: qk_scale * LOG2E for blk in range(4 * nb, 5 * nb)})
            p = _proj(xa, gain1, sh1, sc1, ab_w_in[j].astype(BF16), rope,
                      rope_blocks=range(0, 2 * nb), scales=scales)
            log_gamma = jnp.log1p(-jnp.exp2(-ret_decay[j].astype(F32)))
            mixes = [_retention(p, log_gamma, ret_gn[j]), _neighbourhood(p, _na_bias(na_rpb[j]))]
            w_out = ab_w_out[j]
        else:
            qb = SWA_HEADS * HEAD_DIM // LANES
            kb = SWA_KV_HEADS * HEAD_DIM // LANES
            p = _proj(xa, gain1, sh1, sc1, swa_w_in[j].astype(BF16), rope,
                      rope_blocks=range(0, qb + kb),
                      scales={blk: HEAD_DIM ** -0.5 * LOG2E for blk in range(qb)})
            mixes = [_swa(p, swa_sink[j].astype(F32))]
            w_out = swa_w_out[j]
        t0 = 1 if last else 0
        wr, br = _router_weights(router_g_w[layer], router_g_b[layer], router_e_w[layer], router_e_b[layer])
        xa, h2, rinfo, rt, counts = _outproj(mixes, w_out.astype(BF16), xa, g1, gain2, sh2, sc2, wr, br, t0)
        dest, ys = _moe(h2, rt, counts, expert_w_gu, expert_w_down, layer)
        xa = _combine(dest, ys, xa, rinfo, g2, final_g.reshape(1, d), not last, last)
    return xa
```

```python
import functools

import numpy as np
import jax
import jax.numpy as jnp
from jax import lax
from jax.experimental import pallas as pl
from jax.experimental.pallas import tpu as pltpu

F32 = jnp.float32
BF16 = jnp.bfloat16

D_MODEL = 1024
DEPTH = 2
GRID_W = 64
HEAD_DIM = 64
RET_HEADS = 8
NA_HEADS = 8
RET_W = 512
NA_W = 512
AB_IN = 4 * RET_W + 3 * NA_W
GN_EPS = 1e-5
NA_KH = 8
NA_KW = 16
SWA_HEADS = 16
SWA_KV_HEADS = 4
SWA_WINDOW = 128
SWA_BLOCK = 128
SWA_IN = (SWA_HEADS + 2 * SWA_KV_HEADS) * HEAD_DIM
ROPE_BASE = 10000.0
MOE_GROUPS = 4
MOE_EPG = 8
MOE_EXPERTS = 32
MOE_FF = 512
NORM_EPS = 1e-6
NEG_INF = -1e30

LANES = 128
TM = 256
RET_BLOCK = 256
MOE_ROWS = 512
ROUTE_LANE0 = 4
ROUTE_ROWS = 48
VMEM_LIMIT = 56 * 1024 * 1024


def _cparams(sem):
    return pltpu.CompilerParams(dimension_semantics=sem, vmem_limit_bytes=VMEM_LIMIT)


def _split_bf16(a):
    hi = a.astype(BF16)
    lo = (a - hi.astype(F32)).astype(BF16)
    return hi, lo


def _dot3_split(a, bh, bl):
    ah, al = _split_bf16(a)
    d = lambda x, y: jnp.dot(x, y, preferred_element_type=F32)
    return d(ah, bh) + (d(ah, bl) + d(al, bh))


def _dot3(a, b):
    return _dot3_split(a, *_split_bf16(b))


def _dot_nt(a, b):
    return lax.dot_general(a, b, (((1,), (1,)), ((), ())), preferred_element_type=F32)


def _dot_tn(a, b):
    return lax.dot_general(a, b, (((0,), (0,)), ((), ())), preferred_element_type=F32)


def _silu(x):
    return x / (1.0 + jnp.exp(-x))


def _adaln_kernel(c_ref, w_ref, b_ref, o_ref):
    o_ref[...] = _dot3(_silu(c_ref[...]), w_ref[...]) + b_ref[...]


def _adaln(cvec, ada_w, ada_b):
    depth, d, n6 = ada_w.shape
    rows = cvec.shape[0]
    tn = 1024
    return pl.pallas_call(
        _adaln_kernel,
        grid=(depth, n6 // tn),
        in_specs=[
            pl.BlockSpec((rows, d), lambda l, j: (0, 0)),
            pl.BlockSpec((None, d, tn), lambda l, j: (l, 0, j)),
            pl.BlockSpec((None, 1, tn), lambda l, j: (l, 0, j)),
        ],
        out_specs=pl.BlockSpec((None, rows, tn), lambda l, j: (l, 0, j)),
        out_shape=jax.ShapeDtypeStruct((depth, rows, n6), F32),
        compiler_params=_cparams(("arbitrary", "arbitrary")),
        name="adaln",
    )(cvec, ada_w, ada_b.reshape(depth, 1, n6))


def _rms_mod(x, g, sh, sc):
    ms = jnp.mean(x * x, axis=-1, keepdims=True)
    return (x * lax.rsqrt(ms + NORM_EPS) * g) * (1.0 + sc) + sh


def _stream_tile(refs, is_lat):
    if len(refs) == 1:
        return refs[0][...]
    return jnp.where(is_lat, refs[1][...], refs[0][...])


def _stream_specs(stream, t0):
    def im(f):
        return (lambda bi, t, *_: f(bi, t + t0))
    if not isinstance(stream, tuple):
        return [pl.BlockSpec((None, TM, stream.shape[2]), im(lambda bi, t: (bi, t, 0)))], [stream]
    ctx, x = stream
    d = x.shape[2]
    return ([pl.BlockSpec((None, TM, d), im(lambda bi, t: (bi, 0, 0))),
             pl.BlockSpec((None, TM, d), im(lambda bi, t: (bi, jnp.maximum(t - 1, 0), 0)))], [ctx, x])


def _stream_shape(stream):
    if not isinstance(stream, tuple):
        return stream.shape
    ctx, x = stream
    return (x.shape[0], ctx.shape[1] + x.shape[1], x.shape[2])


def _proj_kernel(*refs, rope_blocks, scales, cn, n_stream):
    x_refs = refs[:n_stream]
    g_ref, sh_ref, sc_ref, w_ref, rope_ref, o_ref = refs[n_stream:]
    is_lat = pl.program_id(1) > 0
    hb = _rms_mod(_stream_tile(x_refs, is_lat), g_ref[...], sh_ref[...], sc_ref[...]).astype(BF16)
    nout = w_ref.shape[1]
    for c in range(nout // cn):
        o = jnp.dot(hb, w_ref[:, c * cn:(c + 1) * cn], preferred_element_type=F32)
        for s in range(cn // LANES):
            blk = c * (cn // LANES) + s
            ob = o[:, s * LANES:(s + 1) * LANES]
            if blk in rope_blocks:
                r = (ob * rope_ref[0] + pltpu.roll(ob, 16, 1) * rope_ref[1]
                     + pltpu.roll(ob, LANES - 16, 1) * rope_ref[2])
                ob = jnp.where(is_lat, r, ob)
            if blk in scales:
                ob = ob * scales[blk]
            o_ref[:, blk * LANES:(blk + 1) * LANES] = ob.astype(o_ref.dtype)


def _proj(xa, gain, sh, sc, w_bf16, rope, rope_blocks, scales):
    b, s, d = _stream_shape(xa)
    nout = w_bf16.shape[1]
    nt = s // TM
    mod_idx = lambda bi, t: (jnp.where(t == 0, b, bi), 0, 0)
    x_specs, x_args = _stream_specs(xa, 0)
    kern = functools.partial(_proj_kernel, rope_blocks=frozenset(rope_blocks), scales=dict(scales), cn=512,
                             n_stream=len(x_args))
    return pl.pallas_call(
        kern,
        grid=(b, nt),
        in_specs=x_specs + [
            pl.BlockSpec((1, d), lambda bi, t: (0, 0)),
            pl.BlockSpec((None, 1, d), mod_idx),
            pl.BlockSpec((None, 1, d), mod_idx),
            pl.BlockSpec((d, nout), lambda bi, t: (0, 0)),
            pl.BlockSpec((3, TM, LANES), lambda bi, t: (0, jnp.maximum(t - 1, 0), 0)),
        ],
        out_specs=pl.BlockSpec((None, TM, nout), lambda bi, t: (bi, t, 0)),
        out_shape=jax.ShapeDtypeStruct((b, s, nout), BF16),
        compiler_params=_cparams(("arbitrary", "arbitrary")),
        name="proj",
    )(*x_args, gain, sh, sc, w_bf16, rope)


def _rope_tables(seq):
    nf = HEAD_DIM // 4
    inv = ROPE_BASE ** (-jnp.arange(nf, dtype=F32) / nf)
    t = jnp.arange(seq)
    row = (t // GRID_W).astype(F32)
    col = (t % GRID_W).astype(F32)
    lane = np.arange(LANES)
    jj = lane % HEAD_DIM
    axis_is_col = (jj // 32) == 1
    second_half = (jj % 32) >= 16
    f = jj % 16
    pos = jnp.where(axis_is_col[None, :], col[:, None], row[:, None])
    ang = pos * inv[f][None, :]
    c, s = jnp.cos(ang), jnp.sin(ang)
    sa = jnp.where(second_half[None, :], s, 0.0)
    sb = jnp.where(second_half[None, :], 0.0, -s)
    return jnp.stack([c, sa, sb], axis=0)


def _ret_kernel(lg_ref, q_ref, k_ref, v_ref, g_ref, gn_ref, o_ref, accf_ref, accb_ref,
                intra_ref, qd_ref, kd_ref):
    hp = pl.program_id(0)
    c = RET_BLOCK
    s_len = q_ref.shape[0]
    n_chunks = s_len // c
    ctx_chunks = TM // c
    pos = lax.broadcasted_iota(jnp.int32, (c, LANES), 0).astype(F32)
    m0 = lax.broadcasted_iota(jnp.int32, (c, LANES), 1) < HEAD_DIM
    same_head = ((lax.broadcasted_iota(jnp.int32, (LANES, LANES), 0) < HEAD_DIM)
                 == (lax.broadcasted_iota(jnp.int32, (LANES, LANES), 1) < HEAD_DIM))
    rel = (lax.broadcasted_iota(jnp.int32, (c, c), 0) - lax.broadcasted_iota(jnp.int32, (c, c), 1)).astype(F32)
    lgf = [lg_ref[0, hp * 2 + hh] for hh in range(2)]
    lgb = [lg_ref[1, hp * 2 + hh] for hh in range(2)]
    lgf_l = jnp.where(m0, lgf[0], lgf[1])
    lgb_l = jnp.where(m0, lgb[0], lgb[1])
    @pl.when(pl.program_id(1) == 0)
    def _():
        for hh in range(2):
            intra_ref[0, :, hh * c:(hh + 1) * c] = jnp.where(
                rel >= 0, jnp.exp(lgf[hh] * jnp.maximum(rel, 0.0)), 0.0)
            intra_ref[1, :, hh * c:(hh + 1) * c] = jnp.where(
                rel <= 0, jnp.exp(lgb[hh] * jnp.maximum(-rel, 0.0)), 0.0)
        qd_ref[0] = jnp.exp(lgf_l * (pos + 1.0))
        qd_ref[1] = jnp.exp(lgb_l * (c - pos))
        kd_ref[0] = jnp.exp(lgf_l * (c - 1.0 - pos))
        kd_ref[1] = jnp.exp(lgb_l * pos)
    cd = [jnp.exp(lgf_l[0:1] * float(c)), jnp.exp(lgb_l[0:1] * float(c))]

    def chunk(r0, state, d):
        q = q_ref[pl.ds(r0, c), :]
        k = k_ref[pl.ds(r0, c), :]
        v = v_ref[pl.ds(r0, c), :]
        qb = q.astype(BF16)
        kcat = jnp.concatenate([jnp.where(m0, k, 0.0), jnp.where(m0, 0.0, k)], axis=0).astype(BF16)
        vcat = jnp.concatenate([jnp.where(m0, v, 0.0), jnp.where(m0, 0.0, v)], axis=0).astype(BF16)
        s = _dot_nt(qb, kcat) * intra_ref[d]
        out = jnp.dot(s.astype(BF16), vcat, preferred_element_type=F32)
        out = out + jnp.dot(qb, state.astype(BF16), preferred_element_type=F32) * qd_ref[d]
        kv = _dot_tn((k * kd_ref[d]).astype(BF16), v.astype(BF16))
        return out, state * cd[d] + jnp.where(same_head, kv, 0.0)

    def body(i, states):
        sf, sb = states
        rf = pl.multiple_of(i * c, c)
        ib = jnp.where(i < ctx_chunks, ctx_chunks - 1 - i, n_chunks + ctx_chunks - 1 - i)
        rb = pl.multiple_of(ib * c, c)
        of, sf = chunk(rf, sf, 0)
        ob, sb = chunk(rb, sb, 1)
        accf_ref[pl.ds(rf, c), :] = of
        accb_ref[pl.ds(rb, c), :] = ob
        return sf, sb

    z = jnp.zeros((LANES, LANES), F32)
    lax.fori_loop(0, n_chunks, body, (z, z), unroll=True)

    avg = jnp.where(same_head, 1.0 / HEAD_DIM, 0.0).astype(BF16)

    def head_mean(x):
        hi, lo = _split_bf16(x)
        return (jnp.dot(hi, avg, preferred_element_type=F32) + jnp.dot(lo, avg, preferred_element_type=F32))

    def readout(i, carry):
        r0 = pl.multiple_of(i * c, c)
        o = accf_ref[pl.ds(r0, c), :] + accb_ref[pl.ds(r0, c), :]
        dlt = o - head_mean(o)
        var = head_mean(dlt * dlt)
        y = dlt * lax.rsqrt(var + GN_EPS) * gn_ref[...]
        o_ref[pl.ds(r0, c), :] = _silu(g_ref[pl.ds(r0, c), :].astype(F32)) * y
        return carry

    lax.fori_loop(0, n_chunks, readout, 0, unroll=True)


def _retention(p, log_gamma, ret_gn):
    b, s, _ = p.shape
    nb = RET_W // LANES
    blk = lambda off: pl.BlockSpec((None, s, LANES), lambda hp, bi: (bi, 0, off + hp))
    return pl.pallas_call(
        _ret_kernel,
        grid=(nb, b),
        in_specs=[
            pl.BlockSpec(memory_space=pltpu.SMEM),
            blk(0), blk(nb), blk(2 * nb), blk(3 * nb),
            pl.BlockSpec((1, LANES), lambda hp, bi: (0, hp)),
        ],
        out_specs=pl.BlockSpec((None, s, LANES), lambda hp, bi: (bi, 0, hp)),
        out_shape=jax.ShapeDtypeStruct((b, s, RET_W), F32),
        scratch_shapes=[pltpu.VMEM((s, LANES), F32), pltpu.VMEM((s, LANES), F32),
                        pltpu.VMEM((2, RET_BLOCK, 2 * RET_BLOCK), F32),
                        pltpu.VMEM((2, RET_BLOCK, LANES), F32), pltpu.VMEM((2, RET_BLOCK, LANES), F32)],
        compiler_params=_cparams(("arbitrary", "arbitrary")),
        name="retention",
    )(log_gamma, p, p, p, p, ret_gn.reshape(1, RET_W))


LOG2E = 1.4426950408889634


def _softmax_pv(s_list, v_list, extra=None):
    m = None
    for s in s_list:
        for j in range(s.shape[1] // LANES):
            blk = s[:, j * LANES:(j + 1) * LANES]
            m = blk if m is None else jnp.maximum(m, blk)
    m = m.max(axis=-1, keepdims=True)
    if extra is not None:
        m = jnp.maximum(m, extra)
    acc = None
    for s, v in zip(s_list, v_list):
        pv = jnp.dot(jnp.exp2(s - m).astype(BF16), v, preferred_element_type=F32)
        acc = pv if acc is None else acc + pv
    o, den = acc[:, :LANES], acc[:, LANES:]
    if extra is not None:
        den = den + jnp.exp2(extra - m)
    return o / den


def _stage_heads(q_ref, k_ref, v_ref, qm_ref, kb_ref, vb_ref):
    m0 = lax.broadcasted_iota(jnp.int32, (TM, LANES), 1) < HEAD_DIM

    def stage(i, carry):
        r0 = pl.multiple_of(i * TM, TM)
        q = q_ref[pl.ds(r0, TM), :]
        qm_ref[0, pl.ds(r0, TM), :] = jnp.where(m0, q, 0.0).astype(BF16)
        qm_ref[1, pl.ds(r0, TM), :] = jnp.where(m0, 0.0, q).astype(BF16)
        kb_ref[pl.ds(r0, TM), :] = k_ref[pl.ds(r0, TM), :].astype(BF16)
        vb_ref[pl.ds(r0, TM), 0:LANES] = v_ref[pl.ds(r0, TM), :].astype(BF16)
        vb_ref[pl.ds(r0, TM), LANES:2 * LANES] = jnp.ones((TM, LANES), BF16)
        return carry

    lax.fori_loop(0, q_ref.shape[0] // TM, stage, 0)


def _na_kernel(q_ref, k_ref, v_ref, bias_ref, o_ref, qm_ref, kb_ref, vb_ref):
    s_len = q_ref.shape[0]
    rows = (s_len - TM) // GRID_W
    nloc = NA_KH * GRID_W
    _stage_heads(q_ref, k_ref, v_ref, qm_ref, kb_ref, vb_ref)

    kc = kb_ref[0:TM, :]
    vc = vb_ref[0:TM, :]
    outs = [_softmax_pv([_dot_nt(qm_ref[hh, 0:TM, :], kc)], [vc]) for hh in range(2)]
    m0c = lax.broadcasted_iota(jnp.int32, (TM, LANES), 1) < HEAD_DIM
    o_ref[0:TM, :] = jnp.where(m0c, outs[0], outs[1])

    m0 = lax.broadcasted_iota(jnp.int32, (GRID_W, LANES), 1) < HEAD_DIM

    def row_block(r, carry):
        rs = jnp.clip(r - NA_KH // 2, 0, rows - NA_KH)
        pat = r - rs
        q0 = pl.multiple_of(TM + r * GRID_W, GRID_W)
        k0 = pl.multiple_of(TM + rs * GRID_W, GRID_W)
        kl = kb_ref[pl.ds(k0, nloc), :]
        vl = vb_ref[pl.ds(k0, nloc), :]
        kc = kb_ref[0:TM, :]
        vc = vb_ref[0:TM, :]
        q = jnp.concatenate([qm_ref[0, pl.ds(q0, GRID_W), :], qm_ref[1, pl.ds(q0, GRID_W), :]], axis=0)
        s_loc = _dot_nt(q, kl) + bias_ref[pat].reshape(2 * GRID_W, nloc)
        s_ctx = _dot_nt(q, kc)
        res = _softmax_pv([s_loc, s_ctx], [vl, vc])
        o_ref[pl.ds(q0, GRID_W), :] = jnp.where(m0, res[:GRID_W], res[GRID_W:])
        return carry

    lax.fori_loop(0, rows, row_block, 0, unroll=True)


def _na_bias(rpb):
    h = rpb.shape[0]
    qc = np.arange(GRID_W)[:, None]
    kc = np.arange(GRID_W)[None, :]
    win = np.clip(qc - NA_KW // 2, 0, GRID_W - NA_KW)
    valid = (kc >= win) & (kc < win + NA_KW)
    col_off = np.clip(kc - qc + NA_KW - 1, 0, 2 * NA_KW - 2)
    onehot = (col_off[None] == np.arange(2 * NA_KW - 1)[:, None, None]).astype(np.float32)
    cols = jnp.einsum("hrc,cqk->hrqk", rpb.astype(F32), onehot, precision=lax.Precision.HIGHEST)
    cols = jnp.where(valid[None, None], cols * LOG2E, NEG_INF)
    bias = jnp.stack([cols[:, NA_KH - 1 - p:2 * NA_KH - 1 - p] for p in range(NA_KH)])
    bias = bias.transpose(0, 1, 3, 2, 4)
    return bias.reshape(NA_KH, h, GRID_W, NA_KH * GRID_W)


def _neighbourhood(p, bias):
    b, s, _ = p.shape
    nb = NA_W // LANES
    c0 = 4 * RET_W // LANES
    blk = lambda off: pl.BlockSpec((None, s, LANES), lambda hp, bi: (bi, 0, c0 + off + hp))
    return pl.pallas_call(
        _na_kernel,
        grid=(nb, b),
        in_specs=[
            blk(0), blk(nb), blk(2 * nb),
            pl.BlockSpec((NA_KH, 2, GRID_W, NA_KH * GRID_W), lambda hp, bi: (0, hp, 0, 0)),
        ],
        out_specs=pl.BlockSpec((None, s, LANES), lambda hp, bi: (bi, 0, hp)),
        out_shape=jax.ShapeDtypeStruct((b, s, NA_W), F32),
        scratch_shapes=[pltpu.VMEM((2, s, LANES), BF16), pltpu.VMEM((s, LANES), BF16),
                        pltpu.VMEM((s, 2 * LANES), BF16)],
        compiler_params=_cparams(("arbitrary", "arbitrary")),
        name="neighbourhood",
    )(p, p, p, bias)


def _swa_kernel(sink_ref, q_ref, k_ref, v_ref, o_ref, kd_ref, vd_ref):
    kp = pl.program_id(1)
    s_len = q_ref.shape[0]
    seq = s_len - TM
    nblk = seq // SWA_BLOCK
    band = SWA_BLOCK + 2 * SWA_WINDOW
    group = SWA_HEADS // SWA_KV_HEADS
    heads_per_step = 2 * group
    m0t = lax.broadcasted_iota(jnp.int32, (TM, LANES), 1) < HEAD_DIM

    def stage(i, carry):
        r0 = pl.multiple_of(i * TM, TM)
        for src, dst in ((k_ref, kd_ref), (v_ref, vd_ref)):
            x = src[pl.ds(r0, TM), :].astype(F32)
            xr = pltpu.roll(x, HEAD_DIM, 1)
            dst[0, pl.ds(r0, TM), 0:LANES] = jnp.where(m0t, x, xr).astype(BF16)
            dst[1, pl.ds(r0, TM), 0:LANES] = jnp.where(m0t, xr, x).astype(BF16)
        for h in range(2):
            vd_ref[h, pl.ds(r0, TM), LANES:2 * LANES] = jnp.ones((TM, LANES), BF16)
        return carry

    lax.fori_loop(0, s_len // TM, stage, 0)
    o_ref[0:TM, :] = jnp.zeros((TM, o_ref.shape[1]), F32)

    rows = group * SWA_BLOCK
    qi = lax.broadcasted_iota(jnp.int32, (rows, band), 0) % SWA_BLOCK
    ki = lax.broadcasted_iota(jnp.int32, (rows, band), 1)
    m0 = lax.broadcasted_iota(jnp.int32, (SWA_BLOCK, LANES), 1) < HEAD_DIM
    head_of_row = lax.broadcasted_iota(jnp.int32, (rows, 1), 0) // SWA_BLOCK

    def block(i, carry):
        start = jnp.clip((i - 1) * SWA_BLOCK, 0, seq - band)
        k0 = pl.multiple_of(TM + start, SWA_BLOCK)
        q0 = pl.multiple_of(TM + i * SWA_BLOCK, SWA_BLOCK)
        valid = jnp.abs(qi + (i * SWA_BLOCK - start) - ki) <= SWA_WINDOW
        for hh in range(2):
            kb = kd_ref[hh, pl.ds(k0, band), :]
            vb = vd_ref[hh, pl.ds(k0, band), :]
            kc = kd_ref[hh, 0:TM, :]
            vc = vd_ref[hh, 0:TM, :]
            parts = []
            for j in range(group // 2):
                pair = hh * (group // 2) + j
                qp = q_ref[pl.ds(q0, SWA_BLOCK), pair * LANES:(pair + 1) * LANES]
                parts += [jnp.where(m0, qp, 0.0), jnp.where(m0, 0.0, qp)]
            q = jnp.concatenate(parts, axis=0).astype(BF16)
            s = jnp.where(valid, _dot_nt(q, kb), NEG_INF)
            s_ctx = _dot_nt(q, kc)
            h0 = kp * heads_per_step + hh * group
            sink = jnp.full((rows, 1), sink_ref[h0], F32)
            for g in range(1, group):
                sink = jnp.where(head_of_row == g, sink_ref[h0 + g], sink)
            res = _softmax_pv([s, s_ctx], [vb, vc], extra=sink * LOG2E)
            for j in range(group // 2):
                pair = hh * (group // 2) + j
                r0 = 2 * j * SWA_BLOCK
                o_ref[pl.ds(q0, SWA_BLOCK), pair * LANES:(pair + 1) * LANES] = jnp.where(
                    m0, res[r0:r0 + SWA_BLOCK], res[r0 + SWA_BLOCK:r0 + 2 * SWA_BLOCK])
        return carry

    lax.fori_loop(0, nblk, block, 0, unroll=True)


def _swa(p, sink):
    b, s, _ = p.shape
    qw = SWA_HEADS * HEAD_DIM // 2
    kblk = SWA_HEADS * HEAD_DIM // LANES
    vblk = kblk + SWA_KV_HEADS * HEAD_DIM // LANES
    return pl.pallas_call(
        _swa_kernel,
        grid=(b, 2),
        in_specs=[
            pl.BlockSpec(memory_space=pltpu.SMEM),
            pl.BlockSpec((None, s, qw), lambda bi, kp: (bi, 0, kp)),
            pl.BlockSpec((None, s, LANES), lambda bi, kp: (bi, 0, kblk + kp)),
            pl.BlockSpec((None, s, LANES), lambda bi, kp: (bi, 0, vblk + kp)),
        ],
        out_specs=pl.BlockSpec((None, s, qw), lambda bi, kp: (bi, 0, kp)),
        out_shape=jax.ShapeDtypeStruct((b, s, SWA_HEADS * HEAD_DIM), F32),
        scratch_shapes=[pltpu.VMEM((2, s, LANES), BF16), pltpu.VMEM((2, s, 2 * LANES), BF16)],
        compiler_params=_cparams(("arbitrary", "arbitrary")),
        name="swa",
    )(sink, p, p, p)


def _outproj_kernel(*refs, n_mix, n_stream, t0):
    mix_refs = refs[:n_mix]
    x_refs = refs[n_mix + 1:n_mix + 1 + n_stream]
    w_ref = refs[n_mix]
    (g1_ref, gn_ref, sh_ref, sc_ref, wr_ref, br_ref,
     xo_ref, h_ref, ri_ref, rt_ref, cnt_ref, carry_ref) = refs[n_mix + 1 + n_stream:]
    first = (pl.program_id(0) == 0) & (pl.program_id(1) == 0)

    @pl.when(first)
    def _():
        carry_ref[...] = jnp.zeros_like(carry_ref)

    o = None
    off = 0
    for m_ref in mix_refs:
        kw = m_ref.shape[1]
        part = jnp.dot(m_ref[...].astype(BF16), w_ref[off:off + kw, :], preferred_element_type=F32)
        o = part if o is None else o + part
        off += kw
    xn = _stream_tile(x_refs, pl.program_id(1) + t0 > 0) + g1_ref[...] * o
    xo_ref[...] = xn
    h = _rms_mod(xn, gn_ref[...], sh_ref[...], sc_ref[...])
    h_ref[...] = h

    logits = _dot3_split(h, wr_ref[0], wr_ref[1]) + br_ref[...]
    tm = logits.shape[0]
    lt = logits.T[0:ROUTE_ROWS, :]
    row = lax.broadcasted_iota(jnp.int32, (ROUTE_ROWS, tm), 0).astype(F32)
    big = 1e9
    gmask = row < MOE_GROUPS
    mg = jnp.max(jnp.where(gmask, lt, -big), axis=0, keepdims=True)
    sg = jnp.sum(jnp.where(gmask, jnp.exp(jnp.minimum(lt - mg, 0.0)), 0.0), axis=0, keepdims=True)
    gw = 1.0 / sg
    gi = jnp.min(jnp.where(gmask & (lt == mg), row, big), axis=0, keepdims=True)
    lo = ROUTE_LANE0 + MOE_EPG * gi
    emask = (row >= lo) & (row < lo + MOE_EPG)
    l1 = jnp.max(jnp.where(emask, lt, -big), axis=0, keepdims=True)
    i1 = jnp.min(jnp.where(emask & (lt == l1), row, big), axis=0, keepdims=True)
    emask2 = emask & (row != i1)
    l2 = jnp.max(jnp.where(emask2, lt, -big), axis=0, keepdims=True)
    i2 = jnp.min(jnp.where(emask2 & (lt == l2), row, big), axis=0, keepdims=True)
    e21 = jnp.exp(l2 - l1)
    w1 = gw / (1.0 + e21)
    w2 = gw * e21 / (1.0 + e21)

    oh = jnp.where((row == i1) | (row == i2), 1.0, 0.0)
    before = (lax.broadcasted_iota(jnp.int32, (tm, tm), 0) < lax.broadcasted_iota(jnp.int32, (tm, tm), 1))
    cum = (jnp.dot(oh.astype(BF16), jnp.where(before, 1.0, 0.0).astype(BF16), preferred_element_type=F32)
           + carry_ref[:, 0:1])
    r1 = jnp.sum(jnp.where(row == i1, cum, 0.0), axis=0, keepdims=True)
    r2 = jnp.sum(jnp.where(row == i2, cum, 0.0), axis=0, keepdims=True)
    carry_ref[...] = carry_ref[...] + jnp.sum(oh, axis=1, keepdims=True)
    cnt_ref[...] = carry_ref[...]

    frow = lax.broadcasted_iota(jnp.int32, (LANES, tm), 0)
    rt = jnp.where(frow == 0, i1 - ROUTE_LANE0, 0.0)
    rt = jnp.where(frow == 1, i2 - ROUTE_LANE0, rt)
    rt = jnp.where(frow == 2, w1, rt)
    rt = jnp.where(frow == 3, w2, rt)
    rt = jnp.where(frow == 4, r1, rt)
    rt = jnp.where(frow == 5, r2, rt)
    rt_ref[...] = rt[0:SUBLANES, :]
    ri_ref[...] = rt.T


def _outproj(mixes, w_bf16, xa, g1, gain2, sh2, sc2, wr, br, t0):
    b, s, d = _stream_shape(xa)
    nt = s // TM - t0
    so = nt * TM
    mod_idx = lambda bi, t: (jnp.where(t + t0 == 0, b, bi), 0, 0)
    tile = lambda wdt: pl.BlockSpec((None, TM, wdt), lambda bi, t: (bi, t + t0, 0))
    otile = lambda wdt: pl.BlockSpec((None, TM, wdt), lambda bi, t: (bi, t, 0))
    const = lambda shape: pl.BlockSpec(shape, lambda bi, t: (0,) * len(shape))
    x_specs, x_args = _stream_specs(xa, t0)
    in_specs = [tile(m.shape[2]) for m in mixes] + [const(w_bf16.shape)] + x_specs + [
        pl.BlockSpec((None, 1, d), mod_idx), const((1, d)),
        pl.BlockSpec((None, 1, d), mod_idx), pl.BlockSpec((None, 1, d), mod_idx),
        const((2, d, LANES)), const((1, LANES)),
    ]
    return pl.pallas_call(
        functools.partial(_outproj_kernel, n_mix=len(mixes), n_stream=len(x_args), t0=t0),
        grid=(b, nt),
        in_specs=in_specs,
        out_specs=[otile(d), otile(d), otile(LANES),
                   pl.BlockSpec((None, None, SUBLANES, TM), lambda bi, t: (bi, t, 0, 0)),
                   const((ROUTE_ROWS, LANES))],
        out_shape=[jax.ShapeDtypeStruct((b, so, d), F32), jax.ShapeDtypeStruct((b, so, d), F32),
                   jax.ShapeDtypeStruct((b, so, LANES), F32), jax.ShapeDtypeStruct((b, nt, SUBLANES, TM), F32),
                   jax.ShapeDtypeStruct((ROUTE_ROWS, LANES), F32)],
        scratch_shapes=[pltpu.VMEM((ROUTE_ROWS, LANES), F32)],
        compiler_params=_cparams(("arbitrary", "arbitrary")),
        name="outproj",
    )(*mixes, w_bf16, *x_args, g1, gain2, sh2, sc2, wr, br)


SUBLANES = 8


def _to_token_tiles(x):
    return x.reshape(x.shape[0], SUBLANES, x.shape[1] // SUBLANES)


def _from_token_tiles(x3):
    return x3.reshape(x3.shape[0], x3.shape[1] * x3.shape[2])


PAD_BITS = tuple(1 << k for k in reversed(range(MOE_ROWS.bit_length() - 1)))


def _dispatch_kernel(dest_ref, pads_ref, h_ref, xs_ref, hbuf, zbuf, sem, zsem):
    nt = pl.num_programs(1)
    step = pl.program_id(0) * nt + pl.program_id(1)
    last = pl.num_programs(0) * nt - 1
    slot = step % 2

    def zero_pads(wait):
        def expert(e, c):
            pos = pads_ref[0, e]
            n = pads_ref[1, e]
            for bit in PAD_BITS:
                take = (n & bit) != 0
                cp = pltpu.make_async_copy(zbuf.at[pl.ds(0, bit)], xs_ref.at[pl.ds(pos, bit)], zsem)

                @pl.when(take)
                def _():
                    cp.wait() if wait else cp.start()

                pos = pos + jnp.where(take, bit, 0)
            return c
        lax.fori_loop(0, MOE_EXPERTS, expert, 0)

        def tail(j, c):
            cp = pltpu.make_async_copy(
                zbuf, xs_ref.at[pl.ds(pads_ref[0, MOE_EXPERTS] + j * PAD_BITS[0], PAD_BITS[0])], zsem)
            cp.wait() if wait else cp.start()
            return c
        lax.fori_loop(0, pads_ref[1, MOE_EXPERTS], tail, 0)

    @pl.when(step == 0)
    def _():
        zbuf[...] = jnp.zeros_like(zbuf)
        zero_pads(wait=False)
    hbuf[slot] = _to_token_tiles(h_ref[...])

    def copy(sl, i, dst_row):
        return pltpu.make_async_copy(hbuf.at[sl, i], xs_ref.at[dst_row], sem.at[sl])

    for i in range(TM):
        copy(slot, i, dest_ref[step, i]).start()
        copy(slot, i, dest_ref[step, TM + i]).start()

    def drain(sl):
        for _ in range(2):
            pltpu.make_async_copy(hbuf.at[sl], xs_ref.at[pl.ds(0, TM)], sem.at[sl]).wait()

    @pl.when(step > 0)
    def _():
        drain(1 - slot)

    @pl.when(step == last)
    def _():
        drain(slot)
        zero_pads(wait=True)


def _dispatch(dest, pads, h2, n_pad):
    b, s, d = h2.shape
    nt = s // TM
    tile = (SUBLANES, d // SUBLANES)
    return pl.pallas_call(
        _dispatch_kernel,
        grid_spec=pltpu.PrefetchScalarGridSpec(
            num_scalar_prefetch=2,
            grid=(b, nt),
            in_specs=[pl.BlockSpec((None, TM, d), lambda bi, t, dr, pd: (bi, t, 0))],
            out_specs=pl.BlockSpec(memory_space=pl.ANY),
            scratch_shapes=[pltpu.VMEM((2, TM) + tile, F32), pltpu.VMEM((PAD_BITS[0],) + tile, F32),
                            pltpu.SemaphoreType.DMA((2,)), pltpu.SemaphoreType.DMA],
        ),
        out_shape=jax.ShapeDtypeStruct((n_pad,) + tile, F32),
        compiler_params=_cparams(("arbitrary", "arbitrary")),
        name="dispatch",
    )(dest, pads, h2)


def _mlp_kernel(be_ref, nu_ref, x_ref, wgu_ref, wd_ref, y_ref, wgu_b, wd_b):
    i = pl.program_id(0)
    prev = be_ref[jnp.maximum(i - 1, 0)]
    used = i < nu_ref[0]

    @pl.when(used & ((i == 0) | (be_ref[i] != prev)))
    def _():
        wgu_b[...] = wgu_ref[...].astype(BF16)
        wd_b[...] = wd_ref[...].astype(BF16)

    @pl.when(used)
    def _():
        x = _from_token_tiles(x_ref[...])
        gu = jnp.dot(x.astype(BF16), wgu_b[...], preferred_element_type=F32)
        act = _silu(gu[:, :MOE_FF]) * gu[:, MOE_FF:]
        y_ref[...] = _to_token_tiles(jnp.dot(act.astype(BF16), wd_b[...], preferred_element_type=F32))

    @pl.when(jnp.logical_not(used))
    def _():
        y_ref[...] = jnp.zeros_like(y_ref)


def _expert_mlp(block_e, n_used, xs, w_gu, w_down, layer):
    n_pad, sub, dl = xs.shape
    d = sub * dl
    ff2 = w_gu.shape[-1]
    slots = pl.BlockSpec((MOE_ROWS, sub, dl), lambda i, be, nu: (i, 0, 0))
    used_slots = pl.BlockSpec((MOE_ROWS, sub, dl), lambda i, be, nu: (jnp.minimum(i, nu[0] - 1), 0, 0))
    return pl.pallas_call(
        _mlp_kernel,
        grid_spec=pltpu.PrefetchScalarGridSpec(
            num_scalar_prefetch=2,
            grid=(n_pad // MOE_ROWS,),
            in_specs=[
                used_slots,
                pl.BlockSpec((None, None, d, ff2), lambda i, be, nu: (layer, be[i], 0, 0)),
                pl.BlockSpec((None, None, ff2 // 2, d), lambda i, be, nu: (layer, be[i], 0, 0)),
            ],
            out_specs=slots,
            scratch_shapes=[pltpu.VMEM((d, ff2), BF16), pltpu.VMEM((ff2 // 2, d), BF16)],
        ),
        out_shape=jax.ShapeDtypeStruct(xs.shape, F32),
        compiler_params=_cparams(("arbitrary",)),
        name="expert_mlp",
    )(block_e, n_used, xs, w_gu, w_down)


def _combine_kernel(dest_ref, ys_ref, x_ref, ri_ref, g2_ref, fg_ref, o_ref, buf, sem, *, final):
    bi = pl.program_id(0)
    t = pl.program_id(1)
    nt = pl.num_programs(1)
    step = bi * nt + t
    total = pl.num_programs(0) * nt

    def copy(src_row, slot, k, i):
        return pltpu.make_async_copy(ys_ref.at[src_row], buf.at[slot, k * TM + i], sem.at[slot])

    def issue(st, slot):
        for i in range(TM):
            copy(dest_ref[st, i], slot, 0, i).start()
            copy(dest_ref[st, TM + i], slot, 1, i).start()

    slot = step % 2

    @pl.when(step == 0)
    def _():
        issue(0, 0)

    @pl.when(step + 1 < total)
    def _():
        issue(step + 1, 1 - slot)

    pltpu.make_async_copy(ys_ref.at[pl.ds(0, 2 * TM)], buf.at[slot], sem.at[slot]).wait()

    lane = lax.broadcasted_iota(jnp.int32, (TM, LANES), 1)
    ri = ri_ref[...]
    w1 = jnp.sum(jnp.where(lane == 2, ri, 0.0), axis=-1, keepdims=True)
    w2 = jnp.sum(jnp.where(lane == 3, ri, 0.0), axis=-1, keepdims=True)
    y = (_from_token_tiles(buf[slot, 0:TM]) * w1 + _from_token_tiles(buf[slot, TM:2 * TM]) * w2)
    xn = x_ref[...] + g2_ref[...] * y
    if final:
        ms = jnp.mean(xn * xn, axis=-1, keepdims=True)
        xn = xn * lax.rsqrt(ms + NORM_EPS) * fg_ref[...]
    o_ref[...] = xn


def _combine(dest, ys, xa, rinfo, g2, final_g, has_ctx, final):
    b, s, d = xa.shape
    nt = s // TM
    mod_idx = lambda bi, t, dr: (jnp.where(t == 0, b, bi) if has_ctx else bi, 0, 0)
    tile = lambda wdt: pl.BlockSpec((None, TM, wdt), lambda bi, t, dr: (bi, t, 0))
    out_spec = tile(d)
    out_shape = jax.ShapeDtypeStruct((b, s, d), F32)
    return pl.pallas_call(
        functools.partial(_combine_kernel, final=final),
        grid_spec=pltpu.PrefetchScalarGridSpec(
            num_scalar_prefetch=1,
            grid=(b, nt),
            in_specs=[
                pl.BlockSpec(memory_space=pl.ANY),
                tile(d), tile(LANES),
                pl.BlockSpec((None, 1, d), mod_idx),
                pl.BlockSpec((1, d), lambda bi, t, dr: (0, 0)),
            ],
            out_specs=out_spec,
            scratch_shapes=[pltpu.VMEM((2, 2 * TM, SUBLANES, d // SUBLANES), F32),
                            pltpu.SemaphoreType.DMA((2,))],
        ),
        out_shape=out_shape,
        compiler_params=_cparams(("arbitrary", "arbitrary")),
        name="combine",
    )(dest, ys, xa, rinfo, g2, final_g)


def _moe(h2, rt, counts, w_gu, w_down, layer):
    b, s, d = h2.shape
    nt = s // TM
    r = rt.reshape(b * nt, SUBLANES, TM)
    fields = lambda i: jnp.concatenate([r[:, i], r[:, i + 1]], axis=1).astype(jnp.int32)
    e = fields(0)
    rank = fields(4)
    cnt = counts[ROUTE_LANE0:ROUTE_LANE0 + MOE_EXPERTS, 0].astype(jnp.int32)
    padded = (cnt + MOE_ROWS - 1) // MOE_ROWS * MOE_ROWS
    ends = jnp.cumsum(padded)
    starts = ends - padded
    eids = jnp.arange(MOE_EXPERTS, dtype=jnp.int32)
    dest = jnp.sum(jnp.where(e[..., None] == eids, starts, 0), axis=-1) + rank
    n_assign = b * nt * TM * 2
    n_blocks = (n_assign + MOE_EXPERTS * (MOE_ROWS - 1)) // MOE_ROWS + 1
    n_pad = n_blocks * MOE_ROWS
    blk_row = jnp.arange(n_blocks, dtype=jnp.int32) * MOE_ROWS
    block_e = jnp.minimum(jnp.sum((blk_row[:, None] >= ends[None, :]).astype(jnp.int32), axis=1),
                          MOE_EXPERTS - 1)
    n_used = (ends[-1] // MOE_ROWS).astype(jnp.int32).reshape(1)
    pads = jnp.stack([jnp.append(starts + cnt, ends[-1]),
                      jnp.append(padded - cnt, (n_pad - ends[-1]) // PAD_BITS[0])]).astype(jnp.int32)
    xs = _dispatch(dest, pads, h2, n_pad)
    ys = _expert_mlp(block_e, n_used, xs, w_gu, w_down, layer)
    return dest, ys


def _router_weights(wg, bg, we, be):
    d = wg.shape[0]
    pad = LANES - MOE_GROUPS - MOE_EXPERTS
    assert ROUTE_LANE0 == MOE_GROUPS
    wr = jnp.concatenate([wg.astype(F32), we.astype(F32), jnp.zeros((d, pad), F32)], axis=1)
    br = jnp.concatenate([bg.astype(F32), be.astype(F32), jnp.zeros((pad,), F32)]).reshape(1, LANES)
    return jnp.stack(_split_bf16(wr)), br


def kernel(x, c, ctx, c_ctx, ada_w, ada_b, norm_g, final_g, ab_w_in, ab_w_out, ret_decay, ret_gn, na_rpb,
           swa_w_in, swa_w_out, swa_sink, router_g_w, router_g_b, router_e_w, router_e_b,
           expert_w_gu, expert_w_down):
    b, seq, d = x.shape
    assert ctx.shape[1] == TM and seq % TM == 0 and d == D_MODEL
    xa = (ctx, x)
    rope = _rope_tables(seq)

    cvec = jnp.concatenate([c, c_ctx[None, :], jnp.zeros((7, d), F32)], axis=0)
    mod = _adaln(cvec, ada_w, ada_b)
    mod = mod.reshape(DEPTH, b + 8, 6, 1, d)[:, :b + 1].transpose(0, 2, 1, 3, 4)

    for layer in range(DEPTH):
        last = layer == DEPTH - 1
        sh1, sc1, g1, sh2, sc2, g2 = (mod[layer, i] for i in range(6))
        gain1 = norm_g[layer, 0].reshape(1, d)
        gain2 = norm_g[layer, 1].reshape(1, d)
        j = layer // 2
        if layer % 2 == 0:
            nb = RET_W // LANES
            qk_scale = HEAD_DIM ** -0.5
            scales = {blk: qk_scale for blk in range(nb, 2 * nb)}
            scales.update({blk: qk_scale * LOG2E for blk in range(4 * nb, 5 * nb)})
            p = _proj(xa, gain1, sh1, sc1, ab_w_in[j].astype(BF16), rope,
                      rope_blocks=range(0, 2 * nb), scales=scales)
            log_gamma = jnp.log1p(-jnp.exp2(-ret_decay[j].astype(F32)))
            mixes = [_retention(p, log_gamma, ret_gn[j]), _neighbourhood(p, _na_bias(na_rpb[j]))]
            w_out = ab_w_out[j]
        else:
            qb = SWA_HEADS * HEAD_DIM // LANES
            kb = SWA_KV_HEADS * HEAD_DIM // LANES
            p = _proj(xa, gain1, sh1, sc1, swa_w_in[j].astype(BF16), rope,
                      rope_blocks=range(0, qb + kb),
                      scales={blk: HEAD_DIM ** -0.5 * LOG2E for blk in range(qb)})
            mixes = [_swa(p, swa_sink[j].astype(F32))]
            w_out = swa_w_out[j]
        t0 = 1 if last else 0
        wr, br = _router_weights(router_g_w[layer], router_g_b[layer], router_e_w[layer], router_e_b[layer])
        xa, h2, rinfo, rt, counts = _outproj(mixes, w_out.astype(BF16), xa, g1, gain2, sh2, sc2, wr, br, t0)
        dest, ys = _moe(h2, rt, counts, expert_w_gu, expert_w_down, layer)
        xa = _combine(dest, ys, xa, rinfo, g2, final_g.reshape(1, d), not last, last)
    return xa
```

```python
import functools

import numpy as np
import jax
import jax.numpy as jnp
from jax import lax
from jax.experimental import pallas as pl
from jax.experimental.pallas import tpu as pltpu

F32 = jnp.float32
BF16 = jnp.bfloat16

D_MODEL = 1024
DEPTH = 2
GRID_W = 64
HEAD_DIM = 64
RET_HEADS = 8
NA_HEADS = 8
RET_W = 512
NA_W = 512
AB_IN = 4 * RET_W + 3 * NA_W
GN_EPS = 1e-5
NA_KH = 8
NA_KW = 16
SWA_HEADS = 16
SWA_KV_HEADS = 4
SWA_WINDOW = 128
SWA_BLOCK = 128
SWA_IN = (SWA_HEADS + 2 * SWA_KV_HEADS) * HEAD_DIM
ROPE_BASE = 10000.0
MOE_GROUPS = 4
MOE_EPG = 8
MOE_EXPERTS = 32
MOE_FF = 512
NORM_EPS = 1e-6
NEG_INF = -1e30

LANES = 128
TM = 256
RET_BLOCK = 256
MOE_ROWS = 1024
ROUTE_LANE0 = 4
ROUTE_ROWS = 48
VMEM_LIMIT = 56 * 1024 * 1024


def _cparams(sem):
    return pltpu.CompilerParams(dimension_semantics=sem, vmem_limit_bytes=VMEM_LIMIT)


def _split_bf16(a):
    hi = a.astype(BF16)
    lo = (a - hi.astype(F32)).astype(BF16)
    return hi, lo


def _dot3_split(a, bh, bl):
    ah, al = _split_bf16(a)
    d = lambda x, y: jnp.dot(x, y, preferred_element_type=F32)
    return d(ah, bh) + (d(ah, bl) + d(al, bh))


def _dot3(a, b):
    return _dot3_split(a, *_split_bf16(b))


def _dot_nt(a, b):
    return lax.dot_general(a, b, (((1,), (1,)), ((), ())), preferred_element_type=F32)


def _dot_tn(a, b):
    return lax.dot_general(a, b, (((0,), (0,)), ((), ())), preferred_element_type=F32)


def _silu(x):
    return x / (1.0 + jnp.exp(-x))


def _adaln_kernel(c_ref, w_ref, b_ref, o_ref):
    o_ref[...] = _dot3(_silu(c_ref[...]), w_ref[...]) + b_ref[...]


def _adaln(cvec, ada_w, ada_b):
    depth, d, n6 = ada_w.shape
    rows = cvec.shape[0]
    tn = 1024
    return pl.pallas_call(
        _adaln_kernel,
        grid=(depth, n6 // tn),
        in_specs=[
            pl.BlockSpec((rows, d), lambda l, j: (0, 0)),
            pl.BlockSpec((None, d, tn), lambda l, j: (l, 0, j)),
            pl.BlockSpec((None, 1, tn), lambda l, j: (l, 0, j)),
        ],
        out_specs=pl.BlockSpec((None, rows, tn), lambda l, j: (l, 0, j)),
        out_shape=jax.ShapeDtypeStruct((depth, rows, n6), F32),
        compiler_params=_cparams(("arbitrary", "arbitrary")),
        name="adaln",
    )(cvec, ada_w, ada_b.reshape(depth, 1, n6))


def _rms_mod(x, g, sh, sc):
    ms = jnp.mean(x * x, axis=-1, keepdims=True)
    return (x * lax.rsqrt(ms + NORM_EPS) * g) * (1.0 + sc) + sh


def _stream_tile(refs, is_lat):
    if len(refs) == 1:
        return refs[0][...]
    return jnp.where(is_lat, refs[1][...], refs[0][...])


def _stream_specs(stream, t0):
    def im(f):
        return (lambda bi, t, *_: f(bi, t + t0))
    if not isinstance(stream, tuple):
        return [pl.BlockSpec((None, TM, stream.shape[2]), im(lambda bi, t: (bi, t, 0)))], [stream]
    ctx, x = stream
    d = x.shape[2]
    return ([pl.BlockSpec((None, TM, d), im(lambda bi, t: (bi, 0, 0))),
             pl.BlockSpec((None, TM, d), im(lambda bi, t: (bi, jnp.maximum(t - 1, 0), 0)))], [ctx, x])


def _stream_shape(stream):
    if not isinstance(stream, tuple):
        return stream.shape
    ctx, x = stream
    return (x.shape[0], ctx.shape[1] + x.shape[1], x.shape[2])


def _proj_kernel(*refs, rope_blocks, scales, cn, n_stream):
    x_refs = refs[:n_stream]
    g_ref, sh_ref, sc_ref, w_ref, rope_ref, o_ref = refs[n_stream:]
    is_lat = pl.program_id(1) > 0
    hb = _rms_mod(_stream_tile(x_refs, is_lat), g_ref[...], sh_ref[...], sc_ref[...]).astype(BF16)
    nout = w_ref.shape[1]
    for c in range(nout // cn):
        o = jnp.dot(hb, w_ref[:, c * cn:(c + 1) * cn], preferred_element_type=F32)
        for s in range(cn // LANES):
            blk = c * (cn // LANES) + s
            ob = o[:, s * LANES:(s + 1) * LANES]
            if blk in rope_blocks:
                r = (ob * rope_ref[0] + pltpu.roll(ob, 16, 1) * rope_ref[1]
                     + pltpu.roll(ob, LANES - 16, 1) * rope_ref[2])
                ob = jnp.where(is_lat, r, ob)
            if blk in scales:
                ob = ob * scales[blk]
            o_ref[:, blk * LANES:(blk + 1) * LANES] = ob.astype(o_ref.dtype)


def _proj(xa, gain, sh, sc, w_bf16, rope, rope_blocks, scales):
    b, s, d = _stream_shape(xa)
    nout = w_bf16.shape[1]
    nt = s // TM
    mod_idx = lambda bi, t: (jnp.where(t == 0, b, bi), 0, 0)
    x_specs, x_args = _stream_specs(xa, 0)
    kern = functools.partial(_proj_kernel, rope_blocks=frozenset(rope_blocks), scales=dict(scales), cn=512,
                             n_stream=len(x_args))
    return pl.pallas_call(
        kern,
        grid=(b, nt),
        in_specs=x_specs + [
            pl.BlockSpec((1, d), lambda bi, t: (0, 0)),
            pl.BlockSpec((None, 1, d), mod_idx),
            pl.BlockSpec((None, 1, d), mod_idx),
            pl.BlockSpec((d, nout), lambda bi, t: (0, 0)),
            pl.BlockSpec((3, TM, LANES), lambda bi, t: (0, jnp.maximum(t - 1, 0), 0)),
        ],
        out_specs=pl.BlockSpec((None, TM, nout), lambda bi, t: (bi, t, 0)),
        out_shape=jax.ShapeDtypeStruct((b, s, nout), BF16),
        compiler_params=_cparams(("arbitrary", "arbitrary")),
        name="proj",
    )(*x_args, gain, sh, sc, w_bf16, rope)


def _rope_tables(seq):
    nf = HEAD_DIM // 4
    inv = ROPE_BASE ** (-jnp.arange(nf, dtype=F32) / nf)
    t = jnp.arange(seq)
    row = (t // GRID_W).astype(F32)
    col = (t % GRID_W).astype(F32)
    lane = np.arange(LANES)
    jj = lane % HEAD_DIM
    axis_is_col = (jj // 32) == 1
    second_half = (jj % 32) >= 16
    f = jj % 16
    pos = jnp.where(axis_is_col[None, :], col[:, None], row[:, None])
    ang = pos * inv[f][None, :]
    c, s = jnp.cos(ang), jnp.sin(ang)
    sa = jnp.where(second_half[None, :], s, 0.0)
    sb = jnp.where(second_half[None, :], 0.0, -s)
    return jnp.stack([c, sa, sb], axis=0)


def _ret_kernel(lg_ref, q_ref, k_ref, v_ref, g_ref, gn_ref, o_ref, accf_ref, accb_ref,
                intra_ref, qd_ref, kd_ref):
    hp = pl.program_id(0)
    c = RET_BLOCK
    s_len = q_ref.shape[0]
    n_chunks = s_len // c
    ctx_chunks = TM // c
    pos = lax.broadcasted_iota(jnp.int32, (c, LANES), 0).astype(F32)
    m0 = lax.broadcasted_iota(jnp.int32, (c, LANES), 1) < HEAD_DIM
    same_head = ((lax.broadcasted_iota(jnp.int32, (LANES, LANES), 0) < HEAD_DIM)
                 == (lax.broadcasted_iota(jnp.int32, (LANES, LANES), 1) < HEAD_DIM))
    rel = (lax.broadcasted_iota(jnp.int32, (c, c), 0) - lax.broadcasted_iota(jnp.int32, (c, c), 1)).astype(F32)
    lgf = [lg_ref[0, hp * 2 + hh] for hh in range(2)]
    lgb = [lg_ref[1, hp * 2 + hh] for hh in range(2)]
    lgf_l = jnp.where(m0, lgf[0], lgf[1])
    lgb_l = jnp.where(m0, lgb[0], lgb[1])
    @pl.when(pl.program_id(1) == 0)
    def _():
        for hh in range(2):
            intra_ref[0, :, hh * c:(hh + 1) * c] = jnp.where(
                rel >= 0, jnp.exp(lgf[hh] * jnp.maximum(rel, 0.0)), 0.0)
            intra_ref[1, :, hh * c:(hh + 1) * c] = jnp.where(
                rel <= 0, jnp.exp(lgb[hh] * jnp.maximum(-rel, 0.0)), 0.0)
        qd_ref[0] = jnp.exp(lgf_l * (pos + 1.0))
        qd_ref[1] = jnp.exp(lgb_l * (c - pos))
        kd_ref[0] = jnp.exp(lgf_l * (c - 1.0 - pos))
        kd_ref[1] = jnp.exp(lgb_l * pos)
    cd = [jnp.exp(lgf_l[0:1] * float(c)), jnp.exp(lgb_l[0:1] * float(c))]

    def chunk(r0, state, d):
        q = q_ref[pl.ds(r0, c), :]
        k = k_ref[pl.ds(r0, c), :]
        v = v_ref[pl.ds(r0, c), :]
        qb = q.astype(BF16)
        kcat = jnp.concatenate([jnp.where(m0, k, 0.0), jnp.where(m0, 0.0, k)], axis=0).astype(BF16)
        vcat = jnp.concatenate([jnp.where(m0, v, 0.0), jnp.where(m0, 0.0, v)], axis=0).astype(BF16)
        s = _dot_nt(qb, kcat) * intra_ref[d]
        out = jnp.dot(s.astype(BF16), vcat, preferred_element_type=F32)
        out = out + jnp.dot(qb, state.astype(BF16), preferred_element_type=F32) * qd_ref[d]
        kv = _dot_tn((k * kd_ref[d]).astype(BF16), v.astype(BF16))
        return out, state * cd[d] + jnp.where(same_head, kv, 0.0)

    def body(i, states):
        sf, sb = states
        rf = pl.multiple_of(i * c, c)
        ib = jnp.where(i < ctx_chunks, ctx_chunks - 1 - i, n_chunks + ctx_chunks - 1 - i)
        rb = pl.multiple_of(ib * c, c)
        of, sf = chunk(rf, sf, 0)
        ob, sb = chunk(rb, sb, 1)
        accf_ref[pl.ds(rf, c), :] = of
        accb_ref[pl.ds(rb, c), :] = ob
        return sf, sb

    z = jnp.zeros((LANES, LANES), F32)
    lax.fori_loop(0, n_chunks, body, (z, z), unroll=True)

    avg = jnp.where(same_head, 1.0 / HEAD_DIM, 0.0).astype(BF16)

    def head_mean(x):
        hi, lo = _split_bf16(x)
        return (jnp.dot(hi, avg, preferred_element_type=F32) + jnp.dot(lo, avg, preferred_element_type=F32))

    def readout(i, carry):
        r0 = pl.multiple_of(i * c, c)
        o = accf_ref[pl.ds(r0, c), :] + accb_ref[pl.ds(r0, c), :]
        dlt = o - head_mean(o)
        var = head_mean(dlt * dlt)
        y = dlt * lax.rsqrt(var + GN_EPS) * gn_ref[...]
        o_ref[pl.ds(r0, c), :] = _silu(g_ref[pl.ds(r0, c), :].astype(F32)) * y
        return carry

    lax.fori_loop(0, n_chunks, readout, 0, unroll=True)


def _retention(p, log_gamma, ret_gn):
    b, s, _ = p.shape
    nb = RET_W // LANES
    blk = lambda off: pl.BlockSpec((None, s, LANES), lambda hp, bi: (bi, 0, off + hp))
    return pl.pallas_call(
        _ret_kernel,
        grid=(nb, b),
        in_specs=[
            pl.BlockSpec(memory_space=pltpu.SMEM),
            blk(0), blk(nb), blk(2 * nb), blk(3 * nb),
            pl.BlockSpec((1, LANES), lambda hp, bi: (0, hp)),
        ],
        out_specs=pl.BlockSpec((None, s, LANES), lambda hp, bi: (bi, 0, hp)),
        out_shape=jax.ShapeDtypeStruct((b, s, RET_W), F32),
        scratch_shapes=[pltpu.VMEM((s, LANES), F32), pltpu.VMEM((s, LANES), F32),
                        pltpu.VMEM((2, RET_BLOCK, 2 * RET_BLOCK), F32),
                        pltpu.VMEM((2, RET_BLOCK, LANES), F32), pltpu.VMEM((2, RET_BLOCK, LANES), F32)],
        compiler_params=_cparams(("arbitrary", "arbitrary")),
        name="retention",
    )(log_gamma, p, p, p, p, ret_gn.reshape(1, RET_W))


LOG2E = 1.4426950408889634


def _softmax_pv(s_list, v_list, extra=None):
    m = None
    for s in s_list:
        for j in range(s.shape[1] // LANES):
            blk = s[:, j * LANES:(j + 1) * LANES]
            m = blk if m is None else jnp.maximum(m, blk)
    m = m.max(axis=-1, keepdims=True)
    if extra is not None:
        m = jnp.maximum(m, extra)
    acc = None
    for s, v in zip(s_list, v_list):
        pv = jnp.dot(jnp.exp2(s - m).astype(BF16), v, preferred_element_type=F32)
        acc = pv if acc is None else acc + pv
    o, den = acc[:, :LANES], acc[:, LANES:]
    if extra is not None:
        den = den + jnp.exp2(extra - m)
    return o / den


def _stage_heads(q_ref, k_ref, v_ref, qm_ref, kb_ref, vb_ref):
    m0 = lax.broadcasted_iota(jnp.int32, (TM, LANES), 1) < HEAD_DIM

    def stage(i, carry):
        r0 = pl.multiple_of(i * TM, TM)
        q = q_ref[pl.ds(r0, TM), :]
        qm_ref[0, pl.ds(r0, TM), :] = jnp.where(m0, q, 0.0).astype(BF16)
        qm_ref[1, pl.ds(r0, TM), :] = jnp.where(m0, 0.0, q).astype(BF16)
        kb_ref[pl.ds(r0, TM), :] = k_ref[pl.ds(r0, TM), :].astype(BF16)
        vb_ref[pl.ds(r0, TM), 0:LANES] = v_ref[pl.ds(r0, TM), :].astype(BF16)
        vb_ref[pl.ds(r0, TM), LANES:2 * LANES] = jnp.ones((TM, LANES), BF16)
        return carry

    lax.fori_loop(0, q_ref.shape[0] // TM, stage, 0, unroll=True)


def _na_kernel(q_ref, k_ref, v_ref, bias_ref, o_ref, qm_ref, kb_ref, vb_ref):
    s_len = q_ref.shape[0]
    rows = (s_len - TM) // GRID_W
    nloc = NA_KH * GRID_W
    _stage_heads(q_ref, k_ref, v_ref, qm_ref, kb_ref, vb_ref)

    kc = kb_ref[0:TM, :]
    vc = vb_ref[0:TM, :]
    outs = [_softmax_pv([_dot_nt(qm_ref[hh, 0:TM, :], kc)], [vc]) for hh in range(2)]
    m0c = lax.broadcasted_iota(jnp.int32, (TM, LANES), 1) < HEAD_DIM
    o_ref[0:TM, :] = jnp.where(m0c, outs[0], outs[1])

    m0 = lax.broadcasted_iota(jnp.int32, (GRID_W, LANES), 1) < HEAD_DIM

    def row_block(r, carry):
        rs = jnp.clip(r - NA_KH // 2, 0, rows - NA_KH)
        pat = r - rs
        q0 = pl.multiple_of(TM + r * GRID_W, GRID_W)
        k0 = pl.multiple_of(TM + rs * GRID_W, GRID_W)
        kl = kb_ref[pl.ds(k0, nloc), :]
        vl = vb_ref[pl.ds(k0, nloc), :]
        kc = kb_ref[0:TM, :]
        vc = vb_ref[0:TM, :]
        q = jnp.concatenate([qm_ref[0, pl.ds(q0, GRID_W), :], qm_ref[1, pl.ds(q0, GRID_W), :]], axis=0)
        s_loc = _dot_nt(q, kl) + bias_ref[pat].reshape(2 * GRID_W, nloc)
        s_ctx = _dot_nt(q, kc)
        res = _softmax_pv([s_loc, s_ctx], [vl, vc])
        o_ref[pl.ds(q0, GRID_W), :] = jnp.where(m0, res[:GRID_W], res[GRID_W:])
        return carry

    lax.fori_loop(0, rows, row_block, 0, unroll=True)


def _na_bias(rpb):
    h = rpb.shape[0]
    qc = np.arange(GRID_W)[:, None]
    kc = np.arange(GRID_W)[None, :]
    win = np.clip(qc - NA_KW // 2, 0, GRID_W - NA_KW)
    valid = (kc >= win) & (kc < win + NA_KW)
    col_off = np.clip(kc - qc + NA_KW - 1, 0, 2 * NA_KW - 2)
    onehot = (col_off[None] == np.arange(2 * NA_KW - 1)[:, None, None]).astype(np.float32)
    cols = jnp.einsum("hrc,cqk->hrqk", rpb.astype(F32), onehot, precision=lax.Precision.HIGHEST)
    cols = jnp.where(valid[None, None], cols * LOG2E, NEG_INF)
    bias = jnp.stack([cols[:, NA_KH - 1 - p:2 * NA_KH - 1 - p] for p in range(NA_KH)])
    bias = bias.transpose(0, 1, 3, 2, 4)
    return bias.reshape(NA_KH, h, GRID_W, NA_KH * GRID_W)


def _neighbourhood(p, bias):
    b, s, _ = p.shape
    nb = NA_W // LANES
    c0 = 4 * RET_W // LANES
    blk = lambda off: pl.BlockSpec((None, s, LANES), lambda hp, bi: (bi, 0, c0 + off + hp))
    return pl.pallas_call(
        _na_kernel,
        grid=(nb, b),
        in_specs=[
            blk(0), blk(nb), blk(2 * nb),
            pl.BlockSpec((NA_KH, 2, GRID_W, NA_KH * GRID_W), lambda hp, bi: (0, hp, 0, 0)),
        ],
        out_specs=pl.BlockSpec((None, s, LANES), lambda hp, bi: (bi, 0, hp)),
        out_shape=jax.ShapeDtypeStruct((b, s, NA_W), F32),
        scratch_shapes=[pltpu.VMEM((2, s, LANES), BF16), pltpu.VMEM((s, LANES), BF16),
                        pltpu.VMEM((s, 2 * LANES), BF16)],
        compiler_params=_cparams(("arbitrary", "arbitrary")),
        name="neighbourhood",
    )(p, p, p, bias)


def _swa_kernel(sink_ref, q_ref, k_ref, v_ref, o_ref, kd_ref, vd_ref):
    kp = pl.program_id(1)
    s_len = q_ref.shape[0]
    seq = s_len - TM
    nblk = seq // SWA_BLOCK
    band = SWA_BLOCK + 2 * SWA_WINDOW
    group = SWA_HEADS // SWA_KV_HEADS
    heads_per_step = 2 * group
    m0t = lax.broadcasted_iota(jnp.int32, (TM, LANES), 1) < HEAD_DIM

    def stage(i, carry):
        r0 = pl.multiple_of(i * TM, TM)
        for src, dst in ((k_ref, kd_ref), (v_ref, vd_ref)):
            x = src[pl.ds(r0, TM), :].astype(F32)
            xr = pltpu.roll(x, HEAD_DIM, 1)
            dst[0, pl.ds(r0, TM), 0:LANES] = jnp.where(m0t, x, xr).astype(BF16)
            dst[1, pl.ds(r0, TM), 0:LANES] = jnp.where(m0t, xr, x).astype(BF16)
        for h in range(2):
            vd_ref[h, pl.ds(r0, TM), LANES:2 * LANES] = jnp.ones((TM, LANES), BF16)
        return carry

    lax.fori_loop(0, s_len // TM, stage, 0, unroll=True)
    o_ref[0:TM, :] = jnp.zeros((TM, o_ref.shape[1]), F32)

    rows = group * SWA_BLOCK
    qi = lax.broadcasted_iota(jnp.int32, (rows, band), 0) % SWA_BLOCK
    ki = lax.broadcasted_iota(jnp.int32, (rows, band), 1)
    m0 = lax.broadcasted_iota(jnp.int32, (SWA_BLOCK, LANES), 1) < HEAD_DIM
    head_of_row = lax.broadcasted_iota(jnp.int32, (rows, 1), 0) // SWA_BLOCK

    def block(i, carry):
        start = jnp.clip((i - 1) * SWA_BLOCK, 0, seq - band)
        k0 = pl.multiple_of(TM + start, SWA_BLOCK)
        q0 = pl.multiple_of(TM + i * SWA_BLOCK, SWA_BLOCK)
        valid = jnp.abs(qi + (i * SWA_BLOCK - start) - ki) <= SWA_WINDOW
        for hh in range(2):
            kb = kd_ref[hh, pl.ds(k0, band), :]
            vb = vd_ref[hh, pl.ds(k0, band), :]
            kc = kd_ref[hh, 0:TM, :]
            vc = vd_ref[hh, 0:TM, :]
            parts = []
            for j in range(group // 2):
                pair = hh * (group // 2) + j
                qp = q_ref[pl.ds(q0, SWA_BLOCK), pair * LANES:(pair + 1) * LANES]
                parts += [jnp.where(m0, qp, 0.0), jnp.where(m0, 0.0, qp)]
            q = jnp.concatenate(parts, axis=0).astype(BF16)
            s = jnp.where(valid, _dot_nt(q, kb), NEG_INF)
            s_ctx = _dot_nt(q, kc)
            h0 = kp * heads_per_step + hh * group
            sink = jnp.full((rows, 1), sink_ref[h0], F32)
            for g in range(1, group):
                sink = jnp.where(head_of_row == g, sink_ref[h0 + g], sink)
            res = _softmax_pv([s, s_ctx], [vb, vc], extra=sink * LOG2E)
            for j in range(group // 2):
                pair = hh * (group // 2) + j
                r0 = 2 * j * SWA_BLOCK
                o_ref[pl.ds(q0, SWA_BLOCK), pair * LANES:(pair + 1) * LANES] = jnp.where(
                    m0, res[r0:r0 + SWA_BLOCK], res[r0 + SWA_BLOCK:r0 + 2 * SWA_BLOCK])
        return carry

    lax.fori_loop(0, nblk, block, 0, unroll=True)


def _swa(p, sink):
    b, s, _ = p.shape
    qw = SWA_HEADS * HEAD_DIM // 2
    kblk = SWA_HEADS * HEAD_DIM // LANES
    vblk = kblk + SWA_KV_HEADS * HEAD_DIM // LANES
    return pl.pallas_call(
        _swa_kernel,
        grid=(b, 2),
        in_specs=[
            pl.BlockSpec(memory_space=pltpu.SMEM),
            pl.BlockSpec((None, s, qw), lambda bi, kp: (bi, 0, kp)),
            pl.BlockSpec((None, s, LANES), lambda bi, kp: (bi, 0, kblk + kp)),
            pl.BlockSpec((None, s, LANES), lambda bi, kp: (bi, 0, vblk + kp)),
        ],
        out_specs=pl.BlockSpec((None, s, qw), lambda bi, kp: (bi, 0, kp)),
        out_shape=jax.ShapeDtypeStruct((b, s, SWA_HEADS * HEAD_DIM), F32),
        scratch_shapes=[pltpu.VMEM((2, s, LANES), BF16), pltpu.VMEM((2, s, 2 * LANES), BF16)],
        compiler_params=_cparams(("arbitrary", "arbitrary")),
        name="swa",
    )(sink, p, p, p)


def _outproj_kernel(*refs, n_mix, n_stream, t0):
    mix_refs = refs[:n_mix]
    x_refs = refs[n_mix + 1:n_mix + 1 + n_stream]
    w_ref = refs[n_mix]
    (g1_ref, gn_ref, sh_ref, sc_ref, wr_ref, br_ref,
     xo_ref, h_ref, ri_ref, rt_ref, cnt_ref, carry_ref) = refs[n_mix + 1 + n_stream:]
    first = (pl.program_id(0) == 0) & (pl.program_id(1) == 0)

    @pl.when(first)
    def _():
        carry_ref[...] = jnp.zeros_like(carry_ref)

    o = None
    off = 0
    for m_ref in mix_refs:
        kw = m_ref.shape[1]
        part = jnp.dot(m_ref[...].astype(BF16), w_ref[off:off + kw, :], preferred_element_type=F32)
        o = part if o is None else o + part
        off += kw
    xn = _stream_tile(x_refs, pl.program_id(1) + t0 > 0) + g1_ref[...] * o
    xo_ref[...] = xn
    h = _rms_mod(xn, gn_ref[...], sh_ref[...], sc_ref[...])
    h_ref[...] = h

    logits = _dot3_split(h, wr_ref[0], wr_ref[1]) + br_ref[...]
    tm = logits.shape[0]
    lt = logits.T[0:ROUTE_ROWS, :]
    row = lax.broadcasted_iota(jnp.int32, (ROUTE_ROWS, tm), 0).astype(F32)
    big = 1e9
    gmask = row < MOE_GROUPS
    mg = jnp.max(jnp.where(gmask, lt, -big), axis=0, keepdims=True)
    sg = jnp.sum(jnp.where(gmask, jnp.exp(jnp.minimum(lt - mg, 0.0)), 0.0), axis=0, keepdims=True)
    gw = 1.0 / sg
    gi = jnp.min(jnp.where(gmask & (lt == mg), row, big), axis=0, keepdims=True)
    lo = ROUTE_LANE0 + MOE_EPG * gi
    emask = (row >= lo) & (row < lo + MOE_EPG)
    l1 = jnp.max(jnp.where(emask, lt, -big), axis=0, keepdims=True)
    i1 = jnp.min(jnp.where(emask & (lt == l1), row, big), axis=0, keepdims=True)
    emask2 = emask & (row != i1)
    l2 = jnp.max(jnp.where(emask2, lt, -big), axis=0, keepdims=True)
    i2 = jnp.min(jnp.where(emask2 & (lt == l2), row, big), axis=0, keepdims=True)
    e21 = jnp.exp(l2 - l1)
    w1 = gw / (1.0 + e21)
    w2 = gw * e21 / (1.0 + e21)

    oh = jnp.where((row == i1) | (row == i2), 1.0, 0.0)
    before = (lax.broadcasted_iota(jnp.int32, (tm, tm), 0) < lax.broadcasted_iota(jnp.int32, (tm, tm), 1))
    cum = (jnp.dot(oh.astype(BF16), jnp.where(before, 1.0, 0.0).astype(BF16), preferred_element_type=F32)
           + carry_ref[:, 0:1])
    r1 = jnp.sum(jnp.where(row == i1, cum, 0.0), axis=0, keepdims=True)
    r2 = jnp.sum(jnp.where(row == i2, cum, 0.0), axis=0, keepdims=True)
    carry_ref[...] = carry_ref[...] + jnp.sum(oh, axis=1, keepdims=True)
    cnt_ref[...] = carry_ref[...]

    frow = lax.broadcasted_iota(jnp.int32, (LANES, tm), 0)
    rt = jnp.where(frow == 0, i1 - ROUTE_LANE0, 0.0)
    rt = jnp.where(frow == 1, i2 - ROUTE_LANE0, rt)
    rt = jnp.where(frow == 2, w1, rt)
    rt = jnp.where(frow == 3, w2, rt)
    rt = jnp.where(frow == 4, r1, rt)
    rt = jnp.where(frow == 5, r2, rt)
    rt_ref[...] = rt[0:SUBLANES, :]
    ri_ref[...] = rt.T


def _outproj(mixes, w_bf16, xa, g1, gain2, sh2, sc2, wr, br, t0):
    b, s, d = _stream_shape(xa)
    nt = s // TM - t0
    so = nt * TM
    mod_idx = lambda bi, t: (jnp.where(t + t0 == 0, b, bi), 0, 0)
    tile = lambda wdt: pl.BlockSpec((None, TM, wdt), lambda bi, t: (bi, t + t0, 0))
    otile = lambda wdt: pl.BlockSpec((None, TM, wdt), lambda bi, t: (bi, t, 0))
    const = lambda shape: pl.BlockSpec(shape, lambda bi, t: (0,) * len(shape))
    x_specs, x_args = _stream_specs(xa, t0)
    in_specs = [tile(m.shape[2]) for m in mixes] + [const(w_bf16.shape)] + x_specs + [
        pl.BlockSpec((None, 1, d), mod_idx), const((1, d)),
        pl.BlockSpec((None, 1, d), mod_idx), pl.BlockSpec((None, 1, d), mod_idx),
        const((2, d, LANES)), const((1, LANES)),
    ]
    return pl.pallas_call(
        functools.partial(_outproj_kernel, n_mix=len(mixes), n_stream=len(x_args), t0=t0),
        grid=(b, nt),
        in_specs=in_specs,
        out_specs=[otile(d), otile(d), otile(LANES),
                   pl.BlockSpec((None, None, SUBLANES, TM), lambda bi, t: (bi, t, 0, 0)),
                   const((ROUTE_ROWS, LANES))],
        out_shape=[jax.ShapeDtypeStruct((b, so, d), F32), jax.ShapeDtypeStruct((b, so, d), F32),
                   jax.ShapeDtypeStruct((b, so, LANES), F32), jax.ShapeDtypeStruct((b, nt, SUBLANES, TM), F32),
                   jax.ShapeDtypeStruct((ROUTE_ROWS, LANES), F32)],
        scratch_shapes=[pltpu.VMEM((ROUTE_ROWS, LANES), F32)],
        compiler_params=_cparams(("arbitrary", "arbitrary")),
        name="outproj",
    )(*mixes, w_bf16, *x_args, g1, gain2, sh2, sc2, wr, br)


SUBLANES = 8


def _to_token_tiles(x):
    return x.reshape(x.shape[0], SUBLANES, x.shape[1] // SUBLANES)


def _from_token_tiles(x3):
    return x3.reshape(x3.shape[0], x3.shape[1] * x3.shape[2])


PAD_BITS = tuple(1 << k for k in reversed(range(MOE_ROWS.bit_length() - 1)))


def _dispatch_kernel(dest_ref, pads_ref, h_ref, xs_ref, hbuf, zbuf, sem, zsem):
    nt = pl.num_programs(1)
    step = pl.program_id(0) * nt + pl.program_id(1)
    last = pl.num_programs(0) * nt - 1
    slot = step % 2

    def zero_pads(wait):
        def expert(e, c):
            pos = pads_ref[0, e]
            n = pads_ref[1, e]
            for bit in PAD_BITS:
                take = (n & bit) != 0
                cp = pltpu.make_async_copy(zbuf.at[pl.ds(0, bit)], xs_ref.at[pl.ds(pos, bit)], zsem)

                @pl.when(take)
                def _():
                    cp.wait() if wait else cp.start()

                pos = pos + jnp.where(take, bit, 0)
            return c
        lax.fori_loop(0, MOE_EXPERTS, expert, 0)

        def tail(j, c):
            cp = pltpu.make_async_copy(
                zbuf, xs_ref.at[pl.ds(pads_ref[0, MOE_EXPERTS] + j * PAD_BITS[0], PAD_BITS[0])], zsem)
            cp.wait() if wait else cp.start()
            return c
        lax.fori_loop(0, pads_ref[1, MOE_EXPERTS], tail, 0)

    @pl.when(step == 0)
    def _():
        zbuf[...] = jnp.zeros_like(zbuf)
        zero_pads(wait=False)
    hbuf[slot] = _to_token_tiles(h_ref[...])

    def copy(sl, i, dst_row):
        return pltpu.make_async_copy(hbuf.at[sl, i], xs_ref.at[dst_row], sem.at[sl])

    for i in range(TM):
        copy(slot, i, dest_ref[step, i]).start()
        copy(slot, i, dest_ref[step, TM + i]).start()

    def drain(sl):
        for _ in range(2):
            pltpu.make_async_copy(hbuf.at[sl], xs_ref.at[pl.ds(0, TM)], sem.at[sl]).wait()

    @pl.when(step > 0)
    def _():
        drain(1 - slot)

    @pl.when(step == last)
    def _():
        drain(slot)
        zero_pads(wait=True)


def _dispatch(dest, pads, h2, n_pad):
    b, s, d = h2.shape
    nt = s // TM
    tile = (SUBLANES, d // SUBLANES)
    return pl.pallas_call(
        _dispatch_kernel,
        grid_spec=pltpu.PrefetchScalarGridSpec(
            num_scalar_prefetch=2,
            grid=(b, nt),
            in_specs=[pl.BlockSpec((None, TM, d), lambda bi, t, dr, pd: (bi, t, 0))],
            out_specs=pl.BlockSpec(memory_space=pl.ANY),
            scratch_shapes=[pltpu.VMEM((2, TM) + tile, F32), pltpu.VMEM((PAD_BITS[0],) + tile, F32),
                            pltpu.SemaphoreType.DMA((2,)), pltpu.SemaphoreType.DMA],
        ),
        out_shape=jax.ShapeDtypeStruct((n_pad,) + tile, F32),
        compiler_params=_cparams(("arbitrary", "arbitrary")),
        name="dispatch",
    )(dest, pads, h2)


def _mlp_kernel(be_ref, nu_ref, x_ref, wgu_ref, wd_ref, y_ref, wgu_b, wd_b):
    i = pl.program_id(0)
    prev = be_ref[jnp.maximum(i - 1, 0)]
    used = i < nu_ref[0]

    @pl.when(used & ((i == 0) | (be_ref[i] != prev)))
    def _():
        wgu_b[...] = wgu_ref[...].astype(BF16)
        wd_b[...] = wd_ref[...].astype(BF16)

    @pl.when(used)
    def _():
        x = _from_token_tiles(x_ref[...])
        gu = jnp.dot(x.astype(BF16), wgu_b[...], preferred_element_type=F32)
        act = _silu(gu[:, :MOE_FF]) * gu[:, MOE_FF:]
        y_ref[...] = _to_token_tiles(jnp.dot(act.astype(BF16), wd_b[...], preferred_element_type=F32))

    @pl.when(jnp.logical_not(used))
    def _():
        y_ref[...] = jnp.zeros_like(y_ref)


def _expert_mlp(block_e, n_used, xs, w_gu, w_down, layer):
    n_pad, sub, dl = xs.shape
    d = sub * dl
    ff2 = w_gu.shape[-1]
    slots = pl.BlockSpec((MOE_ROWS, sub, dl), lambda i, be, nu: (i, 0, 0))
    used_slots = pl.BlockSpec((MOE_ROWS, sub, dl), lambda i, be, nu: (jnp.minimum(i, nu[0] - 1), 0, 0))
    return pl.pallas_call(
        _mlp_kernel,
        grid_spec=pltpu.PrefetchScalarGridSpec(
            num_scalar_prefetch=2,
            grid=(n_pad // MOE_ROWS,),
            in_specs=[
                used_slots,
                pl.BlockSpec((None, None, d, ff2), lambda i, be, nu: (layer, be[i], 0, 0)),
                pl.BlockSpec((None, None, ff2 // 2, d), lambda i, be, nu: (layer, be[i], 0, 0)),
            ],
            out_specs=slots,
            scratch_shapes=[pltpu.VMEM((d, ff2), BF16), pltpu.VMEM((ff2 // 2, d), BF16)],
        ),
        out_shape=jax.ShapeDtypeStruct(xs.shape, F32),
        compiler_params=_cparams(("arbitrary",)),
        name="expert_mlp",
    )(block_e, n_used, xs, w_gu, w_down)


def _combine_kernel(dest_ref, ys_ref, x_ref, ri_ref, g2_ref, fg_ref, o_ref, buf, sem, *, final):
    bi = pl.program_id(0)
    t = pl.program_id(1)
    nt = pl.num_programs(1)
    step = bi * nt + t
    total = pl.num_programs(0) * nt

    def copy(src_row, slot, k, i):
        return pltpu.make_async_copy(ys_ref.at[src_row], buf.at[slot, k * TM + i], sem.at[slot])

    def issue(st, slot):
        for i in range(TM):
            copy(dest_ref[st, i], slot, 0, i).start()
            copy(dest_ref[st, TM + i], slot, 1, i).start()

    slot = step % 2

    @pl.when(step == 0)
    def _():
        issue(0, 0)

    @pl.when(step + 1 < total)
    def _():
        issue(step + 1, 1 - slot)

    pltpu.make_async_copy(ys_ref.at[pl.ds(0, 2 * TM)], buf.at[slot], sem.at[slot]).wait()

    lane = lax.broadcasted_iota(jnp.int32, (TM, LANES), 1)
    ri = ri_ref[...]
    w1 = jnp.sum(jnp.where(lane == 2, ri, 0.0), axis=-1, keepdims=True)
    w2 = jnp.sum(jnp.where(lane == 3, ri, 0.0), axis=-1, keepdims=True)
    y = (_from_token_tiles(buf[slot, 0:TM]) * w1 + _from_token_tiles(buf[slot, TM:2 * TM]) * w2)
    xn = x_ref[...] + g2_ref[...] * y
    if final:
        ms = jnp.mean(xn * xn, axis=-1, keepdims=True)
        xn = xn * lax.rsqrt(ms + NORM_EPS) * fg_ref[...]
    o_ref[...] = xn


def _combine(dest, ys, xa, rinfo, g2, final_g, has_ctx, final):
    b, s, d = xa.shape
    nt = s // TM
    mod_idx = lambda bi, t, dr: (jnp.where(t == 0, b, bi) if has_ctx else bi, 0, 0)
    tile = lambda wdt: pl.BlockSpec((None, TM, wdt), lambda bi, t, dr: (bi, t, 0))
    out_spec = tile(d)
    out_shape = jax.ShapeDtypeStruct((b, s, d), F32)
    return pl.pallas_call(
        functools.partial(_combine_kernel, final=final),
        grid_spec=pltpu.PrefetchScalarGridSpec(
            num_scalar_prefetch=1,
            grid=(b, nt),
            in_specs=[
                pl.BlockSpec(memory_space=pl.ANY),
                tile(d), tile(LANES),
                pl.BlockSpec((None, 1, d), mod_idx),
                pl.BlockSpec((1, d), lambda bi, t, dr: (0, 0)),
            ],
            out_specs=out_spec,
            scratch_shapes=[pltpu.VMEM((2, 2 * TM, SUBLANES, d // SUBLANES), F32),
                            pltpu.SemaphoreType.DMA((2,))],
        ),
        out_shape=out_shape,
        compiler_params=_cparams(("arbitrary", "arbitrary")),
        name="combine",
    )(dest, ys, xa, rinfo, g2, final_g)


def _moe(h2, rt, counts, w_gu, w_down, layer):
    b, s, d = h2.shape
    nt = s // TM
    r = rt.reshape(b * nt, SUBLANES, TM)
    fields = lambda i: jnp.concatenate([r[:, i], r[:, i + 1]], axis=1).astype(jnp.int32)
    e = fields(0)
    rank = fields(4)
    cnt = counts[ROUTE_LANE0:ROUTE_LANE0 + MOE_EXPERTS, 0].astype(jnp.int32)
    padded = (cnt + MOE_ROWS - 1) // MOE_ROWS * MOE_ROWS
    ends = jnp.cumsum(padded)
    starts = ends - padded
    eids = jnp.arange(MOE_EXPERTS, dtype=jnp.int32)
    dest = jnp.sum(jnp.where(e[..., None] == eids, starts, 0), axis=-1) + rank
    n_assign = b * nt * TM * 2
    n_blocks = (n_assign + MOE_EXPERTS * (MOE_ROWS - 1)) // MOE_ROWS + 1
    n_pad = n_blocks * MOE_ROWS
    blk_row = jnp.arange(n_blocks, dtype=jnp.int32) * MOE_ROWS
    block_e = jnp.minimum(jnp.sum((blk_row[:, None] >= ends[None, :]).astype(jnp.int32), axis=1),
                          MOE_EXPERTS - 1)
    n_used = (ends[-1] // MOE_ROWS).astype(jnp.int32).reshape(1)
    pads = jnp.stack([jnp.append(starts + cnt, ends[-1]),
                      jnp.append(padded - cnt, (n_pad - ends[-1]) // PAD_BITS[0])]).astype(jnp.int32)
    xs = _dispatch(dest, pads, h2, n_pad)
    ys = _expert_mlp(block_e, n_used, xs, w_gu, w_down, layer)
    return dest, ys


def _router_weights(wg, bg, we, be):
    d = wg.shape[0]
    pad = LANES - MOE_GROUPS - MOE_EXPERTS
    assert ROUTE_LANE0 == MOE_GROUPS
    wr = jnp.concatenate([wg.astype(F32), we.astype(F32), jnp.zeros((d, pad), F32)], axis=1)
    br = jnp.concatenate([bg.astype(F32), be.astype(F32), jnp.zeros((pad,), F32)]).reshape(1, LANES)
    return jnp.stack(_split_bf16(wr)), br


def kernel(x, c, ctx, c_ctx, ada_w, ada_b, norm_g, final_g, ab_w_in, ab_w_out, ret_decay, ret_gn, na_rpb,
           swa_w_in, swa_w_out, swa_sink, router_g_w, router_g_b, router_e_w, router_e_b,
           expert_w_gu, expert_w_down):
    b, seq, d = x.shape
    assert ctx.shape[1] == TM and seq % TM == 0 and d == D_MODEL
    xa = (ctx, x)
    rope = _rope_tables(seq)

    cvec = jnp.concatenate([c, c_ctx[None, :], jnp.zeros((7, d), F32)], axis=0)
    mod = _adaln(cvec, ada_w, ada_b)
    mod = mod.reshape(DEPTH, b + 8, 6, 1, d)[:, :b + 1].transpose(0, 2, 1, 3, 4)

    for layer in range(DEPTH):
        last = layer == DEPTH - 1
        sh1, sc1, g1, sh2, sc2, g2 = (mod[layer, i] for i in range(6))
        gain1 = norm_g[layer, 0].reshape(1, d)
        gain2 = norm_g[layer, 1].reshape(1, d)
        j = layer // 2
        if layer % 2 == 0:
            nb = RET_W // LANES
            qk_scale = HEAD_DIM ** -0.5
            scales = {blk: qk_scale for blk in range(nb, 2 * nb)}
            scales.update({blk: qk_scale * LOG2E for blk in range(4 * nb, 5 * nb)})
            p = _proj(xa, gain1, sh1, sc1, ab_w_in[j].astype(BF16), rope,
                      rope_blocks=range(0, 2 * nb), scales=scales)
            log_gamma = jnp.log1p(-jnp.exp2(-ret_decay[j].astype(F32)))
            mixes = [_retention(p, log_gamma, ret_gn[j]), _neighbourhood(p, _na_bias(na_rpb[j]))]
            w_out = ab_w_out[j]
        else:
            qb = SWA_HEADS * HEAD_DIM // LANES
            kb = SWA_KV_HEADS * HEAD_DIM // LANES
            p = _proj(xa, gain1, sh1, sc1, swa_w_in[j].astype(BF16), rope,
                      rope_blocks=range(0, qb + kb),
                      scales={blk: HEAD_DIM ** -0.5 * LOG2E for blk in range(qb)})
            mixes = [_swa(p, swa_sink[j].astype(F32))]
            w_out = swa_w_out[j]
        t0 = 1 if last else 0
        wr, br = _router_weights(router_g_w[layer], router_g_b[layer], router_e_w[layer], router_e_b[layer])
        xa, h2, rinfo, rt, counts = _outproj(mixes, w_out.astype(BF16), xa, g1, gain2, sh2, sc2, wr, br, t0)
        dest, ys = _moe(h2, rt, counts, expert_w_gu, expert_w_down, layer)
        xa = _combine(dest, ys, xa, rinfo, g2, final_g.reshape(1, d), not last, last)
    return xa
```

```python
import functools

import numpy as np
import jax
import jax.numpy as jnp
from jax import lax
from jax.experimental import pallas as pl
from jax.experimental.pallas import tpu as pltpu

F32 = jnp.float32
BF16 = jnp.bfloat16

D_MODEL = 1024
DEPTH = 2
GRID_W = 64
HEAD_DIM = 64
RET_HEADS = 8
NA_HEADS = 8
RET_W = 512
NA_W = 512
AB_IN = 4 * RET_W + 3 * NA_W
GN_EPS = 1e-5
NA_KH = 8
NA_KW = 16
SWA_HEADS = 16
SWA_KV_HEADS = 4
SWA_WINDOW = 128
SWA_BLOCK = 128
SWA_IN = (SWA_HEADS + 2 * SWA_KV_HEADS) * HEAD_DIM
ROPE_BASE = 10000.0
MOE_GROUPS = 4
MOE_EPG = 8
MOE_EXPERTS = 32
MOE_FF = 512
NORM_EPS = 1e-6
NEG_INF = -1e30

LANES = 128
TM = 256
RET_BLOCK = 256
MOE_ROWS = 1024
ROUTE_LANE0 = 4
ROUTE_ROWS = 48
VMEM_LIMIT = 56 * 1024 * 1024


def _cparams(sem):
    return pltpu.CompilerParams(dimension_semantics=sem, vmem_limit_bytes=VMEM_LIMIT)


def _split_bf16(a):
    hi = a.astype(BF16)
    lo = (a - hi.astype(F32)).astype(BF16)
    return hi, lo


def _dot3_split(a, bh, bl):
    ah, al = _split_bf16(a)
    d = lambda x, y: jnp.dot(x, y, preferred_element_type=F32)
    return d(ah, bh) + (d(ah, bl) + d(al, bh))


def _dot3(a, b):
    return _dot3_split(a, *_split_bf16(b))


def _dot_nt(a, b):
    return lax.dot_general(a, b, (((1,), (1,)), ((), ())), preferred_element_type=F32)


def _dot_tn(a, b):
    return lax.dot_general(a, b, (((0,), (0,)), ((), ())), preferred_element_type=F32)


def _silu(x):
    return x / (1.0 + jnp.exp(-x))


def _adaln_kernel(c_ref, w_ref, b_ref, o_ref):
    o_ref[...] = _dot3(_silu(c_ref[...]), w_ref[...]) + b_ref[...]


def _adaln(cvec, ada_w, ada_b):
    depth, d, n6 = ada_w.shape
    rows = cvec.shape[0]
    tn = 1024
    return pl.pallas_call(
        _adaln_kernel,
        grid=(depth, n6 // tn),
        in_specs=[
            pl.BlockSpec((rows, d), lambda l, j: (0, 0)),
            pl.BlockSpec((None, d, tn), lambda l, j: (l, 0, j)),
            pl.BlockSpec((None, 1, tn), lambda l, j: (l, 0, j)),
        ],
        out_specs=pl.BlockSpec((None, rows, tn), lambda l, j: (l, 0, j)),
        out_shape=jax.ShapeDtypeStruct((depth, rows, n6), F32),
        compiler_params=_cparams(("arbitrary", "arbitrary")),
        name="adaln",
    )(cvec, ada_w, ada_b.reshape(depth, 1, n6))


def _rms_mod(x, g, sh, sc):
    ms = jnp.mean(x * x, axis=-1, keepdims=True)
    return (x * lax.rsqrt(ms + NORM_EPS) * g) * (1.0 + sc) + sh


def _stream_tile(refs, is_lat):
    if len(refs) == 1:
        return refs[0][...]
    return jnp.where(is_lat, refs[1][...], refs[0][...])


def _stream_specs(stream, t0):
    def im(f):
        return (lambda bi, t, *_: f(bi, t + t0))
    if not isinstance(stream, tuple):
        return [pl.BlockSpec((None, TM, stream.shape[2]), im(lambda bi, t: (bi, t, 0)))], [stream]
    ctx, x = stream
    d = x.shape[2]
    return ([pl.BlockSpec((None, TM, d), im(lambda bi, t: (bi, 0, 0))),
             pl.BlockSpec((None, TM, d), im(lambda bi, t: (bi, jnp.maximum(t - 1, 0), 0)))], [ctx, x])


def _stream_shape(stream):
    if not isinstance(stream, tuple):
        return stream.shape
    ctx, x = stream
    return (x.shape[0], ctx.shape[1] + x.shape[1], x.shape[2])


def _proj_kernel(*refs, rope_blocks, scales, cn, n_stream):
    x_refs = refs[:n_stream]
    g_ref, sh_ref, sc_ref, w_ref, rope_ref, o_ref = refs[n_stream:]
    is_lat = pl.program_id(1) > 0
    hb = _rms_mod(_stream_tile(x_refs, is_lat), g_ref[...], sh_ref[...], sc_ref[...]).astype(BF16)
    nout = w_ref.shape[1]
    for c in range(nout // cn):
        o = jnp.dot(hb, w_ref[:, c * cn:(c + 1) * cn], preferred_element_type=F32)
        for s in range(cn // LANES):
            blk = c * (cn // LANES) + s
            ob = o[:, s * LANES:(s + 1) * LANES]
            if blk in rope_blocks:
                r = (ob * rope_ref[0] + pltpu.roll(ob, 16, 1) * rope_ref[1]
                     + pltpu.roll(ob, LANES - 16, 1) * rope_ref[2])
                ob = jnp.where(is_lat, r, ob)
            if blk in scales:
                ob = ob * scales[blk]
            o_ref[:, blk * LANES:(blk + 1) * LANES] = ob.astype(o_ref.dtype)


def _proj(xa, gain, sh, sc, w_bf16, rope, rope_blocks, scales):
    b, s, d = _stream_shape(xa)
    nout = w_bf16.shape[1]
    nt = s // TM
    mod_idx = lambda bi, t: (jnp.where(t == 0, b, bi), 0, 0)
    x_specs, x_args = _stream_specs(xa, 0)
    kern = functools.partial(_proj_kernel, rope_blocks=frozenset(rope_blocks), scales=dict(scales), cn=512,
                             n_stream=len(x_args))
    return pl.pallas_call(
        kern,
        grid=(b, nt),
        in_specs=x_specs + [
            pl.BlockSpec((1, d), lambda bi, t: (0, 0)),
            pl.BlockSpec((None, 1, d), mod_idx),
            pl.BlockSpec((None, 1, d), mod_idx),
            pl.BlockSpec((d, nout), lambda bi, t: (0, 0)),
            pl.BlockSpec((3, TM, LANES), lambda bi, t: (0, jnp.maximum(t - 1, 0), 0)),
        ],
        out_specs=pl.BlockSpec((None, TM, nout), lambda bi, t: (bi, t, 0)),
        out_shape=jax.ShapeDtypeStruct((b, s, nout), BF16),
        compiler_params=_cparams(("arbitrary", "arbitrary")),
        name="proj",
    )(*x_args, gain, sh, sc, w_bf16, rope)


def _rope_tables(seq):
    nf = HEAD_DIM // 4
    inv = ROPE_BASE ** (-jnp.arange(nf, dtype=F32) / nf)
    t = jnp.arange(seq)
    row = (t // GRID_W).astype(F32)
    col = (t % GRID_W).astype(F32)
    lane = np.arange(LANES)
    jj = lane % HEAD_DIM
    axis_is_col = (jj // 32) == 1
    second_half = (jj % 32) >= 16
    f = jj % 16
    pos = jnp.where(axis_is_col[None, :], col[:, None], row[:, None])
    ang = pos * inv[f][None, :]
    c, s = jnp.cos(ang), jnp.sin(ang)
    sa = jnp.where(second_half[None, :], s, 0.0)
    sb = jnp.where(second_half[None, :], 0.0, -s)
    return jnp.stack([c, sa, sb], axis=0)


def _ret_kernel(lg_ref, q_ref, k_ref, v_ref, g_ref, gn_ref, o_ref, accf_ref, accb_ref,
                intra_ref, qd_ref, kd_ref):
    hp = pl.program_id(0)
    c = RET_BLOCK
    s_len = q_ref.shape[0]
    n_chunks = s_len // c
    ctx_chunks = TM // c
    pos = lax.broadcasted_iota(jnp.int32, (c, LANES), 0).astype(F32)
    m0 = lax.broadcasted_iota(jnp.int32, (c, LANES), 1) < HEAD_DIM
    same_head = ((lax.broadcasted_iota(jnp.int32, (LANES, LANES), 0) < HEAD_DIM)
                 == (lax.broadcasted_iota(jnp.int32, (LANES, LANES), 1) < HEAD_DIM))
    rel = (lax.broadcasted_iota(jnp.int32, (c, c), 0) - lax.broadcasted_iota(jnp.int32, (c, c), 1)).astype(F32)
    lgf = [lg_ref[0, hp * 2 + hh] for hh in range(2)]
    lgb = [lg_ref[1, hp * 2 + hh] for hh in range(2)]
    lgf_l = jnp.where(m0, lgf[0], lgf[1])
    lgb_l = jnp.where(m0, lgb[0], lgb[1])
    @pl.when(pl.program_id(1) == 0)
    def _():
        for hh in range(2):
            intra_ref[0, :, hh * c:(hh + 1) * c] = jnp.where(
                rel >= 0, jnp.exp(lgf[hh] * jnp.maximum(rel, 0.0)), 0.0)
            intra_ref[1, :, hh * c:(hh + 1) * c] = jnp.where(
                rel <= 0, jnp.exp(lgb[hh] * jnp.maximum(-rel, 0.0)), 0.0)
        qd_ref[0] = jnp.exp(lgf_l * (pos + 1.0))
        qd_ref[1] = jnp.exp(lgb_l * (c - pos))
        kd_ref[0] = jnp.exp(lgf_l * (c - 1.0 - pos))
        kd_ref[1] = jnp.exp(lgb_l * pos)
    cd = [jnp.exp(lgf_l[0:1] * float(c)), jnp.exp(lgb_l[0:1] * float(c))]

    def chunk(r0, state, d):
        q = q_ref[pl.ds(r0, c), :]
        k = k_ref[pl.ds(r0, c), :]
        v = v_ref[pl.ds(r0, c), :]
        qb = q.astype(BF16)
        kcat = jnp.concatenate([jnp.where(m0, k, 0.0), jnp.where(m0, 0.0, k)], axis=0).astype(BF16)
        vcat = jnp.concatenate([jnp.where(m0, v, 0.0), jnp.where(m0, 0.0, v)], axis=0).astype(BF16)
        s = _dot_nt(qb, kcat) * intra_ref[d]
        out = jnp.dot(s.astype(BF16), vcat, preferred_element_type=F32)
        out = out + jnp.dot(qb, state.astype(BF16), preferred_element_type=F32) * qd_ref[d]
        kv = _dot_tn((k * kd_ref[d]).astype(BF16), v.astype(BF16))
        return out, state * cd[d] + jnp.where(same_head, kv, 0.0)

    def body(i, states):
        sf, sb = states
        rf = pl.multiple_of(i * c, c)
        ib = jnp.where(i < ctx_chunks, ctx_chunks - 1 - i, n_chunks + ctx_chunks - 1 - i)
        rb = pl.multiple_of(ib * c, c)
        of, sf = chunk(rf, sf, 0)
        ob, sb = chunk(rb, sb, 1)
        accf_ref[pl.ds(rf, c), :] = of
        accb_ref[pl.ds(rb, c), :] = ob
        return sf, sb

    z = jnp.zeros((LANES, LANES), F32)
    lax.fori_loop(0, n_chunks, body, (z, z), unroll=True)

    avg = jnp.where(same_head, 1.0 / HEAD_DIM, 0.0).astype(BF16)

    def head_mean(x):
        hi, lo = _split_bf16(x)
        return (jnp.dot(hi, avg, preferred_element_type=F32) + jnp.dot(lo, avg, preferred_element_type=F32))

    def readout(i, carry):
        r0 = pl.multiple_of(i * c, c)
        o = accf_ref[pl.ds(r0, c), :] + accb_ref[pl.ds(r0, c), :]
        dlt = o - head_mean(o)
        var = head_mean(dlt * dlt)
        y = dlt * lax.rsqrt(var + GN_EPS) * gn_ref[...]
        o_ref[pl.ds(r0, c), :] = _silu(g_ref[pl.ds(r0, c), :].astype(F32)) * y
        return carry

    lax.fori_loop(0, n_chunks, readout, 0, unroll=True)


def _retention(p, log_gamma, ret_gn):
    b, s, _ = p.shape
    nb = RET_W // LANES
    blk = lambda off: pl.BlockSpec((None, s, LANES), lambda hp, bi: (bi, 0, off + hp))
    return pl.pallas_call(
        _ret_kernel,
        grid=(nb, b),
        in_specs=[
            pl.BlockSpec(memory_space=pltpu.SMEM),
            blk(0), blk(nb), blk(2 * nb), blk(3 * nb),
            pl.BlockSpec((1, LANES), lambda hp, bi: (0, hp)),
        ],
        out_specs=pl.BlockSpec((None, s, LANES), lambda hp, bi: (bi, 0, hp)),
        out_shape=jax.ShapeDtypeStruct((b, s, RET_W), F32),
        scratch_shapes=[pltpu.VMEM((s, LANES), F32), pltpu.VMEM((s, LANES), F32),
                        pltpu.VMEM((2, RET_BLOCK, 2 * RET_BLOCK), F32),
                        pltpu.VMEM((2, RET_BLOCK, LANES), F32), pltpu.VMEM((2, RET_BLOCK, LANES), F32)],
        compiler_params=_cparams(("arbitrary", "arbitrary")),
        name="retention",
    )(log_gamma, p, p, p, p, ret_gn.reshape(1, RET_W))


LOG2E = 1.4426950408889634


def _softmax_pv(s_list, v_list, extra=None):
    m = None
    for s in s_list:
        for j in range(s.shape[1] // LANES):
            blk = s[:, j * LANES:(j + 1) * LANES]
            m = blk if m is None else jnp.maximum(m, blk)
    m = m.max(axis=-1, keepdims=True)
    if extra is not None:
        m = jnp.maximum(m, extra)
    acc = None
    for s, v in zip(s_list, v_list):
        pv = jnp.dot(jnp.exp2(s - m).astype(BF16), v, preferred_element_type=F32)
        acc = pv if acc is None else acc + pv
    o, den = acc[:, :LANES], acc[:, LANES:]
    if extra is not None:
        den = den + jnp.exp2(extra - m)
    return o / den


def _stage_heads(q_ref, k_ref, v_ref, qm_ref, kb_ref, vb_ref):
    m0 = lax.broadcasted_iota(jnp.int32, (TM, LANES), 1) < HEAD_DIM

    def stage(i, carry):
        r0 = pl.multiple_of(i * TM, TM)
        q = q_ref[pl.ds(r0, TM), :]
        qm_ref[0, pl.ds(r0, TM), :] = jnp.where(m0, q, 0.0).astype(BF16)
        qm_ref[1, pl.ds(r0, TM), :] = jnp.where(m0, 0.0, q).astype(BF16)
        kb_ref[pl.ds(r0, TM), :] = k_ref[pl.ds(r0, TM), :].astype(BF16)
        vb_ref[pl.ds(r0, TM), 0:LANES] = v_ref[pl.ds(r0, TM), :].astype(BF16)
        vb_ref[pl.ds(r0, TM), LANES:2 * LANES] = jnp.ones((TM, LANES), BF16)
        return carry

    lax.fori_loop(0, q_ref.shape[0] // TM, stage, 0, unroll=True)


def _na_kernel(q_ref, k_ref, v_ref, bias_ref, o_ref, qm_ref, kb_ref, vb_ref):
    s_len = q_ref.shape[0]
    rows = (s_len - TM) // GRID_W
    nloc = NA_KH * GRID_W
    _stage_heads(q_ref, k_ref, v_ref, qm_ref, kb_ref, vb_ref)

    kc = kb_ref[0:TM, :]
    vc = vb_ref[0:TM, :]
    outs = [_softmax_pv([_dot_nt(qm_ref[hh, 0:TM, :], kc)], [vc]) for hh in range(2)]
    m0c = lax.broadcasted_iota(jnp.int32, (TM, LANES), 1) < HEAD_DIM
    o_ref[0:TM, :] = jnp.where(m0c, outs[0], outs[1])

    m0 = lax.broadcasted_iota(jnp.int32, (GRID_W, LANES), 1) < HEAD_DIM

    def row_block(r, carry):
        rs = jnp.clip(r - NA_KH // 2, 0, rows - NA_KH)
        pat = r - rs
        q0 = pl.multiple_of(TM + r * GRID_W, GRID_W)
        k0 = pl.multiple_of(TM + rs * GRID_W, GRID_W)
        kl = kb_ref[pl.ds(k0, nloc), :]
        vl = vb_ref[pl.ds(k0, nloc), :]
        kc = kb_ref[0:TM, :]
        vc = vb_ref[0:TM, :]
        q = jnp.concatenate([qm_ref[0, pl.ds(q0, GRID_W), :], qm_ref[1, pl.ds(q0, GRID_W), :]], axis=0)
        s_loc = _dot_nt(q, kl) + bias_ref[pat].reshape(2 * GRID_W, nloc)
        s_ctx = _dot_nt(q, kc)
        res = _softmax_pv([s_loc, s_ctx], [vl, vc])
        o_ref[pl.ds(q0, GRID_W), :] = jnp.where(m0, res[:GRID_W], res[GRID_W:])
        return carry

    lax.fori_loop(0, rows, row_block, 0, unroll=True)


def _na_bias(rpb):
    h = rpb.shape[0]
    qc = np.arange(GRID_W)[:, None]
    kc = np.arange(GRID_W)[None, :]
    win = np.clip(qc - NA_KW // 2, 0, GRID_W - NA_KW)
    valid = (kc >= win) & (kc < win + NA_KW)
    col_off = np.clip(kc - qc + NA_KW - 1, 0, 2 * NA_KW - 2)
    onehot = (col_off[None] == np.arange(2 * NA_KW - 1)[:, None, None]).astype(np.float32)
    cols = jnp.einsum("hrc,cqk->hrqk", rpb.astype(F32), onehot, precision=lax.Precision.HIGHEST)
    cols = jnp.where(valid[None, None], cols * LOG2E, NEG_INF)
    bias = jnp.stack([cols[:, NA_KH - 1 - p:2 * NA_KH - 1 - p] for p in range(NA_KH)])
    bias = bias.transpose(0, 1, 3, 2, 4)
    return bias.reshape(NA_KH, h, GRID_W, NA_KH * GRID_W)


def _neighbourhood(p, bias):
    b, s, _ = p.shape
    nb = NA_W // LANES
    c0 = 4 * RET_W // LANES
    blk = lambda off: pl.BlockSpec((None, s, LANES), lambda hp, bi: (bi, 0, c0 + off + hp))
    return pl.pallas_call(
        _na_kernel,
        grid=(nb, b),
        in_specs=[
            blk(0), blk(nb), blk(2 * nb),
            pl.BlockSpec((NA_KH, 2, GRID_W, NA_KH * GRID_W), lambda hp, bi: (0, hp, 0, 0)),
        ],
        out_specs=pl.BlockSpec((None, s, LANES), lambda hp, bi: (bi, 0, hp)),
        out_shape=jax.ShapeDtypeStruct((b, s, NA_W), F32),
        scratch_shapes=[pltpu.VMEM((2, s, LANES), BF16), pltpu.VMEM((s, LANES), BF16),
                        pltpu.VMEM((s, 2 * LANES), BF16)],
        compiler_params=_cparams(("arbitrary", "arbitrary")),
        name="neighbourhood",
    )(p, p, p, bias)


def _swa_kernel(sink_ref, q_ref, k_ref, v_ref, o_ref, kd_ref, vd_ref):
    kp = pl.program_id(1)
    s_len = q_ref.shape[0]
    seq = s_len - TM
    nblk = seq // SWA_BLOCK
    band = SWA_BLOCK + 2 * SWA_WINDOW
    group = SWA_HEADS // SWA_KV_HEADS
    heads_per_step = 2 * group
    m0t = lax.broadcasted_iota(jnp.int32, (TM, LANES), 1) < HEAD_DIM

    def stage(i, carry):
        r0 = pl.multiple_of(i * TM, TM)
        for src, dst in ((k_ref, kd_ref), (v_ref, vd_ref)):
            x = src[pl.ds(r0, TM), :].astype(F32)
            xr = pltpu.roll(x, HEAD_DIM, 1)
            dst[0, pl.ds(r0, TM), 0:LANES] = jnp.where(m0t, x, xr).astype(BF16)
            dst[1, pl.ds(r0, TM), 0:LANES] = jnp.where(m0t, xr, x).astype(BF16)
        for h in range(2):
            vd_ref[h, pl.ds(r0, TM), LANES:2 * LANES] = jnp.ones((TM, LANES), BF16)
        return carry

    lax.fori_loop(0, s_len // TM, stage, 0, unroll=True)
    o_ref[0:TM, :] = jnp.zeros((TM, o_ref.shape[1]), F32)

    rows = group * SWA_BLOCK
    qi = lax.broadcasted_iota(jnp.int32, (rows, band), 0) % SWA_BLOCK
    ki = lax.broadcasted_iota(jnp.int32, (rows, band), 1)
    m0 = lax.broadcasted_iota(jnp.int32, (SWA_BLOCK, LANES), 1) < HEAD_DIM
    head_of_row = lax.broadcasted_iota(jnp.int32, (rows, 1), 0) // SWA_BLOCK

    def block(i, carry):
        start = jnp.clip((i - 1) * SWA_BLOCK, 0, seq - band)
        k0 = pl.multiple_of(TM + start, SWA_BLOCK)
        q0 = pl.multiple_of(TM + i * SWA_BLOCK, SWA_BLOCK)
        valid = jnp.abs(qi + (i * SWA_BLOCK - start) - ki) <= SWA_WINDOW
        for hh in range(2):
            kb = kd_ref[hh, pl.ds(k0, band), :]
            vb = vd_ref[hh, pl.ds(k0, band), :]
            kc = kd_ref[hh, 0:TM, :]
            vc = vd_ref[hh, 0:TM, :]
            parts = []
            for j in range(group // 2):
                pair = hh * (group // 2) + j
                qp = q_ref[pl.ds(q0, SWA_BLOCK), pair * LANES:(pair + 1) * LANES]
                parts += [jnp.where(m0, qp, 0.0), jnp.where(m0, 0.0, qp)]
            q = jnp.concatenate(parts, axis=0).astype(BF16)
            s = jnp.where(valid, _dot_nt(q, kb), NEG_INF)
            s_ctx = _dot_nt(q, kc)
            h0 = kp * heads_per_step + hh * group
            sink = jnp.full((rows, 1), sink_ref[h0], F32)
            for g in range(1, group):
                sink = jnp.where(head_of_row == g, sink_ref[h0 + g], sink)
            res = _softmax_pv([s, s_ctx], [vb, vc], extra=sink * LOG2E)
            for j in range(group // 2):
                pair = hh * (group // 2) + j
                r0 = 2 * j * SWA_BLOCK
                o_ref[pl.ds(q0, SWA_BLOCK), pair * LANES:(pair + 1) * LANES] = jnp.where(
                    m0, res[r0:r0 + SWA_BLOCK], res[r0 + SWA_BLOCK:r0 + 2 * SWA_BLOCK])
        return carry

    lax.fori_loop(0, nblk, block, 0, unroll=True)


def _swa(p, sink):
    b, s, _ = p.shape
    qw = SWA_HEADS * HEAD_DIM // 2
    kblk = SWA_HEADS * HEAD_DIM // LANES
    vblk = kblk + SWA_KV_HEADS * HEAD_DIM // LANES
    return pl.pallas_call(
        _swa_kernel,
        grid=(b, 2),
        in_specs=[
            pl.BlockSpec(memory_space=pltpu.SMEM),
            pl.BlockSpec((None, s, qw), lambda bi, kp: (bi, 0, kp)),
            pl.BlockSpec((None, s, LANES), lambda bi, kp: (bi, 0, kblk + kp)),
            pl.BlockSpec((None, s, LANES), lambda bi, kp: (bi, 0, vblk + kp)),
        ],
        out_specs=pl.BlockSpec((None, s, qw), lambda bi, kp: (bi, 0, kp)),
        out_shape=jax.ShapeDtypeStruct((b, s, SWA_HEADS * HEAD_DIM), F32),
        scratch_shapes=[pltpu.VMEM((2, s, LANES), BF16), pltpu.VMEM((2, s, 2 * LANES), BF16)],
        compiler_params=_cparams(("arbitrary", "arbitrary")),
        name="swa",
    )(sink, p, p, p)


def _outproj_kernel(*refs, n_mix, n_stream, t0):
    mix_refs = refs[:n_mix]
    x_refs = refs[n_mix + 1:n_mix + 1 + n_stream]
    w_ref = refs[n_mix]
    (g1_ref, gn_ref, sh_ref, sc_ref, wr_ref, br_ref,
     xo_ref, h_ref, ri_ref, rt_ref, cnt_ref, carry_ref) = refs[n_mix + 1 + n_stream:]
    first = (pl.program_id(0) == 0) & (pl.program_id(1) == 0)

    @pl.when(first)
    def _():
        carry_ref[...] = jnp.zeros_like(carry_ref)

    o = None
    off = 0
    for m_ref in mix_refs:
        kw = m_ref.shape[1]
        part = jnp.dot(m_ref[...].astype(BF16), w_ref[off:off + kw, :], preferred_element_type=F32)
        o = part if o is None else o + part
        off += kw
    xn = _stream_tile(x_refs, pl.program_id(1) + t0 > 0) + g1_ref[...] * o
    xo_ref[...] = xn
    h = _rms_mod(xn, gn_ref[...], sh_ref[...], sc_ref[...])
    h_ref[...] = h

    logits = _dot3_split(h, wr_ref[0], wr_ref[1]) + br_ref[...]
    tm = logits.shape[0]
    lt = logits.T[0:ROUTE_ROWS, :]
    row = lax.broadcasted_iota(jnp.int32, (ROUTE_ROWS, tm), 0).astype(F32)
    big = 1e9
    gmask = row < MOE_GROUPS
    mg = jnp.max(jnp.where(gmask, lt, -big), axis=0, keepdims=True)
    sg = jnp.sum(jnp.where(gmask, jnp.exp(jnp.minimum(lt - mg, 0.0)), 0.0), axis=0, keepdims=True)
    gw = 1.0 / sg
    gi = jnp.min(jnp.where(gmask & (lt == mg), row, big), axis=0, keepdims=True)
    lo = ROUTE_LANE0 + MOE_EPG * gi
    emask = (row >= lo) & (row < lo + MOE_EPG)
    l1 = jnp.max(jnp.where(emask, lt, -big), axis=0, keepdims=True)
    i1 = jnp.min(jnp.where(emask & (lt == l1), row, big), axis=0, keepdims=True)
    emask2 = emask & (row != i1)
    l2 = jnp.max(jnp.where(emask2, lt, -big), axis=0, keepdims=True)
    i2 = jnp.min(jnp.where(emask2 & (lt == l2), row, big), axis=0, keepdims=True)
    e21 = jnp.exp(l2 - l1)
    w1 = gw / (1.0 + e21)
    w2 = gw * e21 / (1.0 + e21)

    oh = jnp.where((row == i1) | (row == i2), 1.0, 0.0)
    before = (lax.broadcasted_iota(jnp.int32, (tm, tm), 0) < lax.broadcasted_iota(jnp.int32, (tm, tm), 1))
    cum = (jnp.dot(oh.astype(BF16), jnp.where(before, 1.0, 0.0).astype(BF16), preferred_element_type=F32)
           + carry_ref[:, 0:1])
    r1 = jnp.sum(jnp.where(row == i1, cum, 0.0), axis=0, keepdims=True)
    r2 = jnp.sum(jnp.where(row == i2, cum, 0.0), axis=0, keepdims=True)
    carry_ref[...] = carry_ref[...] + jnp.sum(oh, axis=1, keepdims=True)
    cnt_ref[...] = carry_ref[...]

    frow = lax.broadcasted_iota(jnp.int32, (LANES, tm), 0)
    rt = jnp.where(frow == 0, i1 - ROUTE_LANE0, 0.0)
    rt = jnp.where(frow == 1, i2 - ROUTE_LANE0, rt)
    rt = jnp.where(frow == 2, w1, rt)
    rt = jnp.where(frow == 3, w2, rt)
    rt = jnp.where(frow == 4, r1, rt)
    rt = jnp.where(frow == 5, r2, rt)
    rt_ref[...] = rt[0:SUBLANES, :]
    ri_ref[...] = rt.T


def _outproj(mixes, w_bf16, xa, g1, gain2, sh2, sc2, wr, br, t0):
    b, s, d = _stream_shape(xa)
    nt = s // TM - t0
    so = nt * TM
    mod_idx = lambda bi, t: (jnp.where(t + t0 == 0, b, bi), 0, 0)
    tile = lambda wdt: pl.BlockSpec((None, TM, wdt), lambda bi, t: (bi, t + t0, 0))
    otile = lambda wdt: pl.BlockSpec((None, TM, wdt), lambda bi, t: (bi, t, 0))
    const = lambda shape: pl.BlockSpec(shape, lambda bi, t: (0,) * len(shape))
    x_specs, x_args = _stream_specs(xa, t0)
    in_specs = [tile(m.shape[2]) for m in mixes] + [const(w_bf16.shape)] + x_specs + [
        pl.BlockSpec((None, 1, d), mod_idx), const((1, d)),
        pl.BlockSpec((None, 1, d), mod_idx), pl.BlockSpec((None, 1, d), mod_idx),
        const((2, d, LANES)), const((1, LANES)),
    ]
    return pl.pallas_call(
        functools.partial(_outproj_kernel, n_mix=len(mixes), n_stream=len(x_args), t0=t0),
        grid=(b, nt),
        in_specs=in_specs,
        out_specs=[otile(d), otile(d), otile(LANES),
                   pl.BlockSpec((None, None, SUBLANES, TM), lambda bi, t: (bi, t, 0, 0)),
                   const((ROUTE_ROWS, LANES))],
        out_shape=[jax.ShapeDtypeStruct((b, so, d), F32), jax.ShapeDtypeStruct((b, so, d), F32),
                   jax.ShapeDtypeStruct((b, so, LANES), F32), jax.ShapeDtypeStruct((b, nt, SUBLANES, TM), F32),
                   jax.ShapeDtypeStruct((ROUTE_ROWS, LANES), F32)],
        scratch_shapes=[pltpu.VMEM((ROUTE_ROWS, LANES), F32)],
        compiler_params=_cparams(("arbitrary", "arbitrary")),
        name="outproj",
    )(*mixes, w_bf16, *x_args, g1, gain2, sh2, sc2, wr, br)


SUBLANES = 8


def _to_token_tiles(x):
    return x.reshape(x.shape[0], SUBLANES, x.shape[1] // SUBLANES)


def _from_token_tiles(x3):
    return x3.reshape(x3.shape[0], x3.shape[1] * x3.shape[2])


PAD_BITS = tuple(1 << k for k in reversed(range(MOE_ROWS.bit_length() - 1)))


def _dispatch_kernel(dest_ref, pads_ref, h_ref, xs_ref, hbuf, zbuf, sem, zsem):
    nt = pl.num_programs(1)
    step = pl.program_id(0) * nt + pl.program_id(1)
    last = pl.num_programs(0) * nt - 1
    slot = step % 2

    def zero_pads(wait):
        def expert(e, c):
            pos = pads_ref[0, e]
            n = pads_ref[1, e]
            for bit in PAD_BITS:
                take = (n & bit) != 0
                cp = pltpu.make_async_copy(zbuf.at[pl.ds(0, bit)], xs_ref.at[pl.ds(pos, bit)], zsem)

                @pl.when(take)
                def _():
                    cp.wait() if wait else cp.start()

                pos = pos + jnp.where(take, bit, 0)
            return c
        lax.fori_loop(0, MOE_EXPERTS, expert, 0)

        def tail(j, c):
            cp = pltpu.make_async_copy(
                zbuf, xs_ref.at[pl.ds(pads_ref[0, MOE_EXPERTS] + j * PAD_BITS[0], PAD_BITS[0])], zsem)
            cp.wait() if wait else cp.start()
            return c
        lax.fori_loop(0, pads_ref[1, MOE_EXPERTS], tail, 0)

    @pl.when(step == 0)
    def _():
        zbuf[...] = jnp.zeros_like(zbuf)
        zero_pads(wait=False)
    hbuf[slot] = _to_token_tiles(h_ref[...])

    def copy(sl, i, dst_row):
        return pltpu.make_async_copy(hbuf.at[sl, i], xs_ref.at[dst_row], sem.at[sl])

    for i in range(TM):
        copy(slot, i, dest_ref[step, i]).start()
        copy(slot, i, dest_ref[step, TM + i]).start()

    def drain(sl):
        for _ in range(2):
            pltpu.make_async_copy(hbuf.at[sl], xs_ref.at[pl.ds(0, TM)], sem.at[sl]).wait()

    @pl.when(step > 0)
    def _():
        drain(1 - slot)

    @pl.when(step == last)
    def _():
        drain(slot)
        zero_pads(wait=True)


def _dispatch(dest, pads, h2, n_pad):
    b, s, d = h2.shape
    nt = s // TM
    tile = (SUBLANES, d // SUBLANES)
    return pl.pallas_call(
        _dispatch_kernel,
        grid_spec=pltpu.PrefetchScalarGridSpec(
            num_scalar_prefetch=2,
            grid=(b, nt),
            in_specs=[pl.BlockSpec((None, TM, d), lambda bi, t, dr, pd: (bi, t, 0))],
            out_specs=pl.BlockSpec(memory_space=pl.ANY),
            scratch_shapes=[pltpu.VMEM((2, TM) + tile, F32), pltpu.VMEM((PAD_BITS[0],) + tile, F32),
                            pltpu.SemaphoreType.DMA((2,)), pltpu.SemaphoreType.DMA],
        ),
        out_shape=jax.ShapeDtypeStruct((n_pad,) + tile, F32),
        compiler_params=_cparams(("arbitrary", "arbitrary")),
        name="dispatch",
    )(dest, pads, h2)


def _mlp_kernel(be_ref, nu_ref, nv_ref, x_ref, wgu_ref, wd_ref, y_ref, wgu_b, wd_b):
    i = pl.program_id(0)
    prev = be_ref[jnp.maximum(i - 1, 0)]
    used = i < nu_ref[0]
    half = MOE_ROWS // 2
    many = nv_ref[i] > half

    @pl.when(used & ((i == 0) | (be_ref[i] != prev)))
    def _():
        wgu_b[...] = wgu_ref[...].astype(BF16)
        wd_b[...] = wd_ref[...].astype(BF16)

    def mlp(rows):
        x = _from_token_tiles(x_ref[0:rows])
        gu = jnp.dot(x.astype(BF16), wgu_b[...], preferred_element_type=F32)
        act = _silu(gu[:, :MOE_FF]) * gu[:, MOE_FF:]
        y_ref[0:rows] = _to_token_tiles(jnp.dot(act.astype(BF16), wd_b[...], preferred_element_type=F32))

    @pl.when(used & many)
    def _():
        mlp(MOE_ROWS)

    @pl.when(used & jnp.logical_not(many))
    def _():
        mlp(half)
        y_ref[half:MOE_ROWS] = jnp.zeros((MOE_ROWS - half,) + y_ref.shape[1:], F32)

    @pl.when(jnp.logical_not(used))
    def _():
        y_ref[...] = jnp.zeros_like(y_ref)


def _expert_mlp(block_e, n_used, n_valid, xs, w_gu, w_down, layer):
    n_pad, sub, dl = xs.shape
    d = sub * dl
    ff2 = w_gu.shape[-1]
    slots = pl.BlockSpec((MOE_ROWS, sub, dl), lambda i, be, nu, nv: (i, 0, 0))
    used_slots = pl.BlockSpec((MOE_ROWS, sub, dl), lambda i, be, nu, nv: (jnp.minimum(i, nu[0] - 1), 0, 0))
    return pl.pallas_call(
        _mlp_kernel,
        grid_spec=pltpu.PrefetchScalarGridSpec(
            num_scalar_prefetch=3,
            grid=(n_pad // MOE_ROWS,),
            in_specs=[
                used_slots,
                pl.BlockSpec((None, None, d, ff2), lambda i, be, nu, nv: (layer, be[i], 0, 0)),
                pl.BlockSpec((None, None, ff2 // 2, d), lambda i, be, nu, nv: (layer, be[i], 0, 0)),
            ],
            out_specs=slots,
            scratch_shapes=[pltpu.VMEM((d, ff2), BF16), pltpu.VMEM((ff2 // 2, d), BF16)],
        ),
        out_shape=jax.ShapeDtypeStruct(xs.shape, F32),
        compiler_params=_cparams(("arbitrary",)),
        name="expert_mlp",
    )(block_e, n_used, n_valid, xs, w_gu, w_down)


def _combine_kernel(dest_ref, ys_ref, x_ref, ri_ref, g2_ref, fg_ref, o_ref, buf, sem, *, final):
    bi = pl.program_id(0)
    t = pl.program_id(1)
    nt = pl.num_programs(1)
    step = bi * nt + t
    total = pl.num_programs(0) * nt

    def copy(src_row, slot, k, i):
        return pltpu.make_async_copy(ys_ref.at[src_row], buf.at[slot, k * TM + i], sem.at[slot])

    def issue(st, slot):
        for i in range(TM):
            copy(dest_ref[st, i], slot, 0, i).start()
            copy(dest_ref[st, TM + i], slot, 1, i).start()

    slot = step % 2

    @pl.when(step == 0)
    def _():
        issue(0, 0)

    @pl.when(step + 1 < total)
    def _():
        issue(step + 1, 1 - slot)

    pltpu.make_async_copy(ys_ref.at[pl.ds(0, 2 * TM)], buf.at[slot], sem.at[slot]).wait()

    lane = lax.broadcasted_iota(jnp.int32, (TM, LANES), 1)
    ri = ri_ref[...]
    w1 = jnp.sum(jnp.where(lane == 2, ri, 0.0), axis=-1, keepdims=True)
    w2 = jnp.sum(jnp.where(lane == 3, ri, 0.0), axis=-1, keepdims=True)
    y = (_from_token_tiles(buf[slot, 0:TM]) * w1 + _from_token_tiles(buf[slot, TM:2 * TM]) * w2)
    xn = x_ref[...] + g2_ref[...] * y
    if final:
        ms = jnp.mean(xn * xn, axis=-1, keepdims=True)
        xn = xn * lax.rsqrt(ms + NORM_EPS) * fg_ref[...]
    o_ref[...] = xn


def _combine(dest, ys, xa, rinfo, g2, final_g, has_ctx, final):
    b, s, d = xa.shape
    nt = s // TM
    mod_idx = lambda bi, t, dr: (jnp.where(t == 0, b, bi) if has_ctx else bi, 0, 0)
    tile = lambda wdt: pl.BlockSpec((None, TM, wdt), lambda bi, t, dr: (bi, t, 0))
    out_spec = tile(d)
    out_shape = jax.ShapeDtypeStruct((b, s, d), F32)
    return pl.pallas_call(
        functools.partial(_combine_kernel, final=final),
        grid_spec=pltpu.PrefetchScalarGridSpec(
            num_scalar_prefetch=1,
            grid=(b, nt),
            in_specs=[
                pl.BlockSpec(memory_space=pl.ANY),
                tile(d), tile(LANES),
                pl.BlockSpec((None, 1, d), mod_idx),
                pl.BlockSpec((1, d), lambda bi, t, dr: (0, 0)),
            ],
            out_specs=out_spec,
            scratch_shapes=[pltpu.VMEM((2, 2 * TM, SUBLANES, d // SUBLANES), F32),
                            pltpu.SemaphoreType.DMA((2,))],
        ),
        out_shape=out_shape,
        compiler_params=_cparams(("arbitrary", "arbitrary")),
        name="combine",
    )(dest, ys, xa, rinfo, g2, final_g)


def _moe(h2, rt, counts, w_gu, w_down, layer):
    b, s, d = h2.shape
    nt = s // TM
    r = rt.reshape(b * nt, SUBLANES, TM)
    fields = lambda i: jnp.concatenate([r[:, i], r[:, i + 1]], axis=1).astype(jnp.int32)
    e = fields(0)
    rank = fields(4)
    cnt = counts[ROUTE_LANE0:ROUTE_LANE0 + MOE_EXPERTS, 0].astype(jnp.int32)
    padded = (cnt + MOE_ROWS - 1) // MOE_ROWS * MOE_ROWS
    ends = jnp.cumsum(padded)
    starts = ends - padded
    eids = jnp.arange(MOE_EXPERTS, dtype=jnp.int32)
    dest = jnp.sum(jnp.where(e[..., None] == eids, starts, 0), axis=-1) + rank
    n_assign = b * nt * TM * 2
    n_blocks = (n_assign + MOE_EXPERTS * (MOE_ROWS - 1)) // MOE_ROWS + 1
    n_pad = n_blocks * MOE_ROWS
    blk_row = jnp.arange(n_blocks, dtype=jnp.int32) * MOE_ROWS
    block_e = jnp.minimum(jnp.sum((blk_row[:, None] >= ends[None, :]).astype(jnp.int32), axis=1),
                          MOE_EXPERTS - 1)
    n_used = (ends[-1] // MOE_ROWS).astype(jnp.int32).reshape(1)
    pads = jnp.stack([jnp.append(starts + cnt, ends[-1]),
                      jnp.append(padded - cnt, (n_pad - ends[-1]) // PAD_BITS[0])]).astype(jnp.int32)
    xs = _dispatch(dest, pads, h2, n_pad)
    ends_b = jnp.sum(jnp.where(block_e[:, None] == eids, starts + cnt, 0), axis=1)
    n_valid = jnp.clip(ends_b - blk_row, 0, MOE_ROWS).astype(jnp.int32)
    ys = _expert_mlp(block_e, n_used, n_valid, xs, w_gu, w_down, layer)
    return dest, ys


def _router_weights(wg, bg, we, be):
    d = wg.shape[0]
    pad = LANES - MOE_GROUPS - MOE_EXPERTS
    assert ROUTE_LANE0 == MOE_GROUPS
    wr = jnp.concatenate([wg.astype(F32), we.astype(F32), jnp.zeros((d, pad), F32)], axis=1)
    br = jnp.concatenate([bg.astype(F32), be.astype(F32), jnp.zeros((pad,), F32)]).reshape(1, LANES)
    return jnp.stack(_split_bf16(wr)), br


def kernel(x, c, ctx, c_ctx, ada_w, ada_b, norm_g, final_g, ab_w_in, ab_w_out, ret_decay, ret_gn, na_rpb,
           swa_w_in, swa_w_out, swa_sink, router_g_w, router_g_b, router_e_w, router_e_b,
           expert_w_gu, expert_w_down):
    b, seq, d = x.shape
    assert ctx.shape[1] == TM and seq % TM == 0 and d == D_MODEL
    xa = (ctx, x)
    rope = _rope_tables(seq)

    cvec = jnp.concatenate([c, c_ctx[None, :], jnp.zeros((7, d), F32)], axis=0)
    mod = _adaln(cvec, ada_w, ada_b)
    mod = mod.reshape(DEPTH, b + 8, 6, 1, d)[:, :b + 1].transpose(0, 2, 1, 3, 4)

    for layer in range(DEPTH):
        last = layer == DEPTH - 1
        sh1, sc1, g1, sh2, sc2, g2 = (mod[layer, i] for i in range(6))
        gain1 = norm_g[layer, 0].reshape(1, d)
        gain2 = norm_g[layer, 1].reshape(1, d)
        j = layer // 2
        if layer % 2 == 0:
            nb = RET_W // LANES
            qk_scale = HEAD_DIM ** -0.5
            scales = {blk: qk_scale for blk in range(nb, 2 * nb)}
            scales.update({blk: qk_scale * LOG2E for blk in range(4 * nb, 5 * nb)})
            p = _proj(xa, gain1, sh1, sc1, ab_w_in[j].astype(BF16), rope,
                      rope_blocks=range(0, 2 * nb), scales=scales)
            log_gamma = jnp.log1p(-jnp.exp2(-ret_decay[j].astype(F32)))
            mixes = [_retention(p, log_gamma, ret_gn[j]), _neighbourhood(p, _na_bias(na_rpb[j]))]
            w_out = ab_w_out[j]
        else:
            qb = SWA_HEADS * HEAD_DIM // LANES
            kb = SWA_KV_HEADS * HEAD_DIM // LANES
            p = _proj(xa, gain1, sh1, sc1, swa_w_in[j].astype(BF16), rope,
                      rope_blocks=range(0, qb + kb),
                      scales={blk: HEAD_DIM ** -0.5 * LOG2E for blk in range(qb)})
            mixes = [_swa(p, swa_sink[j].astype(F32))]
            w_out = swa_w_out[j]
        t0 = 1 if last else 0
        wr, br = _router_weights(router_g_w[layer], router_g_b[layer], router_e_w[layer], router_e_b[layer])
        xa, h2, rinfo, rt, counts = _outproj(mixes, w_out.astype(BF16), xa, g1, gain2, sh2, sc2, wr, br, t0)
        dest, ys = _moe(h2, rt, counts, expert_w_gu, expert_w_down, layer)
        xa = _combine(dest, ys, xa, rinfo, g2, final_g.reshape(1, d), not last, last)
    return xa
```

```python
import functools

import numpy as np
import jax
import jax.numpy as jnp
from jax import lax
from jax.experimental import pallas as pl
from jax.experimental.pallas import tpu as pltpu

F32 = jnp.float32
BF16 = jnp.bfloat16

D_MODEL = 1024
DEPTH = 2
GRID_W = 64
HEAD_DIM = 64
RET_HEADS = 8
NA_HEADS = 8
RET_W = 512
NA_W = 512
AB_IN = 4 * RET_W + 3 * NA_W
GN_EPS = 1e-5
NA_KH = 8
NA_KW = 16
SWA_HEADS = 16
SWA_KV_HEADS = 4
SWA_WINDOW = 128
SWA_BLOCK = 128
SWA_IN = (SWA_HEADS + 2 * SWA_KV_HEADS) * HEAD_DIM
ROPE_BASE = 10000.0
MOE_GROUPS = 4
MOE_EPG = 8
MOE_EXPERTS = 32
MOE_FF = 512
NORM_EPS = 1e-6
NEG_INF = -1e30

LANES = 128
TM = 256
RET_BLOCK = 256
MOE_ROWS = 1024
ROUTE_LANE0 = 4
ROUTE_ROWS = 48
VMEM_LIMIT = 56 * 1024 * 1024


def _cparams(sem):
    return pltpu.CompilerParams(dimension_semantics=sem, vmem_limit_bytes=VMEM_LIMIT)


def _split_bf16(a):
    hi = a.astype(BF16)
    lo = (a - hi.astype(F32)).astype(BF16)
    return hi, lo


def _dot3_split(a, bh, bl):
    ah, al = _split_bf16(a)
    d = lambda x, y: jnp.dot(x, y, preferred_element_type=F32)
    return d(ah, bh) + (d(ah, bl) + d(al, bh))


def _dot3(a, b):
    return _dot3_split(a, *_split_bf16(b))


def _dot_nt(a, b):
    return lax.dot_general(a, b, (((1,), (1,)), ((), ())), preferred_element_type=F32)


def _dot_tn(a, b):
    return lax.dot_general(a, b, (((0,), (0,)), ((), ())), preferred_element_type=F32)


def _silu(x):
    return x / (1.0 + jnp.exp(-x))


def _adaln_kernel(c_ref, w_ref, b_ref, o_ref):
    o_ref[...] = _dot3(_silu(c_ref[...]), w_ref[...]) + b_ref[...]


def _adaln(cvec, ada_w, ada_b):
    depth, d, n6 = ada_w.shape
    rows = cvec.shape[0]
    tn = 1024
    return pl.pallas_call(
        _adaln_kernel,
        grid=(depth, n6 // tn),
        in_specs=[
            pl.BlockSpec((rows, d), lambda l, j: (0, 0)),
            pl.BlockSpec((None, d, tn), lambda l, j: (l, 0, j)),
            pl.BlockSpec((None, 1, tn), lambda l, j: (l, 0, j)),
        ],
        out_specs=pl.BlockSpec((None, rows, tn), lambda l, j: (l, 0, j)),
        out_shape=jax.ShapeDtypeStruct((depth, rows, n6), F32),
        compiler_params=_cparams(("arbitrary", "arbitrary")),
        name="adaln",
    )(cvec, ada_w, ada_b.reshape(depth, 1, n6))


def _rms_mod(x, g, sh, sc):
    ms = jnp.mean(x * x, axis=-1, keepdims=True)
    return (x * lax.rsqrt(ms + NORM_EPS) * g) * (1.0 + sc) + sh


def _stream_tile(refs, is_lat):
    if len(refs) == 1:
        return refs[0][...]
    return jnp.where(is_lat, refs[1][...], refs[0][...])


def _stream_specs(stream, t0):
    def im(f):
        return (lambda bi, t, *_: f(bi, t + t0))
    if not isinstance(stream, tuple):
        return [pl.BlockSpec((None, TM, stream.shape[2]), im(lambda bi, t: (bi, t, 0)))], [stream]
    ctx, x = stream
    d = x.shape[2]
    return ([pl.BlockSpec((None, TM, d), im(lambda bi, t: (bi, 0, 0))),
             pl.BlockSpec((None, TM, d), im(lambda bi, t: (bi, jnp.maximum(t - 1, 0), 0)))], [ctx, x])


def _stream_shape(stream):
    if not isinstance(stream, tuple):
        return stream.shape
    ctx, x = stream
    return (x.shape[0], ctx.shape[1] + x.shape[1], x.shape[2])


def _proj_kernel(*refs, rope_blocks, scales, cn, n_stream):
    x_refs = refs[:n_stream]
    g_ref, sh_ref, sc_ref, w_ref, rope_ref, o_ref = refs[n_stream:]
    is_lat = pl.program_id(1) > 0
    hb = _rms_mod(_stream_tile(x_refs, is_lat), g_ref[...], sh_ref[...], sc_ref[...]).astype(BF16)
    nout = w_ref.shape[1]
    for c in range(nout // cn):
        o = jnp.dot(hb, w_ref[:, c * cn:(c + 1) * cn], preferred_element_type=F32)
        for s in range(cn // LANES):
            blk = c * (cn // LANES) + s
            ob = o[:, s * LANES:(s + 1) * LANES]
            if blk in rope_blocks:
                r = (ob * rope_ref[0] + pltpu.roll(ob, 16, 1) * rope_ref[1]
                     + pltpu.roll(ob, LANES - 16, 1) * rope_ref[2])
                ob = jnp.where(is_lat, r, ob)
            if blk in scales:
                ob = ob * scales[blk]
            o_ref[:, blk * LANES:(blk + 1) * LANES] = ob.astype(o_ref.dtype)


def _proj(xa, gain, sh, sc, w_bf16, rope, rope_blocks, scales):
    b, s, d = _stream_shape(xa)
    nout = w_bf16.shape[1]
    nt = s // TM
    mod_idx = lambda bi, t: (jnp.where(t == 0, b, bi), 0, 0)
    x_specs, x_args = _stream_specs(xa, 0)
    kern = functools.partial(_proj_kernel, rope_blocks=frozenset(rope_blocks), scales=dict(scales), cn=512,
                             n_stream=len(x_args))
    return pl.pallas_call(
        kern,
        grid=(b, nt),
        in_specs=x_specs + [
            pl.BlockSpec((1, d), lambda bi, t: (0, 0)),
            pl.BlockSpec((None, 1, d), mod_idx),
            pl.BlockSpec((None, 1, d), mod_idx),
            pl.BlockSpec((d, nout), lambda bi, t: (0, 0)),
            pl.BlockSpec((3, TM, LANES), lambda bi, t: (0, jnp.maximum(t - 1, 0), 0)),
        ],
        out_specs=pl.BlockSpec((None, TM, nout), lambda bi, t: (bi, t, 0)),
        out_shape=jax.ShapeDtypeStruct((b, s, nout), BF16),
        compiler_params=_cparams(("arbitrary", "arbitrary")),
        name="proj",
    )(*x_args, gain, sh, sc, w_bf16, rope)


def _rope_tables(seq):
    nf = HEAD_DIM // 4
    inv = ROPE_BASE ** (-jnp.arange(nf, dtype=F32) / nf)
    t = jnp.arange(seq)
    row = (t // GRID_W).astype(F32)
    col = (t % GRID_W).astype(F32)
    lane = np.arange(LANES)
    jj = lane % HEAD_DIM
    axis_is_col = (jj // 32) == 1
    second_half = (jj % 32) >= 16
    f = jj % 16
    pos = jnp.where(axis_is_col[None, :], col[:, None], row[:, None])
    ang = pos * inv[f][None, :]
    c, s = jnp.cos(ang), jnp.sin(ang)
    sa = jnp.where(second_half[None, :], s, 0.0)
    sb = jnp.where(second_half[None, :], 0.0, -s)
    return jnp.stack([c, sa, sb], axis=0)


def _ret_kernel(lg_ref, q_ref, k_ref, v_ref, g_ref, gn_ref, o_ref, accf_ref, accb_ref,
                intra_ref, qd_ref, kd_ref):
    hp = pl.program_id(0)
    c = RET_BLOCK
    s_len = q_ref.shape[0]
    n_chunks = s_len // c
    ctx_chunks = TM // c
    pos = lax.broadcasted_iota(jnp.int32, (c, LANES), 0).astype(F32)
    m0 = lax.broadcasted_iota(jnp.int32, (c, LANES), 1) < HEAD_DIM
    same_head = ((lax.broadcasted_iota(jnp.int32, (LANES, LANES), 0) < HEAD_DIM)
                 == (lax.broadcasted_iota(jnp.int32, (LANES, LANES), 1) < HEAD_DIM))
    rel = (lax.broadcasted_iota(jnp.int32, (c, c), 0) - lax.broadcasted_iota(jnp.int32, (c, c), 1)).astype(F32)
    lgf = [lg_ref[0, hp * 2 + hh] for hh in range(2)]
    lgb = [lg_ref[1, hp * 2 + hh] for hh in range(2)]
    lgf_l = jnp.where(m0, lgf[0], lgf[1])
    lgb_l = jnp.where(m0, lgb[0], lgb[1])
    @pl.when(pl.program_id(1) == 0)
    def _():
        for hh in range(2):
            intra_ref[0, :, hh * c:(hh + 1) * c] = jnp.where(
                rel >= 0, jnp.exp(lgf[hh] * jnp.maximum(rel, 0.0)), 0.0)
            intra_ref[1, :, hh * c:(hh + 1) * c] = jnp.where(
                rel <= 0, jnp.exp(lgb[hh] * jnp.maximum(-rel, 0.0)), 0.0)
        qd_ref[0] = jnp.exp(lgf_l * (pos + 1.0))
        qd_ref[1] = jnp.exp(lgb_l * (c - pos))
        kd_ref[0] = jnp.exp(lgf_l * (c - 1.0 - pos))
        kd_ref[1] = jnp.exp(lgb_l * pos)
    cd = [jnp.exp(lgf_l[0:1] * float(c)), jnp.exp(lgb_l[0:1] * float(c))]

    def chunk(r0, state, d):
        q = q_ref[pl.ds(r0, c), :]
        k = k_ref[pl.ds(r0, c), :]
        v = v_ref[pl.ds(r0, c), :]
        qb = q.astype(BF16)
        kcat = jnp.concatenate([jnp.where(m0, k, 0.0), jnp.where(m0, 0.0, k)], axis=0).astype(BF16)
        vcat = jnp.concatenate([jnp.where(m0, v, 0.0), jnp.where(m0, 0.0, v)], axis=0).astype(BF16)
        s = _dot_nt(qb, kcat) * intra_ref[d]
        out = jnp.dot(s.astype(BF16), vcat, preferred_element_type=F32)
        out = out + jnp.dot(qb, state.astype(BF16), preferred_element_type=F32) * qd_ref[d]
        kv = _dot_tn((k * kd_ref[d]).astype(BF16), v.astype(BF16))
        return out, state * cd[d] + jnp.where(same_head, kv, 0.0)

    def body(i, states):
        sf, sb = states
        rf = pl.multiple_of(i * c, c)
        ib = jnp.where(i < ctx_chunks, ctx_chunks - 1 - i, n_chunks + ctx_chunks - 1 - i)
        rb = pl.multiple_of(ib * c, c)
        of, sf = chunk(rf, sf, 0)
        ob, sb = chunk(rb, sb, 1)
        accf_ref[pl.ds(rf, c), :] = of
        accb_ref[pl.ds(rb, c), :] = ob
        return sf, sb

    z = jnp.zeros((LANES, LANES), F32)
    lax.fori_loop(0, n_chunks, body, (z, z), unroll=True)

    avg = jnp.where(same_head, 1.0 / HEAD_DIM, 0.0).astype(BF16)

    def head_mean(x):
        hi, lo = _split_bf16(x)
        return (jnp.dot(hi, avg, preferred_element_type=F32) + jnp.dot(lo, avg, preferred_element_type=F32))

    def readout(i, carry):
        r0 = pl.multiple_of(i * c, c)
        o = accf_ref[pl.ds(r0, c), :] + accb_ref[pl.ds(r0, c), :]
        dlt = o - head_mean(o)
        var = head_mean(dlt * dlt)
        y = dlt * lax.rsqrt(var + GN_EPS) * gn_ref[...]
        o_ref[pl.ds(r0, c), :] = _silu(g_ref[pl.ds(r0, c), :].astype(F32)) * y
        return carry

    lax.fori_loop(0, n_chunks, readout, 0, unroll=True)


def _retention(p, log_gamma, ret_gn):
    b, s, _ = p.shape
    nb = RET_W // LANES
    blk = lambda off: pl.BlockSpec((None, s, LANES), lambda hp, bi: (bi, 0, off + hp))
    return pl.pallas_call(
        _ret_kernel,
        grid=(nb, b),
        in_specs=[
            pl.BlockSpec(memory_space=pltpu.SMEM),
            blk(0), blk(nb), blk(2 * nb), blk(3 * nb),
            pl.BlockSpec((1, LANES), lambda hp, bi: (0, hp)),
        ],
        out_specs=pl.BlockSpec((None, s, LANES), lambda hp, bi: (bi, 0, hp)),
        out_shape=jax.ShapeDtypeStruct((b, s, RET_W), F32),
        scratch_shapes=[pltpu.VMEM((s, LANES), F32), pltpu.VMEM((s, LANES), F32),
                        pltpu.VMEM((2, RET_BLOCK, 2 * RET_BLOCK), F32),
                        pltpu.VMEM((2, RET_BLOCK, LANES), F32), pltpu.VMEM((2, RET_BLOCK, LANES), F32)],
        compiler_params=_cparams(("arbitrary", "arbitrary")),
        name="retention",
    )(log_gamma, p, p, p, p, ret_gn.reshape(1, RET_W))


LOG2E = 1.4426950408889634


def _softmax_pv(s_list, v_list, extra=None):
    m = None
    for s in s_list:
        for j in range(s.shape[1] // LANES):
            blk = s[:, j * LANES:(j + 1) * LANES]
            m = blk if m is None else jnp.maximum(m, blk)
    m = m.max(axis=-1, keepdims=True)
    if extra is not None:
        m = jnp.maximum(m, extra)
    acc = None
    for s, v in zip(s_list, v_list):
        pv = jnp.dot(jnp.exp2(s - m).astype(BF16), v, preferred_element_type=F32)
        acc = pv if acc is None else acc + pv
    o, den = acc[:, :LANES], acc[:, LANES:]
    if extra is not None:
        den = den + jnp.exp2(extra - m)
    return o / den


def _stage_heads(q_ref, k_ref, v_ref, qm_ref, kb_ref, vb_ref):
    m0 = lax.broadcasted_iota(jnp.int32, (TM, LANES), 1) < HEAD_DIM

    def stage(i, carry):
        r0 = pl.multiple_of(i * TM, TM)
        q = q_ref[pl.ds(r0, TM), :]
        qm_ref[0, pl.ds(r0, TM), :] = jnp.where(m0, q, 0.0).astype(BF16)
        qm_ref[1, pl.ds(r0, TM), :] = jnp.where(m0, 0.0, q).astype(BF16)
        kb_ref[pl.ds(r0, TM), :] = k_ref[pl.ds(r0, TM), :].astype(BF16)
        vb_ref[pl.ds(r0, TM), 0:LANES] = v_ref[pl.ds(r0, TM), :].astype(BF16)
        vb_ref[pl.ds(r0, TM), LANES:2 * LANES] = jnp.ones((TM, LANES), BF16)
        return carry

    lax.fori_loop(0, q_ref.shape[0] // TM, stage, 0, unroll=True)


def _na_kernel(q_ref, k_ref, v_ref, bias_ref, o_ref, qm_ref, kb_ref, vb_ref):
    s_len = q_ref.shape[0]
    rows = (s_len - TM) // GRID_W
    nloc = NA_KH * GRID_W
    _stage_heads(q_ref, k_ref, v_ref, qm_ref, kb_ref, vb_ref)

    kc = kb_ref[0:TM, :]
    vc = vb_ref[0:TM, :]
    outs = [_softmax_pv([_dot_nt(qm_ref[hh, 0:TM, :], kc)], [vc]) for hh in range(2)]
    m0c = lax.broadcasted_iota(jnp.int32, (TM, LANES), 1) < HEAD_DIM
    o_ref[0:TM, :] = jnp.where(m0c, outs[0], outs[1])

    m0 = lax.broadcasted_iota(jnp.int32, (GRID_W, LANES), 1) < HEAD_DIM

    def row_block(r, carry):
        rs = jnp.clip(r - NA_KH // 2, 0, rows - NA_KH)
        pat = r - rs
        q0 = pl.multiple_of(TM + r * GRID_W, GRID_W)
        k0 = pl.multiple_of(TM + rs * GRID_W, GRID_W)
        kl = kb_ref[pl.ds(k0, nloc), :]
        vl = vb_ref[pl.ds(k0, nloc), :]
        kc = kb_ref[0:TM, :]
        vc = vb_ref[0:TM, :]
        q = jnp.concatenate([qm_ref[0, pl.ds(q0, GRID_W), :], qm_ref[1, pl.ds(q0, GRID_W), :]], axis=0)
        s_loc = _dot_nt(q, kl) + bias_ref[pat].reshape(2 * GRID_W, nloc)
        s_ctx = _dot_nt(q, kc)
        res = _softmax_pv([s_loc, s_ctx], [vl, vc])
        o_ref[pl.ds(q0, GRID_W), :] = jnp.where(m0, res[:GRID_W], res[GRID_W:])
        return carry

    lax.fori_loop(0, rows, row_block, 0, unroll=True)


def _na_bias(rpb):
    h = rpb.shape[0]
    qc = np.arange(GRID_W)[:, None]
    kc = np.arange(GRID_W)[None, :]
    win = np.clip(qc - NA_KW // 2, 0, GRID_W - NA_KW)
    valid = (kc >= win) & (kc < win + NA_KW)
    col_off = np.clip(kc - qc + NA_KW - 1, 0, 2 * NA_KW - 2)
    onehot = (col_off[None] == np.arange(2 * NA_KW - 1)[:, None, None]).astype(np.float32)
    cols = jnp.einsum("hrc,cqk->hrqk", rpb.astype(F32), onehot, precision=lax.Precision.HIGHEST)
    cols = jnp.where(valid[None, None], cols * LOG2E, NEG_INF)
    bias = jnp.stack([cols[:, NA_KH - 1 - p:2 * NA_KH - 1 - p] for p in range(NA_KH)])
    bias = bias.transpose(0, 1, 3, 2, 4)
    return bias.reshape(NA_KH, h, GRID_W, NA_KH * GRID_W)


def _neighbourhood(p, bias):
    b, s, _ = p.shape
    nb = NA_W // LANES
    c0 = 4 * RET_W // LANES
    blk = lambda off: pl.BlockSpec((None, s, LANES), lambda hp, bi: (bi, 0, c0 + off + hp))
    return pl.pallas_call(
        _na_kernel,
        grid=(nb, b),
        in_specs=[
            blk(0), blk(nb), blk(2 * nb),
            pl.BlockSpec((NA_KH, 2, GRID_W, NA_KH * GRID_W), lambda hp, bi: (0, hp, 0, 0)),
        ],
        out_specs=pl.BlockSpec((None, s, LANES), lambda hp, bi: (bi, 0, hp)),
        out_shape=jax.ShapeDtypeStruct((b, s, NA_W), F32),
        scratch_shapes=[pltpu.VMEM((2, s, LANES), BF16), pltpu.VMEM((s, LANES), BF16),
                        pltpu.VMEM((s, 2 * LANES), BF16)],
        compiler_params=_cparams(("arbitrary", "arbitrary")),
        name="neighbourhood",
    )(p, p, p, bias)


def _swa_kernel(sink_ref, q_ref, k_ref, v_ref, o_ref, kd_ref, vd_ref):
    kp = pl.program_id(1)
    s_len = q_ref.shape[0]
    seq = s_len - TM
    nblk = seq // SWA_BLOCK
    band = SWA_BLOCK + 2 * SWA_WINDOW
    group = SWA_HEADS // SWA_KV_HEADS
    heads_per_step = 2 * group
    m0t = lax.broadcasted_iota(jnp.int32, (TM, LANES), 1) < HEAD_DIM

    def stage(i, carry):
        r0 = pl.multiple_of(i * TM, TM)
        for src, dst in ((k_ref, kd_ref), (v_ref, vd_ref)):
            x = src[pl.ds(r0, TM), :].astype(F32)
            xr = pltpu.roll(x, HEAD_DIM, 1)
            dst[0, pl.ds(r0, TM), 0:LANES] = jnp.where(m0t, x, xr).astype(BF16)
            dst[1, pl.ds(r0, TM), 0:LANES] = jnp.where(m0t, xr, x).astype(BF16)
        for h in range(2):
            vd_ref[h, pl.ds(r0, TM), LANES:2 * LANES] = jnp.ones((TM, LANES), BF16)
        return carry

    lax.fori_loop(0, s_len // TM, stage, 0, unroll=True)
    o_ref[0:TM, :] = jnp.zeros((TM, o_ref.shape[1]), F32)

    rows = group * SWA_BLOCK
    qi = lax.broadcasted_iota(jnp.int32, (rows, band), 0) % SWA_BLOCK
    ki = lax.broadcasted_iota(jnp.int32, (rows, band), 1)
    m0 = lax.broadcasted_iota(jnp.int32, (SWA_BLOCK, LANES), 1) < HEAD_DIM
    head_of_row = lax.broadcasted_iota(jnp.int32, (rows, 1), 0) // SWA_BLOCK

    def block(i, carry):
        start = jnp.clip((i - 1) * SWA_BLOCK, 0, seq - band)
        k0 = pl.multiple_of(TM + start, SWA_BLOCK)
        q0 = pl.multiple_of(TM + i * SWA_BLOCK, SWA_BLOCK)
        valid = jnp.abs(qi + (i * SWA_BLOCK - start) - ki) <= SWA_WINDOW
        for hh in range(2):
            kb = kd_ref[hh, pl.ds(k0, band), :]
            vb = vd_ref[hh, pl.ds(k0, band), :]
            kc = kd_ref[hh, 0:TM, :]
            vc = vd_ref[hh, 0:TM, :]
            parts = []
            for j in range(group // 2):
                pair = hh * (group // 2) + j
                qp = q_ref[pl.ds(q0, SWA_BLOCK), pair * LANES:(pair + 1) * LANES]
                parts += [jnp.where(m0, qp, 0.0), jnp.where(m0, 0.0, qp)]
            q = jnp.concatenate(parts, axis=0).astype(BF16)
            s = jnp.where(valid, _dot_nt(q, kb), NEG_INF)
            s_ctx = _dot_nt(q, kc)
            h0 = kp * heads_per_step + hh * group
            sink = jnp.full((rows, 1), sink_ref[h0], F32)
            for g in range(1, group):
                sink = jnp.where(head_of_row == g, sink_ref[h0 + g], sink)
            res = _softmax_pv([s, s_ctx], [vb, vc], extra=sink * LOG2E)
            for j in range(group // 2):
                pair = hh * (group // 2) + j
                r0 = 2 * j * SWA_BLOCK
                o_ref[pl.ds(q0, SWA_BLOCK), pair * LANES:(pair + 1) * LANES] = jnp.where(
                    m0, res[r0:r0 + SWA_BLOCK], res[r0 + SWA_BLOCK:r0 + 2 * SWA_BLOCK])
        return carry

    lax.fori_loop(0, nblk, block, 0, unroll=True)


def _swa(p, sink):
    b, s, _ = p.shape
    qw = SWA_HEADS * HEAD_DIM // 2
    kblk = SWA_HEADS * HEAD_DIM // LANES
    vblk = kblk + SWA_KV_HEADS * HEAD_DIM // LANES
    return pl.pallas_call(
        _swa_kernel,
        grid=(b, 2),
        in_specs=[
            pl.BlockSpec(memory_space=pltpu.SMEM),
            pl.BlockSpec((None, s, qw), lambda bi, kp: (bi, 0, kp)),
            pl.BlockSpec((None, s, LANES), lambda bi, kp: (bi, 0, kblk + kp)),
            pl.BlockSpec((None, s, LANES), lambda bi, kp: (bi, 0, vblk + kp)),
        ],
        out_specs=pl.BlockSpec((None, s, qw), lambda bi, kp: (bi, 0, kp)),
        out_shape=jax.ShapeDtypeStruct((b, s, SWA_HEADS * HEAD_DIM), F32),
        scratch_shapes=[pltpu.VMEM((2, s, LANES), BF16), pltpu.VMEM((2, s, 2 * LANES), BF16)],
        compiler_params=_cparams(("arbitrary", "arbitrary")),
        name="swa",
    )(sink, p, p, p)


def _outproj_kernel(*refs, n_mix, n_stream, t0):
    mix_refs = refs[:n_mix]
    x_refs = refs[n_mix + 1:n_mix + 1 + n_stream]
    w_ref = refs[n_mix]
    (g1_ref, gn_ref, sh_ref, sc_ref, wr_ref, br_ref,
     xo_ref, h_ref, ri_ref, rt_ref, cnt_ref, carry_ref) = refs[n_mix + 1 + n_stream:]
    first = (pl.program_id(0) == 0) & (pl.program_id(1) == 0)

    @pl.when(first)
    def _():
        carry_ref[...] = jnp.zeros_like(carry_ref)

    o = None
    off = 0
    for m_ref in mix_refs:
        kw = m_ref.shape[1]
        part = jnp.dot(m_ref[...].astype(BF16), w_ref[off:off + kw, :], preferred_element_type=F32)
        o = part if o is None else o + part
        off += kw
    xn = _stream_tile(x_refs, pl.program_id(1) + t0 > 0) + g1_ref[...] * o
    xo_ref[...] = xn
    h = _rms_mod(xn, gn_ref[...], sh_ref[...], sc_ref[...])
    h_ref[...] = h

    logits = _dot3_split(h, wr_ref[0], wr_ref[1]) + br_ref[...]
    tm = logits.shape[0]
    lt = logits.T[0:ROUTE_ROWS, :]
    row = lax.broadcasted_iota(jnp.int32, (ROUTE_ROWS, tm), 0).astype(F32)
    big = 1e9
    gmask = row < MOE_GROUPS
    mg = jnp.max(jnp.where(gmask, lt, -big), axis=0, keepdims=True)
    sg = jnp.sum(jnp.where(gmask, jnp.exp(jnp.minimum(lt - mg, 0.0)), 0.0), axis=0, keepdims=True)
    gw = 1.0 / sg
    gi = jnp.min(jnp.where(gmask & (lt == mg), row, big), axis=0, keepdims=True)
    lo = ROUTE_LANE0 + MOE_EPG * gi
    emask = (row >= lo) & (row < lo + MOE_EPG)
    l1 = jnp.max(jnp.where(emask, lt, -big), axis=0, keepdims=True)
    i1 = jnp.min(jnp.where(emask & (lt == l1), row, big), axis=0, keepdims=True)
    emask2 = emask & (row != i1)
    l2 = jnp.max(jnp.where(emask2, lt, -big), axis=0, keepdims=True)
    i2 = jnp.min(jnp.where(emask2 & (lt == l2), row, big), axis=0, keepdims=True)
    e21 = jnp.exp(l2 - l1)
    w1 = gw / (1.0 + e21)
    w2 = gw * e21 / (1.0 + e21)

    oh = jnp.where((row == i1) | (row == i2), 1.0, 0.0)
    before = (lax.broadcasted_iota(jnp.int32, (tm, tm), 0) < lax.broadcasted_iota(jnp.int32, (tm, tm), 1))
    cum = (jnp.dot(oh.astype(BF16), jnp.where(before, 1.0, 0.0).astype(BF16), preferred_element_type=F32)
           + carry_ref[:, 0:1])
    r1 = jnp.sum(jnp.where(row == i1, cum, 0.0), axis=0, keepdims=True)
    r2 = jnp.sum(jnp.where(row == i2, cum, 0.0), axis=0, keepdims=True)
    carry_ref[...] = carry_ref[...] + jnp.sum(oh, axis=1, keepdims=True)
    cnt_ref[...] = carry_ref[...]

    frow = lax.broadcasted_iota(jnp.int32, (LANES, tm), 0)
    rt = jnp.where(frow == 0, i1 - ROUTE_LANE0, 0.0)
    rt = jnp.where(frow == 1, i2 - ROUTE_LANE0, rt)
    rt = jnp.where(frow == 2, w1, rt)
    rt = jnp.where(frow == 3, w2, rt)
    rt = jnp.where(frow == 4, r1, rt)
    rt = jnp.where(frow == 5, r2, rt)
    rt_ref[...] = rt[0:SUBLANES, :]
    ri_ref[...] = rt.T


def _outproj(mixes, w_bf16, xa, g1, gain2, sh2, sc2, wr, br, t0):
    b, s, d = _stream_shape(xa)
    nt = s // TM - t0
    so = nt * TM
    mod_idx = lambda bi, t: (jnp.where(t + t0 == 0, b, bi), 0, 0)
    tile = lambda wdt: pl.BlockSpec((None, TM, wdt), lambda bi, t: (bi, t + t0, 0))
    otile = lambda wdt: pl.BlockSpec((None, TM, wdt), lambda bi, t: (bi, t, 0))
    const = lambda shape: pl.BlockSpec(shape, lambda bi, t: (0,) * len(shape))
    x_specs, x_args = _stream_specs(xa, t0)
    in_specs = [tile(m.shape[2]) for m in mixes] + [const(w_bf16.shape)] + x_specs + [
        pl.BlockSpec((None, 1, d), mod_idx), const((1, d)),
        pl.BlockSpec((None, 1, d), mod_idx), pl.BlockSpec((None, 1, d), mod_idx),
        const((2, d, LANES)), const((1, LANES)),
    ]
    return pl.pallas_call(
        functools.partial(_outproj_kernel, n_mix=len(mixes), n_stream=len(x_args), t0=t0),
        grid=(b, nt),
        in_specs=in_specs,
        out_specs=[otile(d), otile(d), otile(LANES),
                   pl.BlockSpec((None, None, SUBLANES, TM), lambda bi, t: (bi, t, 0, 0)),
                   const((ROUTE_ROWS, LANES))],
        out_shape=[jax.ShapeDtypeStruct((b, so, d), F32), jax.ShapeDtypeStruct((b, so, d), F32),
                   jax.ShapeDtypeStruct((b, so, LANES), F32), jax.ShapeDtypeStruct((b, nt, SUBLANES, TM), F32),
                   jax.ShapeDtypeStruct((ROUTE_ROWS, LANES), F32)],
        scratch_shapes=[pltpu.VMEM((ROUTE_ROWS, LANES), F32)],
        compiler_params=_cparams(("arbitrary", "arbitrary")),
        name="outproj",
    )(*mixes, w_bf16, *x_args, g1, gain2, sh2, sc2, wr, br)


SUBLANES = 8


def _to_token_tiles(x):
    return x.reshape(x.shape[0], SUBLANES, x.shape[1] // SUBLANES)


def _from_token_tiles(x3):
    return x3.reshape(x3.shape[0], x3.shape[1] * x3.shape[2])


PAD_BITS = tuple(1 << k for k in reversed(range(MOE_ROWS.bit_length() - 1)))


def _dispatch_kernel(dest_ref, pads_ref, h_ref, xs_ref, hbuf, zbuf, sem, zsem):
    nt = pl.num_programs(1)
    step = pl.program_id(0) * nt + pl.program_id(1)
    last = pl.num_programs(0) * nt - 1
    slot = step % 2

    def zero_pads(wait):
        def expert(e, c):
            pos = pads_ref[0, e]
            n = pads_ref[1, e]
            for bit in PAD_BITS:
                take = (n & bit) != 0
                cp = pltpu.make_async_copy(zbuf.at[pl.ds(0, bit)], xs_ref.at[pl.ds(pos, bit)], zsem)

                @pl.when(take)
                def _():
                    cp.wait() if wait else cp.start()

                pos = pos + jnp.where(take, bit, 0)
            return c
        lax.fori_loop(0, MOE_EXPERTS, expert, 0)

        def tail(j, c):
            cp = pltpu.make_async_copy(
                zbuf, xs_ref.at[pl.ds(pads_ref[0, MOE_EXPERTS] + j * PAD_BITS[0], PAD_BITS[0])], zsem)
            cp.wait() if wait else cp.start()
            return c
        lax.fori_loop(0, pads_ref[1, MOE_EXPERTS], tail, 0)

    @pl.when(step == 0)
    def _():
        zbuf[...] = jnp.zeros_like(zbuf)
        zero_pads(wait=False)
    hbuf[slot] = _to_token_tiles(h_ref[...])

    def copy(sl, i, dst_row):
        return pltpu.make_async_copy(hbuf.at[sl, i], xs_ref.at[dst_row], sem.at[sl])

    for i in range(TM):
        copy(slot, i, dest_ref[step, i]).start()
        copy(slot, i, dest_ref[step, TM + i]).start()

    def drain(sl):
        for _ in range(2):
            pltpu.make_async_copy(hbuf.at[sl], xs_ref.at[pl.ds(0, TM)], sem.at[sl]).wait()

    @pl.when(step > 0)
    def _():
        drain(1 - slot)

    @pl.when(step == last)
    def _():
        drain(slot)
        zero_pads(wait=True)


def _dispatch(dest, pads, h2, n_pad):
    b, s, d = h2.shape
    nt = s // TM
    tile = (SUBLANES, d // SUBLANES)
    return pl.pallas_call(
        _dispatch_kernel,
        grid_spec=pltpu.PrefetchScalarGridSpec(
            num_scalar_prefetch=2,
            grid=(b, nt),
            in_specs=[pl.BlockSpec((None, TM, d), lambda bi, t, dr, pd: (bi, t, 0))],
            out_specs=pl.BlockSpec(memory_space=pl.ANY),
            scratch_shapes=[pltpu.VMEM((2, TM) + tile, F32), pltpu.VMEM((PAD_BITS[0],) + tile, F32),
                            pltpu.SemaphoreType.DMA((2,)), pltpu.SemaphoreType.DMA],
        ),
        out_shape=jax.ShapeDtypeStruct((n_pad,) + tile, F32),
        compiler_params=_cparams(("arbitrary", "arbitrary")),
        name="dispatch",
    )(dest, pads, h2)


def _mlp_kernel(be_ref, nu_ref, x_ref, wgu_ref, wd_ref, y_ref, wgu_b, wd_b):
    i = pl.program_id(0)
    prev = be_ref[jnp.maximum(i - 1, 0)]
    used = i < nu_ref[0]

    @pl.when(used & ((i == 0) | (be_ref[i] != prev)))
    def _():
        wgu_b[...] = wgu_ref[...].astype(BF16)
        wd_b[...] = wd_ref[...].astype(BF16)

    @pl.when(used)
    def _():
        x = _from_token_tiles(x_ref[...])
        gu = jnp.dot(x.astype(BF16), wgu_b[...], preferred_element_type=F32)
        act = _silu(gu[:, :MOE_FF]) * gu[:, MOE_FF:]
        y_ref[...] = _to_token_tiles(jnp.dot(act.astype(BF16), wd_b[...], preferred_element_type=F32))

    @pl.when(jnp.logical_not(used))
    def _():
        y_ref[...] = jnp.zeros_like(y_ref)


def _expert_mlp(block_e, n_used, xs, w_gu, w_down, layer):
    n_pad, sub, dl = xs.shape
    d = sub * dl
    ff2 = w_gu.shape[-1]
    slots = pl.BlockSpec((MOE_ROWS, sub, dl), lambda i, be, nu: (i, 0, 0))
    used_slots = pl.BlockSpec((MOE_ROWS, sub, dl), lambda i, be, nu: (jnp.minimum(i, nu[0] - 1), 0, 0))
    return pl.pallas_call(
        _mlp_kernel,
        grid_spec=pltpu.PrefetchScalarGridSpec(
            num_scalar_prefetch=2,
            grid=(n_pad // MOE_ROWS,),
            in_specs=[
                used_slots,
                pl.BlockSpec((None, None, d, ff2), lambda i, be, nu: (layer, be[i], 0, 0)),
                pl.BlockSpec((None, None, ff2 // 2, d), lambda i, be, nu: (layer, be[i], 0, 0)),
            ],
            out_specs=slots,
            scratch_shapes=[pltpu.VMEM((d, ff2), BF16), pltpu.VMEM((ff2 // 2, d), BF16)],
        ),
        out_shape=jax.ShapeDtypeStruct(xs.shape, F32),
        compiler_params=_cparams(("arbitrary",)),
        name="expert_mlp",
    )(block_e, n_used, xs, w_gu, w_down)


COMBINE_CHUNK = 32


def _combine_kernel(dest_ref, ys_ref, x_ref, ri_ref, g2_ref, fg_ref, o_ref, buf, sem, *, final):
    bi = pl.program_id(0)
    t = pl.program_id(1)
    nt = pl.num_programs(1)
    step = bi * nt + t
    total = pl.num_programs(0) * nt

    def copy(src_row, slot, k, i):
        return pltpu.make_async_copy(ys_ref.at[src_row], buf.at[slot, k * TM + i], sem.at[slot])

    def issue(st, slot, rows):
        for i in rows:
            copy(dest_ref[st, i], slot, 0, i).start()
            copy(dest_ref[st, TM + i], slot, 1, i).start()

    def wait_slot(slot):
        pltpu.make_async_copy(ys_ref.at[pl.ds(0, 2 * TM)], buf.at[slot], sem.at[slot]).wait()

    slot = step % 2

    @pl.when(step == 0)
    def _():
        issue(0, 0, range(TM))

    wait_slot(slot)

    lane = lax.broadcasted_iota(jnp.int32, (TM, LANES), 1)
    ri = ri_ref[...]
    w1 = jnp.sum(jnp.where(lane == 2, ri, 0.0), axis=-1, keepdims=True)
    w2 = jnp.sum(jnp.where(lane == 3, ri, 0.0), axis=-1, keepdims=True)
    nxt = jnp.minimum(step + 1, total - 1)
    for r0 in range(0, TM, COMBINE_CHUNK):
        rows = slice(r0, r0 + COMBINE_CHUNK)
        y = (_from_token_tiles(buf[slot, r0:r0 + COMBINE_CHUNK]) * w1[rows]
             + _from_token_tiles(buf[slot, TM + r0:TM + r0 + COMBINE_CHUNK]) * w2[rows])
        xn = x_ref[rows, :] + g2_ref[...] * y
        if final:
            ms = jnp.mean(xn * xn, axis=-1, keepdims=True)
            xn = xn * lax.rsqrt(ms + NORM_EPS) * fg_ref[...]
        o_ref[rows, :] = xn
        issue(nxt, 1 - slot, range(r0, r0 + COMBINE_CHUNK))

    @pl.when(step == total - 1)
    def _():
        wait_slot(1 - slot)


def _combine(dest, ys, xa, rinfo, g2, final_g, has_ctx, final):
    b, s, d = xa.shape
    nt = s // TM
    mod_idx = lambda bi, t, dr: (jnp.where(t == 0, b, bi) if has_ctx else bi, 0, 0)
    tile = lambda wdt: pl.BlockSpec((None, TM, wdt), lambda bi, t, dr: (bi, t, 0))
    out_spec = tile(d)
    out_shape = jax.ShapeDtypeStruct((b, s, d), F32)
    return pl.pallas_call(
        functools.partial(_combine_kernel, final=final),
        grid_spec=pltpu.PrefetchScalarGridSpec(
            num_scalar_prefetch=1,
            grid=(b, nt),
            in_specs=[
                pl.BlockSpec(memory_space=pl.ANY),
                tile(d), tile(LANES),
                pl.BlockSpec((None, 1, d), mod_idx),
                pl.BlockSpec((1, d), lambda bi, t, dr: (0, 0)),
            ],
            out_specs=out_spec,
            scratch_shapes=[pltpu.VMEM((2, 2 * TM, SUBLANES, d // SUBLANES), F32),
                            pltpu.SemaphoreType.DMA((2,))],
        ),
        out_shape=out_shape,
        compiler_params=_cparams(("arbitrary", "arbitrary")),
        name="combine",
    )(dest, ys, xa, rinfo, g2, final_g)


def _moe(h2, rt, counts, w_gu, w_down, layer):
    b, s, d = h2.shape
    nt = s // TM
    r = rt.reshape(b * nt, SUBLANES, TM)
    fields = lambda i: jnp.concatenate([r[:, i], r[:, i + 1]], axis=1).astype(jnp.int32)
    e = fields(0)
    rank = fields(4)
    cnt = counts[ROUTE_LANE0:ROUTE_LANE0 + MOE_EXPERTS, 0].astype(jnp.int32)
    padded = (cnt + MOE_ROWS - 1) // MOE_ROWS * MOE_ROWS
    ends = jnp.cumsum(padded)
    starts = ends - padded
    eids = jnp.arange(MOE_EXPERTS, dtype=jnp.int32)
    dest = jnp.sum(jnp.where(e[..., None] == eids, starts, 0), axis=-1) + rank
    n_assign = b * nt * TM * 2
    n_blocks = (n_assign + MOE_EXPERTS * (MOE_ROWS - 1)) // MOE_ROWS + 1
    n_pad = n_blocks * MOE_ROWS
    blk_row = jnp.arange(n_blocks, dtype=jnp.int32) * MOE_ROWS
    block_e = jnp.minimum(jnp.sum((blk_row[:, None] >= ends[None, :]).astype(jnp.int32), axis=1),
                          MOE_EXPERTS - 1)
    n_used = (ends[-1] // MOE_ROWS).astype(jnp.int32).reshape(1)
    pads = jnp.stack([jnp.append(starts + cnt, ends[-1]),
                      jnp.append(padded - cnt, (n_pad - ends[-1]) // PAD_BITS[0])]).astype(jnp.int32)
    xs = _dispatch(dest, pads, h2, n_pad)
    ys = _expert_mlp(block_e, n_used, xs, w_gu, w_down, layer)
    return dest, ys


def _router_weights(wg, bg, we, be):
    d = wg.shape[0]
    pad = LANES - MOE_GROUPS - MOE_EXPERTS
    assert ROUTE_LANE0 == MOE_GROUPS
    wr = jnp.concatenate([wg.astype(F32), we.astype(F32), jnp.zeros((d, pad), F32)], axis=1)
    br = jnp.concatenate([bg.astype(F32), be.astype(F32), jnp.zeros((pad,), F32)]).reshape(1, LANES)
    return jnp.stack(_split_bf16(wr)), br


def kernel(x, c, ctx, c_ctx, ada_w, ada_b, norm_g, final_g, ab_w_in, ab_w_out, ret_decay, ret_gn, na_rpb,
           swa_w_in, swa_w_out, swa_sink, router_g_w, router_g_b, router_e_w, router_e_b,
           expert_w_gu, expert_w_down):
    b, seq, d = x.shape
    assert ctx.shape[1] == TM and seq % TM == 0 and d == D_MODEL
    xa = (ctx, x)
    rope = _rope_tables(seq)

    cvec = jnp.concatenate([c, c_ctx[None, :], jnp.zeros((7, d), F32)], axis=0)
    mod = _adaln(cvec, ada_w, ada_b)
    mod = mod.reshape(DEPTH, b + 8, 6, 1, d)[:, :b + 1].transpose(0, 2, 1, 3, 4)

    for layer in range(DEPTH):
        last = layer == DEPTH - 1
        sh1, sc1, g1, sh2, sc2, g2 = (mod[layer, i] for i in range(6))
        gain1 = norm_g[layer, 0].reshape(1, d)
        gain2 = norm_g[layer, 1].reshape(1, d)
        j = layer // 2
        if layer % 2 == 0:
            nb = RET_W // LANES
            qk_scale = HEAD_DIM ** -0.5
            scales = {blk: qk_scale for blk in range(nb, 2 * nb)}
            scales.update({blk: qk_scale * LOG2E for blk in range(4 * nb, 5 * nb)})
            p = _proj(xa, gain1, sh1, sc1, ab_w_in[j].astype(BF16), rope,
                      rope_blocks=range(0, 2 * nb), scales=scales)
            log_gamma = jnp.log1p(-jnp.exp2(-ret_decay[j].astype(F32)))
            mixes = [_retention(p, log_gamma, ret_gn[j]), _neighbourhood(p, _na_bias(na_rpb[j]))]
            w_out = ab_w_out[j]
        else:
            qb = SWA_HEADS * HEAD_DIM // LANES
            kb = SWA_KV_HEADS * HEAD_DIM // LANES
            p = _proj(xa, gain1, sh1, sc1, swa_w_in[j].astype(BF16), rope,
                      rope_blocks=range(0, qb + kb),
                      scales={blk: HEAD_DIM ** -0.5 * LOG2E for blk in range(qb)})
            mixes = [_swa(p, swa_sink[j].astype(F32))]
            w_out = swa_w_out[j]
        t0 = 1 if last else 0
        wr, br = _router_weights(router_g_w[layer], router_g_b[layer], router_e_w[layer], router_e_b[layer])
        xa, h2, rinfo, rt, counts = _outproj(mixes, w_out.astype(BF16), xa, g1, gain2, sh2, sc2, wr, br, t0)
        dest, ys = _moe(h2, rt, counts, expert_w_gu, expert_w_down, layer)
        xa = _combine(dest, ys, xa, rinfo, g2, final_g.reshape(1, d), not last, last)
    return xa
```

```python
import functools

import numpy as np
import jax
import jax.numpy as jnp
from jax import lax
from jax.experimental import pallas as pl
from jax.experimental.pallas import tpu as pltpu

F32 = jnp.float32
BF16 = jnp.bfloat16

D_MODEL = 1024
DEPTH = 2
GRID_W = 64
HEAD_DIM = 64
RET_HEADS = 8
NA_HEADS = 8
RET_W = 512
NA_W = 512
AB_IN = 4 * RET_W + 3 * NA_W
GN_EPS = 1e-5
NA_KH = 8
NA_KW = 16
SWA_HEADS = 16
SWA_KV_HEADS = 4
SWA_WINDOW = 128
SWA_BLOCK = 128
SWA_IN = (SWA_HEADS + 2 * SWA_KV_HEADS) * HEAD_DIM
ROPE_BASE = 10000.0
MOE_GROUPS = 4
MOE_EPG = 8
MOE_EXPERTS = 32
MOE_FF = 512
NORM_EPS = 1e-6
NEG_INF = -1e30

LANES = 128
TM = 256
RET_BLOCK = 256
MOE_ROWS = 1024
ROUTE_LANE0 = 4
ROUTE_ROWS = 48
VMEM_LIMIT = 56 * 1024 * 1024


def _cparams(sem):
    return pltpu.CompilerParams(dimension_semantics=sem, vmem_limit_bytes=VMEM_LIMIT)


def _split_bf16(a):
    hi = a.astype(BF16)
    lo = (a - hi.astype(F32)).astype(BF16)
    return hi, lo


def _dot3_split(a, bh, bl):
    ah, al = _split_bf16(a)
    d = lambda x, y: jnp.dot(x, y, preferred_element_type=F32)
    return d(ah, bh) + (d(ah, bl) + d(al, bh))


def _dot3(a, b):
    return _dot3_split(a, *_split_bf16(b))


def _dot_nt(a, b):
    return lax.dot_general(a, b, (((1,), (1,)), ((), ())), preferred_element_type=F32)


def _dot_tn(a, b):
    return lax.dot_general(a, b, (((0,), (0,)), ((), ())), preferred_element_type=F32)


def _silu(x):
    return x / (1.0 + jnp.exp(-x))


def _adaln_kernel(c_ref, w_ref, b_ref, o_ref):
    o_ref[...] = _dot3(_silu(c_ref[...]), w_ref[...]) + b_ref[...]


def _adaln(cvec, ada_w, ada_b):
    depth, d, n6 = ada_w.shape
    rows = cvec.shape[0]
    tn = 1024
    return pl.pallas_call(
        _adaln_kernel,
        grid=(depth, n6 // tn),
        in_specs=[
            pl.BlockSpec((rows, d), lambda l, j: (0, 0)),
            pl.BlockSpec((None, d, tn), lambda l, j: (l, 0, j)),
            pl.BlockSpec((None, 1, tn), lambda l, j: (l, 0, j)),
        ],
        out_specs=pl.BlockSpec((None, rows, tn), lambda l, j: (l, 0, j)),
        out_shape=jax.ShapeDtypeStruct((depth, rows, n6), F32),
        compiler_params=_cparams(("arbitrary", "arbitrary")),
        name="adaln",
    )(cvec, ada_w, ada_b.reshape(depth, 1, n6))


def _rms_mod(x, g, sh, sc):
    ms = jnp.mean(x * x, axis=-1, keepdims=True)
    return (x * lax.rsqrt(ms + NORM_EPS) * g) * (1.0 + sc) + sh


def _stream_tile(refs, is_lat):
    if len(refs) == 1:
        return refs[0][...]
    return jnp.where(is_lat, refs[1][...], refs[0][...])


def _stream_specs(stream, t0):
    def im(f):
        return (lambda bi, t, *_: f(bi, t + t0))
    if not isinstance(stream, tuple):
        return [pl.BlockSpec((None, TM, stream.shape[2]), im(lambda bi, t: (bi, t, 0)))], [stream]
    ctx, x = stream
    d = x.shape[2]
    return ([pl.BlockSpec((None, TM, d), im(lambda bi, t: (bi, 0, 0))),
             pl.BlockSpec((None, TM, d), im(lambda bi, t: (bi, jnp.maximum(t - 1, 0), 0)))], [ctx, x])


def _stream_shape(stream):
    if not isinstance(stream, tuple):
        return stream.shape
    ctx, x = stream
    return (x.shape[0], ctx.shape[1] + x.shape[1], x.shape[2])


def _proj_kernel(*refs, rope_blocks, scales, cn, n_stream):
    x_refs = refs[:n_stream]
    g_ref, sh_ref, sc_ref, w_ref, rope_ref, o_ref = refs[n_stream:]
    is_lat = pl.program_id(1) > 0
    hb = _rms_mod(_stream_tile(x_refs, is_lat), g_ref[...], sh_ref[...], sc_ref[...]).astype(BF16)
    nout = w_ref.shape[1]
    for c in range(nout // cn):
        o = jnp.dot(hb, w_ref[:, c * cn:(c + 1) * cn], preferred_element_type=F32)
        for s in range(cn // LANES):
            blk = c * (cn // LANES) + s
            ob = o[:, s * LANES:(s + 1) * LANES]
            if blk in rope_blocks:
                r = (ob * rope_ref[0] + pltpu.roll(ob, 16, 1) * rope_ref[1]
                     + pltpu.roll(ob, LANES - 16, 1) * rope_ref[2])
                ob = jnp.where(is_lat, r, ob)
            if blk in scales:
                ob = ob * scales[blk]
            o_ref[:, blk * LANES:(blk + 1) * LANES] = ob.astype(o_ref.dtype)


def _proj(xa, gain, sh, sc, w_bf16, rope, rope_blocks, scales):
    b, s, d = _stream_shape(xa)
    nout = w_bf16.shape[1]
    nt = s // TM
    mod_idx = lambda bi, t: (jnp.where(t == 0, b, bi), 0, 0)
    x_specs, x_args = _stream_specs(xa, 0)
    kern = functools.partial(_proj_kernel, rope_blocks=frozenset(rope_blocks), scales=dict(scales), cn=512,
                             n_stream=len(x_args))
    return pl.pallas_call(
        kern,
        grid=(b, nt),
        in_specs=x_specs + [
            pl.BlockSpec((1, d), lambda bi, t: (0, 0)),
            pl.BlockSpec((None, 1, d), mod_idx),
            pl.BlockSpec((None, 1, d), mod_idx),
            pl.BlockSpec((d, nout), lambda bi, t: (0, 0)),
            pl.BlockSpec((3, TM, LANES), lambda bi, t: (0, jnp.maximum(t - 1, 0), 0)),
        ],
        out_specs=pl.BlockSpec((None, TM, nout), lambda bi, t: (bi, t, 0)),
        out_shape=jax.ShapeDtypeStruct((b, s, nout), BF16),
        compiler_params=_cparams(("arbitrary", "arbitrary")),
        name="proj",
    )(*x_args, gain, sh, sc, w_bf16, rope)


def _rope_tables(seq):
    nf = HEAD_DIM // 4
    inv = ROPE_BASE ** (-jnp.arange(nf, dtype=F32) / nf)
    t = jnp.arange(seq)
    row = (t // GRID_W).astype(F32)
    col = (t % GRID_W).astype(F32)
    lane = np.arange(LANES)
    jj = lane % HEAD_DIM
    axis_is_col = (jj // 32) == 1
    second_half = (jj % 32) >= 16
    f = jj % 16
    pos = jnp.where(axis_is_col[None, :], col[:, None], row[:, None])
    ang = pos * inv[f][None, :]
    c, s = jnp.cos(ang), jnp.sin(ang)
    sa = jnp.where(second_half[None, :], s, 0.0)
    sb = jnp.where(second_half[None, :], 0.0, -s)
    return jnp.stack([c, sa, sb], axis=0)


def _ret_kernel(lg_ref, q_ref, k_ref, v_ref, g_ref, gn_ref, o_ref, accf_ref, accb_ref,
                intra_ref, qd_ref, kd_ref):
    hp = pl.program_id(0)
    c = RET_BLOCK
    s_len = q_ref.shape[0]
    n_chunks = s_len // c
    ctx_chunks = TM // c
    pos = lax.broadcasted_iota(jnp.int32, (c, LANES), 0).astype(F32)
    m0 = lax.broadcasted_iota(jnp.int32, (c, LANES), 1) < HEAD_DIM
    same_head = ((lax.broadcasted_iota(jnp.int32, (LANES, LANES), 0) < HEAD_DIM)
                 == (lax.broadcasted_iota(jnp.int32, (LANES, LANES), 1) < HEAD_DIM))
    rel = (lax.broadcasted_iota(jnp.int32, (c, c), 0) - lax.broadcasted_iota(jnp.int32, (c, c), 1)).astype(F32)
    lgf = [lg_ref[0, hp * 2 + hh] for hh in range(2)]
    lgb = [lg_ref[1, hp * 2 + hh] for hh in range(2)]
    lgf_l = jnp.where(m0, lgf[0], lgf[1])
    lgb_l = jnp.where(m0, lgb[0], lgb[1])
    @pl.when(pl.program_id(1) == 0)
    def _():
        for hh in range(2):
            intra_ref[0, :, hh * c:(hh + 1) * c] = jnp.where(
                rel >= 0, jnp.exp(lgf[hh] * jnp.maximum(rel, 0.0)), 0.0)
            intra_ref[1, :, hh * c:(hh + 1) * c] = jnp.where(
                rel <= 0, jnp.exp(lgb[hh] * jnp.maximum(-rel, 0.0)), 0.0)
        qd_ref[0] = jnp.exp(lgf_l * (pos + 1.0))
        qd_ref[1] = jnp.exp(lgb_l * (c - pos))
        kd_ref[0] = jnp.exp(lgf_l * (c - 1.0 - pos))
        kd_ref[1] = jnp.exp(lgb_l * pos)
    cd = [jnp.exp(lgf_l[0:1] * float(c)), jnp.exp(lgb_l[0:1] * float(c))]

    def chunk(r0, state, d):
        q = q_ref[pl.ds(r0, c), :]
        k = k_ref[pl.ds(r0, c), :]
        v = v_ref[pl.ds(r0, c), :]
        qb = q.astype(BF16)
        kcat = jnp.concatenate([jnp.where(m0, k, 0.0), jnp.where(m0, 0.0, k)], axis=0).astype(BF16)
        vcat = jnp.concatenate([jnp.where(m0, v, 0.0), jnp.where(m0, 0.0, v)], axis=0).astype(BF16)
        s = _dot_nt(qb, kcat) * intra_ref[d]
        out = jnp.dot(s.astype(BF16), vcat, preferred_element_type=F32)
        out = out + jnp.dot(qb, state.astype(BF16), preferred_element_type=F32) * qd_ref[d]
        kv = _dot_tn((k * kd_ref[d]).astype(BF16), v.astype(BF16))
        return out, state * cd[d] + jnp.where(same_head, kv, 0.0)

    def body(i, states):
        sf, sb = states
        rf = pl.multiple_of(i * c, c)
        ib = jnp.where(i < ctx_chunks, ctx_chunks - 1 - i, n_chunks + ctx_chunks - 1 - i)
        rb = pl.multiple_of(ib * c, c)
        of, sf = chunk(rf, sf, 0)
        ob, sb = chunk(rb, sb, 1)
        accf_ref[pl.ds(rf, c), :] = of
        accb_ref[pl.ds(rb, c), :] = ob
        return sf, sb

    z = jnp.zeros((LANES, LANES), F32)
    lax.fori_loop(0, n_chunks, body, (z, z), unroll=True)

    avg = jnp.where(same_head, 1.0 / HEAD_DIM, 0.0).astype(BF16)

    def head_mean(x):
        hi, lo = _split_bf16(x)
        return (jnp.dot(hi, avg, preferred_element_type=F32) + jnp.dot(lo, avg, preferred_element_type=F32))

    def readout(i, carry):
        r0 = pl.multiple_of(i * c, c)
        o = accf_ref[pl.ds(r0, c), :] + accb_ref[pl.ds(r0, c), :]
        dlt = o - head_mean(o)
        var = head_mean(dlt * dlt)
        y = dlt * lax.rsqrt(var + GN_EPS) * gn_ref[...]
        o_ref[pl.ds(r0, c), :] = _silu(g_ref[pl.ds(r0, c), :].astype(F32)) * y
        return carry

    lax.fori_loop(0, n_chunks, readout, 0, unroll=True)


def _retention(p, log_gamma, ret_gn):
    b, s, _ = p.shape
    nb = RET_W // LANES
    blk = lambda off: pl.BlockSpec((None, s, LANES), lambda hp, bi: (bi, 0, off + hp))
    return pl.pallas_call(
        _ret_kernel,
        grid=(nb, b),
        in_specs=[
            pl.BlockSpec(memory_space=pltpu.SMEM),
            blk(0), blk(nb), blk(2 * nb), blk(3 * nb),
            pl.BlockSpec((1, LANES), lambda hp, bi: (0, hp)),
        ],
        out_specs=pl.BlockSpec((None, s, LANES), lambda hp, bi: (bi, 0, hp)),
        out_shape=jax.ShapeDtypeStruct((b, s, RET_W), F32),
        scratch_shapes=[pltpu.VMEM((s, LANES), F32), pltpu.VMEM((s, LANES), F32),
                        pltpu.VMEM((2, RET_BLOCK, 2 * RET_BLOCK), F32),
                        pltpu.VMEM((2, RET_BLOCK, LANES), F32), pltpu.VMEM((2, RET_BLOCK, LANES), F32)],
        compiler_params=_cparams(("arbitrary", "arbitrary")),
        name="retention",
    )(log_gamma, p, p, p, p, ret_gn.reshape(1, RET_W))


LOG2E = 1.4426950408889634


def _softmax_pv(s_list, v_list, extra=None):
    m = None
    for s in s_list:
        for j in range(s.shape[1] // LANES):
            blk = s[:, j * LANES:(j + 1) * LANES]
            m = blk if m is None else jnp.maximum(m, blk)
    m = m.max(axis=-1, keepdims=True)
    if extra is not None:
        m = jnp.maximum(m, extra)
    acc = None
    for s, v in zip(s_list, v_list):
        pv = jnp.dot(jnp.exp2(s - m).astype(BF16), v, preferred_element_type=F32)
        acc = pv if acc is None else acc + pv
    o, den = acc[:, :LANES], acc[:, LANES:]
    if extra is not None:
        den = den + jnp.exp2(extra - m)
    return o / den


def _stage_heads(q_ref, k_ref, v_ref, qm_ref, kb_ref, vb_ref):
    m0 = lax.broadcasted_iota(jnp.int32, (TM, LANES), 1) < HEAD_DIM

    def stage(i, carry):
        r0 = pl.multiple_of(i * TM, TM)
        q = q_ref[pl.ds(r0, TM), :]
        qm_ref[0, pl.ds(r0, TM), :] = jnp.where(m0, q, 0.0).astype(BF16)
        qm_ref[1, pl.ds(r0, TM), :] = jnp.where(m0, 0.0, q).astype(BF16)
        kb_ref[pl.ds(r0, TM), :] = k_ref[pl.ds(r0, TM), :].astype(BF16)
        vb_ref[pl.ds(r0, TM), 0:LANES] = v_ref[pl.ds(r0, TM), :].astype(BF16)
        vb_ref[pl.ds(r0, TM), LANES:2 * LANES] = jnp.ones((TM, LANES), BF16)
        return carry

    lax.fori_loop(0, q_ref.shape[0] // TM, stage, 0, unroll=True)


def _na_kernel(q_ref, k_ref, v_ref, bias_ref, o_ref, qm_ref, kb_ref, vb_ref):
    s_len = q_ref.shape[0]
    rows = (s_len - TM) // GRID_W
    nloc = NA_KH * GRID_W
    _stage_heads(q_ref, k_ref, v_ref, qm_ref, kb_ref, vb_ref)

    kc = kb_ref[0:TM, :]
    vc = vb_ref[0:TM, :]
    outs = [_softmax_pv([_dot_nt(qm_ref[hh, 0:TM, :], kc)], [vc]) for hh in range(2)]
    m0c = lax.broadcasted_iota(jnp.int32, (TM, LANES), 1) < HEAD_DIM
    o_ref[0:TM, :] = jnp.where(m0c, outs[0], outs[1])

    m0 = lax.broadcasted_iota(jnp.int32, (GRID_W, LANES), 1) < HEAD_DIM

    def row_block(r, carry):
        rs = jnp.clip(r - NA_KH // 2, 0, rows - NA_KH)
        pat = r - rs
        q0 = pl.multiple_of(TM + r * GRID_W, GRID_W)
        k0 = pl.multiple_of(TM + rs * GRID_W, GRID_W)
        kl = kb_ref[pl.ds(k0, nloc), :]
        vl = vb_ref[pl.ds(k0, nloc), :]
        kc = kb_ref[0:TM, :]
        vc = vb_ref[0:TM, :]
        q = jnp.concatenate([qm_ref[0, pl.ds(q0, GRID_W), :], qm_ref[1, pl.ds(q0, GRID_W), :]], axis=0)
        s_loc = _dot_nt(q, kl) + bias_ref[pat].reshape(2 * GRID_W, nloc)
        s_ctx = _dot_nt(q, kc)
        res = _softmax_pv([s_loc, s_ctx], [vl, vc])
        o_ref[pl.ds(q0, GRID_W), :] = jnp.where(m0, res[:GRID_W], res[GRID_W:])
        return carry

    lax.fori_loop(0, rows, row_block, 0, unroll=True)


def _na_bias(rpb):
    h = rpb.shape[0]
    qc = np.arange(GRID_W)[:, None]
    kc = np.arange(GRID_W)[None, :]
    win = np.clip(qc - NA_KW // 2, 0, GRID_W - NA_KW)
    valid = (kc >= win) & (kc < win + NA_KW)
    col_off = np.clip(kc - qc + NA_KW - 1, 0, 2 * NA_KW - 2)
    onehot = (col_off[None] == np.arange(2 * NA_KW - 1)[:, None, None]).astype(np.float32)
    cols = jnp.einsum("hrc,cqk->hrqk", rpb.astype(F32), onehot, precision=lax.Precision.HIGHEST)
    cols = jnp.where(valid[None, None], cols * LOG2E, NEG_INF)
    bias = jnp.stack([cols[:, NA_KH - 1 - p:2 * NA_KH - 1 - p] for p in range(NA_KH)])
    bias = bias.transpose(0, 1, 3, 2, 4)
    return bias.reshape(NA_KH, h, GRID_W, NA_KH * GRID_W)


def _neighbourhood(p, bias):
    b, s, _ = p.shape
    nb = NA_W // LANES
    c0 = 4 * RET_W // LANES
    blk = lambda off: pl.BlockSpec((None, s, LANES), lambda hp, bi: (bi, 0, c0 + off + hp))
    return pl.pallas_call(
        _na_kernel,
        grid=(nb, b),
        in_specs=[
            blk(0), blk(nb), blk(2 * nb),
            pl.BlockSpec((NA_KH, 2, GRID_W, NA_KH * GRID_W), lambda hp, bi: (0, hp, 0, 0)),
        ],
        out_specs=pl.BlockSpec((None, s, LANES), lambda hp, bi: (bi, 0, hp)),
        out_shape=jax.ShapeDtypeStruct((b, s, NA_W), F32),
        scratch_shapes=[pltpu.VMEM((2, s, LANES), BF16), pltpu.VMEM((s, LANES), BF16),
                        pltpu.VMEM((s, 2 * LANES), BF16)],
        compiler_params=_cparams(("arbitrary", "arbitrary")),
        name="neighbourhood",
    )(p, p, p, bias)


def _swa_kernel(sink_ref, q_ref, k_ref, v_ref, o_ref, kd_ref, vd_ref):
    kp = pl.program_id(1)
    s_len = q_ref.shape[0]
    seq = s_len - TM
    nblk = seq // SWA_BLOCK
    band = SWA_BLOCK + 2 * SWA_WINDOW
    group = SWA_HEADS // SWA_KV_HEADS
    heads_per_step = 2 * group
    m0t = lax.broadcasted_iota(jnp.int32, (TM, LANES), 1) < HEAD_DIM

    def stage(i, carry):
        r0 = pl.multiple_of(i * TM, TM)
        for src, dst in ((k_ref, kd_ref), (v_ref, vd_ref)):
            x = src[pl.ds(r0, TM), :].astype(F32)
            xr = pltpu.roll(x, HEAD_DIM, 1)
            dst[0, pl.ds(r0, TM), 0:LANES] = jnp.where(m0t, x, xr).astype(BF16)
            dst[1, pl.ds(r0, TM), 0:LANES] = jnp.where(m0t, xr, x).astype(BF16)
        for h in range(2):
            vd_ref[h, pl.ds(r0, TM), LANES:2 * LANES] = jnp.ones((TM, LANES), BF16)
        return carry

    lax.fori_loop(0, s_len // TM, stage, 0, unroll=True)
    o_ref[0:TM, :] = jnp.zeros((TM, o_ref.shape[1]), F32)

    rows = group * SWA_BLOCK
    qi = lax.broadcasted_iota(jnp.int32, (rows, band), 0) % SWA_BLOCK
    ki = lax.broadcasted_iota(jnp.int32, (rows, band), 1)
    m0 = lax.broadcasted_iota(jnp.int32, (SWA_BLOCK, LANES), 1) < HEAD_DIM
    head_of_row = lax.broadcasted_iota(jnp.int32, (rows, 1), 0) // SWA_BLOCK

    def block(i, carry):
        start = jnp.clip((i - 1) * SWA_BLOCK, 0, seq - band)
        k0 = pl.multiple_of(TM + start, SWA_BLOCK)
        q0 = pl.multiple_of(TM + i * SWA_BLOCK, SWA_BLOCK)
        valid = jnp.abs(qi + (i * SWA_BLOCK - start) - ki) <= SWA_WINDOW
        for hh in range(2):
            kb = kd_ref[hh, pl.ds(k0, band), :]
            vb = vd_ref[hh, pl.ds(k0, band), :]
            kc = kd_ref[hh, 0:TM, :]
            vc = vd_ref[hh, 0:TM, :]
            parts = []
            for j in range(group // 2):
                pair = hh * (group // 2) + j
                qp = q_ref[pl.ds(q0, SWA_BLOCK), pair * LANES:(pair + 1) * LANES]
                parts += [jnp.where(m0, qp, 0.0), jnp.where(m0, 0.0, qp)]
            q = jnp.concatenate(parts, axis=0).astype(BF16)
            s = jnp.where(valid, _dot_nt(q, kb), NEG_INF)
            s_ctx = _dot_nt(q, kc)
            h0 = kp * heads_per_step + hh * group
            sink = jnp.full((rows, 1), sink_ref[h0], F32)
            for g in range(1, group):
                sink = jnp.where(head_of_row == g, sink_ref[h0 + g], sink)
            res = _softmax_pv([s, s_ctx], [vb, vc], extra=sink * LOG2E)
            for j in range(group // 2):
                pair = hh * (group // 2) + j
                r0 = 2 * j * SWA_BLOCK
                o_ref[pl.ds(q0, SWA_BLOCK), pair * LANES:(pair + 1) * LANES] = jnp.where(
                    m0, res[r0:r0 + SWA_BLOCK], res[r0 + SWA_BLOCK:r0 + 2 * SWA_BLOCK])
        return carry

    lax.fori_loop(0, nblk, block, 0, unroll=True)


def _swa(p, sink):
    b, s, _ = p.shape
    qw = SWA_HEADS * HEAD_DIM // 2
    kblk = SWA_HEADS * HEAD_DIM // LANES
    vblk = kblk + SWA_KV_HEADS * HEAD_DIM // LANES
    return pl.pallas_call(
        _swa_kernel,
        grid=(b, 2),
        in_specs=[
            pl.BlockSpec(memory_space=pltpu.SMEM),
            pl.BlockSpec((None, s, qw), lambda bi, kp: (bi, 0, kp)),
            pl.BlockSpec((None, s, LANES), lambda bi, kp: (bi, 0, kblk + kp)),
            pl.BlockSpec((None, s, LANES), lambda bi, kp: (bi, 0, vblk + kp)),
        ],
        out_specs=pl.BlockSpec((None, s, qw), lambda bi, kp: (bi, 0, kp)),
        out_shape=jax.ShapeDtypeStruct((b, s, SWA_HEADS * HEAD_DIM), F32),
        scratch_shapes=[pltpu.VMEM((2, s, LANES), BF16), pltpu.VMEM((2, s, 2 * LANES), BF16)],
        compiler_params=_cparams(("arbitrary", "arbitrary")),
        name="swa",
    )(sink, p, p, p)


def _outproj_kernel(*refs, n_mix, n_stream, t0):
    mix_refs = refs[:n_mix]
    x_refs = refs[n_mix + 1:n_mix + 1 + n_stream]
    w_ref = refs[n_mix]
    (g1_ref, gn_ref, sh_ref, sc_ref, wr_ref, br_ref,
     xo_ref, h_ref, ri_ref, rt_ref, cnt_ref, carry_ref) = refs[n_mix + 1 + n_stream:]
    first = (pl.program_id(0) == 0) & (pl.program_id(1) == 0)

    @pl.when(first)
    def _():
        carry_ref[...] = jnp.zeros_like(carry_ref)

    parts = [m_ref[...].astype(BF16) for m_ref in mix_refs]
    mix = parts[0] if len(parts) == 1 else jnp.concatenate(parts, axis=1)
    o = jnp.dot(mix, w_ref[...], preferred_element_type=F32)
    xn = _stream_tile(x_refs, pl.program_id(1) + t0 > 0) + g1_ref[...] * o
    xo_ref[...] = xn
    h = _rms_mod(xn, gn_ref[...], sh_ref[...], sc_ref[...])
    h_ref[...] = h

    logits = _dot3_split(h, wr_ref[0], wr_ref[1]) + br_ref[...]
    tm = logits.shape[0]
    lt = logits.T[0:ROUTE_ROWS, :]
    row = lax.broadcasted_iota(jnp.int32, (ROUTE_ROWS, tm), 0).astype(F32)
    big = 1e9
    gmask = row < MOE_GROUPS
    mg = jnp.max(jnp.where(gmask, lt, -big), axis=0, keepdims=True)
    sg = jnp.sum(jnp.where(gmask, jnp.exp(jnp.minimum(lt - mg, 0.0)), 0.0), axis=0, keepdims=True)
    gw = 1.0 / sg
    gi = jnp.min(jnp.where(gmask & (lt == mg), row, big), axis=0, keepdims=True)
    lo = ROUTE_LANE0 + MOE_EPG * gi
    emask = (row >= lo) & (row < lo + MOE_EPG)
    l1 = jnp.max(jnp.where(emask, lt, -big), axis=0, keepdims=True)
    i1 = jnp.min(jnp.where(emask & (lt == l1), row, big), axis=0, keepdims=True)
    emask2 = emask & (row != i1)
    l2 = jnp.max(jnp.where(emask2, lt, -big), axis=0, keepdims=True)
    i2 = jnp.min(jnp.where(emask2 & (lt == l2), row, big), axis=0, keepdims=True)
    e21 = jnp.exp(l2 - l1)
    w1 = gw / (1.0 + e21)
    w2 = gw * e21 / (1.0 + e21)

    oh = jnp.where((row == i1) | (row == i2), 1.0, 0.0)
    before = (lax.broadcasted_iota(jnp.int32, (tm, tm), 0) < lax.broadcasted_iota(jnp.int32, (tm, tm), 1))
    cum = (jnp.dot(oh.astype(BF16), jnp.where(before, 1.0, 0.0).astype(BF16), preferred_element_type=F32)
           + carry_ref[:, 0:1])
    r1 = jnp.sum(jnp.where(row == i1, cum, 0.0), axis=0, keepdims=True)
    r2 = jnp.sum(jnp.where(row == i2, cum, 0.0), axis=0, keepdims=True)
    carry_ref[...] = carry_ref[...] + jnp.sum(oh, axis=1, keepdims=True)
    cnt_ref[...] = carry_ref[...]

    frow = lax.broadcasted_iota(jnp.int32, (LANES, tm), 0)
    rt = jnp.where(frow == 0, i1 - ROUTE_LANE0, 0.0)
    rt = jnp.where(frow == 1, i2 - ROUTE_LANE0, rt)
    rt = jnp.where(frow == 2, w1, rt)
    rt = jnp.where(frow == 3, w2, rt)
    rt = jnp.where(frow == 4, r1, rt)
    rt = jnp.where(frow == 5, r2, rt)
    rt_ref[...] = rt[0:SUBLANES, :]
    ri_ref[...] = rt.T


def _outproj(mixes, w_bf16, xa, g1, gain2, sh2, sc2, wr, br, t0):
    b, s, d = _stream_shape(xa)
    nt = s // TM - t0
    so = nt * TM
    mod_idx = lambda bi, t: (jnp.where(t + t0 == 0, b, bi), 0, 0)
    tile = lambda wdt: pl.BlockSpec((None, TM, wdt), lambda bi, t: (bi, t + t0, 0))
    otile = lambda wdt: pl.BlockSpec((None, TM, wdt), lambda bi, t: (bi, t, 0))
    const = lambda shape: pl.BlockSpec(shape, lambda bi, t: (0,) * len(shape))
    x_specs, x_args = _stream_specs(xa, t0)
    in_specs = [tile(m.shape[2]) for m in mixes] + [const(w_bf16.shape)] + x_specs + [
        pl.BlockSpec((None, 1, d), mod_idx), const((1, d)),
        pl.BlockSpec((None, 1, d), mod_idx), pl.BlockSpec((None, 1, d), mod_idx),
        const((2, d, LANES)), const((1, LANES)),
    ]
    return pl.pallas_call(
        functools.partial(_outproj_kernel, n_mix=len(mixes), n_stream=len(x_args), t0=t0),
        grid=(b, nt),
        in_specs=in_specs,
        out_specs=[otile(d), otile(d), otile(LANES),
                   pl.BlockSpec((None, None, SUBLANES, TM), lambda bi, t: (bi, t, 0, 0)),
                   const((ROUTE_ROWS, LANES))],
        out_shape=[jax.ShapeDtypeStruct((b, so, d), F32), jax.ShapeDtypeStruct((b, so, d), F32),
                   jax.ShapeDtypeStruct((b, so, LANES), F32), jax.ShapeDtypeStruct((b, nt, SUBLANES, TM), F32),
                   jax.ShapeDtypeStruct((ROUTE_ROWS, LANES), F32)],
        scratch_shapes=[pltpu.VMEM((ROUTE_ROWS, LANES), F32)],
        compiler_params=_cparams(("arbitrary", "arbitrary")),
        name="outproj",
    )(*mixes, w_bf16, *x_args, g1, gain2, sh2, sc2, wr, br)


SUBLANES = 8


def _to_token_tiles(x):
    return x.reshape(x.shape[0], SUBLANES, x.shape[1] // SUBLANES)


def _from_token_tiles(x3):
    return x3.reshape(x3.shape[0], x3.shape[1] * x3.shape[2])


PAD_BITS = tuple(1 << k for k in reversed(range(MOE_ROWS.bit_length() - 1)))


def _dispatch_kernel(dest_ref, pads_ref, h_ref, xs_ref, hbuf, zbuf, sem, zsem):
    nt = pl.num_programs(1)
    step = pl.program_id(0) * nt + pl.program_id(1)
    last = pl.num_programs(0) * nt - 1
    slot = step % 2

    def zero_pads(wait):
        def expert(e, c):
            pos = pads_ref[0, e]
            n = pads_ref[1, e]
            for bit in PAD_BITS:
                take = (n & bit) != 0
                cp = pltpu.make_async_copy(zbuf.at[pl.ds(0, bit)], xs_ref.at[pl.ds(pos, bit)], zsem)

                @pl.when(take)
                def _():
                    cp.wait() if wait else cp.start()

                pos = pos + jnp.where(take, bit, 0)
            return c
        lax.fori_loop(0, MOE_EXPERTS, expert, 0)

        def tail(j, c):
            cp = pltpu.make_async_copy(
                zbuf, xs_ref.at[pl.ds(pads_ref[0, MOE_EXPERTS] + j * PAD_BITS[0], PAD_BITS[0])], zsem)
            cp.wait() if wait else cp.start()
            return c
        lax.fori_loop(0, pads_ref[1, MOE_EXPERTS], tail, 0)

    @pl.when(step == 0)
    def _():
        zbuf[...] = jnp.zeros_like(zbuf)
        zero_pads(wait=False)
    hbuf[slot] = _to_token_tiles(h_ref[...])

    def copy(sl, i, dst_row):
        return pltpu.make_async_copy(hbuf.at[sl, i], xs_ref.at[dst_row], sem.at[sl])

    for i in range(TM):
        copy(slot, i, dest_ref[step, i]).start()
        copy(slot, i, dest_ref[step, TM + i]).start()

    def drain(sl):
        for _ in range(2):
            pltpu.make_async_copy(hbuf.at[sl], xs_ref.at[pl.ds(0, TM)], sem.at[sl]).wait()

    @pl.when(step > 0)
    def _():
        drain(1 - slot)

    @pl.when(step == last)
    def _():
        drain(slot)
        zero_pads(wait=True)


def _dispatch(dest, pads, h2, n_pad):
    b, s, d = h2.shape
    nt = s // TM
    tile = (SUBLANES, d // SUBLANES)
    return pl.pallas_call(
        _dispatch_kernel,
        grid_spec=pltpu.PrefetchScalarGridSpec(
            num_scalar_prefetch=2,
            grid=(b, nt),
            in_specs=[pl.BlockSpec((None, TM, d), lambda bi, t, dr, pd: (bi, t, 0))],
            out_specs=pl.BlockSpec(memory_space=pl.ANY),
            scratch_shapes=[pltpu.VMEM((2, TM) + tile, F32), pltpu.VMEM((PAD_BITS[0],) + tile, F32),
                            pltpu.SemaphoreType.DMA((2,)), pltpu.SemaphoreType.DMA],
        ),
        out_shape=jax.ShapeDtypeStruct((n_pad,) + tile, F32),
        compiler_params=_cparams(("arbitrary", "arbitrary")),
        name="dispatch",
    )(dest, pads, h2)


def _mlp_kernel(be_ref, nu_ref, x_ref, wgu_ref, wd_ref, y_ref, wgu_b, wd_b):
    i = pl.program_id(0)
    prev = be_ref[jnp.maximum(i - 1, 0)]
    used = i < nu_ref[0]

    @pl.when(used & ((i == 0) | (be_ref[i] != prev)))
    def _():
        wgu_b[...] = wgu_ref[...].astype(BF16)
        wd_b[...] = wd_ref[...].astype(BF16)

    @pl.when(used)
    def _():
        x = _from_token_tiles(x_ref[...])
        gu = jnp.dot(x.astype(BF16), wgu_b[...], preferred_element_type=F32)
        act = _silu(gu[:, :MOE_FF]) * gu[:, MOE_FF:]
        y_ref[...] = _to_token_tiles(jnp.dot(act.astype(BF16), wd_b[...], preferred_element_type=F32))

    @pl.when(jnp.logical_not(used))
    def _():
        y_ref[...] = jnp.zeros_like(y_ref)


def _expert_mlp(block_e, n_used, xs, w_gu, w_down, layer):
    n_pad, sub, dl = xs.shape
    d = sub * dl
    ff2 = w_gu.shape[-1]
    slots = pl.BlockSpec((MOE_ROWS, sub, dl), lambda i, be, nu: (i, 0, 0))
    used_slots = pl.BlockSpec((MOE_ROWS, sub, dl), lambda i, be, nu: (jnp.minimum(i, nu[0] - 1), 0, 0))
    return pl.pallas_call(
        _mlp_kernel,
        grid_spec=pltpu.PrefetchScalarGridSpec(
            num_scalar_prefetch=2,
            grid=(n_pad // MOE_ROWS,),
            in_specs=[
                used_slots,
                pl.BlockSpec((None, None, d, ff2), lambda i, be, nu: (layer, be[i], 0, 0)),
                pl.BlockSpec((None, None, ff2 // 2, d), lambda i, be, nu: (layer, be[i], 0, 0)),
            ],
            out_specs=slots,
            scratch_shapes=[pltpu.VMEM((d, ff2), BF16), pltpu.VMEM((ff2 // 2, d), BF16)],
        ),
        out_shape=jax.ShapeDtypeStruct(xs.shape, F32),
        compiler_params=_cparams(("arbitrary",)),
        name="expert_mlp",
    )(block_e, n_used, xs, w_gu, w_down)


def _combine_kernel(dest_ref, ys_ref, x_ref, ri_ref, g2_ref, fg_ref, o_ref, buf, sem, *, final):
    bi = pl.program_id(0)
    t = pl.program_id(1)
    nt = pl.num_programs(1)
    step = bi * nt + t
    total = pl.num_programs(0) * nt

    def copy(src_row, slot, k, i):
        return pltpu.make_async_copy(ys_ref.at[src_row], buf.at[slot, k * TM + i], sem.at[slot])

    def issue(st, slot):
        for i in range(TM):
            copy(dest_ref[st, i], slot, 0, i).start()
            copy(dest_ref[st, TM + i], slot, 1, i).start()

    slot = step % 2

    @pl.when(step == 0)
    def _():
        issue(0, 0)

    @pl.when(step + 1 < total)
    def _():
        issue(step + 1, 1 - slot)

    pltpu.make_async_copy(ys_ref.at[pl.ds(0, 2 * TM)], buf.at[slot], sem.at[slot]).wait()

    lane = lax.broadcasted_iota(jnp.int32, (TM, LANES), 1)
    ri = ri_ref[...]
    w1 = jnp.sum(jnp.where(lane == 2, ri, 0.0), axis=-1, keepdims=True)
    w2 = jnp.sum(jnp.where(lane == 3, ri, 0.0), axis=-1, keepdims=True)
    y = (_from_token_tiles(buf[slot, 0:TM]) * w1 + _from_token_tiles(buf[slot, TM:2 * TM]) * w2)
    xn = x_ref[...] + g2_ref[...] * y
    if final:
        ms = jnp.mean(xn * xn, axis=-1, keepdims=True)
        xn = xn * lax.rsqrt(ms + NORM_EPS) * fg_ref[...]
    o_ref[...] = xn


def _combine(dest, ys, xa, rinfo, g2, final_g, has_ctx, final):
    b, s, d = xa.shape
    nt = s // TM
    mod_idx = lambda bi, t, dr: (jnp.where(t == 0, b, bi) if has_ctx else bi, 0, 0)
    tile = lambda wdt: pl.BlockSpec((None, TM, wdt), lambda bi, t, dr: (bi, t, 0))
    out_spec = tile(d)
    out_shape = jax.ShapeDtypeStruct((b, s, d), F32)
    return pl.pallas_call(
        functools.partial(_combine_kernel, final=final),
        grid_spec=pltpu.PrefetchScalarGridSpec(
            num_scalar_prefetch=1,
            grid=(b, nt),
            in_specs=[
                pl.BlockSpec(memory_space=pl.ANY),
                tile(d), tile(LANES),
                pl.BlockSpec((None, 1, d), mod_idx),
                pl.BlockSpec((1, d), lambda bi, t, dr: (0, 0)),
            ],
            out_specs=out_spec,
            scratch_shapes=[pltpu.VMEM((2, 2 * TM, SUBLANES, d // SUBLANES), F32),
                            pltpu.SemaphoreType.DMA((2,))],
        ),
        out_shape=out_shape,
        compiler_params=_cparams(("arbitrary", "arbitrary")),
        name="combine",
    )(dest, ys, xa, rinfo, g2, final_g)


def _moe(h2, rt, counts, w_gu, w_down, layer):
    b, s, d = h2.shape
    nt = s // TM
    r = rt.reshape(b * nt, SUBLANES, TM)
    fields = lambda i: jnp.concatenate([r[:, i], r[:, i + 1]], axis=1).astype(jnp.int32)
    e = fields(0)
    rank = fields(4)
    cnt = counts[ROUTE_LANE0:ROUTE_LANE0 + MOE_EXPERTS, 0].astype(jnp.int32)
    padded = (cnt + MOE_ROWS - 1) // MOE_ROWS * MOE_ROWS
    ends = jnp.cumsum(padded)
    starts = ends - padded
    eids = jnp.arange(MOE_EXPERTS, dtype=jnp.int32)
    dest = jnp.sum(jnp.where(e[..., None] == eids, starts, 0), axis=-1) + rank
    n_assign = b * nt * TM * 2
    n_blocks = (n_assign + MOE_EXPERTS * (MOE_ROWS - 1)) // MOE_ROWS + 1
    n_pad = n_blocks * MOE_ROWS
    blk_row = jnp.arange(n_blocks, dtype=jnp.int32) * MOE_ROWS
    block_e = jnp.minimum(jnp.sum((blk_row[:, None] >= ends[None, :]).astype(jnp.int32), axis=1),
                          MOE_EXPERTS - 1)
    n_used = (ends[-1] // MOE_ROWS).astype(jnp.int32).reshape(1)
    pads = jnp.stack([jnp.append(starts + cnt, ends[-1]),
                      jnp.append(padded - cnt, (n_pad - ends[-1]) // PAD_BITS[0])]).astype(jnp.int32)
    xs = _dispatch(dest, pads, h2, n_pad)
    ys = _expert_mlp(block_e, n_used, xs, w_gu, w_down, layer)
    return dest, ys


def _router_weights(wg, bg, we, be):
    d = wg.shape[0]
    pad = LANES - MOE_GROUPS - MOE_EXPERTS
    assert ROUTE_LANE0 == MOE_GROUPS
    wr = jnp.concatenate([wg.astype(F32), we.astype(F32), jnp.zeros((d, pad), F32)], axis=1)
    br = jnp.concatenate([bg.astype(F32), be.astype(F32), jnp.zeros((pad,), F32)]).reshape(1, LANES)
    return jnp.stack(_split_bf16(wr)), br


def kernel(x, c, ctx, c_ctx, ada_w, ada_b, norm_g, final_g, ab_w_in, ab_w_out, ret_decay, ret_gn, na_rpb,
           swa_w_in, swa_w_out, swa_sink, router_g_w, router_g_b, router_e_w, router_e_b,
           expert_w_gu, expert_w_down):
    b, seq, d = x.shape
    assert ctx.shape[1] == TM and seq % TM == 0 and d == D_MODEL
    xa = (ctx, x)
    rope = _rope_tables(seq)

    cvec = jnp.concatenate([c, c_ctx[None, :], jnp.zeros((7, d), F32)], axis=0)
    mod = _adaln(cvec, ada_w, ada_b)
    mod = mod.reshape(DEPTH, b + 8, 6, 1, d)[:, :b + 1].transpose(0, 2, 1, 3, 4)

    for layer in range(DEPTH):
        last = layer == DEPTH - 1
        sh1, sc1, g1, sh2, sc2, g2 = (mod[layer, i] for i in range(6))
        gain1 = norm_g[layer, 0].reshape(1, d)
        gain2 = norm_g[layer, 1].reshape(1, d)
        j = layer // 2
        if layer % 2 == 0:
            nb = RET_W // LANES
            qk_scale = HEAD_DIM ** -0.5
            scales = {blk: qk_scale for blk in range(nb, 2 * nb)}
            scales.update({blk: qk_scale * LOG2E for blk in range(4 * nb, 5 * nb)})
            p = _proj(xa, gain1, sh1, sc1, ab_w_in[j].astype(BF16), rope,
                      rope_blocks=range(0, 2 * nb), scales=scales)
            log_gamma = jnp.log1p(-jnp.exp2(-ret_decay[j].astype(F32)))
            mixes = [_retention(p, log_gamma, ret_gn[j]), _neighbourhood(p, _na_bias(na_rpb[j]))]
            w_out = ab_w_out[j]
        else:
            qb = SWA_HEADS * HEAD_DIM // LANES
            kb = SWA_KV_HEADS * HEAD_DIM // LANES
            p = _proj(xa, gain1, sh1, sc1, swa_w_in[j].astype(BF16), rope,
                      rope_blocks=range(0, qb + kb),
                      scales={blk: HEAD_DIM ** -0.5 * LOG2E for blk in range(qb)})
            mixes = [_swa(p, swa_sink[j].astype(F32))]
            w_out = swa_w_out[j]
        t0 = 1 if last else 0
        wr, br = _router_weights(router_g_w[layer], router_g_b[layer], router_e_w[layer], router_e_b[layer])
        xa, h2, rinfo, rt, counts = _outproj(mixes, w_out.astype(BF16), xa, g1, gain2, sh2, sc2, wr, br, t0)
        dest, ys = _moe(h2, rt, counts, expert_w_gu, expert_w_down, layer)
        xa = _combine(dest, ys, xa, rinfo, g2, final_g.reshape(1, d), not last, last)
    return xa
```
